```python
import jax
import jax.numpy as jnp
from jax import lax
import numpy as np

D_MODEL = 1024
BATCH = 32
SEQ = 2048
DEPTH = 4

CTX_LEN = 256
GRID_W = 64
ROPE_THETA = 10000.0
EPS = 1e-6
NEG_INF = -1e30
F32 = jnp.float32
ATTN_BLOCK = 128

MLA_HEADS = 4
MLA_Q_LORA = 256
MLA_KV_LORA = 256
MLA_NOPE = 128
MLA_ROPE = 64
MLA_V = 128
CMLP_GROUPS = 4
CMLP_GROUP_DIM = 128
CMLP_CHUNK = 128
CMLP_WIDTH = CMLP_GROUPS * CMLP_GROUP_DIM
RET_HEADS = 4
RET_QK = 64
RET_V = 128
RET_CHUNK = 128
RET_QK_W = RET_HEADS * RET_QK
RET_V_W = RET_HEADS * RET_V
SWA_Q_HEADS = 8
SWA_KV_HEADS = 2
SWA_HEAD_DIM = 64
SWA_WINDOW = 128
SWA_BLOCK = 128
SWA_Q_W = SWA_Q_HEADS * SWA_HEAD_DIM
SWA_KV_W = SWA_KV_HEADS * SWA_HEAD_DIM
FFN_HIDDEN = -(-8 * D_MODEL // (3 * 256)) * 256

AB_KV_WIDTH = MLA_KV_LORA + MLA_ROPE
AB_IN = AB_KV_WIDTH + MLA_Q_LORA + 2 * CMLP_WIDTH
AB_OUT = MLA_HEADS * MLA_V + CMLP_WIDTH
CD_KV_WIDTH = RET_QK_W + RET_V_W + 2 * SWA_KV_W
CD_IN = CD_KV_WIDTH + RET_QK_W + RET_V_W + SWA_Q_W
CD_OUT = RET_V_W + SWA_Q_W

N_EVEN = (DEPTH + 1) // 2
N_ODD = DEPTH // 2

kernel_name = 'hybrid_mla_gmlp_retention_swa_dit_trunk'


def _rms(x, g):
    xf = x.astype(F32)
    y = xf * lax.rsqrt(jnp.mean(xf * xf, axis=-1, keepdims=True) + EPS)
    return (y * g.astype(F32)).astype(x.dtype)


def _group_rms(y, g, groups):
    shp = y.shape
    yf = y.astype(F32).reshape(shp[:-1] + (groups, shp[-1] // groups))
    yf = yf * lax.rsqrt(jnp.mean(yf * yf, axis=-1, keepdims=True) + EPS)
    return (yf.reshape(shp) * g.astype(F32)).astype(y.dtype)


def _modulate(x, shift, scale):
    return x * (1.0 + scale) + shift


def _split_cols(z, widths):
    return jnp.split(z, np.cumsum(widths)[:-1].tolist(), axis=-1)


def _heads(a, h):
    return a.reshape(a.shape[:2] + (h, a.shape[-1] // h))


def _flip(a):
    return jnp.flip(a, axis=1)


def _axial_rope(n, rot_dim):
    t = jnp.arange(n)
    row = (t // GRID_W).astype(F32)
    col = (t % GRID_W).astype(F32)
    n_freq = rot_dim // 4
    freqs = ROPE_THETA ** (-jnp.arange(n_freq, dtype=F32) / n_freq)
    ang = jnp.concatenate([row[:, None] * freqs, col[:, None] * freqs], axis=-1)
    return jnp.cos(ang), jnp.sin(ang)


def _apply_rope(x, rope):
    cos, sin = rope
    cos = cos[None, :, None, :].astype(x.dtype)
    sin = sin[None, :, None, :].astype(x.dtype)
    x1, x2 = jnp.split(x, 2, axis=-1)
    return jnp.concatenate([x1 * cos - x2 * sin, x1 * sin + x2 * cos], axis=-1)


def _to_blocks(a, blk):
    B, n = a.shape[:2]
    return a.reshape((B, n // blk, blk) + a.shape[2:]).swapaxes(0, 1)


def _from_blocks(a):
    nb, B, blk = a.shape[:3]
    return a.swapaxes(0, 1).reshape((B, nb * blk) + a.shape[3:])


def _attend_block(q, k, v, scale, mask=None, sink=None):
    s = jnp.einsum('bqhgd,bkhd->bhgqk', q, k, preferred_element_type=F32) * scale
    if mask is not None:
        s = jnp.where(mask, s, NEG_INF)
    if sink is not None:
        sink_col = jnp.broadcast_to(sink.astype(F32)[None, :, :, None, None], s.shape[:-1] + (1,))
        p = jax.nn.softmax(jnp.concatenate([s, sink_col], axis=-1), axis=-1)[..., :-1]
    else:
        p = jax.nn.softmax(s, axis=-1)
    return jnp.einsum('bhgqk,bkhe->bqhge', p.astype(v.dtype), v)


def _dense_attention(q, k, v, scale, sink=None):
    out = lax.map(lambda qb: _attend_block(qb, k, v, scale, sink=sink), _to_blocks(q, ATTN_BLOCK))
    return _from_blocks(out)


def _window_attention(q, k, v, k_ctx, v_ctx, scale, sink):
    n = q.shape[1]
    blk = SWA_BLOCK
    nb = n // blk
    pad = ((0, 0), (blk, blk), (0, 0), (0, 0))
    k_pad = jnp.pad(k, pad)
    v_pad = jnp.pad(v, pad)
    rel_k = jnp.arange(3 * blk) - blk
    band = jnp.abs(rel_k[None, :] - jnp.arange(blk)[:, None]) <= SWA_WINDOW
    ctx_mask = jnp.ones((blk, k_ctx.shape[1]), dtype=bool)

    def one_block(args):
        i, qb = args
        kpos = i * blk + rel_k
        local = band & ((kpos >= 0) & (kpos < n))[None, :]
        kb = lax.dynamic_slice_in_dim(k_pad, i * blk, 3 * blk, axis=1)
        vb = lax.dynamic_slice_in_dim(v_pad, i * blk, 3 * blk, axis=1)
        return _attend_block(qb, jnp.concatenate([kb, k_ctx], axis=1), jnp.concatenate([vb, v_ctx], axis=1),
                             scale, jnp.concatenate([local, ctx_mask], axis=1), sink)

    out = lax.map(one_block, (jnp.arange(nb), _to_blocks(q, blk)))
    return _from_blocks(out)


def _retention_chunkwise(q, k, v, log_gamma, s0):
    C = RET_CHUNK
    idx = jnp.arange(C, dtype=F32)
    diff = idx[:, None] - idx[None, :]
    intra = jnp.where(diff[None] >= 0, jnp.exp(log_gamma[:, None, None] * jnp.maximum(diff, 0.0)[None]), 0.0)
    q_dec = jnp.exp((idx[:, None] + 1.0) * log_gamma[None, :])[None, :, :, None]
    k_dec = jnp.exp((C - 1.0 - idx[:, None]) * log_gamma[None, :])[None, :, :, None]
    c_dec = jnp.exp(C * log_gamma)[None, :, None, None]

    def step(s, inp):
        qc, kc, vc = inp
        vc = vc.astype(F32)
        att = jnp.einsum('bqhd,bkhd->bhqk', qc, kc, preferred_element_type=F32) * intra[None]
        y = jnp.einsum('bhqk,bkhe->bqhe', att, vc) + jnp.einsum('bqhd,bhde->bqhe', qc.astype(F32), s) * q_dec
        s = s * c_dec + jnp.einsum('bkhd,bkhe->bhde', kc.astype(F32) * k_dec, vc)
        return s, y

    s, ys = lax.scan(step, s0, (_to_blocks(q, C), _to_blocks(k, C), _to_blocks(v, C)))
    return _from_blocks(ys), s


def _retention_state(k, v, log_gamma):
    n = k.shape[1]
    w = jnp.exp((n - 1.0 - jnp.arange(n, dtype=F32))[:, None] * log_gamma[None, :])
    return jnp.einsum('bjhd,bjhe,jh->bhde', k.astype(F32), v.astype(F32), w)


def _mixer_ab(xn, hn, w_in, w_out, q_norm, kv_norm, wq_b, wkv_b, v_norm, w_s, b_s, with_ctx_out):
    n = xn.shape[1]
    rope = _axial_rope(n, MLA_ROPE)
    scale = (MLA_NOPE + MLA_ROPE) ** -0.5

    def kv_side(z, rope_):
        kv_lat, k_pe = _split_cols(z[..., :AB_KV_WIDTH], [MLA_KV_LORA, MLA_ROPE])
        kv = _heads(_rms(kv_lat, kv_norm) @ wkv_b, MLA_HEADS)
        k_nope, v = jnp.split(kv, [MLA_NOPE], axis=-1)
        k_pe = k_pe[:, :, None, :]
        if rope_ is not None:
            k_pe = _apply_rope(k_pe, rope_)
        k = jnp.concatenate([k_nope, jnp.broadcast_to(k_pe, k_nope.shape[:3] + (MLA_ROPE,))], axis=-1)
        return k, v

    def q_side(z, rope_):
        q_lat = z[..., AB_KV_WIDTH:AB_KV_WIDTH + MLA_Q_LORA]
        q = _heads(_rms(q_lat, q_norm) @ wq_b, MLA_HEADS)
        if rope_ is not None:
            q = jnp.concatenate([q[..., :MLA_NOPE], _apply_rope(q[..., MLA_NOPE:], rope_)], axis=-1)
        return q[:, :, :, None, :]

    def chunk_mlp(z):
        u, v = jnp.split(jax.nn.gelu(z[..., AB_KV_WIDTH + MLA_Q_LORA:]), 2, axis=-1)
        v = _group_rms(v, v_norm, CMLP_GROUPS)
        B, m = v.shape[:2]
        v = v.reshape(B, m // CMLP_CHUNK, CMLP_CHUNK, CMLP_GROUPS, CMLP_GROUP_DIM)
        v = jnp.einsum('gpq,bcqgd->bcpgd', w_s, v) + b_s.T[None, None, :, :, None]
        return u * v.reshape(B, m, CMLP_WIDTH)

    def merge(o, z):
        o = o.reshape(o.shape[:2] + (MLA_HEADS * MLA_V,))
        return jnp.concatenate([o, chunk_mlp(z)], axis=-1) @ w_out

    zx = xn @ w_in
    zh = hn @ (w_in if with_ctx_out else w_in[:, :AB_KV_WIDTH])
    kx, vx = kv_side(zx, rope)
    kh, vh = kv_side(zh, None)
    ox = _dense_attention(q_side(zx, rope), jnp.concatenate([kx, kh], axis=1),
                          jnp.concatenate([vx, vh], axis=1), scale)
    yx = merge(ox, zx)
    yh = None
    if with_ctx_out:
        yh = merge(_dense_attention(q_side(zh, None), kh, vh, scale), zh)
    return yx, yh


def _mixer_cd(xn, hn, w_in, w_out, dec_f, dec_b, ret_norm, sink, with_ctx_out):
    B, n = xn.shape[:2]
    rope_ret = _axial_rope(n, RET_QK)
    rope_swa = _axial_rope(n, SWA_HEAD_DIM)
    lg_f = jax.nn.log_sigmoid(dec_f.astype(F32))
    lg_b = jax.nn.log_sigmoid(dec_b.astype(F32))
    swa_scale = SWA_HEAD_DIM ** -0.5
    groups = SWA_Q_HEADS // SWA_KV_HEADS
    sink_g = sink.reshape(SWA_KV_HEADS, groups)

    def kv_side(z, use_rope):
        rk, rv, sk, sv = _split_cols(z[..., :CD_KV_WIDTH], [RET_QK_W, RET_V_W, SWA_KV_W, SWA_KV_W])
        rk = _heads(rk, RET_HEADS) * (RET_QK ** -0.5)
        sk = _heads(sk, SWA_KV_HEADS)
        if use_rope:
            rk = _apply_rope(rk, rope_ret)
            sk = _apply_rope(sk, rope_swa)
        return rk, _heads(rv, RET_HEADS), sk, _heads(sv, SWA_KV_HEADS)

    def q_side(z, use_rope):
        rq, rg, sq = _split_cols(z[..., CD_KV_WIDTH:], [RET_QK_W, RET_V_W, SWA_Q_W])
        rq = _heads(rq, RET_HEADS)
        sq = _heads(sq, SWA_Q_HEADS)
        if use_rope:
            rq = _apply_rope(rq, rope_ret)
            sq = _apply_rope(sq, rope_swa)
        return rq, rg, sq.reshape(sq.shape[:2] + (SWA_KV_HEADS, groups, SWA_HEAD_DIM))

    def merge(y_ret, gate, o_swa, dtype):
        y_ret = _group_rms(y_ret.reshape(y_ret.shape[:2] + (RET_V_W,)), ret_norm, RET_HEADS).astype(dtype)
        y_ret = y_ret * jax.nn.silu(gate)
        return jnp.concatenate([y_ret, o_swa.reshape(o_swa.shape[:2] + (SWA_Q_W,))], axis=-1) @ w_out

    zx = xn @ w_in
    zh = hn @ (w_in if with_ctx_out else w_in[:, :CD_KV_WIDTH])
    rkh, rvh, skh, svh = kv_side(zh, False)
    yh = None
    if with_ctx_out:
        rqh, rgh, sqh = q_side(zh, False)
        s0 = jnp.zeros((B, RET_HEADS, RET_QK, RET_V), F32)
        yh_f, s_f = _retention_chunkwise(rqh, rkh, rvh, lg_f, s0)
        yh_b, s_b = _retention_chunkwise(_flip(rqh), _flip(rkh), _flip(rvh), lg_b, s0)
        yh = merge(yh_f + _flip(yh_b), rgh, _dense_attention(sqh, skh, svh, swa_scale, sink_g), hn.dtype)
    else:
        s_f = _retention_state(rkh, rvh, lg_f)
        s_b = _retention_state(_flip(rkh), _flip(rvh), lg_b)
    rkx, rvx, skx, svx = kv_side(zx, True)
    rqx, rgx, sqx = q_side(zx, True)
    yx_f, _ = _retention_chunkwise(rqx, rkx, rvx, lg_f, s_f)
    yx_b, _ = _retention_chunkwise(_flip(rqx), _flip(rkx), _flip(rvx), lg_b, s_b)
    ox = _window_attention(sqx, skx, svx, skh, svh, swa_scale, sink_g)
    yx = merge(yx_f + _flip(yx_b), rgx, ox, xn.dtype)
    return yx, yh


def _swiglu(x, w_in, w_out):
    a, b = jnp.split(x @ w_in, 2, axis=-1)
    return (jax.nn.silu(a) * b) @ w_out


def _fwd_setup_inputs(seed: int = 0) -> dict:
    key = jax.random.key(seed)
    ks = iter(jax.random.split(key, 32))
    D = D_MODEL

    def nrm(shape, s):
        return jax.random.normal(next(ks), shape, F32) * s

    decay_logit = jnp.asarray(np.log(2.0 ** (5 + np.arange(RET_HEADS)) - 1.0), dtype=F32)
    return dict(
        x=nrm((BATCH, SEQ, D), 1.0),
        c=nrm((BATCH, D), 1.0),
        ctx=nrm((BATCH, CTX_LEN, D), 1.0),
        c_ctx=nrm((D,), 1.0),
        ada_w=nrm((DEPTH, D, 6 * D), 0.5 * D ** -0.5),
        ada_b=nrm((DEPTH, 6 * D), 0.02),
        norm_mix=1.0 + nrm((DEPTH, D), 0.02),
        norm_ffn=1.0 + nrm((DEPTH, D), 0.02),
        norm_final=1.0 + nrm((D,), 0.02),
        ffn_in=nrm((DEPTH, D, 2 * FFN_HIDDEN), D ** -0.5),
        ffn_out=nrm((DEPTH, FFN_HIDDEN, D), FFN_HIDDEN ** -0.5),
        ab_in=nrm((N_EVEN, D, AB_IN), D ** -0.5),
        ab_out=nrm((N_EVEN, AB_OUT, D), AB_OUT ** -0.5),
        mla_q_norm=1.0 + nrm((N_EVEN, MLA_Q_LORA), 0.02),
        mla_kv_norm=1.0 + nrm((N_EVEN, MLA_KV_LORA), 0.02),
        mla_wq_b=nrm((N_EVEN, MLA_Q_LORA, MLA_HEADS * (MLA_NOPE + MLA_ROPE)), MLA_Q_LORA ** -0.5),
        mla_wkv_b=nrm((N_EVEN, MLA_KV_LORA, MLA_HEADS * (MLA_NOPE + MLA_V)), MLA_KV_LORA ** -0.5),
        cmlp_v_norm=1.0 + nrm((N_EVEN, CMLP_WIDTH), 0.02),
        cmlp_ws=nrm((N_EVEN, CMLP_GROUPS, CMLP_CHUNK, CMLP_CHUNK), CMLP_CHUNK ** -0.5),
        cmlp_bs=1.0 + nrm((N_EVEN, CMLP_GROUPS, CMLP_CHUNK), 0.02),
        cd_in=nrm((N_ODD, D, CD_IN), D ** -0.5),
        cd_out=nrm((N_ODD, CD_OUT, D), CD_OUT ** -0.5),
        ret_decay_fwd=decay_logit + nrm((N_ODD, RET_HEADS), 0.05),
        ret_decay_bwd=decay_logit + nrm((N_ODD, RET_HEADS), 0.05),
        ret_norm=1.0 + nrm((N_ODD, RET_V_W), 0.02),
        swa_sink=nrm((N_ODD, SWA_Q_HEADS), 0.5),
    )


def _fwd_reference(x, c, ctx, c_ctx, ada_w, ada_b, norm_mix, norm_ffn, norm_final, ffn_in, ffn_out,
              ab_in, ab_out, mla_q_norm, mla_kv_norm, mla_wq_b, mla_wkv_b, cmlp_v_norm, cmlp_ws, cmlp_bs,
              cd_in, cd_out, ret_decay_fwd, ret_decay_bwd, ret_norm, swa_sink):
    h = ctx
    cond = jax.nn.silu(c)
    cond_ctx = jax.nn.silu(c_ctx)
    for layer in range(DEPTH):
        last = layer == DEPTH - 1
        mx = [m[:, None, :] for m in jnp.split(cond @ ada_w[layer] + ada_b[layer], 6, axis=-1)]
        mh = jnp.split(cond_ctx @ ada_w[layer] + ada_b[layer], 6, axis=-1)
        xn = _modulate(_rms(x, norm_mix[layer]), mx[0], mx[1])
        hn = _modulate(_rms(h, norm_mix[layer]), mh[0], mh[1])
        j = layer // 2
        if layer % 2 == 0:
            yx, yh = _mixer_ab(xn, hn, ab_in[j], ab_out[j], mla_q_norm[j], mla_kv_norm[j], mla_wq_b[j],
                               mla_wkv_b[j], cmlp_v_norm[j], cmlp_ws[j], cmlp_bs[j], not last)
        else:
            yx, yh = _mixer_cd(xn, hn, cd_in[j], cd_out[j], ret_decay_fwd[j], ret_decay_bwd[j],
                               ret_norm[j], swa_sink[j], not last)
        x = x + mx[2] * yx
        x = x + mx[5] * _swiglu(_modulate(_rms(x, norm_ffn[layer]), mx[3], mx[4]), ffn_in[layer], ffn_out[layer])
        if not last:
            h = h + mh[2] * yh
            h = h + mh[5] * _swiglu(_modulate(_rms(h, norm_ffn[layer]), mh[3], mh[4]), ffn_in[layer], ffn_out[layer])
    return _rms(x, norm_final)


import jax as _jax
import jax.numpy as _jnp

TWIN_FORMAT = 'train_step'
FWD_PARAMS = ['x', 'c', 'ctx', 'c_ctx', 'ada_w', 'ada_b', 'norm_mix', 'norm_ffn', 'norm_final', 'ffn_in', 'ffn_out', 'ab_in', 'ab_out', 'mla_q_norm', 'mla_kv_norm', 'mla_wq_b', 'mla_wkv_b', 'cmlp_v_norm', 'cmlp_ws', 'cmlp_bs', 'cd_in', 'cd_out', 'ret_decay_fwd', 'ret_decay_bwd', 'ret_norm', 'swa_sink']
TWIN_WEIGHTS = ['c_ctx', 'ada_w', 'ada_b', 'norm_mix', 'norm_ffn', 'norm_final', 'ffn_in', 'ffn_out', 'ab_in', 'ab_out', 'mla_q_norm', 'mla_kv_norm', 'mla_wq_b', 'mla_wkv_b', 'cmlp_v_norm', 'cmlp_ws', 'cmlp_bs', 'cd_in', 'cd_out', 'ret_decay_fwd', 'ret_decay_bwd', 'ret_norm', 'swa_sink']
TWIN_DIFF_INPUT = 'x'
TWIN_INPUTS = ['x', 'c', 'ctx', 'c_ctx', 'ada_w', 'ada_b', 'norm_mix', 'norm_ffn', 'norm_final', 'ffn_in', 'ffn_out', 'ab_in', 'ab_out', 'mla_q_norm', 'mla_kv_norm', 'mla_wq_b', 'mla_wkv_b', 'cmlp_v_norm', 'cmlp_ws', 'cmlp_bs', 'cd_in', 'cd_out', 'ret_decay_fwd', 'ret_decay_bwd', 'ret_norm', 'swa_sink', 'loss_target', 'm_c_ctx', 'm_ada_w', 'm_ada_b', 'm_norm_mix', 'm_norm_ffn', 'm_norm_final', 'm_ffn_in', 'm_ffn_out', 'm_ab_in', 'm_ab_out', 'm_mla_q_norm', 'm_mla_kv_norm', 'm_mla_wq_b', 'm_mla_wkv_b', 'm_cmlp_v_norm', 'm_cmlp_ws', 'm_cmlp_bs', 'm_cd_in', 'm_cd_out', 'm_ret_decay_fwd', 'm_ret_decay_bwd', 'm_ret_norm', 'm_swa_sink', 'v_c_ctx', 'v_ada_w', 'v_ada_b', 'v_norm_mix', 'v_norm_ffn', 'v_norm_final', 'v_ffn_in', 'v_ffn_out', 'v_ab_in', 'v_ab_out', 'v_mla_q_norm', 'v_mla_kv_norm', 'v_mla_wq_b', 'v_mla_wkv_b', 'v_cmlp_v_norm', 'v_cmlp_ws', 'v_cmlp_bs', 'v_cd_in', 'v_cd_out', 'v_ret_decay_fwd', 'v_ret_decay_bwd', 'v_ret_norm', 'v_swa_sink']
TWIN_OUTPUTS = ['loss', 'grad_x', 'grad_c_ctx', 'grad_ada_w', 'grad_ada_b', 'grad_norm_mix', 'grad_norm_ffn', 'grad_norm_final', 'grad_ffn_in', 'grad_ffn_out', 'grad_ab_in', 'grad_ab_out', 'grad_mla_q_norm', 'grad_mla_kv_norm', 'grad_mla_wq_b', 'grad_mla_wkv_b', 'grad_cmlp_v_norm', 'grad_cmlp_ws', 'grad_cmlp_bs', 'grad_cd_in', 'grad_cd_out', 'grad_ret_decay_fwd', 'grad_ret_decay_bwd', 'grad_ret_norm', 'grad_swa_sink', 'delta_c_ctx', 'delta_ada_w', 'delta_ada_b', 'delta_norm_mix', 'delta_norm_ffn', 'delta_norm_final', 'delta_ffn_in', 'delta_ffn_out', 'delta_ab_in', 'delta_ab_out', 'delta_mla_q_norm', 'delta_mla_kv_norm', 'delta_mla_wq_b', 'delta_mla_wkv_b', 'delta_cmlp_v_norm', 'delta_cmlp_ws', 'delta_cmlp_bs', 'delta_cd_in', 'delta_cd_out', 'delta_ret_decay_fwd', 'delta_ret_decay_bwd', 'delta_ret_norm', 'delta_swa_sink', 'new_m_c_ctx', 'new_m_ada_w', 'new_m_ada_b', 'new_m_norm_mix', 'new_m_norm_ffn', 'new_m_norm_final', 'new_m_ffn_in', 'new_m_ffn_out', 'new_m_ab_in', 'new_m_ab_out', 'new_m_mla_q_norm', 'new_m_mla_kv_norm', 'new_m_mla_wq_b', 'new_m_mla_wkv_b', 'new_m_cmlp_v_norm', 'new_m_cmlp_ws', 'new_m_cmlp_bs', 'new_m_cd_in', 'new_m_cd_out', 'new_m_ret_decay_fwd', 'new_m_ret_decay_bwd', 'new_m_ret_norm', 'new_m_swa_sink', 'new_v_c_ctx', 'new_v_ada_w', 'new_v_ada_b', 'new_v_norm_mix', 'new_v_norm_ffn', 'new_v_norm_final', 'new_v_ffn_in', 'new_v_ffn_out', 'new_v_ab_in', 'new_v_ab_out', 'new_v_mla_q_norm', 'new_v_mla_kv_norm', 'new_v_mla_wq_b', 'new_v_mla_wkv_b', 'new_v_cmlp_v_norm', 'new_v_cmlp_ws', 'new_v_cmlp_bs', 'new_v_cd_in', 'new_v_cd_out', 'new_v_ret_decay_fwd', 'new_v_ret_decay_bwd', 'new_v_ret_norm', 'new_v_swa_sink']
TWIN_LEAF_KINDS = {'loss': 'loss', 'grad_x': 'grad_x', 'grad_c_ctx': 'grad_w', 'grad_ada_w': 'grad_w', 'grad_ada_b': 'grad_w', 'grad_norm_mix': 'grad_w', 'grad_norm_ffn': 'grad_w', 'grad_norm_final': 'grad_w', 'grad_ffn_in': 'grad_w', 'grad_ffn_out': 'grad_w', 'grad_ab_in': 'grad_w', 'grad_ab_out': 'grad_w', 'grad_mla_q_norm': 'grad_w', 'grad_mla_kv_norm': 'grad_w', 'grad_mla_wq_b': 'grad_w', 'grad_mla_wkv_b': 'grad_w', 'grad_cmlp_v_norm': 'grad_w', 'grad_cmlp_ws': 'grad_w', 'grad_cmlp_bs': 'grad_w', 'grad_cd_in': 'grad_w', 'grad_cd_out': 'grad_w', 'grad_ret_decay_fwd': 'grad_w', 'grad_ret_decay_bwd': 'grad_w', 'grad_ret_norm': 'grad_w', 'grad_swa_sink': 'grad_w', 'delta_c_ctx': 'delta_w', 'delta_ada_w': 'delta_w', 'delta_ada_b': 'delta_w', 'delta_norm_mix': 'delta_w', 'delta_norm_ffn': 'delta_w', 'delta_norm_final': 'delta_w', 'delta_ffn_in': 'delta_w', 'delta_ffn_out': 'delta_w', 'delta_ab_in': 'delta_w', 'delta_ab_out': 'delta_w', 'delta_mla_q_norm': 'delta_w', 'delta_mla_kv_norm': 'delta_w', 'delta_mla_wq_b': 'delta_w', 'delta_mla_wkv_b': 'delta_w', 'delta_cmlp_v_norm': 'delta_w', 'delta_cmlp_ws': 'delta_w', 'delta_cmlp_bs': 'delta_w', 'delta_cd_in': 'delta_w', 'delta_cd_out': 'delta_w', 'delta_ret_decay_fwd': 'delta_w', 'delta_ret_decay_bwd': 'delta_w', 'delta_ret_norm': 'delta_w', 'delta_swa_sink': 'delta_w', 'new_m_c_ctx': 'new_m', 'new_m_ada_w': 'new_m', 'new_m_ada_b': 'new_m', 'new_m_norm_mix': 'new_m', 'new_m_norm_ffn': 'new_m', 'new_m_norm_final': 'new_m', 'new_m_ffn_in': 'new_m', 'new_m_ffn_out': 'new_m', 'new_m_ab_in': 'new_m', 'new_m_ab_out': 'new_m', 'new_m_mla_q_norm': 'new_m', 'new_m_mla_kv_norm': 'new_m', 'new_m_mla_wq_b': 'new_m', 'new_m_mla_wkv_b': 'new_m', 'new_m_cmlp_v_norm': 'new_m', 'new_m_cmlp_ws': 'new_m', 'new_m_cmlp_bs': 'new_m', 'new_m_cd_in': 'new_m', 'new_m_cd_out': 'new_m', 'new_m_ret_decay_fwd': 'new_m', 'new_m_ret_decay_bwd': 'new_m', 'new_m_ret_norm': 'new_m', 'new_m_swa_sink': 'new_m', 'new_v_c_ctx': 'new_v', 'new_v_ada_w': 'new_v', 'new_v_ada_b': 'new_v', 'new_v_norm_mix': 'new_v', 'new_v_norm_ffn': 'new_v', 'new_v_norm_final': 'new_v', 'new_v_ffn_in': 'new_v', 'new_v_ffn_out': 'new_v', 'new_v_ab_in': 'new_v', 'new_v_ab_out': 'new_v', 'new_v_mla_q_norm': 'new_v', 'new_v_mla_kv_norm': 'new_v', 'new_v_mla_wq_b': 'new_v', 'new_v_mla_wkv_b': 'new_v', 'new_v_cmlp_v_norm': 'new_v', 'new_v_cmlp_ws': 'new_v', 'new_v_cmlp_bs': 'new_v', 'new_v_cd_in': 'new_v', 'new_v_cd_out': 'new_v', 'new_v_ret_decay_fwd': 'new_v', 'new_v_ret_decay_bwd': 'new_v', 'new_v_ret_norm': 'new_v', 'new_v_swa_sink': 'new_v'}


def _forward(args):
    return _fwd_reference(*[args[k] for k in FWD_PARAMS])


def _output_shape():
    out = _jax.eval_shape(lambda: _forward(_fwd_setup_inputs(0)))
    return out.shape, out.dtype

N_MICROBATCH = 1
ADAM_LR = 0.001
ADAM_B1 = 0.9
ADAM_B2 = 0.999
ADAM_EPS = 1e-08
ADAM_WD = 0.01
ADAM_STEP = 10
PER_EXAMPLE_BATCH_AXIS = {'x': 0, 'c': 0, 'ctx': 0, 'loss_target': 0}
SHARED_INPUTS = []
_WEIGHT_DTYPES = {'c_ctx': _jnp.float32, 'ada_w': _jnp.float32, 'ada_b': _jnp.float32, 'norm_mix': _jnp.float32, 'norm_ffn': _jnp.float32, 'norm_final': _jnp.float32, 'ffn_in': _jnp.float32, 'ffn_out': _jnp.float32, 'ab_in': _jnp.float32, 'ab_out': _jnp.float32, 'mla_q_norm': _jnp.float32, 'mla_kv_norm': _jnp.float32, 'mla_wq_b': _jnp.float32, 'mla_wkv_b': _jnp.float32, 'cmlp_v_norm': _jnp.float32, 'cmlp_ws': _jnp.float32, 'cmlp_bs': _jnp.float32, 'cd_in': _jnp.float32, 'cd_out': _jnp.float32, 'ret_decay_fwd': _jnp.float32, 'ret_decay_bwd': _jnp.float32, 'ret_norm': _jnp.float32, 'swa_sink': _jnp.float32}
MOMENT_SCALE = {'c_ctx': 3.624454e-02, 'ada_w': 7.642674e-02, 'ada_b': 1.235927e-01, 'norm_mix': 6.787001e-02, 'norm_ffn': 7.365174e-02, 'norm_final': 6.410774e+01, 'ffn_in': 3.280478e-02, 'ffn_out': 5.350201e-02, 'ab_in': 5.997319e-02, 'ab_out': 6.177291e-02, 'mla_q_norm': 1.076169e-02, 'mla_kv_norm': 3.444380e-02, 'mla_wq_b': 6.331540e-03, 'mla_wkv_b': 1.861183e-02, 'cmlp_v_norm': 5.868921e-02, 'cmlp_ws': 5.759196e-02, 'cmlp_bs': 6.483866e-02, 'cd_in': 5.337776e-02, 'cd_out': 3.640481e-02, 'ret_decay_fwd': 2.090422e-01, 'ret_decay_bwd': 5.348397e-01, 'ret_norm': 4.815582e-02, 'swa_sink': 5.182539e-04}


def _to_microbatches(a, axis):
    t = _jnp.moveaxis(a, axis, 0)
    t = t.reshape((N_MICROBATCH, t.shape[0] // N_MICROBATCH) + t.shape[1:])
    return _jnp.moveaxis(t, 1, axis + 1)


def setup_inputs(seed: int = 0) -> dict:
    inp = _fwd_setup_inputs(seed)
    key = _jax.random.fold_in(_jax.random.key(seed), 7919)
    shape, _ = _output_shape()
    out = dict(inp)
    out["loss_target"] = _jax.random.normal(_jax.random.fold_in(key, 0), shape, _jnp.float32)
    for i, name in enumerate(TWIN_WEIGHTS):
        w = inp[name].astype(_jnp.float32)
        if MOMENT_SCALE is None:
            s = _jnp.sqrt(_jnp.mean(_jnp.square(w)) + 1e-30)
        else:
            s = MOMENT_SCALE[name]
        km, kv = _jax.random.split(_jax.random.fold_in(key, i + 1))
        out[name] = w
        out["m_" + name] = s * _jax.random.normal(km, w.shape, _jnp.float32)
        out["v_" + name] = (s * s) * _jax.random.uniform(kv, w.shape, _jnp.float32, 0.5, 1.5)
    if N_MICROBATCH > 1:
        for name, axis in PER_EXAMPLE_BATCH_AXIS.items():
            out[name] = _to_microbatches(out[name], axis)
    return {'x': out['x'], 'c': out['c'], 'ctx': out['ctx'], 'c_ctx': out['c_ctx'], 'ada_w': out['ada_w'], 'ada_b': out['ada_b'], 'norm_mix': out['norm_mix'], 'norm_ffn': out['norm_ffn'], 'norm_final': out['norm_final'], 'ffn_in': out['ffn_in'], 'ffn_out': out['ffn_out'], 'ab_in': out['ab_in'], 'ab_out': out['ab_out'], 'mla_q_norm': out['mla_q_norm'], 'mla_kv_norm': out['mla_kv_norm'], 'mla_wq_b': out['mla_wq_b'], 'mla_wkv_b': out['mla_wkv_b'], 'cmlp_v_norm': out['cmlp_v_norm'], 'cmlp_ws': out['cmlp_ws'], 'cmlp_bs': out['cmlp_bs'], 'cd_in': out['cd_in'], 'cd_out': out['cd_out'], 'ret_decay_fwd': out['ret_decay_fwd'], 'ret_decay_bwd': out['ret_decay_bwd'], 'ret_norm': out['ret_norm'], 'swa_sink': out['swa_sink'], 'loss_target': out['loss_target'], 'm_c_ctx': out['m_c_ctx'], 'm_ada_w': out['m_ada_w'], 'm_ada_b': out['m_ada_b'], 'm_norm_mix': out['m_norm_mix'], 'm_norm_ffn': out['m_norm_ffn'], 'm_norm_final': out['m_norm_final'], 'm_ffn_in': out['m_ffn_in'], 'm_ffn_out': out['m_ffn_out'], 'm_ab_in': out['m_ab_in'], 'm_ab_out': out['m_ab_out'], 'm_mla_q_norm': out['m_mla_q_norm'], 'm_mla_kv_norm': out['m_mla_kv_norm'], 'm_mla_wq_b': out['m_mla_wq_b'], 'm_mla_wkv_b': out['m_mla_wkv_b'], 'm_cmlp_v_norm': out['m_cmlp_v_norm'], 'm_cmlp_ws': out['m_cmlp_ws'], 'm_cmlp_bs': out['m_cmlp_bs'], 'm_cd_in': out['m_cd_in'], 'm_cd_out': out['m_cd_out'], 'm_ret_decay_fwd': out['m_ret_decay_fwd'], 'm_ret_decay_bwd': out['m_ret_decay_bwd'], 'm_ret_norm': out['m_ret_norm'], 'm_swa_sink': out['m_swa_sink'], 'v_c_ctx': out['v_c_ctx'], 'v_ada_w': out['v_ada_w'], 'v_ada_b': out['v_ada_b'], 'v_norm_mix': out['v_norm_mix'], 'v_norm_ffn': out['v_norm_ffn'], 'v_norm_final': out['v_norm_final'], 'v_ffn_in': out['v_ffn_in'], 'v_ffn_out': out['v_ffn_out'], 'v_ab_in': out['v_ab_in'], 'v_ab_out': out['v_ab_out'], 'v_mla_q_norm': out['v_mla_q_norm'], 'v_mla_kv_norm': out['v_mla_kv_norm'], 'v_mla_wq_b': out['v_mla_wq_b'], 'v_mla_wkv_b': out['v_mla_wkv_b'], 'v_cmlp_v_norm': out['v_cmlp_v_norm'], 'v_cmlp_ws': out['v_cmlp_ws'], 'v_cmlp_bs': out['v_cmlp_bs'], 'v_cd_in': out['v_cd_in'], 'v_cd_out': out['v_cd_out'], 'v_ret_decay_fwd': out['v_ret_decay_fwd'], 'v_ret_decay_bwd': out['v_ret_decay_bwd'], 'v_ret_norm': out['v_ret_norm'], 'v_swa_sink': out['v_swa_sink']}


def _loss(weights, diff, rest, loss_target):
    with _jax.named_scope("forward"):
        args = {**rest, TWIN_DIFF_INPUT: diff, **{k: w.astype(_WEIGHT_DTYPES[k]) for k, w in weights.items()}}
        y = _forward(args)
    with _jax.named_scope("loss_head"):
        err = _jnp.square(y.astype(_jnp.float32) - loss_target)
        return 0.5 * _jnp.sum(_jnp.mean(err, axis=-1)) if err.ndim else 0.5 * err


def _adamw(w, g, m, v):
    m = ADAM_B1 * m + (1.0 - ADAM_B1) * g
    v = ADAM_B2 * v + (1.0 - ADAM_B2) * _jnp.square(g)
    m_hat = m / (1.0 - ADAM_B1 ** ADAM_STEP)
    v_hat = v / (1.0 - ADAM_B2 ** ADAM_STEP)
    delta = -ADAM_LR * (m_hat / (_jnp.sqrt(v_hat) + ADAM_EPS) + ADAM_WD * w)
    return delta, m, v


def reference(x, c, ctx, c_ctx, ada_w, ada_b, norm_mix, norm_ffn, norm_final, ffn_in, ffn_out, ab_in, ab_out, mla_q_norm, mla_kv_norm, mla_wq_b, mla_wkv_b, cmlp_v_norm, cmlp_ws, cmlp_bs, cd_in, cd_out, ret_decay_fwd, ret_decay_bwd, ret_norm, swa_sink, loss_target, m_c_ctx, m_ada_w, m_ada_b, m_norm_mix, m_norm_ffn, m_norm_final, m_ffn_in, m_ffn_out, m_ab_in, m_ab_out, m_mla_q_norm, m_mla_kv_norm, m_mla_wq_b, m_mla_wkv_b, m_cmlp_v_norm, m_cmlp_ws, m_cmlp_bs, m_cd_in, m_cd_out, m_ret_decay_fwd, m_ret_decay_bwd, m_ret_norm, m_swa_sink, v_c_ctx, v_ada_w, v_ada_b, v_norm_mix, v_norm_ffn, v_norm_final, v_ffn_in, v_ffn_out, v_ab_in, v_ab_out, v_mla_q_norm, v_mla_kv_norm, v_mla_wq_b, v_mla_wkv_b, v_cmlp_v_norm, v_cmlp_ws, v_cmlp_bs, v_cd_in, v_cd_out, v_ret_decay_fwd, v_ret_decay_bwd, v_ret_norm, v_swa_sink):
    given = dict(x=x, c=c, ctx=ctx, c_ctx=c_ctx, ada_w=ada_w, ada_b=ada_b, norm_mix=norm_mix, norm_ffn=norm_ffn, norm_final=norm_final, ffn_in=ffn_in, ffn_out=ffn_out, ab_in=ab_in, ab_out=ab_out, mla_q_norm=mla_q_norm, mla_kv_norm=mla_kv_norm, mla_wq_b=mla_wq_b, mla_wkv_b=mla_wkv_b, cmlp_v_norm=cmlp_v_norm, cmlp_ws=cmlp_ws, cmlp_bs=cmlp_bs, cd_in=cd_in, cd_out=cd_out, ret_decay_fwd=ret_decay_fwd, ret_decay_bwd=ret_decay_bwd, ret_norm=ret_norm, swa_sink=swa_sink, loss_target=loss_target, m_c_ctx=m_c_ctx, m_ada_w=m_ada_w, m_ada_b=m_ada_b, m_norm_mix=m_norm_mix, m_norm_ffn=m_norm_ffn, m_norm_final=m_norm_final, m_ffn_in=m_ffn_in, m_ffn_out=m_ffn_out, m_ab_in=m_ab_in, m_ab_out=m_ab_out, m_mla_q_norm=m_mla_q_norm, m_mla_kv_norm=m_mla_kv_norm, m_mla_wq_b=m_mla_wq_b, m_mla_wkv_b=m_mla_wkv_b, m_cmlp_v_norm=m_cmlp_v_norm, m_cmlp_ws=m_cmlp_ws, m_cmlp_bs=m_cmlp_bs, m_cd_in=m_cd_in, m_cd_out=m_cd_out, m_ret_decay_fwd=m_ret_decay_fwd, m_ret_decay_bwd=m_ret_decay_bwd, m_ret_norm=m_ret_norm, m_swa_sink=m_swa_sink, v_c_ctx=v_c_ctx, v_ada_w=v_ada_w, v_ada_b=v_ada_b, v_norm_mix=v_norm_mix, v_norm_ffn=v_norm_ffn, v_norm_final=v_norm_final, v_ffn_in=v_ffn_in, v_ffn_out=v_ffn_out, v_ab_in=v_ab_in, v_ab_out=v_ab_out, v_mla_q_norm=v_mla_q_norm, v_mla_kv_norm=v_mla_kv_norm, v_mla_wq_b=v_mla_wq_b, v_mla_wkv_b=v_mla_wkv_b, v_cmlp_v_norm=v_cmlp_v_norm, v_cmlp_ws=v_cmlp_ws, v_cmlp_bs=v_cmlp_bs, v_cd_in=v_cd_in, v_cd_out=v_cd_out, v_ret_decay_fwd=v_ret_decay_fwd, v_ret_decay_bwd=v_ret_decay_bwd, v_ret_norm=v_ret_norm, v_swa_sink=v_swa_sink)
    weights = {n: given[n] for n in TWIN_WEIGHTS}
    shared = {n: given[n] for n in SHARED_INPUTS}
    per_example = {n: given[n] for n in ['x', 'c', 'ctx']}
    grad_fn = _jax.value_and_grad(_loss, argnums=(0, 1))

    def one_microbatch(ex, loss_target):
        ex = dict(ex)
        diff = ex.pop(TWIN_DIFF_INPUT)
        return grad_fn(weights, diff, {**shared, **ex}, loss_target)

    if N_MICROBATCH == 1:
        loss, (grad_w, grad_x) = one_microbatch(per_example, given["loss_target"])
    else:
        def body(carry, xs):
            loss_sum, grad_sum = carry
            l_k, (gw_k, gx_k) = one_microbatch(xs[0], xs[1])
            with _jax.named_scope("update"):
                return (loss_sum + l_k, _jax.tree.map(_jnp.add, grad_sum, gw_k)), gx_k

        init = (_jnp.zeros((), _jnp.float32), _jax.tree.map(_jnp.zeros_like, weights))
        (loss, grad_w), grad_x = _jax.lax.scan(body, init, (per_example, given["loss_target"]))
    with _jax.named_scope("update"):
        delta_w, new_m, new_v = {}, {}, {}
        for n in TWIN_WEIGHTS:
            delta_w[n], new_m[n], new_v[n] = _adamw(weights[n], grad_w[n], given["m_" + n], given["v_" + n])
    return (loss, grad_x, *[grad_w[n] for n in TWIN_WEIGHTS], *[delta_w[n] for n in TWIN_WEIGHTS],
            *[new_m[n] for n in TWIN_WEIGHTS], *[new_v[n] for n in TWIN_WEIGHTS])
```

```python
import functools

import numpy as np
import jax
import jax.numpy as jnp
from jax import lax
from jax.experimental import pallas as pl
from jax.experimental.pallas import tpu as pltpu

F32 = jnp.float32
BF16 = jnp.bfloat16
EPS = 1e-6
NEG_INF = -1e30
N_DEV = 8
GRID_W = 64
ROPE_THETA = 10000.0
ROPE_DIM = 64
SWA_WINDOW = 128
MLA_SCALE = (128 + 64) ** -0.5
SWA_SCALE = 64 ** -0.5
RET_K_SCALE = 64 ** -0.5
ADAM_LR, ADAM_B1, ADAM_B2, ADAM_EPS, ADAM_WD, ADAM_STEP = 0.001, 0.9, 0.999, 1e-08, 0.01, 10
V7X_VMEM_LIMIT = 56 * 1024 * 1024
MESH = pl.DeviceIdType.MESH


def _pallas(body, **kw):
    return pl.pallas_call(body, **kw)


def _params(sem=None):
    return pltpu.CompilerParams(dimension_semantics=sem, vmem_limit_bytes=V7X_VMEM_LIMIT)


def _tile(n, cap, align):
    best = None
    for t in range(align, min(n, cap) + 1, align):
        if n % t == 0:
            best = t
    return n if best is None else best


def _sds(shape, dtype):
    return jax.ShapeDtypeStruct(tuple(shape), dtype)


def _ew(name, f, ins, out_dtypes, cap_elems=131072):
    R, C = ins[0].shape[-2:]
    tr = _tile(R, max(16, cap_elems // C), 16)
    n_in = len(ins)

    def spec(a):
        if a.ndim == 2:
            return pl.BlockSpec((tr, C), lambda i: (i, 0))
        return pl.BlockSpec((a.shape[0], tr, C), lambda i: (0, i, 0))

    def body(*refs):
        outs = f(*[r[...] for r in refs[:n_in]])
        for r, o in zip(refs[n_in:], outs):
            r[...] = o.astype(r.dtype)

    return _pallas(
        body, name=name, grid=(R // tr,), in_specs=[spec(a) for a in ins],
        out_specs=[pl.BlockSpec((tr, C), lambda i: (i, 0)) for _ in out_dtypes],
        out_shape=[_sds((R, C), d) for d in out_dtypes], compiler_params=_params(("parallel",)),
    )(*ins)


def _to_bf16(name, w):
    w2 = w.reshape(-1, w.shape[-1])
    return _ew(name, lambda v: (v,), [w2], [BF16])[0].reshape(w.shape)


def _adamw_math(w, g, m, v):
    m = ADAM_B1 * m + (1.0 - ADAM_B1) * g
    v = ADAM_B2 * v + (1.0 - ADAM_B2) * (g * g)
    m_hat = m / (1.0 - ADAM_B1 ** ADAM_STEP)
    v_hat = v / (1.0 - ADAM_B2 ** ADAM_STEP)
    delta = -ADAM_LR * (m_hat / (jnp.sqrt(v_hat) + ADAM_EPS) + ADAM_WD * w)
    return delta, m, v


def _sum_slots(land):
    g = land[0].astype(F32)
    for s in range(1, land.shape[0]):
        g = g + land[s].astype(F32)
    return g


def _adamw_from_slots(name, w, m, v, land):
    shp = w.shape
    C = shp[-1]

    def f(w_, m_, v_, land_):
        g = _sum_slots(land_)
        return (g,) + _adamw_math(w_, g, m_, v_)

    outs = _ew(name, f, [w.reshape(-1, C), m.reshape(-1, C), v.reshape(-1, C), land.reshape(N_DEV, -1, C)],
               [F32] * 4, cap_elems=65536)
    return [o.reshape(shp) for o in outs]


def _adamw(name, w, g, m, v):
    outs = _ew(name, lambda w_, g_, m_, v_: _adamw_math(w_, g_, m_, v_), [w, g, m, v], [F32] * 3)
    return outs


def _mm(name, a, b, ta=False, tb=False, out_dtype=F32, add=None):
    M, K = (a.shape[1], a.shape[0]) if ta else a.shape
    N = b.shape[0] if tb else b.shape[1]
    tm = _tile(M, 512, 128)
    tn = _tile(N, 1024, 128)
    if tn < 256 and N <= 2432:
        tn = N
    tk = _tile(K, 1792, 128)
    nk = K // tk
    a_spec = pl.BlockSpec((tk, tm), lambda i, j, k: (k, i)) if ta else pl.BlockSpec((tm, tk), lambda i, j, k: (i, k))
    b_spec = pl.BlockSpec((tn, tk), lambda i, j, k: (j, k)) if tb else pl.BlockSpec((tk, tn), lambda i, j, k: (k, j))
    o_spec = pl.BlockSpec((tm, tn), lambda i, j, k: (i, j))
    dims = (((0 if ta else 1,), (1 if tb else 0,)), ((), ()))
    has_add = add is not None

    def body(*refs):
        a_ref, b_ref = refs[0], refs[1]
        add_ref = refs[2] if has_add else None
        o_ref, acc = refs[-2], refs[-1]
        k = pl.program_id(2)

        @pl.when(k == 0)
        def _():
            acc[...] = add_ref[...] if has_add else jnp.zeros_like(acc)

        acc[...] += lax.dot_general(a_ref[...].astype(BF16), b_ref[...].astype(BF16), dims,
                                    preferred_element_type=F32)

        @pl.when(k == nk - 1)
        def _():
            o_ref[...] = acc[...].astype(o_ref.dtype)

    ins = [a, b] + ([add] if has_add else [])
    specs = [a_spec, b_spec] + ([o_spec] if has_add else [])
    return _pallas(
        body, name=name, grid=(M // tm, N // tn, nk), in_specs=specs, out_specs=o_spec,
        out_shape=_sds((M, N), out_dtype), scratch_shapes=[pltpu.VMEM((tm, tn), F32)],
        compiler_params=_params(("parallel", "parallel", "arbitrary")),
    )(*ins)


class _Rows:
    def __init__(self, name, f, B, L, n, tm, arrays, pieces, samp, samp_pieces, glob, out_arrays, out_pieces):
        self.name, self.f, self.B, self.L, self.n, self.tm = name, f, B, L, n, tm
        self.arrays, self.pieces, self.samp, self.samp_pieces, self.glob = arrays, pieces, samp, samp_pieces, glob
        self.out_arrays, self.out_pieces = out_arrays, out_pieces
        self.nx = n // tm
        self.grid = (B, L // tm)

    def _row_spec(self, C, batched=True):
        tm = self.tm
        if batched:
            return pl.BlockSpec((None, tm, C), lambda b, i: (b, i, 0))
        return pl.BlockSpec((tm, C), lambda b, i: (i, 0))

    def _in_specs(self):
        nx = self.nx
        specs = [self._row_spec(a.shape[-1], a.ndim == 3) for a in self.arrays]
        specs += [pl.BlockSpec((None, None) + s.shape[2:], lambda b, i: (b, i // nx, 0, 0)) for s in self.samp]
        specs += [pl.BlockSpec(g.shape, lambda b, i, nd=g.ndim: (0,) * nd) for g in self.glob]
        return specs

    def _load(self, a_refs, s_refs, g_refs):
        args = [a_refs[ai][:, cs:cs + cw].astype(F32) for ai, cs, cw in self.pieces]
        args += [s_refs[si][r:r + 1, :].astype(F32) for si, r in self.samp_pieces]
        args += [g[...].astype(F32) for g in g_refs]
        return args

    def fwd(self):
        na, ns, ng = len(self.arrays), len(self.samp), len(self.glob)

        def body(*refs):
            a_refs, s_refs, g_refs = refs[:na], refs[na:na + ns], refs[na + ns:na + ns + ng]
            o_refs = refs[na + ns + ng:]
            outs = self.f(*self._load(a_refs, s_refs, g_refs))
            for (oi, cs, cw), o in zip(self.out_pieces, outs):
                o_refs[oi][:, cs:cs + cw] = o.astype(o_refs[oi].dtype)

        return _pallas(
            body, name=self.name + "_fwd", grid=self.grid, in_specs=self._in_specs(),
            out_specs=[self._row_spec(C) for C, _ in self.out_arrays],
            out_shape=[_sds((self.B, self.L, C), d) for C, d in self.out_arrays],
            compiler_params=_params(("parallel", "parallel")),
        )(*self.arrays, *self.samp, *self.glob)

    def bwd(self, cts, grad_arrays, grad_glob=(), add=None):
        na, ns, ng, nc = len(self.arrays), len(self.samp), len(self.glob), len(cts)
        add = add or {}
        add_keys = list(add)
        g_idx = list(grad_arrays)
        nx, B = self.nx, self.B
        n_in = na + ns + ng + nc + len(add_keys)
        n_pieces, n_sp = len(self.pieces), len(self.samp_pieces)

        def body(*refs):
            a_refs, s_refs, g_refs = refs[:na], refs[na:na + ns], refs[na + ns:na + ns + ng]
            c_refs = refs[na + ns + ng:na + ns + ng + nc]
            add_refs = refs[na + ns + ng + nc:n_in]
            d_refs = refs[n_in:n_in + len(g_idx)]
            ds_refs = refs[n_in + len(g_idx):n_in + len(g_idx) + n_sp]
            dg_refs = refs[n_in + len(g_idx) + n_sp:]
            b, i = pl.program_id(0), pl.program_id(1)
            args = self._load(a_refs, s_refs, g_refs)
            _, vjp = jax.vjp(lambda *xs: tuple(self.f(*xs)), *args)
            grads = vjp(tuple(c_refs[oi][:, cs:cs + cw].astype(F32) for oi, cs, cw in self.out_pieces))
            for k, ai in enumerate(g_idx):
                covered = sum(cw for pa, _, cw in self.pieces if pa == ai)
                if covered < self.arrays[ai].shape[-1]:
                    d_refs[k][...] = jnp.zeros_like(d_refs[k])
                for (pa, cs, cw), gr in zip(self.pieces, grads[:n_pieces]):
                    if pa == ai:
                        if ai in add:
                            gr = gr + add_refs[add_keys.index(ai)][:, cs:cs + cw]
                        d_refs[k][:, cs:cs + cw] = gr.astype(d_refs[k].dtype)

            @pl.when((i == 0) | (i == nx))
            def _():
                for r in ds_refs:
                    r[...] = jnp.zeros_like(r)

            for r, gr in zip(ds_refs, grads[n_pieces:n_pieces + n_sp]):
                r[...] += gr

            @pl.when((b == 0) & (i == 0))
            def _():
                for r in dg_refs:
                    r[...] = jnp.zeros_like(r)

            for r, gi in zip(dg_refs, grad_glob):
                r[...] += grads[n_pieces + n_sp + gi]

        in_specs = self._in_specs() + [self._row_spec(c.shape[-1]) for c in cts]
        in_specs += [self._row_spec(add[k].shape[-1]) for k in add_keys]
        out_specs = [self._row_spec(self.arrays[ai].shape[-1]) for ai in g_idx]
        out_shape = [_sds(self.arrays[ai].shape, grad_arrays[ai]) for ai in g_idx]
        for si, _ in self.samp_pieces:
            C = self.samp[si].shape[-1]
            out_specs.append(pl.BlockSpec((None, None, 1, C), lambda b, i: (b, i // nx, 0, 0)))
            out_shape.append(_sds((B, 2, 1, C), F32))
        for gi in grad_glob:
            g = self.glob[gi]
            out_specs.append(pl.BlockSpec(g.shape, lambda b, i, nd=g.ndim: (0,) * nd))
            out_shape.append(_sds(g.shape, F32))
        outs = _pallas(
            body, name=self.name + "_bwd", grid=self.grid, in_specs=in_specs, out_specs=out_specs,
            out_shape=out_shape, compiler_params=_params(("arbitrary", "arbitrary")),
        )(*self.arrays, *self.samp, *self.glob, *cts, *[add[k] for k in add_keys])
        ng_ = len(g_idx)
        return outs[:ng_], outs[ng_:ng_ + n_sp], outs[ng_ + n_sp:]


def _rms(x, g):
    return x * lax.rsqrt(jnp.mean(x * x, axis=-1, keepdims=True) + EPS) * g


@jax.custom_vjp
def _swap_halves(x):
    w = x.shape[-1]
    lane = lax.broadcasted_iota(jnp.int32, x.shape, x.ndim - 1)
    up = pltpu.roll(x, w - ROPE_DIM // 2, x.ndim - 1)
    down = pltpu.roll(x, ROPE_DIM // 2, x.ndim - 1)
    return jnp.where(lane % ROPE_DIM < ROPE_DIM // 2, up, down)


_swap_halves.defvjp(lambda x: (_swap_halves(x), None), lambda _, ct: (_swap_halves(ct),))


@jax.custom_vjp
def _bdot(a, b):
    return jnp.dot(a.astype(BF16), b.astype(BF16), preferred_element_type=F32)


_bdot.defvjp(lambda a, b: (_bdot(a, b), (a, b)),
             lambda res, ct: (_dot_nt(ct.astype(BF16), res[1].astype(BF16)), _dot_tn(res[0].astype(BF16), ct.astype(BF16))))


def _rope(x, cos2, sin2):
    return x * cos2 + _swap_halves(x) * sin2


def _f_pre(s, shift, scale, g):
    return (_rms(s, g) * (1.0 + scale) + shift,)


def _f_res(s, y, gate):
    return (s + gate * y,)


def _f_swiglu(a, b):
    return (jax.nn.silu(a) * b,)


def _f_ab_pre(*args):
    kv_lat, q_lat, kpe = args[0:3]
    us, vs = args[3:7], args[7:11]
    cos2, sin2 = args[11:13]
    kv_norm, q_norm = args[13:15]
    vns, wss, bss = args[15:19], args[19:23], args[23:27]
    outs = [_rms(kv_lat, kv_norm), _rms(q_lat, q_norm), _rope(kpe, cos2, sin2)]
    for u, v, vn, ws, bs in zip(us, vs, vns, wss, bss):
        vg = _rms(jax.nn.gelu(v), vn)
        mixed = _bdot(ws, vg) + bs
        outs.append(jax.nn.gelu(u) * mixed)
    return tuple(outs)


def _f_q_rope(qn, qr, cos2, sin2):
    return qn, _rope(qr, cos2, sin2)


def _f_cd_pre(rk, rv, sk, sv, rq, rg, sq, c256, s256, c128, s128, c512, s512):
    return (_rope(rq, c256, s256), _rope(rk * RET_K_SCALE, c256, s256), rv,
            _rope(sq, c512, s512), _rope(sk, c128, s128), sv, rg)


def _f_cd_merge(*args):
    ys, rgs, rns = args[0:4], args[4:8], args[8:12]
    return tuple(_rms(y, rn) * jax.nn.silu(rg) for y, rg, rn in zip(ys, rgs, rns))


def _dot_nt(a, b):
    return lax.dot_general(a, b, (((1,), (1,)), ((), ())), preferred_element_type=F32)


def _dot_tn(a, b):
    return lax.dot_general(a, b, (((0,), (0,)), ((), ())), preferred_element_type=F32)


def _dot(a, b):
    return jnp.dot(a, b, preferred_element_type=F32)


def _tile_spec(tq, C):
    return pl.BlockSpec((None, tq, C), lambda b, i: (b, i, 0))


def _full_spec(L, C):
    return pl.BlockSpec((None, L, C), lambda b, i: (b, 0, 0))


def _mla_valid(i, tq, L, n):
    qpos = i * tq + lax.broadcasted_iota(jnp.int32, (tq, 1), 0)
    kpos = lax.broadcasted_iota(jnp.int32, (1, L), 1)
    return (qpos < n) | (kpos >= n)


def _mla_fwd(q, kv, kpe, n, tq):
    B, L, _ = q.shape

    def body(q_ref, kv_ref, kpe_ref, o_ref, lse_ref):
        valid = _mla_valid(pl.program_id(1), tq, L, n)
        kp = kpe_ref[:, 0:64]
        for h in range(4):
            s = _dot_nt(q_ref[:, 128 * h:128 * h + 128], kv_ref[:, 256 * h:256 * h + 128])
            s = s + _dot_nt(q_ref[:, 512 + 64 * h:576 + 64 * h], kp)
            s = jnp.where(valid, s * MLA_SCALE, NEG_INF)
            m = jnp.max(s, axis=1, keepdims=True)
            e = jnp.exp(s - m)
            l = jnp.sum(e, axis=1, keepdims=True)
            p = (e / l).astype(BF16)
            o_ref[:, 128 * h:128 * h + 128] = _dot(p, kv_ref[:, 256 * h + 128:256 * h + 256]).astype(o_ref.dtype)
            lse_ref[:, h:h + 1] = m + jnp.log(l)

    return _pallas(
        body, name="mla_fwd", grid=(B, L // tq),
        in_specs=[_tile_spec(tq, 768), _full_spec(L, 1024), _full_spec(L, 128)],
        out_specs=[_tile_spec(tq, 512), _tile_spec(tq, 4)],
        out_shape=[_sds((B, L, 512), BF16), _sds((B, L, 4), F32)],
        compiler_params=_params(("parallel", "arbitrary")),
    )(q, kv, kpe)


def _mla_bwd(q, kv, kpe, lse, do, n, tq):
    B, L, _ = q.shape

    def body(q_ref, kv_ref, kpe_ref, lse_ref, do_ref, dq_ref, dkv_ref, dkpe_ref):
        i = pl.program_id(1)

        @pl.when(i == 0)
        def _():
            dkv_ref[...] = jnp.zeros_like(dkv_ref)
            dkpe_ref[...] = jnp.zeros_like(dkpe_ref)

        valid = _mla_valid(i, tq, L, n)
        kp = kpe_ref[:, 0:64]
        for h in range(4):
            qn, qr = q_ref[:, 128 * h:128 * h + 128], q_ref[:, 512 + 64 * h:576 + 64 * h]
            kn, v = kv_ref[:, 256 * h:256 * h + 128], kv_ref[:, 256 * h + 128:256 * h + 256]
            s = jnp.where(valid, (_dot_nt(qn, kn) + _dot_nt(qr, kp)) * MLA_SCALE, NEG_INF)
            p = jnp.exp(s - lse_ref[:, h:h + 1])
            doh = do_ref[:, 128 * h:128 * h + 128].astype(BF16)
            dp = _dot_nt(doh, v)
            delta = jnp.sum(p * dp, axis=1, keepdims=True)
            ds = (p * (dp - delta) * MLA_SCALE).astype(BF16)
            dq_ref[:, 128 * h:128 * h + 128] = _dot(ds, kn)
            dq_ref[:, 512 + 64 * h:576 + 64 * h] = _dot(ds, kp)
            dkv_ref[:, 256 * h:256 * h + 128] += _dot_tn(ds, qn)
            dkv_ref[:, 256 * h + 128:256 * h + 256] += _dot_tn(p.astype(BF16), doh)
            dkpe_ref[:, 0:64] += _dot_tn(ds, qr)

    return _pallas(
        body, name="mla_bwd", grid=(B, L // tq),
        in_specs=[_tile_spec(tq, 768), _full_spec(L, 1024), _full_spec(L, 128), _tile_spec(tq, 4),
                  pl.BlockSpec((None, tq, 512), lambda b, i: (b, i, 0))],
        out_specs=[_tile_spec(tq, 768), _full_spec(L, 1024), _full_spec(L, 128)],
        out_shape=[_sds((B, L, 768), F32), _sds((B, L, 1024), F32), _sds((B, L, 128), F32)],
        compiler_params=_params(("parallel", "arbitrary")),
    )(q, kv, kpe, lse, do)


def _swa_window(i, tq, n):
    W = min(tq + 2 * SWA_WINDOW, n)
    lo = jnp.clip(i * tq - SWA_WINDOW, 0, n - W)
    return pl.multiple_of(lo, 128), W


def _swa_mask(i, lo, tq, W, n):
    qpos = i * tq + lax.broadcasted_iota(jnp.int32, (tq, 1), 0)
    kpos = lo + lax.broadcasted_iota(jnp.int32, (1, W), 1)
    return (jnp.abs(qpos - kpos) <= SWA_WINDOW) & (qpos < n)


def _swa_fwd(q, k, v, sink, n, tq):
    B, L, _ = q.shape

    def body(q_ref, k_ref, v_ref, sink_ref, o_ref, lse_ref):
        i = pl.program_id(1)
        lo, W = _swa_window(i, tq, n)
        mask = _swa_mask(i, lo, tq, W, n)
        kl, vl = k_ref[pl.ds(lo, W), :], v_ref[pl.ds(lo, W), :]
        kc, vc = k_ref[n:L, :], v_ref[n:L, :]
        for h in range(8):
            g = h // 4
            qh = q_ref[:, 64 * h:64 * h + 64]
            s1 = jnp.where(mask, _dot_nt(qh, kl[:, 64 * g:64 * g + 64]) * SWA_SCALE, NEG_INF)
            s2 = _dot_nt(qh, kc[:, 64 * g:64 * g + 64]) * SWA_SCALE
            sk = sink_ref[0:1, h:h + 1]
            m = jnp.maximum(jnp.maximum(jnp.max(s1, axis=1, keepdims=True), jnp.max(s2, axis=1, keepdims=True)), sk)
            e1, e2 = jnp.exp(s1 - m), jnp.exp(s2 - m)
            l = jnp.sum(e1, axis=1, keepdims=True) + jnp.sum(e2, axis=1, keepdims=True) + jnp.exp(sk - m)
            o = _dot((e1 / l).astype(BF16), vl[:, 64 * g:64 * g + 64]) + _dot((e2 / l).astype(BF16), vc[:, 64 * g:64 * g + 64])
            o_ref[:, 64 * h:64 * h + 64] = o.astype(o_ref.dtype)
            lse_ref[:, h:h + 1] = m + jnp.log(l)

    return _pallas(
        body, name="swa_fwd", grid=(B, L // tq),
        in_specs=[_tile_spec(tq, 512), _full_spec(L, 128), _full_spec(L, 128), pl.BlockSpec((1, 8), lambda b, i: (0, 0))],
        out_specs=[_tile_spec(tq, 512), _tile_spec(tq, 8)],
        out_shape=[_sds((B, L, 512), BF16), _sds((B, L, 8), F32)],
        compiler_params=_params(("parallel", "arbitrary")),
    )(q, k, v, sink)


def _swa_bwd(q, k, v, sink, lse, do, n, tq):
    B, L, _ = q.shape

    def body(q_ref, k_ref, v_ref, sink_ref, lse_ref, do_ref, dq_ref, dk_ref, dv_ref, dsink_ref):
        i = pl.program_id(1)

        @pl.when(i == 0)
        def _():
            dk_ref[...] = jnp.zeros_like(dk_ref)
            dv_ref[...] = jnp.zeros_like(dv_ref)
            dsink_ref[...] = jnp.zeros_like(dsink_ref)

        lo, W = _swa_window(i, tq, n)
        mask = _swa_mask(i, lo, tq, W, n)
        kl, vl = k_ref[pl.ds(lo, W), :], v_ref[pl.ds(lo, W), :]
        kc, vc = k_ref[n:L, :], v_ref[n:L, :]
        for h in range(8):
            g = h // 4
            cols = slice(64 * g, 64 * g + 64)
            qh = q_ref[:, 64 * h:64 * h + 64]
            lse_h = lse_ref[:, h:h + 1]
            s1 = jnp.where(mask, _dot_nt(qh, kl[:, cols]) * SWA_SCALE, NEG_INF)
            s2 = _dot_nt(qh, kc[:, cols]) * SWA_SCALE
            p1, p2 = jnp.exp(s1 - lse_h), jnp.exp(s2 - lse_h)
            ps = jnp.exp(sink_ref[0:1, h:h + 1] - lse_h)
            doh = do_ref[:, 64 * h:64 * h + 64].astype(BF16)
            dp1, dp2 = _dot_nt(doh, vl[:, cols]), _dot_nt(doh, vc[:, cols])
            delta = jnp.sum(p1 * dp1, axis=1, keepdims=True) + jnp.sum(p2 * dp2, axis=1, keepdims=True)
            ds1 = (p1 * (dp1 - delta) * SWA_SCALE).astype(BF16)
            ds2 = (p2 * (dp2 - delta) * SWA_SCALE).astype(BF16)
            dq_ref[:, 64 * h:64 * h + 64] = _dot(ds1, kl[:, cols]) + _dot(ds2, kc[:, cols])
            dk_ref[pl.ds(lo, W), cols] += _dot_tn(ds1, qh)
            dk_ref[n:L, cols] += _dot_tn(ds2, qh)
            dv_ref[pl.ds(lo, W), cols] += _dot_tn(p1.astype(BF16), doh)
            dv_ref[n:L, cols] += _dot_tn(p2.astype(BF16), doh)
            dsink_ref[0:1, h:h + 1] += jnp.sum(-ps * delta, axis=0, keepdims=True)

    return _pallas(
        body, name="swa_bwd", grid=(B, L // tq),
        in_specs=[_tile_spec(tq, 512), _full_spec(L, 128), _full_spec(L, 128), pl.BlockSpec((1, 8), lambda b, i: (0, 0)),
                  _tile_spec(tq, 8), pl.BlockSpec((None, tq, 512), lambda b, i: (b, i, 1))],
        out_specs=[_tile_spec(tq, 512), _full_spec(L, 128), _full_spec(L, 128),
                   pl.BlockSpec((None, 1, 8), lambda b, i: (b, 0, 0))],
        out_shape=[_sds((B, L, 512), F32), _sds((B, L, 128), F32), _sds((B, L, 128), F32), _sds((B, 1, 8), F32)],
        compiler_params=_params(("parallel", "arbitrary")),
    )(q, k, v, sink, lse, do)


def _ret_decay(i, tq, L, n):
    qi = i * tq + lax.broadcasted_iota(jnp.int32, (tq, 1), 0)
    ki = lax.broadcasted_iota(jnp.int32, (1, L), 1)
    qf = jnp.where(qi < n, qi, qi - L)
    kf = jnp.where(ki < n, ki, ki - L)
    return (qf - kf).astype(F32), (ki - qi).astype(F32)


def _ret_masks(dist, lg):
    return jnp.where(dist >= 0.0, jnp.exp(lg * jnp.maximum(dist, 0.0)), 0.0)


def _ret_fwd(q, k, v, lg, n, tq):
    B, L, _ = q.shape

    def body(q_ref, k_ref, v_ref, lg_ref, y_ref):
        d_f, d_b = _ret_decay(pl.program_id(1), tq, L, n)
        for h in range(4):
            dec = _ret_masks(d_f, lg_ref[0:1, h:h + 1]) + _ret_masks(d_b, lg_ref[1:2, h:h + 1])
            a = _dot_nt(q_ref[:, 64 * h:64 * h + 64], k_ref[:, 64 * h:64 * h + 64]) * dec
            y_ref[:, 128 * h:128 * h + 128] = _dot(a.astype(BF16), v_ref[:, 128 * h:128 * h + 128])

    return _pallas(
        body, name="ret_fwd", grid=(B, L // tq),
        in_specs=[_tile_spec(tq, 256), _full_spec(L, 256), _full_spec(L, 512), pl.BlockSpec((2, 4), lambda b, i: (0, 0))],
        out_specs=_tile_spec(tq, 512), out_shape=_sds((B, L, 512), F32),
        compiler_params=_params(("parallel", "arbitrary")),
    )(q, k, v, lg)


def _ret_bwd(q, k, v, lg, dy, n, tq):
    B, L, _ = q.shape

    def body(q_ref, k_ref, v_ref, lg_ref, dy_ref, dq_ref, dk_ref, dv_ref, dlg_ref):
        i = pl.program_id(1)

        @pl.when(i == 0)
        def _():
            dk_ref[...] = jnp.zeros_like(dk_ref)
            dv_ref[...] = jnp.zeros_like(dv_ref)
            dlg_ref[...] = jnp.zeros_like(dlg_ref)

        d_f, d_b = _ret_decay(i, tq, L, n)
        for h in range(4):
            qh, kh, vh = q_ref[:, 64 * h:64 * h + 64], k_ref[:, 64 * h:64 * h + 64], v_ref[:, 128 * h:128 * h + 128]
            m_f, m_b = _ret_masks(d_f, lg_ref[0:1, h:h + 1]), _ret_masks(d_b, lg_ref[1:2, h:h + 1])
            dec = m_f + m_b
            s = _dot_nt(qh, kh)
            dyh = dy_ref[:, 128 * h:128 * h + 128].astype(BF16)
            gr = _dot_nt(dyh, vh)
            ds = (gr * dec).astype(BF16)
            dq_ref[:, 64 * h:64 * h + 64] = _dot(ds, kh)
            dk_ref[:, 64 * h:64 * h + 64] += _dot_tn(ds, qh)
            dv_ref[:, 128 * h:128 * h + 128] += _dot_tn((s * dec).astype(BF16), dyh)
            gs = gr * s
            tot_f = jnp.sum(jnp.sum(gs * m_f * jnp.maximum(d_f, 0.0), axis=1, keepdims=True), axis=0, keepdims=True)
            tot_b = jnp.sum(jnp.sum(gs * m_b * jnp.maximum(d_b, 0.0), axis=1, keepdims=True), axis=0, keepdims=True)
            dlg_ref[0:1, h:h + 1] += tot_f
            dlg_ref[1:2, h:h + 1] += tot_b

    return _pallas(
        body, name="ret_bwd", grid=(B, L // tq),
        in_specs=[_tile_spec(tq, 256), _full_spec(L, 256), _full_spec(L, 512), pl.BlockSpec((2, 4), lambda b, i: (0, 0)),
                  _tile_spec(tq, 512)],
        out_specs=[_tile_spec(tq, 256), _full_spec(L, 256), _full_spec(L, 512),
                   pl.BlockSpec((None, 2, 4), lambda b, i: (b, 0, 0))],
        out_shape=[_sds((B, L, 256), F32), _sds((B, L, 256), F32), _sds((B, L, 512), F32), _sds((B, 2, 4), F32)],
        compiler_params=_params(("parallel", "arbitrary")),
    )(q, k, v, lg, dy)


def _loss_head(s, target, g, n, tm):
    B, L, D = s.shape
    nx = n // tm

    def body(s_ref, t_ref, g_ref, ds_ref, dg_ref, loss_ref):
        b, i = pl.program_id(0), pl.program_id(1)

        @pl.when((b == 0) & (i == 0))
        def _():
            dg_ref[...] = jnp.zeros_like(dg_ref)
            loss_ref[...] = jnp.zeros_like(loss_ref)

        @pl.when(i < nx)
        def _():
            y, vjp = jax.vjp(_rms, s_ref[...], g_ref[...])
            err = y - t_ref[...]
            d_s, d_g = vjp(err * (1.0 / D))
            ds_ref[...] = d_s
            dg_ref[...] += d_g
            part = jnp.sum(jnp.sum(err * err, axis=1, keepdims=True), axis=0, keepdims=True) * (0.5 / D)
            loss_ref[...] += jnp.broadcast_to(part, loss_ref.shape)

        @pl.when(i >= nx)
        def _():
            ds_ref[...] = jnp.zeros_like(ds_ref)

    return _pallas(
        body, name="loss_head", grid=(B, L // tm),
        in_specs=[pl.BlockSpec((None, tm, D), lambda b, i: (b, i, 0)),
                  pl.BlockSpec((None, tm, D), lambda b, i: (b, jnp.minimum(i, nx - 1), 0)),
                  pl.BlockSpec((1, D), lambda b, i: (0, 0))],
        out_specs=[pl.BlockSpec((None, tm, D), lambda b, i: (b, i, 0)), pl.BlockSpec((1, D), lambda b, i: (0, 0)),
                   pl.BlockSpec((1, 128), lambda b, i: (0, 0))],
        out_shape=[_sds((B, L, D), F32), _sds((1, D), F32), _sds((1, 128), F32)],
        compiler_params=_params(("arbitrary", "arbitrary")),
    )(s, target, g)


def _ada_fwd(c_all, ada_w, ada_b):
    NL, D, Ns = ada_w.shape
    R = c_all.shape[0]

    def body(c_ref, w_ref, b_ref, o_ref):
        cond = jax.nn.silu(c_ref[...]).astype(BF16)
        o_ref[...] = _dot(cond, w_ref[...].astype(BF16)) + b_ref[...]

    return _pallas(
        body, name="ada_fwd", grid=(NL,),
        in_specs=[pl.BlockSpec((R, D), lambda l: (0, 0)), pl.BlockSpec((None, D, Ns), lambda l: (l, 0, 0)),
                  pl.BlockSpec((None, 1, Ns), lambda l: (l, 0, 0))],
        out_specs=pl.BlockSpec((None, R, Ns), lambda l: (l, 0, 0)), out_shape=_sds((NL, R, Ns), F32),
        compiler_params=_params(("parallel",)),
    )(c_all, ada_w, ada_b)


def _ada_bwd(c_all, ada_w, dmods):
    NL, D, Ns = ada_w.shape
    R = c_all.shape[0]

    def body(c_ref, w_ref, dm_ref, dw_ref, dc_ref):
        cond = jax.nn.silu(c_ref[...]).astype(BF16)
        dm = dm_ref[...].astype(BF16)
        dw_ref[...] = _dot_tn(cond, dm)
        dc_ref[...] = _dot_nt(dm, w_ref[...].astype(BF16))

    return _pallas(
        body, name="ada_bwd", grid=(NL,),
        in_specs=[pl.BlockSpec((R, D), lambda l: (0, 0)), pl.BlockSpec((None, D, Ns), lambda l: (l, 0, 0)),
                  pl.BlockSpec((None, R, Ns), lambda l: (l, 0, 0))],
        out_specs=[pl.BlockSpec((None, D, Ns), lambda l: (l, 0, 0)), pl.BlockSpec((None, R, D), lambda l: (l, 0, 0))],
        out_shape=[_sds((NL, D, Ns), F32), _sds((NL, R, D), F32)],
        compiler_params=_params(("parallel",)),
    )(c_all, ada_w, dmods)


def _my_index():
    return 4 * lax.axis_index("x") + 2 * lax.axis_index("y") + lax.axis_index("c")


def _peer(k):
    x, y, c = lax.axis_index("x"), lax.axis_index("y"), lax.axis_index("c")
    kx, ky, kc = (k >> 2) & 1, (k >> 1) & 1, k & 1
    px, py, pc = (x + kx) % 2, (y + ky) % 2, (c + kc) % 2
    return (px, py, pc), 4 * px + 2 * py + pc


def _all_gather(name, shards):
    na = len(shards)
    hbm = pl.BlockSpec(memory_space=pl.ANY)

    def body(*refs):
        in_refs, out_refs = refs[:na], refs[na:2 * na]
        send_sems, recv_sems, local_sems = refs[2 * na:]
        me = _my_index()
        sib_id, sib = _peer(1)
        chips = [_peer(k) for k in (4, 2, 6)]
        sib_chips = [4 * px + 2 * py + (1 - pc) for (px, py, pc), _ in chips]

        def copy(a, k, slot, to, src=None):
            dst = out_refs[a].at[slot]
            return pltpu.make_async_remote_copy(
                src_ref=dst if src is None else src, dst_ref=dst, send_sem=send_sems.at[a, k],
                recv_sem=recv_sems.at[a, k], device_id=to, device_id_type=MESH)

        first, passed, mine = [], [], []
        for a in range(na):
            cp = pltpu.make_async_copy(in_refs[a], out_refs[a].at[me], local_sems.at[a])
            cp.start()
            mine.append(cp)
            first.append(copy(a, 0, me, sib_id, src=in_refs[a]))
            first += [copy(a, 1 + j, me, pid, src=in_refs[a]) for j, (pid, _) in enumerate(chips)]
        for cp in first:
            cp.start()
        for a in range(na):
            for j, (pid, pidx) in enumerate(chips):
                copy(a, 1 + j, pidx, pid).wait_recv()
                fwd = copy(a, 4 + j, pidx, sib_id)
                fwd.start()
                passed.append(fwd)
        for a in range(na):
            copy(a, 0, sib, sib_id).wait_recv()
            for j in range(3):
                copy(a, 4 + j, sib_chips[j], sib_id).wait_recv()
        for cp in first + passed:
            cp.wait_send()
        for cp in mine:
            cp.wait()

    return _pallas(
        body, name=name, in_specs=[hbm] * na, out_specs=[hbm] * na,
        out_shape=[_sds((N_DEV,) + s.shape, s.dtype) for s in shards],
        scratch_shapes=[pltpu.SemaphoreType.DMA((na, 7)), pltpu.SemaphoreType.DMA((na, 7)),
                        pltpu.SemaphoreType.DMA((na,))],
    )(*shards)


def _all_to_all(name, parts):
    na = len(parts)
    hbm = pl.BlockSpec(memory_space=pl.ANY)

    def body(*refs):
        in_refs, out_refs = refs[:na], refs[na:2 * na]
        send_sems, recv_sems, local_sems = refs[2 * na:]
        me = _my_index()
        copies = []
        for a in range(na):
            cp = pltpu.make_async_copy(in_refs[a].at[me], out_refs[a].at[me], local_sems.at[a])
            cp.start()
            copies.append(cp)
            for k in range(1, N_DEV):
                pid, pidx = _peer(k)
                cp = pltpu.make_async_remote_copy(
                    src_ref=in_refs[a].at[pidx], dst_ref=out_refs[a].at[me], send_sem=send_sems.at[a, k - 1],
                    recv_sem=recv_sems.at[a, k - 1], device_id=pid, device_id_type=MESH)
                cp.start()
                copies.append(cp)
        for cp in copies:
            cp.wait()

    return _pallas(
        body, name=name, in_specs=[hbm] * na, out_specs=[hbm] * na,
        out_shape=[_sds(p.shape, p.dtype) for p in parts],
        scratch_shapes=[pltpu.SemaphoreType.DMA((na, 7)), pltpu.SemaphoreType.DMA((na, 7)),
                        pltpu.SemaphoreType.DMA((na,))],
    )(*parts)


def _rope_tables(n, L, width):
    t = jnp.arange(n)
    row = (t // GRID_W).astype(F32)
    col = (t % GRID_W).astype(F32)
    n_freq = ROPE_DIM // 4
    freqs = ROPE_THETA ** (-jnp.arange(n_freq, dtype=F32) / n_freq)
    ang = jnp.concatenate([row[:, None] * freqs, col[:, None] * freqs], axis=-1)
    cos, sin = jnp.cos(ang), jnp.sin(ang)
    cos2 = jnp.concatenate([cos, cos], axis=-1)
    sin2 = jnp.concatenate([-sin, sin], axis=-1)
    cos2 = jnp.concatenate([cos2, jnp.ones((L - n, ROPE_DIM), F32)], axis=0)
    sin2 = jnp.concatenate([sin2, jnp.zeros((L - n, ROPE_DIM), F32)], axis=0)
    reps = width // ROPE_DIM
    return jnp.tile(cos2, (1, reps)), jnp.tile(sin2, (1, reps))


def _ab_perm(w):
    return jnp.concatenate([w[:, 0:256], w[:, 320:1600], w[:, 256:320], jnp.zeros((w.shape[0], 64), w.dtype)], axis=1)


def _ab_unperm(g):
    return jnp.concatenate([g[:, 0:256], g[:, 1536:1600], g[:, 256:1536]], axis=1)


def _wq_perm(w):
    return jnp.concatenate([w[:, 192 * h:192 * h + 128] for h in range(4)]
                           + [w[:, 192 * h + 128:192 * h + 192] for h in range(4)], axis=1)


def _wq_unperm(g):
    return jnp.concatenate([g[:, sl] for h in range(4)
                            for sl in (slice(128 * h, 128 * h + 128), slice(512 + 64 * h, 576 + 64 * h))], axis=1)


def _flat(a):
    return a.reshape(-1, a.shape[-1])


def _layer_weights(full, p):
    NL, NE, NO = full["ffn_in"].shape[0], full["ab_in"].shape[0], full["cd_in"].shape[0]
    groups = range(4)
    return dict(
        norm_mix=[p["norm_mix"][l][None] for l in range(NL)], norm_ffn=[p["norm_ffn"][l][None] for l in range(NL)],
        norm_final=p["norm_final"][None],
        ffn_in=[full["ffn_in"][l] for l in range(NL)], ffn_out=[full["ffn_out"][l] for l in range(NL)],
        ab_in=[_ab_perm(full["ab_in"][j]) for j in range(NE)],
        ab_out=[full["ab_out"][j] for j in range(NE)],
        wq=[_wq_perm(full["mla_wq_b"][j]) for j in range(NE)], wkv=[full["mla_wkv_b"][j] for j in range(NE)],
        kv_norm=[p["mla_kv_norm"][j][None] for j in range(NE)], q_norm=[p["mla_q_norm"][j][None] for j in range(NE)],
        v_norm=[[p["cmlp_v_norm"][j][None, 128 * g:128 * g + 128] for g in groups] for j in range(NE)],
        ws=[[p["cmlp_ws"][j, g] for g in groups] for j in range(NE)],
        bs=[[p["cmlp_bs"][j, g][:, None] for g in groups] for j in range(NE)],
        cd_in=[full["cd_in"][j] for j in range(NO)], cd_out=[full["cd_out"][j] for j in range(NO)],
        lg=[jnp.stack([jax.nn.log_sigmoid(p["ret_decay_fwd"][j]), jax.nn.log_sigmoid(p["ret_decay_bwd"][j])])
            for j in range(NO)],
        sink=[p["swa_sink"][j][None] for j in range(NO)],
        ret_norm=[[p["ret_norm"][j][None, 128 * g:128 * g + 128] for g in groups] for j in range(NO)],
    )


def _local_step(x, ctx, target, mods, W):
    B, n, D = x.shape
    m = ctx.shape[1]
    L = n + m
    NL = mods.shape[0]
    tm = min(256, m)
    tq = min(256, m)
    cos512, sin512 = _rope_tables(n, L, 512)
    s = jnp.concatenate([x, ctx], axis=1)
    saved = []

    def rows(name, f, arrays, pieces, samp, samp_pieces, glob, out_arrays, out_pieces, tile=tm):
        return _Rows(name, f, B, L, n, tile, arrays, pieces, samp, samp_pieces, glob, out_arrays, out_pieces)

    def full(width, start=0):
        return [(0, start, width)]

    for l in range(NL):
        j = l // 2
        even = l % 2 == 0
        md = mods[l]
        r = {"s0": s}
        pre1 = rows(f"pre_mix{l}", _f_pre, [s], full(D), [md], [(0, 0), (0, 1)], [W["norm_mix"][l]], [(D, BF16)], full(D))
        (xn,) = pre1.fwd()
        r["pre1"], r["xn"] = pre1, xn
        if even:
            z = _mm(f"ab_in{l}", _flat(xn), W["ab_in"][j]).reshape(B, L, 1664)
            pieces = [(0, 0, 256), (0, 256, 256), (0, 1536, 128)]
            pieces += [(0, 512 + 128 * g, 128) for g in range(4)] + [(0, 1024 + 128 * g, 128) for g in range(4)]
            pieces += [(1, 0, 128), (2, 0, 128)]
            glob = [W["kv_norm"][j], W["q_norm"][j]] + W["v_norm"][j] + W["ws"][j] + W["bs"][j]
            abp = rows(f"ab_pre{l}", _f_ab_pre, [z, cos512, sin512], pieces, [], [], glob,
                       [(256, BF16), (256, BF16), (128, BF16), (512, BF16)],
                       [(0, 0, 256), (1, 0, 256), (2, 0, 128)] + [(3, 128 * g, 128) for g in range(4)], tile=128)
            kvn, qn, kpe, cm = abp.fwd()
            kv = _mm(f"wkv{l}", _flat(kvn), W["wkv"][j], out_dtype=BF16).reshape(B, L, 1024)
            q0 = _mm(f"wq{l}", _flat(qn), W["wq"][j]).reshape(B, L, 768)
            qrp = rows(f"q_rope{l}", _f_q_rope, [q0, cos512, sin512], [(0, 0, 512), (0, 512, 256), (1, 0, 256), (2, 0, 256)],
                       [], [], [], [(768, BF16)], [(0, 0, 512), (0, 512, 256)])
            (q,) = qrp.fwd()
            o, lse = _mla_fwd(q, kv, kpe, n, tq)
            y = _mm(f"ab_out_a{l}", _flat(o), W["ab_out"][j][:512])
            y = _mm(f"ab_out_b{l}", _flat(cm), W["ab_out"][j][512:], add=y).reshape(B, L, D)
            r.update(z=z, abp=abp, kvn=kvn, qn=qn, kpe=kpe, cm=cm, kv=kv, qrp=qrp, q=q, o=o, lse=lse)
        else:
            z = _mm(f"cd_in{l}", _flat(xn), W["cd_in"][j]).reshape(B, L, 2304)
            pieces = [(0, 0, 256), (0, 256, 512), (0, 768, 128), (0, 896, 128), (0, 1024, 256), (0, 1280, 512), (0, 1792, 512)]
            pieces += [(1, 0, 256), (2, 0, 256), (1, 0, 128), (2, 0, 128), (1, 0, 512), (2, 0, 512)]
            cdp = rows(f"cd_pre{l}", _f_cd_pre, [z, cos512, sin512], pieces, [], [], [],
                       [(256, BF16), (256, BF16), (512, BF16), (512, BF16), (128, BF16), (128, BF16), (512, F32)],
                       [(k_, 0, w_) for k_, w_ in enumerate((256, 256, 512, 512, 128, 128, 512))])
            rq, rk, rv, sq, sk, sv, rg = cdp.fwd()
            yret = _ret_fwd(rq, rk, rv, W["lg"][j], n, tq)
            osw, lse = _swa_fwd(sq, sk, sv, W["sink"][j], n, tq)
            mrg = rows(f"cd_merge{l}", _f_cd_merge, [yret, rg],
                       [(0, 128 * g, 128) for g in range(4)] + [(1, 128 * g, 128) for g in range(4)], [], [],
                       W["ret_norm"][j], [(512, BF16)], [(0, 128 * g, 128) for g in range(4)])
            (yr,) = mrg.fwd()
            y = _mm(f"cd_out_a{l}", _flat(yr), W["cd_out"][j][:512])
            y = _mm(f"cd_out_b{l}", _flat(osw), W["cd_out"][j][512:], add=y).reshape(B, L, D)
            r.update(z=z, cdp=cdp, rq=rq, rk=rk, rv=rv, sq=sq, sk=sk, sv=sv, rg=rg, yret=yret, osw=osw, lse=lse,
                     mrg=mrg, yr=yr)
        res1 = rows(f"res_mix{l}", _f_res, [s, y], [(0, 0, D), (1, 0, D)], [md], [(0, 2)], [], [(D, F32)], full(D))
        (s1,) = res1.fwd()
        pre2 = rows(f"pre_ffn{l}", _f_pre, [s1], full(D), [md], [(0, 3), (0, 4)], [W["norm_ffn"][l]], [(D, BF16)], full(D))
        (xn2,) = pre2.fwd()
        Fh = W["ffn_out"][l].shape[0]
        z2 = _mm(f"ffn_in{l}", _flat(xn2), W["ffn_in"][l]).reshape(B, L, 2 * Fh)
        swi = rows(f"swiglu{l}", _f_swiglu, [z2], [(0, 0, Fh), (0, Fh, Fh)], [], [], [], [(Fh, BF16)], full(Fh))
        (hid,) = swi.fwd()
        y2 = _mm(f"ffn_out{l}", _flat(hid), W["ffn_out"][l]).reshape(B, L, D)
        res2 = rows(f"res_ffn{l}", _f_res, [s1, y2], [(0, 0, D), (1, 0, D)], [md], [(0, 5)], [], [(D, F32)], full(D))
        (s,) = res2.fwd()
        r.update(res1=res1, pre2=pre2, xn2=xn2, swi=swi, hid=hid, res2=res2)
        saved.append(r)

    ds, d_norm_final, loss = _loss_head(s, target, W["norm_final"], n, tm)

    G = {k: [None] * len(v) for k, v in W.items() if isinstance(v, list)}
    G["norm_final"] = d_norm_final
    dmods = [None] * NL
    for l in reversed(range(NL)):
        j = l // 2
        even = l % 2 == 0
        r = saved[l]
        dm = [None] * 6
        (ds1, dy2), (dm[5],), _ = r["res2"].bwd([ds], {0: F32, 1: BF16})
        dy2f = _flat(dy2)
        dhid = _mm(f"d_hid{l}", dy2f, W["ffn_out"][l], tb=True).reshape(B, L, -1)
        G["ffn_out"][l] = _mm(f"g_ffn_out{l}", _flat(r["hid"]), dy2f, ta=True, out_dtype=BF16)
        (dz2,), _, _ = r["swi"].bwd([dhid], {0: BF16})
        dz2f = _flat(dz2)
        G["ffn_in"][l] = _mm(f"g_ffn_in{l}", _flat(r["xn2"]), dz2f, ta=True, out_dtype=BF16)
        dxn2 = _mm(f"d_xn2{l}", dz2f, W["ffn_in"][l], tb=True).reshape(B, L, D)
        (ds1,), (dm[3], dm[4]), (G["norm_ffn"][l],) = r["pre2"].bwd([dxn2], {0: F32}, grad_glob=(0,), add={0: ds1})
        (ds0, dy), (dm[2],), _ = r["res1"].bwd([ds1], {0: F32, 1: BF16})
        dyf = _flat(dy)
        if even:
            w_out = W["ab_out"][j]
            dcat = _mm(f"d_cat{l}", dyf, w_out, tb=True).reshape(B, L, -1)
            G["ab_out"][l // 2] = jnp.concatenate(
                [_mm(f"g_ab_out_a{l}", _flat(r["o"]), dyf, ta=True, out_dtype=BF16),
                 _mm(f"g_ab_out_b{l}", _flat(r["cm"]), dyf, ta=True, out_dtype=BF16)], axis=0)
            dq, dkv, dkpe = _mla_bwd(r["q"], r["kv"], r["kpe"], r["lse"], dcat, n, tq)
            (dq0,), _, _ = r["qrp"].bwd([dq], {0: BF16})
            dq0f, dkvf = _flat(dq0), _flat(dkv)
            G["wq"][j] = _mm(f"g_wq{l}", _flat(r["qn"]), dq0f, ta=True, out_dtype=BF16)
            G["wkv"][j] = _mm(f"g_wkv{l}", _flat(r["kvn"]), dkvf, ta=True, out_dtype=BF16)
            dqn = _mm(f"d_qn{l}", dq0f, W["wq"][j], tb=True).reshape(B, L, 256)
            dkvn = _mm(f"d_kvn{l}", dkvf, W["wkv"][j], tb=True).reshape(B, L, 256)
            dcm = dcat[:, :, 512:]
            (dz,), _, gg = r["abp"].bwd([dkvn, dqn, dkpe, dcm], {0: BF16}, grad_glob=tuple(range(14)))
            G["kv_norm"][j], G["q_norm"][j] = gg[0], gg[1]
            G["v_norm"][j], G["ws"][j], G["bs"][j] = list(gg[2:6]), list(gg[6:10]), list(gg[10:14])
            w_in, key = W["ab_in"][j], "ab_in"
        else:
            w_out = W["cd_out"][j]
            dcat = _mm(f"d_cat{l}", dyf, w_out, tb=True).reshape(B, L, -1)
            G["cd_out"][j] = jnp.concatenate(
                [_mm(f"g_cd_out_a{l}", _flat(r["yr"]), dyf, ta=True, out_dtype=BF16),
                 _mm(f"g_cd_out_b{l}", _flat(r["osw"]), dyf, ta=True, out_dtype=BF16)], axis=0)
            dyr = dcat[:, :, :512]
            (dyret, drg), _, gg = r["mrg"].bwd([dyr], {0: F32, 1: F32}, grad_glob=(0, 1, 2, 3))
            G["ret_norm"][j] = list(gg)
            drq, drk, drv, dlg = _ret_bwd(r["rq"], r["rk"], r["rv"], W["lg"][j], dyret, n, tq)
            dsq, dsk, dsv, dsink = _swa_bwd(r["sq"], r["sk"], r["sv"], W["sink"][j], r["lse"], dcat, n, tq)
            G["lg"][j], G["sink"][j] = dlg, dsink
            (dz,), _, _ = r["cdp"].bwd([drq, drk, drv, dsq, dsk, dsv, drg], {0: BF16})
            w_in, key = W["cd_in"][j], "cd_in"
        dzf = _flat(dz)
        G[key][j] = _mm(f"g_{key}{l}", _flat(r["xn"]), dzf, ta=True, out_dtype=BF16)
        dxn = _mm(f"d_xn{l}", dzf, w_in, tb=True).reshape(B, L, D)
        (ds,), (dm[0], dm[1]), (G["norm_mix"][l],) = r["pre1"].bwd([dxn], {0: F32}, grad_glob=(0,), add={0: ds0})
        dmods[l] = jnp.concatenate(dm, axis=2)
    return loss, ds[:, :n], jnp.stack(dmods), G


def kernel(x, c, ctx, c_ctx, ada_w, ada_b, norm_mix, norm_ffn, norm_final, ffn_in, ffn_out, ab_in, ab_out, mla_q_norm, mla_kv_norm, mla_wq_b, mla_wkv_b, cmlp_v_norm, cmlp_ws, cmlp_bs, cd_in, cd_out, ret_decay_fwd, ret_decay_bwd, ret_norm, swa_sink, loss_target, m_c_ctx, m_ada_w, m_ada_b, m_norm_mix, m_norm_ffn, m_norm_final, m_ffn_in, m_ffn_out, m_ab_in, m_ab_out, m_mla_q_norm, m_mla_kv_norm, m_mla_wq_b, m_mla_wkv_b, m_cmlp_v_norm, m_cmlp_ws, m_cmlp_bs, m_cd_in, m_cd_out, m_ret_decay_fwd, m_ret_decay_bwd, m_ret_norm, m_swa_sink, v_c_ctx, v_ada_w, v_ada_b, v_norm_mix, v_norm_ffn, v_norm_final, v_ffn_in, v_ffn_out, v_ab_in, v_ab_out, v_mla_q_norm, v_mla_kv_norm, v_mla_wq_b, v_mla_wkv_b, v_cmlp_v_norm, v_cmlp_ws, v_cmlp_bs, v_cd_in, v_cd_out, v_ret_decay_fwd, v_ret_decay_bwd, v_ret_norm, v_swa_sink):
    B, n, D = x.shape
    NL = ada_w.shape[0]
    NE, NO = ab_in.shape[0], cd_in.shape[0]
    me = _my_index()
    weights = dict(c_ctx=c_ctx, ada_w=ada_w, ada_b=ada_b, norm_mix=norm_mix, norm_ffn=norm_ffn, norm_final=norm_final,
                   ffn_in=ffn_in, ffn_out=ffn_out, ab_in=ab_in, ab_out=ab_out, mla_q_norm=mla_q_norm,
                   mla_kv_norm=mla_kv_norm, mla_wq_b=mla_wq_b, mla_wkv_b=mla_wkv_b, cmlp_v_norm=cmlp_v_norm,
                   cmlp_ws=cmlp_ws, cmlp_bs=cmlp_bs, cd_in=cd_in, cd_out=cd_out, ret_decay_fwd=ret_decay_fwd,
                   ret_decay_bwd=ret_decay_bwd, ret_norm=ret_norm, swa_sink=swa_sink)
    moments_m = dict(c_ctx=m_c_ctx, ada_w=m_ada_w, ada_b=m_ada_b, norm_mix=m_norm_mix, norm_ffn=m_norm_ffn,
                     norm_final=m_norm_final, ffn_in=m_ffn_in, ffn_out=m_ffn_out, ab_in=m_ab_in, ab_out=m_ab_out,
                     mla_q_norm=m_mla_q_norm, mla_kv_norm=m_mla_kv_norm, mla_wq_b=m_mla_wq_b, mla_wkv_b=m_mla_wkv_b,
                     cmlp_v_norm=m_cmlp_v_norm, cmlp_ws=m_cmlp_ws, cmlp_bs=m_cmlp_bs, cd_in=m_cd_in, cd_out=m_cd_out,
                     ret_decay_fwd=m_ret_decay_fwd, ret_decay_bwd=m_ret_decay_bwd, ret_norm=m_ret_norm,
                     swa_sink=m_swa_sink)
    moments_v = dict(c_ctx=v_c_ctx, ada_w=v_ada_w, ada_b=v_ada_b, norm_mix=v_norm_mix, norm_ffn=v_norm_ffn,
                     norm_final=v_norm_final, ffn_in=v_ffn_in, ffn_out=v_ffn_out, ab_in=v_ab_in, ab_out=v_ab_out,
                     mla_q_norm=v_mla_q_norm, mla_kv_norm=v_mla_kv_norm, mla_wq_b=v_mla_wq_b, mla_wkv_b=v_mla_wkv_b,
                     cmlp_v_norm=v_cmlp_v_norm, cmlp_ws=v_cmlp_ws, cmlp_bs=v_cmlp_bs, cd_in=v_cd_in, cd_out=v_cd_out,
                     ret_decay_fwd=v_ret_decay_fwd, ret_decay_bwd=v_ret_decay_bwd, ret_norm=v_ret_norm,
                     swa_sink=v_swa_sink)
    order = list(weights)

    Ns = ada_w.shape[2]
    (c_g,) = _all_gather("gather_c", [c])
    R = N_DEV * B + 8
    c_all = jnp.concatenate([c_g.reshape(N_DEV * B, D), jnp.broadcast_to(c_ctx[None], (8, D))], axis=0)
    ada_b_mine = lax.dynamic_slice_in_dim(ada_b, me * Ns, Ns, axis=1)[:, None, :]
    mods_shard = _ada_fwd(c_all, ada_w, ada_b_mine)
    (mods_g,) = _all_gather("gather_mods", [mods_shard])
    mods_full = jnp.transpose(mods_g, (1, 2, 0, 3)).reshape(NL, R, 6, D)
    mx = lax.dynamic_slice_in_dim(mods_full, me * B, B, axis=1)
    mh = jnp.broadcast_to(mods_full[:, N_DEV * B][:, None], (NL, B, 6, D))
    mods = jnp.stack([mx, mh], axis=2)

    big = ["ffn_in", "ffn_out", "ab_in", "ab_out", "cd_in", "cd_out", "mla_wq_b", "mla_wkv_b"]
    col_sharded = {"ffn_in", "ab_in", "cd_in", "mla_wq_b", "mla_wkv_b"}
    gathered = _all_gather("gather_weights", [_to_bf16("cast_" + k, weights[k]) for k in big])
    full = {}
    for k, g in zip(big, gathered):
        if k in col_sharded:
            full[k] = jnp.transpose(g, (1, 2, 0, 3)).reshape(g.shape[1], g.shape[2], -1)
        else:
            full[k] = jnp.transpose(g, (1, 0, 2, 3)).reshape(g.shape[1], -1, g.shape[3])
    (rn_g,) = _all_gather("gather_ret_norm", [ret_norm])
    rn_full = jnp.transpose(rn_g, (1, 0, 2)).reshape(NO, -1)
    W = _layer_weights(full, dict(weights, ret_norm=rn_full))

    loss_part, grad_x, dmods, G = _local_step(x, ctx, loss_target, mods, W)
    loss = lax.psum(loss_part[0, 0], ("x", "y", "c"))

    dmx = dmods[:, :, 0].reshape(NL, B, 6 * D)
    dmh = jnp.sum(dmods[:, :, 1], axis=1).reshape(NL, 1, 6 * D)
    (dm_g,) = _all_gather("gather_dmods", [jnp.concatenate([dmx, dmh], axis=1)])
    dmx_all = jnp.transpose(dm_g[:, :, :B], (1, 0, 2, 3)).reshape(NL, N_DEV * B, 6 * D)
    dmh_all = jnp.sum(dm_g[:, :, B], axis=0)
    dm_rows = jnp.concatenate([dmx_all, dmh_all[:, None], jnp.zeros((NL, 7, 6 * D), F32)], axis=1)
    g_ada_b = jnp.sum(dm_rows, axis=1)
    dm_mine = lax.dynamic_slice_in_dim(dm_rows, me * Ns, Ns, axis=2)
    g_ada_w, dcond = _ada_bwd(c_all, ada_w, dm_mine)
    sg = jax.nn.sigmoid(c_ctx)
    d_c_ctx_part = jnp.sum(dcond[:, N_DEV * B], axis=0) * (sg * (1.0 + c_ctx * (1.0 - sg)))

    def cat(parts):
        return jnp.concatenate([p.reshape(-1) for p in parts])

    dlg = jnp.stack([jnp.sum(G["lg"][j], axis=0) for j in range(NO)])
    sig_f, sig_b = jax.nn.sigmoid(-ret_decay_fwd), jax.nn.sigmoid(-ret_decay_bwd)
    small = dict(
        c_ctx=d_c_ctx_part,
        norm_mix=cat(G["norm_mix"]), norm_ffn=cat(G["norm_ffn"]), norm_final=G["norm_final"].reshape(-1),
        mla_q_norm=cat(G["q_norm"]), mla_kv_norm=cat(G["kv_norm"]),
        cmlp_v_norm=cat([cat(G["v_norm"][j]) for j in range(NE)]),
        cmlp_ws=cat([jnp.stack(G["ws"][j]) for j in range(NE)]),
        cmlp_bs=cat([jnp.stack([b_[:, 0] for b_ in G["bs"][j]]) for j in range(NE)]),
        ret_decay_fwd=(dlg[:, 0] * sig_f).reshape(-1), ret_decay_bwd=(dlg[:, 1] * sig_b).reshape(-1),
        ret_norm=cat([cat(G["ret_norm"][j]) for j in range(NO)]),
        swa_sink=cat([jnp.sum(G["sink"][j], axis=0) for j in range(NO)]),
    )
    small_keys = list(small)
    sizes = [small[k].shape[0] for k in small_keys]
    total = sum(sizes)
    padded = -(-total // 2048) * 2048
    packed = jnp.concatenate([small[k] for k in small_keys] + [jnp.zeros((padded - total,), F32)]).reshape(-1, 128)
    (small_g,) = _all_gather("gather_small", [packed])

    def to_slots(k, gl):
        g = jnp.stack(gl)
        if k in col_sharded:
            return jnp.transpose(g.reshape(g.shape[0], g.shape[1], N_DEV, -1), (2, 0, 1, 3))
        return jnp.transpose(g.reshape(g.shape[0], N_DEV, -1, g.shape[2]), (1, 0, 2, 3))

    gwq = [_wq_unperm(g) for g in G["wq"]]
    gab = [_ab_unperm(g) for g in G["ab_in"]]
    parts = dict(ffn_in=G["ffn_in"], ffn_out=G["ffn_out"], ab_in=gab, ab_out=G["ab_out"], cd_in=G["cd_in"],
                 cd_out=G["cd_out"], mla_wq_b=gwq, mla_wkv_b=G["wkv"])
    landed = _all_to_all("scatter_grads", [to_slots(k, parts[k]) for k in big])

    grads, deltas, new_m, new_v = {}, {}, {}, {}
    for k, land in zip(big, landed):
        grads[k], deltas[k], new_m[k], new_v[k] = _adamw_from_slots("adamw_" + k, weights[k], moments_m[k], moments_v[k], land)
    deltas["ada_w"], new_m["ada_w"], new_v["ada_w"] = [
        o.reshape(ada_w.shape) for o in _adamw("adamw_ada_w", _flat(ada_w), _flat(g_ada_w), _flat(m_ada_w), _flat(v_ada_w))]
    grads["ada_w"] = g_ada_w

    def packed_of(src, fill):
        vals = [src[k].reshape(-1) if k != "ret_norm" else jnp.full((sizes[i],), fill, F32)
                for i, k in enumerate(small_keys)]
        return jnp.concatenate(vals + [jnp.full((padded - total,), fill, F32)]).reshape(-1, 128)

    w_p, m_p, v_p = packed_of(weights, 0.0), packed_of(moments_m, 0.0), packed_of(moments_v, 1.0)

    def f_small(w_, m_, v_, land_):
        g = _sum_slots(land_)
        return (g,) + _adamw_math(w_, g, m_, v_)

    g_p, d_p, nm_p, nv_p = _ew("adamw_small", f_small, [w_p, m_p, v_p, small_g], [F32] * 4)
    offs = np.cumsum([0] + sizes)
    for i, k in enumerate(small_keys):
        sl = slice(int(offs[i]), int(offs[i + 1]))
        if k == "ret_norm":
            g_full = g_p.reshape(-1)[sl].reshape(NO, -1)
            g_mine = lax.dynamic_slice_in_dim(g_full, me * ret_norm.shape[1], ret_norm.shape[1], axis=1)
            d_, m_, v_ = _adamw("adamw_ret_norm", *[jnp.pad(a, ((0, 8 - NO), (0, 128 - a.shape[1])), constant_values=cv)
                                                     for a, cv in ((ret_norm, 0.0), (g_mine, 0.0), (m_ret_norm, 0.0), (v_ret_norm, 1.0))])
            grads[k] = g_mine
            deltas[k], new_m[k], new_v[k] = [a[:NO, :ret_norm.shape[1]] for a in (d_, m_, v_)]
        else:
            shp = weights[k].shape
            grads[k], deltas[k], new_m[k], new_v[k] = [a.reshape(-1)[sl].reshape(shp) for a in (g_p, d_p, nm_p, nv_p)]
    pad_b = lambda a, cv=0.0: jnp.pad(a, ((0, 8 - NL), (0, 0)), constant_values=cv)
    d_, m_, v_ = _adamw("adamw_ada_b", pad_b(ada_b), pad_b(g_ada_b), pad_b(m_ada_b), pad_b(v_ada_b, 1.0))
    grads["ada_b"] = g_ada_b
    deltas["ada_b"], new_m["ada_b"], new_v["ada_b"] = d_[:NL], m_[:NL], v_[:NL]

    return (loss, grad_x, *[grads[k] for k in order], *[deltas[k] for k in order],
            *[new_m[k] for k in order], *[new_v[k] for k in order])
```

```python
import functools

import numpy as np
import jax
import jax.numpy as jnp
from jax import lax
from jax.experimental import pallas as pl
from jax.experimental.pallas import tpu as pltpu

F32 = jnp.float32
BF16 = jnp.bfloat16
EPS = 1e-6
NEG_INF = -1e30
N_DEV = 8
GRID_W = 64
ROPE_THETA = 10000.0
ROPE_DIM = 64
SWA_WINDOW = 128
MLA_SCALE = (128 + 64) ** -0.5
SWA_SCALE = 64 ** -0.5
RET_K_SCALE = 64 ** -0.5
ADAM_LR, ADAM_B1, ADAM_B2, ADAM_EPS, ADAM_WD, ADAM_STEP = 0.001, 0.9, 0.999, 1e-08, 0.01, 10
V7X_VMEM_LIMIT = 56 * 1024 * 1024
MESH = pl.DeviceIdType.MESH


def _pallas(body, **kw):
    return pl.pallas_call(body, **kw)


def _params(sem=None):
    return pltpu.CompilerParams(dimension_semantics=sem, vmem_limit_bytes=V7X_VMEM_LIMIT)


def _tile(n, cap, align):
    best = None
    for t in range(align, min(n, cap) + 1, align):
        if n % t == 0:
            best = t
    return n if best is None else best


def _sds(shape, dtype):
    return jax.ShapeDtypeStruct(tuple(shape), dtype)


def _ew(name, f, ins, out_dtypes, cap_elems=131072):
    R, C = ins[0].shape[-2:]
    tr = _tile(R, max(16, cap_elems // C), 16)
    n_in = len(ins)

    def spec(a):
        if a.ndim == 2:
            return pl.BlockSpec((tr, C), lambda i: (i, 0))
        return pl.BlockSpec((a.shape[0], tr, C), lambda i: (0, i, 0))

    def body(*refs):
        outs = f(*[r[...] for r in refs[:n_in]])
        for r, o in zip(refs[n_in:], outs):
            r[...] = o.astype(r.dtype)

    return _pallas(
        body, name=name, grid=(R // tr,), in_specs=[spec(a) for a in ins],
        out_specs=[pl.BlockSpec((tr, C), lambda i: (i, 0)) for _ in out_dtypes],
        out_shape=[_sds((R, C), d) for d in out_dtypes], compiler_params=_params(("parallel",)),
    )(*ins)


def _to_bf16(name, w):
    w2 = w.reshape(-1, w.shape[-1])
    return _ew(name, lambda v: (v,), [w2], [BF16])[0].reshape(w.shape)


def _adamw_math(w, g, m, v):
    m = ADAM_B1 * m + (1.0 - ADAM_B1) * g
    v = ADAM_B2 * v + (1.0 - ADAM_B2) * (g * g)
    m_hat = m / (1.0 - ADAM_B1 ** ADAM_STEP)
    v_hat = v / (1.0 - ADAM_B2 ** ADAM_STEP)
    delta = -ADAM_LR * (m_hat / (jnp.sqrt(v_hat) + ADAM_EPS) + ADAM_WD * w)
    return delta, m, v


def _sum_slots(land):
    g = land[0].astype(F32)
    for s in range(1, land.shape[0]):
        g = g + land[s].astype(F32)
    return g


def _adamw_from_slots(name, w, m, v, land):
    shp = w.shape
    C = shp[-1]

    def f(w_, m_, v_, land_):
        g = _sum_slots(land_)
        return (g,) + _adamw_math(w_, g, m_, v_)

    outs = _ew(name, f, [w.reshape(-1, C), m.reshape(-1, C), v.reshape(-1, C), land.reshape(N_DEV, -1, C)],
               [F32] * 4, cap_elems=65536)
    return [o.reshape(shp) for o in outs]


def _adamw(name, w, g, m, v):
    outs = _ew(name, lambda w_, g_, m_, v_: _adamw_math(w_, g_, m_, v_), [w, g, m, v], [F32] * 3)
    return outs


def _mm(name, a, b, ta=False, tb=False, out_dtype=F32, add=None):
    M, K = (a.shape[1], a.shape[0]) if ta else a.shape
    N = b.shape[0] if tb else b.shape[1]
    tm = _tile(M, 1024, 128)
    tn = _tile(N, 1024, 128)
    if tn < 256 and N <= 2432:
        tn = N
    tk = _tile(K, 1792, 128)
    nk = K // tk
    a_spec = pl.BlockSpec((tk, tm), lambda i, j, k: (k, i)) if ta else pl.BlockSpec((tm, tk), lambda i, j, k: (i, k))
    b_spec = pl.BlockSpec((tn, tk), lambda i, j, k: (j, k)) if tb else pl.BlockSpec((tk, tn), lambda i, j, k: (k, j))
    o_spec = pl.BlockSpec((tm, tn), lambda i, j, k: (i, j))
    dims = (((0 if ta else 1,), (1 if tb else 0,)), ((), ()))
    has_add = add is not None

    def product(a_ref, b_ref):
        return lax.dot_general(a_ref[...].astype(BF16), b_ref[...].astype(BF16), dims, preferred_element_type=F32)

    def body_single(*refs):
        acc = product(refs[0], refs[1])
        if has_add:
            acc = acc + refs[2][...]
        refs[-1][...] = acc.astype(refs[-1].dtype)

    def body(*refs):
        a_ref, b_ref = refs[0], refs[1]
        add_ref = refs[2] if has_add else None
        o_ref, acc = refs[-2], refs[-1]
        k = pl.program_id(2)

        @pl.when(k == 0)
        def _():
            acc[...] = add_ref[...] if has_add else jnp.zeros_like(acc)

        acc[...] += product(a_ref, b_ref)

        @pl.when(k == nk - 1)
        def _():
            o_ref[...] = acc[...].astype(o_ref.dtype)

    ins = [a, b] + ([add] if has_add else [])
    specs = [a_spec, b_spec] + ([o_spec] if has_add else [])
    return _pallas(
        body_single if nk == 1 else body, name=name, grid=(M // tm, N // tn, nk), in_specs=specs, out_specs=o_spec,
        out_shape=_sds((M, N), out_dtype), scratch_shapes=[] if nk == 1 else [pltpu.VMEM((tm, tn), F32)],
        compiler_params=_params(("parallel", "parallel", "arbitrary")),
    )(*ins)


class _Rows:
    def __init__(self, name, f, B, L, n, tm, arrays, pieces, samp, samp_pieces, glob, out_arrays, out_pieces):
        self.name, self.f, self.B, self.L, self.n, self.tm = name, f, B, L, n, tm
        self.arrays, self.pieces, self.samp, self.samp_pieces, self.glob = arrays, pieces, samp, samp_pieces, glob
        self.out_arrays, self.out_pieces = out_arrays, out_pieces
        self.nx = n // tm
        self.grid = (B, L // tm)

    def _row_spec(self, C, batched=True):
        tm = self.tm
        if batched:
            return pl.BlockSpec((None, tm, C), lambda b, i: (b, i, 0))
        return pl.BlockSpec((tm, C), lambda b, i: (i, 0))

    def _in_specs(self):
        nx = self.nx
        specs = [self._row_spec(a.shape[-1], a.ndim == 3) for a in self.arrays]
        specs += [pl.BlockSpec((None, None) + s.shape[2:], lambda b, i: (b, i // nx, 0, 0)) for s in self.samp]
        specs += [pl.BlockSpec(g.shape, lambda b, i, nd=g.ndim: (0,) * nd) for g in self.glob]
        return specs

    def _load(self, a_refs, s_refs, g_refs):
        args = [a_refs[ai][:, cs:cs + cw].astype(F32) for ai, cs, cw in self.pieces]
        args += [s_refs[si][r:r + 1, :].astype(F32) for si, r in self.samp_pieces]
        args += [g[...].astype(F32) for g in g_refs]
        return args

    def fwd(self):
        na, ns, ng = len(self.arrays), len(self.samp), len(self.glob)

        def body(*refs):
            a_refs, s_refs, g_refs = refs[:na], refs[na:na + ns], refs[na + ns:na + ns + ng]
            o_refs = refs[na + ns + ng:]
            outs = self.f(*self._load(a_refs, s_refs, g_refs))
            for (oi, cs, cw), o in zip(self.out_pieces, outs):
                o_refs[oi][:, cs:cs + cw] = o.astype(o_refs[oi].dtype)

        return _pallas(
            body, name=self.name + "_fwd", grid=self.grid, in_specs=self._in_specs(),
            out_specs=[self._row_spec(C) for C, _ in self.out_arrays],
            out_shape=[_sds((self.B, self.L, C), d) for C, d in self.out_arrays],
            compiler_params=_params(("parallel", "parallel")),
        )(*self.arrays, *self.samp, *self.glob)

    def bwd(self, cts, grad_arrays, grad_glob=(), add=None, samp=None):
        samp = self.samp if samp is None else samp
        na, ns, ng, nc = len(self.arrays), len(self.samp), len(self.glob), len(cts)
        ct_off = [c[1] if isinstance(c, tuple) else 0 for c in cts]
        cts = [c[0] if isinstance(c, tuple) else c for c in cts]
        add = add or {}
        add_keys = list(add)
        g_idx = list(grad_arrays)
        nx, B = self.nx, self.B
        n_in = na + ns + ng + nc + len(add_keys)
        n_pieces, n_sp = len(self.pieces), len(self.samp_pieces)

        def body(*refs):
            a_refs, s_refs, g_refs = refs[:na], refs[na:na + ns], refs[na + ns:na + ns + ng]
            c_refs = refs[na + ns + ng:na + ns + ng + nc]
            add_refs = refs[na + ns + ng + nc:n_in]
            d_refs = refs[n_in:n_in + len(g_idx)]
            ds_refs = refs[n_in + len(g_idx):n_in + len(g_idx) + n_sp]
            dg_refs = refs[n_in + len(g_idx) + n_sp:]
            b, i = pl.program_id(0), pl.program_id(1)
            args = self._load(a_refs, s_refs, g_refs)
            _, vjp = jax.vjp(lambda *xs: tuple(self.f(*xs)), *args)
            grads = vjp(tuple(c_refs[oi][:, ct_off[oi] + cs:ct_off[oi] + cs + cw].astype(F32)
                              for oi, cs, cw in self.out_pieces))
            for k, ai in enumerate(g_idx):
                covered = sum(cw for pa, _, cw in self.pieces if pa == ai)
                if covered < self.arrays[ai].shape[-1]:
                    d_refs[k][...] = jnp.zeros_like(d_refs[k])
                for (pa, cs, cw), gr in zip(self.pieces, grads[:n_pieces]):
                    if pa == ai:
                        if ai in add:
                            gr = gr + add_refs[add_keys.index(ai)][:, cs:cs + cw]
                        d_refs[k][:, cs:cs + cw] = gr.astype(d_refs[k].dtype)

            @pl.when((i == 0) | (i == nx))
            def _():
                for r in ds_refs:
                    r[...] = jnp.zeros_like(r)

            for r, gr in zip(ds_refs, grads[n_pieces:n_pieces + n_sp]):
                r[...] += gr

            @pl.when((b == 0) & (i == 0))
            def _():
                for r in dg_refs:
                    r[...] = jnp.zeros_like(r)

            for r, gi in zip(dg_refs, grad_glob):
                r[...] += grads[n_pieces + n_sp + gi]

        in_specs = self._in_specs() + [self._row_spec(c.shape[-1]) for c in cts]
        in_specs += [self._row_spec(add[k].shape[-1]) for k in add_keys]
        out_specs = [self._row_spec(self.arrays[ai].shape[-1]) for ai in g_idx]
        out_shape = [_sds(self.arrays[ai].shape, grad_arrays[ai]) for ai in g_idx]
        for si, _ in self.samp_pieces:
            C = self.samp[si].shape[-1]
            out_specs.append(pl.BlockSpec((None, None, 1, C), lambda b, i: (b, i // nx, 0, 0)))
            out_shape.append(_sds((B, 2, 1, C), F32))
        for gi in grad_glob:
            g = self.glob[gi]
            out_specs.append(pl.BlockSpec(g.shape, lambda b, i, nd=g.ndim: (0,) * nd))
            out_shape.append(_sds(g.shape, F32))
        outs = _pallas(
            body, name=self.name + "_bwd", grid=self.grid, in_specs=in_specs, out_specs=out_specs,
            out_shape=out_shape, compiler_params=_params(("arbitrary", "arbitrary")),
        )(*self.arrays, *samp, *self.glob, *cts, *[add[k] for k in add_keys])
        ng_ = len(g_idx)
        return outs[:ng_], outs[ng_:ng_ + n_sp], outs[ng_ + n_sp:]


def _rms(x, g):
    return x * lax.rsqrt(jnp.mean(x * x, axis=-1, keepdims=True) + EPS) * g


@jax.custom_vjp
def _swap_halves(x):
    w = x.shape[-1]
    lane = lax.broadcasted_iota(jnp.int32, x.shape, x.ndim - 1)
    up = pltpu.roll(x, w - ROPE_DIM // 2, x.ndim - 1)
    down = pltpu.roll(x, ROPE_DIM // 2, x.ndim - 1)
    return jnp.where(lane % ROPE_DIM < ROPE_DIM // 2, up, down)


_swap_halves.defvjp(lambda x: (_swap_halves(x), None), lambda _, ct: (_swap_halves(ct),))


@jax.custom_vjp
def _bdot(a, b):
    return jnp.dot(a.astype(BF16), b.astype(BF16), preferred_element_type=F32)


_bdot.defvjp(lambda a, b: (_bdot(a, b), (a, b)),
             lambda res, ct: (_dot_nt(ct.astype(BF16), res[1].astype(BF16)), _dot_tn(res[0].astype(BF16), ct.astype(BF16))))


def _rope(x, cos2, sin2):
    return x * cos2 + _swap_halves(x) * sin2


def _f_pre(s, shift, scale, g):
    return (_rms(s, g) * (1.0 + scale) + shift,)


def _f_res(s, y, gate):
    return (s + gate * y,)


def _f_swiglu(a, b):
    return (jax.nn.silu(a) * b,)


def _f_ab_pre(*args):
    kv_lat, q_lat, kpe = args[0:3]
    us, vs = args[3:7], args[7:11]
    cos2, sin2 = args[11:13]
    kv_norm, q_norm = args[13:15]
    vns, wss, bss = args[15:19], args[19:23], args[23:27]
    outs = [_rms(kv_lat, kv_norm), _rms(q_lat, q_norm), _rope(kpe, cos2, sin2)]
    for u, v, vn, ws, bs in zip(us, vs, vns, wss, bss):
        vg = _rms(jax.nn.gelu(v), vn)
        mixed = _bdot(ws, vg) + bs
        outs.append(jax.nn.gelu(u) * mixed)
    return tuple(outs)


def _f_q_rope(qn, qr, cos2, sin2):
    return qn, _rope(qr, cos2, sin2)


def _f_cd_pre(rk, rv, sk, sv, rq, rg, sq, c256, s256, c128, s128, c512, s512):
    return (_rope(rq, c256, s256), _rope(rk * RET_K_SCALE, c256, s256), rv,
            _rope(sq, c512, s512), _rope(sk, c128, s128), sv, rg)


def _f_cd_merge(*args):
    ys, rgs, rns = args[0:4], args[4:8], args[8:12]
    return tuple(_rms(y, rn) * jax.nn.silu(rg) for y, rg, rn in zip(ys, rgs, rns))


def _dot_nt(a, b):
    return lax.dot_general(a, b, (((1,), (1,)), ((), ())), preferred_element_type=F32)


def _dot_tn(a, b):
    return lax.dot_general(a, b, (((0,), (0,)), ((), ())), preferred_element_type=F32)


def _dot(a, b):
    return jnp.dot(a, b, preferred_element_type=F32)


def _tile_spec(tq, C):
    return pl.BlockSpec((None, tq, C), lambda b, i: (b, i, 0))


def _full_spec(L, C):
    return pl.BlockSpec((None, L, C), lambda b, i: (b, 0, 0))


def _mla_valid(i, tq, L, n):
    qpos = i * tq + lax.broadcasted_iota(jnp.int32, (tq, 1), 0)
    kpos = lax.broadcasted_iota(jnp.int32, (1, L), 1)
    return (qpos < n) | (kpos >= n)


def _mla_fwd(q, kv, kpe, n, tq):
    B, L, _ = q.shape

    def body(q_ref, kv_ref, kpe_ref, o_ref, lse_ref):
        valid = _mla_valid(pl.program_id(1), tq, L, n)
        kp = kpe_ref[:, 0:64]
        for h in range(4):
            s = _dot_nt(q_ref[:, 128 * h:128 * h + 128], kv_ref[:, 256 * h:256 * h + 128])
            s = s + _dot_nt(q_ref[:, 512 + 64 * h:576 + 64 * h], kp)
            s = jnp.where(valid, s * MLA_SCALE, NEG_INF)
            m = jnp.max(s, axis=1, keepdims=True)
            e = jnp.exp(s - m)
            l = jnp.sum(e, axis=1, keepdims=True)
            p = (e * (1.0 / l)).astype(BF16)
            o_ref[:, 128 * h:128 * h + 128] = _dot(p, kv_ref[:, 256 * h + 128:256 * h + 256]).astype(o_ref.dtype)
            lse_ref[:, h:h + 1] = m + jnp.log(l)

    return _pallas(
        body, name="mla_fwd", grid=(B, L // tq),
        in_specs=[_tile_spec(tq, 768), _full_spec(L, 1024), _full_spec(L, 128)],
        out_specs=[_tile_spec(tq, 512), _tile_spec(tq, 4)],
        out_shape=[_sds((B, L, 512), BF16), _sds((B, L, 4), F32)],
        compiler_params=_params(("parallel", "arbitrary")),
    )(q, kv, kpe)


def _mla_bwd(q, kv, kpe, lse, do, n, tq):
    B, L, _ = q.shape

    def body(q_ref, kv_ref, kpe_ref, lse_ref, do_ref, dq_ref, dkv_ref, dkpe_ref):
        i = pl.program_id(1)

        @pl.when(i == 0)
        def _():
            dkv_ref[...] = jnp.zeros_like(dkv_ref)
            dkpe_ref[...] = jnp.zeros_like(dkpe_ref)

        valid = _mla_valid(i, tq, L, n)
        kp = kpe_ref[:, 0:64]
        for h in range(4):
            qn, qr = q_ref[:, 128 * h:128 * h + 128], q_ref[:, 512 + 64 * h:576 + 64 * h]
            kn, v = kv_ref[:, 256 * h:256 * h + 128], kv_ref[:, 256 * h + 128:256 * h + 256]
            s = jnp.where(valid, (_dot_nt(qn, kn) + _dot_nt(qr, kp)) * MLA_SCALE, NEG_INF)
            p = jnp.exp(s - lse_ref[:, h:h + 1])
            doh = do_ref[:, 128 * h:128 * h + 128].astype(BF16)
            dp = _dot_nt(doh, v)
            delta = jnp.sum(p * dp, axis=1, keepdims=True)
            ds = (p * (dp - delta) * MLA_SCALE).astype(BF16)
            dq_ref[:, 128 * h:128 * h + 128] = _dot(ds, kn)
            dq_ref[:, 512 + 64 * h:576 + 64 * h] = _dot(ds, kp)
            dkv_ref[:, 256 * h:256 * h + 128] += _dot_tn(ds, qn)
            dkv_ref[:, 256 * h + 128:256 * h + 256] += _dot_tn(p.astype(BF16), doh)
            dkpe_ref[:, 0:64] += _dot_tn(ds, qr)

    return _pallas(
        body, name="mla_bwd", grid=(B, L // tq),
        in_specs=[_tile_spec(tq, 768), _full_spec(L, 1024), _full_spec(L, 128), _tile_spec(tq, 4),
                  pl.BlockSpec((None, tq, 512), lambda b, i: (b, i, 0))],
        out_specs=[_tile_spec(tq, 768), _full_spec(L, 1024), _full_spec(L, 128)],
        out_shape=[_sds((B, L, 768), F32), _sds((B, L, 1024), F32), _sds((B, L, 128), F32)],
        compiler_params=_params(("parallel", "arbitrary")),
    )(q, kv, kpe, lse, do)


def _swa_window(i, tq, n):
    W = min(tq + 2 * SWA_WINDOW, n)
    lo = jnp.clip(i * tq - SWA_WINDOW, 0, n - W)
    return pl.multiple_of(lo, 128), W


def _swa_mask(i, lo, tq, W, n):
    qpos = i * tq + lax.broadcasted_iota(jnp.int32, (tq, 1), 0)
    kpos = lo + lax.broadcasted_iota(jnp.int32, (1, W), 1)
    return (jnp.abs(qpos - kpos) <= SWA_WINDOW) & (qpos < n)


def _swa_fwd(q, k, v, sink, n, tq):
    B, L, _ = q.shape

    def body(q_ref, k_ref, v_ref, sink_ref, o_ref, lse_ref):
        i = pl.program_id(1)
        lo, W = _swa_window(i, tq, n)
        mask = _swa_mask(i, lo, tq, W, n)
        kl, vl = k_ref[pl.ds(lo, W), :], v_ref[pl.ds(lo, W), :]
        kc, vc = k_ref[n:L, :], v_ref[n:L, :]
        for h in range(8):
            g = h // 4
            qh = q_ref[:, 64 * h:64 * h + 64]
            s1 = jnp.where(mask, _dot_nt(qh, kl[:, 64 * g:64 * g + 64]) * SWA_SCALE, NEG_INF)
            s2 = _dot_nt(qh, kc[:, 64 * g:64 * g + 64]) * SWA_SCALE
            sk = sink_ref[0:1, h:h + 1]
            m = jnp.maximum(jnp.maximum(jnp.max(s1, axis=1, keepdims=True), jnp.max(s2, axis=1, keepdims=True)), sk)
            e1, e2 = jnp.exp(s1 - m), jnp.exp(s2 - m)
            l = jnp.sum(e1, axis=1, keepdims=True) + jnp.sum(e2, axis=1, keepdims=True) + jnp.exp(sk - m)
            r = 1.0 / l
            o = _dot((e1 * r).astype(BF16), vl[:, 64 * g:64 * g + 64]) + _dot((e2 * r).astype(BF16), vc[:, 64 * g:64 * g + 64])
            o_ref[:, 64 * h:64 * h + 64] = o.astype(o_ref.dtype)
            lse_ref[:, h:h + 1] = m + jnp.log(l)

    return _pallas(
        body, name="swa_fwd", grid=(B, L // tq),
        in_specs=[_tile_spec(tq, 512), _full_spec(L, 128), _full_spec(L, 128), pl.BlockSpec((1, 8), lambda b, i: (0, 0))],
        out_specs=[_tile_spec(tq, 512), _tile_spec(tq, 8)],
        out_shape=[_sds((B, L, 512), BF16), _sds((B, L, 8), F32)],
        compiler_params=_params(("parallel", "arbitrary")),
    )(q, k, v, sink)


def _swa_bwd(q, k, v, sink, lse, do, n, tq):
    B, L, _ = q.shape

    def body(q_ref, k_ref, v_ref, sink_ref, lse_ref, do_ref, dq_ref, dk_ref, dv_ref, dsink_ref):
        i = pl.program_id(1)

        @pl.when(i == 0)
        def _():
            dk_ref[...] = jnp.zeros_like(dk_ref)
            dv_ref[...] = jnp.zeros_like(dv_ref)
            dsink_ref[...] = jnp.zeros_like(dsink_ref)

        lo, W = _swa_window(i, tq, n)
        mask = _swa_mask(i, lo, tq, W, n)
        kl, vl = k_ref[pl.ds(lo, W), :], v_ref[pl.ds(lo, W), :]
        kc, vc = k_ref[n:L, :], v_ref[n:L, :]
        for h in range(8):
            g = h // 4
            cols = slice(64 * g, 64 * g + 64)
            qh = q_ref[:, 64 * h:64 * h + 64]
            lse_h = lse_ref[:, h:h + 1]
            s1 = jnp.where(mask, _dot_nt(qh, kl[:, cols]) * SWA_SCALE, NEG_INF)
            s2 = _dot_nt(qh, kc[:, cols]) * SWA_SCALE
            p1, p2 = jnp.exp(s1 - lse_h), jnp.exp(s2 - lse_h)
            ps = jnp.exp(sink_ref[0:1, h:h + 1] - lse_h)
            doh = do_ref[:, 64 * h:64 * h + 64].astype(BF16)
            dp1, dp2 = _dot_nt(doh, vl[:, cols]), _dot_nt(doh, vc[:, cols])
            delta = jnp.sum(p1 * dp1, axis=1, keepdims=True) + jnp.sum(p2 * dp2, axis=1, keepdims=True)
            ds1 = (p1 * (dp1 - delta) * SWA_SCALE).astype(BF16)
            ds2 = (p2 * (dp2 - delta) * SWA_SCALE).astype(BF16)
            dq_ref[:, 64 * h:64 * h + 64] = _dot(ds1, kl[:, cols]) + _dot(ds2, kc[:, cols])
            dk_ref[pl.ds(lo, W), cols] += _dot_tn(ds1, qh)
            dk_ref[n:L, cols] += _dot_tn(ds2, qh)
            dv_ref[pl.ds(lo, W), cols] += _dot_tn(p1.astype(BF16), doh)
            dv_ref[n:L, cols] += _dot_tn(p2.astype(BF16), doh)
            dsink_ref[0:1, h:h + 1] += jnp.sum(-ps * delta, axis=0, keepdims=True)

    return _pallas(
        body, name="swa_bwd", grid=(B, L // tq),
        in_specs=[_tile_spec(tq, 512), _full_spec(L, 128), _full_spec(L, 128), pl.BlockSpec((1, 8), lambda b, i: (0, 0)),
                  _tile_spec(tq, 8), pl.BlockSpec((None, tq, 512), lambda b, i: (b, i, 1))],
        out_specs=[_tile_spec(tq, 512), _full_spec(L, 128), _full_spec(L, 128),
                   pl.BlockSpec((None, 1, 8), lambda b, i: (b, 0, 0))],
        out_shape=[_sds((B, L, 512), F32), _sds((B, L, 128), F32), _sds((B, L, 128), F32), _sds((B, 1, 8), F32)],
        compiler_params=_params(("parallel", "arbitrary")),
    )(q, k, v, sink, lse, do)


def _ret_decay(i, tq, L, n):
    qi = i * tq + lax.broadcasted_iota(jnp.int32, (tq, 1), 0)
    q_latent = qi < n
    d_x = (qi - lax.broadcasted_iota(jnp.int32, (1, n), 1)).astype(F32)
    kc = lax.broadcasted_iota(jnp.int32, (1, L - n), 1)
    d_hf = (jnp.where(q_latent, qi, qi - L) - (kc - (L - n))).astype(F32)
    d_hb = (n + kc - qi).astype(F32)
    return q_latent, d_x, d_hf, d_hb


def _ret_mask(dist, lg):
    return jnp.where(dist >= 0.0, jnp.exp(lg * jnp.maximum(dist, 0.0)), 0.0)


def _ret_weights(q_latent, d_x, d_hf, d_hb, lg_f, lg_b, with_grad):
    e_x = jnp.exp(jnp.where(d_x >= 0.0, lg_f, -lg_b) * d_x)
    e_x = jnp.where(q_latent, e_x, 0.0)
    dec_x = jnp.where(d_x == 0.0, 2.0 * e_x, e_x)
    m_f, m_b = _ret_mask(d_hf, lg_f), _ret_mask(d_hb, lg_b)
    dec_h = m_f + m_b
    if not with_grad:
        return (dec_x, dec_h), None, None
    w_f = (e_x * jnp.maximum(d_x, 0.0), m_f * jnp.maximum(d_hf, 0.0))
    w_b = (e_x * jnp.maximum(-d_x, 0.0), m_b * jnp.maximum(d_hb, 0.0))
    return (dec_x, dec_h), w_f, w_b


def _ret_fwd(q, k, v, lg, n, tq):
    B, L, _ = q.shape

    def body(q_ref, k_ref, v_ref, lg_ref, y_ref):
        dist = _ret_decay(pl.program_id(1), tq, L, n)
        for h in range(4):
            dec, _, _ = _ret_weights(*dist, lg_ref[0:1, h:h + 1], lg_ref[1:2, h:h + 1], False)
            qh = q_ref[:, 64 * h:64 * h + 64]
            y = None
            for rows, dec_r in zip((slice(0, n), slice(n, L)), dec):
                a = _dot_nt(qh, k_ref[rows, 64 * h:64 * h + 64]) * dec_r
                part = _dot(a.astype(BF16), v_ref[rows, 128 * h:128 * h + 128])
                y = part if y is None else y + part
            y_ref[:, 128 * h:128 * h + 128] = y

    return _pallas(
        body, name="ret_fwd", grid=(B, L // tq),
        in_specs=[_tile_spec(tq, 256), _full_spec(L, 256), _full_spec(L, 512), pl.BlockSpec((2, 4), lambda b, i: (0, 0))],
        out_specs=_tile_spec(tq, 512), out_shape=_sds((B, L, 512), F32),
        compiler_params=_params(("parallel", "arbitrary")),
    )(q, k, v, lg)


def _ret_bwd(q, k, v, lg, dy, n, tq):
    B, L, _ = q.shape

    def body(q_ref, k_ref, v_ref, lg_ref, dy_ref, dq_ref, dk_ref, dv_ref, dlg_ref):
        i = pl.program_id(1)

        @pl.when(i == 0)
        def _():
            dk_ref[...] = jnp.zeros_like(dk_ref)
            dv_ref[...] = jnp.zeros_like(dv_ref)
            dlg_ref[...] = jnp.zeros_like(dlg_ref)

        def total(a):
            return jnp.sum(jnp.sum(a, axis=1, keepdims=True), axis=0, keepdims=True)

        dist = _ret_decay(i, tq, L, n)
        for h in range(4):
            qc, vc = slice(64 * h, 64 * h + 64), slice(128 * h, 128 * h + 128)
            dec, w_f, w_b = _ret_weights(*dist, lg_ref[0:1, h:h + 1], lg_ref[1:2, h:h + 1], True)
            qh = q_ref[:, qc]
            dyh = dy_ref[:, vc].astype(BF16)
            dq = None
            for rows, dec_r, wf_r, wb_r in zip((slice(0, n), slice(n, L)), dec, w_f, w_b):
                kh, vh = k_ref[rows, qc], v_ref[rows, vc]
                s = _dot_nt(qh, kh)
                gr = _dot_nt(dyh, vh)
                ds = (gr * dec_r).astype(BF16)
                part = _dot(ds, kh)
                dq = part if dq is None else dq + part
                dk_ref[rows, qc] += _dot_tn(ds, qh)
                dv_ref[rows, vc] += _dot_tn((s * dec_r).astype(BF16), dyh)
                gs = gr * s
                dlg_ref[0:1, h:h + 1] += total(gs * wf_r)
                dlg_ref[1:2, h:h + 1] += total(gs * wb_r)
            dq_ref[:, qc] = dq

    return _pallas(
        body, name="ret_bwd", grid=(B, L // tq),
        in_specs=[_tile_spec(tq, 256), _full_spec(L, 256), _full_spec(L, 512), pl.BlockSpec((2, 4), lambda b, i: (0, 0)),
                  _tile_spec(tq, 512)],
        out_specs=[_tile_spec(tq, 256), _full_spec(L, 256), _full_spec(L, 512),
                   pl.BlockSpec((None, 2, 4), lambda b, i: (b, 0, 0))],
        out_shape=[_sds((B, L, 256), F32), _sds((B, L, 256), F32), _sds((B, L, 512), F32), _sds((B, 2, 4), F32)],
        compiler_params=_params(("parallel", "arbitrary")),
    )(q, k, v, lg, dy)


def _loss_head(s, target, g, n, tm):
    B, L, D = s.shape
    nx = n // tm

    def body(s_ref, t_ref, g_ref, ds_ref, dg_ref, loss_ref):
        b, i = pl.program_id(0), pl.program_id(1)

        @pl.when((b == 0) & (i == 0))
        def _():
            dg_ref[...] = jnp.zeros_like(dg_ref)
            loss_ref[...] = jnp.zeros_like(loss_ref)

        @pl.when(i < nx)
        def _():
            y, vjp = jax.vjp(_rms, s_ref[...], g_ref[...])
            err = y - t_ref[...]
            d_s, d_g = vjp(err * (1.0 / D))
            ds_ref[...] = d_s
            dg_ref[...] += d_g
            part = jnp.sum(jnp.sum(err * err, axis=1, keepdims=True), axis=0, keepdims=True) * (0.5 / D)
            loss_ref[...] += jnp.broadcast_to(part, loss_ref.shape)

        @pl.when(i >= nx)
        def _():
            ds_ref[...] = jnp.zeros_like(ds_ref)

    return _pallas(
        body, name="loss_head", grid=(B, L // tm),
        in_specs=[pl.BlockSpec((None, tm, D), lambda b, i: (b, i, 0)),
                  pl.BlockSpec((None, tm, D), lambda b, i: (b, jnp.minimum(i, nx - 1), 0)),
                  pl.BlockSpec((1, D), lambda b, i: (0, 0))],
        out_specs=[pl.BlockSpec((None, tm, D), lambda b, i: (b, i, 0)), pl.BlockSpec((1, D), lambda b, i: (0, 0)),
                   pl.BlockSpec((1, 128), lambda b, i: (0, 0))],
        out_shape=[_sds((B, L, D), F32), _sds((1, D), F32), _sds((1, 128), F32)],
        compiler_params=_params(("arbitrary", "arbitrary")),
    )(s, target, g)


def _ada_fwd(c_all, ada_w, ada_b):
    NL, D, Ns = ada_w.shape
    R = c_all.shape[0]

    def body(c_ref, w_ref, b_ref, o_ref):
        cond = jax.nn.silu(c_ref[...]).astype(BF16)
        o_ref[...] = _dot(cond, w_ref[...].astype(BF16)) + b_ref[...]

    return _pallas(
        body, name="ada_fwd", grid=(NL,),
        in_specs=[pl.BlockSpec((R, D), lambda l: (0, 0)), pl.BlockSpec((None, D, Ns), lambda l: (l, 0, 0)),
                  pl.BlockSpec((None, 1, Ns), lambda l: (l, 0, 0))],
        out_specs=pl.BlockSpec((None, R, Ns), lambda l: (l, 0, 0)), out_shape=_sds((NL, R, Ns), F32),
        compiler_params=_params(("parallel",)),
    )(c_all, ada_w, ada_b)


def _ada_bwd(c_all, ada_w, dmods):
    NL, D, Ns = ada_w.shape
    R = c_all.shape[0]

    def body(c_ref, w_ref, dm_ref, dw_ref, dc_ref):
        cond = jax.nn.silu(c_ref[...]).astype(BF16)
        dm = dm_ref[...].astype(BF16)
        dw_ref[...] = _dot_tn(cond, dm)
        dc_ref[...] = _dot_nt(dm, w_ref[...].astype(BF16))

    return _pallas(
        body, name="ada_bwd", grid=(NL,),
        in_specs=[pl.BlockSpec((R, D), lambda l: (0, 0)), pl.BlockSpec((None, D, Ns), lambda l: (l, 0, 0)),
                  pl.BlockSpec((None, R, Ns), lambda l: (l, 0, 0))],
        out_specs=[pl.BlockSpec((None, D, Ns), lambda l: (l, 0, 0)), pl.BlockSpec((None, R, D), lambda l: (l, 0, 0))],
        out_shape=[_sds((NL, D, Ns), F32), _sds((NL, R, D), F32)],
        compiler_params=_params(("parallel",)),
    )(c_all, ada_w, dmods)


def _my_index():
    return 4 * lax.axis_index("x") + 2 * lax.axis_index("y") + lax.axis_index("c")


def _peer(k):
    x, y, c = lax.axis_index("x"), lax.axis_index("y"), lax.axis_index("c")
    kx, ky, kc = (k >> 2) & 1, (k >> 1) & 1, k & 1
    px, py, pc = (x + kx) % 2, (y + ky) % 2, (c + kc) % 2
    return (px, py, pc), 4 * px + 2 * py + pc


def _all_gather(name, shards):
    na = len(shards)
    hbm = pl.BlockSpec(memory_space=pl.ANY)

    def body(*refs):
        in_refs, out_refs = refs[:na], refs[na:2 * na]
        send_sems, recv_sems, local_sems = refs[2 * na:]
        me = _my_index()
        sib_id, sib = _peer(1)
        chips = [_peer(k) for k in (4, 2, 6)]
        sib_chips = [4 * px + 2 * py + (1 - pc) for (px, py, pc), _ in chips]

        def copy(a, k, slot, to, src=None):
            dst = out_refs[a].at[slot]
            return pltpu.make_async_remote_copy(
                src_ref=dst if src is None else src, dst_ref=dst, send_sem=send_sems.at[a, k],
                recv_sem=recv_sems.at[a, k], device_id=to, device_id_type=MESH)

        first, passed, mine = [], [], []
        for a in range(na):
            cp = pltpu.make_async_copy(in_refs[a], out_refs[a].at[me], local_sems.at[a])
            cp.start()
            mine.append(cp)
            first.append(copy(a, 0, me, sib_id, src=in_refs[a]))
            first += [copy(a, 1 + j, me, pid, src=in_refs[a]) for j, (pid, _) in enumerate(chips)]
        for cp in first:
            cp.start()
        for a in range(na):
            for j, (pid, pidx) in enumerate(chips):
                copy(a, 1 + j, pidx, pid).wait_recv()
                fwd = copy(a, 4 + j, pidx, sib_id)
                fwd.start()
                passed.append(fwd)
        for a in range(na):
            copy(a, 0, sib, sib_id).wait_recv()
            for j in range(3):
                copy(a, 4 + j, sib_chips[j], sib_id).wait_recv()
        for cp in first + passed:
            cp.wait_send()
        for cp in mine:
            cp.wait()

    return _pallas(
        body, name=name, in_specs=[hbm] * na, out_specs=[hbm] * na,
        out_shape=[_sds((N_DEV,) + s.shape, s.dtype) for s in shards],
        scratch_shapes=[pltpu.SemaphoreType.DMA((na, 7)), pltpu.SemaphoreType.DMA((na, 7)),
                        pltpu.SemaphoreType.DMA((na,))],
    )(*shards)


def _all_to_all(name, parts):
    na = len(parts)
    hbm = pl.BlockSpec(memory_space=pl.ANY)

    def body(*refs):
        in_refs, out_refs = refs[:na], refs[na:2 * na]
        send_sems, recv_sems, local_sems = refs[2 * na:]
        me = _my_index()
        copies = []
        for a in range(na):
            cp = pltpu.make_async_copy(in_refs[a].at[me], out_refs[a].at[me], local_sems.at[a])
            cp.start()
            copies.append(cp)
            for k in range(1, N_DEV):
                pid, pidx = _peer(k)
                cp = pltpu.make_async_remote_copy(
                    src_ref=in_refs[a].at[pidx], dst_ref=out_refs[a].at[me], send_sem=send_sems.at[a, k - 1],
                    recv_sem=recv_sems.at[a, k - 1], device_id=pid, device_id_type=MESH)
                cp.start()
                copies.append(cp)
        for cp in copies:
            cp.wait()

    return _pallas(
        body, name=name, in_specs=[hbm] * na, out_specs=[hbm] * na,
        out_shape=[_sds(p.shape, p.dtype) for p in parts],
        scratch_shapes=[pltpu.SemaphoreType.DMA((na, 7)), pltpu.SemaphoreType.DMA((na, 7)),
                        pltpu.SemaphoreType.DMA((na,))],
    )(*parts)


_HBM = pl.BlockSpec(memory_space=pltpu.HBM)
_SEM = pl.BlockSpec(memory_space=pltpu.SEMAPHORE)
_DATAFLOW = pltpu.SideEffectType.DATAFLOW_SIDE_EFFECTING


def _exchange_start(name, srcs, slotted):
    na = len(srcs)
    lands = [lax.empty((N_DEV,) + (s.shape[1:] if slotted else s.shape), s.dtype) for s in srcs]

    def body(*refs):
        src_refs, land_refs = refs[:na], refs[na:2 * na]
        send_sems, recv_sems = refs[2 * na:3 * na], refs[3 * na:4 * na]
        token = refs[6 * na]
        me = _my_index()
        for a in range(na):
            for k in range(1, N_DEV):
                pid, pidx = _peer(k)
                pltpu.make_async_remote_copy(
                    src_ref=src_refs[a].at[pidx] if slotted else src_refs[a], dst_ref=land_refs[a].at[me],
                    send_sem=send_sems[a], recv_sem=recv_sems[a], device_id=pid, device_id_type=MESH).start()
        token[...] = jnp.zeros_like(token)

    ops = [pltpu.with_memory_space_constraint(a, pltpu.HBM) for a in list(srcs) + lands]
    outs = _pallas(
        body, name=name,
        out_shape=[pltpu.SemaphoreType.DMA(())] * (2 * na) + [pltpu.HBM(a.shape, a.dtype) for a in ops]
        + [_sds((8, 128), F32)],
        in_specs=[_HBM] * (2 * na),
        out_specs=[_SEM] * (2 * na) + [_HBM] * (2 * na) + [pl.BlockSpec(memory_space=pltpu.VMEM)],
        input_output_aliases={a: 2 * na + a for a in range(2 * na)},
        compiler_params=pltpu.CompilerParams(has_side_effects=_DATAFLOW),
    )(*ops)
    return (na, outs[:2 * na], outs[2 * na:4 * na]), outs[4 * na]


def _exchange_wait(name, handle, after):
    na, sems, thru = handle

    def body(*refs):
        land_refs = refs[na:2 * na]
        send_sems, recv_sems = refs[2 * na:3 * na], refs[3 * na:4 * na]
        me_id = (lax.axis_index("x"), lax.axis_index("y"), lax.axis_index("c"))
        for a in range(na):
            seven = land_refs[a].at[pl.ds(0, N_DEV - 1)]
            drain = pltpu.make_async_remote_copy(src_ref=seven, dst_ref=seven, send_sem=send_sems[a],
                                                 recv_sem=recv_sems[a], device_id=me_id, device_id_type=MESH)
            drain.wait_send()
            drain.wait_recv()

    outs = _pallas(
        body, name=name, out_shape=[pltpu.HBM(a.shape, a.dtype) for a in thru],
        in_specs=[_HBM] * (2 * na) + [_SEM] * (2 * na) + [pl.BlockSpec(memory_space=pl.ANY)],
        out_specs=[_HBM] * (2 * na), input_output_aliases={a: a for a in range(2 * na)},
        compiler_params=pltpu.CompilerParams(has_side_effects=_DATAFLOW),
    )(*thru, *sems, after)
    return outs[na:]


def _with_own(landed, own, slot_axis=0):
    me = _my_index()
    slot = lax.broadcasted_iota(jnp.int32, landed.shape, slot_axis)
    return jnp.where(slot == me, own, landed)


def _rope_tables(n, L, width):
    t = jnp.arange(n)
    row = (t // GRID_W).astype(F32)
    col = (t % GRID_W).astype(F32)
    n_freq = ROPE_DIM // 4
    freqs = ROPE_THETA ** (-jnp.arange(n_freq, dtype=F32) / n_freq)
    ang = jnp.concatenate([row[:, None] * freqs, col[:, None] * freqs], axis=-1)
    cos, sin = jnp.cos(ang), jnp.sin(ang)
    cos2 = jnp.concatenate([cos, cos], axis=-1)
    sin2 = jnp.concatenate([-sin, sin], axis=-1)
    cos2 = jnp.concatenate([cos2, jnp.ones((L - n, ROPE_DIM), F32)], axis=0)
    sin2 = jnp.concatenate([sin2, jnp.zeros((L - n, ROPE_DIM), F32)], axis=0)
    reps = width // ROPE_DIM
    return jnp.tile(cos2, (1, reps)), jnp.tile(sin2, (1, reps))


def _ab_perm(w):
    return jnp.concatenate([w[:, 0:256], w[:, 320:1600], w[:, 256:320], jnp.zeros((w.shape[0], 64), w.dtype)], axis=1)


def _ab_unperm(g):
    return jnp.concatenate([g[:, 0:256], g[:, 1536:1600], g[:, 256:1536]], axis=1)


def _wq_perm(w):
    return jnp.concatenate([w[:, 192 * h:192 * h + 128] for h in range(4)]
                           + [w[:, 192 * h + 128:192 * h + 192] for h in range(4)], axis=1)


def _wq_unperm(g):
    return jnp.concatenate([g[:, sl] for h in range(4)
                            for sl in (slice(128 * h, 128 * h + 128), slice(512 + 64 * h, 576 + 64 * h))], axis=1)


def _flat(a):
    return a.reshape(-1, a.shape[-1])


def _layer_weights(full, p):
    NL, NE, NO = len(full["ffn_in"]), len(full["ab_in"]), len(full["cd_in"])
    groups = range(4)

    def each(f, mats):
        return [None if w is None else f(w) for w in mats]

    return dict(
        norm_mix=[p["norm_mix"][l][None] for l in range(NL)], norm_ffn=[p["norm_ffn"][l][None] for l in range(NL)],
        norm_final=p["norm_final"][None],
        ffn_in=list(full["ffn_in"]), ffn_out=list(full["ffn_out"]),
        ab_in=each(_ab_perm, full["ab_in"]), ab_out=list(full["ab_out"]),
        wq=each(_wq_perm, full["mla_wq_b"]), wkv=list(full["mla_wkv_b"]),
        kv_norm=[p["mla_kv_norm"][j][None] for j in range(NE)], q_norm=[p["mla_q_norm"][j][None] for j in range(NE)],
        v_norm=[[p["cmlp_v_norm"][j][None, 128 * g:128 * g + 128] for g in groups] for j in range(NE)],
        ws=[[p["cmlp_ws"][j, g] for g in groups] for j in range(NE)],
        bs=[[p["cmlp_bs"][j, g][:, None] for g in groups] for j in range(NE)],
        cd_in=list(full["cd_in"]), cd_out=list(full["cd_out"]),
        lg=[jnp.stack([jax.nn.log_sigmoid(p["ret_decay_fwd"][j]), jax.nn.log_sigmoid(p["ret_decay_bwd"][j])])
            for j in range(NO)],
        sink=[p["swa_sink"][j][None] for j in range(NO)],
        ret_norm=[[p["ret_norm"][j][None, 128 * g:128 * g + 128] for g in groups] for j in range(NO)],
    )


def _local_step(x, ctx, target, mods, W, later_weights=None, early_grads=None):
    B, n, D = x.shape
    m = ctx.shape[1]
    L = n + m
    NL = mods.shape[0]
    tm = min(256, m)
    tq = min(256, m)
    cos512, sin512 = _rope_tables(n, L, 512)
    s = jnp.concatenate([x, ctx], axis=1)
    saved = []

    def rows(name, f, arrays, pieces, samp, samp_pieces, glob, out_arrays, out_pieces, tile=tm):
        return _Rows(name, f, B, L, n, tile, arrays, pieces, samp, samp_pieces, glob, out_arrays, out_pieces)

    def full(width, start=0):
        return [(0, start, width)]

    for l in range(NL):
        j = l // 2
        even = l % 2 == 0
        md = mods[l]
        if l == 1 and later_weights is not None:
            W = later_weights(s)
        r = {"s0": s}
        pre1 = rows(f"pre_mix{l}", _f_pre, [s], full(D), [md], [(0, 0), (0, 1)], [W["norm_mix"][l]], [(D, BF16)], full(D))
        (xn,) = pre1.fwd()
        r["pre1"], r["xn"] = pre1, xn
        if even:
            z = _mm(f"ab_in{l}", _flat(xn), W["ab_in"][j]).reshape(B, L, 1664)
            pieces = [(0, 0, 256), (0, 256, 256), (0, 1536, 128)]
            pieces += [(0, 512 + 128 * g, 128) for g in range(4)] + [(0, 1024 + 128 * g, 128) for g in range(4)]
            pieces += [(1, 0, 128), (2, 0, 128)]
            glob = [W["kv_norm"][j], W["q_norm"][j]] + W["v_norm"][j] + W["ws"][j] + W["bs"][j]
            abp = rows(f"ab_pre{l}", _f_ab_pre, [z, cos512, sin512], pieces, [], [], glob,
                       [(256, BF16), (256, BF16), (128, BF16), (512, BF16)],
                       [(0, 0, 256), (1, 0, 256), (2, 0, 128)] + [(3, 128 * g, 128) for g in range(4)], tile=128)
            kvn, qn, kpe, cm = abp.fwd()
            kv = _mm(f"wkv{l}", _flat(kvn), W["wkv"][j], out_dtype=BF16).reshape(B, L, 1024)
            q0 = _mm(f"wq{l}", _flat(qn), W["wq"][j]).reshape(B, L, 768)
            qrp = rows(f"q_rope{l}", _f_q_rope, [q0, cos512, sin512], [(0, 0, 512), (0, 512, 256), (1, 0, 256), (2, 0, 256)],
                       [], [], [], [(768, BF16)], [(0, 0, 512), (0, 512, 256)])
            (q,) = qrp.fwd()
            o, lse = _mla_fwd(q, kv, kpe, n, tq)
            y = _mm(f"ab_out_a{l}", _flat(o), W["ab_out"][j][:512])
            y = _mm(f"ab_out_b{l}", _flat(cm), W["ab_out"][j][512:], add=y).reshape(B, L, D)
            r.update(z=z, abp=abp, kvn=kvn, qn=qn, kpe=kpe, cm=cm, kv=kv, qrp=qrp, q=q, o=o, lse=lse)
        else:
            z = _mm(f"cd_in{l}", _flat(xn), W["cd_in"][j]).reshape(B, L, 2304)
            pieces = [(0, 0, 256), (0, 256, 512), (0, 768, 128), (0, 896, 128), (0, 1024, 256), (0, 1280, 512), (0, 1792, 512)]
            pieces += [(1, 0, 256), (2, 0, 256), (1, 0, 128), (2, 0, 128), (1, 0, 512), (2, 0, 512)]
            cdp = rows(f"cd_pre{l}", _f_cd_pre, [z, cos512, sin512], pieces, [], [], [],
                       [(256, BF16), (256, BF16), (512, BF16), (512, BF16), (128, BF16), (128, BF16), (512, F32)],
                       [(k_, 0, w_) for k_, w_ in enumerate((256, 256, 512, 512, 128, 128, 512))])
            rq, rk, rv, sq, sk, sv, rg = cdp.fwd()
            yret = _ret_fwd(rq, rk, rv, W["lg"][j], n, tq)
            osw, lse = _swa_fwd(sq, sk, sv, W["sink"][j], n, tq)
            mrg = rows(f"cd_merge{l}", _f_cd_merge, [yret, rg],
                       [(0, 128 * g, 128) for g in range(4)] + [(1, 128 * g, 128) for g in range(4)], [], [],
                       W["ret_norm"][j], [(512, BF16)], [(0, 128 * g, 128) for g in range(4)])
            (yr,) = mrg.fwd()
            y = _mm(f"cd_out_a{l}", _flat(yr), W["cd_out"][j][:512])
            y = _mm(f"cd_out_b{l}", _flat(osw), W["cd_out"][j][512:], add=y).reshape(B, L, D)
            r.update(z=z, cdp=cdp, rq=rq, rk=rk, rv=rv, sq=sq, sk=sk, sv=sv, rg=rg, yret=yret, osw=osw, lse=lse,
                     mrg=mrg, yr=yr)
        res1 = rows(f"res_mix{l}", _f_res, [s, y], [(0, 0, D), (1, 0, D)], [md], [(0, 2)], [], [(D, F32)], full(D))
        (s1,) = res1.fwd()
        pre2 = rows(f"pre_ffn{l}", _f_pre, [s1], full(D), [md], [(0, 3), (0, 4)], [W["norm_ffn"][l]], [(D, BF16)], full(D))
        (xn2,) = pre2.fwd()
        Fh = W["ffn_out"][l].shape[0]
        z2 = _mm(f"ffn_in{l}", _flat(xn2), W["ffn_in"][l]).reshape(B, L, 2 * Fh)
        swi = rows(f"swiglu{l}", _f_swiglu, [z2], [(0, 0, Fh), (0, Fh, Fh)], [], [], [], [(Fh, BF16)], full(Fh))
        (hid,) = swi.fwd()
        y2 = _mm(f"ffn_out{l}", _flat(hid), W["ffn_out"][l]).reshape(B, L, D)
        res2 = rows(f"res_ffn{l}", _f_res, [s1, y2], [(0, 0, D), (1, 0, D)], [md], [(0, 5)], [], [(D, F32)], full(D))
        (s,) = res2.fwd()
        r.update(res1=res1, pre2=pre2, xn2=xn2, swi=swi, hid=hid, res2=res2)
        saved.append(r)

    ds, d_norm_final, loss = _loss_head(s, target, W["norm_final"], n, tm)

    G = {k: [None] * len(v) for k, v in W.items() if isinstance(v, list)}
    G["norm_final"] = d_norm_final
    dmods = [None] * NL
    for l in reversed(range(NL)):
        j = l // 2
        even = l % 2 == 0
        r = saved[l]
        dm = [None] * 6
        smp = None
        if l == 0 and early_grads is not None:
            zero = early_grads(G)
            smp = None if zero is None else [mods[0] + zero]
        (ds1, dy2), (dm[5],), _ = r["res2"].bwd([ds], {0: F32, 1: BF16}, samp=smp)
        dy2f = _flat(dy2)
        dhid = _mm(f"d_hid{l}", dy2f, W["ffn_out"][l], tb=True).reshape(B, L, -1)
        G["ffn_out"][l] = _mm(f"g_ffn_out{l}", _flat(r["hid"]), dy2f, ta=True, out_dtype=BF16)
        (dz2,), _, _ = r["swi"].bwd([dhid], {0: BF16})
        dz2f = _flat(dz2)
        G["ffn_in"][l] = _mm(f"g_ffn_in{l}", _flat(r["xn2"]), dz2f, ta=True, out_dtype=BF16)
        dxn2 = _mm(f"d_xn2{l}", dz2f, W["ffn_in"][l], tb=True).reshape(B, L, D)
        (ds1,), (dm[3], dm[4]), (G["norm_ffn"][l],) = r["pre2"].bwd([dxn2], {0: F32}, grad_glob=(0,), add={0: ds1},
                                                                       samp=smp)
        (ds0, dy), (dm[2],), _ = r["res1"].bwd([ds1], {0: F32, 1: BF16}, samp=smp)
        dyf = _flat(dy)
        if even:
            w_out = W["ab_out"][j]
            dcat = _mm(f"d_cat{l}", dyf, w_out, tb=True).reshape(B, L, -1)
            G["ab_out"][l // 2] = jnp.concatenate(
                [_mm(f"g_ab_out_a{l}", _flat(r["o"]), dyf, ta=True, out_dtype=BF16),
                 _mm(f"g_ab_out_b{l}", _flat(r["cm"]), dyf, ta=True, out_dtype=BF16)], axis=0)
            dq, dkv, dkpe = _mla_bwd(r["q"], r["kv"], r["kpe"], r["lse"], dcat, n, tq)
            (dq0,), _, _ = r["qrp"].bwd([dq], {0: BF16})
            dq0f, dkvf = _flat(dq0), _flat(dkv)
            G["wq"][j] = _mm(f"g_wq{l}", _flat(r["qn"]), dq0f, ta=True, out_dtype=BF16)
            G["wkv"][j] = _mm(f"g_wkv{l}", _flat(r["kvn"]), dkvf, ta=True, out_dtype=BF16)
            dqn = _mm(f"d_qn{l}", dq0f, W["wq"][j], tb=True).reshape(B, L, 256)
            dkvn = _mm(f"d_kvn{l}", dkvf, W["wkv"][j], tb=True).reshape(B, L, 256)
            (dz,), _, gg = r["abp"].bwd([dkvn, dqn, dkpe, (dcat, 512)], {0: BF16}, grad_glob=tuple(range(14)))
            G["kv_norm"][j], G["q_norm"][j] = gg[0], gg[1]
            G["v_norm"][j], G["ws"][j], G["bs"][j] = list(gg[2:6]), list(gg[6:10]), list(gg[10:14])
            w_in, key = W["ab_in"][j], "ab_in"
        else:
            w_out = W["cd_out"][j]
            dcat = _mm(f"d_cat{l}", dyf, w_out, tb=True).reshape(B, L, -1)
            G["cd_out"][j] = jnp.concatenate(
                [_mm(f"g_cd_out_a{l}", _flat(r["yr"]), dyf, ta=True, out_dtype=BF16),
                 _mm(f"g_cd_out_b{l}", _flat(r["osw"]), dyf, ta=True, out_dtype=BF16)], axis=0)
            (dyret, drg), _, gg = r["mrg"].bwd([(dcat, 0)], {0: F32, 1: F32}, grad_glob=(0, 1, 2, 3))
            G["ret_norm"][j] = list(gg)
            drq, drk, drv, dlg = _ret_bwd(r["rq"], r["rk"], r["rv"], W["lg"][j], dyret, n, tq)
            dsq, dsk, dsv, dsink = _swa_bwd(r["sq"], r["sk"], r["sv"], W["sink"][j], r["lse"], dcat, n, tq)
            G["lg"][j], G["sink"][j] = dlg, dsink
            (dz,), _, _ = r["cdp"].bwd([drq, drk, drv, dsq, dsk, dsv, drg], {0: BF16})
            w_in, key = W["cd_in"][j], "cd_in"
        dzf = _flat(dz)
        G[key][j] = _mm(f"g_{key}{l}", _flat(r["xn"]), dzf, ta=True, out_dtype=BF16)
        dxn = _mm(f"d_xn{l}", dzf, w_in, tb=True).reshape(B, L, D)
        (ds,), (dm[0], dm[1]), (G["norm_mix"][l],) = r["pre1"].bwd([dxn], {0: F32}, grad_glob=(0,), add={0: ds0},
                                                                       samp=smp)
        dmods[l] = jnp.concatenate(dm, axis=2)
    return loss, ds[:, :n], jnp.stack(dmods), G


def kernel(x, c, ctx, c_ctx, ada_w, ada_b, norm_mix, norm_ffn, norm_final, ffn_in, ffn_out, ab_in, ab_out, mla_q_norm, mla_kv_norm, mla_wq_b, mla_wkv_b, cmlp_v_norm, cmlp_ws, cmlp_bs, cd_in, cd_out, ret_decay_fwd, ret_decay_bwd, ret_norm, swa_sink, loss_target, m_c_ctx, m_ada_w, m_ada_b, m_norm_mix, m_norm_ffn, m_norm_final, m_ffn_in, m_ffn_out, m_ab_in, m_ab_out, m_mla_q_norm, m_mla_kv_norm, m_mla_wq_b, m_mla_wkv_b, m_cmlp_v_norm, m_cmlp_ws, m_cmlp_bs, m_cd_in, m_cd_out, m_ret_decay_fwd, m_ret_decay_bwd, m_ret_norm, m_swa_sink, v_c_ctx, v_ada_w, v_ada_b, v_norm_mix, v_norm_ffn, v_norm_final, v_ffn_in, v_ffn_out, v_ab_in, v_ab_out, v_mla_q_norm, v_mla_kv_norm, v_mla_wq_b, v_mla_wkv_b, v_cmlp_v_norm, v_cmlp_ws, v_cmlp_bs, v_cd_in, v_cd_out, v_ret_decay_fwd, v_ret_decay_bwd, v_ret_norm, v_swa_sink):
    B, n, D = x.shape
    NL = ada_w.shape[0]
    NE, NO = ab_in.shape[0], cd_in.shape[0]
    me = _my_index()
    weights = dict(c_ctx=c_ctx, ada_w=ada_w, ada_b=ada_b, norm_mix=norm_mix, norm_ffn=norm_ffn, norm_final=norm_final,
                   ffn_in=ffn_in, ffn_out=ffn_out, ab_in=ab_in, ab_out=ab_out, mla_q_norm=mla_q_norm,
                   mla_kv_norm=mla_kv_norm, mla_wq_b=mla_wq_b, mla_wkv_b=mla_wkv_b, cmlp_v_norm=cmlp_v_norm,
                   cmlp_ws=cmlp_ws, cmlp_bs=cmlp_bs, cd_in=cd_in, cd_out=cd_out, ret_decay_fwd=ret_decay_fwd,
                   ret_decay_bwd=ret_decay_bwd, ret_norm=ret_norm, swa_sink=swa_sink)
    moments_m = dict(c_ctx=m_c_ctx, ada_w=m_ada_w, ada_b=m_ada_b, norm_mix=m_norm_mix, norm_ffn=m_norm_ffn,
                     norm_final=m_norm_final, ffn_in=m_ffn_in, ffn_out=m_ffn_out, ab_in=m_ab_in, ab_out=m_ab_out,
                     mla_q_norm=m_mla_q_norm, mla_kv_norm=m_mla_kv_norm, mla_wq_b=m_mla_wq_b, mla_wkv_b=m_mla_wkv_b,
                     cmlp_v_norm=m_cmlp_v_norm, cmlp_ws=m_cmlp_ws, cmlp_bs=m_cmlp_bs, cd_in=m_cd_in, cd_out=m_cd_out,
                     ret_decay_fwd=m_ret_decay_fwd, ret_decay_bwd=m_ret_decay_bwd, ret_norm=m_ret_norm,
                     swa_sink=m_swa_sink)
    moments_v = dict(c_ctx=v_c_ctx, ada_w=v_ada_w, ada_b=v_ada_b, norm_mix=v_norm_mix, norm_ffn=v_norm_ffn,
                     norm_final=v_norm_final, ffn_in=v_ffn_in, ffn_out=v_ffn_out, ab_in=v_ab_in, ab_out=v_ab_out,
                     mla_q_norm=v_mla_q_norm, mla_kv_norm=v_mla_kv_norm, mla_wq_b=v_mla_wq_b, mla_wkv_b=v_mla_wkv_b,
                     cmlp_v_norm=v_cmlp_v_norm, cmlp_ws=v_cmlp_ws, cmlp_bs=v_cmlp_bs, cd_in=v_cd_in, cd_out=v_cd_out,
                     ret_decay_fwd=v_ret_decay_fwd, ret_decay_bwd=v_ret_decay_bwd, ret_norm=v_ret_norm,
                     swa_sink=v_swa_sink)
    order = list(weights)

    Ns = ada_w.shape[2]
    (c_g,) = _all_gather("gather_c", [c])
    R = N_DEV * B + 8
    c_all = jnp.concatenate([c_g.reshape(N_DEV * B, D), jnp.broadcast_to(c_ctx[None], (8, D))], axis=0)
    ada_b_mine = lax.dynamic_slice_in_dim(ada_b, me * Ns, Ns, axis=1)[:, None, :]
    mods_shard = _ada_fwd(c_all, ada_w, ada_b_mine)
    (mods_g,) = _all_gather("gather_mods", [mods_shard])
    mods_full = jnp.transpose(mods_g, (1, 2, 0, 3)).reshape(NL, R, 6, D)
    mx = lax.dynamic_slice_in_dim(mods_full, me * B, B, axis=1)
    mh = jnp.broadcast_to(mods_full[:, N_DEV * B][:, None], (NL, B, 6, D))
    mods = jnp.stack([mx, mh], axis=2)

    big = ["ffn_in", "ffn_out", "ab_in", "ab_out", "cd_in", "cd_out", "mla_wq_b", "mla_wkv_b"]
    col_sharded = {"ffn_in", "ab_in", "cd_in", "mla_wq_b", "mla_wkv_b"}
    shards = {k: _to_bf16("cast_" + k, weights[k]) for k in big}
    first = {k: 0 if k.startswith("cd_") else 1 for k in big}
    early_keys = [k for k in big if first[k]]
    early = _all_gather("gather_w0", [shards[k][:1] for k in early_keys])
    w_handle, w_token = _exchange_start("gather_w_start", [shards[k][first[k]:] for k in big], slotted=False)
    mods = mods + w_token[0, 0]

    def unshard(k, g):
        if k in col_sharded:
            f = jnp.transpose(g, (1, 2, 0, 3)).reshape(g.shape[1], g.shape[2], -1)
        else:
            f = jnp.transpose(g, (1, 0, 2, 3)).reshape(g.shape[1], -1, g.shape[3])
        return [f[i] for i in range(f.shape[0])]

    (rn_g,) = _all_gather("gather_ret_norm", [ret_norm])
    rn_full = jnp.transpose(rn_g, (1, 0, 2)).reshape(NO, -1)
    small_p = dict(weights, ret_norm=rn_full)
    full0 = {k: [None] * weights[k].shape[0] for k in big}
    for k, g in zip(early_keys, early):
        full0[k][:1] = unshard(k, g)

    def later_weights(s_after_layer0):
        landed = _exchange_wait("gather_w_wait", w_handle, s_after_layer0)
        full = {k: full0[k][:first[k]] + unshard(k, _with_own(land, shards[k][first[k]:]))
                for k, land in zip(big, landed)}
        return _layer_weights(full, small_p)

    def to_slots(k, gl):
        g = jnp.stack(gl)
        if k in col_sharded:
            return jnp.transpose(g.reshape(g.shape[0], g.shape[1], N_DEV, -1), (2, 0, 1, 3))
        return jnp.transpose(g.reshape(g.shape[0], N_DEV, -1, g.shape[2]), (1, 0, 2, 3))

    def big_grads(G):
        return dict(ffn_in=G["ffn_in"], ffn_out=G["ffn_out"], ab_in=[g if g is None else _ab_unperm(g) for g in G["ab_in"]],
                    ab_out=G["ab_out"], cd_in=G["cd_in"], cd_out=G["cd_out"],
                    mla_wq_b=[g if g is None else _wq_unperm(g) for g in G["wq"]], mla_wkv_b=G["wkv"])

    sent = {}

    def early_grads(G):
        parts = big_grads(G)
        sent["srcs"] = [to_slots(k, parts[k][first[k]:]) for k in big]
        sent["handle"], g_token = _exchange_start("scatter_g_start", sent["srcs"], slotted=True)
        return g_token[0, 0]

    loss_part, grad_x, dmods, G = _local_step(x, ctx, loss_target, mods, _layer_weights(full0, small_p),
                                              later_weights, early_grads)
    loss = lax.psum(loss_part[0, 0], ("x", "y", "c"))

    dmx = dmods[:, :, 0].reshape(NL, B, 6 * D)
    dmh = jnp.sum(dmods[:, :, 1], axis=1).reshape(NL, 1, 6 * D)
    (dm_g,) = _all_gather("gather_dmods", [jnp.concatenate([dmx, dmh], axis=1)])
    dmx_all = jnp.transpose(dm_g[:, :, :B], (1, 0, 2, 3)).reshape(NL, N_DEV * B, 6 * D)
    dmh_all = jnp.sum(dm_g[:, :, B], axis=0)
    dm_rows = jnp.concatenate([dmx_all, dmh_all[:, None], jnp.zeros((NL, 7, 6 * D), F32)], axis=1)
    g_ada_b = jnp.sum(dm_rows, axis=1)
    dm_mine = lax.dynamic_slice_in_dim(dm_rows, me * Ns, Ns, axis=2)
    g_ada_w, dcond = _ada_bwd(c_all, ada_w, dm_mine)
    sg = jax.nn.sigmoid(c_ctx)
    d_c_ctx_part = jnp.sum(dcond[:, N_DEV * B], axis=0) * (sg * (1.0 + c_ctx * (1.0 - sg)))

    def cat(parts):
        return jnp.concatenate([p.reshape(-1) for p in parts])

    dlg = jnp.stack([jnp.sum(G["lg"][j], axis=0) for j in range(NO)])
    sig_f, sig_b = jax.nn.sigmoid(-ret_decay_fwd), jax.nn.sigmoid(-ret_decay_bwd)
    small = dict(
        c_ctx=d_c_ctx_part,
        norm_mix=cat(G["norm_mix"]), norm_ffn=cat(G["norm_ffn"]), norm_final=G["norm_final"].reshape(-1),
        mla_q_norm=cat(G["q_norm"]), mla_kv_norm=cat(G["kv_norm"]),
        cmlp_v_norm=cat([cat(G["v_norm"][j]) for j in range(NE)]),
        cmlp_ws=cat([jnp.stack(G["ws"][j]) for j in range(NE)]),
        cmlp_bs=cat([jnp.stack([b_[:, 0] for b_ in G["bs"][j]]) for j in range(NE)]),
        ret_decay_fwd=(dlg[:, 0] * sig_f).reshape(-1), ret_decay_bwd=(dlg[:, 1] * sig_b).reshape(-1),
        ret_norm=cat([cat(G["ret_norm"][j]) for j in range(NO)]),
        swa_sink=cat([jnp.sum(G["sink"][j], axis=0) for j in range(NO)]),
    )
    small_keys = list(small)
    sizes = [small[k].shape[0] for k in small_keys]
    total = sum(sizes)
    padded = -(-total // 2048) * 2048
    packed = jnp.concatenate([small[k] for k in small_keys] + [jnp.zeros((padded - total,), F32)]).reshape(-1, 128)
    (small_g,) = _all_gather("gather_small", [packed])

    parts = big_grads(G)
    landed0 = dict(zip(early_keys, _all_to_all("scatter_g0", [to_slots(k, parts[k][:1]) for k in early_keys])))
    landed_rest = _exchange_wait("scatter_g_wait", sent["handle"], grad_x)
    landed = []
    for k, land, src in zip(big, landed_rest, sent["srcs"]):
        rest = _with_own(land, src)
        landed.append(jnp.concatenate([landed0[k], rest], axis=1) if first[k] else rest)

    grads, deltas, new_m, new_v = {}, {}, {}, {}
    for k, land in zip(big, landed):
        grads[k], deltas[k], new_m[k], new_v[k] = _adamw_from_slots("adamw_" + k, weights[k], moments_m[k], moments_v[k], land)
    deltas["ada_w"], new_m["ada_w"], new_v["ada_w"] = [
        o.reshape(ada_w.shape) for o in _adamw("adamw_ada_w", _flat(ada_w), _flat(g_ada_w), _flat(m_ada_w), _flat(v_ada_w))]
    grads["ada_w"] = g_ada_w

    def packed_of(src, fill):
        vals = [src[k].reshape(-1) if k != "ret_norm" else jnp.full((sizes[i],), fill, F32)
                for i, k in enumerate(small_keys)]
        return jnp.concatenate(vals + [jnp.full((padded - total,), fill, F32)]).reshape(-1, 128)

    w_p, m_p, v_p = packed_of(weights, 0.0), packed_of(moments_m, 0.0), packed_of(moments_v, 1.0)

    def f_small(w_, m_, v_, land_):
        g = _sum_slots(land_)
        return (g,) + _adamw_math(w_, g, m_, v_)

    g_p, d_p, nm_p, nv_p = _ew("adamw_small", f_small, [w_p, m_p, v_p, small_g], [F32] * 4)
    offs = np.cumsum([0] + sizes)
    for i, k in enumerate(small_keys):
        sl = slice(int(offs[i]), int(offs[i + 1]))
        if k == "ret_norm":
            g_full = g_p.reshape(-1)[sl].reshape(NO, -1)
            g_mine = lax.dynamic_slice_in_dim(g_full, me * ret_norm.shape[1], ret_norm.shape[1], axis=1)
            d_, m_, v_ = _adamw("adamw_ret_norm", *[jnp.pad(a, ((0, 8 - NO), (0, 128 - a.shape[1])), constant_values=cv)
                                                     for a, cv in ((ret_norm, 0.0), (g_mine, 0.0), (m_ret_norm, 0.0), (v_ret_norm, 1.0))])
            grads[k] = g_mine
            deltas[k], new_m[k], new_v[k] = [a[:NO, :ret_norm.shape[1]] for a in (d_, m_, v_)]
        else:
            shp = weights[k].shape
            grads[k], deltas[k], new_m[k], new_v[k] = [a.reshape(-1)[sl].reshape(shp) for a in (g_p, d_p, nm_p, nv_p)]
    pad_b = lambda a, cv=0.0: jnp.pad(a, ((0, 8 - NL), (0, 0)), constant_values=cv)
    d_, m_, v_ = _adamw("adamw_ada_b", pad_b(ada_b), pad_b(g_ada_b), pad_b(m_ada_b), pad_b(v_ada_b, 1.0))
    grads["ada_b"] = g_ada_b
    deltas["ada_b"], new_m["ada_b"], new_v["ada_b"] = d_[:NL], m_[:NL], v_[:NL]

    return (loss, grad_x, *[grads[k] for k in order], *[deltas[k] for k in order],
            *[new_m[k] for k in order], *[new_v[k] for k in order])
```

```python
import functools

import numpy as np
import jax
import jax.numpy as jnp
from jax import lax
from jax.experimental import pallas as pl
from jax.experimental.pallas import tpu as pltpu

F32 = jnp.float32
BF16 = jnp.bfloat16
EPS = 1e-6
NEG_INF = -1e30
N_DEV = 8
GRID_W = 64
ROPE_THETA = 10000.0
ROPE_DIM = 64
SWA_WINDOW = 128
MLA_SCALE = (128 + 64) ** -0.5
SWA_SCALE = 64 ** -0.5
RET_K_SCALE = 64 ** -0.5
ADAM_LR, ADAM_B1, ADAM_B2, ADAM_EPS, ADAM_WD, ADAM_STEP = 0.001, 0.9, 0.999, 1e-08, 0.01, 10
V7X_VMEM_LIMIT = 56 * 1024 * 1024
MESH = pl.DeviceIdType.MESH


def _pallas(body, **kw):
    return pl.pallas_call(body, **kw)


def _params(sem=None):
    return pltpu.CompilerParams(dimension_semantics=sem, vmem_limit_bytes=V7X_VMEM_LIMIT)


def _tile(n, cap, align):
    best = None
    for t in range(align, min(n, cap) + 1, align):
        if n % t == 0:
            best = t
    return n if best is None else best


def _sds(shape, dtype):
    return jax.ShapeDtypeStruct(tuple(shape), dtype)


def _ew(name, f, ins, out_dtypes, cap_elems=131072):
    R, C = ins[0].shape[-2:]
    tr = _tile(R, max(16, cap_elems // C), 16)
    n_in = len(ins)

    def spec(a):
        if a.ndim == 2:
            return pl.BlockSpec((tr, C), lambda i: (i, 0))
        return pl.BlockSpec((a.shape[0], tr, C), lambda i: (0, i, 0))

    def body(*refs):
        outs = f(*[r[...] for r in refs[:n_in]])
        for r, o in zip(refs[n_in:], outs):
            r[...] = o.astype(r.dtype)

    return _pallas(
        body, name=name, grid=(R // tr,), in_specs=[spec(a) for a in ins],
        out_specs=[pl.BlockSpec((tr, C), lambda i: (i, 0)) for _ in out_dtypes],
        out_shape=[_sds((R, C), d) for d in out_dtypes], compiler_params=_params(("parallel",)),
    )(*ins)


def _to_bf16(name, w):
    w2 = w.reshape(-1, w.shape[-1])
    return _ew(name, lambda v: (v,), [w2], [BF16])[0].reshape(w.shape)


def _adamw_math(w, g, m, v):
    m = ADAM_B1 * m + (1.0 - ADAM_B1) * g
    v = ADAM_B2 * v + (1.0 - ADAM_B2) * (g * g)
    m_hat = m / (1.0 - ADAM_B1 ** ADAM_STEP)
    v_hat = v / (1.0 - ADAM_B2 ** ADAM_STEP)
    delta = -ADAM_LR * (m_hat / (jnp.sqrt(v_hat) + ADAM_EPS) + ADAM_WD * w)
    return delta, m, v


def _sum_slots(land):
    g = land[0].astype(F32)
    for s in range(1, land.shape[0]):
        g = g + land[s].astype(F32)
    return g


def _adamw_from_slots(name, w, m, v, land):
    shp = w.shape
    C = shp[-1]

    def f(w_, m_, v_, land_):
        g = _sum_slots(land_)
        return (g,) + _adamw_math(w_, g, m_, v_)

    outs = _ew(name, f, [w.reshape(-1, C), m.reshape(-1, C), v.reshape(-1, C), land.reshape(N_DEV, -1, C)],
               [F32] * 4, cap_elems=65536)
    return [o.reshape(shp) for o in outs]


def _adamw(name, w, g, m, v):
    outs = _ew(name, lambda w_, g_, m_, v_: _adamw_math(w_, g_, m_, v_), [w, g, m, v], [F32] * 3)
    return outs


def _mm(name, a, b, ta=False, tb=False, out_dtype=F32, add=None):
    M, K = (a.shape[1], a.shape[0]) if ta else a.shape
    N = b.shape[0] if tb else b.shape[1]
    tm = _tile(M, 1024, 128)
    tn = _tile(N, 1024, 128)
    if tn < 256 and N <= 2432:
        tn = N
    tk = _tile(K, 1792, 128)
    nk = K // tk
    a_spec = pl.BlockSpec((tk, tm), lambda i, j, k: (k, i)) if ta else pl.BlockSpec((tm, tk), lambda i, j, k: (i, k))
    b_spec = pl.BlockSpec((tn, tk), lambda i, j, k: (j, k)) if tb else pl.BlockSpec((tk, tn), lambda i, j, k: (k, j))
    o_spec = pl.BlockSpec((tm, tn), lambda i, j, k: (i, j))
    dims = (((0 if ta else 1,), (1 if tb else 0,)), ((), ()))
    has_add = add is not None

    def product(a_ref, b_ref):
        return lax.dot_general(a_ref[...].astype(BF16), b_ref[...].astype(BF16), dims, preferred_element_type=F32)

    def body_single(*refs):
        acc = product(refs[0], refs[1])
        if has_add:
            acc = acc + refs[2][...]
        refs[-1][...] = acc.astype(refs[-1].dtype)

    def body(*refs):
        a_ref, b_ref = refs[0], refs[1]
        add_ref = refs[2] if has_add else None
        o_ref, acc = refs[-2], refs[-1]
        k = pl.program_id(2)

        @pl.when(k == 0)
        def _():
            acc[...] = add_ref[...] if has_add else jnp.zeros_like(acc)

        acc[...] += product(a_ref, b_ref)

        @pl.when(k == nk - 1)
        def _():
            o_ref[...] = acc[...].astype(o_ref.dtype)

    ins = [a, b] + ([add] if has_add else [])
    specs = [a_spec, b_spec] + ([o_spec] if has_add else [])
    return _pallas(
        body_single if nk == 1 else body, name=name, grid=(M // tm, N // tn, nk), in_specs=specs, out_specs=o_spec,
        out_shape=_sds((M, N), out_dtype), scratch_shapes=[] if nk == 1 else [pltpu.VMEM((tm, tn), F32)],
        compiler_params=_params(("parallel", "parallel", "arbitrary")),
    )(*ins)


def _ffn_specs(M, D, F):
    tm = _tile(M, 1024, 128)
    tf = _tile(F, 256, 128)
    nj = F // tf
    x_spec = pl.BlockSpec((tm, D), lambda i, j: (i, 0))
    wa_spec = pl.BlockSpec((D, tf), lambda i, j: (0, j))
    wb_spec = pl.BlockSpec((D, tf), lambda i, j: (0, j + nj))
    wo_spec = pl.BlockSpec((tf, D), lambda i, j: (j, 0))
    h_spec = pl.BlockSpec((tm, tf), lambda i, j: (i, j))
    return tm, nj, x_spec, wa_spec, wb_spec, wo_spec, h_spec


def _ffn_fwd(name, xn, w_in, w_out):
    M, D = xn.shape
    F = w_out.shape[0]
    tm, nj, x_spec, wa_spec, wb_spec, wo_spec, _ = _ffn_specs(M, D, F)

    def body(x_ref, wa_ref, wb_ref, wo_ref, y_ref):
        j = pl.program_id(1)
        x = x_ref[...]
        hid = (jax.nn.silu(_dot(x, wa_ref[...])) * _dot(x, wb_ref[...])).astype(BF16)
        part = _dot(hid, wo_ref[...])

        @pl.when(j == 0)
        def _():
            y_ref[...] = part

        @pl.when(j > 0)
        def _():
            y_ref[...] += part

    return _pallas(
        body, name=name, grid=(M // tm, nj), in_specs=[x_spec, wa_spec, wb_spec, wo_spec], out_specs=x_spec,
        out_shape=_sds((M, D), F32), compiler_params=_params(("parallel", "arbitrary")),
    )(xn, w_in, w_in, w_out)


def _ffn_mid_bwd(name, xn, dy, w_in, w_out):
    M, D = xn.shape
    F = w_out.shape[0]
    tm, nj, x_spec, wa_spec, wb_spec, wo_spec, h_spec = _ffn_specs(M, D, F)

    def body(x_ref, dy_ref, wa_ref, wb_ref, wo_ref, h_ref, da_ref, db_ref):
        x = x_ref[...]
        a, b = _dot(x, wa_ref[...]), _dot(x, wb_ref[...])
        dh = _dot_nt(dy_ref[...], wo_ref[...])
        sig = jax.nn.sigmoid(a)
        act = a * sig
        h_ref[...] = (act * b).astype(BF16)
        da_ref[...] = (dh * b * (sig * (1.0 + a * (1.0 - sig)))).astype(BF16)
        db_ref[...] = (dh * act).astype(BF16)

    return _pallas(
        body, name=name, grid=(M // tm, nj), in_specs=[x_spec, x_spec, wa_spec, wb_spec, wo_spec],
        out_specs=[h_spec] * 3, out_shape=[_sds((M, F), BF16)] * 3,
        compiler_params=_params(("parallel", "parallel")),
    )(xn, dy, w_in, w_in, w_out)


class _Rows:
    def __init__(self, name, f, B, L, n, tm, arrays, pieces, samp, samp_pieces, glob, out_arrays, out_pieces):
        self.name, self.f, self.B, self.L, self.n, self.tm = name, f, B, L, n, tm
        self.arrays, self.pieces, self.samp, self.samp_pieces, self.glob = arrays, pieces, samp, samp_pieces, glob
        self.out_arrays, self.out_pieces = out_arrays, out_pieces
        self.nx = n // tm
        self.grid = (B, L // tm)

    def _row_spec(self, C, batched=True):
        tm = self.tm
        if batched:
            return pl.BlockSpec((None, tm, C), lambda b, i: (b, i, 0))
        return pl.BlockSpec((tm, C), lambda b, i: (i, 0))

    def _in_specs(self):
        nx = self.nx
        specs = [self._row_spec(a.shape[-1], a.ndim == 3) for a in self.arrays]
        specs += [pl.BlockSpec((None, None) + s.shape[2:], lambda b, i: (b, i // nx, 0, 0)) for s in self.samp]
        specs += [pl.BlockSpec(g.shape, lambda b, i, nd=g.ndim: (0,) * nd) for g in self.glob]
        return specs

    def _load(self, a_refs, s_refs, g_refs):
        args = [a_refs[ai][:, cs:cs + cw].astype(F32) for ai, cs, cw in self.pieces]
        args += [s_refs[si][r:r + 1, :].astype(F32) for si, r in self.samp_pieces]
        args += [g[...].astype(F32) for g in g_refs]
        return args

    def fwd(self):
        na, ns, ng = len(self.arrays), len(self.samp), len(self.glob)

        def body(*refs):
            a_refs, s_refs, g_refs = refs[:na], refs[na:na + ns], refs[na + ns:na + ns + ng]
            o_refs = refs[na + ns + ng:]
            outs = self.f(*self._load(a_refs, s_refs, g_refs))
            for (oi, cs, cw), o in zip(self.out_pieces, outs):
                o_refs[oi][:, cs:cs + cw] = o.astype(o_refs[oi].dtype)

        return _pallas(
            body, name=self.name + "_fwd", grid=self.grid, in_specs=self._in_specs(),
            out_specs=[self._row_spec(C) for C, _ in self.out_arrays],
            out_shape=[_sds((self.B, self.L, C), d) for C, d in self.out_arrays],
            compiler_params=_params(("parallel", "parallel")),
        )(*self.arrays, *self.samp, *self.glob)

    def bwd(self, cts, grad_arrays, grad_glob=(), add=None, samp=None):
        samp = self.samp if samp is None else samp
        na, ns, ng, nc = len(self.arrays), len(self.samp), len(self.glob), len(cts)
        ct_off = [c[1] if isinstance(c, tuple) else 0 for c in cts]
        cts = [c[0] if isinstance(c, tuple) else c for c in cts]
        add = add or {}
        add_keys = list(add)
        g_idx = list(grad_arrays)
        nx, B = self.nx, self.B
        n_in = na + ns + ng + nc + len(add_keys)
        n_pieces, n_sp = len(self.pieces), len(self.samp_pieces)

        def body(*refs):
            a_refs, s_refs, g_refs = refs[:na], refs[na:na + ns], refs[na + ns:na + ns + ng]
            c_refs = refs[na + ns + ng:na + ns + ng + nc]
            add_refs = refs[na + ns + ng + nc:n_in]
            d_refs = refs[n_in:n_in + len(g_idx)]
            ds_refs = refs[n_in + len(g_idx):n_in + len(g_idx) + n_sp]
            dg_refs = refs[n_in + len(g_idx) + n_sp:]
            b, i = pl.program_id(0), pl.program_id(1)
            args = self._load(a_refs, s_refs, g_refs)
            _, vjp = jax.vjp(lambda *xs: tuple(self.f(*xs)), *args)
            grads = vjp(tuple(c_refs[oi][:, ct_off[oi] + cs:ct_off[oi] + cs + cw].astype(F32)
                              for oi, cs, cw in self.out_pieces))
            for k, ai in enumerate(g_idx):
                covered = sum(cw for pa, _, cw in self.pieces if pa == ai)
                if covered < self.arrays[ai].shape[-1]:
                    d_refs[k][...] = jnp.zeros_like(d_refs[k])
                for (pa, cs, cw), gr in zip(self.pieces, grads[:n_pieces]):
                    if pa == ai:
                        if ai in add:
                            gr = gr + add_refs[add_keys.index(ai)][:, cs:cs + cw]
                        d_refs[k][:, cs:cs + cw] = gr.astype(d_refs[k].dtype)

            @pl.when((i == 0) | (i == nx))
            def _():
                for r in ds_refs:
                    r[...] = jnp.zeros_like(r)

            for r, gr in zip(ds_refs, grads[n_pieces:n_pieces + n_sp]):
                r[...] += gr

            @pl.when((b == 0) & (i == 0))
            def _():
                for r in dg_refs:
                    r[...] = jnp.zeros_like(r)

            for r, gi in zip(dg_refs, grad_glob):
                r[...] += grads[n_pieces + n_sp + gi]

        in_specs = self._in_specs() + [self._row_spec(c.shape[-1]) for c in cts]
        in_specs += [self._row_spec(add[k].shape[-1]) for k in add_keys]
        out_specs = [self._row_spec(self.arrays[ai].shape[-1]) for ai in g_idx]
        out_shape = [_sds(self.arrays[ai].shape, grad_arrays[ai]) for ai in g_idx]
        for si, _ in self.samp_pieces:
            C = self.samp[si].shape[-1]
            out_specs.append(pl.BlockSpec((None, None, 1, C), lambda b, i: (b, i // nx, 0, 0)))
            out_shape.append(_sds((B, 2, 1, C), F32))
        for gi in grad_glob:
            g = self.glob[gi]
            out_specs.append(pl.BlockSpec(g.shape, lambda b, i, nd=g.ndim: (0,) * nd))
            out_shape.append(_sds(g.shape, F32))
        outs = _pallas(
            body, name=self.name + "_bwd", grid=self.grid, in_specs=in_specs, out_specs=out_specs,
            out_shape=out_shape, compiler_params=_params(("arbitrary", "arbitrary")),
        )(*self.arrays, *samp, *self.glob, *cts, *[add[k] for k in add_keys])
        ng_ = len(g_idx)
        return outs[:ng_], outs[ng_:ng_ + n_sp], outs[ng_ + n_sp:]


def _rms(x, g):
    return x * lax.rsqrt(jnp.mean(x * x, axis=-1, keepdims=True) + EPS) * g


@jax.custom_vjp
def _swap_halves(x):
    w = x.shape[-1]
    lane = lax.broadcasted_iota(jnp.int32, x.shape, x.ndim - 1)
    up = pltpu.roll(x, w - ROPE_DIM // 2, x.ndim - 1)
    down = pltpu.roll(x, ROPE_DIM // 2, x.ndim - 1)
    return jnp.where(lane % ROPE_DIM < ROPE_DIM // 2, up, down)


_swap_halves.defvjp(lambda x: (_swap_halves(x), None), lambda _, ct: (_swap_halves(ct),))


@jax.custom_vjp
def _bdot(a, b):
    return jnp.dot(a.astype(BF16), b.astype(BF16), preferred_element_type=F32)


_bdot.defvjp(lambda a, b: (_bdot(a, b), (a, b)),
             lambda res, ct: (_dot_nt(ct.astype(BF16), res[1].astype(BF16)), _dot_tn(res[0].astype(BF16), ct.astype(BF16))))


def _rope(x, cos2, sin2):
    return x * cos2 + _swap_halves(x) * sin2


def _f_pre(s, shift, scale, g):
    return (_rms(s, g) * (1.0 + scale) + shift,)


def _f_res(s, y, gate):
    return (s + gate * y,)


def _f_ab_pre(*args):
    kv_lat, q_lat, kpe = args[0:3]
    us, vs = args[3:7], args[7:11]
    cos2, sin2 = args[11:13]
    kv_norm, q_norm = args[13:15]
    vns, wss, bss = args[15:19], args[19:23], args[23:27]
    outs = [_rms(kv_lat, kv_norm), _rms(q_lat, q_norm), _rope(kpe, cos2, sin2)]
    for u, v, vn, ws, bs in zip(us, vs, vns, wss, bss):
        vg = _rms(jax.nn.gelu(v), vn)
        mixed = _bdot(ws, vg) + bs
        outs.append(jax.nn.gelu(u) * mixed)
    return tuple(outs)


def _f_q_rope(qn, qr, cos2, sin2):
    return qn, _rope(qr, cos2, sin2)


def _f_cd_pre(rk, rv, sk, sv, rq, rg, sq, c256, s256, c128, s128, c512, s512):
    return (_rope(rq, c256, s256), _rope(rk * RET_K_SCALE, c256, s256), rv,
            _rope(sq, c512, s512), _rope(sk, c128, s128), sv, rg)


def _f_cd_merge(*args):
    ys, rgs, rns = args[0:4], args[4:8], args[8:12]
    return tuple(_rms(y, rn) * jax.nn.silu(rg) for y, rg, rn in zip(ys, rgs, rns))


def _dot_nt(a, b):
    return lax.dot_general(a, b, (((1,), (1,)), ((), ())), preferred_element_type=F32)


def _dot_tn(a, b):
    return lax.dot_general(a, b, (((0,), (0,)), ((), ())), preferred_element_type=F32)


def _dot(a, b):
    return jnp.dot(a, b, preferred_element_type=F32)


def _tile_spec(tq, C):
    return pl.BlockSpec((None, tq, C), lambda b, i: (b, i, 0))


def _full_spec(L, C):
    return pl.BlockSpec((None, L, C), lambda b, i: (b, 0, 0))


def _mla_valid(i, tq, L, n):
    qpos = i * tq + lax.broadcasted_iota(jnp.int32, (tq, 1), 0)
    kpos = lax.broadcasted_iota(jnp.int32, (1, L), 1)
    return (qpos < n) | (kpos >= n)


def _mla_fwd(q, kv, kpe, n, tq):
    B, L, _ = q.shape

    def body(q_ref, kv_ref, kpe_ref, o_ref, lse_ref):
        valid = _mla_valid(pl.program_id(1), tq, L, n)
        kp = kpe_ref[:, 0:64]
        for h in range(4):
            s = _dot_nt(q_ref[:, 128 * h:128 * h + 128], kv_ref[:, 256 * h:256 * h + 128])
            s = s + _dot_nt(q_ref[:, 512 + 64 * h:576 + 64 * h], kp)
            s = jnp.where(valid, s * MLA_SCALE, NEG_INF)
            m = jnp.max(s, axis=1, keepdims=True)
            e = jnp.exp(s - m)
            l = jnp.sum(e, axis=1, keepdims=True)
            p = (e * (1.0 / l)).astype(BF16)
            o_ref[:, 128 * h:128 * h + 128] = _dot(p, kv_ref[:, 256 * h + 128:256 * h + 256]).astype(o_ref.dtype)
            lse_ref[:, h:h + 1] = m + jnp.log(l)

    return _pallas(
        body, name="mla_fwd", grid=(B, L // tq),
        in_specs=[_tile_spec(tq, 768), _full_spec(L, 1024), _full_spec(L, 128)],
        out_specs=[_tile_spec(tq, 512), _tile_spec(tq, 4)],
        out_shape=[_sds((B, L, 512), BF16), _sds((B, L, 4), F32)],
        compiler_params=_params(("parallel", "arbitrary")),
    )(q, kv, kpe)


def _mla_bwd(q, kv, kpe, lse, do, n, tq):
    B, L, _ = q.shape

    def body(q_ref, kv_ref, kpe_ref, lse_ref, do_ref, dq_ref, dkv_ref, dkpe_ref):
        i = pl.program_id(1)

        @pl.when(i == 0)
        def _():
            dkv_ref[...] = jnp.zeros_like(dkv_ref)
            dkpe_ref[...] = jnp.zeros_like(dkpe_ref)

        valid = _mla_valid(i, tq, L, n)
        kp = kpe_ref[:, 0:64]
        for h in range(4):
            qn, qr = q_ref[:, 128 * h:128 * h + 128], q_ref[:, 512 + 64 * h:576 + 64 * h]
            kn, v = kv_ref[:, 256 * h:256 * h + 128], kv_ref[:, 256 * h + 128:256 * h + 256]
            s = jnp.where(valid, (_dot_nt(qn, kn) + _dot_nt(qr, kp)) * MLA_SCALE, NEG_INF)
            p = jnp.exp(s - lse_ref[:, h:h + 1])
            doh = do_ref[:, 128 * h:128 * h + 128].astype(BF16)
            dp = _dot_nt(doh, v)
            delta = jnp.sum(p * dp, axis=1, keepdims=True)
            ds = (p * (dp - delta) * MLA_SCALE).astype(BF16)
            dq_ref[:, 128 * h:128 * h + 128] = _dot(ds, kn)
            dq_ref[:, 512 + 64 * h:576 + 64 * h] = _dot(ds, kp)
            dkv_ref[:, 256 * h:256 * h + 128] += _dot_tn(ds, qn)
            dkv_ref[:, 256 * h + 128:256 * h + 256] += _dot_tn(p.astype(BF16), doh)
            dkpe_ref[:, 0:64] += _dot_tn(ds, qr)

    return _pallas(
        body, name="mla_bwd", grid=(B, L // tq),
        in_specs=[_tile_spec(tq, 768), _full_spec(L, 1024), _full_spec(L, 128), _tile_spec(tq, 4),
                  pl.BlockSpec((None, tq, 512), lambda b, i: (b, i, 0))],
        out_specs=[_tile_spec(tq, 768), _full_spec(L, 1024), _full_spec(L, 128)],
        out_shape=[_sds((B, L, 768), F32), _sds((B, L, 1024), F32), _sds((B, L, 128), F32)],
        compiler_params=_params(("parallel", "arbitrary")),
    )(q, kv, kpe, lse, do)


def _swa_window(i, tq, n):
    W = min(tq + 2 * SWA_WINDOW, n)
    lo = jnp.clip(i * tq - SWA_WINDOW, 0, n - W)
    return pl.multiple_of(lo, 128), W


def _swa_mask(i, lo, tq, W, n):
    qpos = i * tq + lax.broadcasted_iota(jnp.int32, (tq, 1), 0)
    kpos = lo + lax.broadcasted_iota(jnp.int32, (1, W), 1)
    return (jnp.abs(qpos - kpos) <= SWA_WINDOW) & (qpos < n)


def _swa_fwd(q, k, v, sink, n, tq):
    B, L, _ = q.shape

    def body(q_ref, k_ref, v_ref, sink_ref, o_ref, lse_ref):
        i = pl.program_id(1)
        lo, W = _swa_window(i, tq, n)
        mask = _swa_mask(i, lo, tq, W, n)
        kl, vl = k_ref[pl.ds(lo, W), :], v_ref[pl.ds(lo, W), :]
        kc, vc = k_ref[n:L, :], v_ref[n:L, :]
        for h in range(8):
            g = h // 4
            qh = q_ref[:, 64 * h:64 * h + 64]
            s1 = jnp.where(mask, _dot_nt(qh, kl[:, 64 * g:64 * g + 64]) * SWA_SCALE, NEG_INF)
            s2 = _dot_nt(qh, kc[:, 64 * g:64 * g + 64]) * SWA_SCALE
            sk = sink_ref[0:1, h:h + 1]
            m = jnp.maximum(jnp.maximum(jnp.max(s1, axis=1, keepdims=True), jnp.max(s2, axis=1, keepdims=True)), sk)
            e1, e2 = jnp.exp(s1 - m), jnp.exp(s2 - m)
            l = jnp.sum(e1, axis=1, keepdims=True) + jnp.sum(e2, axis=1, keepdims=True) + jnp.exp(sk - m)
            r = 1.0 / l
            o = _dot((e1 * r).astype(BF16), vl[:, 64 * g:64 * g + 64]) + _dot((e2 * r).astype(BF16), vc[:, 64 * g:64 * g + 64])
            o_ref[:, 64 * h:64 * h + 64] = o.astype(o_ref.dtype)
            lse_ref[:, h:h + 1] = m + jnp.log(l)

    return _pallas(
        body, name="swa_fwd", grid=(B, L // tq),
        in_specs=[_tile_spec(tq, 512), _full_spec(L, 128), _full_spec(L, 128), pl.BlockSpec((1, 8), lambda b, i: (0, 0))],
        out_specs=[_tile_spec(tq, 512), _tile_spec(tq, 8)],
        out_shape=[_sds((B, L, 512), BF16), _sds((B, L, 8), F32)],
        compiler_params=_params(("parallel", "arbitrary")),
    )(q, k, v, sink)


def _swa_bwd(q, k, v, sink, lse, do, n, tq):
    B, L, _ = q.shape

    def body(q_ref, k_ref, v_ref, sink_ref, lse_ref, do_ref, dq_ref, dk_ref, dv_ref, dsink_ref):
        i = pl.program_id(1)

        @pl.when(i == 0)
        def _():
            dk_ref[...] = jnp.zeros_like(dk_ref)
            dv_ref[...] = jnp.zeros_like(dv_ref)
            dsink_ref[...] = jnp.zeros_like(dsink_ref)

        lo, W = _swa_window(i, tq, n)
        mask = _swa_mask(i, lo, tq, W, n)
        kl, vl = k_ref[pl.ds(lo, W), :], v_ref[pl.ds(lo, W), :]
        kc, vc = k_ref[n:L, :], v_ref[n:L, :]
        for h in range(8):
            g = h // 4
            cols = slice(64 * g, 64 * g + 64)
            qh = q_ref[:, 64 * h:64 * h + 64]
            lse_h = lse_ref[:, h:h + 1]
            s1 = jnp.where(mask, _dot_nt(qh, kl[:, cols]) * SWA_SCALE, NEG_INF)
            s2 = _dot_nt(qh, kc[:, cols]) * SWA_SCALE
            p1, p2 = jnp.exp(s1 - lse_h), jnp.exp(s2 - lse_h)
            ps = jnp.exp(sink_ref[0:1, h:h + 1] - lse_h)
            doh = do_ref[:, 64 * h:64 * h + 64].astype(BF16)
            dp1, dp2 = _dot_nt(doh, vl[:, cols]), _dot_nt(doh, vc[:, cols])
            delta = jnp.sum(p1 * dp1, axis=1, keepdims=True) + jnp.sum(p2 * dp2, axis=1, keepdims=True)
            ds1 = (p1 * (dp1 - delta) * SWA_SCALE).astype(BF16)
            ds2 = (p2 * (dp2 - delta) * SWA_SCALE).astype(BF16)
            dq_ref[:, 64 * h:64 * h + 64] = _dot(ds1, kl[:, cols]) + _dot(ds2, kc[:, cols])
            dk_ref[pl.ds(lo, W), cols] += _dot_tn(ds1, qh)
            dk_ref[n:L, cols] += _dot_tn(ds2, qh)
            dv_ref[pl.ds(lo, W), cols] += _dot_tn(p1.astype(BF16), doh)
            dv_ref[n:L, cols] += _dot_tn(p2.astype(BF16), doh)
            dsink_ref[0:1, h:h + 1] += jnp.sum(-ps * delta, axis=0, keepdims=True)

    return _pallas(
        body, name="swa_bwd", grid=(B, L // tq),
        in_specs=[_tile_spec(tq, 512), _full_spec(L, 128), _full_spec(L, 128), pl.BlockSpec((1, 8), lambda b, i: (0, 0)),
                  _tile_spec(tq, 8), pl.BlockSpec((None, tq, 512), lambda b, i: (b, i, 1))],
        out_specs=[_tile_spec(tq, 512), _full_spec(L, 128), _full_spec(L, 128),
                   pl.BlockSpec((None, 1, 8), lambda b, i: (b, 0, 0))],
        out_shape=[_sds((B, L, 512), F32), _sds((B, L, 128), F32), _sds((B, L, 128), F32), _sds((B, 1, 8), F32)],
        compiler_params=_params(("parallel", "arbitrary")),
    )(q, k, v, sink, lse, do)


def _ret_decay(i, tq, L, n):
    qi = i * tq + lax.broadcasted_iota(jnp.int32, (tq, 1), 0)
    q_latent = qi < n
    d_x = (qi - lax.broadcasted_iota(jnp.int32, (1, n), 1)).astype(F32)
    kc = lax.broadcasted_iota(jnp.int32, (1, L - n), 1)
    d_hf = (jnp.where(q_latent, qi, qi - L) - (kc - (L - n))).astype(F32)
    d_hb = (n + kc - qi).astype(F32)
    return q_latent, d_x, d_hf, d_hb


def _ret_mask(dist, lg):
    return jnp.where(dist >= 0.0, jnp.exp(lg * jnp.maximum(dist, 0.0)), 0.0)


def _ret_weights(q_latent, d_x, d_hf, d_hb, lg_f, lg_b, with_grad):
    e_x = jnp.exp(jnp.where(d_x >= 0.0, lg_f, -lg_b) * d_x)
    e_x = jnp.where(q_latent, e_x, 0.0)
    dec_x = jnp.where(d_x == 0.0, 2.0 * e_x, e_x)
    m_f, m_b = _ret_mask(d_hf, lg_f), _ret_mask(d_hb, lg_b)
    dec_h = m_f + m_b
    if not with_grad:
        return (dec_x, dec_h), None, None
    w_f = (e_x * jnp.maximum(d_x, 0.0), m_f * jnp.maximum(d_hf, 0.0))
    w_b = (e_x * jnp.maximum(-d_x, 0.0), m_b * jnp.maximum(d_hb, 0.0))
    return (dec_x, dec_h), w_f, w_b


def _ret_fwd(q, k, v, lg, n, tq):
    B, L, _ = q.shape

    def body(q_ref, k_ref, v_ref, lg_ref, y_ref):
        dist = _ret_decay(pl.program_id(1), tq, L, n)
        for h in range(4):
            dec, _, _ = _ret_weights(*dist, lg_ref[0:1, h:h + 1], lg_ref[1:2, h:h + 1], False)
            qh = q_ref[:, 64 * h:64 * h + 64]
            y = None
            for rows, dec_r in zip((slice(0, n), slice(n, L)), dec):
                a = _dot_nt(qh, k_ref[rows, 64 * h:64 * h + 64]) * dec_r
                part = _dot(a.astype(BF16), v_ref[rows, 128 * h:128 * h + 128])
                y = part if y is None else y + part
            y_ref[:, 128 * h:128 * h + 128] = y

    return _pallas(
        body, name="ret_fwd", grid=(B, L // tq),
        in_specs=[_tile_spec(tq, 256), _full_spec(L, 256), _full_spec(L, 512), pl.BlockSpec((2, 4), lambda b, i: (0, 0))],
        out_specs=_tile_spec(tq, 512), out_shape=_sds((B, L, 512), F32),
        compiler_params=_params(("parallel", "arbitrary")),
    )(q, k, v, lg)


def _ret_bwd(q, k, v, lg, dy, n, tq):
    B, L, _ = q.shape

    def body(q_ref, k_ref, v_ref, lg_ref, dy_ref, dq_ref, dk_ref, dv_ref, dlg_ref):
        i = pl.program_id(1)

        @pl.when(i == 0)
        def _():
            dk_ref[...] = jnp.zeros_like(dk_ref)
            dv_ref[...] = jnp.zeros_like(dv_ref)
            dlg_ref[...] = jnp.zeros_like(dlg_ref)

        def total(a):
            return jnp.sum(jnp.sum(a, axis=1, keepdims=True), axis=0, keepdims=True)

        dist = _ret_decay(i, tq, L, n)
        for h in range(4):
            qc, vc = slice(64 * h, 64 * h + 64), slice(128 * h, 128 * h + 128)
            dec, w_f, w_b = _ret_weights(*dist, lg_ref[0:1, h:h + 1], lg_ref[1:2, h:h + 1], True)
            qh = q_ref[:, qc]
            dyh = dy_ref[:, vc].astype(BF16)
            dq = None
            for rows, dec_r, wf_r, wb_r in zip((slice(0, n), slice(n, L)), dec, w_f, w_b):
                kh, vh = k_ref[rows, qc], v_ref[rows, vc]
                s = _dot_nt(qh, kh)
                gr = _dot_nt(dyh, vh)
                ds = (gr * dec_r).astype(BF16)
                part = _dot(ds, kh)
                dq = part if dq is None else dq + part
                dk_ref[rows, qc] += _dot_tn(ds, qh)
                dv_ref[rows, vc] += _dot_tn((s * dec_r).astype(BF16), dyh)
                gs = gr * s
                dlg_ref[0:1, h:h + 1] += total(gs * wf_r)
                dlg_ref[1:2, h:h + 1] += total(gs * wb_r)
            dq_ref[:, qc] = dq

    return _pallas(
        body, name="ret_bwd", grid=(B, L // tq),
        in_specs=[_tile_spec(tq, 256), _full_spec(L, 256), _full_spec(L, 512), pl.BlockSpec((2, 4), lambda b, i: (0, 0)),
                  _tile_spec(tq, 512)],
        out_specs=[_tile_spec(tq, 256), _full_spec(L, 256), _full_spec(L, 512),
                   pl.BlockSpec((None, 2, 4), lambda b, i: (b, 0, 0))],
        out_shape=[_sds((B, L, 256), F32), _sds((B, L, 256), F32), _sds((B, L, 512), F32), _sds((B, 2, 4), F32)],
        compiler_params=_params(("parallel", "arbitrary")),
    )(q, k, v, lg, dy)


def _loss_head(s, target, g, n, tm):
    B, L, D = s.shape
    nx = n // tm

    def body(s_ref, t_ref, g_ref, ds_ref, dg_ref, loss_ref):
        b, i = pl.program_id(0), pl.program_id(1)

        @pl.when((b == 0) & (i == 0))
        def _():
            dg_ref[...] = jnp.zeros_like(dg_ref)
            loss_ref[...] = jnp.zeros_like(loss_ref)

        @pl.when(i < nx)
        def _():
            y, vjp = jax.vjp(_rms, s_ref[...], g_ref[...])
            err = y - t_ref[...]
            d_s, d_g = vjp(err * (1.0 / D))
            ds_ref[...] = d_s
            dg_ref[...] += d_g
            part = jnp.sum(jnp.sum(err * err, axis=1, keepdims=True), axis=0, keepdims=True) * (0.5 / D)
            loss_ref[...] += jnp.broadcast_to(part, loss_ref.shape)

        @pl.when(i >= nx)
        def _():
            ds_ref[...] = jnp.zeros_like(ds_ref)

    return _pallas(
        body, name="loss_head", grid=(B, L // tm),
        in_specs=[pl.BlockSpec((None, tm, D), lambda b, i: (b, i, 0)),
                  pl.BlockSpec((None, tm, D), lambda b, i: (b, jnp.minimum(i, nx - 1), 0)),
                  pl.BlockSpec((1, D), lambda b, i: (0, 0))],
        out_specs=[pl.BlockSpec((None, tm, D), lambda b, i: (b, i, 0)), pl.BlockSpec((1, D), lambda b, i: (0, 0)),
                   pl.BlockSpec((1, 128), lambda b, i: (0, 0))],
        out_shape=[_sds((B, L, D), F32), _sds((1, D), F32), _sds((1, 128), F32)],
        compiler_params=_params(("arbitrary", "arbitrary")),
    )(s, target, g)


def _ada_fwd(c_all, ada_w, ada_b):
    NL, D, Ns = ada_w.shape
    R = c_all.shape[0]

    def body(c_ref, w_ref, b_ref, o_ref):
        cond = jax.nn.silu(c_ref[...]).astype(BF16)
        o_ref[...] = _dot(cond, w_ref[...].astype(BF16)) + b_ref[...]

    return _pallas(
        body, name="ada_fwd", grid=(NL,),
        in_specs=[pl.BlockSpec((R, D), lambda l: (0, 0)), pl.BlockSpec((None, D, Ns), lambda l: (l, 0, 0)),
                  pl.BlockSpec((None, 1, Ns), lambda l: (l, 0, 0))],
        out_specs=pl.BlockSpec((None, R, Ns), lambda l: (l, 0, 0)), out_shape=_sds((NL, R, Ns), F32),
        compiler_params=_params(("parallel",)),
    )(c_all, ada_w, ada_b)


def _ada_bwd(c_all, ada_w, dmods):
    NL, D, Ns = ada_w.shape
    R = c_all.shape[0]

    def body(c_ref, w_ref, dm_ref, dw_ref, dc_ref):
        cond = jax.nn.silu(c_ref[...]).astype(BF16)
        dm = dm_ref[...].astype(BF16)
        dw_ref[...] = _dot_tn(cond, dm)
        dc_ref[...] = _dot_nt(dm, w_ref[...].astype(BF16))

    return _pallas(
        body, name="ada_bwd", grid=(NL,),
        in_specs=[pl.BlockSpec((R, D), lambda l: (0, 0)), pl.BlockSpec((None, D, Ns), lambda l: (l, 0, 0)),
                  pl.BlockSpec((None, R, Ns), lambda l: (l, 0, 0))],
        out_specs=[pl.BlockSpec((None, D, Ns), lambda l: (l, 0, 0)), pl.BlockSpec((None, R, D), lambda l: (l, 0, 0))],
        out_shape=[_sds((NL, D, Ns), F32), _sds((NL, R, D), F32)],
        compiler_params=_params(("parallel",)),
    )(c_all, ada_w, dmods)


def _my_index():
    return 4 * lax.axis_index("x") + 2 * lax.axis_index("y") + lax.axis_index("c")


def _peer(k):
    x, y, c = lax.axis_index("x"), lax.axis_index("y"), lax.axis_index("c")
    kx, ky, kc = (k >> 2) & 1, (k >> 1) & 1, k & 1
    px, py, pc = (x + kx) % 2, (y + ky) % 2, (c + kc) % 2
    return (px, py, pc), 4 * px + 2 * py + pc


def _all_gather(name, shards):
    na = len(shards)
    hbm = pl.BlockSpec(memory_space=pl.ANY)

    def body(*refs):
        in_refs, out_refs = refs[:na], refs[na:2 * na]
        send_sems, recv_sems, local_sems = refs[2 * na:]
        me = _my_index()
        sib_id, sib = _peer(1)
        chips = [_peer(k) for k in (4, 2, 6)]
        sib_chips = [4 * px + 2 * py + (1 - pc) for (px, py, pc), _ in chips]

        def copy(a, k, slot, to, src=None):
            dst = out_refs[a].at[slot]
            return pltpu.make_async_remote_copy(
                src_ref=dst if src is None else src, dst_ref=dst, send_sem=send_sems.at[a, k],
                recv_sem=recv_sems.at[a, k], device_id=to, device_id_type=MESH)

        first, passed, mine = [], [], []
        for a in range(na):
            cp = pltpu.make_async_copy(in_refs[a], out_refs[a].at[me], local_sems.at[a])
            cp.start()
            mine.append(cp)
            first.append(copy(a, 0, me, sib_id, src=in_refs[a]))
            first += [copy(a, 1 + j, me, pid, src=in_refs[a]) for j, (pid, _) in enumerate(chips)]
        for cp in first:
            cp.start()
        for a in range(na):
            for j, (pid, pidx) in enumerate(chips):
                copy(a, 1 + j, pidx, pid).wait_recv()
                fwd = copy(a, 4 + j, pidx, sib_id)
                fwd.start()
                passed.append(fwd)
        for a in range(na):
            copy(a, 0, sib, sib_id).wait_recv()
            for j in range(3):
                copy(a, 4 + j, sib_chips[j], sib_id).wait_recv()
        for cp in first + passed:
            cp.wait_send()
        for cp in mine:
            cp.wait()

    return _pallas(
        body, name=name, in_specs=[hbm] * na, out_specs=[hbm] * na,
        out_shape=[_sds((N_DEV,) + s.shape, s.dtype) for s in shards],
        scratch_shapes=[pltpu.SemaphoreType.DMA((na, 7)), pltpu.SemaphoreType.DMA((na, 7)),
                        pltpu.SemaphoreType.DMA((na,))],
    )(*shards)


def _all_to_all(name, parts):
    na = len(parts)
    hbm = pl.BlockSpec(memory_space=pl.ANY)

    def body(*refs):
        in_refs, out_refs = refs[:na], refs[na:2 * na]
        send_sems, recv_sems, local_sems = refs[2 * na:]
        me = _my_index()
        copies = []
        for a in range(na):
            cp = pltpu.make_async_copy(in_refs[a].at[me], out_refs[a].at[me], local_sems.at[a])
            cp.start()
            copies.append(cp)
            for k in range(1, N_DEV):
                pid, pidx = _peer(k)
                cp = pltpu.make_async_remote_copy(
                    src_ref=in_refs[a].at[pidx], dst_ref=out_refs[a].at[me], send_sem=send_sems.at[a, k - 1],
                    recv_sem=recv_sems.at[a, k - 1], device_id=pid, device_id_type=MESH)
                cp.start()
                copies.append(cp)
        for cp in copies:
            cp.wait()

    return _pallas(
        body, name=name, in_specs=[hbm] * na, out_specs=[hbm] * na,
        out_shape=[_sds(p.shape, p.dtype) for p in parts],
        scratch_shapes=[pltpu.SemaphoreType.DMA((na, 7)), pltpu.SemaphoreType.DMA((na, 7)),
                        pltpu.SemaphoreType.DMA((na,))],
    )(*parts)


_HBM = pl.BlockSpec(memory_space=pltpu.HBM)
_SEM = pl.BlockSpec(memory_space=pltpu.SEMAPHORE)
_DATAFLOW = pltpu.SideEffectType.DATAFLOW_SIDE_EFFECTING


def _exchange_start(name, srcs, slotted):
    na = len(srcs)
    lands = [lax.empty((N_DEV,) + (s.shape[1:] if slotted else s.shape), s.dtype) for s in srcs]

    def body(*refs):
        src_refs, land_refs = refs[:na], refs[na:2 * na]
        send_sems, recv_sems = refs[2 * na:3 * na], refs[3 * na:4 * na]
        token = refs[6 * na]
        me = _my_index()
        for a in range(na):
            for k in range(1, N_DEV):
                pid, pidx = _peer(k)
                pltpu.make_async_remote_copy(
                    src_ref=src_refs[a].at[pidx] if slotted else src_refs[a], dst_ref=land_refs[a].at[me],
                    send_sem=send_sems[a], recv_sem=recv_sems[a], device_id=pid, device_id_type=MESH).start()
        token[...] = jnp.zeros_like(token)

    ops = [pltpu.with_memory_space_constraint(a, pltpu.HBM) for a in list(srcs) + lands]
    outs = _pallas(
        body, name=name,
        out_shape=[pltpu.SemaphoreType.DMA(())] * (2 * na) + [pltpu.HBM(a.shape, a.dtype) for a in ops]
        + [_sds((8, 128), F32)],
        in_specs=[_HBM] * (2 * na),
        out_specs=[_SEM] * (2 * na) + [_HBM] * (2 * na) + [pl.BlockSpec(memory_space=pltpu.VMEM)],
        input_output_aliases={a: 2 * na + a for a in range(2 * na)},
        compiler_params=pltpu.CompilerParams(has_side_effects=_DATAFLOW),
    )(*ops)
    return (na, outs[:2 * na], outs[2 * na:4 * na]), outs[4 * na]


def _exchange_wait(name, handle, after):
    na, sems, thru = handle

    def body(*refs):
        land_refs = refs[na:2 * na]
        send_sems, recv_sems = refs[2 * na:3 * na], refs[3 * na:4 * na]
        me_id = (lax.axis_index("x"), lax.axis_index("y"), lax.axis_index("c"))
        for a in range(na):
            seven = land_refs[a].at[pl.ds(0, N_DEV - 1)]
            drain = pltpu.make_async_remote_copy(src_ref=seven, dst_ref=seven, send_sem=send_sems[a],
                                                 recv_sem=recv_sems[a], device_id=me_id, device_id_type=MESH)
            drain.wait_send()
            drain.wait_recv()

    outs = _pallas(
        body, name=name, out_shape=[pltpu.HBM(a.shape, a.dtype) for a in thru],
        in_specs=[_HBM] * (2 * na) + [_SEM] * (2 * na) + [pl.BlockSpec(memory_space=pl.ANY)],
        out_specs=[_HBM] * (2 * na), input_output_aliases={a: a for a in range(2 * na)},
        compiler_params=pltpu.CompilerParams(has_side_effects=_DATAFLOW),
    )(*thru, *sems, after)
    return outs[na:]


def _with_own(landed, own, slot_axis=0):
    me = _my_index()
    slot = lax.broadcasted_iota(jnp.int32, landed.shape, slot_axis)
    return jnp.where(slot == me, own, landed)


def _rope_tables(n, L, width):
    t = jnp.arange(n)
    row = (t // GRID_W).astype(F32)
    col = (t % GRID_W).astype(F32)
    n_freq = ROPE_DIM // 4
    freqs = ROPE_THETA ** (-jnp.arange(n_freq, dtype=F32) / n_freq)
    ang = jnp.concatenate([row[:, None] * freqs, col[:, None] * freqs], axis=-1)
    cos, sin = jnp.cos(ang), jnp.sin(ang)
    cos2 = jnp.concatenate([cos, cos], axis=-1)
    sin2 = jnp.concatenate([-sin, sin], axis=-1)
    cos2 = jnp.concatenate([cos2, jnp.ones((L - n, ROPE_DIM), F32)], axis=0)
    sin2 = jnp.concatenate([sin2, jnp.zeros((L - n, ROPE_DIM), F32)], axis=0)
    reps = width // ROPE_DIM
    return jnp.tile(cos2, (1, reps)), jnp.tile(sin2, (1, reps))


def _ab_perm(w):
    return jnp.concatenate([w[:, 0:256], w[:, 320:1600], w[:, 256:320], jnp.zeros((w.shape[0], 64), w.dtype)], axis=1)


def _ab_unperm(g):
    return jnp.concatenate([g[:, 0:256], g[:, 1536:1600], g[:, 256:1536]], axis=1)


def _wq_perm(w):
    return jnp.concatenate([w[:, 192 * h:192 * h + 128] for h in range(4)]
                           + [w[:, 192 * h + 128:192 * h + 192] for h in range(4)], axis=1)


def _wq_unperm(g):
    return jnp.concatenate([g[:, sl] for h in range(4)
                            for sl in (slice(128 * h, 128 * h + 128), slice(512 + 64 * h, 576 + 64 * h))], axis=1)


def _flat(a):
    return a.reshape(-1, a.shape[-1])


def _layer_weights(full, p):
    NL, NE, NO = len(full["ffn_in"]), len(full["ab_in"]), len(full["cd_in"])
    groups = range(4)

    def each(f, mats):
        return [None if w is None else f(w) for w in mats]

    return dict(
        norm_mix=[p["norm_mix"][l][None] for l in range(NL)], norm_ffn=[p["norm_ffn"][l][None] for l in range(NL)],
        norm_final=p["norm_final"][None],
        ffn_in=list(full["ffn_in"]), ffn_out=list(full["ffn_out"]),
        ab_in=each(_ab_perm, full["ab_in"]), ab_out=list(full["ab_out"]),
        wq=each(_wq_perm, full["mla_wq_b"]), wkv=list(full["mla_wkv_b"]),
        kv_norm=[p["mla_kv_norm"][j][None] for j in range(NE)], q_norm=[p["mla_q_norm"][j][None] for j in range(NE)],
        v_norm=[[p["cmlp_v_norm"][j][None, 128 * g:128 * g + 128] for g in groups] for j in range(NE)],
        ws=[[p["cmlp_ws"][j, g] for g in groups] for j in range(NE)],
        bs=[[p["cmlp_bs"][j, g][:, None] for g in groups] for j in range(NE)],
        cd_in=list(full["cd_in"]), cd_out=list(full["cd_out"]),
        lg=[jnp.stack([jax.nn.log_sigmoid(p["ret_decay_fwd"][j]), jax.nn.log_sigmoid(p["ret_decay_bwd"][j])])
            for j in range(NO)],
        sink=[p["swa_sink"][j][None] for j in range(NO)],
        ret_norm=[[p["ret_norm"][j][None, 128 * g:128 * g + 128] for g in groups] for j in range(NO)],
    )


def _local_step(x, ctx, target, mods, W, later_weights=None, early_grads=None):
    B, n, D = x.shape
    m = ctx.shape[1]
    L = n + m
    NL = mods.shape[0]
    tm = min(256, m)
    tq = min(256, m)
    cos512, sin512 = _rope_tables(n, L, 512)
    s = jnp.concatenate([x, ctx], axis=1)
    saved = []

    def rows(name, f, arrays, pieces, samp, samp_pieces, glob, out_arrays, out_pieces, tile=tm):
        return _Rows(name, f, B, L, n, tile, arrays, pieces, samp, samp_pieces, glob, out_arrays, out_pieces)

    def full(width, start=0):
        return [(0, start, width)]

    for l in range(NL):
        j = l // 2
        even = l % 2 == 0
        md = mods[l]
        if l == 1 and later_weights is not None:
            W = later_weights(1, s)
        r = {"s0": s}
        pre1 = rows(f"pre_mix{l}", _f_pre, [s], full(D), [md], [(0, 0), (0, 1)], [W["norm_mix"][l]], [(D, BF16)], full(D))
        (xn,) = pre1.fwd()
        r["pre1"], r["xn"] = pre1, xn
        if even:
            z = _mm(f"ab_in{l}", _flat(xn), W["ab_in"][j]).reshape(B, L, 1664)
            pieces = [(0, 0, 256), (0, 256, 256), (0, 1536, 128)]
            pieces += [(0, 512 + 128 * g, 128) for g in range(4)] + [(0, 1024 + 128 * g, 128) for g in range(4)]
            pieces += [(1, 0, 128), (2, 0, 128)]
            glob = [W["kv_norm"][j], W["q_norm"][j]] + W["v_norm"][j] + W["ws"][j] + W["bs"][j]
            abp = rows(f"ab_pre{l}", _f_ab_pre, [z, cos512, sin512], pieces, [], [], glob,
                       [(256, BF16), (256, BF16), (128, BF16), (512, BF16)],
                       [(0, 0, 256), (1, 0, 256), (2, 0, 128)] + [(3, 128 * g, 128) for g in range(4)], tile=128)
            kvn, qn, kpe, cm = abp.fwd()
            kv = _mm(f"wkv{l}", _flat(kvn), W["wkv"][j], out_dtype=BF16).reshape(B, L, 1024)
            q0 = _mm(f"wq{l}", _flat(qn), W["wq"][j]).reshape(B, L, 768)
            qrp = rows(f"q_rope{l}", _f_q_rope, [q0, cos512, sin512], [(0, 0, 512), (0, 512, 256), (1, 0, 256), (2, 0, 256)],
                       [], [], [], [(768, BF16)], [(0, 0, 512), (0, 512, 256)])
            (q,) = qrp.fwd()
            o, lse = _mla_fwd(q, kv, kpe, n, tq)
            y = _mm(f"ab_out_a{l}", _flat(o), W["ab_out"][j][:512])
            y = _mm(f"ab_out_b{l}", _flat(cm), W["ab_out"][j][512:], add=y).reshape(B, L, D)
            r.update(z=z, abp=abp, kvn=kvn, qn=qn, kpe=kpe, cm=cm, kv=kv, qrp=qrp, q=q, o=o, lse=lse)
        else:
            z = _mm(f"cd_in{l}", _flat(xn), W["cd_in"][j]).reshape(B, L, 2304)
            pieces = [(0, 0, 256), (0, 256, 512), (0, 768, 128), (0, 896, 128), (0, 1024, 256), (0, 1280, 512), (0, 1792, 512)]
            pieces += [(1, 0, 256), (2, 0, 256), (1, 0, 128), (2, 0, 128), (1, 0, 512), (2, 0, 512)]
            cdp = rows(f"cd_pre{l}", _f_cd_pre, [z, cos512, sin512], pieces, [], [], [],
                       [(256, BF16), (256, BF16), (512, BF16), (512, BF16), (128, BF16), (128, BF16), (512, F32)],
                       [(k_, 0, w_) for k_, w_ in enumerate((256, 256, 512, 512, 128, 128, 512))])
            rq, rk, rv, sq, sk, sv, rg = cdp.fwd()
            yret = _ret_fwd(rq, rk, rv, W["lg"][j], n, tq)
            osw, lse = _swa_fwd(sq, sk, sv, W["sink"][j], n, tq)
            mrg = rows(f"cd_merge{l}", _f_cd_merge, [yret, rg],
                       [(0, 128 * g, 128) for g in range(4)] + [(1, 128 * g, 128) for g in range(4)], [], [],
                       W["ret_norm"][j], [(512, BF16)], [(0, 128 * g, 128) for g in range(4)])
            (yr,) = mrg.fwd()
            y = _mm(f"cd_out_a{l}", _flat(yr), W["cd_out"][j][:512])
            y = _mm(f"cd_out_b{l}", _flat(osw), W["cd_out"][j][512:], add=y).reshape(B, L, D)
            r.update(z=z, cdp=cdp, rq=rq, rk=rk, rv=rv, sq=sq, sk=sk, sv=sv, rg=rg, yret=yret, osw=osw, lse=lse,
                     mrg=mrg, yr=yr)
        res1 = rows(f"res_mix{l}", _f_res, [s, y], [(0, 0, D), (1, 0, D)], [md], [(0, 2)], [], [(D, F32)], full(D))
        (s1,) = res1.fwd()
        pre2 = rows(f"pre_ffn{l}", _f_pre, [s1], full(D), [md], [(0, 3), (0, 4)], [W["norm_ffn"][l]], [(D, BF16)], full(D))
        (xn2,) = pre2.fwd()
        if l == 0 and later_weights is not None:
            W = later_weights(0, s1)
        y2 = _ffn_fwd(f"ffn{l}", _flat(xn2), W["ffn_in"][l], W["ffn_out"][l]).reshape(B, L, D)
        res2 = rows(f"res_ffn{l}", _f_res, [s1, y2], [(0, 0, D), (1, 0, D)], [md], [(0, 5)], [], [(D, F32)], full(D))
        (s,) = res2.fwd()
        r.update(res1=res1, pre2=pre2, xn2=xn2, res2=res2)
        saved.append(r)

    ds, d_norm_final, loss = _loss_head(s, target, W["norm_final"], n, tm)

    G = {k: [None] * len(v) for k, v in W.items() if isinstance(v, list)}
    G["norm_final"] = d_norm_final
    dmods = [None] * NL
    for l in reversed(range(NL)):
        j = l // 2
        even = l % 2 == 0
        r = saved[l]
        dm = [None] * 6
        smp = None
        if l == 0 and early_grads is not None:
            smp = [mods[0] + early_grads(0, G)]
        (ds1, dy2), (dm[5],), _ = r["res2"].bwd([ds], {0: F32, 1: BF16}, samp=smp)
        dy2f, xn2f = _flat(dy2), _flat(r["xn2"])
        w_in, Fh = W["ffn_in"][l], W["ffn_out"][l].shape[0]
        hid, da, db = _ffn_mid_bwd(f"ffn_mid{l}", xn2f, dy2f, w_in, W["ffn_out"][l])
        G["ffn_out"][l] = _mm(f"g_ffn_out{l}", hid, dy2f, ta=True, out_dtype=BF16)
        G["ffn_in"][l] = jnp.concatenate([_mm(f"g_ffn_in_a{l}", xn2f, da, ta=True, out_dtype=BF16),
                                          _mm(f"g_ffn_in_b{l}", xn2f, db, ta=True, out_dtype=BF16)], axis=1)
        dxn2 = _mm(f"d_xn2_a{l}", da, w_in[:, :Fh], tb=True)
        dxn2 = _mm(f"d_xn2_b{l}", db, w_in[:, Fh:], tb=True, add=dxn2).reshape(B, L, D)
        (ds1,), (dm[3], dm[4]), (G["norm_ffn"][l],) = r["pre2"].bwd([dxn2], {0: F32}, grad_glob=(0,), add={0: ds1},
                                                                       samp=smp)
        if l == 0 and early_grads is not None:
            smp = [mods[0] + early_grads(1, G)]
        (ds0, dy), (dm[2],), _ = r["res1"].bwd([ds1], {0: F32, 1: BF16}, samp=smp)
        dyf = _flat(dy)
        if even:
            w_out = W["ab_out"][j]
            dcat = _mm(f"d_cat{l}", dyf, w_out, tb=True).reshape(B, L, -1)
            G["ab_out"][l // 2] = jnp.concatenate(
                [_mm(f"g_ab_out_a{l}", _flat(r["o"]), dyf, ta=True, out_dtype=BF16),
                 _mm(f"g_ab_out_b{l}", _flat(r["cm"]), dyf, ta=True, out_dtype=BF16)], axis=0)
            dq, dkv, dkpe = _mla_bwd(r["q"], r["kv"], r["kpe"], r["lse"], dcat, n, tq)
            (dq0,), _, _ = r["qrp"].bwd([dq], {0: BF16})
            dq0f, dkvf = _flat(dq0), _flat(dkv)
            G["wq"][j] = _mm(f"g_wq{l}", _flat(r["qn"]), dq0f, ta=True, out_dtype=BF16)
            G["wkv"][j] = _mm(f"g_wkv{l}", _flat(r["kvn"]), dkvf, ta=True, out_dtype=BF16)
            dqn = _mm(f"d_qn{l}", dq0f, W["wq"][j], tb=True).reshape(B, L, 256)
            dkvn = _mm(f"d_kvn{l}", dkvf, W["wkv"][j], tb=True).reshape(B, L, 256)
            (dz,), _, gg = r["abp"].bwd([dkvn, dqn, dkpe, (dcat, 512)], {0: BF16}, grad_glob=tuple(range(14)))
            G["kv_norm"][j], G["q_norm"][j] = gg[0], gg[1]
            G["v_norm"][j], G["ws"][j], G["bs"][j] = list(gg[2:6]), list(gg[6:10]), list(gg[10:14])
            w_in, key = W["ab_in"][j], "ab_in"
        else:
            w_out = W["cd_out"][j]
            dcat = _mm(f"d_cat{l}", dyf, w_out, tb=True).reshape(B, L, -1)
            G["cd_out"][j] = jnp.concatenate(
                [_mm(f"g_cd_out_a{l}", _flat(r["yr"]), dyf, ta=True, out_dtype=BF16),
                 _mm(f"g_cd_out_b{l}", _flat(r["osw"]), dyf, ta=True, out_dtype=BF16)], axis=0)
            (dyret, drg), _, gg = r["mrg"].bwd([(dcat, 0)], {0: F32, 1: F32}, grad_glob=(0, 1, 2, 3))
            G["ret_norm"][j] = list(gg)
            drq, drk, drv, dlg = _ret_bwd(r["rq"], r["rk"], r["rv"], W["lg"][j], dyret, n, tq)
            dsq, dsk, dsv, dsink = _swa_bwd(r["sq"], r["sk"], r["sv"], W["sink"][j], r["lse"], dcat, n, tq)
            G["lg"][j], G["sink"][j] = dlg, dsink
            (dz,), _, _ = r["cdp"].bwd([drq, drk, drv, dsq, dsk, dsv, drg], {0: BF16})
            w_in, key = W["cd_in"][j], "cd_in"
        dzf = _flat(dz)
        G[key][j] = _mm(f"g_{key}{l}", _flat(r["xn"]), dzf, ta=True, out_dtype=BF16)
        dxn = _mm(f"d_xn{l}", dzf, w_in, tb=True).reshape(B, L, D)
        (ds,), (dm[0], dm[1]), (G["norm_mix"][l],) = r["pre1"].bwd([dxn], {0: F32}, grad_glob=(0,), add={0: ds0},
                                                                       samp=smp)
        dmods[l] = jnp.concatenate(dm, axis=2)
    return loss, ds[:, :n], jnp.stack(dmods), G


def kernel(x, c, ctx, c_ctx, ada_w, ada_b, norm_mix, norm_ffn, norm_final, ffn_in, ffn_out, ab_in, ab_out, mla_q_norm, mla_kv_norm, mla_wq_b, mla_wkv_b, cmlp_v_norm, cmlp_ws, cmlp_bs, cd_in, cd_out, ret_decay_fwd, ret_decay_bwd, ret_norm, swa_sink, loss_target, m_c_ctx, m_ada_w, m_ada_b, m_norm_mix, m_norm_ffn, m_norm_final, m_ffn_in, m_ffn_out, m_ab_in, m_ab_out, m_mla_q_norm, m_mla_kv_norm, m_mla_wq_b, m_mla_wkv_b, m_cmlp_v_norm, m_cmlp_ws, m_cmlp_bs, m_cd_in, m_cd_out, m_ret_decay_fwd, m_ret_decay_bwd, m_ret_norm, m_swa_sink, v_c_ctx, v_ada_w, v_ada_b, v_norm_mix, v_norm_ffn, v_norm_final, v_ffn_in, v_ffn_out, v_ab_in, v_ab_out, v_mla_q_norm, v_mla_kv_norm, v_mla_wq_b, v_mla_wkv_b, v_cmlp_v_norm, v_cmlp_ws, v_cmlp_bs, v_cd_in, v_cd_out, v_ret_decay_fwd, v_ret_decay_bwd, v_ret_norm, v_swa_sink):
    B, n, D = x.shape
    NL = ada_w.shape[0]
    NE, NO = ab_in.shape[0], cd_in.shape[0]
    me = _my_index()
    weights = dict(c_ctx=c_ctx, ada_w=ada_w, ada_b=ada_b, norm_mix=norm_mix, norm_ffn=norm_ffn, norm_final=norm_final,
                   ffn_in=ffn_in, ffn_out=ffn_out, ab_in=ab_in, ab_out=ab_out, mla_q_norm=mla_q_norm,
                   mla_kv_norm=mla_kv_norm, mla_wq_b=mla_wq_b, mla_wkv_b=mla_wkv_b, cmlp_v_norm=cmlp_v_norm,
                   cmlp_ws=cmlp_ws, cmlp_bs=cmlp_bs, cd_in=cd_in, cd_out=cd_out, ret_decay_fwd=ret_decay_fwd,
                   ret_decay_bwd=ret_decay_bwd, ret_norm=ret_norm, swa_sink=swa_sink)
    moments_m = dict(c_ctx=m_c_ctx, ada_w=m_ada_w, ada_b=m_ada_b, norm_mix=m_norm_mix, norm_ffn=m_norm_ffn,
                     norm_final=m_norm_final, ffn_in=m_ffn_in, ffn_out=m_ffn_out, ab_in=m_ab_in, ab_out=m_ab_out,
                     mla_q_norm=m_mla_q_norm, mla_kv_norm=m_mla_kv_norm, mla_wq_b=m_mla_wq_b, mla_wkv_b=m_mla_wkv_b,
                     cmlp_v_norm=m_cmlp_v_norm, cmlp_ws=m_cmlp_ws, cmlp_bs=m_cmlp_bs, cd_in=m_cd_in, cd_out=m_cd_out,
                     ret_decay_fwd=m_ret_decay_fwd, ret_decay_bwd=m_ret_decay_bwd, ret_norm=m_ret_norm,
                     swa_sink=m_swa_sink)
    moments_v = dict(c_ctx=v_c_ctx, ada_w=v_ada_w, ada_b=v_ada_b, norm_mix=v_norm_mix, norm_ffn=v_norm_ffn,
                     norm_final=v_norm_final, ffn_in=v_ffn_in, ffn_out=v_ffn_out, ab_in=v_ab_in, ab_out=v_ab_out,
                     mla_q_norm=v_mla_q_norm, mla_kv_norm=v_mla_kv_norm, mla_wq_b=v_mla_wq_b, mla_wkv_b=v_mla_wkv_b,
                     cmlp_v_norm=v_cmlp_v_norm, cmlp_ws=v_cmlp_ws, cmlp_bs=v_cmlp_bs, cd_in=v_cd_in, cd_out=v_cd_out,
                     ret_decay_fwd=v_ret_decay_fwd, ret_decay_bwd=v_ret_decay_bwd, ret_norm=v_ret_norm,
                     swa_sink=v_swa_sink)
    order = list(weights)

    Ns = ada_w.shape[2]
    (c_g,) = _all_gather("gather_c", [c])
    R = N_DEV * B + 8
    c_all = jnp.concatenate([c_g.reshape(N_DEV * B, D), jnp.broadcast_to(c_ctx[None], (8, D))], axis=0)
    ada_b_mine = lax.dynamic_slice_in_dim(ada_b, me * Ns, Ns, axis=1)[:, None, :]
    mods_shard = _ada_fwd(c_all, ada_w, ada_b_mine)
    (mods_g,) = _all_gather("gather_mods", [mods_shard])
    mods_full = jnp.transpose(mods_g, (1, 2, 0, 3)).reshape(NL, R, 6, D)
    mx = lax.dynamic_slice_in_dim(mods_full, me * B, B, axis=1)
    mh = jnp.broadcast_to(mods_full[:, N_DEV * B][:, None], (NL, B, 6, D))
    mods = jnp.stack([mx, mh], axis=2)

    big = ["ffn_in", "ffn_out", "ab_in", "ab_out", "cd_in", "cd_out", "mla_wq_b", "mla_wkv_b"]
    col_sharded = {"ffn_in", "ab_in", "cd_in", "mla_wq_b", "mla_wkv_b"}
    shards = {k: _to_bf16("cast_" + k, weights[k]) for k in big}
    first = {k: 0 if k.startswith("cd_") else 1 for k in big}
    a_keys = [k for k in big if first[k] and not k.startswith("ffn_")]
    b_keys = ["ffn_in", "ffn_out"]
    early = _all_gather("gather_wA", [shards[k][:1] for k in a_keys])
    wb_handle, wb_token = _exchange_start("gather_wB_start", [shards[k][:1] for k in b_keys], slotted=False)
    wc_handle, wc_token = _exchange_start("gather_wC_start", [shards[k][first[k]:] for k in big], slotted=False)
    mods = mods + (wb_token[0, 0] + wc_token[0, 0])

    def unshard(k, g):
        if k in col_sharded:
            f = jnp.transpose(g, (1, 2, 0, 3)).reshape(g.shape[1], g.shape[2], -1)
        else:
            f = jnp.transpose(g, (1, 0, 2, 3)).reshape(g.shape[1], -1, g.shape[3])
        return [f[i] for i in range(f.shape[0])]

    (rn_g,) = _all_gather("gather_ret_norm", [ret_norm])
    rn_full = jnp.transpose(rn_g, (1, 0, 2)).reshape(NO, -1)
    small_p = dict(weights, ret_norm=rn_full)
    full0 = {k: [None] * weights[k].shape[0] for k in big}
    for k, g in zip(a_keys, early):
        full0[k][:1] = unshard(k, g)

    def later_weights(stage, newest):
        if stage == 0:
            landed = _exchange_wait("gather_wB_wait", wb_handle, newest)
            for k, land in zip(b_keys, landed):
                full0[k][:1] = unshard(k, _with_own(land, shards[k][:1]))
        else:
            landed = _exchange_wait("gather_wC_wait", wc_handle, newest)
            for k, land in zip(big, landed):
                full0[k][first[k]:] = unshard(k, _with_own(land, shards[k][first[k]:]))
        return _layer_weights(full0, small_p)

    def to_slots(k, gl):
        g = jnp.stack(gl)
        if k in col_sharded:
            return jnp.transpose(g.reshape(g.shape[0], g.shape[1], N_DEV, -1), (2, 0, 1, 3))
        return jnp.transpose(g.reshape(g.shape[0], N_DEV, -1, g.shape[2]), (1, 0, 2, 3))

    def big_grads(G):
        return dict(ffn_in=G["ffn_in"], ffn_out=G["ffn_out"], ab_in=[g if g is None else _ab_unperm(g) for g in G["ab_in"]],
                    ab_out=G["ab_out"], cd_in=G["cd_in"], cd_out=G["cd_out"],
                    mla_wq_b=[g if g is None else _wq_unperm(g) for g in G["wq"]], mla_wkv_b=G["wkv"])

    sent = {}

    def early_grads(stage, G):
        parts = big_grads(G)
        if stage == 0:
            srcs = [to_slots(k, parts[k][first[k]:]) for k in big]
            handle, g_token = _exchange_start("scatter_gC_start", srcs, slotted=True)
        else:
            srcs = [to_slots(k, parts[k][:1]) for k in b_keys]
            handle, g_token = _exchange_start("scatter_gB_start", srcs, slotted=True)
        sent[stage] = (handle, srcs)
        return g_token[0, 0]

    loss_part, grad_x, dmods, G = _local_step(x, ctx, loss_target, mods, _layer_weights(full0, small_p),
                                              later_weights, early_grads)
    loss = lax.psum(loss_part[0, 0], ("x", "y", "c"))

    dmx = dmods[:, :, 0].reshape(NL, B, 6 * D)
    dmh = jnp.sum(dmods[:, :, 1], axis=1).reshape(NL, 1, 6 * D)
    (dm_g,) = _all_gather("gather_dmods", [jnp.concatenate([dmx, dmh], axis=1)])
    dmx_all = jnp.transpose(dm_g[:, :, :B], (1, 0, 2, 3)).reshape(NL, N_DEV * B, 6 * D)
    dmh_all = jnp.sum(dm_g[:, :, B], axis=0)
    dm_rows = jnp.concatenate([dmx_all, dmh_all[:, None], jnp.zeros((NL, 7, 6 * D), F32)], axis=1)
    g_ada_b = jnp.sum(dm_rows, axis=1)
    dm_mine = lax.dynamic_slice_in_dim(dm_rows, me * Ns, Ns, axis=2)
    g_ada_w, dcond = _ada_bwd(c_all, ada_w, dm_mine)
    sg = jax.nn.sigmoid(c_ctx)
    d_c_ctx_part = jnp.sum(dcond[:, N_DEV * B], axis=0) * (sg * (1.0 + c_ctx * (1.0 - sg)))

    def cat(parts):
        return jnp.concatenate([p.reshape(-1) for p in parts])

    dlg = jnp.stack([jnp.sum(G["lg"][j], axis=0) for j in range(NO)])
    sig_f, sig_b = jax.nn.sigmoid(-ret_decay_fwd), jax.nn.sigmoid(-ret_decay_bwd)
    small = dict(
        c_ctx=d_c_ctx_part,
        norm_mix=cat(G["norm_mix"]), norm_ffn=cat(G["norm_ffn"]), norm_final=G["norm_final"].reshape(-1),
        mla_q_norm=cat(G["q_norm"]), mla_kv_norm=cat(G["kv_norm"]),
        cmlp_v_norm=cat([cat(G["v_norm"][j]) for j in range(NE)]),
        cmlp_ws=cat([jnp.stack(G["ws"][j]) for j in range(NE)]),
        cmlp_bs=cat([jnp.stack([b_[:, 0] for b_ in G["bs"][j]]) for j in range(NE)]),
        ret_decay_fwd=(dlg[:, 0] * sig_f).reshape(-1), ret_decay_bwd=(dlg[:, 1] * sig_b).reshape(-1),
        ret_norm=cat([cat(G["ret_norm"][j]) for j in range(NO)]),
        swa_sink=cat([jnp.sum(G["sink"][j], axis=0) for j in range(NO)]),
    )
    small_keys = list(small)
    sizes = [small[k].shape[0] for k in small_keys]
    total = sum(sizes)
    padded = -(-total // 2048) * 2048
    packed = jnp.concatenate([small[k] for k in small_keys] + [jnp.zeros((padded - total,), F32)]).reshape(-1, 128)
    (small_g,) = _all_gather("gather_small", [packed])

    parts = big_grads(G)
    landed0 = dict(zip(a_keys, _all_to_all("scatter_gA", [to_slots(k, parts[k][:1]) for k in a_keys])))
    for k, land, src in zip(b_keys, _exchange_wait("scatter_gB_wait", sent[1][0], grad_x), sent[1][1]):
        landed0[k] = _with_own(land, src)
    landed = []
    for k, land, src in zip(big, _exchange_wait("scatter_gC_wait", sent[0][0], grad_x), sent[0][1]):
        rest = _with_own(land, src)
        landed.append(jnp.concatenate([landed0[k], rest], axis=1) if first[k] else rest)

    grads, deltas, new_m, new_v = {}, {}, {}, {}
    for k, land in zip(big, landed):
        grads[k], deltas[k], new_m[k], new_v[k] = _adamw_from_slots("adamw_" + k, weights[k], moments_m[k], moments_v[k], land)
    deltas["ada_w"], new_m["ada_w"], new_v["ada_w"] = [
        o.reshape(ada_w.shape) for o in _adamw("adamw_ada_w", _flat(ada_w), _flat(g_ada_w), _flat(m_ada_w), _flat(v_ada_w))]
    grads["ada_w"] = g_ada_w

    def packed_of(src, fill):
        vals = [src[k].reshape(-1) if k != "ret_norm" else jnp.full((sizes[i],), fill, F32)
                for i, k in enumerate(small_keys)]
        return jnp.concatenate(vals + [jnp.full((padded - total,), fill, F32)]).reshape(-1, 128)

    w_p, m_p, v_p = packed_of(weights, 0.0), packed_of(moments_m, 0.0), packed_of(moments_v, 1.0)

    def f_small(w_, m_, v_, land_):
        g = _sum_slots(land_)
        return (g,) + _adamw_math(w_, g, m_, v_)

    g_p, d_p, nm_p, nv_p = _ew("adamw_small", f_small, [w_p, m_p, v_p, small_g], [F32] * 4)
    offs = np.cumsum([0] + sizes)
    for i, k in enumerate(small_keys):
        sl = slice(int(offs[i]), int(offs[i + 1]))
        if k == "ret_norm":
            g_full = g_p.reshape(-1)[sl].reshape(NO, -1)
            g_mine = lax.dynamic_slice_in_dim(g_full, me * ret_norm.shape[1], ret_norm.shape[1], axis=1)
            d_, m_, v_ = _adamw("adamw_ret_norm", *[jnp.pad(a, ((0, 8 - NO), (0, 128 - a.shape[1])), constant_values=cv)
                                                     for a, cv in ((ret_norm, 0.0), (g_mine, 0.0), (m_ret_norm, 0.0), (v_ret_norm, 1.0))])
            grads[k] = g_mine
            deltas[k], new_m[k], new_v[k] = [a[:NO, :ret_norm.shape[1]] for a in (d_, m_, v_)]
        else:
            shp = weights[k].shape
            grads[k], deltas[k], new_m[k], new_v[k] = [a.reshape(-1)[sl].reshape(shp) for a in (g_p, d_p, nm_p, nv_p)]
    pad_b = lambda a, cv=0.0: jnp.pad(a, ((0, 8 - NL), (0, 0)), constant_values=cv)
    d_, m_, v_ = _adamw("adamw_ada_b", pad_b(ada_b), pad_b(g_ada_b), pad_b(m_ada_b), pad_b(v_ada_b, 1.0))
    grads["ada_b"] = g_ada_b
    deltas["ada_b"], new_m["ada_b"], new_v["ada_b"] = d_[:NL], m_[:NL], v_[:NL]

    return (loss, grad_x, *[grads[k] for k in order], *[deltas[k] for k in order],
            *[new_m[k] for k in order], *[new_v[k] for k in order])
```

```python
import functools

import numpy as np
import jax
import jax.numpy as jnp
from jax import lax
from jax.experimental import pallas as pl
from jax.experimental.pallas import tpu as pltpu

F32 = jnp.float32
BF16 = jnp.bfloat16
EPS = 1e-6
NEG_INF = -1e30
N_DEV = 8
GRID_W = 64
ROPE_THETA = 10000.0
ROPE_DIM = 64
SWA_WINDOW = 128
MLA_SCALE = (128 + 64) ** -0.5
SWA_SCALE = 64 ** -0.5
RET_K_SCALE = 64 ** -0.5
ADAM_LR, ADAM_B1, ADAM_B2, ADAM_EPS, ADAM_WD, ADAM_STEP = 0.001, 0.9, 0.999, 1e-08, 0.01, 10
V7X_VMEM_LIMIT = 56 * 1024 * 1024
MESH = pl.DeviceIdType.MESH


def _pallas(body, **kw):
    return pl.pallas_call(body, **kw)


def _params(sem=None):
    return pltpu.CompilerParams(dimension_semantics=sem, vmem_limit_bytes=V7X_VMEM_LIMIT)


def _tile(n, cap, align):
    best = None
    for t in range(align, min(n, cap) + 1, align):
        if n % t == 0:
            best = t
    return n if best is None else best


def _sds(shape, dtype):
    return jax.ShapeDtypeStruct(tuple(shape), dtype)


def _ew(name, f, ins, out_dtypes, cap_elems=131072):
    R, C = ins[0].shape[-2:]
    tr = _tile(R, max(16, cap_elems // C), 16)
    n_in = len(ins)

    def spec(a):
        if a.ndim == 2:
            return pl.BlockSpec((tr, C), lambda i: (i, 0))
        return pl.BlockSpec((a.shape[0], tr, C), lambda i: (0, i, 0))

    def body(*refs):
        outs = f(*[r[...] for r in refs[:n_in]])
        for r, o in zip(refs[n_in:], outs):
            r[...] = o.astype(r.dtype)

    return _pallas(
        body, name=name, grid=(R // tr,), in_specs=[spec(a) for a in ins],
        out_specs=[pl.BlockSpec((tr, C), lambda i: (i, 0)) for _ in out_dtypes],
        out_shape=[_sds((R, C), d) for d in out_dtypes], compiler_params=_params(("parallel",)),
    )(*ins)


def _to_bf16(name, w):
    w2 = w.reshape(-1, w.shape[-1])
    return _ew(name, lambda v: (v,), [w2], [BF16])[0].reshape(w.shape)


def _adamw_math(w, g, m, v):
    m = ADAM_B1 * m + (1.0 - ADAM_B1) * g
    v = ADAM_B2 * v + (1.0 - ADAM_B2) * (g * g)
    m_hat = m / (1.0 - ADAM_B1 ** ADAM_STEP)
    v_hat = v / (1.0 - ADAM_B2 ** ADAM_STEP)
    delta = -ADAM_LR * (m_hat / (jnp.sqrt(v_hat) + ADAM_EPS) + ADAM_WD * w)
    return delta, m, v


def _sum_slots(land):
    g = land[0].astype(F32)
    for s in range(1, land.shape[0]):
        g = g + land[s].astype(F32)
    return g


def _adamw_from_slots(name, w, m, v, land):
    shp = w.shape
    C = shp[-1]

    def f(w_, m_, v_, land_):
        g = _sum_slots(land_)
        return (g,) + _adamw_math(w_, g, m_, v_)

    outs = _ew(name, f, [w.reshape(-1, C), m.reshape(-1, C), v.reshape(-1, C), land.reshape(N_DEV, -1, C)],
               [F32] * 4, cap_elems=65536)
    return [o.reshape(shp) for o in outs]


def _adamw(name, w, g, m, v):
    outs = _ew(name, lambda w_, g_, m_, v_: _adamw_math(w_, g_, m_, v_), [w, g, m, v], [F32] * 3)
    return outs


def _mm(name, a, b, ta=False, tb=False, out_dtype=F32, add=None):
    M, K = (a.shape[1], a.shape[0]) if ta else a.shape
    N = b.shape[0] if tb else b.shape[1]
    tm = _tile(M, 1024, 128)
    tn = _tile(N, 1024, 128)
    if tn < 256 and N <= 2432:
        tn = N
    tk = _tile(K, 1792, 128)
    nk = K // tk
    a_spec = pl.BlockSpec((tk, tm), lambda i, j, k: (k, i)) if ta else pl.BlockSpec((tm, tk), lambda i, j, k: (i, k))
    b_spec = pl.BlockSpec((tn, tk), lambda i, j, k: (j, k)) if tb else pl.BlockSpec((tk, tn), lambda i, j, k: (k, j))
    o_spec = pl.BlockSpec((tm, tn), lambda i, j, k: (i, j))
    dims = (((0 if ta else 1,), (1 if tb else 0,)), ((), ()))
    has_add = add is not None

    def product(a_ref, b_ref):
        return lax.dot_general(a_ref[...].astype(BF16), b_ref[...].astype(BF16), dims, preferred_element_type=F32)

    def body_single(*refs):
        acc = product(refs[0], refs[1])
        if has_add:
            acc = acc + refs[2][...]
        refs[-1][...] = acc.astype(refs[-1].dtype)

    def body(*refs):
        a_ref, b_ref = refs[0], refs[1]
        add_ref = refs[2] if has_add else None
        o_ref, acc = refs[-2], refs[-1]
        k = pl.program_id(2)

        @pl.when(k == 0)
        def _():
            acc[...] = add_ref[...] if has_add else jnp.zeros_like(acc)

        acc[...] += product(a_ref, b_ref)

        @pl.when(k == nk - 1)
        def _():
            o_ref[...] = acc[...].astype(o_ref.dtype)

    ins = [a, b] + ([add] if has_add else [])
    specs = [a_spec, b_spec] + ([o_spec] if has_add else [])
    return _pallas(
        body_single if nk == 1 else body, name=name, grid=(M // tm, N // tn, nk), in_specs=specs, out_specs=o_spec,
        out_shape=_sds((M, N), out_dtype), scratch_shapes=[] if nk == 1 else [pltpu.VMEM((tm, tn), F32)],
        compiler_params=_params(("parallel", "parallel", "arbitrary")),
    )(*ins)


FFN_TILE = 256


def _ffn_interleave(w):
    D, F2 = w.shape
    return jnp.transpose(w.reshape(D, 2, F2 // (2 * FFN_TILE), FFN_TILE), (0, 2, 1, 3)).reshape(D, F2)


def _ffn_deinterleave(w):
    D, F2 = w.shape
    return jnp.transpose(w.reshape(D, F2 // (2 * FFN_TILE), 2, FFN_TILE), (0, 2, 1, 3)).reshape(D, F2)


def _ffn_specs(M, D, F):
    tm = _tile(M, 1024, 128)
    x_spec = pl.BlockSpec((tm, D), lambda i, j: (i, 0))
    wi_spec = pl.BlockSpec((D, 2 * FFN_TILE), lambda i, j: (0, j))
    wo_spec = pl.BlockSpec((FFN_TILE, D), lambda i, j: (j, 0))
    return tm, F // FFN_TILE, x_spec, wi_spec, wo_spec


def _ffn_fwd(name, xn, w_in, w_out):
    M, D = xn.shape
    tm, nj, x_spec, wi_spec, wo_spec = _ffn_specs(M, D, w_out.shape[0])

    def body(x_ref, wi_ref, wo_ref, y_ref):
        j = pl.program_id(1)
        z = _dot(x_ref[...], wi_ref[...])
        hid = (jax.nn.silu(z[:, :FFN_TILE]) * z[:, FFN_TILE:]).astype(BF16)
        part = _dot(hid, wo_ref[...])

        @pl.when(j == 0)
        def _():
            y_ref[...] = part

        @pl.when(j > 0)
        def _():
            y_ref[...] += part

    return _pallas(
        body, name=name, grid=(M // tm, nj), in_specs=[x_spec, wi_spec, wo_spec], out_specs=x_spec,
        out_shape=_sds((M, D), F32), compiler_params=_params(("parallel", "arbitrary")),
    )(xn, w_in, w_out)


def _ffn_mid_bwd(name, xn, dy, w_in, w_out):
    M, D = xn.shape
    F = w_out.shape[0]
    tm, nj, x_spec, wi_spec, wo_spec = _ffn_specs(M, D, F)

    def body(x_ref, dy_ref, wi_ref, wo_ref, h_ref, dz_ref):
        z = _dot(x_ref[...], wi_ref[...])
        a, b = z[:, :FFN_TILE], z[:, FFN_TILE:]
        dh = _dot_nt(dy_ref[...], wo_ref[...])
        sig = jax.nn.sigmoid(a)
        act = a * sig
        h_ref[...] = (act * b).astype(BF16)
        dz_ref[:, :FFN_TILE] = (dh * b * (sig * (1.0 + a * (1.0 - sig)))).astype(BF16)
        dz_ref[:, FFN_TILE:] = (dh * act).astype(BF16)

    return _pallas(
        body, name=name, grid=(M // tm, nj), in_specs=[x_spec, x_spec, wi_spec, wo_spec],
        out_specs=[pl.BlockSpec((tm, FFN_TILE), lambda i, j: (i, j)),
                   pl.BlockSpec((tm, 2 * FFN_TILE), lambda i, j: (i, j))],
        out_shape=[_sds((M, F), BF16), _sds((M, 2 * F), BF16)],
        compiler_params=_params(("parallel", "parallel")),
    )(xn, dy, w_in, w_out)


class _Rows:
    def __init__(self, name, f, B, L, n, tm, arrays, pieces, samp, samp_pieces, glob, out_arrays, out_pieces):
        self.name, self.f, self.B, self.L, self.n, self.tm = name, f, B, L, n, tm
        self.arrays, self.pieces, self.samp, self.samp_pieces, self.glob = arrays, pieces, samp, samp_pieces, glob
        self.out_arrays, self.out_pieces = out_arrays, out_pieces
        self.nx = n // tm
        self.grid = (B, L // tm)

    def _row_spec(self, C, batched=True):
        tm = self.tm
        if batched:
            return pl.BlockSpec((None, tm, C), lambda b, i: (b, i, 0))
        return pl.BlockSpec((tm, C), lambda b, i: (i, 0))

    def _in_specs(self, unread=()):
        nx, tm = self.nx, self.tm
        specs = [self._row_spec(a.shape[-1], a.ndim == 3) for a in self.arrays]
        for ai in unread:
            specs[ai] = pl.BlockSpec((None, tm, self.arrays[ai].shape[-1]), lambda b, i: (0, 0, 0))
        specs += [pl.BlockSpec((None, None) + s.shape[2:], lambda b, i: (b, i // nx, 0, 0)) for s in self.samp]
        specs += [pl.BlockSpec(g.shape, lambda b, i, nd=g.ndim: (0,) * nd) for g in self.glob]
        return specs

    def _load(self, a_refs, s_refs, g_refs):
        args = [a_refs[ai][:, cs:cs + cw].astype(F32) for ai, cs, cw in self.pieces]
        args += [s_refs[si][r:r + 1, :].astype(F32) for si, r in self.samp_pieces]
        args += [g[...].astype(F32) for g in g_refs]
        return args

    def fwd(self):
        na, ns, ng = len(self.arrays), len(self.samp), len(self.glob)

        def body(*refs):
            a_refs, s_refs, g_refs = refs[:na], refs[na:na + ns], refs[na + ns:na + ns + ng]
            o_refs = refs[na + ns + ng:]
            outs = self.f(*self._load(a_refs, s_refs, g_refs))
            for (oi, cs, cw), o in zip(self.out_pieces, outs):
                o_refs[oi][:, cs:cs + cw] = o.astype(o_refs[oi].dtype)

        return _pallas(
            body, name=self.name + "_fwd", grid=self.grid, in_specs=self._in_specs(),
            out_specs=[self._row_spec(C) for C, _ in self.out_arrays],
            out_shape=[_sds((self.B, self.L, C), d) for C, d in self.out_arrays],
            compiler_params=_params(("parallel", "parallel")),
        )(*self.arrays, *self.samp, *self.glob)

    def bwd(self, cts, grad_arrays, grad_glob=(), add=None, samp=None, unread=()):
        samp = self.samp if samp is None else samp
        na, ns, ng, nc = len(self.arrays), len(self.samp), len(self.glob), len(cts)
        ct_off = [c[1] if isinstance(c, tuple) else 0 for c in cts]
        cts = [c[0] if isinstance(c, tuple) else c for c in cts]
        add = add or {}
        add_keys = list(add)
        g_idx = list(grad_arrays)
        nx, B = self.nx, self.B
        n_in = na + ns + ng + nc + len(add_keys)
        n_pieces, n_sp = len(self.pieces), len(self.samp_pieces)

        def body(*refs):
            a_refs, s_refs, g_refs = refs[:na], refs[na:na + ns], refs[na + ns:na + ns + ng]
            c_refs = refs[na + ns + ng:na + ns + ng + nc]
            add_refs = refs[na + ns + ng + nc:n_in]
            d_refs = refs[n_in:n_in + len(g_idx)]
            ds_refs = refs[n_in + len(g_idx):n_in + len(g_idx) + n_sp]
            dg_refs = refs[n_in + len(g_idx) + n_sp:]
            b, i = pl.program_id(0), pl.program_id(1)
            args = self._load(a_refs, s_refs, g_refs)
            _, vjp = jax.vjp(lambda *xs: tuple(self.f(*xs)), *args)
            grads = vjp(tuple(c_refs[oi][:, ct_off[oi] + cs:ct_off[oi] + cs + cw].astype(F32)
                              for oi, cs, cw in self.out_pieces))
            for k, ai in enumerate(g_idx):
                covered = sum(cw for pa, _, cw in self.pieces if pa == ai)
                if covered < self.arrays[ai].shape[-1]:
                    d_refs[k][...] = jnp.zeros_like(d_refs[k])
                for (pa, cs, cw), gr in zip(self.pieces, grads[:n_pieces]):
                    if pa == ai:
                        if ai in add:
                            gr = gr + add_refs[add_keys.index(ai)][:, cs:cs + cw]
                        d_refs[k][:, cs:cs + cw] = gr.astype(d_refs[k].dtype)

            @pl.when((i == 0) | (i == nx))
            def _():
                for r in ds_refs:
                    r[...] = jnp.zeros_like(r)

            for r, gr in zip(ds_refs, grads[n_pieces:n_pieces + n_sp]):
                r[...] += gr

            @pl.when((b == 0) & (i == 0))
            def _():
                for r in dg_refs:
                    r[...] = jnp.zeros_like(r)

            for r, gi in zip(dg_refs, grad_glob):
                r[...] += grads[n_pieces + n_sp + gi]

        in_specs = self._in_specs(unread) + [self._row_spec(c.shape[-1]) for c in cts]
        in_specs += [self._row_spec(add[k].shape[-1]) for k in add_keys]
        out_specs = [self._row_spec(self.arrays[ai].shape[-1]) for ai in g_idx]
        out_shape = [_sds(self.arrays[ai].shape, grad_arrays[ai]) for ai in g_idx]
        for si, _ in self.samp_pieces:
            C = self.samp[si].shape[-1]
            out_specs.append(pl.BlockSpec((None, None, 1, C), lambda b, i: (b, i // nx, 0, 0)))
            out_shape.append(_sds((B, 2, 1, C), F32))
        for gi in grad_glob:
            g = self.glob[gi]
            out_specs.append(pl.BlockSpec(g.shape, lambda b, i, nd=g.ndim: (0,) * nd))
            out_shape.append(_sds(g.shape, F32))
        outs = _pallas(
            body, name=self.name + "_bwd", grid=self.grid, in_specs=in_specs, out_specs=out_specs,
            out_shape=out_shape, compiler_params=_params(("arbitrary", "arbitrary")),
        )(*self.arrays, *samp, *self.glob, *cts, *[add[k] for k in add_keys])
        ng_ = len(g_idx)
        return outs[:ng_], outs[ng_:ng_ + n_sp], outs[ng_ + n_sp:]


def _rms(x, g):
    return x * lax.rsqrt(jnp.mean(x * x, axis=-1, keepdims=True) + EPS) * g


@jax.custom_vjp
def _swap_halves(x):
    w = x.shape[-1]
    lane = lax.broadcasted_iota(jnp.int32, x.shape, x.ndim - 1)
    up = pltpu.roll(x, w - ROPE_DIM // 2, x.ndim - 1)
    down = pltpu.roll(x, ROPE_DIM // 2, x.ndim - 1)
    return jnp.where(lane % ROPE_DIM < ROPE_DIM // 2, up, down)


_swap_halves.defvjp(lambda x: (_swap_halves(x), None), lambda _, ct: (_swap_halves(ct),))


@jax.custom_vjp
def _bdot(a, b):
    return jnp.dot(a.astype(BF16), b.astype(BF16), preferred_element_type=F32)


_bdot.defvjp(lambda a, b: (_bdot(a, b), (a, b)),
             lambda res, ct: (_dot_nt(ct.astype(BF16), res[1].astype(BF16)), _dot_tn(res[0].astype(BF16), ct.astype(BF16))))


def _rope(x, cos2, sin2):
    return x * cos2 + _swap_halves(x) * sin2


def _f_pre(s, shift, scale, g):
    return (_rms(s, g) * (1.0 + scale) + shift,)


def _f_res(s, y, gate):
    return (s + gate * y,)


def _f_ab_pre(*args):
    kv_lat, q_lat, kpe = args[0:3]
    us, vs = args[3:7], args[7:11]
    cos2, sin2 = args[11:13]
    kv_norm, q_norm = args[13:15]
    vns, wss, bss = args[15:19], args[19:23], args[23:27]
    outs = [_rms(kv_lat, kv_norm), _rms(q_lat, q_norm), _rope(kpe, cos2, sin2)]
    for u, v, vn, ws, bs in zip(us, vs, vns, wss, bss):
        vg = _rms(jax.nn.gelu(v), vn)
        mixed = _bdot(ws, vg) + bs
        outs.append(jax.nn.gelu(u) * mixed)
    return tuple(outs)


def _f_q_rope(qn, qr, cos2, sin2):
    return qn, _rope(qr, cos2, sin2)


def _f_cd_pre(rk, rv, sk, sv, rq, rg, sq, c256, s256, c128, s128, c512, s512):
    return (_rope(rq, c256, s256), _rope(rk * RET_K_SCALE, c256, s256), rv,
            _rope(sq, c512, s512), _rope(sk, c128, s128), sv, rg)


def _f_cd_merge(*args):
    ys, rgs, rns = args[0:4], args[4:8], args[8:12]
    return tuple(_rms(y, rn) * jax.nn.silu(rg) for y, rg, rn in zip(ys, rgs, rns))


def _dot_nt(a, b):
    return lax.dot_general(a, b, (((1,), (1,)), ((), ())), preferred_element_type=F32)


def _dot_tn(a, b):
    return lax.dot_general(a, b, (((0,), (0,)), ((), ())), preferred_element_type=F32)


def _dot(a, b):
    return jnp.dot(a, b, preferred_element_type=F32)


def _tile_spec(tq, C):
    return pl.BlockSpec((None, tq, C), lambda b, i: (b, i, 0))


def _full_spec(L, C):
    return pl.BlockSpec((None, L, C), lambda b, i: (b, 0, 0))


def _mla_valid(i, tq, L, n):
    qpos = i * tq + lax.broadcasted_iota(jnp.int32, (tq, 1), 0)
    kpos = lax.broadcasted_iota(jnp.int32, (1, L), 1)
    return (qpos < n) | (kpos >= n)


def _mla_fwd(q, kv, kpe, n, tq):
    B, L, _ = q.shape

    def body(q_ref, kv_ref, kpe_ref, o_ref, lse_ref):
        valid = _mla_valid(pl.program_id(1), tq, L, n)
        kp = kpe_ref[:, 0:64]
        for h in range(4):
            s = _dot_nt(q_ref[:, 128 * h:128 * h + 128], kv_ref[:, 256 * h:256 * h + 128])
            s = s + _dot_nt(q_ref[:, 512 + 64 * h:576 + 64 * h], kp)
            s = jnp.where(valid, s * MLA_SCALE, NEG_INF)
            m = jnp.max(s, axis=1, keepdims=True)
            e = jnp.exp(s - m)
            l = jnp.sum(e, axis=1, keepdims=True)
            p = (e * (1.0 / l)).astype(BF16)
            o_ref[:, 128 * h:128 * h + 128] = _dot(p, kv_ref[:, 256 * h + 128:256 * h + 256]).astype(o_ref.dtype)
            lse_ref[:, h:h + 1] = m + jnp.log(l)

    return _pallas(
        body, name="mla_fwd", grid=(B, L // tq),
        in_specs=[_tile_spec(tq, 768), _full_spec(L, 1024), _full_spec(L, 128)],
        out_specs=[_tile_spec(tq, 512), _tile_spec(tq, 4)],
        out_shape=[_sds((B, L, 512), BF16), _sds((B, L, 4), F32)],
        compiler_params=_params(("parallel", "arbitrary")),
    )(q, kv, kpe)


def _mla_bwd(q, kv, kpe, lse, do, n, tq):
    B, L, _ = q.shape

    def body(q_ref, kv_ref, kpe_ref, lse_ref, do_ref, dq_ref, dkv_ref, dkpe_ref):
        i = pl.program_id(1)

        @pl.when(i == 0)
        def _():
            dkv_ref[...] = jnp.zeros_like(dkv_ref)
            dkpe_ref[...] = jnp.zeros_like(dkpe_ref)

        valid = _mla_valid(i, tq, L, n)
        kp = kpe_ref[:, 0:64]
        for h in range(4):
            qn, qr = q_ref[:, 128 * h:128 * h + 128], q_ref[:, 512 + 64 * h:576 + 64 * h]
            kn, v = kv_ref[:, 256 * h:256 * h + 128], kv_ref[:, 256 * h + 128:256 * h + 256]
            s = jnp.where(valid, (_dot_nt(qn, kn) + _dot_nt(qr, kp)) * MLA_SCALE, NEG_INF)
            p = jnp.exp(s - lse_ref[:, h:h + 1])
            doh = do_ref[:, 128 * h:128 * h + 128].astype(BF16)
            dp = _dot_nt(doh, v)
            delta = jnp.sum(p * dp, axis=1, keepdims=True)
            ds = (p * (dp - delta) * MLA_SCALE).astype(BF16)
            dq_ref[:, 128 * h:128 * h + 128] = _dot(ds, kn)
            dq_ref[:, 512 + 64 * h:576 + 64 * h] = _dot(ds, kp)
            dkv_ref[:, 256 * h:256 * h + 128] += _dot_tn(ds, qn)
            dkv_ref[:, 256 * h + 128:256 * h + 256] += _dot_tn(p.astype(BF16), doh)
            dkpe_ref[:, 0:64] += _dot_tn(ds, qr)

    return _pallas(
        body, name="mla_bwd", grid=(B, L // tq),
        in_specs=[_tile_spec(tq, 768), _full_spec(L, 1024), _full_spec(L, 128), _tile_spec(tq, 4),
                  pl.BlockSpec((None, tq, 512), lambda b, i: (b, i, 0))],
        out_specs=[_tile_spec(tq, 768), _full_spec(L, 1024), _full_spec(L, 128)],
        out_shape=[_sds((B, L, 768), F32), _sds((B, L, 1024), F32), _sds((B, L, 128), F32)],
        compiler_params=_params(("parallel", "arbitrary")),
    )(q, kv, kpe, lse, do)


def _swa_window(i, tq, n):
    W = min(tq + 2 * SWA_WINDOW, n)
    lo = jnp.clip(i * tq - SWA_WINDOW, 0, n - W)
    return pl.multiple_of(lo, 128), W


def _swa_mask(i, lo, tq, W, n):
    qpos = i * tq + lax.broadcasted_iota(jnp.int32, (tq, 1), 0)
    kpos = lo + lax.broadcasted_iota(jnp.int32, (1, W), 1)
    return (jnp.abs(qpos - kpos) <= SWA_WINDOW) & (qpos < n)


def _swa_fwd(q, k, v, sink, n, tq):
    B, L, _ = q.shape

    def body(q_ref, k_ref, v_ref, sink_ref, o_ref, lse_ref):
        i = pl.program_id(1)
        lo, W = _swa_window(i, tq, n)
        mask = _swa_mask(i, lo, tq, W, n)
        kl, vl = k_ref[pl.ds(lo, W), :], v_ref[pl.ds(lo, W), :]
        kc, vc = k_ref[n:L, :], v_ref[n:L, :]
        for h in range(8):
            g = h // 4
            qh = q_ref[:, 64 * h:64 * h + 64]
            s1 = jnp.where(mask, _dot_nt(qh, kl[:, 64 * g:64 * g + 64]) * SWA_SCALE, NEG_INF)
            s2 = _dot_nt(qh, kc[:, 64 * g:64 * g + 64]) * SWA_SCALE
            sk = sink_ref[0:1, h:h + 1]
            m = jnp.maximum(jnp.maximum(jnp.max(s1, axis=1, keepdims=True), jnp.max(s2, axis=1, keepdims=True)), sk)
            e1, e2 = jnp.exp(s1 - m), jnp.exp(s2 - m)
            l = jnp.sum(e1, axis=1, keepdims=True) + jnp.sum(e2, axis=1, keepdims=True) + jnp.exp(sk - m)
            r = 1.0 / l
            o = _dot((e1 * r).astype(BF16), vl[:, 64 * g:64 * g + 64]) + _dot((e2 * r).astype(BF16), vc[:, 64 * g:64 * g + 64])
            o_ref[:, 64 * h:64 * h + 64] = o.astype(o_ref.dtype)
            lse_ref[:, h:h + 1] = m + jnp.log(l)

    return _pallas(
        body, name="swa_fwd", grid=(B, L // tq),
        in_specs=[_tile_spec(tq, 512), _full_spec(L, 128), _full_spec(L, 128), pl.BlockSpec((1, 8), lambda b, i: (0, 0))],
        out_specs=[_tile_spec(tq, 512), _tile_spec(tq, 8)],
        out_shape=[_sds((B, L, 512), BF16), _sds((B, L, 8), F32)],
        compiler_params=_params(("parallel", "arbitrary")),
    )(q, k, v, sink)


def _swa_bwd(q, k, v, sink, lse, do, n, tq):
    B, L, _ = q.shape

    def body(q_ref, k_ref, v_ref, sink_ref, lse_ref, do_ref, dq_ref, dk_ref, dv_ref, dsink_ref):
        i = pl.program_id(1)

        @pl.when(i == 0)
        def _():
            dk_ref[...] = jnp.zeros_like(dk_ref)
            dv_ref[...] = jnp.zeros_like(dv_ref)
            dsink_ref[...] = jnp.zeros_like(dsink_ref)

        lo, W = _swa_window(i, tq, n)
        mask = _swa_mask(i, lo, tq, W, n)
        kl, vl = k_ref[pl.ds(lo, W), :], v_ref[pl.ds(lo, W), :]
        kc, vc = k_ref[n:L, :], v_ref[n:L, :]
        for h in range(8):
            g = h // 4
            cols = slice(64 * g, 64 * g + 64)
            qh = q_ref[:, 64 * h:64 * h + 64]
            lse_h = lse_ref[:, h:h + 1]
            s1 = jnp.where(mask, _dot_nt(qh, kl[:, cols]) * SWA_SCALE, NEG_INF)
            s2 = _dot_nt(qh, kc[:, cols]) * SWA_SCALE
            p1, p2 = jnp.exp(s1 - lse_h), jnp.exp(s2 - lse_h)
            ps = jnp.exp(sink_ref[0:1, h:h + 1] - lse_h)
            doh = do_ref[:, 64 * h:64 * h + 64].astype(BF16)
            dp1, dp2 = _dot_nt(doh, vl[:, cols]), _dot_nt(doh, vc[:, cols])
            delta = jnp.sum(p1 * dp1, axis=1, keepdims=True) + jnp.sum(p2 * dp2, axis=1, keepdims=True)
            ds1 = (p1 * (dp1 - delta) * SWA_SCALE).astype(BF16)
            ds2 = (p2 * (dp2 - delta) * SWA_SCALE).astype(BF16)
            dq_ref[:, 64 * h:64 * h + 64] = _dot(ds1, kl[:, cols]) + _dot(ds2, kc[:, cols])
            dk_ref[pl.ds(lo, W), cols] += _dot_tn(ds1, qh)
            dk_ref[n:L, cols] += _dot_tn(ds2, qh)
            dv_ref[pl.ds(lo, W), cols] += _dot_tn(p1.astype(BF16), doh)
            dv_ref[n:L, cols] += _dot_tn(p2.astype(BF16), doh)
            dsink_ref[0:1, h:h + 1] += jnp.sum(-ps * delta, axis=0, keepdims=True)

    return _pallas(
        body, name="swa_bwd", grid=(B, L // tq),
        in_specs=[_tile_spec(tq, 512), _full_spec(L, 128), _full_spec(L, 128), pl.BlockSpec((1, 8), lambda b, i: (0, 0)),
                  _tile_spec(tq, 8), pl.BlockSpec((None, tq, 512), lambda b, i: (b, i, 1))],
        out_specs=[_tile_spec(tq, 512), _full_spec(L, 128), _full_spec(L, 128),
                   pl.BlockSpec((None, 1, 8), lambda b, i: (b, 0, 0))],
        out_shape=[_sds((B, L, 512), F32), _sds((B, L, 128), F32), _sds((B, L, 128), F32), _sds((B, 1, 8), F32)],
        compiler_params=_params(("parallel", "arbitrary")),
    )(q, k, v, sink, lse, do)


def _ret_decay(i, tq, L, n):
    qi = i * tq + lax.broadcasted_iota(jnp.int32, (tq, 1), 0)
    q_latent = qi < n
    d_x = (qi - lax.broadcasted_iota(jnp.int32, (1, n), 1)).astype(F32)
    kc = lax.broadcasted_iota(jnp.int32, (1, L - n), 1)
    d_hf = (jnp.where(q_latent, qi, qi - L) - (kc - (L - n))).astype(F32)
    d_hb = (n + kc - qi).astype(F32)
    return q_latent, d_x, d_hf, d_hb


def _ret_mask(dist, lg):
    return jnp.where(dist >= 0.0, jnp.exp(lg * jnp.maximum(dist, 0.0)), 0.0)


def _ret_weights(q_latent, d_x, d_hf, d_hb, lg_f, lg_b, with_grad):
    e_x = jnp.exp(jnp.where(d_x >= 0.0, lg_f, -lg_b) * d_x)
    e_x = jnp.where(q_latent, e_x, 0.0)
    dec_x = jnp.where(d_x == 0.0, 2.0 * e_x, e_x)
    m_f, m_b = _ret_mask(d_hf, lg_f), _ret_mask(d_hb, lg_b)
    dec_h = m_f + m_b
    if not with_grad:
        return (dec_x, dec_h), None, None
    w_f = (e_x * jnp.maximum(d_x, 0.0), m_f * jnp.maximum(d_hf, 0.0))
    w_b = (e_x * jnp.maximum(-d_x, 0.0), m_b * jnp.maximum(d_hb, 0.0))
    return (dec_x, dec_h), w_f, w_b


def _ret_fwd(q, k, v, lg, n, tq):
    B, L, _ = q.shape

    def body(q_ref, k_ref, v_ref, lg_ref, y_ref):
        dist = _ret_decay(pl.program_id(1), tq, L, n)
        for h in range(4):
            dec, _, _ = _ret_weights(*dist, lg_ref[0:1, h:h + 1], lg_ref[1:2, h:h + 1], False)
            qh = q_ref[:, 64 * h:64 * h + 64]
            y = None
            for rows, dec_r in zip((slice(0, n), slice(n, L)), dec):
                a = _dot_nt(qh, k_ref[rows, 64 * h:64 * h + 64]) * dec_r
                part = _dot(a.astype(BF16), v_ref[rows, 128 * h:128 * h + 128])
                y = part if y is None else y + part
            y_ref[:, 128 * h:128 * h + 128] = y

    return _pallas(
        body, name="ret_fwd", grid=(B, L // tq),
        in_specs=[_tile_spec(tq, 256), _full_spec(L, 256), _full_spec(L, 512), pl.BlockSpec((2, 4), lambda b, i: (0, 0))],
        out_specs=_tile_spec(tq, 512), out_shape=_sds((B, L, 512), F32),
        compiler_params=_params(("parallel", "arbitrary")),
    )(q, k, v, lg)


def _ret_bwd(q, k, v, lg, dy, n, tq):
    B, L, _ = q.shape

    def body(q_ref, k_ref, v_ref, lg_ref, dy_ref, dq_ref, dk_ref, dv_ref, dlg_ref):
        i = pl.program_id(1)

        @pl.when(i == 0)
        def _():
            dk_ref[...] = jnp.zeros_like(dk_ref)
            dv_ref[...] = jnp.zeros_like(dv_ref)
            dlg_ref[...] = jnp.zeros_like(dlg_ref)

        def total(a):
            return jnp.sum(jnp.sum(a, axis=1, keepdims=True), axis=0, keepdims=True)

        dist = _ret_decay(i, tq, L, n)
        for h in range(4):
            qc, vc = slice(64 * h, 64 * h + 64), slice(128 * h, 128 * h + 128)
            dec, w_f, w_b = _ret_weights(*dist, lg_ref[0:1, h:h + 1], lg_ref[1:2, h:h + 1], True)
            qh = q_ref[:, qc]
            dyh = dy_ref[:, vc].astype(BF16)
            dq = None
            for rows, dec_r, wf_r, wb_r in zip((slice(0, n), slice(n, L)), dec, w_f, w_b):
                kh, vh = k_ref[rows, qc], v_ref[rows, vc]
                s = _dot_nt(qh, kh)
                gr = _dot_nt(dyh, vh)
                ds = (gr * dec_r).astype(BF16)
                part = _dot(ds, kh)
                dq = part if dq is None else dq + part
                dk_ref[rows, qc] += _dot_tn(ds, qh)
                dv_ref[rows, vc] += _dot_tn((s * dec_r).astype(BF16), dyh)
                gs = gr * s
                dlg_ref[0:1, h:h + 1] += total(gs * wf_r)
                dlg_ref[1:2, h:h + 1] += total(gs * wb_r)
            dq_ref[:, qc] = dq

    return _pallas(
        body, name="ret_bwd", grid=(B, L // tq),
        in_specs=[_tile_spec(tq, 256), _full_spec(L, 256), _full_spec(L, 512), pl.BlockSpec((2, 4), lambda b, i: (0, 0)),
                  _tile_spec(tq, 512)],
        out_specs=[_tile_spec(tq, 256), _full_spec(L, 256), _full_spec(L, 512),
                   pl.BlockSpec((None, 2, 4), lambda b, i: (b, 0, 0))],
        out_shape=[_sds((B, L, 256), F32), _sds((B, L, 256), F32), _sds((B, L, 512), F32), _sds((B, 2, 4), F32)],
        compiler_params=_params(("parallel", "arbitrary")),
    )(q, k, v, lg, dy)


def _loss_head(s, target, g, n, tm):
    B, L, D = s.shape
    nx = n // tm

    def body(s_ref, t_ref, g_ref, ds_ref, dg_ref, loss_ref):
        b, i = pl.program_id(0), pl.program_id(1)

        @pl.when((b == 0) & (i == 0))
        def _():
            dg_ref[...] = jnp.zeros_like(dg_ref)
            loss_ref[...] = jnp.zeros_like(loss_ref)

        @pl.when(i < nx)
        def _():
            y, vjp = jax.vjp(_rms, s_ref[...], g_ref[...])
            err = y - t_ref[...]
            d_s, d_g = vjp(err * (1.0 / D))
            ds_ref[...] = d_s
            dg_ref[...] += d_g
            part = jnp.sum(jnp.sum(err * err, axis=1, keepdims=True), axis=0, keepdims=True) * (0.5 / D)
            loss_ref[...] += jnp.broadcast_to(part, loss_ref.shape)

        @pl.when(i >= nx)
        def _():
            ds_ref[...] = jnp.zeros_like(ds_ref)

    return _pallas(
        body, name="loss_head", grid=(B, L // tm),
        in_specs=[pl.BlockSpec((None, tm, D), lambda b, i: (b, i, 0)),
                  pl.BlockSpec((None, tm, D), lambda b, i: (b, jnp.minimum(i, nx - 1), 0)),
                  pl.BlockSpec((1, D), lambda b, i: (0, 0))],
        out_specs=[pl.BlockSpec((None, tm, D), lambda b, i: (b, i, 0)), pl.BlockSpec((1, D), lambda b, i: (0, 0)),
                   pl.BlockSpec((1, 128), lambda b, i: (0, 0))],
        out_shape=[_sds((B, L, D), F32), _sds((1, D), F32), _sds((1, 128), F32)],
        compiler_params=_params(("arbitrary", "arbitrary")),
    )(s, target, g)


def _ada_fwd(c_all, ada_w, ada_b):
    NL, D, Ns = ada_w.shape
    R = c_all.shape[0]

    def body(c_ref, w_ref, b_ref, o_ref):
        cond = jax.nn.silu(c_ref[...]).astype(BF16)
        o_ref[...] = _dot(cond, w_ref[...].astype(BF16)) + b_ref[...]

    return _pallas(
        body, name="ada_fwd", grid=(NL,),
        in_specs=[pl.BlockSpec((R, D), lambda l: (0, 0)), pl.BlockSpec((None, D, Ns), lambda l: (l, 0, 0)),
                  pl.BlockSpec((None, 1, Ns), lambda l: (l, 0, 0))],
        out_specs=pl.BlockSpec((None, R, Ns), lambda l: (l, 0, 0)), out_shape=_sds((NL, R, Ns), F32),
        compiler_params=_params(("parallel",)),
    )(c_all, ada_w, ada_b)


def _ada_bwd(c_all, ada_w, dmods):
    NL, D, Ns = ada_w.shape
    R = c_all.shape[0]

    def body(c_ref, w_ref, dm_ref, dw_ref, dc_ref):
        cond = jax.nn.silu(c_ref[...]).astype(BF16)
        dm = dm_ref[...].astype(BF16)
        dw_ref[...] = _dot_tn(cond, dm)
        dc_ref[...] = _dot_nt(dm, w_ref[...].astype(BF16))

    return _pallas(
        body, name="ada_bwd", grid=(NL,),
        in_specs=[pl.BlockSpec((R, D), lambda l: (0, 0)), pl.BlockSpec((None, D, Ns), lambda l: (l, 0, 0)),
                  pl.BlockSpec((None, R, Ns), lambda l: (l, 0, 0))],
        out_specs=[pl.BlockSpec((None, D, Ns), lambda l: (l, 0, 0)), pl.BlockSpec((None, R, D), lambda l: (l, 0, 0))],
        out_shape=[_sds((NL, D, Ns), F32), _sds((NL, R, D), F32)],
        compiler_params=_params(("parallel",)),
    )(c_all, ada_w, dmods)


def _my_index():
    return 4 * lax.axis_index("x") + 2 * lax.axis_index("y") + lax.axis_index("c")


def _peer(k):
    x, y, c = lax.axis_index("x"), lax.axis_index("y"), lax.axis_index("c")
    kx, ky, kc = (k >> 2) & 1, (k >> 1) & 1, k & 1
    px, py, pc = (x + kx) % 2, (y + ky) % 2, (c + kc) % 2
    return (px, py, pc), 4 * px + 2 * py + pc


def _all_gather(name, shards):
    na = len(shards)
    hbm = pl.BlockSpec(memory_space=pl.ANY)

    def body(*refs):
        in_refs, out_refs = refs[:na], refs[na:2 * na]
        send_sems, recv_sems, local_sems = refs[2 * na:]
        me = _my_index()
        sib_id, sib = _peer(1)
        chips = [_peer(k) for k in (4, 2, 6)]
        sib_chips = [4 * px + 2 * py + (1 - pc) for (px, py, pc), _ in chips]

        def copy(a, k, slot, to, src=None):
            dst = out_refs[a].at[slot]
            return pltpu.make_async_remote_copy(
                src_ref=dst if src is None else src, dst_ref=dst, send_sem=send_sems.at[a, k],
                recv_sem=recv_sems.at[a, k], device_id=to, device_id_type=MESH)

        first, passed, mine = [], [], []
        for a in range(na):
            cp = pltpu.make_async_copy(in_refs[a], out_refs[a].at[me], local_sems.at[a])
            cp.start()
            mine.append(cp)
            first.append(copy(a, 0, me, sib_id, src=in_refs[a]))
            first += [copy(a, 1 + j, me, pid, src=in_refs[a]) for j, (pid, _) in enumerate(chips)]
        for cp in first:
            cp.start()
        for a in range(na):
            for j, (pid, pidx) in enumerate(chips):
                copy(a, 1 + j, pidx, pid).wait_recv()
                fwd = copy(a, 4 + j, pidx, sib_id)
                fwd.start()
                passed.append(fwd)
        for a in range(na):
            copy(a, 0, sib, sib_id).wait_recv()
            for j in range(3):
                copy(a, 4 + j, sib_chips[j], sib_id).wait_recv()
        for cp in first + passed:
            cp.wait_send()
        for cp in mine:
            cp.wait()

    return _pallas(
        body, name=name, in_specs=[hbm] * na, out_specs=[hbm] * na,
        out_shape=[_sds((N_DEV,) + s.shape, s.dtype) for s in shards],
        scratch_shapes=[pltpu.SemaphoreType.DMA((na, 7)), pltpu.SemaphoreType.DMA((na, 7)),
                        pltpu.SemaphoreType.DMA((na,))],
    )(*shards)


def _all_to_all(name, parts):
    na = len(parts)
    hbm = pl.BlockSpec(memory_space=pl.ANY)

    def body(*refs):
        in_refs, out_refs = refs[:na], refs[na:2 * na]
        send_sems, recv_sems, local_sems = refs[2 * na:]
        me = _my_index()
        copies = []
        for a in range(na):
            cp = pltpu.make_async_copy(in_refs[a].at[me], out_refs[a].at[me], local_sems.at[a])
            cp.start()
            copies.append(cp)
            for k in range(1, N_DEV):
                pid, pidx = _peer(k)
                cp = pltpu.make_async_remote_copy(
                    src_ref=in_refs[a].at[pidx], dst_ref=out_refs[a].at[me], send_sem=send_sems.at[a, k - 1],
                    recv_sem=recv_sems.at[a, k - 1], device_id=pid, device_id_type=MESH)
                cp.start()
                copies.append(cp)
        for cp in copies:
            cp.wait()

    return _pallas(
        body, name=name, in_specs=[hbm] * na, out_specs=[hbm] * na,
        out_shape=[_sds(p.shape, p.dtype) for p in parts],
        scratch_shapes=[pltpu.SemaphoreType.DMA((na, 7)), pltpu.SemaphoreType.DMA((na, 7)),
                        pltpu.SemaphoreType.DMA((na,))],
    )(*parts)


_HBM = pl.BlockSpec(memory_space=pltpu.HBM)
_SEM = pl.BlockSpec(memory_space=pltpu.SEMAPHORE)
_DATAFLOW = pltpu.SideEffectType.DATAFLOW_SIDE_EFFECTING


def _exchange_start(name, srcs, slotted, after=()):
    na = len(srcs)
    lands = [lax.empty((N_DEV,) + (s.shape[1:] if slotted else s.shape), s.dtype) for s in srcs]

    def body(*refs):
        src_refs, land_refs = refs[:na], refs[na:2 * na]
        outs = refs[2 * na + len(after):]
        send_sems, recv_sems, token = outs[:na], outs[na:2 * na], outs[4 * na]
        me = _my_index()
        for a in range(na):
            for k in range(1, N_DEV):
                pid, pidx = _peer(k)
                pltpu.make_async_remote_copy(
                    src_ref=src_refs[a].at[pidx] if slotted else src_refs[a], dst_ref=land_refs[a].at[me],
                    send_sem=send_sems[a], recv_sem=recv_sems[a], device_id=pid, device_id_type=MESH).start()
        token[...] = jnp.zeros_like(token)

    ops = [pltpu.with_memory_space_constraint(a, pltpu.HBM) for a in list(srcs) + lands]
    outs = _pallas(
        body, name=name,
        out_shape=[pltpu.SemaphoreType.DMA(())] * (2 * na) + [pltpu.HBM(a.shape, a.dtype) for a in ops]
        + [_sds((8, 128), F32)],
        in_specs=[_HBM] * (2 * na) + [pl.BlockSpec(memory_space=pl.ANY)] * len(after),
        out_specs=[_SEM] * (2 * na) + [_HBM] * (2 * na) + [pl.BlockSpec(memory_space=pltpu.VMEM)],
        input_output_aliases={a: 2 * na + a for a in range(2 * na)},
        compiler_params=pltpu.CompilerParams(has_side_effects=_DATAFLOW),
    )(*ops, *after)
    return (na, outs[:2 * na], outs[2 * na:4 * na]), outs[4 * na]


def _exchange_wait(name, handle, after):
    na, sems, thru = handle

    def body(*refs):
        land_refs = refs[na:2 * na]
        send_sems, recv_sems = refs[2 * na:3 * na], refs[3 * na:4 * na]
        me_id = (lax.axis_index("x"), lax.axis_index("y"), lax.axis_index("c"))
        for a in range(na):
            seven = land_refs[a].at[pl.ds(0, N_DEV - 1)]
            drain = pltpu.make_async_remote_copy(src_ref=seven, dst_ref=seven, send_sem=send_sems[a],
                                                 recv_sem=recv_sems[a], device_id=me_id, device_id_type=MESH)
            drain.wait_send()
            drain.wait_recv()

    outs = _pallas(
        body, name=name, out_shape=[pltpu.HBM(a.shape, a.dtype) for a in thru],
        in_specs=[_HBM] * (2 * na) + [_SEM] * (2 * na) + [pl.BlockSpec(memory_space=pl.ANY)],
        out_specs=[_HBM] * (2 * na), input_output_aliases={a: a for a in range(2 * na)},
        compiler_params=pltpu.CompilerParams(has_side_effects=_DATAFLOW),
    )(*thru, *sems, after)
    return outs[na:]


def _with_own(landed, own, slot_axis=0):
    me = _my_index()
    slot = lax.broadcasted_iota(jnp.int32, landed.shape, slot_axis)
    return jnp.where(slot == me, own, landed)


def _rope_tables(n, L, width):
    t = jnp.arange(n)
    row = (t // GRID_W).astype(F32)
    col = (t % GRID_W).astype(F32)
    n_freq = ROPE_DIM // 4
    freqs = ROPE_THETA ** (-jnp.arange(n_freq, dtype=F32) / n_freq)
    ang = jnp.concatenate([row[:, None] * freqs, col[:, None] * freqs], axis=-1)
    cos, sin = jnp.cos(ang), jnp.sin(ang)
    cos2 = jnp.concatenate([cos, cos], axis=-1)
    sin2 = jnp.concatenate([-sin, sin], axis=-1)
    cos2 = jnp.concatenate([cos2, jnp.ones((L - n, ROPE_DIM), F32)], axis=0)
    sin2 = jnp.concatenate([sin2, jnp.zeros((L - n, ROPE_DIM), F32)], axis=0)
    reps = width // ROPE_DIM
    return jnp.tile(cos2, (1, reps)), jnp.tile(sin2, (1, reps))


def _ab_perm(w):
    return jnp.concatenate([w[:, 0:256], w[:, 320:1600], w[:, 256:320], jnp.zeros((w.shape[0], 64), w.dtype)], axis=1)


def _ab_unperm(g):
    return jnp.concatenate([g[:, 0:256], g[:, 1536:1600], g[:, 256:1536]], axis=1)


def _wq_perm(w):
    return jnp.concatenate([w[:, 192 * h:192 * h + 128] for h in range(4)]
                           + [w[:, 192 * h + 128:192 * h + 192] for h in range(4)], axis=1)


def _wq_unperm(g):
    return jnp.concatenate([g[:, sl] for h in range(4)
                            for sl in (slice(128 * h, 128 * h + 128), slice(512 + 64 * h, 576 + 64 * h))], axis=1)


def _flat(a):
    return a.reshape(-1, a.shape[-1])


def _layer_weights(full, p):
    NL, NE, NO = len(full["ffn_in"]), len(full["ab_in"]), len(full["cd_in"])
    groups = range(4)

    def each(f, mats):
        return [None if w is None else f(w) for w in mats]

    return dict(
        norm_mix=[p["norm_mix"][l][None] for l in range(NL)], norm_ffn=[p["norm_ffn"][l][None] for l in range(NL)],
        norm_final=p["norm_final"][None],
        ffn_in=each(_ffn_interleave, full["ffn_in"]), ffn_out=list(full["ffn_out"]),
        ab_in=each(_ab_perm, full["ab_in"]), ab_out=list(full["ab_out"]),
        wq=each(_wq_perm, full["mla_wq_b"]), wkv=list(full["mla_wkv_b"]),
        kv_norm=[p["mla_kv_norm"][j][None] for j in range(NE)], q_norm=[p["mla_q_norm"][j][None] for j in range(NE)],
        v_norm=[[p["cmlp_v_norm"][j][None, 128 * g:128 * g + 128] for g in groups] for j in range(NE)],
        ws=[[p["cmlp_ws"][j, g] for g in groups] for j in range(NE)],
        bs=[[p["cmlp_bs"][j, g][:, None] for g in groups] for j in range(NE)],
        cd_in=list(full["cd_in"]), cd_out=list(full["cd_out"]),
        lg=[jnp.stack([jax.nn.log_sigmoid(p["ret_decay_fwd"][j]), jax.nn.log_sigmoid(p["ret_decay_bwd"][j])])
            for j in range(NO)],
        sink=[p["swa_sink"][j][None] for j in range(NO)],
        ret_norm=[[p["ret_norm"][j][None, 128 * g:128 * g + 128] for g in groups] for j in range(NO)],
    )


def _local_step(x, ctx, target, mods, W, later_weights=None, early_grads=None):
    B, n, D = x.shape
    m = ctx.shape[1]
    L = n + m
    NL = mods.shape[0]
    tm = min(256, m)
    tq = min(256, m)
    cos512, sin512 = _rope_tables(n, L, 512)
    s = jnp.concatenate([x, ctx], axis=1)
    saved = []

    def rows(name, f, arrays, pieces, samp, samp_pieces, glob, out_arrays, out_pieces, tile=tm):
        return _Rows(name, f, B, L, n, tile, arrays, pieces, samp, samp_pieces, glob, out_arrays, out_pieces)

    def full(width, start=0):
        return [(0, start, width)]

    for l in range(NL):
        j = l // 2
        even = l % 2 == 0
        md = mods[l]
        if l == 1 and later_weights is not None:
            W = later_weights(1, s)
        r = {"s0": s}
        pre1 = rows(f"pre_mix{l}", _f_pre, [s], full(D), [md], [(0, 0), (0, 1)], [W["norm_mix"][l]], [(D, BF16)], full(D))
        (xn,) = pre1.fwd()
        r["pre1"], r["xn"] = pre1, xn
        if even:
            z = _mm(f"ab_in{l}", _flat(xn), W["ab_in"][j]).reshape(B, L, 1664)
            pieces = [(0, 0, 256), (0, 256, 256), (0, 1536, 128)]
            pieces += [(0, 512 + 128 * g, 128) for g in range(4)] + [(0, 1024 + 128 * g, 128) for g in range(4)]
            pieces += [(1, 0, 128), (2, 0, 128)]
            glob = [W["kv_norm"][j], W["q_norm"][j]] + W["v_norm"][j] + W["ws"][j] + W["bs"][j]
            abp = rows(f"ab_pre{l}", _f_ab_pre, [z, cos512, sin512], pieces, [], [], glob,
                       [(256, BF16), (256, BF16), (128, BF16), (512, BF16)],
                       [(0, 0, 256), (1, 0, 256), (2, 0, 128)] + [(3, 128 * g, 128) for g in range(4)], tile=128)
            kvn, qn, kpe, cm = abp.fwd()
            kv = _mm(f"wkv{l}", _flat(kvn), W["wkv"][j], out_dtype=BF16).reshape(B, L, 1024)
            q0 = _mm(f"wq{l}", _flat(qn), W["wq"][j]).reshape(B, L, 768)
            qrp = rows(f"q_rope{l}", _f_q_rope, [q0, cos512, sin512], [(0, 0, 512), (0, 512, 256), (1, 0, 256), (2, 0, 256)],
                       [], [], [], [(768, BF16)], [(0, 0, 512), (0, 512, 256)])
            (q,) = qrp.fwd()
            o, lse = _mla_fwd(q, kv, kpe, n, tq)
            y = _mm(f"ab_out_a{l}", _flat(o), W["ab_out"][j][:512])
            y = _mm(f"ab_out_b{l}", _flat(cm), W["ab_out"][j][512:], add=y).reshape(B, L, D)
            r.update(z=z, abp=abp, kvn=kvn, qn=qn, kpe=kpe, cm=cm, kv=kv, qrp=qrp, q=q, o=o, lse=lse)
        else:
            z = _mm(f"cd_in{l}", _flat(xn), W["cd_in"][j]).reshape(B, L, 2304)
            pieces = [(0, 0, 256), (0, 256, 512), (0, 768, 128), (0, 896, 128), (0, 1024, 256), (0, 1280, 512), (0, 1792, 512)]
            pieces += [(1, 0, 256), (2, 0, 256), (1, 0, 128), (2, 0, 128), (1, 0, 512), (2, 0, 512)]
            cdp = rows(f"cd_pre{l}", _f_cd_pre, [z, cos512, sin512], pieces, [], [], [],
                       [(256, BF16), (256, BF16), (512, BF16), (512, BF16), (128, BF16), (128, BF16), (512, F32)],
                       [(k_, 0, w_) for k_, w_ in enumerate((256, 256, 512, 512, 128, 128, 512))])
            rq, rk, rv, sq, sk, sv, rg = cdp.fwd()
            yret = _ret_fwd(rq, rk, rv, W["lg"][j], n, tq)
            osw, lse = _swa_fwd(sq, sk, sv, W["sink"][j], n, tq)
            mrg = rows(f"cd_merge{l}", _f_cd_merge, [yret, rg],
                       [(0, 128 * g, 128) for g in range(4)] + [(1, 128 * g, 128) for g in range(4)], [], [],
                       W["ret_norm"][j], [(512, BF16)], [(0, 128 * g, 128) for g in range(4)])
            (yr,) = mrg.fwd()
            y = _mm(f"cd_out_a{l}", _flat(yr), W["cd_out"][j][:512])
            y = _mm(f"cd_out_b{l}", _flat(osw), W["cd_out"][j][512:], add=y).reshape(B, L, D)
            r.update(z=z, cdp=cdp, rq=rq, rk=rk, rv=rv, sq=sq, sk=sk, sv=sv, rg=rg, yret=yret, osw=osw, lse=lse,
                     mrg=mrg, yr=yr)
        res1 = rows(f"res_mix{l}", _f_res, [s, y], [(0, 0, D), (1, 0, D)], [md], [(0, 2)], [], [(D, F32)], full(D))
        (s1,) = res1.fwd()
        pre2 = rows(f"pre_ffn{l}", _f_pre, [s1], full(D), [md], [(0, 3), (0, 4)], [W["norm_ffn"][l]], [(D, BF16)], full(D))
        (xn2,) = pre2.fwd()
        if l == 0 and later_weights is not None:
            W = later_weights(0, s1)
        y2 = _ffn_fwd(f"ffn{l}", _flat(xn2), W["ffn_in"][l], W["ffn_out"][l]).reshape(B, L, D)
        res2 = rows(f"res_ffn{l}", _f_res, [s1, y2], [(0, 0, D), (1, 0, D)], [md], [(0, 5)], [], [(D, F32)], full(D))
        (s,) = res2.fwd()
        r.update(res1=res1, pre2=pre2, xn2=xn2, res2=res2)
        saved.append(r)

    ds, d_norm_final, loss = _loss_head(s, target, W["norm_final"], n, tm)

    G = {k: [None] * len(v) for k, v in W.items() if isinstance(v, list)}
    G["norm_final"] = d_norm_final
    dmods = [None] * NL
    for l in reversed(range(NL)):
        j = l // 2
        even = l % 2 == 0
        r = saved[l]
        dm = [None] * 6
        smp = None
        if l == 0 and early_grads is not None:
            smp = [mods[0] + early_grads(0, G)]
        (ds1, dy2), (dm[5],), _ = r["res2"].bwd([ds], {0: F32, 1: BF16}, samp=smp, unread=(0,))
        dy2f, xn2f = _flat(dy2), _flat(r["xn2"])
        hid, dz2 = _ffn_mid_bwd(f"ffn_mid{l}", xn2f, dy2f, W["ffn_in"][l], W["ffn_out"][l])
        G["ffn_out"][l] = _mm(f"g_ffn_out{l}", hid, dy2f, ta=True, out_dtype=BF16)
        G["ffn_in"][l] = _mm(f"g_ffn_in{l}", xn2f, dz2, ta=True, out_dtype=BF16)
        dxn2 = _mm(f"d_xn2{l}", dz2, W["ffn_in"][l], tb=True).reshape(B, L, D)
        (ds1,), (dm[3], dm[4]), (G["norm_ffn"][l],) = r["pre2"].bwd([dxn2], {0: F32}, grad_glob=(0,), add={0: ds1},
                                                                       samp=smp)
        if l == 0 and early_grads is not None:
            smp = [mods[0] + early_grads(1, G)]
        (ds0, dy), (dm[2],), _ = r["res1"].bwd([ds1], {0: F32, 1: BF16}, samp=smp, unread=(0,))
        dyf = _flat(dy)
        if even:
            w_out = W["ab_out"][j]
            dcat = _mm(f"d_cat{l}", dyf, w_out, tb=True).reshape(B, L, -1)
            G["ab_out"][l // 2] = jnp.concatenate(
                [_mm(f"g_ab_out_a{l}", _flat(r["o"]), dyf, ta=True, out_dtype=BF16),
                 _mm(f"g_ab_out_b{l}", _flat(r["cm"]), dyf, ta=True, out_dtype=BF16)], axis=0)
            dq, dkv, dkpe = _mla_bwd(r["q"], r["kv"], r["kpe"], r["lse"], dcat, n, tq)
            (dq0,), _, _ = r["qrp"].bwd([dq], {0: BF16}, unread=(0,))
            dq0f, dkvf = _flat(dq0), _flat(dkv)
            G["wq"][j] = _mm(f"g_wq{l}", _flat(r["qn"]), dq0f, ta=True, out_dtype=BF16)
            G["wkv"][j] = _mm(f"g_wkv{l}", _flat(r["kvn"]), dkvf, ta=True, out_dtype=BF16)
            dqn = _mm(f"d_qn{l}", dq0f, W["wq"][j], tb=True).reshape(B, L, 256)
            dkvn = _mm(f"d_kvn{l}", dkvf, W["wkv"][j], tb=True).reshape(B, L, 256)
            (dz,), _, gg = r["abp"].bwd([dkvn, dqn, dkpe, (dcat, 512)], {0: BF16}, grad_glob=tuple(range(14)))
            G["kv_norm"][j], G["q_norm"][j] = gg[0], gg[1]
            G["v_norm"][j], G["ws"][j], G["bs"][j] = list(gg[2:6]), list(gg[6:10]), list(gg[10:14])
            w_in, key = W["ab_in"][j], "ab_in"
        else:
            w_out = W["cd_out"][j]
            dcat = _mm(f"d_cat{l}", dyf, w_out, tb=True).reshape(B, L, -1)
            G["cd_out"][j] = jnp.concatenate(
                [_mm(f"g_cd_out_a{l}", _flat(r["yr"]), dyf, ta=True, out_dtype=BF16),
                 _mm(f"g_cd_out_b{l}", _flat(r["osw"]), dyf, ta=True, out_dtype=BF16)], axis=0)
            (dyret, drg), _, gg = r["mrg"].bwd([(dcat, 0)], {0: F32, 1: F32}, grad_glob=(0, 1, 2, 3))
            G["ret_norm"][j] = list(gg)
            drq, drk, drv, dlg = _ret_bwd(r["rq"], r["rk"], r["rv"], W["lg"][j], dyret, n, tq)
            dsq, dsk, dsv, dsink = _swa_bwd(r["sq"], r["sk"], r["sv"], W["sink"][j], r["lse"], dcat, n, tq)
            G["lg"][j], G["sink"][j] = dlg, dsink
            (dz,), _, _ = r["cdp"].bwd([drq, drk, drv, dsq, dsk, dsv, drg], {0: BF16}, unread=(0,))
            w_in, key = W["cd_in"][j], "cd_in"
        dzf = _flat(dz)
        G[key][j] = _mm(f"g_{key}{l}", _flat(r["xn"]), dzf, ta=True, out_dtype=BF16)
        dxn = _mm(f"d_xn{l}", dzf, w_in, tb=True).reshape(B, L, D)
        (ds,), (dm[0], dm[1]), (G["norm_mix"][l],) = r["pre1"].bwd([dxn], {0: F32}, grad_glob=(0,), add={0: ds0},
                                                                       samp=smp)
        dmods[l] = jnp.concatenate(dm, axis=2)
    return loss, ds[:, :n], jnp.stack(dmods), G


def kernel(x, c, ctx, c_ctx, ada_w, ada_b, norm_mix, norm_ffn, norm_final, ffn_in, ffn_out, ab_in, ab_out, mla_q_norm, mla_kv_norm, mla_wq_b, mla_wkv_b, cmlp_v_norm, cmlp_ws, cmlp_bs, cd_in, cd_out, ret_decay_fwd, ret_decay_bwd, ret_norm, swa_sink, loss_target, m_c_ctx, m_ada_w, m_ada_b, m_norm_mix, m_norm_ffn, m_norm_final, m_ffn_in, m_ffn_out, m_ab_in, m_ab_out, m_mla_q_norm, m_mla_kv_norm, m_mla_wq_b, m_mla_wkv_b, m_cmlp_v_norm, m_cmlp_ws, m_cmlp_bs, m_cd_in, m_cd_out, m_ret_decay_fwd, m_ret_decay_bwd, m_ret_norm, m_swa_sink, v_c_ctx, v_ada_w, v_ada_b, v_norm_mix, v_norm_ffn, v_norm_final, v_ffn_in, v_ffn_out, v_ab_in, v_ab_out, v_mla_q_norm, v_mla_kv_norm, v_mla_wq_b, v_mla_wkv_b, v_cmlp_v_norm, v_cmlp_ws, v_cmlp_bs, v_cd_in, v_cd_out, v_ret_decay_fwd, v_ret_decay_bwd, v_ret_norm, v_swa_sink):
    B, n, D = x.shape
    NL = ada_w.shape[0]
    NE, NO = ab_in.shape[0], cd_in.shape[0]
    me = _my_index()
    weights = dict(c_ctx=c_ctx, ada_w=ada_w, ada_b=ada_b, norm_mix=norm_mix, norm_ffn=norm_ffn, norm_final=norm_final,
                   ffn_in=ffn_in, ffn_out=ffn_out, ab_in=ab_in, ab_out=ab_out, mla_q_norm=mla_q_norm,
                   mla_kv_norm=mla_kv_norm, mla_wq_b=mla_wq_b, mla_wkv_b=mla_wkv_b, cmlp_v_norm=cmlp_v_norm,
                   cmlp_ws=cmlp_ws, cmlp_bs=cmlp_bs, cd_in=cd_in, cd_out=cd_out, ret_decay_fwd=ret_decay_fwd,
                   ret_decay_bwd=ret_decay_bwd, ret_norm=ret_norm, swa_sink=swa_sink)
    moments_m = dict(c_ctx=m_c_ctx, ada_w=m_ada_w, ada_b=m_ada_b, norm_mix=m_norm_mix, norm_ffn=m_norm_ffn,
                     norm_final=m_norm_final, ffn_in=m_ffn_in, ffn_out=m_ffn_out, ab_in=m_ab_in, ab_out=m_ab_out,
                     mla_q_norm=m_mla_q_norm, mla_kv_norm=m_mla_kv_norm, mla_wq_b=m_mla_wq_b, mla_wkv_b=m_mla_wkv_b,
                     cmlp_v_norm=m_cmlp_v_norm, cmlp_ws=m_cmlp_ws, cmlp_bs=m_cmlp_bs, cd_in=m_cd_in, cd_out=m_cd_out,
                     ret_decay_fwd=m_ret_decay_fwd, ret_decay_bwd=m_ret_decay_bwd, ret_norm=m_ret_norm,
                     swa_sink=m_swa_sink)
    moments_v = dict(c_ctx=v_c_ctx, ada_w=v_ada_w, ada_b=v_ada_b, norm_mix=v_norm_mix, norm_ffn=v_norm_ffn,
                     norm_final=v_norm_final, ffn_in=v_ffn_in, ffn_out=v_ffn_out, ab_in=v_ab_in, ab_out=v_ab_out,
                     mla_q_norm=v_mla_q_norm, mla_kv_norm=v_mla_kv_norm, mla_wq_b=v_mla_wq_b, mla_wkv_b=v_mla_wkv_b,
                     cmlp_v_norm=v_cmlp_v_norm, cmlp_ws=v_cmlp_ws, cmlp_bs=v_cmlp_bs, cd_in=v_cd_in, cd_out=v_cd_out,
                     ret_decay_fwd=v_ret_decay_fwd, ret_decay_bwd=v_ret_decay_bwd, ret_norm=v_ret_norm,
                     swa_sink=v_swa_sink)
    order = list(weights)

    Ns = ada_w.shape[2]
    (c_g,) = _all_gather("gather_c", [c])
    R = N_DEV * B + 8
    c_all = jnp.concatenate([c_g.reshape(N_DEV * B, D), jnp.broadcast_to(c_ctx[None], (8, D))], axis=0)
    ada_b_mine = lax.dynamic_slice_in_dim(ada_b, me * Ns, Ns, axis=1)[:, None, :]
    mods_shard = _ada_fwd(c_all, ada_w, ada_b_mine)
    (mods_g,) = _all_gather("gather_mods", [mods_shard])
    mods_full = jnp.transpose(mods_g, (1, 2, 0, 3)).reshape(NL, R, 6, D)
    mx = lax.dynamic_slice_in_dim(mods_full, me * B, B, axis=1)
    mh = jnp.broadcast_to(mods_full[:, N_DEV * B][:, None], (NL, B, 6, D))
    mods = jnp.stack([mx, mh], axis=2)

    big = ["ffn_in", "ffn_out", "ab_in", "ab_out", "cd_in", "cd_out", "mla_wq_b", "mla_wkv_b"]
    col_sharded = {"ffn_in", "ab_in", "cd_in", "mla_wq_b", "mla_wkv_b"}
    shards = {k: _to_bf16("cast_" + k, weights[k]) for k in big}
    first = {k: 0 if k.startswith("cd_") else 1 for k in big}
    a_keys = [k for k in big if first[k] and not k.startswith("ffn_")]
    b_keys = ["ffn_in", "ffn_out"]
    early = _all_gather("gather_wA", [shards[k][:1] for k in a_keys])
    (rn_g,) = _all_gather("gather_ret_norm", [ret_norm])
    small_first = [mods, rn_g] + list(early)
    wb_handle, wb_token = _exchange_start("gather_wB_start", [shards[k][:1] for k in b_keys], False, small_first)
    wc_handle, wc_token = _exchange_start("gather_wC_start", [shards[k][first[k]:] for k in big], False,
                                          small_first + [wb_token])
    mods = mods + (wb_token[0, 0] + wc_token[0, 0])

    def unshard(k, g):
        if k in col_sharded:
            f = jnp.transpose(g, (1, 2, 0, 3)).reshape(g.shape[1], g.shape[2], -1)
        else:
            f = jnp.transpose(g, (1, 0, 2, 3)).reshape(g.shape[1], -1, g.shape[3])
        return [f[i] for i in range(f.shape[0])]

    rn_full = jnp.transpose(rn_g, (1, 0, 2)).reshape(NO, -1)
    small_p = dict(weights, ret_norm=rn_full)
    full0 = {k: [None] * weights[k].shape[0] for k in big}
    for k, g in zip(a_keys, early):
        full0[k][:1] = unshard(k, g)

    def later_weights(stage, newest):
        if stage == 0:
            landed = _exchange_wait("gather_wB_wait", wb_handle, newest)
            for k, land in zip(b_keys, landed):
                full0[k][:1] = unshard(k, _with_own(land, shards[k][:1]))
        else:
            landed = _exchange_wait("gather_wC_wait", wc_handle, newest)
            for k, land in zip(big, landed):
                full0[k][first[k]:] = unshard(k, _with_own(land, shards[k][first[k]:]))
        return _layer_weights(full0, small_p)

    def to_slots(k, gl):
        g = jnp.stack(gl)
        if k in col_sharded:
            return jnp.transpose(g.reshape(g.shape[0], g.shape[1], N_DEV, -1), (2, 0, 1, 3))
        return jnp.transpose(g.reshape(g.shape[0], N_DEV, -1, g.shape[2]), (1, 0, 2, 3))

    def big_grads(G):
        return dict(ffn_in=[g if g is None else _ffn_deinterleave(g) for g in G["ffn_in"]], ffn_out=G["ffn_out"],
                    ab_in=[g if g is None else _ab_unperm(g) for g in G["ab_in"]],
                    ab_out=G["ab_out"], cd_in=G["cd_in"], cd_out=G["cd_out"],
                    mla_wq_b=[g if g is None else _wq_unperm(g) for g in G["wq"]], mla_wkv_b=G["wkv"])

    sent = {}

    def early_grads(stage, G):
        parts = big_grads(G)
        if stage == 0:
            srcs = [to_slots(k, parts[k][first[k]:]) for k in big]
            handle, g_token = _exchange_start("scatter_gC_start", srcs, slotted=True)
        else:
            srcs = [to_slots(k, parts[k][:1]) for k in b_keys]
            handle, g_token = _exchange_start("scatter_gB_start", srcs, slotted=True)
        sent[stage] = (handle, srcs)
        return g_token[0, 0]

    loss_part, grad_x, dmods, G = _local_step(x, ctx, loss_target, mods, _layer_weights(full0, small_p),
                                              later_weights, early_grads)

    dmx = dmods[:, :, 0].reshape(NL, B, 6 * D)
    dmh = jnp.sum(dmods[:, :, 1], axis=1).reshape(NL, 1, 6 * D)
    (dm_g,) = _all_gather("gather_dmods", [jnp.concatenate([dmx, dmh], axis=1)])
    dmx_all = jnp.transpose(dm_g[:, :, :B], (1, 0, 2, 3)).reshape(NL, N_DEV * B, 6 * D)
    dmh_all = jnp.sum(dm_g[:, :, B], axis=0)
    dm_rows = jnp.concatenate([dmx_all, dmh_all[:, None], jnp.zeros((NL, 7, 6 * D), F32)], axis=1)
    g_ada_b = jnp.sum(dm_rows, axis=1)
    dm_mine = lax.dynamic_slice_in_dim(dm_rows, me * Ns, Ns, axis=2)
    g_ada_w, dcond = _ada_bwd(c_all, ada_w, dm_mine)
    sg = jax.nn.sigmoid(c_ctx)
    d_c_ctx_part = jnp.sum(dcond[:, N_DEV * B], axis=0) * (sg * (1.0 + c_ctx * (1.0 - sg)))

    def cat(parts):
        return jnp.concatenate([p.reshape(-1) for p in parts])

    dlg = jnp.stack([jnp.sum(G["lg"][j], axis=0) for j in range(NO)])
    sig_f, sig_b = jax.nn.sigmoid(-ret_decay_fwd), jax.nn.sigmoid(-ret_decay_bwd)
    small = dict(
        loss=loss_part[0, 0:1],
        c_ctx=d_c_ctx_part,
        norm_mix=cat(G["norm_mix"]), norm_ffn=cat(G["norm_ffn"]), norm_final=G["norm_final"].reshape(-1),
        mla_q_norm=cat(G["q_norm"]), mla_kv_norm=cat(G["kv_norm"]),
        cmlp_v_norm=cat([cat(G["v_norm"][j]) for j in range(NE)]),
        cmlp_ws=cat([jnp.stack(G["ws"][j]) for j in range(NE)]),
        cmlp_bs=cat([jnp.stack([b_[:, 0] for b_ in G["bs"][j]]) for j in range(NE)]),
        ret_decay_fwd=(dlg[:, 0] * sig_f).reshape(-1), ret_decay_bwd=(dlg[:, 1] * sig_b).reshape(-1),
        ret_norm=cat([cat(G["ret_norm"][j]) for j in range(NO)]),
        swa_sink=cat([jnp.sum(G["sink"][j], axis=0) for j in range(NO)]),
    )
    small_keys = list(small)
    sizes = [small[k].shape[0] for k in small_keys]
    total = sum(sizes)
    padded = -(-total // 2048) * 2048
    packed = jnp.concatenate([small[k] for k in small_keys] + [jnp.zeros((padded - total,), F32)]).reshape(-1, 128)
    (small_g,) = _all_gather("gather_small", [packed])

    parts = big_grads(G)
    landed0 = dict(zip(a_keys, _all_to_all("scatter_gA", [to_slots(k, parts[k][:1]) for k in a_keys])))
    for k, land, src in zip(b_keys, _exchange_wait("scatter_gB_wait", sent[1][0], grad_x), sent[1][1]):
        landed0[k] = _with_own(land, src)
    landed = []
    for k, land, src in zip(big, _exchange_wait("scatter_gC_wait", sent[0][0], grad_x), sent[0][1]):
        rest = _with_own(land, src)
        landed.append(jnp.concatenate([landed0[k], rest], axis=1) if first[k] else rest)

    grads, deltas, new_m, new_v = {}, {}, {}, {}
    for k, land in zip(big, landed):
        grads[k], deltas[k], new_m[k], new_v[k] = _adamw_from_slots("adamw_" + k, weights[k], moments_m[k], moments_v[k], land)
    deltas["ada_w"], new_m["ada_w"], new_v["ada_w"] = [
        o.reshape(ada_w.shape) for o in _adamw("adamw_ada_w", _flat(ada_w), _flat(g_ada_w), _flat(m_ada_w), _flat(v_ada_w))]
    grads["ada_w"] = g_ada_w

    sums_only = ("loss", "ret_norm")

    def packed_of(src, fill):
        vals = [jnp.full((sizes[i],), fill, F32) if k in sums_only else src[k].reshape(-1)
                for i, k in enumerate(small_keys)]
        return jnp.concatenate(vals + [jnp.full((padded - total,), fill, F32)]).reshape(-1, 128)

    w_p, m_p, v_p = packed_of(weights, 0.0), packed_of(moments_m, 0.0), packed_of(moments_v, 1.0)

    def f_small(w_, m_, v_, land_):
        g = _sum_slots(land_)
        return (g,) + _adamw_math(w_, g, m_, v_)

    g_p, d_p, nm_p, nv_p = _ew("adamw_small", f_small, [w_p, m_p, v_p, small_g], [F32] * 4)
    offs = np.cumsum([0] + sizes)
    for i, k in enumerate(small_keys):
        sl = slice(int(offs[i]), int(offs[i + 1]))
        if k == "loss":
            loss = g_p.reshape(-1)[int(offs[i])]
        elif k == "ret_norm":
            g_full = g_p.reshape(-1)[sl].reshape(NO, -1)
            g_mine = lax.dynamic_slice_in_dim(g_full, me * ret_norm.shape[1], ret_norm.shape[1], axis=1)
            d_, m_, v_ = _adamw("adamw_ret_norm", *[jnp.pad(a, ((0, 8 - NO), (0, 128 - a.shape[1])), constant_values=cv)
                                                     for a, cv in ((ret_norm, 0.0), (g_mine, 0.0), (m_ret_norm, 0.0), (v_ret_norm, 1.0))])
            grads[k] = g_mine
            deltas[k], new_m[k], new_v[k] = [a[:NO, :ret_norm.shape[1]] for a in (d_, m_, v_)]
        else:
            shp = weights[k].shape
            grads[k], deltas[k], new_m[k], new_v[k] = [a.reshape(-1)[sl].reshape(shp) for a in (g_p, d_p, nm_p, nv_p)]
    pad_b = lambda a, cv=0.0: jnp.pad(a, ((0, 8 - NL), (0, 0)), constant_values=cv)
    d_, m_, v_ = _adamw("adamw_ada_b", pad_b(ada_b), pad_b(g_ada_b), pad_b(m_ada_b), pad_b(v_ada_b, 1.0))
    grads["ada_b"] = g_ada_b
    deltas["ada_b"], new_m["ada_b"], new_v["ada_b"] = d_[:NL], m_[:NL], v_[:NL]

    return (loss, grad_x, *[grads[k] for k in order], *[deltas[k] for k in order],
            *[new_m[k] for k in order], *[new_v[k] for k in order])
```

```python
import functools

import numpy as np
import jax
import jax.numpy as jnp
from jax import lax
from jax.experimental import pallas as pl
from jax.experimental.pallas import tpu as pltpu

F32 = jnp.float32
BF16 = jnp.bfloat16
EPS = 1e-6
NEG_INF = -1e30
N_DEV = 8
GRID_W = 64
ROPE_THETA = 10000.0
ROPE_DIM = 64
SWA_WINDOW = 128
MLA_SCALE = (128 + 64) ** -0.5
SWA_SCALE = 64 ** -0.5
RET_K_SCALE = 64 ** -0.5
ADAM_LR, ADAM_B1, ADAM_B2, ADAM_EPS, ADAM_WD, ADAM_STEP = 0.001, 0.9, 0.999, 1e-08, 0.01, 10
V7X_VMEM_LIMIT = 56 * 1024 * 1024
MESH = pl.DeviceIdType.MESH


def _pallas(body, **kw):
    return pl.pallas_call(body, **kw)


def _params(sem=None):
    return pltpu.CompilerParams(dimension_semantics=sem, vmem_limit_bytes=V7X_VMEM_LIMIT)


def _tile(n, cap, align):
    best = None
    for t in range(align, min(n, cap) + 1, align):
        if n % t == 0:
            best = t
    return n if best is None else best


def _sds(shape, dtype):
    return jax.ShapeDtypeStruct(tuple(shape), dtype)


def _ew(name, f, ins, out_dtypes, cap_elems=131072):
    R, C = ins[0].shape[-2:]
    tr = _tile(R, max(16, cap_elems // C), 16)
    n_in = len(ins)

    def spec(a):
        if a.ndim == 2:
            return pl.BlockSpec((tr, C), lambda i: (i, 0))
        return pl.BlockSpec((a.shape[0], tr, C), lambda i: (0, i, 0))

    def body(*refs):
        outs = f(*[r[...] for r in refs[:n_in]])
        for r, o in zip(refs[n_in:], outs):
            r[...] = o.astype(r.dtype)

    return _pallas(
        body, name=name, grid=(R // tr,), in_specs=[spec(a) for a in ins],
        out_specs=[pl.BlockSpec((tr, C), lambda i: (i, 0)) for _ in out_dtypes],
        out_shape=[_sds((R, C), d) for d in out_dtypes], compiler_params=_params(("parallel",)),
    )(*ins)


def _to_bf16(name, w):
    w2 = w.reshape(-1, w.shape[-1])
    return _ew(name, lambda v: (v,), [w2], [BF16])[0].reshape(w.shape)


def _adamw_math(w, g, m, v):
    m = ADAM_B1 * m + (1.0 - ADAM_B1) * g
    v = ADAM_B2 * v + (1.0 - ADAM_B2) * (g * g)
    m_hat = m / (1.0 - ADAM_B1 ** ADAM_STEP)
    v_hat = v / (1.0 - ADAM_B2 ** ADAM_STEP)
    delta = -ADAM_LR * (m_hat / (jnp.sqrt(v_hat) + ADAM_EPS) + ADAM_WD * w)
    return delta, m, v


def _sum_slots(land):
    g = land[0].astype(F32)
    for s in range(1, land.shape[0]):
        g = g + land[s].astype(F32)
    return g


def _adamw_from_slots(name, w, m, v, land):
    shp = w.shape
    C = shp[-1]

    def f(w_, m_, v_, land_):
        g = _sum_slots(land_)
        return (g,) + _adamw_math(w_, g, m_, v_)

    outs = _ew(name, f, [w.reshape(-1, C), m.reshape(-1, C), v.reshape(-1, C), land.reshape(N_DEV, -1, C)],
               [F32] * 4, cap_elems=65536)
    return [o.reshape(shp) for o in outs]


def _adamw(name, w, g, m, v):
    outs = _ew(name, lambda w_, g_, m_, v_: _adamw_math(w_, g_, m_, v_), [w, g, m, v], [F32] * 3)
    return outs


def _mm(name, a, b, ta=False, tb=False, out_dtype=F32, add=None):
    M, K = (a.shape[1], a.shape[0]) if ta else a.shape
    N = b.shape[0] if tb else b.shape[1]
    tm = _tile(M, 1024, 128)
    tn = _tile(N, 1024, 128)
    if tn < 256 and N <= 2432:
        tn = N
    tk = _tile(K, 1792, 128)
    nk = K // tk
    a_spec = pl.BlockSpec((tk, tm), lambda i, j, k: (k, i)) if ta else pl.BlockSpec((tm, tk), lambda i, j, k: (i, k))
    b_spec = pl.BlockSpec((tn, tk), lambda i, j, k: (j, k)) if tb else pl.BlockSpec((tk, tn), lambda i, j, k: (k, j))
    o_spec = pl.BlockSpec((tm, tn), lambda i, j, k: (i, j))
    dims = (((0 if ta else 1,), (1 if tb else 0,)), ((), ()))
    has_add = add is not None

    def product(a_ref, b_ref):
        return lax.dot_general(a_ref[...].astype(BF16), b_ref[...].astype(BF16), dims, preferred_element_type=F32)

    def body_single(*refs):
        acc = product(refs[0], refs[1])
        if has_add:
            acc = acc + refs[2][...]
        refs[-1][...] = acc.astype(refs[-1].dtype)

    def body(*refs):
        a_ref, b_ref = refs[0], refs[1]
        add_ref = refs[2] if has_add else None
        o_ref, acc = refs[-2], refs[-1]
        k = pl.program_id(2)

        @pl.when(k == 0)
        def _():
            acc[...] = add_ref[...] if has_add else jnp.zeros_like(acc)

        acc[...] += product(a_ref, b_ref)

        @pl.when(k == nk - 1)
        def _():
            o_ref[...] = acc[...].astype(o_ref.dtype)

    ins = [a, b] + ([add] if has_add else [])
    specs = [a_spec, b_spec] + ([o_spec] if has_add else [])
    return _pallas(
        body_single if nk == 1 else body, name=name, grid=(M // tm, N // tn, nk), in_specs=specs, out_specs=o_spec,
        out_shape=_sds((M, N), out_dtype), scratch_shapes=[] if nk == 1 else [pltpu.VMEM((tm, tn), F32)],
        compiler_params=_params(("parallel", "parallel", "arbitrary")),
    )(*ins)


FFN_TILE = 256


def _ffn_interleave(w):
    D, F2 = w.shape
    nj = F2 // (2 * FFN_TILE)

    def body(a_ref, b_ref, o_ref):
        o_ref[:, :FFN_TILE] = a_ref[...]
        o_ref[:, FFN_TILE:] = b_ref[...]

    return _pallas(
        body, name="ffn_interleave", grid=(nj,),
        in_specs=[pl.BlockSpec((D, FFN_TILE), lambda j: (0, j)), pl.BlockSpec((D, FFN_TILE), lambda j: (0, j + nj))],
        out_specs=pl.BlockSpec((D, 2 * FFN_TILE), lambda j: (0, j)), out_shape=_sds((D, F2), w.dtype),
        compiler_params=_params(("parallel",)),
    )(w, w)


def _ffn_deinterleave(w):
    D, F2 = w.shape
    nj = F2 // (2 * FFN_TILE)

    def body(w_ref, o_ref):
        o_ref[0] = w_ref[:, :FFN_TILE]
        o_ref[1] = w_ref[:, FFN_TILE:]

    return _pallas(
        body, name="ffn_deinterleave", grid=(nj,), in_specs=[pl.BlockSpec((D, 2 * FFN_TILE), lambda j: (0, j))],
        out_specs=pl.BlockSpec((2, D, FFN_TILE), lambda j: (0, 0, j)), out_shape=_sds((2, D, F2 // 2), w.dtype),
        compiler_params=_params(("parallel",)),
    )(w)


def _ffn_specs(M, D, F):
    tm = _tile(M, 1024, 128)
    x_spec = pl.BlockSpec((tm, D), lambda i, j: (i, 0))
    wi_spec = pl.BlockSpec((D, 2 * FFN_TILE), lambda i, j: (0, j))
    wo_spec = pl.BlockSpec((FFN_TILE, D), lambda i, j: (j, 0))
    return tm, F // FFN_TILE, x_spec, wi_spec, wo_spec


def _ffn_fwd(name, xn, w_in, w_out):
    M, D = xn.shape
    tm, nj, x_spec, wi_spec, wo_spec = _ffn_specs(M, D, w_out.shape[0])

    def body(x_ref, wi_ref, wo_ref, y_ref):
        j = pl.program_id(1)
        z = _dot(x_ref[...], wi_ref[...])
        hid = (jax.nn.silu(z[:, :FFN_TILE]) * z[:, FFN_TILE:]).astype(BF16)
        part = _dot(hid, wo_ref[...])

        @pl.when(j == 0)
        def _():
            y_ref[...] = part

        @pl.when(j > 0)
        def _():
            y_ref[...] += part

    return _pallas(
        body, name=name, grid=(M // tm, nj), in_specs=[x_spec, wi_spec, wo_spec], out_specs=x_spec,
        out_shape=_sds((M, D), F32), compiler_params=_params(("parallel", "arbitrary")),
    )(xn, w_in, w_out)


def _ffn_mid_bwd(name, xn, dy, w_in, w_out):
    M, D = xn.shape
    F = w_out.shape[0]
    tm, nj, x_spec, wi_spec, wo_spec = _ffn_specs(M, D, F)

    def body(x_ref, dy_ref, wi_ref, wo_ref, h_ref, dz_ref):
        z = _dot(x_ref[...], wi_ref[...])
        a, b = z[:, :FFN_TILE], z[:, FFN_TILE:]
        dh = _dot_nt(dy_ref[...], wo_ref[...])
        sig = jax.nn.sigmoid(a)
        act = a * sig
        h_ref[...] = (act * b).astype(BF16)
        dz_ref[:, :FFN_TILE] = (dh * b * (sig * (1.0 + a * (1.0 - sig)))).astype(BF16)
        dz_ref[:, FFN_TILE:] = (dh * act).astype(BF16)

    return _pallas(
        body, name=name, grid=(M // tm, nj), in_specs=[x_spec, x_spec, wi_spec, wo_spec],
        out_specs=[pl.BlockSpec((tm, FFN_TILE), lambda i, j: (i, j)),
                   pl.BlockSpec((tm, 2 * FFN_TILE), lambda i, j: (i, j))],
        out_shape=[_sds((M, F), BF16), _sds((M, 2 * F), BF16)],
        compiler_params=_params(("parallel", "parallel")),
    )(xn, dy, w_in, w_out)


class _Rows:
    def __init__(self, name, f, B, L, n, tm, arrays, pieces, samp, samp_pieces, glob, out_arrays, out_pieces):
        self.name, self.f, self.B, self.L, self.n, self.tm = name, f, B, L, n, tm
        self.arrays, self.pieces, self.samp, self.samp_pieces, self.glob = arrays, pieces, samp, samp_pieces, glob
        self.out_arrays, self.out_pieces = out_arrays, out_pieces
        self.nx = n // tm
        self.grid = (B, L // tm)

    def _row_spec(self, C, batched=True):
        tm = self.tm
        if batched:
            return pl.BlockSpec((None, tm, C), lambda b, i: (b, i, 0))
        return pl.BlockSpec((tm, C), lambda b, i: (i, 0))

    def _in_specs(self, unread=()):
        nx, tm = self.nx, self.tm
        specs = [self._row_spec(a.shape[-1], a.ndim == 3) for a in self.arrays]
        for ai in unread:
            specs[ai] = pl.BlockSpec((None, tm, self.arrays[ai].shape[-1]), lambda b, i: (0, 0, 0))
        specs += [pl.BlockSpec((None, None) + s.shape[2:], lambda b, i: (b, i // nx, 0, 0)) for s in self.samp]
        specs += [pl.BlockSpec(g.shape, lambda b, i, nd=g.ndim: (0,) * nd) for g in self.glob]
        return specs

    def _load(self, a_refs, s_refs, g_refs):
        args = [a_refs[ai][:, cs:cs + cw].astype(F32) for ai, cs, cw in self.pieces]
        args += [s_refs[si][r:r + 1, :].astype(F32) for si, r in self.samp_pieces]
        args += [g[...].astype(F32) for g in g_refs]
        return args

    def fwd(self):
        na, ns, ng = len(self.arrays), len(self.samp), len(self.glob)

        def body(*refs):
            a_refs, s_refs, g_refs = refs[:na], refs[na:na + ns], refs[na + ns:na + ns + ng]
            o_refs = refs[na + ns + ng:]
            outs = self.f(*self._load(a_refs, s_refs, g_refs))
            for (oi, cs, cw), o in zip(self.out_pieces, outs):
                o_refs[oi][:, cs:cs + cw] = o.astype(o_refs[oi].dtype)

        return _pallas(
            body, name=self.name + "_fwd", grid=self.grid, in_specs=self._in_specs(),
            out_specs=[self._row_spec(C) for C, _ in self.out_arrays],
            out_shape=[_sds((self.B, self.L, C), d) for C, d in self.out_arrays],
            compiler_params=_params(("parallel", "parallel")),
        )(*self.arrays, *self.samp, *self.glob)

    def bwd(self, cts, grad_arrays, grad_glob=(), add=None, samp=None, unread=()):
        samp = self.samp if samp is None else samp
        na, ns, ng, nc = len(self.arrays), len(self.samp), len(self.glob), len(cts)
        ct_off = [c[1] if isinstance(c, tuple) else 0 for c in cts]
        cts = [c[0] if isinstance(c, tuple) else c for c in cts]
        add = add or {}
        add_keys = list(add)
        g_idx = list(grad_arrays)
        nx, B = self.nx, self.B
        n_in = na + ns + ng + nc + len(add_keys)
        n_pieces, n_sp = len(self.pieces), len(self.samp_pieces)

        def body(*refs):
            a_refs, s_refs, g_refs = refs[:na], refs[na:na + ns], refs[na + ns:na + ns + ng]
            c_refs = refs[na + ns + ng:na + ns + ng + nc]
            add_refs = refs[na + ns + ng + nc:n_in]
            d_refs = refs[n_in:n_in + len(g_idx)]
            ds_refs = refs[n_in + len(g_idx):n_in + len(g_idx) + n_sp]
            dg_refs = refs[n_in + len(g_idx) + n_sp:]
            b, i = pl.program_id(0), pl.program_id(1)
            args = self._load(a_refs, s_refs, g_refs)
            _, vjp = jax.vjp(lambda *xs: tuple(self.f(*xs)), *args)
            grads = vjp(tuple(c_refs[oi][:, ct_off[oi] + cs:ct_off[oi] + cs + cw].astype(F32)
                              for oi, cs, cw in self.out_pieces))
            for k, ai in enumerate(g_idx):
                covered = sum(cw for pa, _, cw in self.pieces if pa == ai)
                if covered < self.arrays[ai].shape[-1]:
                    d_refs[k][...] = jnp.zeros_like(d_refs[k])
                for (pa, cs, cw), gr in zip(self.pieces, grads[:n_pieces]):
                    if pa == ai:
                        if ai in add:
                            gr = gr + add_refs[add_keys.index(ai)][:, cs:cs + cw]
                        d_refs[k][:, cs:cs + cw] = gr.astype(d_refs[k].dtype)

            @pl.when((i == 0) | (i == nx))
            def _():
                for r in ds_refs:
                    r[...] = jnp.zeros_like(r)

            for r, gr in zip(ds_refs, grads[n_pieces:n_pieces + n_sp]):
                r[...] += gr

            @pl.when((b == 0) & (i == 0))
            def _():
                for r in dg_refs:
                    r[...] = jnp.zeros_like(r)

            for r, gi in zip(dg_refs, grad_glob):
                r[...] += grads[n_pieces + n_sp + gi]

        in_specs = self._in_specs(unread) + [self._row_spec(c.shape[-1]) for c in cts]
        in_specs += [self._row_spec(add[k].shape[-1]) for k in add_keys]
        out_specs = [self._row_spec(self.arrays[ai].shape[-1]) for ai in g_idx]
        out_shape = [_sds(self.arrays[ai].shape, grad_arrays[ai]) for ai in g_idx]
        for si, _ in self.samp_pieces:
            C = self.samp[si].shape[-1]
            out_specs.append(pl.BlockSpec((None, None, 1, C), lambda b, i: (b, i // nx, 0, 0)))
            out_shape.append(_sds((B, 2, 1, C), F32))
        for gi in grad_glob:
            g = self.glob[gi]
            out_specs.append(pl.BlockSpec(g.shape, lambda b, i, nd=g.ndim: (0,) * nd))
            out_shape.append(_sds(g.shape, F32))
        outs = _pallas(
            body, name=self.name + "_bwd", grid=self.grid, in_specs=in_specs, out_specs=out_specs,
            out_shape=out_shape, compiler_params=_params(("arbitrary", "arbitrary")),
        )(*self.arrays, *samp, *self.glob, *cts, *[add[k] for k in add_keys])
        ng_ = len(g_idx)
        return outs[:ng_], outs[ng_:ng_ + n_sp], outs[ng_ + n_sp:]


def _rms(x, g):
    return x * lax.rsqrt(jnp.mean(x * x, axis=-1, keepdims=True) + EPS) * g


@jax.custom_vjp
def _swap_halves(x):
    w = x.shape[-1]
    lane = lax.broadcasted_iota(jnp.int32, x.shape, x.ndim - 1)
    up = pltpu.roll(x, w - ROPE_DIM // 2, x.ndim - 1)
    down = pltpu.roll(x, ROPE_DIM // 2, x.ndim - 1)
    return jnp.where(lane % ROPE_DIM < ROPE_DIM // 2, up, down)


_swap_halves.defvjp(lambda x: (_swap_halves(x), None), lambda _, ct: (_swap_halves(ct),))


@jax.custom_vjp
def _bdot(a, b):
    return jnp.dot(a.astype(BF16), b.astype(BF16), preferred_element_type=F32)


_bdot.defvjp(lambda a, b: (_bdot(a, b), (a, b)),
             lambda res, ct: (_dot_nt(ct.astype(BF16), res[1].astype(BF16)), _dot_tn(res[0].astype(BF16), ct.astype(BF16))))


def _rope(x, cos2, sin2):
    return x * cos2 + _swap_halves(x) * sin2


def _f_pre(s, shift, scale, g):
    return (_rms(s, g) * (1.0 + scale) + shift,)


def _f_res(s, y, gate):
    return (s + gate * y,)


def _f_res_pre(s, y, gate, shift, scale, g):
    s1 = s + gate * y
    return s1, _rms(s1, g) * (1.0 + scale) + shift


def _f_ab_pre(*args):
    kv_lat, q_lat, kpe = args[0:3]
    us, vs = args[3:7], args[7:11]
    cos2, sin2 = args[11:13]
    kv_norm, q_norm = args[13:15]
    vns, wss, bss = args[15:19], args[19:23], args[23:27]
    outs = [_rms(kv_lat, kv_norm), _rms(q_lat, q_norm), _rope(kpe, cos2, sin2)]
    for u, v, vn, ws, bs in zip(us, vs, vns, wss, bss):
        vg = _rms(jax.nn.gelu(v), vn)
        mixed = _bdot(ws, vg) + bs
        outs.append(jax.nn.gelu(u) * mixed)
    return tuple(outs)


def _f_q_rope(qn, qr, cos2, sin2):
    return qn, _rope(qr, cos2, sin2)


def _f_cd_pre(rk, rv, sk, sv, rq, rg, sq, c256, s256, c128, s128, c512, s512):
    return (_rope(rq, c256, s256), _rope(rk * RET_K_SCALE, c256, s256), rv,
            _rope(sq, c512, s512), _rope(sk, c128, s128), sv, rg)


def _f_cd_merge(*args):
    ys, rgs, rns = args[0:4], args[4:8], args[8:12]
    return tuple(_rms(y, rn) * jax.nn.silu(rg) for y, rg, rn in zip(ys, rgs, rns))


def _dot_nt(a, b):
    return lax.dot_general(a, b, (((1,), (1,)), ((), ())), preferred_element_type=F32)


def _dot_tn(a, b):
    return lax.dot_general(a, b, (((0,), (0,)), ((), ())), preferred_element_type=F32)


def _dot(a, b):
    return jnp.dot(a, b, preferred_element_type=F32)


def _tile_spec(tq, C):
    return pl.BlockSpec((None, tq, C), lambda b, i: (b, i, 0))


def _full_spec(L, C):
    return pl.BlockSpec((None, L, C), lambda b, i: (b, 0, 0))


def _mla_valid(i, tq, L, n):
    qpos = i * tq + lax.broadcasted_iota(jnp.int32, (tq, 1), 0)
    kpos = lax.broadcasted_iota(jnp.int32, (1, L), 1)
    return (qpos < n) | (kpos >= n)


def _mla_half(h):
    lane = lax.broadcasted_iota(jnp.int32, (1, 128), 1)
    return (lane < 64) if h % 2 == 0 else (lane >= 64)


def _mla_query(q_ref, h):
    pair = q_ref[:, 512 + 128 * (h // 2):640 + 128 * (h // 2)]
    return jnp.concatenate([q_ref[:, 128 * h:128 * h + 128], jnp.where(_mla_half(h), pair, jnp.zeros_like(pair))], axis=1)


def _mla_fill_keys(kcat, kv_ref, kpe_ref):
    kp = kpe_ref[...]
    for h in range(4):
        kcat[h, :, 0:128] = kv_ref[:, 256 * h:256 * h + 128]
        kcat[h, :, 128:256] = jnp.where(_mla_half(h), kp, jnp.zeros_like(kp))


def _mla_fwd(q, kv, kpe, n, tq):
    B, L, _ = q.shape

    def body(q_ref, kv_ref, kpe_ref, o_ref, lse_ref, kcat):
        i = pl.program_id(1)

        @pl.when(i == 0)
        def _():
            _mla_fill_keys(kcat, kv_ref, kpe_ref)

        valid = _mla_valid(i, tq, L, n)
        for h in range(4):
            s = _dot_nt(_mla_query(q_ref, h), kcat[h])
            s = jnp.where(valid, s * MLA_SCALE, NEG_INF)
            m = jnp.max(s, axis=1, keepdims=True)
            e = jnp.exp(s - m)
            l = jnp.sum(e, axis=1, keepdims=True)
            p = (e * (1.0 / l)).astype(BF16)
            o_ref[:, 128 * h:128 * h + 128] = _dot(p, kv_ref[:, 256 * h + 128:256 * h + 256]).astype(o_ref.dtype)
            lse_ref[:, h:h + 1] = m + jnp.log(l)

    return _pallas(
        body, name="mla_fwd", grid=(B, L // tq),
        in_specs=[_tile_spec(tq, 768), _full_spec(L, 1024), _full_spec(L, 128)],
        out_specs=[_tile_spec(tq, 512), _tile_spec(tq, 4)],
        out_shape=[_sds((B, L, 512), BF16), _sds((B, L, 4), F32)],
        scratch_shapes=[pltpu.VMEM((4, L, 256), BF16)],
        compiler_params=_params(("parallel", "arbitrary")),
    )(q, kv, kpe)


def _mla_bwd(q, kv, kpe, lse, do, n, tq):
    B, L, _ = q.shape

    def body(q_ref, kv_ref, kpe_ref, lse_ref, do_ref, dq_ref, dkv_ref, dkpe_ref, kcat):
        i = pl.program_id(1)

        @pl.when(i == 0)
        def _():
            dkv_ref[...] = jnp.zeros_like(dkv_ref)
            dkpe_ref[...] = jnp.zeros_like(dkpe_ref)
            _mla_fill_keys(kcat, kv_ref, kpe_ref)

        valid = _mla_valid(i, tq, L, n)
        rope_pair = None
        for h in range(4):
            qc, kc = _mla_query(q_ref, h), kcat[h]
            v = kv_ref[:, 256 * h + 128:256 * h + 256]
            s = jnp.where(valid, _dot_nt(qc, kc) * MLA_SCALE, NEG_INF)
            p = jnp.exp(s - lse_ref[:, h:h + 1])
            doh = do_ref[:, 128 * h:128 * h + 128].astype(BF16)
            dp = _dot_nt(doh, v)
            delta = jnp.sum(p * dp, axis=1, keepdims=True)
            ds = (p * (dp - delta) * MLA_SCALE).astype(BF16)
            dqc = _dot(ds, kc)
            dq_ref[:, 128 * h:128 * h + 128] = dqc[:, 0:128]
            if h % 2 == 0:
                rope_pair = dqc[:, 128:256]
            else:
                dq_ref[:, 512 + 128 * (h // 2):640 + 128 * (h // 2)] = jnp.where(_mla_half(h), dqc[:, 128:256], rope_pair)
            dkc = _dot_tn(ds, qc)
            dkv_ref[:, 256 * h:256 * h + 128] += dkc[:, 0:128]
            dkpe_ref[...] += dkc[:, 128:256]
            dkv_ref[:, 256 * h + 128:256 * h + 256] += _dot_tn(p.astype(BF16), doh)

    return _pallas(
        body, name="mla_bwd", grid=(B, L // tq),
        in_specs=[_tile_spec(tq, 768), _full_spec(L, 1024), _full_spec(L, 128), _tile_spec(tq, 4),
                  pl.BlockSpec((None, tq, 512), lambda b, i: (b, i, 0))],
        out_specs=[_tile_spec(tq, 768), _full_spec(L, 1024), _full_spec(L, 128)],
        out_shape=[_sds((B, L, 768), F32), _sds((B, L, 1024), F32), _sds((B, L, 128), F32)],
        scratch_shapes=[pltpu.VMEM((4, L, 256), BF16)],
        compiler_params=_params(("parallel", "arbitrary")),
    )(q, kv, kpe, lse, do)


def _swa_window(i, tq, n):
    W = min(tq + 2 * SWA_WINDOW, n)
    lo = jnp.clip(i * tq - SWA_WINDOW, 0, n - W)
    return pl.multiple_of(lo, 128), W


def _swa_mask(i, lo, tq, W, n):
    qpos = i * tq + lax.broadcasted_iota(jnp.int32, (tq, 1), 0)
    kpos = lo + lax.broadcasted_iota(jnp.int32, (1, W), 1)
    return (jnp.abs(qpos - kpos) <= SWA_WINDOW) & (qpos < n)


def _swa_fwd(q, k, v, sink, n, tq):
    B, L, _ = q.shape

    def body(q_ref, k_ref, v_ref, sink_ref, o_ref, lse_ref):
        i = pl.program_id(1)
        lo, W = _swa_window(i, tq, n)
        mask = _swa_mask(i, lo, tq, W, n)
        kl, vl = k_ref[pl.ds(lo, W), :], v_ref[pl.ds(lo, W), :]
        kc, vc = k_ref[n:L, :], v_ref[n:L, :]
        for h in range(8):
            g = h // 4
            qh = q_ref[:, 64 * h:64 * h + 64]
            s1 = jnp.where(mask, _dot_nt(qh, kl[:, 64 * g:64 * g + 64]) * SWA_SCALE, NEG_INF)
            s2 = _dot_nt(qh, kc[:, 64 * g:64 * g + 64]) * SWA_SCALE
            sk = sink_ref[0:1, h:h + 1]
            m = jnp.maximum(jnp.maximum(jnp.max(s1, axis=1, keepdims=True), jnp.max(s2, axis=1, keepdims=True)), sk)
            e1, e2 = jnp.exp(s1 - m), jnp.exp(s2 - m)
            l = jnp.sum(e1, axis=1, keepdims=True) + jnp.sum(e2, axis=1, keepdims=True) + jnp.exp(sk - m)
            r = 1.0 / l
            o = _dot((e1 * r).astype(BF16), vl[:, 64 * g:64 * g + 64]) + _dot((e2 * r).astype(BF16), vc[:, 64 * g:64 * g + 64])
            o_ref[:, 64 * h:64 * h + 64] = o.astype(o_ref.dtype)
            lse_ref[:, h:h + 1] = m + jnp.log(l)

    return _pallas(
        body, name="swa_fwd", grid=(B, L // tq),
        in_specs=[_tile_spec(tq, 512), _full_spec(L, 128), _full_spec(L, 128), pl.BlockSpec((1, 8), lambda b, i: (0, 0))],
        out_specs=[_tile_spec(tq, 512), _tile_spec(tq, 8)],
        out_shape=[_sds((B, L, 512), BF16), _sds((B, L, 8), F32)],
        compiler_params=_params(("parallel", "arbitrary")),
    )(q, k, v, sink)


def _swa_bwd(q, k, v, sink, lse, do, n, tq):
    B, L, _ = q.shape

    def body(q_ref, k_ref, v_ref, sink_ref, lse_ref, do_ref, dq_ref, dk_ref, dv_ref, dsink_ref):
        i = pl.program_id(1)

        @pl.when(i == 0)
        def _():
            dk_ref[...] = jnp.zeros_like(dk_ref)
            dv_ref[...] = jnp.zeros_like(dv_ref)
            dsink_ref[...] = jnp.zeros_like(dsink_ref)

        lo, W = _swa_window(i, tq, n)
        mask = _swa_mask(i, lo, tq, W, n)
        kl, vl = k_ref[pl.ds(lo, W), :], v_ref[pl.ds(lo, W), :]
        kc, vc = k_ref[n:L, :], v_ref[n:L, :]
        for h in range(8):
            g = h // 4
            cols = slice(64 * g, 64 * g + 64)
            qh = q_ref[:, 64 * h:64 * h + 64]
            lse_h = lse_ref[:, h:h + 1]
            s1 = jnp.where(mask, _dot_nt(qh, kl[:, cols]) * SWA_SCALE, NEG_INF)
            s2 = _dot_nt(qh, kc[:, cols]) * SWA_SCALE
            p1, p2 = jnp.exp(s1 - lse_h), jnp.exp(s2 - lse_h)
            ps = jnp.exp(sink_ref[0:1, h:h + 1] - lse_h)
            doh = do_ref[:, 64 * h:64 * h + 64].astype(BF16)
            dp1, dp2 = _dot_nt(doh, vl[:, cols]), _dot_nt(doh, vc[:, cols])
            delta = jnp.sum(p1 * dp1, axis=1, keepdims=True) + jnp.sum(p2 * dp2, axis=1, keepdims=True)
            ds1 = (p1 * (dp1 - delta) * SWA_SCALE).astype(BF16)
            ds2 = (p2 * (dp2 - delta) * SWA_SCALE).astype(BF16)
            dq_ref[:, 64 * h:64 * h + 64] = _dot(ds1, kl[:, cols]) + _dot(ds2, kc[:, cols])
            dk_ref[pl.ds(lo, W), cols] += _dot_tn(ds1, qh)
            dk_ref[n:L, cols] += _dot_tn(ds2, qh)
            dv_ref[pl.ds(lo, W), cols] += _dot_tn(p1.astype(BF16), doh)
            dv_ref[n:L, cols] += _dot_tn(p2.astype(BF16), doh)
            dsink_ref[0:1, h:h + 1] += jnp.sum(-ps * delta, axis=0, keepdims=True)

    return _pallas(
        body, name="swa_bwd", grid=(B, L // tq),
        in_specs=[_tile_spec(tq, 512), _full_spec(L, 128), _full_spec(L, 128), pl.BlockSpec((1, 8), lambda b, i: (0, 0)),
                  _tile_spec(tq, 8), pl.BlockSpec((None, tq, 512), lambda b, i: (b, i, 1))],
        out_specs=[_tile_spec(tq, 512), _full_spec(L, 128), _full_spec(L, 128),
                   pl.BlockSpec((None, 1, 8), lambda b, i: (b, 0, 0))],
        out_shape=[_sds((B, L, 512), F32), _sds((B, L, 128), F32), _sds((B, L, 128), F32), _sds((B, 1, 8), F32)],
        compiler_params=_params(("parallel", "arbitrary")),
    )(q, k, v, sink, lse, do)


def _ret_decay(i, tq, L, n):
    qi = i * tq + lax.broadcasted_iota(jnp.int32, (tq, 1), 0)
    q_latent = qi < n
    d_x = (qi - lax.broadcasted_iota(jnp.int32, (1, n), 1)).astype(F32)
    kc = lax.broadcasted_iota(jnp.int32, (1, L - n), 1)
    d_hf = (jnp.where(q_latent, qi, qi - L) - (kc - (L - n))).astype(F32)
    d_hb = (n + kc - qi).astype(F32)
    return q_latent, d_x, d_hf, d_hb


def _ret_head(a, h):
    lane = lax.broadcasted_iota(jnp.int32, (1, a.shape[1]), 1)
    return jnp.where(lane // 64 == h, a, jnp.zeros_like(a))


def _ret_mask(dist, lg):
    return jnp.where(dist >= 0.0, jnp.exp(lg * jnp.maximum(dist, 0.0)), 0.0)


def _ret_weights(q_latent, d_x, d_hf, d_hb, lg_f, lg_b, with_grad):
    e_x = jnp.exp(jnp.where(d_x >= 0.0, lg_f, -lg_b) * d_x)
    e_x = jnp.where(q_latent, e_x, 0.0)
    dec_x = jnp.where(d_x == 0.0, 2.0 * e_x, e_x)
    m_f, m_b = _ret_mask(d_hf, lg_f), _ret_mask(d_hb, lg_b)
    dec_h = m_f + m_b
    if not with_grad:
        return (dec_x, dec_h), None, None
    w_f = (e_x * jnp.maximum(d_x, 0.0), m_f * jnp.maximum(d_hf, 0.0))
    w_b = (e_x * jnp.maximum(-d_x, 0.0), m_b * jnp.maximum(d_hb, 0.0))
    return (dec_x, dec_h), w_f, w_b


def _ret_fwd(q, k, v, lg, n, tq):
    B, L, _ = q.shape

    def body(q_ref, k_ref, v_ref, lg_ref, y_ref):
        dist = _ret_decay(pl.program_id(1), tq, L, n)
        for h in range(4):
            dec, _, _ = _ret_weights(*dist, lg_ref[0:1, h:h + 1], lg_ref[1:2, h:h + 1], False)
            qh = _ret_head(q_ref[...], h)
            y = None
            for rows, dec_r in zip((slice(0, n), slice(n, L)), dec):
                a = _dot_nt(qh, k_ref[rows, :]) * dec_r
                part = _dot(a.astype(BF16), v_ref[rows, 128 * h:128 * h + 128])
                y = part if y is None else y + part
            y_ref[:, 128 * h:128 * h + 128] = y

    return _pallas(
        body, name="ret_fwd", grid=(B, L // tq),
        in_specs=[_tile_spec(tq, 256), _full_spec(L, 256), _full_spec(L, 512), pl.BlockSpec((2, 4), lambda b, i: (0, 0))],
        out_specs=_tile_spec(tq, 512), out_shape=_sds((B, L, 512), F32),
        compiler_params=_params(("parallel", "arbitrary")),
    )(q, k, v, lg)


def _ret_bwd(q, k, v, lg, dy, n, tq):
    B, L, _ = q.shape

    def body(q_ref, k_ref, v_ref, lg_ref, dy_ref, dq_ref, dk_ref, dv_ref, dlg_ref):
        i = pl.program_id(1)

        @pl.when(i == 0)
        def _():
            dk_ref[...] = jnp.zeros_like(dk_ref)
            dv_ref[...] = jnp.zeros_like(dv_ref)
            dlg_ref[...] = jnp.zeros_like(dlg_ref)

        def total(a):
            return jnp.sum(jnp.sum(a, axis=1, keepdims=True), axis=0, keepdims=True)

        dist = _ret_decay(i, tq, L, n)
        dq = None
        for h in range(4):
            vc = slice(128 * h, 128 * h + 128)
            dec, w_f, w_b = _ret_weights(*dist, lg_ref[0:1, h:h + 1], lg_ref[1:2, h:h + 1], True)
            qh = _ret_head(q_ref[...], h)
            dyh = dy_ref[:, vc].astype(BF16)
            dqh = None
            for rows, dec_r, wf_r, wb_r in zip((slice(0, n), slice(n, L)), dec, w_f, w_b):
                k_all, vh = k_ref[rows, :], v_ref[rows, vc]
                s = _dot_nt(qh, k_all)
                gr = _dot_nt(dyh, vh)
                ds = (gr * dec_r).astype(BF16)
                part = _dot(ds, k_all)
                dqh = part if dqh is None else dqh + part
                dk_ref[rows, :] += _dot_tn(ds, qh)
                dv_ref[rows, vc] += _dot_tn((s * dec_r).astype(BF16), dyh)
                gs = gr * s
                dlg_ref[0:1, h:h + 1] += total(gs * wf_r)
                dlg_ref[1:2, h:h + 1] += total(gs * wb_r)
            dqh = _ret_head(dqh, h)
            dq = dqh if dq is None else dq + dqh
        dq_ref[...] = dq

    return _pallas(
        body, name="ret_bwd", grid=(B, L // tq),
        in_specs=[_tile_spec(tq, 256), _full_spec(L, 256), _full_spec(L, 512), pl.BlockSpec((2, 4), lambda b, i: (0, 0)),
                  _tile_spec(tq, 512)],
        out_specs=[_tile_spec(tq, 256), _full_spec(L, 256), _full_spec(L, 512),
                   pl.BlockSpec((None, 2, 4), lambda b, i: (b, 0, 0))],
        out_shape=[_sds((B, L, 256), F32), _sds((B, L, 256), F32), _sds((B, L, 512), F32), _sds((B, 2, 4), F32)],
        compiler_params=_params(("parallel", "arbitrary")),
    )(q, k, v, lg, dy)


def _loss_head(s, target, g, n, tm):
    B, L, D = s.shape
    nx = n // tm

    def body(s_ref, t_ref, g_ref, ds_ref, dg_ref, loss_ref):
        b, i = pl.program_id(0), pl.program_id(1)

        @pl.when((b == 0) & (i == 0))
        def _():
            dg_ref[...] = jnp.zeros_like(dg_ref)
            loss_ref[...] = jnp.zeros_like(loss_ref)

        @pl.when(i < nx)
        def _():
            y, vjp = jax.vjp(_rms, s_ref[...], g_ref[...])
            err = y - t_ref[...]
            d_s, d_g = vjp(err * (1.0 / D))
            ds_ref[...] = d_s
            dg_ref[...] += d_g
            part = jnp.sum(jnp.sum(err * err, axis=1, keepdims=True), axis=0, keepdims=True) * (0.5 / D)
            loss_ref[...] += jnp.broadcast_to(part, loss_ref.shape)

        @pl.when(i >= nx)
        def _():
            ds_ref[...] = jnp.zeros_like(ds_ref)

    return _pallas(
        body, name="loss_head", grid=(B, L // tm),
        in_specs=[pl.BlockSpec((None, tm, D), lambda b, i: (b, i, 0)),
                  pl.BlockSpec((None, tm, D), lambda b, i: (b, jnp.minimum(i, nx - 1), 0)),
                  pl.BlockSpec((1, D), lambda b, i: (0, 0))],
        out_specs=[pl.BlockSpec((None, tm, D), lambda b, i: (b, i, 0)), pl.BlockSpec((1, D), lambda b, i: (0, 0)),
                   pl.BlockSpec((1, 128), lambda b, i: (0, 0))],
        out_shape=[_sds((B, L, D), F32), _sds((1, D), F32), _sds((1, 128), F32)],
        compiler_params=_params(("arbitrary", "arbitrary")),
    )(s, target, g)


def _ada_fwd(c_all, ada_w, ada_b):
    NL, D, Ns = ada_w.shape
    R = c_all.shape[0]

    def body(c_ref, w_ref, b_ref, o_ref):
        cond = jax.nn.silu(c_ref[...]).astype(BF16)
        o_ref[...] = _dot(cond, w_ref[...].astype(BF16)) + b_ref[...]

    return _pallas(
        body, name="ada_fwd", grid=(NL,),
        in_specs=[pl.BlockSpec((R, D), lambda l: (0, 0)), pl.BlockSpec((None, D, Ns), lambda l: (l, 0, 0)),
                  pl.BlockSpec((None, 1, Ns), lambda l: (l, 0, 0))],
        out_specs=pl.BlockSpec((None, R, Ns), lambda l: (l, 0, 0)), out_shape=_sds((NL, R, Ns), F32),
        compiler_params=_params(("parallel",)),
    )(c_all, ada_w, ada_b)


def _ada_bwd(c_all, ada_w, dmods):
    NL, D, Ns = ada_w.shape
    R = c_all.shape[0]

    def body(c_ref, w_ref, dm_ref, dw_ref, dc_ref):
        cond = jax.nn.silu(c_ref[...]).astype(BF16)
        dm = dm_ref[...].astype(BF16)
        dw_ref[...] = _dot_tn(cond, dm)
        dc_ref[...] = _dot_nt(dm, w_ref[...].astype(BF16))

    return _pallas(
        body, name="ada_bwd", grid=(NL,),
        in_specs=[pl.BlockSpec((R, D), lambda l: (0, 0)), pl.BlockSpec((None, D, Ns), lambda l: (l, 0, 0)),
                  pl.BlockSpec((None, R, Ns), lambda l: (l, 0, 0))],
        out_specs=[pl.BlockSpec((None, D, Ns), lambda l: (l, 0, 0)), pl.BlockSpec((None, R, D), lambda l: (l, 0, 0))],
        out_shape=[_sds((NL, D, Ns), F32), _sds((NL, R, D), F32)],
        compiler_params=_params(("parallel",)),
    )(c_all, ada_w, dmods)


def _my_index():
    return 4 * lax.axis_index("x") + 2 * lax.axis_index("y") + lax.axis_index("c")


def _peer(k):
    x, y, c = lax.axis_index("x"), lax.axis_index("y"), lax.axis_index("c")
    kx, ky, kc = (k >> 2) & 1, (k >> 1) & 1, k & 1
    px, py, pc = (x + kx) % 2, (y + ky) % 2, (c + kc) % 2
    return (px, py, pc), 4 * px + 2 * py + pc


def _all_gather(name, shards):
    na = len(shards)
    hbm = pl.BlockSpec(memory_space=pl.ANY)

    def body(*refs):
        in_refs, out_refs = refs[:na], refs[na:2 * na]
        send_sems, recv_sems, local_sems = refs[2 * na:]
        me = _my_index()
        sib_id, sib = _peer(1)
        chips = [_peer(k) for k in (4, 2, 6)]
        sib_chips = [4 * px + 2 * py + (1 - pc) for (px, py, pc), _ in chips]

        def copy(a, k, slot, to, src=None):
            dst = out_refs[a].at[slot]
            return pltpu.make_async_remote_copy(
                src_ref=dst if src is None else src, dst_ref=dst, send_sem=send_sems.at[a, k],
                recv_sem=recv_sems.at[a, k], device_id=to, device_id_type=MESH)

        first, passed, mine = [], [], []
        for a in range(na):
            cp = pltpu.make_async_copy(in_refs[a], out_refs[a].at[me], local_sems.at[a])
            cp.start()
            mine.append(cp)
            first.append(copy(a, 0, me, sib_id, src=in_refs[a]))
            first += [copy(a, 1 + j, me, pid, src=in_refs[a]) for j, (pid, _) in enumerate(chips)]
        for cp in first:
            cp.start()
        for a in range(na):
            for j, (pid, pidx) in enumerate(chips):
                copy(a, 1 + j, pidx, pid).wait_recv()
                fwd = copy(a, 4 + j, pidx, sib_id)
                fwd.start()
                passed.append(fwd)
        for a in range(na):
            copy(a, 0, sib, sib_id).wait_recv()
            for j in range(3):
                copy(a, 4 + j, sib_chips[j], sib_id).wait_recv()
        for cp in first + passed:
            cp.wait_send()
        for cp in mine:
            cp.wait()

    return _pallas(
        body, name=name, in_specs=[hbm] * na, out_specs=[hbm] * na,
        out_shape=[_sds((N_DEV,) + s.shape, s.dtype) for s in shards],
        scratch_shapes=[pltpu.SemaphoreType.DMA((na, 7)), pltpu.SemaphoreType.DMA((na, 7)),
                        pltpu.SemaphoreType.DMA((na,))],
    )(*shards)


def _all_to_all(name, parts):
    na = len(parts)
    hbm = pl.BlockSpec(memory_space=pl.ANY)

    def body(*refs):
        in_refs, out_refs = refs[:na], refs[na:2 * na]
        send_sems, recv_sems, local_sems = refs[2 * na:]
        me = _my_index()
        copies = []
        for a in range(na):
            cp = pltpu.make_async_copy(in_refs[a].at[me], out_refs[a].at[me], local_sems.at[a])
            cp.start()
            copies.append(cp)
            for k in range(1, N_DEV):
                pid, pidx = _peer(k)
                cp = pltpu.make_async_remote_copy(
                    src_ref=in_refs[a].at[pidx], dst_ref=out_refs[a].at[me], send_sem=send_sems.at[a, k - 1],
                    recv_sem=recv_sems.at[a, k - 1], device_id=pid, device_id_type=MESH)
                cp.start()
                copies.append(cp)
        for cp in copies:
            cp.wait()

    return _pallas(
        body, name=name, in_specs=[hbm] * na, out_specs=[hbm] * na,
        out_shape=[_sds(p.shape, p.dtype) for p in parts],
        scratch_shapes=[pltpu.SemaphoreType.DMA((na, 7)), pltpu.SemaphoreType.DMA((na, 7)),
                        pltpu.SemaphoreType.DMA((na,))],
    )(*parts)


_HBM = pl.BlockSpec(memory_space=pltpu.HBM)
_SEM = pl.BlockSpec(memory_space=pltpu.SEMAPHORE)
_DATAFLOW = pltpu.SideEffectType.DATAFLOW_SIDE_EFFECTING


def _exchange_start(name, srcs, slotted, after=()):
    na = len(srcs)
    lands = [lax.empty((N_DEV,) + (s.shape[1:] if slotted else s.shape), s.dtype) for s in srcs]

    def body(*refs):
        src_refs, land_refs = refs[:na], refs[na:2 * na]
        outs = refs[2 * na + len(after):]
        send_sems, recv_sems, token = outs[:na], outs[na:2 * na], outs[4 * na]
        me = _my_index()
        for a in range(na):
            for k in range(1, N_DEV):
                pid, pidx = _peer(k)
                pltpu.make_async_remote_copy(
                    src_ref=src_refs[a].at[pidx] if slotted else src_refs[a], dst_ref=land_refs[a].at[me],
                    send_sem=send_sems[a], recv_sem=recv_sems[a], device_id=pid, device_id_type=MESH).start()
        token[...] = jnp.zeros_like(token)

    ops = [pltpu.with_memory_space_constraint(a, pltpu.HBM) for a in list(srcs) + lands]
    outs = _pallas(
        body, name=name,
        out_shape=[pltpu.SemaphoreType.DMA(())] * (2 * na) + [pltpu.HBM(a.shape, a.dtype) for a in ops]
        + [_sds((8, 128), F32)],
        in_specs=[_HBM] * (2 * na) + [pl.BlockSpec(memory_space=pl.ANY)] * len(after),
        out_specs=[_SEM] * (2 * na) + [_HBM] * (2 * na) + [pl.BlockSpec(memory_space=pltpu.VMEM)],
        input_output_aliases={a: 2 * na + a for a in range(2 * na)},
        compiler_params=pltpu.CompilerParams(has_side_effects=_DATAFLOW),
    )(*ops, *after)
    return (na, outs[:2 * na], outs[2 * na:4 * na]), outs[4 * na]


def _exchange_wait(name, handle, after):
    na, sems, thru = handle

    def body(*refs):
        land_refs = refs[na:2 * na]
        send_sems, recv_sems = refs[2 * na:3 * na], refs[3 * na:4 * na]
        me_id = (lax.axis_index("x"), lax.axis_index("y"), lax.axis_index("c"))
        for a in range(na):
            seven = land_refs[a].at[pl.ds(0, N_DEV - 1)]
            drain = pltpu.make_async_remote_copy(src_ref=seven, dst_ref=seven, send_sem=send_sems[a],
                                                 recv_sem=recv_sems[a], device_id=me_id, device_id_type=MESH)
            drain.wait_send()
            drain.wait_recv()

    outs = _pallas(
        body, name=name, out_shape=[pltpu.HBM(a.shape, a.dtype) for a in thru],
        in_specs=[_HBM] * (2 * na) + [_SEM] * (2 * na) + [pl.BlockSpec(memory_space=pl.ANY)],
        out_specs=[_HBM] * (2 * na), input_output_aliases={a: a for a in range(2 * na)},
        compiler_params=pltpu.CompilerParams(has_side_effects=_DATAFLOW),
    )(*thru, *sems, after)
    return outs[na:]


def _with_own(landed, own, slot_axis=0):
    me = _my_index()
    slot = lax.broadcasted_iota(jnp.int32, landed.shape, slot_axis)
    return jnp.where(slot == me, own, landed)


def _rope_tables(n, L, width):
    t = jnp.arange(n)
    row = (t // GRID_W).astype(F32)
    col = (t % GRID_W).astype(F32)
    n_freq = ROPE_DIM // 4
    freqs = ROPE_THETA ** (-jnp.arange(n_freq, dtype=F32) / n_freq)
    ang = jnp.concatenate([row[:, None] * freqs, col[:, None] * freqs], axis=-1)
    cos, sin = jnp.cos(ang), jnp.sin(ang)
    cos2 = jnp.concatenate([cos, cos], axis=-1)
    sin2 = jnp.concatenate([-sin, sin], axis=-1)
    cos2 = jnp.concatenate([cos2, jnp.ones((L - n, ROPE_DIM), F32)], axis=0)
    sin2 = jnp.concatenate([sin2, jnp.zeros((L - n, ROPE_DIM), F32)], axis=0)
    reps = width // ROPE_DIM
    return jnp.tile(cos2, (1, reps)), jnp.tile(sin2, (1, reps))


def _ab_perm(w):
    return jnp.concatenate([w[:, 0:256], w[:, 320:1600], w[:, 256:320], w[:, 256:320]], axis=1)


def _ab_unperm(g):
    rope_key = (g[:, 1536:1600].astype(F32) + g[:, 1600:1664].astype(F32)).astype(g.dtype)
    return jnp.concatenate([g[:, 0:256], rope_key, g[:, 256:1536]], axis=1)


def _wq_perm(w):
    return jnp.concatenate([w[:, 192 * h:192 * h + 128] for h in range(4)]
                           + [w[:, 192 * h + 128:192 * h + 192] for h in range(4)], axis=1)


def _wq_unperm(g):
    return jnp.concatenate([g[:, sl] for h in range(4)
                            for sl in (slice(128 * h, 128 * h + 128), slice(512 + 64 * h, 576 + 64 * h))], axis=1)


def _flat(a):
    return a.reshape(-1, a.shape[-1])


def _layer_weights(full, p):
    NL, NE, NO = len(full["ffn_in"]), len(full["ab_in"]), len(full["cd_in"])
    groups = range(4)

    def each(f, mats):
        return [None if w is None else f(w) for w in mats]

    return dict(
        norm_mix=[p["norm_mix"][l][None] for l in range(NL)], norm_ffn=[p["norm_ffn"][l][None] for l in range(NL)],
        norm_final=p["norm_final"][None],
        ffn_in=each(_ffn_interleave, full["ffn_in"]), ffn_out=list(full["ffn_out"]),
        ab_in=each(_ab_perm, full["ab_in"]), ab_out=list(full["ab_out"]),
        wq=each(_wq_perm, full["mla_wq_b"]), wkv=list(full["mla_wkv_b"]),
        kv_norm=[p["mla_kv_norm"][j][None] for j in range(NE)], q_norm=[p["mla_q_norm"][j][None] for j in range(NE)],
        v_norm=[[p["cmlp_v_norm"][j][None, 128 * g:128 * g + 128] for g in groups] for j in range(NE)],
        ws=[[p["cmlp_ws"][j, g] for g in groups] for j in range(NE)],
        bs=[[p["cmlp_bs"][j, g][:, None] for g in groups] for j in range(NE)],
        cd_in=list(full["cd_in"]), cd_out=list(full["cd_out"]),
        lg=[jnp.stack([jax.nn.log_sigmoid(p["ret_decay_fwd"][j]), jax.nn.log_sigmoid(p["ret_decay_bwd"][j])])
            for j in range(NO)],
        sink=[p["swa_sink"][j][None] for j in range(NO)],
        ret_norm=[[p["ret_norm"][j][None, 128 * g:128 * g + 128] for g in groups] for j in range(NO)],
    )


def _local_step(x, ctx, target, mods, W, later_weights=None, early_grads=None):
    B, n, D = x.shape
    m = ctx.shape[1]
    L = n + m
    NL = mods.shape[0]
    tm = min(256, m)
    tq = min(256, m)
    cos512, sin512 = _rope_tables(n, L, 512)
    s = jnp.concatenate([x, ctx], axis=1)
    saved = []

    def rows(name, f, arrays, pieces, samp, samp_pieces, glob, out_arrays, out_pieces, tile=tm):
        return _Rows(name, f, B, L, n, tile, arrays, pieces, samp, samp_pieces, glob, out_arrays, out_pieces)

    def full(width, start=0):
        return [(0, start, width)]

    for l in range(NL):
        j = l // 2
        even = l % 2 == 0
        md = mods[l]
        if l == 1 and later_weights is not None:
            W = later_weights(1, s)
        r = {}
        if l == 0:
            r["pre1"] = rows("pre_mix0", _f_pre, [s], full(D), [md], [(0, 0), (0, 1)], [W["norm_mix"][0]], [(D, BF16)], full(D))
            (xn,) = r["pre1"].fwd()
        r["xn"] = xn
        if even:
            z = _mm(f"ab_in{l}", _flat(xn), W["ab_in"][j]).reshape(B, L, 1664)
            pieces = [(0, 0, 256), (0, 256, 256), (0, 1536, 128)]
            pieces += [(0, 512 + 128 * g, 128) for g in range(4)] + [(0, 1024 + 128 * g, 128) for g in range(4)]
            pieces += [(1, 0, 128), (2, 0, 128)]
            glob = [W["kv_norm"][j], W["q_norm"][j]] + W["v_norm"][j] + W["ws"][j] + W["bs"][j]
            abp = rows(f"ab_pre{l}", _f_ab_pre, [z, cos512, sin512], pieces, [], [], glob,
                       [(256, BF16), (256, BF16), (128, BF16), (512, BF16)],
                       [(0, 0, 256), (1, 0, 256), (2, 0, 128)] + [(3, 128 * g, 128) for g in range(4)], tile=128)
            kvn, qn, kpe, cm = abp.fwd()
            kv = _mm(f"wkv{l}", _flat(kvn), W["wkv"][j], out_dtype=BF16).reshape(B, L, 1024)
            q0 = _mm(f"wq{l}", _flat(qn), W["wq"][j]).reshape(B, L, 768)
            qrp = rows(f"q_rope{l}", _f_q_rope, [q0, cos512, sin512], [(0, 0, 512), (0, 512, 256), (1, 0, 256), (2, 0, 256)],
                       [], [], [], [(768, BF16)], [(0, 0, 512), (0, 512, 256)])
            (q,) = qrp.fwd()
            o, lse = _mla_fwd(q, kv, kpe, n, tq)
            y = _mm(f"ab_out_a{l}", _flat(o), W["ab_out"][j][:512])
            y = _mm(f"ab_out_b{l}", _flat(cm), W["ab_out"][j][512:], add=y).reshape(B, L, D)
            r.update(z=z, abp=abp, kvn=kvn, qn=qn, kpe=kpe, cm=cm, kv=kv, qrp=qrp, q=q, o=o, lse=lse)
        else:
            z = _mm(f"cd_in{l}", _flat(xn), W["cd_in"][j]).reshape(B, L, 2304)
            pieces = [(0, 0, 256), (0, 256, 512), (0, 768, 128), (0, 896, 128), (0, 1024, 256), (0, 1280, 512), (0, 1792, 512)]
            pieces += [(1, 0, 256), (2, 0, 256), (1, 0, 128), (2, 0, 128), (1, 0, 512), (2, 0, 512)]
            cdp = rows(f"cd_pre{l}", _f_cd_pre, [z, cos512, sin512], pieces, [], [], [],
                       [(256, BF16), (256, BF16), (512, BF16), (512, BF16), (128, BF16), (128, BF16), (512, F32)],
                       [(k_, 0, w_) for k_, w_ in enumerate((256, 256, 512, 512, 128, 128, 512))])
            rq, rk, rv, sq, sk, sv, rg = cdp.fwd()
            yret = _ret_fwd(rq, rk, rv, W["lg"][j], n, tq)
            osw, lse = _swa_fwd(sq, sk, sv, W["sink"][j], n, tq)
            mrg = rows(f"cd_merge{l}", _f_cd_merge, [yret, rg],
                       [(0, 128 * g, 128) for g in range(4)] + [(1, 128 * g, 128) for g in range(4)], [], [],
                       W["ret_norm"][j], [(512, BF16)], [(0, 128 * g, 128) for g in range(4)])
            (yr,) = mrg.fwd()
            y = _mm(f"cd_out_a{l}", _flat(yr), W["cd_out"][j][:512])
            y = _mm(f"cd_out_b{l}", _flat(osw), W["cd_out"][j][512:], add=y).reshape(B, L, D)
            r.update(z=z, cdp=cdp, rq=rq, rk=rk, rv=rv, sq=sq, sk=sk, sv=sv, rg=rg, yret=yret, osw=osw, lse=lse,
                     mrg=mrg, yr=yr)
        two, outs2 = [(0, 0, D), (1, 0, D)], [(D, F32), (D, BF16)]
        r["mix_out"] = rows(f"mix_res_pre{l}", _f_res_pre, [s, y], two, [md], [(0, 2), (0, 3), (0, 4)],
                            [W["norm_ffn"][l]], outs2, two)
        s1, xn2 = r["mix_out"].fwd()
        if l == 0 and later_weights is not None:
            W = later_weights(0, s1)
        y2 = _ffn_fwd(f"ffn{l}", _flat(xn2), W["ffn_in"][l], W["ffn_out"][l]).reshape(B, L, D)
        if l < NL - 1:
            r["ffn_out"] = rows(f"ffn_res_pre{l}", _f_res_pre, [s1, y2], two, [md, mods[l + 1]], [(0, 5), (1, 0), (1, 1)],
                                [W["norm_mix"][l + 1]], outs2, two)
            s, xn = r["ffn_out"].fwd()
        else:
            r["ffn_out"] = rows(f"res_ffn{l}", _f_res, [s1, y2], two, [md], [(0, 5)], [], [(D, F32)], full(D))
            (s,) = r["ffn_out"].fwd()
        r["xn2"] = xn2
        saved.append(r)

    ds, d_norm_final, loss = _loss_head(s, target, W["norm_final"], n, tm)

    G = {k: [None] * len(v) for k, v in W.items() if isinstance(v, list)}
    G["norm_final"] = d_norm_final
    dm = [[None] * 6 for _ in range(NL)]
    dxn_next = None
    for l in reversed(range(NL)):
        j = l // 2
        even = l % 2 == 0
        r = saved[l]
        zero = early_grads(0, G) if l == 0 and early_grads is not None else None
        if l == NL - 1:
            (ds1, dy2), (dm[l][5],), _ = r["ffn_out"].bwd([ds], {0: F32, 1: BF16}, unread=(0,))
        else:
            smp = None if zero is None else [mods[l] + zero, mods[l + 1] + zero]
            (ds1, dy2), (dm[l][5], dm[l + 1][0], dm[l + 1][1]), (G["norm_mix"][l + 1],) = r["ffn_out"].bwd(
                [ds, dxn_next], {0: F32, 1: BF16}, grad_glob=(0,), samp=smp)
        dy2f, xn2f = _flat(dy2), _flat(r["xn2"])
        hid, dz2 = _ffn_mid_bwd(f"ffn_mid{l}", xn2f, dy2f, W["ffn_in"][l], W["ffn_out"][l])
        G["ffn_out"][l] = _mm(f"g_ffn_out{l}", hid, dy2f, ta=True, out_dtype=BF16)
        G["ffn_in"][l] = _mm(f"g_ffn_in{l}", xn2f, dz2, ta=True, out_dtype=BF16)
        dxn2 = _mm(f"d_xn2{l}", dz2, W["ffn_in"][l], tb=True).reshape(B, L, D)
        zero = early_grads(1, G) if l == 0 and early_grads is not None else None
        smp = None if zero is None else [mods[l] + zero]
        (ds0, dy), (dm[l][2], dm[l][3], dm[l][4]), (G["norm_ffn"][l],) = r["mix_out"].bwd(
            [ds1, dxn2], {0: F32, 1: BF16}, grad_glob=(0,), samp=smp)
        dyf = _flat(dy)
        if even:
            w_out = W["ab_out"][j]
            dcat = _mm(f"d_cat{l}", dyf, w_out, tb=True).reshape(B, L, -1)
            G["ab_out"][l // 2] = jnp.concatenate(
                [_mm(f"g_ab_out_a{l}", _flat(r["o"]), dyf, ta=True, out_dtype=BF16),
                 _mm(f"g_ab_out_b{l}", _flat(r["cm"]), dyf, ta=True, out_dtype=BF16)], axis=0)
            dq, dkv, dkpe = _mla_bwd(r["q"], r["kv"], r["kpe"], r["lse"], dcat, n, tq)
            (dq0,), _, _ = r["qrp"].bwd([dq], {0: BF16}, unread=(0,))
            dq0f, dkvf = _flat(dq0), _flat(dkv)
            G["wq"][j] = _mm(f"g_wq{l}", _flat(r["qn"]), dq0f, ta=True, out_dtype=BF16)
            G["wkv"][j] = _mm(f"g_wkv{l}", _flat(r["kvn"]), dkvf, ta=True, out_dtype=BF16)
            dqn = _mm(f"d_qn{l}", dq0f, W["wq"][j], tb=True).reshape(B, L, 256)
            dkvn = _mm(f"d_kvn{l}", dkvf, W["wkv"][j], tb=True).reshape(B, L, 256)
            (dz,), _, gg = r["abp"].bwd([dkvn, dqn, dkpe, (dcat, 512)], {0: BF16}, grad_glob=tuple(range(14)))
            G["kv_norm"][j], G["q_norm"][j] = gg[0], gg[1]
            G["v_norm"][j], G["ws"][j], G["bs"][j] = list(gg[2:6]), list(gg[6:10]), list(gg[10:14])
            w_in, key = W["ab_in"][j], "ab_in"
        else:
            w_out = W["cd_out"][j]
            dcat = _mm(f"d_cat{l}", dyf, w_out, tb=True).reshape(B, L, -1)
            G["cd_out"][j] = jnp.concatenate(
                [_mm(f"g_cd_out_a{l}", _flat(r["yr"]), dyf, ta=True, out_dtype=BF16),
                 _mm(f"g_cd_out_b{l}", _flat(r["osw"]), dyf, ta=True, out_dtype=BF16)], axis=0)
            (dyret, drg), _, gg = r["mrg"].bwd([(dcat, 0)], {0: F32, 1: F32}, grad_glob=(0, 1, 2, 3))
            G["ret_norm"][j] = list(gg)
            drq, drk, drv, dlg = _ret_bwd(r["rq"], r["rk"], r["rv"], W["lg"][j], dyret, n, tq)
            dsq, dsk, dsv, dsink = _swa_bwd(r["sq"], r["sk"], r["sv"], W["sink"][j], r["lse"], dcat, n, tq)
            G["lg"][j], G["sink"][j] = dlg, dsink
            (dz,), _, _ = r["cdp"].bwd([drq, drk, drv, dsq, dsk, dsv, drg], {0: BF16}, unread=(0,))
            w_in, key = W["cd_in"][j], "cd_in"
        dzf = _flat(dz)
        G[key][j] = _mm(f"g_{key}{l}", _flat(r["xn"]), dzf, ta=True, out_dtype=BF16)
        dxn = _mm(f"d_xn{l}", dzf, w_in, tb=True).reshape(B, L, D)
        if l == 0:
            (ds,), (dm[0][0], dm[0][1]), (G["norm_mix"][0],) = r["pre1"].bwd([dxn], {0: F32}, grad_glob=(0,),
                                                                            add={0: ds0}, samp=smp)
        else:
            ds, dxn_next = ds0, dxn
    dmods = jnp.stack([jnp.concatenate(d, axis=2) for d in dm])
    return loss, ds[:, :n], dmods, G


def kernel(x, c, ctx, c_ctx, ada_w, ada_b, norm_mix, norm_ffn, norm_final, ffn_in, ffn_out, ab_in, ab_out, mla_q_norm, mla_kv_norm, mla_wq_b, mla_wkv_b, cmlp_v_norm, cmlp_ws, cmlp_bs, cd_in, cd_out, ret_decay_fwd, ret_decay_bwd, ret_norm, swa_sink, loss_target, m_c_ctx, m_ada_w, m_ada_b, m_norm_mix, m_norm_ffn, m_norm_final, m_ffn_in, m_ffn_out, m_ab_in, m_ab_out, m_mla_q_norm, m_mla_kv_norm, m_mla_wq_b, m_mla_wkv_b, m_cmlp_v_norm, m_cmlp_ws, m_cmlp_bs, m_cd_in, m_cd_out, m_ret_decay_fwd, m_ret_decay_bwd, m_ret_norm, m_swa_sink, v_c_ctx, v_ada_w, v_ada_b, v_norm_mix, v_norm_ffn, v_norm_final, v_ffn_in, v_ffn_out, v_ab_in, v_ab_out, v_mla_q_norm, v_mla_kv_norm, v_mla_wq_b, v_mla_wkv_b, v_cmlp_v_norm, v_cmlp_ws, v_cmlp_bs, v_cd_in, v_cd_out, v_ret_decay_fwd, v_ret_decay_bwd, v_ret_norm, v_swa_sink):
    B, n, D = x.shape
    NL = ada_w.shape[0]
    NE, NO = ab_in.shape[0], cd_in.shape[0]
    me = _my_index()
    weights = dict(c_ctx=c_ctx, ada_w=ada_w, ada_b=ada_b, norm_mix=norm_mix, norm_ffn=norm_ffn, norm_final=norm_final,
                   ffn_in=ffn_in, ffn_out=ffn_out, ab_in=ab_in, ab_out=ab_out, mla_q_norm=mla_q_norm,
                   mla_kv_norm=mla_kv_norm, mla_wq_b=mla_wq_b, mla_wkv_b=mla_wkv_b, cmlp_v_norm=cmlp_v_norm,
                   cmlp_ws=cmlp_ws, cmlp_bs=cmlp_bs, cd_in=cd_in, cd_out=cd_out, ret_decay_fwd=ret_decay_fwd,
                   ret_decay_bwd=ret_decay_bwd, ret_norm=ret_norm, swa_sink=swa_sink)
    moments_m = dict(c_ctx=m_c_ctx, ada_w=m_ada_w, ada_b=m_ada_b, norm_mix=m_norm_mix, norm_ffn=m_norm_ffn,
                     norm_final=m_norm_final, ffn_in=m_ffn_in, ffn_out=m_ffn_out, ab_in=m_ab_in, ab_out=m_ab_out,
                     mla_q_norm=m_mla_q_norm, mla_kv_norm=m_mla_kv_norm, mla_wq_b=m_mla_wq_b, mla_wkv_b=m_mla_wkv_b,
                     cmlp_v_norm=m_cmlp_v_norm, cmlp_ws=m_cmlp_ws, cmlp_bs=m_cmlp_bs, cd_in=m_cd_in, cd_out=m_cd_out,
                     ret_decay_fwd=m_ret_decay_fwd, ret_decay_bwd=m_ret_decay_bwd, ret_norm=m_ret_norm,
                     swa_sink=m_swa_sink)
    moments_v = dict(c_ctx=v_c_ctx, ada_w=v_ada_w, ada_b=v_ada_b, norm_mix=v_norm_mix, norm_ffn=v_norm_ffn,
                     norm_final=v_norm_final, ffn_in=v_ffn_in, ffn_out=v_ffn_out, ab_in=v_ab_in, ab_out=v_ab_out,
                     mla_q_norm=v_mla_q_norm, mla_kv_norm=v_mla_kv_norm, mla_wq_b=v_mla_wq_b, mla_wkv_b=v_mla_wkv_b,
                     cmlp_v_norm=v_cmlp_v_norm, cmlp_ws=v_cmlp_ws, cmlp_bs=v_cmlp_bs, cd_in=v_cd_in, cd_out=v_cd_out,
                     ret_decay_fwd=v_ret_decay_fwd, ret_decay_bwd=v_ret_decay_bwd, ret_norm=v_ret_norm,
                     swa_sink=v_swa_sink)
    order = list(weights)

    Ns = ada_w.shape[2]
    (c_g,) = _all_gather("gather_c", [c])
    R = N_DEV * B + 8
    c_all = jnp.concatenate([c_g.reshape(N_DEV * B, D), jnp.broadcast_to(c_ctx[None], (8, D))], axis=0)
    ada_b_mine = lax.dynamic_slice_in_dim(ada_b, me * Ns, Ns, axis=1)[:, None, :]
    mods_shard = _ada_fwd(c_all, ada_w, ada_b_mine)
    (mods_g,) = _all_gather("gather_mods", [mods_shard])
    mods_full = jnp.transpose(mods_g, (1, 2, 0, 3)).reshape(NL, R, 6, D)
    mx = lax.dynamic_slice_in_dim(mods_full, me * B, B, axis=1)
    mh = jnp.broadcast_to(mods_full[:, N_DEV * B][:, None], (NL, B, 6, D))
    mods = jnp.stack([mx, mh], axis=2)

    big = ["ffn_in", "ffn_out", "ab_in", "ab_out", "cd_in", "cd_out", "mla_wq_b", "mla_wkv_b"]
    col_sharded = {"ffn_in", "ab_in", "cd_in", "mla_wq_b", "mla_wkv_b"}
    shards = {k: _to_bf16("cast_" + k, weights[k]) for k in big}
    first = {k: 0 if k.startswith("cd_") else 1 for k in big}
    a_keys = [k for k in big if first[k] and not k.startswith("ffn_")]
    b_keys = ["ffn_in", "ffn_out"]
    early = _all_gather("gather_wA", [shards[k][:1] for k in a_keys])
    (rn_g,) = _all_gather("gather_ret_norm", [ret_norm])
    small_first = [mods, rn_g] + list(early)
    wb_handle, wb_token = _exchange_start("gather_wB_start", [shards[k][:1] for k in b_keys], False, small_first)
    wc_handle, wc_token = _exchange_start("gather_wC_start", [shards[k][first[k]:] for k in big], False,
                                          small_first + [wb_token])
    mods = mods + (wb_token[0, 0] + wc_token[0, 0])

    def unshard(k, g):
        if k in col_sharded:
            f = jnp.transpose(g, (1, 2, 0, 3)).reshape(g.shape[1], g.shape[2], -1)
        else:
            f = jnp.transpose(g, (1, 0, 2, 3)).reshape(g.shape[1], -1, g.shape[3])
        return [f[i] for i in range(f.shape[0])]

    rn_full = jnp.transpose(rn_g, (1, 0, 2)).reshape(NO, -1)
    small_p = dict(weights, ret_norm=rn_full)
    full0 = {k: [None] * weights[k].shape[0] for k in big}
    for k, g in zip(a_keys, early):
        full0[k][:1] = unshard(k, g)

    def later_weights(stage, newest):
        if stage == 0:
            landed = _exchange_wait("gather_wB_wait", wb_handle, newest)
            for k, land in zip(b_keys, landed):
                full0[k][:1] = unshard(k, _with_own(land, shards[k][:1]))
        else:
            landed = _exchange_wait("gather_wC_wait", wc_handle, newest)
            for k, land in zip(big, landed):
                full0[k][first[k]:] = unshard(k, _with_own(land, shards[k][first[k]:]))
        return _layer_weights(full0, small_p)

    def to_slots(k, gl):
        g = jnp.stack(gl)
        if k == "ffn_in":
            half = N_DEV // 2
            g = g.reshape(g.shape[0], 2, g.shape[2], half, -1)
            return jnp.transpose(g, (1, 3, 0, 2, 4)).reshape(N_DEV, g.shape[0], g.shape[2], -1)
        if k in col_sharded:
            return jnp.transpose(g.reshape(g.shape[0], g.shape[1], N_DEV, -1), (2, 0, 1, 3))
        return jnp.transpose(g.reshape(g.shape[0], N_DEV, -1, g.shape[2]), (1, 0, 2, 3))

    def big_grads(G):
        return dict(ffn_in=[g if g is None else _ffn_deinterleave(g) for g in G["ffn_in"]], ffn_out=G["ffn_out"],
                    ab_in=[g if g is None else _ab_unperm(g) for g in G["ab_in"]],
                    ab_out=G["ab_out"], cd_in=G["cd_in"], cd_out=G["cd_out"],
                    mla_wq_b=[g if g is None else _wq_unperm(g) for g in G["wq"]], mla_wkv_b=G["wkv"])

    sent = {}

    def early_grads(stage, G):
        parts = big_grads(G)
        if stage == 0:
            srcs = [to_slots(k, parts[k][first[k]:]) for k in big]
            handle, g_token = _exchange_start("scatter_gC_start", srcs, slotted=True)
        else:
            srcs = [to_slots(k, parts[k][:1]) for k in b_keys]
            handle, g_token = _exchange_start("scatter_gB_start", srcs, slotted=True)
        sent[stage] = (handle, srcs)
        return g_token[0, 0]

    loss_part, grad_x, dmods, G = _local_step(x, ctx, loss_target, mods, _layer_weights(full0, small_p),
                                              later_weights, early_grads)

    dmx = dmods[:, :, 0].reshape(NL, B, 6 * D)
    dmh = jnp.sum(dmods[:, :, 1], axis=1).reshape(NL, 1, 6 * D)
    (dm_g,) = _all_gather("gather_dmods", [jnp.concatenate([dmx, dmh], axis=1)])
    dmx_all = jnp.transpose(dm_g[:, :, :B], (1, 0, 2, 3)).reshape(NL, N_DEV * B, 6 * D)
    dmh_all = jnp.sum(dm_g[:, :, B], axis=0)
    dm_rows = jnp.concatenate([dmx_all, dmh_all[:, None], jnp.zeros((NL, 7, 6 * D), F32)], axis=1)
    g_ada_b = jnp.sum(dm_rows, axis=1)
    dm_mine = lax.dynamic_slice_in_dim(dm_rows, me * Ns, Ns, axis=2)
    g_ada_w, dcond = _ada_bwd(c_all, ada_w, dm_mine)
    sg = jax.nn.sigmoid(c_ctx)
    d_c_ctx_part = jnp.sum(dcond[:, N_DEV * B], axis=0) * (sg * (1.0 + c_ctx * (1.0 - sg)))

    def cat(parts):
        return jnp.concatenate([p.reshape(-1) for p in parts])

    dlg = jnp.stack([jnp.sum(G["lg"][j], axis=0) for j in range(NO)])
    sig_f, sig_b = jax.nn.sigmoid(-ret_decay_fwd), jax.nn.sigmoid(-ret_decay_bwd)
    small = dict(
        loss=loss_part[0, 0:1],
        c_ctx=d_c_ctx_part,
        norm_mix=cat(G["norm_mix"]), norm_ffn=cat(G["norm_ffn"]), norm_final=G["norm_final"].reshape(-1),
        mla_q_norm=cat(G["q_norm"]), mla_kv_norm=cat(G["kv_norm"]),
        cmlp_v_norm=cat([cat(G["v_norm"][j]) for j in range(NE)]),
        cmlp_ws=cat([jnp.stack(G["ws"][j]) for j in range(NE)]),
        cmlp_bs=cat([jnp.stack([b_[:, 0] for b_ in G["bs"][j]]) for j in range(NE)]),
        ret_decay_fwd=(dlg[:, 0] * sig_f).reshape(-1), ret_decay_bwd=(dlg[:, 1] * sig_b).reshape(-1),
        ret_norm=cat([cat(G["ret_norm"][j]) for j in range(NO)]),
        swa_sink=cat([jnp.sum(G["sink"][j], axis=0) for j in range(NO)]),
    )
    small_keys = list(small)
    sizes = [small[k].shape[0] for k in small_keys]
    total = sum(sizes)
    padded = -(-total // 2048) * 2048
    packed = jnp.concatenate([small[k] for k in small_keys] + [jnp.zeros((padded - total,), F32)]).reshape(-1, 128)
    (small_g,) = _all_gather("gather_small", [packed])

    parts = big_grads(G)
    landed0 = dict(zip(a_keys, _all_to_all("scatter_gA", [to_slots(k, parts[k][:1]) for k in a_keys])))
    for k, land, src in zip(b_keys, _exchange_wait("scatter_gB_wait", sent[1][0], grad_x), sent[1][1]):
        landed0[k] = _with_own(land, src)
    landed = []
    for k, land, src in zip(big, _exchange_wait("scatter_gC_wait", sent[0][0], grad_x), sent[0][1]):
        rest = _with_own(land, src)
        landed.append(jnp.concatenate([landed0[k], rest], axis=1) if first[k] else rest)

    grads, deltas, new_m, new_v = {}, {}, {}, {}
    for k, land in zip(big, landed):
        grads[k], deltas[k], new_m[k], new_v[k] = _adamw_from_slots("adamw_" + k, weights[k], moments_m[k], moments_v[k], land)
    deltas["ada_w"], new_m["ada_w"], new_v["ada_w"] = [
        o.reshape(ada_w.shape) for o in _adamw("adamw_ada_w", _flat(ada_w), _flat(g_ada_w), _flat(m_ada_w), _flat(v_ada_w))]
    grads["ada_w"] = g_ada_w

    sums_only = ("loss", "ret_norm")

    def packed_of(src, fill):
        vals = [jnp.full((sizes[i],), fill, F32) if k in sums_only else src[k].reshape(-1)
                for i, k in enumerate(small_keys)]
        return jnp.concatenate(vals + [jnp.full((padded - total,), fill, F32)]).reshape(-1, 128)

    w_p, m_p, v_p = packed_of(weights, 0.0), packed_of(moments_m, 0.0), packed_of(moments_v, 1.0)

    def f_small(w_, m_, v_, land_):
        g = _sum_slots(land_)
        return (g,) + _adamw_math(w_, g, m_, v_)

    g_p, d_p, nm_p, nv_p = _ew("adamw_small", f_small, [w_p, m_p, v_p, small_g], [F32] * 4)
    offs = np.cumsum([0] + sizes)
    for i, k in enumerate(small_keys):
        sl = slice(int(offs[i]), int(offs[i + 1]))
        if k == "loss":
            loss = g_p.reshape(-1)[int(offs[i])]
        elif k == "ret_norm":
            g_full = g_p.reshape(-1)[sl].reshape(NO, -1)
            g_mine = lax.dynamic_slice_in_dim(g_full, me * ret_norm.shape[1], ret_norm.shape[1], axis=1)
            d_, m_, v_ = _adamw("adamw_ret_norm", *[jnp.pad(a, ((0, 8 - NO), (0, 128 - a.shape[1])), constant_values=cv)
                                                     for a, cv in ((ret_norm, 0.0), (g_mine, 0.0), (m_ret_norm, 0.0), (v_ret_norm, 1.0))])
            grads[k] = g_mine
            deltas[k], new_m[k], new_v[k] = [a[:NO, :ret_norm.shape[1]] for a in (d_, m_, v_)]
        else:
            shp = weights[k].shape
            grads[k], deltas[k], new_m[k], new_v[k] = [a.reshape(-1)[sl].reshape(shp) for a in (g_p, d_p, nm_p, nv_p)]
    pad_b = lambda a, cv=0.0: jnp.pad(a, ((0, 8 - NL), (0, 0)), constant_values=cv)
    d_, m_, v_ = _adamw("adamw_ada_b", pad_b(ada_b), pad_b(g_ada_b), pad_b(m_ada_b), pad_b(v_ada_b, 1.0))
    grads["ada_b"] = g_ada_b
    deltas["ada_b"], new_m["ada_b"], new_v["ada_b"] = d_[:NL], m_[:NL], v_[:NL]

    return (loss, grad_x, *[grads[k] for k in order], *[deltas[k] for k in order],
            *[new_m[k] for k in order], *[new_v[k] for k in order])
```

```python
import functools

import numpy as np
import jax
import jax.numpy as jnp
from jax import lax
from jax.experimental import pallas as pl
from jax.experimental.pallas import tpu as pltpu

F32 = jnp.float32
BF16 = jnp.bfloat16
EPS = 1e-6
NEG_INF = -1e30
N_DEV = 8
GRID_W = 64
ROPE_THETA = 10000.0
ROPE_DIM = 64
SWA_WINDOW = 128
MLA_SCALE = (128 + 64) ** -0.5
SWA_SCALE = 64 ** -0.5
RET_K_SCALE = 64 ** -0.5
ADAM_LR, ADAM_B1, ADAM_B2, ADAM_EPS, ADAM_WD, ADAM_STEP = 0.001, 0.9, 0.999, 1e-08, 0.01, 10
V7X_VMEM_LIMIT = 56 * 1024 * 1024
MESH = pl.DeviceIdType.MESH


def _pallas(body, **kw):
    return pl.pallas_call(body, **kw)


def _params(sem=None):
    return pltpu.CompilerParams(dimension_semantics=sem, vmem_limit_bytes=V7X_VMEM_LIMIT)


def _tile(n, cap, align):
    best = None
    for t in range(align, min(n, cap) + 1, align):
        if n % t == 0:
            best = t
    return n if best is None else best


def _sds(shape, dtype):
    return jax.ShapeDtypeStruct(tuple(shape), dtype)


def _ew(name, f, ins, out_dtypes, cap_elems=131072):
    R, C = ins[0].shape[-2:]
    tr = _tile(R, max(16, cap_elems // C), 16)
    n_in = len(ins)

    def spec(a):
        if a.ndim == 2:
            return pl.BlockSpec((tr, C), lambda i: (i, 0))
        return pl.BlockSpec((a.shape[0], tr, C), lambda i: (0, i, 0))

    def body(*refs):
        outs = f(*[r[...] for r in refs[:n_in]])
        for r, o in zip(refs[n_in:], outs):
            r[...] = o.astype(r.dtype)

    return _pallas(
        body, name=name, grid=(R // tr,), in_specs=[spec(a) for a in ins],
        out_specs=[pl.BlockSpec((tr, C), lambda i: (i, 0)) for _ in out_dtypes],
        out_shape=[_sds((R, C), d) for d in out_dtypes], compiler_params=_params(("parallel",)),
    )(*ins)


def _to_bf16(name, w):
    w2 = w.reshape(-1, w.shape[-1])
    return _ew(name, lambda v: (v,), [w2], [BF16])[0].reshape(w.shape)


def _adamw_math(w, g, m, v):
    m = ADAM_B1 * m + (1.0 - ADAM_B1) * g
    v = ADAM_B2 * v + (1.0 - ADAM_B2) * (g * g)
    m_hat = m / (1.0 - ADAM_B1 ** ADAM_STEP)
    v_hat = v / (1.0 - ADAM_B2 ** ADAM_STEP)
    delta = -ADAM_LR * (m_hat / (jnp.sqrt(v_hat) + ADAM_EPS) + ADAM_WD * w)
    return delta, m, v


def _sum_slots(land):
    g = land[0].astype(F32)
    for s in range(1, land.shape[0]):
        g = g + land[s].astype(F32)
    return g


def _adamw_from_slots(name, w, m, v, land):
    shp = w.shape
    C = shp[-1]

    def f(w_, m_, v_, land_):
        g = _sum_slots(land_)
        return (g,) + _adamw_math(w_, g, m_, v_)

    outs = _ew(name, f, [w.reshape(-1, C), m.reshape(-1, C), v.reshape(-1, C), land.reshape(N_DEV, -1, C)],
               [F32] * 4, cap_elems=65536)
    return [o.reshape(shp) for o in outs]


def _adamw(name, w, g, m, v):
    outs = _ew(name, lambda w_, g_, m_, v_: _adamw_math(w_, g_, m_, v_), [w, g, m, v], [F32] * 3)
    return outs


def _mm(name, a, b, ta=False, tb=False, out_dtype=F32, add=None):
    M, K = (a.shape[1], a.shape[0]) if ta else a.shape
    N = b.shape[0] if tb else b.shape[1]
    tm = _tile(M, 1024, 128)
    tn = _tile(N, 1024, 128)
    if tn < 256 and N <= 2432:
        tn = N
    tk = _tile(K, 1792, 128)
    nk = K // tk
    a_spec = pl.BlockSpec((tk, tm), lambda i, j, k: (k, i)) if ta else pl.BlockSpec((tm, tk), lambda i, j, k: (i, k))
    b_spec = pl.BlockSpec((tn, tk), lambda i, j, k: (j, k)) if tb else pl.BlockSpec((tk, tn), lambda i, j, k: (k, j))
    o_spec = pl.BlockSpec((tm, tn), lambda i, j, k: (i, j))
    dims = (((0 if ta else 1,), (1 if tb else 0,)), ((), ()))
    has_add = add is not None

    def product(a_ref, b_ref):
        return lax.dot_general(a_ref[...].astype(BF16), b_ref[...].astype(BF16), dims, preferred_element_type=F32)

    def body_single(*refs):
        acc = product(refs[0], refs[1])
        if has_add:
            acc = acc + refs[2][...]
        refs[-1][...] = acc.astype(refs[-1].dtype)

    def body(*refs):
        a_ref, b_ref = refs[0], refs[1]
        add_ref = refs[2] if has_add else None
        o_ref, acc = refs[-2], refs[-1]
        k = pl.program_id(2)

        @pl.when(k == 0)
        def _():
            acc[...] = add_ref[...] if has_add else jnp.zeros_like(acc)

        acc[...] += product(a_ref, b_ref)

        @pl.when(k == nk - 1)
        def _():
            o_ref[...] = acc[...].astype(o_ref.dtype)

    ins = [a, b] + ([add] if has_add else [])
    specs = [a_spec, b_spec] + ([o_spec] if has_add else [])
    return _pallas(
        body_single if nk == 1 else body, name=name, grid=(M // tm, N // tn, nk), in_specs=specs, out_specs=o_spec,
        out_shape=_sds((M, N), out_dtype), scratch_shapes=[] if nk == 1 else [pltpu.VMEM((tm, tn), F32)],
        compiler_params=_params(("parallel", "parallel", "arbitrary")),
    )(*ins)


FFN_TILE = 256


def _ffn_interleave(w):
    D, F2 = w.shape
    nj = F2 // (2 * FFN_TILE)

    def body(a_ref, b_ref, o_ref):
        o_ref[:, :FFN_TILE] = a_ref[...]
        o_ref[:, FFN_TILE:] = b_ref[...]

    return _pallas(
        body, name="ffn_interleave", grid=(nj,),
        in_specs=[pl.BlockSpec((D, FFN_TILE), lambda j: (0, j)), pl.BlockSpec((D, FFN_TILE), lambda j: (0, j + nj))],
        out_specs=pl.BlockSpec((D, 2 * FFN_TILE), lambda j: (0, j)), out_shape=_sds((D, F2), w.dtype),
        compiler_params=_params(("parallel",)),
    )(w, w)


def _ffn_deinterleave(w):
    D, F2 = w.shape
    nj = F2 // (2 * FFN_TILE)

    def body(w_ref, o_ref):
        o_ref[0] = w_ref[:, :FFN_TILE]
        o_ref[1] = w_ref[:, FFN_TILE:]

    return _pallas(
        body, name="ffn_deinterleave", grid=(nj,), in_specs=[pl.BlockSpec((D, 2 * FFN_TILE), lambda j: (0, j))],
        out_specs=pl.BlockSpec((2, D, FFN_TILE), lambda j: (0, 0, j)), out_shape=_sds((2, D, F2 // 2), w.dtype),
        compiler_params=_params(("parallel",)),
    )(w)


def _ffn_specs(M, D, F):
    tm = _tile(M, 1024, 128)
    x_spec = pl.BlockSpec((tm, D), lambda i, j: (i, 0))
    wi_spec = pl.BlockSpec((D, 2 * FFN_TILE), lambda i, j: (0, j))
    wo_spec = pl.BlockSpec((FFN_TILE, D), lambda i, j: (j, 0))
    return tm, F // FFN_TILE, x_spec, wi_spec, wo_spec


def _ffn_fwd(name, xn, w_in, w_out):
    M, D = xn.shape
    tm, nj, x_spec, wi_spec, wo_spec = _ffn_specs(M, D, w_out.shape[0])

    def body(x_ref, wi_ref, wo_ref, y_ref):
        j = pl.program_id(1)
        z = _dot(x_ref[...], wi_ref[...])
        hid = (jax.nn.silu(z[:, :FFN_TILE]) * z[:, FFN_TILE:]).astype(BF16)
        part = _dot(hid, wo_ref[...])

        @pl.when(j == 0)
        def _():
            y_ref[...] = part

        @pl.when(j > 0)
        def _():
            y_ref[...] += part

    return _pallas(
        body, name=name, grid=(M // tm, nj), in_specs=[x_spec, wi_spec, wo_spec], out_specs=x_spec,
        out_shape=_sds((M, D), F32), compiler_params=_params(("parallel", "arbitrary")),
    )(xn, w_in, w_out)


def _ffn_mid_bwd(name, xn, dy, w_in, w_out):
    M, D = xn.shape
    F = w_out.shape[0]
    tm, nj, x_spec, wi_spec, wo_spec = _ffn_specs(M, D, F)

    def body(x_ref, dy_ref, wi_ref, wo_ref, h_ref, dz_ref):
        z = _dot(x_ref[...], wi_ref[...])
        a, b = z[:, :FFN_TILE], z[:, FFN_TILE:]
        dh = _dot_nt(dy_ref[...], wo_ref[...])
        sig = jax.nn.sigmoid(a)
        act = a * sig
        h_ref[...] = (act * b).astype(BF16)
        dz_ref[:, :FFN_TILE] = (dh * b * (sig * (1.0 + a * (1.0 - sig)))).astype(BF16)
        dz_ref[:, FFN_TILE:] = (dh * act).astype(BF16)

    return _pallas(
        body, name=name, grid=(M // tm, nj), in_specs=[x_spec, x_spec, wi_spec, wo_spec],
        out_specs=[pl.BlockSpec((tm, FFN_TILE), lambda i, j: (i, j)),
                   pl.BlockSpec((tm, 2 * FFN_TILE), lambda i, j: (i, j))],
        out_shape=[_sds((M, F), BF16), _sds((M, 2 * F), BF16)],
        compiler_params=_params(("parallel", "parallel")),
    )(xn, dy, w_in, w_out)


class _Rows:
    def __init__(self, name, f, B, L, n, tm, arrays, pieces, samp, samp_pieces, glob, out_arrays, out_pieces):
        self.name, self.f, self.B, self.L, self.n, self.tm = name, f, B, L, n, tm
        self.arrays, self.pieces, self.samp, self.samp_pieces, self.glob = arrays, pieces, samp, samp_pieces, glob
        self.out_arrays, self.out_pieces = out_arrays, out_pieces
        self.nx = n // tm
        self.grid = (B, L // tm)

    def _row_spec(self, C, batched=True):
        tm = self.tm
        if batched:
            return pl.BlockSpec((None, tm, C), lambda b, i: (b, i, 0))
        return pl.BlockSpec((tm, C), lambda b, i: (i, 0))

    def _in_specs(self, unread=()):
        nx, tm = self.nx, self.tm
        specs = [self._row_spec(a.shape[-1], a.ndim == 3) for a in self.arrays]
        for ai in unread:
            specs[ai] = pl.BlockSpec((None, tm, self.arrays[ai].shape[-1]), lambda b, i: (0, 0, 0))
        specs += [pl.BlockSpec((None, None) + s.shape[2:], lambda b, i: (b, i // nx, 0, 0)) for s in self.samp]
        specs += [pl.BlockSpec(g.shape, lambda b, i, nd=g.ndim: (0,) * nd) for g in self.glob]
        return specs

    def _load(self, a_refs, s_refs, g_refs):
        args = [a_refs[ai][:, cs:cs + cw].astype(F32) for ai, cs, cw in self.pieces]
        args += [s_refs[si][r:r + 1, :].astype(F32) for si, r in self.samp_pieces]
        args += [g[...].astype(F32) for g in g_refs]
        return args

    def fwd(self):
        na, ns, ng = len(self.arrays), len(self.samp), len(self.glob)

        def body(*refs):
            a_refs, s_refs, g_refs = refs[:na], refs[na:na + ns], refs[na + ns:na + ns + ng]
            o_refs = refs[na + ns + ng:]
            outs = self.f(*self._load(a_refs, s_refs, g_refs))
            for (oi, cs, cw), o in zip(self.out_pieces, outs):
                o_refs[oi][:, cs:cs + cw] = o.astype(o_refs[oi].dtype)

        return _pallas(
            body, name=self.name + "_fwd", grid=self.grid, in_specs=self._in_specs(),
            out_specs=[self._row_spec(C) for C, _ in self.out_arrays],
            out_shape=[_sds((self.B, self.L, C), d) for C, d in self.out_arrays],
            compiler_params=_params(("parallel", "parallel")),
        )(*self.arrays, *self.samp, *self.glob)

    def bwd(self, cts, grad_arrays, grad_glob=(), add=None, samp=None, unread=()):
        samp = self.samp if samp is None else samp
        na, ns, ng, nc = len(self.arrays), len(self.samp), len(self.glob), len(cts)
        ct_off = [c[1] if isinstance(c, tuple) else 0 for c in cts]
        cts = [c[0] if isinstance(c, tuple) else c for c in cts]
        add = add or {}
        add_keys = list(add)
        g_idx = list(grad_arrays)
        nx, B = self.nx, self.B
        n_in = na + ns + ng + nc + len(add_keys)
        n_pieces, n_sp = len(self.pieces), len(self.samp_pieces)

        def body(*refs):
            a_refs, s_refs, g_refs = refs[:na], refs[na:na + ns], refs[na + ns:na + ns + ng]
            c_refs = refs[na + ns + ng:na + ns + ng + nc]
            add_refs = refs[na + ns + ng + nc:n_in]
            d_refs = refs[n_in:n_in + len(g_idx)]
            ds_refs = refs[n_in + len(g_idx):n_in + len(g_idx) + n_sp]
            dg_refs = refs[n_in + len(g_idx) + n_sp:]
            b, i = pl.program_id(0), pl.program_id(1)
            args = self._load(a_refs, s_refs, g_refs)
            _, vjp = jax.vjp(lambda *xs: tuple(self.f(*xs)), *args)
            grads = vjp(tuple(c_refs[oi][:, ct_off[oi] + cs:ct_off[oi] + cs + cw].astype(F32)
                              for oi, cs, cw in self.out_pieces))
            for k, ai in enumerate(g_idx):
                covered = sum(cw for pa, _, cw in self.pieces if pa == ai)
                if covered < self.arrays[ai].shape[-1]:
                    d_refs[k][...] = jnp.zeros_like(d_refs[k])
                for (pa, cs, cw), gr in zip(self.pieces, grads[:n_pieces]):
                    if pa == ai:
                        if ai in add:
                            gr = gr + add_refs[add_keys.index(ai)][:, cs:cs + cw]
                        d_refs[k][:, cs:cs + cw] = gr.astype(d_refs[k].dtype)

            @pl.when((i == 0) | (i == nx))
            def _():
                for r in ds_refs:
                    r[...] = jnp.zeros_like(r)

            for r, gr in zip(ds_refs, grads[n_pieces:n_pieces + n_sp]):
                r[...] += gr

            @pl.when((b == 0) & (i == 0))
            def _():
                for r in dg_refs:
                    r[...] = jnp.zeros_like(r)

            for r, gi in zip(dg_refs, grad_glob):
                r[...] += grads[n_pieces + n_sp + gi]

        in_specs = self._in_specs(unread) + [self._row_spec(c.shape[-1]) for c in cts]
        in_specs += [self._row_spec(add[k].shape[-1]) for k in add_keys]
        out_specs = [self._row_spec(self.arrays[ai].shape[-1]) for ai in g_idx]
        out_shape = [_sds(self.arrays[ai].shape, grad_arrays[ai]) for ai in g_idx]
        for si, _ in self.samp_pieces:
            C = self.samp[si].shape[-1]
            out_specs.append(pl.BlockSpec((None, None, 1, C), lambda b, i: (b, i // nx, 0, 0)))
            out_shape.append(_sds((B, 2, 1, C), F32))
        for gi in grad_glob:
            g = self.glob[gi]
            out_specs.append(pl.BlockSpec(g.shape, lambda b, i, nd=g.ndim: (0,) * nd))
            out_shape.append(_sds(g.shape, F32))
        outs = _pallas(
            body, name=self.name + "_bwd", grid=self.grid, in_specs=in_specs, out_specs=out_specs,
            out_shape=out_shape, compiler_params=_params(("arbitrary", "arbitrary")),
        )(*self.arrays, *samp, *self.glob, *cts, *[add[k] for k in add_keys])
        ng_ = len(g_idx)
        return outs[:ng_], outs[ng_:ng_ + n_sp], outs[ng_ + n_sp:]


def _rms(x, g):
    return x * lax.rsqrt(jnp.mean(x * x, axis=-1, keepdims=True) + EPS) * g


@jax.custom_vjp
def _swap_halves(x):
    w = x.shape[-1]
    lane = lax.broadcasted_iota(jnp.int32, x.shape, x.ndim - 1)
    up = pltpu.roll(x, w - ROPE_DIM // 2, x.ndim - 1)
    down = pltpu.roll(x, ROPE_DIM // 2, x.ndim - 1)
    return jnp.where(lane % ROPE_DIM < ROPE_DIM // 2, up, down)


_swap_halves.defvjp(lambda x: (_swap_halves(x), None), lambda _, ct: (_swap_halves(ct),))


@jax.custom_vjp
def _bdot(a, b):
    return jnp.dot(a.astype(BF16), b.astype(BF16), preferred_element_type=F32)


_bdot.defvjp(lambda a, b: (_bdot(a, b), (a, b)),
             lambda res, ct: (_dot_nt(ct.astype(BF16), res[1].astype(BF16)), _dot_tn(res[0].astype(BF16), ct.astype(BF16))))


def _rope(x, cos2, sin2):
    return x * cos2 + _swap_halves(x) * sin2


def _f_pre(s, shift, scale, g):
    return (_rms(s, g) * (1.0 + scale) + shift,)


def _f_res(s, y, gate):
    return (s + gate * y,)


def _f_res_pre(s, y, gate, shift, scale, g):
    s1 = s + gate * y
    return s1, _rms(s1, g) * (1.0 + scale) + shift


def _f_ab_pre(*args):
    kv_lat, q_lat, kpe = args[0:3]
    us, vs = args[3:7], args[7:11]
    cos2, sin2 = args[11:13]
    kv_norm, q_norm = args[13:15]
    vns, wss, bss = args[15:19], args[19:23], args[23:27]
    outs = [_rms(kv_lat, kv_norm), _rms(q_lat, q_norm), _rope(kpe, cos2, sin2)]
    for u, v, vn, ws, bs in zip(us, vs, vns, wss, bss):
        vg = _rms(jax.nn.gelu(v), vn)
        mixed = _bdot(ws, vg) + bs
        outs.append(jax.nn.gelu(u) * mixed)
    return tuple(outs)


def _f_q_rope(qn, qr, cos2, sin2):
    return qn, _rope(qr, cos2, sin2)


def _f_cd_pre(rk, rv, sk, sv, rq, rg, sq, c256, s256, c128, s128, c512, s512):
    return (_rope(rq, c256, s256), _rope(rk * RET_K_SCALE, c256, s256), rv,
            _rope(sq, c512, s512), _rope(sk, c128, s128), sv, rg)


def _f_cd_merge(*args):
    ys, rgs, rns = args[0:4], args[4:8], args[8:12]
    return tuple(_rms(y, rn) * jax.nn.silu(rg) for y, rg, rn in zip(ys, rgs, rns))


def _dot_nt(a, b):
    return lax.dot_general(a, b, (((1,), (1,)), ((), ())), preferred_element_type=F32)


def _dot_tn(a, b):
    return lax.dot_general(a, b, (((0,), (0,)), ((), ())), preferred_element_type=F32)


def _dot(a, b):
    return jnp.dot(a, b, preferred_element_type=F32)


def _tile_spec(tq, C):
    return pl.BlockSpec((None, tq, C), lambda b, i: (b, i, 0))


def _full_spec(L, C):
    return pl.BlockSpec((None, L, C), lambda b, i: (b, 0, 0))


def _mla_valid(i, tq, L, n):
    qpos = i * tq + lax.broadcasted_iota(jnp.int32, (tq, 1), 0)
    kpos = lax.broadcasted_iota(jnp.int32, (1, L), 1)
    return (qpos < n) | (kpos >= n)


def _mla_half(h):
    lane = lax.broadcasted_iota(jnp.int32, (1, 128), 1)
    return (lane < 64) if h % 2 == 0 else (lane >= 64)


def _mla_query(q_ref, h):
    pair = q_ref[:, 512 + 128 * (h // 2):640 + 128 * (h // 2)]
    return jnp.concatenate([q_ref[:, 128 * h:128 * h + 128], jnp.where(_mla_half(h), pair, jnp.zeros_like(pair))], axis=1)


def _mla_fill_keys(kcat, kv_ref, kpe_ref):
    kp = kpe_ref[...]
    for h in range(4):
        kcat[h, :, 0:128] = kv_ref[:, 256 * h:256 * h + 128]
        kcat[h, :, 128:256] = jnp.where(_mla_half(h), kp, jnp.zeros_like(kp))


def _mla_fwd(q, kv, kpe, n, tq):
    B, L, _ = q.shape

    def body(q_ref, kv_ref, kpe_ref, o_ref, lse_ref, kcat):
        i = pl.program_id(1)

        @pl.when(i == 0)
        def _():
            _mla_fill_keys(kcat, kv_ref, kpe_ref)

        valid = _mla_valid(i, tq, L, n)
        for h in range(4):
            s = _dot_nt(_mla_query(q_ref, h), kcat[h])
            s = jnp.where(valid, s * MLA_SCALE, NEG_INF)
            m = jnp.max(s, axis=1, keepdims=True)
            e = jnp.exp(s - m)
            l = jnp.sum(e, axis=1, keepdims=True)
            p = (e * (1.0 / l)).astype(BF16)
            o_ref[:, 128 * h:128 * h + 128] = _dot(p, kv_ref[:, 256 * h + 128:256 * h + 256]).astype(o_ref.dtype)
            lse_ref[:, h:h + 1] = m + jnp.log(l)

    return _pallas(
        body, name="mla_fwd", grid=(B, L // tq),
        in_specs=[_tile_spec(tq, 768), _full_spec(L, 1024), _full_spec(L, 128)],
        out_specs=[_tile_spec(tq, 512), _tile_spec(tq, 4)],
        out_shape=[_sds((B, L, 512), BF16), _sds((B, L, 4), F32)],
        scratch_shapes=[pltpu.VMEM((4, L, 256), BF16)],
        compiler_params=_params(("parallel", "arbitrary")),
    )(q, kv, kpe)


def _mla_bwd(q, kv, kpe, lse, do, n, tq):
    B, L, _ = q.shape

    def body(q_ref, kv_ref, kpe_ref, lse_ref, do_ref, dq_ref, dkv_ref, dkpe_ref, kcat):
        i = pl.program_id(1)

        @pl.when(i == 0)
        def _():
            dkv_ref[...] = jnp.zeros_like(dkv_ref)
            dkpe_ref[...] = jnp.zeros_like(dkpe_ref)
            _mla_fill_keys(kcat, kv_ref, kpe_ref)

        valid = _mla_valid(i, tq, L, n)
        rope_pair = None
        for h in range(4):
            qc, kc = _mla_query(q_ref, h), kcat[h]
            v = kv_ref[:, 256 * h + 128:256 * h + 256]
            s = jnp.where(valid, _dot_nt(qc, kc) * MLA_SCALE, NEG_INF)
            p = jnp.exp(s - lse_ref[:, h:h + 1])
            doh = do_ref[:, 128 * h:128 * h + 128].astype(BF16)
            dp = _dot_nt(doh, v)
            delta = jnp.sum(p * dp, axis=1, keepdims=True)
            ds = (p * (dp - delta) * MLA_SCALE).astype(BF16)
            dqc = _dot(ds, kc)
            dq_ref[:, 128 * h:128 * h + 128] = dqc[:, 0:128]
            if h % 2 == 0:
                rope_pair = dqc[:, 128:256]
            else:
                dq_ref[:, 512 + 128 * (h // 2):640 + 128 * (h // 2)] = jnp.where(_mla_half(h), dqc[:, 128:256], rope_pair)
            dkc = _dot_tn(ds, qc)
            dkv_ref[:, 256 * h:256 * h + 128] += dkc[:, 0:128]
            dkpe_ref[...] += dkc[:, 128:256]
            dkv_ref[:, 256 * h + 128:256 * h + 256] += _dot_tn(p.astype(BF16), doh)

    return _pallas(
        body, name="mla_bwd", grid=(B, L // tq),
        in_specs=[_tile_spec(tq, 768), _full_spec(L, 1024), _full_spec(L, 128), _tile_spec(tq, 4),
                  pl.BlockSpec((None, tq, 512), lambda b, i: (b, i, 0))],
        out_specs=[_tile_spec(tq, 768), _full_spec(L, 1024), _full_spec(L, 128)],
        out_shape=[_sds((B, L, 768), F32), _sds((B, L, 1024), F32), _sds((B, L, 128), F32)],
        scratch_shapes=[pltpu.VMEM((4, L, 256), BF16)],
        compiler_params=_params(("parallel", "arbitrary")),
    )(q, kv, kpe, lse, do)


def _swa_window(i, tq, n):
    W = min(tq + 2 * SWA_WINDOW, n)
    lo = jnp.clip(i * tq - SWA_WINDOW, 0, n - W)
    return pl.multiple_of(lo, 128), W


def _swa_mask(i, lo, tq, W, n):
    qpos = i * tq + lax.broadcasted_iota(jnp.int32, (tq, 1), 0)
    kpos = lo + lax.broadcasted_iota(jnp.int32, (1, W), 1)
    return (jnp.abs(qpos - kpos) <= SWA_WINDOW) & (qpos < n)


def _swa_fill(rep, src_ref):
    for g in range(2):
        rep[:, 256 * g:256 * g + 256] = jnp.concatenate([src_ref[:, 64 * g:64 * g + 64]] * 4, axis=1)


def _swa_head_rows(a):
    lane = lax.broadcasted_iota(jnp.int32, (1, 256), 1)
    return jnp.concatenate([jnp.where(lane // 64 == h, a, jnp.zeros_like(a)) for h in range(4)], axis=0)


def _swa_fold_rows(a, tq):
    lane = lax.broadcasted_iota(jnp.int32, (1, 256), 1)
    out = jnp.where(lane // 64 == 0, a[0:tq], 0.0)
    for h in range(1, 4):
        out = jnp.where(lane // 64 == h, a[h * tq:(h + 1) * tq], out)
    return out


def _swa_scores(i, g, tq, n, L, q_ref, krep, sink_ref):
    lo, W = _swa_window(i, tq, n)
    mask = jnp.concatenate([_swa_mask(i, lo, tq, W, n)] * 4, axis=0)
    qs = _swa_head_rows(q_ref[:, 256 * g:256 * g + 256])
    kl, kc = krep[pl.ds(lo, W), 256 * g:256 * g + 256], krep[n:L, 256 * g:256 * g + 256]
    s1 = jnp.where(mask, _dot_nt(qs, kl) * SWA_SCALE, NEG_INF)
    s2 = _dot_nt(qs, kc) * SWA_SCALE
    sk = jnp.concatenate([jnp.broadcast_to(sink_ref[0:1, 4 * g + h:4 * g + h + 1], (tq, 1)) for h in range(4)], axis=0)
    return qs, kl, kc, s1, s2, sk, lo, W


def _swa_fwd(q, k, v, sink, n, tq):
    B, L, _ = q.shape

    def body(q_ref, k_ref, v_ref, sink_ref, o_ref, lse_ref, krep, vrep):
        i = pl.program_id(1)

        @pl.when(i == 0)
        def _():
            _swa_fill(krep, k_ref)
            _swa_fill(vrep, v_ref)

        for g in range(2):
            _, _, _, s1, s2, sk, lo, W = _swa_scores(i, g, tq, n, L, q_ref, krep, sink_ref)
            vl, vc = vrep[pl.ds(lo, W), 256 * g:256 * g + 256], vrep[n:L, 256 * g:256 * g + 256]
            m = jnp.maximum(jnp.maximum(jnp.max(s1, axis=1, keepdims=True), jnp.max(s2, axis=1, keepdims=True)), sk)
            e1, e2 = jnp.exp(s1 - m), jnp.exp(s2 - m)
            l = jnp.sum(e1, axis=1, keepdims=True) + jnp.sum(e2, axis=1, keepdims=True) + jnp.exp(sk - m)
            r = 1.0 / l
            o = _dot((e1 * r).astype(BF16), vl) + _dot((e2 * r).astype(BF16), vc)
            o_ref[:, 256 * g:256 * g + 256] = _swa_fold_rows(o, tq).astype(o_ref.dtype)
            lse = m + jnp.log(l)
            for h in range(4):
                lse_ref[:, 4 * g + h:4 * g + h + 1] = lse[h * tq:(h + 1) * tq]

    return _pallas(
        body, name="swa_fwd", grid=(B, L // tq),
        in_specs=[_tile_spec(tq, 512), _full_spec(L, 128), _full_spec(L, 128), pl.BlockSpec((1, 8), lambda b, i: (0, 0))],
        out_specs=[_tile_spec(tq, 512), _tile_spec(tq, 8)],
        out_shape=[_sds((B, L, 512), BF16), _sds((B, L, 8), F32)],
        scratch_shapes=[pltpu.VMEM((L, 512), BF16), pltpu.VMEM((L, 512), BF16)],
        compiler_params=_params(("parallel", "arbitrary")),
    )(q, k, v, sink)


def _swa_bwd(q, k, v, sink, lse, do, n, tq):
    B, L, _ = q.shape

    def body(q_ref, k_ref, v_ref, sink_ref, lse_ref, do_ref, dq_ref, dk_ref, dv_ref, dsink_ref, krep, vrep, dk_acc, dv_acc):
        i = pl.program_id(1)

        @pl.when(i == 0)
        def _():
            dk_acc[...] = jnp.zeros_like(dk_acc)
            dv_acc[...] = jnp.zeros_like(dv_acc)
            dsink_ref[...] = jnp.zeros_like(dsink_ref)
            _swa_fill(krep, k_ref)
            _swa_fill(vrep, v_ref)

        for g in range(2):
            gc = slice(256 * g, 256 * g + 256)
            qs, kl, kc, s1, s2, sk, lo, W = _swa_scores(i, g, tq, n, L, q_ref, krep, sink_ref)
            vl, vc = vrep[pl.ds(lo, W), gc], vrep[n:L, gc]
            lse = jnp.concatenate([lse_ref[:, 4 * g + h:4 * g + h + 1] for h in range(4)], axis=0)
            p1, p2, ps = jnp.exp(s1 - lse), jnp.exp(s2 - lse), jnp.exp(sk - lse)
            dos = _swa_head_rows(do_ref[:, gc].astype(BF16))
            dp1, dp2 = _dot_nt(dos, vl), _dot_nt(dos, vc)
            delta = jnp.sum(p1 * dp1, axis=1, keepdims=True) + jnp.sum(p2 * dp2, axis=1, keepdims=True)
            ds1 = (p1 * (dp1 - delta) * SWA_SCALE).astype(BF16)
            ds2 = (p2 * (dp2 - delta) * SWA_SCALE).astype(BF16)
            dq_ref[:, gc] = _swa_fold_rows(_dot(ds1, kl) + _dot(ds2, kc), tq)
            dk_acc[pl.ds(lo, W), gc] += _dot_tn(ds1, qs)
            dk_acc[n:L, gc] += _dot_tn(ds2, qs)
            dv_acc[pl.ds(lo, W), gc] += _dot_tn(p1.astype(BF16), dos)
            dv_acc[n:L, gc] += _dot_tn(p2.astype(BF16), dos)
            dsk = -ps * delta
            for h in range(4):
                dsink_ref[0:1, 4 * g + h:4 * g + h + 1] += jnp.sum(dsk[h * tq:(h + 1) * tq], axis=0, keepdims=True)

        @pl.when(i == L // tq - 1)
        def _():
            for acc, out in ((dk_acc, dk_ref), (dv_acc, dv_ref)):
                for g in range(2):
                    tot = acc[:, 256 * g:256 * g + 64]
                    for h in range(1, 4):
                        tot = tot + acc[:, 256 * g + 64 * h:256 * g + 64 * h + 64]
                    out[:, 64 * g:64 * g + 64] = tot

    return _pallas(
        body, name="swa_bwd", grid=(B, L // tq),
        in_specs=[_tile_spec(tq, 512), _full_spec(L, 128), _full_spec(L, 128), pl.BlockSpec((1, 8), lambda b, i: (0, 0)),
                  _tile_spec(tq, 8), pl.BlockSpec((None, tq, 512), lambda b, i: (b, i, 1))],
        out_specs=[_tile_spec(tq, 512), _full_spec(L, 128), _full_spec(L, 128),
                   pl.BlockSpec((None, 1, 8), lambda b, i: (b, 0, 0))],
        out_shape=[_sds((B, L, 512), F32), _sds((B, L, 128), F32), _sds((B, L, 128), F32), _sds((B, 1, 8), F32)],
        scratch_shapes=[pltpu.VMEM((L, 512), BF16), pltpu.VMEM((L, 512), BF16),
                        pltpu.VMEM((L, 512), F32), pltpu.VMEM((L, 512), F32)],
        compiler_params=_params(("parallel", "arbitrary")),
    )(q, k, v, sink, lse, do)


def _ret_decay(i, tq, L, n):
    qi = i * tq + lax.broadcasted_iota(jnp.int32, (tq, 1), 0)
    q_latent = qi < n
    d_x = (qi - lax.broadcasted_iota(jnp.int32, (1, n), 1)).astype(F32)
    kc = lax.broadcasted_iota(jnp.int32, (1, L - n), 1)
    d_hf = (jnp.where(q_latent, qi, qi - L) - (kc - (L - n))).astype(F32)
    d_hb = (n + kc - qi).astype(F32)
    return q_latent, d_x, d_hf, d_hb


def _ret_head(a, h):
    lane = lax.broadcasted_iota(jnp.int32, (1, a.shape[1]), 1)
    return jnp.where(lane // 64 == h, a, jnp.zeros_like(a))


def _ret_mask(dist, lg):
    return jnp.where(dist >= 0.0, jnp.exp(lg * jnp.maximum(dist, 0.0)), 0.0)


def _ret_weights(q_latent, d_x, d_hf, d_hb, lg_f, lg_b, with_grad):
    e_x = jnp.exp(jnp.where(d_x >= 0.0, lg_f, -lg_b) * d_x)
    e_x = jnp.where(q_latent, e_x, 0.0)
    dec_x = jnp.where(d_x == 0.0, 2.0 * e_x, e_x)
    m_f, m_b = _ret_mask(d_hf, lg_f), _ret_mask(d_hb, lg_b)
    dec_h = m_f + m_b
    if not with_grad:
        return (dec_x, dec_h), None, None
    w_f = (e_x * jnp.maximum(d_x, 0.0), m_f * jnp.maximum(d_hf, 0.0))
    w_b = (e_x * jnp.maximum(-d_x, 0.0), m_b * jnp.maximum(d_hb, 0.0))
    return (dec_x, dec_h), w_f, w_b


def _ret_fwd(q, k, v, lg, n, tq):
    B, L, _ = q.shape

    def body(q_ref, k_ref, v_ref, lg_ref, y_ref):
        dist = _ret_decay(pl.program_id(1), tq, L, n)
        for h in range(4):
            dec, _, _ = _ret_weights(*dist, lg_ref[0:1, h:h + 1], lg_ref[1:2, h:h + 1], False)
            qh = _ret_head(q_ref[...], h)
            y = None
            for rows, dec_r in zip((slice(0, n), slice(n, L)), dec):
                a = _dot_nt(qh, k_ref[rows, :]) * dec_r
                part = _dot(a.astype(BF16), v_ref[rows, 128 * h:128 * h + 128])
                y = part if y is None else y + part
            y_ref[:, 128 * h:128 * h + 128] = y

    return _pallas(
        body, name="ret_fwd", grid=(B, L // tq),
        in_specs=[_tile_spec(tq, 256), _full_spec(L, 256), _full_spec(L, 512), pl.BlockSpec((2, 4), lambda b, i: (0, 0))],
        out_specs=_tile_spec(tq, 512), out_shape=_sds((B, L, 512), F32),
        compiler_params=_params(("parallel", "arbitrary")),
    )(q, k, v, lg)


def _ret_bwd(q, k, v, lg, dy, n, tq):
    B, L, _ = q.shape

    def body(q_ref, k_ref, v_ref, lg_ref, dy_ref, dq_ref, dk_ref, dv_ref, dlg_ref):
        i = pl.program_id(1)

        @pl.when(i == 0)
        def _():
            dk_ref[...] = jnp.zeros_like(dk_ref)
            dv_ref[...] = jnp.zeros_like(dv_ref)
            dlg_ref[...] = jnp.zeros_like(dlg_ref)

        def total(a):
            return jnp.sum(jnp.sum(a, axis=1, keepdims=True), axis=0, keepdims=True)

        dist = _ret_decay(i, tq, L, n)
        dq = None
        for h in range(4):
            vc = slice(128 * h, 128 * h + 128)
            dec, w_f, w_b = _ret_weights(*dist, lg_ref[0:1, h:h + 1], lg_ref[1:2, h:h + 1], True)
            qh = _ret_head(q_ref[...], h)
            dyh = dy_ref[:, vc].astype(BF16)
            dqh = None
            for rows, dec_r, wf_r, wb_r in zip((slice(0, n), slice(n, L)), dec, w_f, w_b):
                k_all, vh = k_ref[rows, :], v_ref[rows, vc]
                s = _dot_nt(qh, k_all)
                gr = _dot_nt(dyh, vh)
                ds = (gr * dec_r).astype(BF16)
                part = _dot(ds, k_all)
                dqh = part if dqh is None else dqh + part
                dk_ref[rows, :] += _dot_tn(ds, qh)
                dv_ref[rows, vc] += _dot_tn((s * dec_r).astype(BF16), dyh)
                gs = gr * s
                dlg_ref[0:1, h:h + 1] += total(gs * wf_r)
                dlg_ref[1:2, h:h + 1] += total(gs * wb_r)
            dqh = _ret_head(dqh, h)
            dq = dqh if dq is None else dq + dqh
        dq_ref[...] = dq

    return _pallas(
        body, name="ret_bwd", grid=(B, L // tq),
        in_specs=[_tile_spec(tq, 256), _full_spec(L, 256), _full_spec(L, 512), pl.BlockSpec((2, 4), lambda b, i: (0, 0)),
                  _tile_spec(tq, 512)],
        out_specs=[_tile_spec(tq, 256), _full_spec(L, 256), _full_spec(L, 512),
                   pl.BlockSpec((None, 2, 4), lambda b, i: (b, 0, 0))],
        out_shape=[_sds((B, L, 256), F32), _sds((B, L, 256), F32), _sds((B, L, 512), F32), _sds((B, 2, 4), F32)],
        compiler_params=_params(("parallel", "arbitrary")),
    )(q, k, v, lg, dy)


def _loss_head(s, target, g, n, tm):
    B, L, D = s.shape
    nx = n // tm

    def body(s_ref, t_ref, g_ref, ds_ref, dg_ref, loss_ref):
        b, i = pl.program_id(0), pl.program_id(1)

        @pl.when((b == 0) & (i == 0))
        def _():
            dg_ref[...] = jnp.zeros_like(dg_ref)
            loss_ref[...] = jnp.zeros_like(loss_ref)

        @pl.when(i < nx)
        def _():
            y, vjp = jax.vjp(_rms, s_ref[...], g_ref[...])
            err = y - t_ref[...]
            d_s, d_g = vjp(err * (1.0 / D))
            ds_ref[...] = d_s
            dg_ref[...] += d_g
            part = jnp.sum(jnp.sum(err * err, axis=1, keepdims=True), axis=0, keepdims=True) * (0.5 / D)
            loss_ref[...] += jnp.broadcast_to(part, loss_ref.shape)

        @pl.when(i >= nx)
        def _():
            ds_ref[...] = jnp.zeros_like(ds_ref)

    return _pallas(
        body, name="loss_head", grid=(B, L // tm),
        in_specs=[pl.BlockSpec((None, tm, D), lambda b, i: (b, i, 0)),
                  pl.BlockSpec((None, tm, D), lambda b, i: (b, jnp.minimum(i, nx - 1), 0)),
                  pl.BlockSpec((1, D), lambda b, i: (0, 0))],
        out_specs=[pl.BlockSpec((None, tm, D), lambda b, i: (b, i, 0)), pl.BlockSpec((1, D), lambda b, i: (0, 0)),
                   pl.BlockSpec((1, 128), lambda b, i: (0, 0))],
        out_shape=[_sds((B, L, D), F32), _sds((1, D), F32), _sds((1, 128), F32)],
        compiler_params=_params(("arbitrary", "arbitrary")),
    )(s, target, g)


def _ada_fwd(c_all, ada_w, ada_b):
    NL, D, Ns = ada_w.shape
    R = c_all.shape[0]

    def body(c_ref, w_ref, b_ref, o_ref):
        cond = jax.nn.silu(c_ref[...]).astype(BF16)
        o_ref[...] = _dot(cond, w_ref[...].astype(BF16)) + b_ref[...]

    return _pallas(
        body, name="ada_fwd", grid=(NL,),
        in_specs=[pl.BlockSpec((R, D), lambda l: (0, 0)), pl.BlockSpec((None, D, Ns), lambda l: (l, 0, 0)),
                  pl.BlockSpec((None, 1, Ns), lambda l: (l, 0, 0))],
        out_specs=pl.BlockSpec((None, R, Ns), lambda l: (l, 0, 0)), out_shape=_sds((NL, R, Ns), F32),
        compiler_params=_params(("parallel",)),
    )(c_all, ada_w, ada_b)


def _ada_bwd(c_all, ada_w, dmods):
    NL, D, Ns = ada_w.shape
    R = c_all.shape[0]

    def body(c_ref, w_ref, dm_ref, dw_ref, dc_ref):
        cond = jax.nn.silu(c_ref[...]).astype(BF16)
        dm = dm_ref[...].astype(BF16)
        dw_ref[...] = _dot_tn(cond, dm)
        dc_ref[...] = _dot_nt(dm, w_ref[...].astype(BF16))

    return _pallas(
        body, name="ada_bwd", grid=(NL,),
        in_specs=[pl.BlockSpec((R, D), lambda l: (0, 0)), pl.BlockSpec((None, D, Ns), lambda l: (l, 0, 0)),
                  pl.BlockSpec((None, R, Ns), lambda l: (l, 0, 0))],
        out_specs=[pl.BlockSpec((None, D, Ns), lambda l: (l, 0, 0)), pl.BlockSpec((None, R, D), lambda l: (l, 0, 0))],
        out_shape=[_sds((NL, D, Ns), F32), _sds((NL, R, D), F32)],
        compiler_params=_params(("parallel",)),
    )(c_all, ada_w, dmods)


def _my_index():
    return 4 * lax.axis_index("x") + 2 * lax.axis_index("y") + lax.axis_index("c")


def _peer(k):
    x, y, c = lax.axis_index("x"), lax.axis_index("y"), lax.axis_index("c")
    kx, ky, kc = (k >> 2) & 1, (k >> 1) & 1, k & 1
    px, py, pc = (x + kx) % 2, (y + ky) % 2, (c + kc) % 2
    return (px, py, pc), 4 * px + 2 * py + pc


def _all_gather(name, shards):
    na = len(shards)
    hbm = pl.BlockSpec(memory_space=pl.ANY)

    def body(*refs):
        in_refs, out_refs = refs[:na], refs[na:2 * na]
        send_sems, recv_sems, local_sems = refs[2 * na:]
        me = _my_index()
        sib_id, sib = _peer(1)
        chips = [_peer(k) for k in (4, 2, 6)]
        sib_chips = [4 * px + 2 * py + (1 - pc) for (px, py, pc), _ in chips]

        def copy(a, k, slot, to, src=None):
            dst = out_refs[a].at[slot]
            return pltpu.make_async_remote_copy(
                src_ref=dst if src is None else src, dst_ref=dst, send_sem=send_sems.at[a, k],
                recv_sem=recv_sems.at[a, k], device_id=to, device_id_type=MESH)

        first, passed, mine = [], [], []
        for a in range(na):
            cp = pltpu.make_async_copy(in_refs[a], out_refs[a].at[me], local_sems.at[a])
            cp.start()
            mine.append(cp)
            first.append(copy(a, 0, me, sib_id, src=in_refs[a]))
            first += [copy(a, 1 + j, me, pid, src=in_refs[a]) for j, (pid, _) in enumerate(chips)]
        for cp in first:
            cp.start()
        for a in range(na):
            for j, (pid, pidx) in enumerate(chips):
                copy(a, 1 + j, pidx, pid).wait_recv()
                fwd = copy(a, 4 + j, pidx, sib_id)
                fwd.start()
                passed.append(fwd)
        for a in range(na):
            copy(a, 0, sib, sib_id).wait_recv()
            for j in range(3):
                copy(a, 4 + j, sib_chips[j], sib_id).wait_recv()
        for cp in first + passed:
            cp.wait_send()
        for cp in mine:
            cp.wait()

    return _pallas(
        body, name=name, in_specs=[hbm] * na, out_specs=[hbm] * na,
        out_shape=[_sds((N_DEV,) + s.shape, s.dtype) for s in shards],
        scratch_shapes=[pltpu.SemaphoreType.DMA((na, 7)), pltpu.SemaphoreType.DMA((na, 7)),
                        pltpu.SemaphoreType.DMA((na,))],
    )(*shards)


def _all_to_all(name, parts):
    na = len(parts)
    hbm = pl.BlockSpec(memory_space=pl.ANY)

    def body(*refs):
        in_refs, out_refs = refs[:na], refs[na:2 * na]
        send_sems, recv_sems, local_sems = refs[2 * na:]
        me = _my_index()
        copies = []
        for a in range(na):
            cp = pltpu.make_async_copy(in_refs[a].at[me], out_refs[a].at[me], local_sems.at[a])
            cp.start()
            copies.append(cp)
            for k in range(1, N_DEV):
                pid, pidx = _peer(k)
                cp = pltpu.make_async_remote_copy(
                    src_ref=in_refs[a].at[pidx], dst_ref=out_refs[a].at[me], send_sem=send_sems.at[a, k - 1],
                    recv_sem=recv_sems.at[a, k - 1], device_id=pid, device_id_type=MESH)
                cp.start()
                copies.append(cp)
        for cp in copies:
            cp.wait()

    return _pallas(
        body, name=name, in_specs=[hbm] * na, out_specs=[hbm] * na,
        out_shape=[_sds(p.shape, p.dtype) for p in parts],
        scratch_shapes=[pltpu.SemaphoreType.DMA((na, 7)), pltpu.SemaphoreType.DMA((na, 7)),
                        pltpu.SemaphoreType.DMA((na,))],
    )(*parts)


_HBM = pl.BlockSpec(memory_space=pltpu.HBM)
_SEM = pl.BlockSpec(memory_space=pltpu.SEMAPHORE)
_DATAFLOW = pltpu.SideEffectType.DATAFLOW_SIDE_EFFECTING


def _exchange_start(name, srcs, slotted, after=()):
    na = len(srcs)
    lands = [lax.empty((N_DEV,) + (s.shape[1:] if slotted else s.shape), s.dtype) for s in srcs]

    def body(*refs):
        src_refs, land_refs = refs[:na], refs[na:2 * na]
        outs = refs[2 * na + len(after):]
        send_sems, recv_sems, token = outs[:na], outs[na:2 * na], outs[4 * na]
        me = _my_index()
        for a in range(na):
            for k in range(1, N_DEV):
                pid, pidx = _peer(k)
                pltpu.make_async_remote_copy(
                    src_ref=src_refs[a].at[pidx] if slotted else src_refs[a], dst_ref=land_refs[a].at[me],
                    send_sem=send_sems[a], recv_sem=recv_sems[a], device_id=pid, device_id_type=MESH).start()
        token[...] = jnp.zeros_like(token)

    ops = [pltpu.with_memory_space_constraint(a, pltpu.HBM) for a in list(srcs) + lands]
    outs = _pallas(
        body, name=name,
        out_shape=[pltpu.SemaphoreType.DMA(())] * (2 * na) + [pltpu.HBM(a.shape, a.dtype) for a in ops]
        + [_sds((8, 128), F32)],
        in_specs=[_HBM] * (2 * na) + [pl.BlockSpec(memory_space=pl.ANY)] * len(after),
        out_specs=[_SEM] * (2 * na) + [_HBM] * (2 * na) + [pl.BlockSpec(memory_space=pltpu.VMEM)],
        input_output_aliases={a: 2 * na + a for a in range(2 * na)},
        compiler_params=pltpu.CompilerParams(has_side_effects=_DATAFLOW),
    )(*ops, *after)
    return (na, outs[:2 * na], outs[2 * na:4 * na]), outs[4 * na]


def _exchange_wait(name, handle, after):
    na, sems, thru = handle

    def body(*refs):
        land_refs = refs[na:2 * na]
        send_sems, recv_sems = refs[2 * na:3 * na], refs[3 * na:4 * na]
        me_id = (lax.axis_index("x"), lax.axis_index("y"), lax.axis_index("c"))
        for a in range(na):
            seven = land_refs[a].at[pl.ds(0, N_DEV - 1)]
            drain = pltpu.make_async_remote_copy(src_ref=seven, dst_ref=seven, send_sem=send_sems[a],
                                                 recv_sem=recv_sems[a], device_id=me_id, device_id_type=MESH)
            drain.wait_send()
            drain.wait_recv()

    outs = _pallas(
        body, name=name, out_shape=[pltpu.HBM(a.shape, a.dtype) for a in thru],
        in_specs=[_HBM] * (2 * na) + [_SEM] * (2 * na) + [pl.BlockSpec(memory_space=pl.ANY)],
        out_specs=[_HBM] * (2 * na), input_output_aliases={a: a for a in range(2 * na)},
        compiler_params=pltpu.CompilerParams(has_side_effects=_DATAFLOW),
    )(*thru, *sems, after)
    return outs[na:]


def _with_own(landed, own, slot_axis=0):
    me = _my_index()
    slot = lax.broadcasted_iota(jnp.int32, landed.shape, slot_axis)
    return jnp.where(slot == me, own, landed)


def _rope_tables(n, L, width):
    t = jnp.arange(n)
    row = (t // GRID_W).astype(F32)
    col = (t % GRID_W).astype(F32)
    n_freq = ROPE_DIM // 4
    freqs = ROPE_THETA ** (-jnp.arange(n_freq, dtype=F32) / n_freq)
    ang = jnp.concatenate([row[:, None] * freqs, col[:, None] * freqs], axis=-1)
    cos, sin = jnp.cos(ang), jnp.sin(ang)
    cos2 = jnp.concatenate([cos, cos], axis=-1)
    sin2 = jnp.concatenate([-sin, sin], axis=-1)
    cos2 = jnp.concatenate([cos2, jnp.ones((L - n, ROPE_DIM), F32)], axis=0)
    sin2 = jnp.concatenate([sin2, jnp.zeros((L - n, ROPE_DIM), F32)], axis=0)
    reps = width // ROPE_DIM
    return jnp.tile(cos2, (1, reps)), jnp.tile(sin2, (1, reps))


def _ab_perm(w):
    return jnp.concatenate([w[:, 0:256], w[:, 320:1600], w[:, 256:320], w[:, 256:320]], axis=1)


def _ab_unperm(g):
    rope_key = (g[:, 1536:1600].astype(F32) + g[:, 1600:1664].astype(F32)).astype(g.dtype)
    return jnp.concatenate([g[:, 0:256], rope_key, g[:, 256:1536]], axis=1)


def _wq_perm(w):
    return jnp.concatenate([w[:, 192 * h:192 * h + 128] for h in range(4)]
                           + [w[:, 192 * h + 128:192 * h + 192] for h in range(4)], axis=1)


def _wq_unperm(g):
    return jnp.concatenate([g[:, sl] for h in range(4)
                            for sl in (slice(128 * h, 128 * h + 128), slice(512 + 64 * h, 576 + 64 * h))], axis=1)


def _flat(a):
    return a.reshape(-1, a.shape[-1])


def _layer_weights(full, p):
    NL, NE, NO = len(full["ffn_in"]), len(full["ab_in"]), len(full["cd_in"])
    groups = range(4)

    def each(f, mats):
        return [None if w is None else f(w) for w in mats]

    return dict(
        norm_mix=[p["norm_mix"][l][None] for l in range(NL)], norm_ffn=[p["norm_ffn"][l][None] for l in range(NL)],
        norm_final=p["norm_final"][None],
        ffn_in=each(_ffn_interleave, full["ffn_in"]), ffn_out=list(full["ffn_out"]),
        ab_in=each(_ab_perm, full["ab_in"]), ab_out=list(full["ab_out"]),
        wq=each(_wq_perm, full["mla_wq_b"]), wkv=list(full["mla_wkv_b"]),
        kv_norm=[p["mla_kv_norm"][j][None] for j in range(NE)], q_norm=[p["mla_q_norm"][j][None] for j in range(NE)],
        v_norm=[[p["cmlp_v_norm"][j][None, 128 * g:128 * g + 128] for g in groups] for j in range(NE)],
        ws=[[p["cmlp_ws"][j, g] for g in groups] for j in range(NE)],
        bs=[[p["cmlp_bs"][j, g][:, None] for g in groups] for j in range(NE)],
        cd_in=list(full["cd_in"]), cd_out=list(full["cd_out"]),
        lg=[jnp.stack([jax.nn.log_sigmoid(p["ret_decay_fwd"][j]), jax.nn.log_sigmoid(p["ret_decay_bwd"][j])])
            for j in range(NO)],
        sink=[p["swa_sink"][j][None] for j in range(NO)],
        ret_norm=[[p["ret_norm"][j][None, 128 * g:128 * g + 128] for g in groups] for j in range(NO)],
    )


def _local_step(x, ctx, target, mods, W, later_weights=None, early_grads=None):
    B, n, D = x.shape
    m = ctx.shape[1]
    L = n + m
    NL = mods.shape[0]
    tm = min(256, m)
    tq = min(256, m)
    cos512, sin512 = _rope_tables(n, L, 512)
    s = jnp.concatenate([x, ctx], axis=1)
    saved = []

    def rows(name, f, arrays, pieces, samp, samp_pieces, glob, out_arrays, out_pieces, tile=tm):
        return _Rows(name, f, B, L, n, tile, arrays, pieces, samp, samp_pieces, glob, out_arrays, out_pieces)

    def full(width, start=0):
        return [(0, start, width)]

    for l in range(NL):
        j = l // 2
        even = l % 2 == 0
        md = mods[l]
        if l == 1 and later_weights is not None:
            W = later_weights(1, s)
        r = {}
        if l == 0:
            r["pre1"] = rows("pre_mix0", _f_pre, [s], full(D), [md], [(0, 0), (0, 1)], [W["norm_mix"][0]], [(D, BF16)], full(D))
            (xn,) = r["pre1"].fwd()
        r["xn"] = xn
        if even:
            z = _mm(f"ab_in{l}", _flat(xn), W["ab_in"][j]).reshape(B, L, 1664)
            pieces = [(0, 0, 256), (0, 256, 256), (0, 1536, 128)]
            pieces += [(0, 512 + 128 * g, 128) for g in range(4)] + [(0, 1024 + 128 * g, 128) for g in range(4)]
            pieces += [(1, 0, 128), (2, 0, 128)]
            glob = [W["kv_norm"][j], W["q_norm"][j]] + W["v_norm"][j] + W["ws"][j] + W["bs"][j]
            abp = rows(f"ab_pre{l}", _f_ab_pre, [z, cos512, sin512], pieces, [], [], glob,
                       [(256, BF16), (256, BF16), (128, BF16), (512, BF16)],
                       [(0, 0, 256), (1, 0, 256), (2, 0, 128)] + [(3, 128 * g, 128) for g in range(4)], tile=128)
            kvn, qn, kpe, cm = abp.fwd()
            kv = _mm(f"wkv{l}", _flat(kvn), W["wkv"][j], out_dtype=BF16).reshape(B, L, 1024)
            q0 = _mm(f"wq{l}", _flat(qn), W["wq"][j]).reshape(B, L, 768)
            qrp = rows(f"q_rope{l}", _f_q_rope, [q0, cos512, sin512], [(0, 0, 512), (0, 512, 256), (1, 0, 256), (2, 0, 256)],
                       [], [], [], [(768, BF16)], [(0, 0, 512), (0, 512, 256)])
            (q,) = qrp.fwd()
            o, lse = _mla_fwd(q, kv, kpe, n, tq)
            y = _mm(f"ab_out_a{l}", _flat(o), W["ab_out"][j][:512])
            y = _mm(f"ab_out_b{l}", _flat(cm), W["ab_out"][j][512:], add=y).reshape(B, L, D)
            r.update(z=z, abp=abp, kvn=kvn, qn=qn, kpe=kpe, cm=cm, kv=kv, qrp=qrp, q=q, o=o, lse=lse)
        else:
            z = _mm(f"cd_in{l}", _flat(xn), W["cd_in"][j]).reshape(B, L, 2304)
            pieces = [(0, 0, 256), (0, 256, 512), (0, 768, 128), (0, 896, 128), (0, 1024, 256), (0, 1280, 512), (0, 1792, 512)]
            pieces += [(1, 0, 256), (2, 0, 256), (1, 0, 128), (2, 0, 128), (1, 0, 512), (2, 0, 512)]
            cdp = rows(f"cd_pre{l}", _f_cd_pre, [z, cos512, sin512], pieces, [], [], [],
                       [(256, BF16), (256, BF16), (512, BF16), (512, BF16), (128, BF16), (128, BF16), (512, F32)],
                       [(k_, 0, w_) for k_, w_ in enumerate((256, 256, 512, 512, 128, 128, 512))])
            rq, rk, rv, sq, sk, sv, rg = cdp.fwd()
            yret = _ret_fwd(rq, rk, rv, W["lg"][j], n, tq)
            osw, lse = _swa_fwd(sq, sk, sv, W["sink"][j], n, tq)
            mrg = rows(f"cd_merge{l}", _f_cd_merge, [yret, rg],
                       [(0, 128 * g, 128) for g in range(4)] + [(1, 128 * g, 128) for g in range(4)], [], [],
                       W["ret_norm"][j], [(512, BF16)], [(0, 128 * g, 128) for g in range(4)])
            (yr,) = mrg.fwd()
            y = _mm(f"cd_out_a{l}", _flat(yr), W["cd_out"][j][:512])
            y = _mm(f"cd_out_b{l}", _flat(osw), W["cd_out"][j][512:], add=y).reshape(B, L, D)
            r.update(z=z, cdp=cdp, rq=rq, rk=rk, rv=rv, sq=sq, sk=sk, sv=sv, rg=rg, yret=yret, osw=osw, lse=lse,
                     mrg=mrg, yr=yr)
        two, outs2 = [(0, 0, D), (1, 0, D)], [(D, F32), (D, BF16)]
        r["mix_out"] = rows(f"mix_res_pre{l}", _f_res_pre, [s, y], two, [md], [(0, 2), (0, 3), (0, 4)],
                            [W["norm_ffn"][l]], outs2, two)
        s1, xn2 = r["mix_out"].fwd()
        if l == 0 and later_weights is not None:
            W = later_weights(0, s1)
        y2 = _ffn_fwd(f"ffn{l}", _flat(xn2), W["ffn_in"][l], W["ffn_out"][l]).reshape(B, L, D)
        if l < NL - 1:
            r["ffn_out"] = rows(f"ffn_res_pre{l}", _f_res_pre, [s1, y2], two, [md, mods[l + 1]], [(0, 5), (1, 0), (1, 1)],
                                [W["norm_mix"][l + 1]], outs2, two)
            s, xn = r["ffn_out"].fwd()
        else:
            r["ffn_out"] = rows(f"res_ffn{l}", _f_res, [s1, y2], two, [md], [(0, 5)], [], [(D, F32)], full(D))
            (s,) = r["ffn_out"].fwd()
        r["xn2"] = xn2
        saved.append(r)

    ds, d_norm_final, loss = _loss_head(s, target, W["norm_final"], n, tm)

    G = {k: [None] * len(v) for k, v in W.items() if isinstance(v, list)}
    G["norm_final"] = d_norm_final
    dm = [[None] * 6 for _ in range(NL)]
    dxn_next = None
    for l in reversed(range(NL)):
        j = l // 2
        even = l % 2 == 0
        r = saved[l]
        zero = early_grads(0, G) if l == 0 and early_grads is not None else None
        if l == NL - 1:
            (ds1, dy2), (dm[l][5],), _ = r["ffn_out"].bwd([ds], {0: F32, 1: BF16}, unread=(0,))
        else:
            smp = None if zero is None else [mods[l] + zero, mods[l + 1] + zero]
            (ds1, dy2), (dm[l][5], dm[l + 1][0], dm[l + 1][1]), (G["norm_mix"][l + 1],) = r["ffn_out"].bwd(
                [ds, dxn_next], {0: F32, 1: BF16}, grad_glob=(0,), samp=smp)
        dy2f, xn2f = _flat(dy2), _flat(r["xn2"])
        hid, dz2 = _ffn_mid_bwd(f"ffn_mid{l}", xn2f, dy2f, W["ffn_in"][l], W["ffn_out"][l])
        G["ffn_out"][l] = _mm(f"g_ffn_out{l}", hid, dy2f, ta=True, out_dtype=BF16)
        G["ffn_in"][l] = _mm(f"g_ffn_in{l}", xn2f, dz2, ta=True, out_dtype=BF16)
        dxn2 = _mm(f"d_xn2{l}", dz2, W["ffn_in"][l], tb=True).reshape(B, L, D)
        zero = early_grads(1, G) if l == 0 and early_grads is not None else None
        smp = None if zero is None else [mods[l] + zero]
        (ds0, dy), (dm[l][2], dm[l][3], dm[l][4]), (G["norm_ffn"][l],) = r["mix_out"].bwd(
            [ds1, dxn2], {0: F32, 1: BF16}, grad_glob=(0,), samp=smp)
        dyf = _flat(dy)
        if even:
            w_out = W["ab_out"][j]
            dcat = _mm(f"d_cat{l}", dyf, w_out, tb=True).reshape(B, L, -1)
            G["ab_out"][l // 2] = jnp.concatenate(
                [_mm(f"g_ab_out_a{l}", _flat(r["o"]), dyf, ta=True, out_dtype=BF16),
                 _mm(f"g_ab_out_b{l}", _flat(r["cm"]), dyf, ta=True, out_dtype=BF16)], axis=0)
            dq, dkv, dkpe = _mla_bwd(r["q"], r["kv"], r["kpe"], r["lse"], dcat, n, tq)
            (dq0,), _, _ = r["qrp"].bwd([dq], {0: BF16}, unread=(0,))
            dq0f, dkvf = _flat(dq0), _flat(dkv)
            G["wq"][j] = _mm(f"g_wq{l}", _flat(r["qn"]), dq0f, ta=True, out_dtype=BF16)
            G["wkv"][j] = _mm(f"g_wkv{l}", _flat(r["kvn"]), dkvf, ta=True, out_dtype=BF16)
            dqn = _mm(f"d_qn{l}", dq0f, W["wq"][j], tb=True).reshape(B, L, 256)
            dkvn = _mm(f"d_kvn{l}", dkvf, W["wkv"][j], tb=True).reshape(B, L, 256)
            (dz,), _, gg = r["abp"].bwd([dkvn, dqn, dkpe, (dcat, 512)], {0: BF16}, grad_glob=tuple(range(14)))
            G["kv_norm"][j], G["q_norm"][j] = gg[0], gg[1]
            G["v_norm"][j], G["ws"][j], G["bs"][j] = list(gg[2:6]), list(gg[6:10]), list(gg[10:14])
            w_in, key = W["ab_in"][j], "ab_in"
        else:
            w_out = W["cd_out"][j]
            dcat = _mm(f"d_cat{l}", dyf, w_out, tb=True).reshape(B, L, -1)
            G["cd_out"][j] = jnp.concatenate(
                [_mm(f"g_cd_out_a{l}", _flat(r["yr"]), dyf, ta=True, out_dtype=BF16),
                 _mm(f"g_cd_out_b{l}", _flat(r["osw"]), dyf, ta=True, out_dtype=BF16)], axis=0)
            (dyret, drg), _, gg = r["mrg"].bwd([(dcat, 0)], {0: F32, 1: F32}, grad_glob=(0, 1, 2, 3))
            G["ret_norm"][j] = list(gg)
            drq, drk, drv, dlg = _ret_bwd(r["rq"], r["rk"], r["rv"], W["lg"][j], dyret, n, tq)
            dsq, dsk, dsv, dsink = _swa_bwd(r["sq"], r["sk"], r["sv"], W["sink"][j], r["lse"], dcat, n, tq)
            G["lg"][j], G["sink"][j] = dlg, dsink
            (dz,), _, _ = r["cdp"].bwd([drq, drk, drv, dsq, dsk, dsv, drg], {0: BF16}, unread=(0,))
            w_in, key = W["cd_in"][j], "cd_in"
        dzf = _flat(dz)
        G[key][j] = _mm(f"g_{key}{l}", _flat(r["xn"]), dzf, ta=True, out_dtype=BF16)
        dxn = _mm(f"d_xn{l}", dzf, w_in, tb=True).reshape(B, L, D)
        if l == 0:
            (ds,), (dm[0][0], dm[0][1]), (G["norm_mix"][0],) = r["pre1"].bwd([dxn], {0: F32}, grad_glob=(0,),
                                                                            add={0: ds0}, samp=smp)
        else:
            ds, dxn_next = ds0, dxn
    dmods = jnp.stack([jnp.concatenate(d, axis=2) for d in dm])
    return loss, ds[:, :n], dmods, G


def kernel(x, c, ctx, c_ctx, ada_w, ada_b, norm_mix, norm_ffn, norm_final, ffn_in, ffn_out, ab_in, ab_out, mla_q_norm, mla_kv_norm, mla_wq_b, mla_wkv_b, cmlp_v_norm, cmlp_ws, cmlp_bs, cd_in, cd_out, ret_decay_fwd, ret_decay_bwd, ret_norm, swa_sink, loss_target, m_c_ctx, m_ada_w, m_ada_b, m_norm_mix, m_norm_ffn, m_norm_final, m_ffn_in, m_ffn_out, m_ab_in, m_ab_out, m_mla_q_norm, m_mla_kv_norm, m_mla_wq_b, m_mla_wkv_b, m_cmlp_v_norm, m_cmlp_ws, m_cmlp_bs, m_cd_in, m_cd_out, m_ret_decay_fwd, m_ret_decay_bwd, m_ret_norm, m_swa_sink, v_c_ctx, v_ada_w, v_ada_b, v_norm_mix, v_norm_ffn, v_norm_final, v_ffn_in, v_ffn_out, v_ab_in, v_ab_out, v_mla_q_norm, v_mla_kv_norm, v_mla_wq_b, v_mla_wkv_b, v_cmlp_v_norm, v_cmlp_ws, v_cmlp_bs, v_cd_in, v_cd_out, v_ret_decay_fwd, v_ret_decay_bwd, v_ret_norm, v_swa_sink):
    B, n, D = x.shape
    NL = ada_w.shape[0]
    NE, NO = ab_in.shape[0], cd_in.shape[0]
    me = _my_index()
    weights = dict(c_ctx=c_ctx, ada_w=ada_w, ada_b=ada_b, norm_mix=norm_mix, norm_ffn=norm_ffn, norm_final=norm_final,
                   ffn_in=ffn_in, ffn_out=ffn_out, ab_in=ab_in, ab_out=ab_out, mla_q_norm=mla_q_norm,
                   mla_kv_norm=mla_kv_norm, mla_wq_b=mla_wq_b, mla_wkv_b=mla_wkv_b, cmlp_v_norm=cmlp_v_norm,
                   cmlp_ws=cmlp_ws, cmlp_bs=cmlp_bs, cd_in=cd_in, cd_out=cd_out, ret_decay_fwd=ret_decay_fwd,
                   ret_decay_bwd=ret_decay_bwd, ret_norm=ret_norm, swa_sink=swa_sink)
    moments_m = dict(c_ctx=m_c_ctx, ada_w=m_ada_w, ada_b=m_ada_b, norm_mix=m_norm_mix, norm_ffn=m_norm_ffn,
                     norm_final=m_norm_final, ffn_in=m_ffn_in, ffn_out=m_ffn_out, ab_in=m_ab_in, ab_out=m_ab_out,
                     mla_q_norm=m_mla_q_norm, mla_kv_norm=m_mla_kv_norm, mla_wq_b=m_mla_wq_b, mla_wkv_b=m_mla_wkv_b,
                     cmlp_v_norm=m_cmlp_v_norm, cmlp_ws=m_cmlp_ws, cmlp_bs=m_cmlp_bs, cd_in=m_cd_in, cd_out=m_cd_out,
                     ret_decay_fwd=m_ret_decay_fwd, ret_decay_bwd=m_ret_decay_bwd, ret_norm=m_ret_norm,
                     swa_sink=m_swa_sink)
    moments_v = dict(c_ctx=v_c_ctx, ada_w=v_ada_w, ada_b=v_ada_b, norm_mix=v_norm_mix, norm_ffn=v_norm_ffn,
                     norm_final=v_norm_final, ffn_in=v_ffn_in, ffn_out=v_ffn_out, ab_in=v_ab_in, ab_out=v_ab_out,
                     mla_q_norm=v_mla_q_norm, mla_kv_norm=v_mla_kv_norm, mla_wq_b=v_mla_wq_b, mla_wkv_b=v_mla_wkv_b,
                     cmlp_v_norm=v_cmlp_v_norm, cmlp_ws=v_cmlp_ws, cmlp_bs=v_cmlp_bs, cd_in=v_cd_in, cd_out=v_cd_out,
                     ret_decay_fwd=v_ret_decay_fwd, ret_decay_bwd=v_ret_decay_bwd, ret_norm=v_ret_norm,
                     swa_sink=v_swa_sink)
    order = list(weights)

    Ns = ada_w.shape[2]
    (c_g,) = _all_gather("gather_c", [c])
    R = N_DEV * B + 8
    c_all = jnp.concatenate([c_g.reshape(N_DEV * B, D), jnp.broadcast_to(c_ctx[None], (8, D))], axis=0)
    ada_b_mine = lax.dynamic_slice_in_dim(ada_b, me * Ns, Ns, axis=1)[:, None, :]
    mods_shard = _ada_fwd(c_all, ada_w, ada_b_mine)
    (mods_g,) = _all_gather("gather_mods", [mods_shard])
    mods_full = jnp.transpose(mods_g, (1, 2, 0, 3)).reshape(NL, R, 6, D)
    mx = lax.dynamic_slice_in_dim(mods_full, me * B, B, axis=1)
    mh = jnp.broadcast_to(mods_full[:, N_DEV * B][:, None], (NL, B, 6, D))
    mods = jnp.stack([mx, mh], axis=2)

    big = ["ffn_in", "ffn_out", "ab_in", "ab_out", "cd_in", "cd_out", "mla_wq_b", "mla_wkv_b"]
    col_sharded = {"ffn_in", "ab_in", "cd_in", "mla_wq_b", "mla_wkv_b"}
    shards = {k: _to_bf16("cast_" + k, weights[k]) for k in big}
    first = {k: 0 if k.startswith("cd_") else 1 for k in big}
    a_keys = [k for k in big if first[k] and not k.startswith("ffn_")]
    b_keys = ["ffn_in", "ffn_out"]
    early = _all_gather("gather_wA", [shards[k][:1] for k in a_keys])
    (rn_g,) = _all_gather("gather_ret_norm", [ret_norm])
    small_first = [mods, rn_g] + list(early)
    wb_handle, wb_token = _exchange_start("gather_wB_start", [shards[k][:1] for k in b_keys], False, small_first)
    wc_handle, wc_token = _exchange_start("gather_wC_start", [shards[k][first[k]:] for k in big], False,
                                          small_first + [wb_token])
    mods = mods + (wb_token[0, 0] + wc_token[0, 0])

    def unshard(k, g):
        if k in col_sharded:
            f = jnp.transpose(g, (1, 2, 0, 3)).reshape(g.shape[1], g.shape[2], -1)
        else:
            f = jnp.transpose(g, (1, 0, 2, 3)).reshape(g.shape[1], -1, g.shape[3])
        return [f[i] for i in range(f.shape[0])]

    rn_full = jnp.transpose(rn_g, (1, 0, 2)).reshape(NO, -1)
    small_p = dict(weights, ret_norm=rn_full)
    full0 = {k: [None] * weights[k].shape[0] for k in big}
    for k, g in zip(a_keys, early):
        full0[k][:1] = unshard(k, g)

    def later_weights(stage, newest):
        if stage == 0:
            landed = _exchange_wait("gather_wB_wait", wb_handle, newest)
            for k, land in zip(b_keys, landed):
                full0[k][:1] = unshard(k, _with_own(land, shards[k][:1]))
        else:
            landed = _exchange_wait("gather_wC_wait", wc_handle, newest)
            for k, land in zip(big, landed):
                full0[k][first[k]:] = unshard(k, _with_own(land, shards[k][first[k]:]))
        return _layer_weights(full0, small_p)

    def to_slots(k, gl):
        g = jnp.stack(gl)
        if k == "ffn_in":
            half = N_DEV // 2
            g = g.reshape(g.shape[0], 2, g.shape[2], half, -1)
            return jnp.transpose(g, (1, 3, 0, 2, 4)).reshape(N_DEV, g.shape[0], g.shape[2], -1)
        if k in col_sharded:
            return jnp.transpose(g.reshape(g.shape[0], g.shape[1], N_DEV, -1), (2, 0, 1, 3))
        return jnp.transpose(g.reshape(g.shape[0], N_DEV, -1, g.shape[2]), (1, 0, 2, 3))

    def big_grads(G):
        return dict(ffn_in=[g if g is None else _ffn_deinterleave(g) for g in G["ffn_in"]], ffn_out=G["ffn_out"],
                    ab_in=[g if g is None else _ab_unperm(g) for g in G["ab_in"]],
                    ab_out=G["ab_out"], cd_in=G["cd_in"], cd_out=G["cd_out"],
                    mla_wq_b=[g if g is None else _wq_unperm(g) for g in G["wq"]], mla_wkv_b=G["wkv"])

    sent = {}

    def early_grads(stage, G):
        parts = big_grads(G)
        if stage == 0:
            srcs = [to_slots(k, parts[k][first[k]:]) for k in big]
            handle, g_token = _exchange_start("scatter_gC_start", srcs, slotted=True)
        else:
            srcs = [to_slots(k, parts[k][:1]) for k in b_keys]
            handle, g_token = _exchange_start("scatter_gB_start", srcs, slotted=True)
        sent[stage] = (handle, srcs)
        return g_token[0, 0]

    loss_part, grad_x, dmods, G = _local_step(x, ctx, loss_target, mods, _layer_weights(full0, small_p),
                                              later_weights, early_grads)

    dmx = dmods[:, :, 0].reshape(NL, B, 6 * D)
    dmh = jnp.sum(dmods[:, :, 1], axis=1).reshape(NL, 1, 6 * D)
    (dm_g,) = _all_gather("gather_dmods", [jnp.concatenate([dmx, dmh], axis=1)])
    dmx_all = jnp.transpose(dm_g[:, :, :B], (1, 0, 2, 3)).reshape(NL, N_DEV * B, 6 * D)
    dmh_all = jnp.sum(dm_g[:, :, B], axis=0)
    dm_rows = jnp.concatenate([dmx_all, dmh_all[:, None], jnp.zeros((NL, 7, 6 * D), F32)], axis=1)
    g_ada_b = jnp.sum(dm_rows, axis=1)
    dm_mine = lax.dynamic_slice_in_dim(dm_rows, me * Ns, Ns, axis=2)
    g_ada_w, dcond = _ada_bwd(c_all, ada_w, dm_mine)
    sg = jax.nn.sigmoid(c_ctx)
    d_c_ctx_part = jnp.sum(dcond[:, N_DEV * B], axis=0) * (sg * (1.0 + c_ctx * (1.0 - sg)))

    def cat(parts):
        return jnp.concatenate([p.reshape(-1) for p in parts])

    dlg = jnp.stack([jnp.sum(G["lg"][j], axis=0) for j in range(NO)])
    sig_f, sig_b = jax.nn.sigmoid(-ret_decay_fwd), jax.nn.sigmoid(-ret_decay_bwd)
    small = dict(
        loss=loss_part[0, 0:1],
        c_ctx=d_c_ctx_part,
        norm_mix=cat(G["norm_mix"]), norm_ffn=cat(G["norm_ffn"]), norm_final=G["norm_final"].reshape(-1),
        mla_q_norm=cat(G["q_norm"]), mla_kv_norm=cat(G["kv_norm"]),
        cmlp_v_norm=cat([cat(G["v_norm"][j]) for j in range(NE)]),
        cmlp_ws=cat([jnp.stack(G["ws"][j]) for j in range(NE)]),
        cmlp_bs=cat([jnp.stack([b_[:, 0] for b_ in G["bs"][j]]) for j in range(NE)]),
        ret_decay_fwd=(dlg[:, 0] * sig_f).reshape(-1), ret_decay_bwd=(dlg[:, 1] * sig_b).reshape(-1),
        ret_norm=cat([cat(G["ret_norm"][j]) for j in range(NO)]),
        swa_sink=cat([jnp.sum(G["sink"][j], axis=0) for j in range(NO)]),
    )
    small_keys = list(small)
    sizes = [small[k].shape[0] for k in small_keys]
    total = sum(sizes)
    padded = -(-total // 2048) * 2048
    packed = jnp.concatenate([small[k] for k in small_keys] + [jnp.zeros((padded - total,), F32)]).reshape(-1, 128)
    (small_g,) = _all_gather("gather_small", [packed])

    parts = big_grads(G)
    landed0 = dict(zip(a_keys, _all_to_all("scatter_gA", [to_slots(k, parts[k][:1]) for k in a_keys])))
    for k, land, src in zip(b_keys, _exchange_wait("scatter_gB_wait", sent[1][0], grad_x), sent[1][1]):
        landed0[k] = _with_own(land, src)
    landed = []
    for k, land, src in zip(big, _exchange_wait("scatter_gC_wait", sent[0][0], grad_x), sent[0][1]):
        rest = _with_own(land, src)
        landed.append(jnp.concatenate([landed0[k], rest], axis=1) if first[k] else rest)

    grads, deltas, new_m, new_v = {}, {}, {}, {}
    for k, land in zip(big, landed):
        grads[k], deltas[k], new_m[k], new_v[k] = _adamw_from_slots("adamw_" + k, weights[k], moments_m[k], moments_v[k], land)
    deltas["ada_w"], new_m["ada_w"], new_v["ada_w"] = [
        o.reshape(ada_w.shape) for o in _adamw("adamw_ada_w", _flat(ada_w), _flat(g_ada_w), _flat(m_ada_w), _flat(v_ada_w))]
    grads["ada_w"] = g_ada_w

    sums_only = ("loss", "ret_norm")

    def packed_of(src, fill):
        vals = [jnp.full((sizes[i],), fill, F32) if k in sums_only else src[k].reshape(-1)
                for i, k in enumerate(small_keys)]
        return jnp.concatenate(vals + [jnp.full((padded - total,), fill, F32)]).reshape(-1, 128)

    w_p, m_p, v_p = packed_of(weights, 0.0), packed_of(moments_m, 0.0), packed_of(moments_v, 1.0)

    def f_small(w_, m_, v_, land_):
        g = _sum_slots(land_)
        return (g,) + _adamw_math(w_, g, m_, v_)

    g_p, d_p, nm_p, nv_p = _ew("adamw_small", f_small, [w_p, m_p, v_p, small_g], [F32] * 4)
    offs = np.cumsum([0] + sizes)
    for i, k in enumerate(small_keys):
        sl = slice(int(offs[i]), int(offs[i + 1]))
        if k == "loss":
            loss = g_p.reshape(-1)[int(offs[i])]
        elif k == "ret_norm":
            g_full = g_p.reshape(-1)[sl].reshape(NO, -1)
            g_mine = lax.dynamic_slice_in_dim(g_full, me * ret_norm.shape[1], ret_norm.shape[1], axis=1)
            d_, m_, v_ = _adamw("adamw_ret_norm", *[jnp.pad(a, ((0, 8 - NO), (0, 128 - a.shape[1])), constant_values=cv)
                                                     for a, cv in ((ret_norm, 0.0), (g_mine, 0.0), (m_ret_norm, 0.0), (v_ret_norm, 1.0))])
            grads[k] = g_mine
            deltas[k], new_m[k], new_v[k] = [a[:NO, :ret_norm.shape[1]] for a in (d_, m_, v_)]
        else:
            shp = weights[k].shape
            grads[k], deltas[k], new_m[k], new_v[k] = [a.reshape(-1)[sl].reshape(shp) for a in (g_p, d_p, nm_p, nv_p)]
    pad_b = lambda a, cv=0.0: jnp.pad(a, ((0, 8 - NL), (0, 0)), constant_values=cv)
    d_, m_, v_ = _adamw("adamw_ada_b", pad_b(ada_b), pad_b(g_ada_b), pad_b(m_ada_b), pad_b(v_ada_b, 1.0))
    grads["ada_b"] = g_ada_b
    deltas["ada_b"], new_m["ada_b"], new_v["ada_b"] = d_[:NL], m_[:NL], v_[:NL]

    return (loss, grad_x, *[grads[k] for k in order], *[deltas[k] for k in order],
            *[new_m[k] for k in order], *[new_v[k] for k in order])
```

```python
import functools

import numpy as np
import jax
import jax.numpy as jnp
from jax import lax
from jax.experimental import pallas as pl
from jax.experimental.pallas import tpu as pltpu

F32 = jnp.float32
BF16 = jnp.bfloat16
EPS = 1e-6
NEG_INF = -1e30
N_DEV = 8
GRID_W = 64
ROPE_THETA = 10000.0
ROPE_DIM = 64
SWA_WINDOW = 128
MLA_SCALE = (128 + 64) ** -0.5
SWA_SCALE = 64 ** -0.5
RET_K_SCALE = 64 ** -0.5
ADAM_LR, ADAM_B1, ADAM_B2, ADAM_EPS, ADAM_WD, ADAM_STEP = 0.001, 0.9, 0.999, 1e-08, 0.01, 10
V7X_VMEM_LIMIT = 56 * 1024 * 1024
MESH = pl.DeviceIdType.MESH


def _pallas(body, **kw):
    return pl.pallas_call(body, **kw)


def _params(sem=None):
    return pltpu.CompilerParams(dimension_semantics=sem, vmem_limit_bytes=V7X_VMEM_LIMIT)


def _tile(n, cap, align):
    best = None
    for t in range(align, min(n, cap) + 1, align):
        if n % t == 0:
            best = t
    return n if best is None else best


def _sds(shape, dtype):
    return jax.ShapeDtypeStruct(tuple(shape), dtype)


def _ew(name, f, ins, out_dtypes, cap_elems=131072):
    R, C = ins[0].shape[-2:]
    tr = _tile(R, max(16, cap_elems // C), 16)
    n_in = len(ins)

    def spec(a):
        if a.ndim == 2:
            return pl.BlockSpec((tr, C), lambda i: (i, 0))
        return pl.BlockSpec((a.shape[0], tr, C), lambda i: (0, i, 0))

    def body(*refs):
        outs = f(*[r[...] for r in refs[:n_in]])
        for r, o in zip(refs[n_in:], outs):
            r[...] = o.astype(r.dtype)

    return _pallas(
        body, name=name, grid=(R // tr,), in_specs=[spec(a) for a in ins],
        out_specs=[pl.BlockSpec((tr, C), lambda i: (i, 0)) for _ in out_dtypes],
        out_shape=[_sds((R, C), d) for d in out_dtypes], compiler_params=_params(("parallel",)),
    )(*ins)


def _to_bf16(name, w):
    w2 = w.reshape(-1, w.shape[-1])
    return _ew(name, lambda v: (v,), [w2], [BF16])[0].reshape(w.shape)


def _adamw_math(w, g, m, v):
    m = ADAM_B1 * m + (1.0 - ADAM_B1) * g
    v = ADAM_B2 * v + (1.0 - ADAM_B2) * (g * g)
    m_hat = m / (1.0 - ADAM_B1 ** ADAM_STEP)
    v_hat = v / (1.0 - ADAM_B2 ** ADAM_STEP)
    delta = -ADAM_LR * (m_hat / (jnp.sqrt(v_hat) + ADAM_EPS) + ADAM_WD * w)
    return delta, m, v


def _sum_slots(land):
    g = land[0].astype(F32)
    for s in range(1, land.shape[0]):
        g = g + land[s].astype(F32)
    return g


def _adamw_from_slots(name, w, m, v, land):
    shp = w.shape
    C = shp[-1]

    def f(w_, m_, v_, land_):
        g = _sum_slots(land_)
        return (g,) + _adamw_math(w_, g, m_, v_)

    outs = _ew(name, f, [w.reshape(-1, C), m.reshape(-1, C), v.reshape(-1, C), land.reshape(N_DEV, -1, C)],
               [F32] * 4, cap_elems=65536)
    return [o.reshape(shp) for o in outs]


def _adamw(name, w, g, m, v):
    outs = _ew(name, lambda w_, g_, m_, v_: _adamw_math(w_, g_, m_, v_), [w, g, m, v], [F32] * 3)
    return outs


def _mm(name, a, b, ta=False, tb=False, out_dtype=F32, add=None):
    M, K = (a.shape[1], a.shape[0]) if ta else a.shape
    N = b.shape[0] if tb else b.shape[1]
    tm = _tile(M, 1408, 128)
    tn = _tile(N, 1024, 128)
    if tn < 256 and N <= 2432:
        tn = N
    tk = _tile(K, 1792, 128)
    nk = K // tk
    a_spec = pl.BlockSpec((tk, tm), lambda i, j, k: (k, i)) if ta else pl.BlockSpec((tm, tk), lambda i, j, k: (i, k))
    b_spec = pl.BlockSpec((tn, tk), lambda i, j, k: (j, k)) if tb else pl.BlockSpec((tk, tn), lambda i, j, k: (k, j))
    o_spec = pl.BlockSpec((tm, tn), lambda i, j, k: (i, j))
    dims = (((0 if ta else 1,), (1 if tb else 0,)), ((), ()))
    has_add = add is not None

    def product(a_ref, b_ref):
        return lax.dot_general(a_ref[...].astype(BF16), b_ref[...].astype(BF16), dims, preferred_element_type=F32)

    def body_single(*refs):
        acc = product(refs[0], refs[1])
        if has_add:
            acc = acc + refs[2][...]
        refs[-1][...] = acc.astype(refs[-1].dtype)

    def body(*refs):
        a_ref, b_ref = refs[0], refs[1]
        add_ref = refs[2] if has_add else None
        o_ref, acc = refs[-2], refs[-1]
        k = pl.program_id(2)

        @pl.when(k == 0)
        def _():
            acc[...] = add_ref[...] if has_add else jnp.zeros_like(acc)

        acc[...] += product(a_ref, b_ref)

        @pl.when(k == nk - 1)
        def _():
            o_ref[...] = acc[...].astype(o_ref.dtype)

    ins = [a, b] + ([add] if has_add else [])
    specs = [a_spec, b_spec] + ([o_spec] if has_add else [])
    return _pallas(
        body_single if nk == 1 else body, name=name, grid=(M // tm, N // tn, nk), in_specs=specs, out_specs=o_spec,
        out_shape=_sds((M, N), out_dtype), scratch_shapes=[] if nk == 1 else [pltpu.VMEM((tm, tn), F32)],
        compiler_params=_params(("parallel", "parallel", "arbitrary")),
    )(*ins)


FFN_TILE = 256


def _ffn_interleave(w):
    D, F2 = w.shape
    nj = F2 // (2 * FFN_TILE)

    def body(a_ref, b_ref, o_ref):
        o_ref[:, :FFN_TILE] = a_ref[...]
        o_ref[:, FFN_TILE:] = b_ref[...]

    return _pallas(
        body, name="ffn_interleave", grid=(nj,),
        in_specs=[pl.BlockSpec((D, FFN_TILE), lambda j: (0, j)), pl.BlockSpec((D, FFN_TILE), lambda j: (0, j + nj))],
        out_specs=pl.BlockSpec((D, 2 * FFN_TILE), lambda j: (0, j)), out_shape=_sds((D, F2), w.dtype),
        compiler_params=_params(("parallel",)),
    )(w, w)


def _ffn_deinterleave(w):
    D, F2 = w.shape
    nj = F2 // (2 * FFN_TILE)

    def body(w_ref, o_ref):
        o_ref[0] = w_ref[:, :FFN_TILE]
        o_ref[1] = w_ref[:, FFN_TILE:]

    return _pallas(
        body, name="ffn_deinterleave", grid=(nj,), in_specs=[pl.BlockSpec((D, 2 * FFN_TILE), lambda j: (0, j))],
        out_specs=pl.BlockSpec((2, D, FFN_TILE), lambda j: (0, 0, j)), out_shape=_sds((2, D, F2 // 2), w.dtype),
        compiler_params=_params(("parallel",)),
    )(w)


def _ffn_specs(M, D, F):
    tm = _tile(M, 1024, 128)
    x_spec = pl.BlockSpec((tm, D), lambda i, j: (i, 0))
    wi_spec = pl.BlockSpec((D, 2 * FFN_TILE), lambda i, j: (0, j))
    wo_spec = pl.BlockSpec((FFN_TILE, D), lambda i, j: (j, 0))
    return tm, F // FFN_TILE, x_spec, wi_spec, wo_spec


def _ffn_fwd(name, xn, w_in, w_out):
    M, D = xn.shape
    tm, nj, x_spec, wi_spec, wo_spec = _ffn_specs(M, D, w_out.shape[0])

    def body(x_ref, wi_ref, wo_ref, y_ref):
        j = pl.program_id(1)
        z = _dot(x_ref[...], wi_ref[...])
        hid = (jax.nn.silu(z[:, :FFN_TILE]) * z[:, FFN_TILE:]).astype(BF16)
        part = _dot(hid, wo_ref[...])

        @pl.when(j == 0)
        def _():
            y_ref[...] = part

        @pl.when(j > 0)
        def _():
            y_ref[...] += part

    return _pallas(
        body, name=name, grid=(M // tm, nj), in_specs=[x_spec, wi_spec, wo_spec], out_specs=x_spec,
        out_shape=_sds((M, D), F32), compiler_params=_params(("parallel", "arbitrary")),
    )(xn, w_in, w_out)


def _ffn_mid_bwd(name, xn, dy, w_in, w_out):
    M, D = xn.shape
    F = w_out.shape[0]
    tm, nj, x_spec, wi_spec, wo_spec = _ffn_specs(M, D, F)

    def body(x_ref, dy_ref, wi_ref, wo_ref, h_ref, dz_ref):
        z = _dot(x_ref[...], wi_ref[...])
        a, b = z[:, :FFN_TILE], z[:, FFN_TILE:]
        dh = _dot_nt(dy_ref[...], wo_ref[...])
        sig = jax.nn.sigmoid(a)
        act = a * sig
        h_ref[...] = (act * b).astype(BF16)
        dz_ref[:, :FFN_TILE] = (dh * b * (sig * (1.0 + a * (1.0 - sig)))).astype(BF16)
        dz_ref[:, FFN_TILE:] = (dh * act).astype(BF16)

    return _pallas(
        body, name=name, grid=(M // tm, nj), in_specs=[x_spec, x_spec, wi_spec, wo_spec],
        out_specs=[pl.BlockSpec((tm, FFN_TILE), lambda i, j: (i, j)),
                   pl.BlockSpec((tm, 2 * FFN_TILE), lambda i, j: (i, j))],
        out_shape=[_sds((M, F), BF16), _sds((M, 2 * F), BF16)],
        compiler_params=_params(("parallel", "parallel")),
    )(xn, dy, w_in, w_out)


class _Rows:
    def __init__(self, name, f, B, L, n, tm, arrays, pieces, samp, samp_pieces, glob, out_arrays, out_pieces):
        self.name, self.f, self.B, self.L, self.n, self.tm = name, f, B, L, n, tm
        self.arrays, self.pieces, self.samp, self.samp_pieces, self.glob = arrays, pieces, samp, samp_pieces, glob
        self.out_arrays, self.out_pieces = out_arrays, out_pieces
        self.nx = n // tm
        self.grid = (B, L // tm)

    def _row_spec(self, C, batched=True):
        tm = self.tm
        if batched:
            return pl.BlockSpec((None, tm, C), lambda b, i: (b, i, 0))
        return pl.BlockSpec((tm, C), lambda b, i: (i, 0))

    def _in_specs(self, unread=()):
        nx, tm = self.nx, self.tm
        specs = [self._row_spec(a.shape[-1], a.ndim == 3) for a in self.arrays]
        for ai in unread:
            specs[ai] = pl.BlockSpec((None, tm, self.arrays[ai].shape[-1]), lambda b, i: (0, 0, 0))
        specs += [pl.BlockSpec((None, None) + s.shape[2:], lambda b, i: (b, i // nx, 0, 0)) for s in self.samp]
        specs += [pl.BlockSpec(g.shape, lambda b, i, nd=g.ndim: (0,) * nd) for g in self.glob]
        return specs

    def _load(self, a_refs, s_refs, g_refs):
        args = [a_refs[ai][:, cs:cs + cw].astype(F32) for ai, cs, cw in self.pieces]
        args += [s_refs[si][r:r + 1, :].astype(F32) for si, r in self.samp_pieces]
        args += [g[...].astype(F32) for g in g_refs]
        return args

    def fwd(self):
        na, ns, ng = len(self.arrays), len(self.samp), len(self.glob)

        def body(*refs):
            a_refs, s_refs, g_refs = refs[:na], refs[na:na + ns], refs[na + ns:na + ns + ng]
            o_refs = refs[na + ns + ng:]
            outs = self.f(*self._load(a_refs, s_refs, g_refs))
            for (oi, cs, cw), o in zip(self.out_pieces, outs):
                o_refs[oi][:, cs:cs + cw] = o.astype(o_refs[oi].dtype)

        return _pallas(
            body, name=self.name + "_fwd", grid=self.grid, in_specs=self._in_specs(),
            out_specs=[self._row_spec(C) for C, _ in self.out_arrays],
            out_shape=[_sds((self.B, self.L, C), d) for C, d in self.out_arrays],
            compiler_params=_params(("parallel", "parallel")),
        )(*self.arrays, *self.samp, *self.glob)

    def bwd(self, cts, grad_arrays, grad_glob=(), add=None, samp=None, unread=()):
        samp = self.samp if samp is None else samp
        na, ns, ng, nc = len(self.arrays), len(self.samp), len(self.glob), len(cts)
        ct_off = [c[1] if isinstance(c, tuple) else 0 for c in cts]
        cts = [c[0] if isinstance(c, tuple) else c for c in cts]
        add = add or {}
        add_keys = list(add)
        g_idx = list(grad_arrays)
        nx, B = self.nx, self.B
        n_in = na + ns + ng + nc + len(add_keys)
        n_pieces, n_sp = len(self.pieces), len(self.samp_pieces)

        def body(*refs):
            a_refs, s_refs, g_refs = refs[:na], refs[na:na + ns], refs[na + ns:na + ns + ng]
            c_refs = refs[na + ns + ng:na + ns + ng + nc]
            add_refs = refs[na + ns + ng + nc:n_in]
            d_refs = refs[n_in:n_in + len(g_idx)]
            ds_refs = refs[n_in + len(g_idx):n_in + len(g_idx) + n_sp]
            dg_refs = refs[n_in + len(g_idx) + n_sp:]
            b, i = pl.program_id(0), pl.program_id(1)
            args = self._load(a_refs, s_refs, g_refs)
            _, vjp = jax.vjp(lambda *xs: tuple(self.f(*xs)), *args)
            grads = vjp(tuple(c_refs[oi][:, ct_off[oi] + cs:ct_off[oi] + cs + cw].astype(F32)
                              for oi, cs, cw in self.out_pieces))
            for k, ai in enumerate(g_idx):
                covered = sum(cw for pa, _, cw in self.pieces if pa == ai)
                if covered < self.arrays[ai].shape[-1]:
                    d_refs[k][...] = jnp.zeros_like(d_refs[k])
                for (pa, cs, cw), gr in zip(self.pieces, grads[:n_pieces]):
                    if pa == ai:
                        if ai in add:
                            gr = gr + add_refs[add_keys.index(ai)][:, cs:cs + cw]
                        d_refs[k][:, cs:cs + cw] = gr.astype(d_refs[k].dtype)

            @pl.when((i == 0) | (i == nx))
            def _():
                for r in ds_refs:
                    r[...] = jnp.zeros_like(r)

            for r, gr in zip(ds_refs, grads[n_pieces:n_pieces + n_sp]):
                r[...] += gr

            @pl.when((b == 0) & (i == 0))
            def _():
                for r in dg_refs:
                    r[...] = jnp.zeros_like(r)

            for r, gi in zip(dg_refs, grad_glob):
                r[...] += grads[n_pieces + n_sp + gi]

        in_specs = self._in_specs(unread) + [self._row_spec(c.shape[-1]) for c in cts]
        in_specs += [self._row_spec(add[k].shape[-1]) for k in add_keys]
        out_specs = [self._row_spec(self.arrays[ai].shape[-1]) for ai in g_idx]
        out_shape = [_sds(self.arrays[ai].shape, grad_arrays[ai]) for ai in g_idx]
        for si, _ in self.samp_pieces:
            C = self.samp[si].shape[-1]
            out_specs.append(pl.BlockSpec((None, None, 1, C), lambda b, i: (b, i // nx, 0, 0)))
            out_shape.append(_sds((B, 2, 1, C), F32))
        for gi in grad_glob:
            g = self.glob[gi]
            out_specs.append(pl.BlockSpec(g.shape, lambda b, i, nd=g.ndim: (0,) * nd))
            out_shape.append(_sds(g.shape, F32))
        outs = _pallas(
            body, name=self.name + "_bwd", grid=self.grid, in_specs=in_specs, out_specs=out_specs,
            out_shape=out_shape, compiler_params=_params(("arbitrary", "arbitrary")),
        )(*self.arrays, *samp, *self.glob, *cts, *[add[k] for k in add_keys])
        ng_ = len(g_idx)
        return outs[:ng_], outs[ng_:ng_ + n_sp], outs[ng_ + n_sp:]


def _rms(x, g):
    return x * lax.rsqrt(jnp.mean(x * x, axis=-1, keepdims=True) + EPS) * g


@jax.custom_vjp
def _swap_halves(x):
    w = x.shape[-1]
    lane = lax.broadcasted_iota(jnp.int32, x.shape, x.ndim - 1)
    up = pltpu.roll(x, w - ROPE_DIM // 2, x.ndim - 1)
    down = pltpu.roll(x, ROPE_DIM // 2, x.ndim - 1)
    return jnp.where(lane % ROPE_DIM < ROPE_DIM // 2, up, down)


_swap_halves.defvjp(lambda x: (_swap_halves(x), None), lambda _, ct: (_swap_halves(ct),))


@jax.custom_vjp
def _bdot(a, b):
    return jnp.dot(a.astype(BF16), b.astype(BF16), preferred_element_type=F32)


_bdot.defvjp(lambda a, b: (_bdot(a, b), (a, b)),
             lambda res, ct: (_dot_nt(ct.astype(BF16), res[1].astype(BF16)), _dot_tn(res[0].astype(BF16), ct.astype(BF16))))


def _rope(x, cos2, sin2):
    return x * cos2 + _swap_halves(x) * sin2


def _f_pre(s, shift, scale, g):
    return (_rms(s, g) * (1.0 + scale) + shift,)


def _f_res(s, y, gate):
    return (s + gate * y,)


def _f_res_pre(s, y, gate, shift, scale, g):
    s1 = s + gate * y
    return s1, _rms(s1, g) * (1.0 + scale) + shift


def _f_ab_pre(*args):
    kv_lat, q_lat, kpe = args[0:3]
    us, vs = args[3:7], args[7:11]
    cos2, sin2 = args[11:13]
    kv_norm, q_norm = args[13:15]
    vns, wss, bss = args[15:19], args[19:23], args[23:27]
    outs = [_rms(kv_lat, kv_norm), _rms(q_lat, q_norm), _rope(kpe, cos2, sin2)]
    for u, v, vn, ws, bs in zip(us, vs, vns, wss, bss):
        vg = _rms(jax.nn.gelu(v), vn)
        mixed = _bdot(ws, vg) + bs
        outs.append(jax.nn.gelu(u) * mixed)
    return tuple(outs)


def _f_q_rope(qn, qr, cos2, sin2):
    return qn, _rope(qr, cos2, sin2)


def _f_cd_pre(rk, rv, sk, sv, rq, rg, sq, c256, s256, c128, s128, c512, s512):
    return (_rope(rq, c256, s256), _rope(rk * RET_K_SCALE, c256, s256), rv,
            _rope(sq, c512, s512), _rope(sk, c128, s128), sv, rg)


def _f_cd_merge(*args):
    ys, rgs, rns = args[0:4], args[4:8], args[8:12]
    return tuple(_rms(y, rn) * jax.nn.silu(rg) for y, rg, rn in zip(ys, rgs, rns))


def _dot_nt(a, b):
    return lax.dot_general(a, b, (((1,), (1,)), ((), ())), preferred_element_type=F32)


def _dot_tn(a, b):
    return lax.dot_general(a, b, (((0,), (0,)), ((), ())), preferred_element_type=F32)


def _dot(a, b):
    return jnp.dot(a, b, preferred_element_type=F32)


def _tile_spec(tq, C):
    return pl.BlockSpec((None, tq, C), lambda b, i: (b, i, 0))


def _full_spec(L, C):
    return pl.BlockSpec((None, L, C), lambda b, i: (b, 0, 0))


def _mla_half(h):
    lane = lax.broadcasted_iota(jnp.int32, (1, 128), 1)
    return (lane < 64) if h % 2 == 0 else (lane >= 64)


def _mla_query(q_ref, h):
    pair = q_ref[:, 512 + 128 * (h // 2):640 + 128 * (h // 2)]
    return jnp.concatenate([q_ref[:, 128 * h:128 * h + 128], jnp.where(_mla_half(h), pair, jnp.zeros_like(pair))], axis=1)


def _mla_fill_keys(kcat, kv_ref, kpe_ref):
    kp = kpe_ref[...]
    for h in range(4):
        kcat[h, :, 0:128] = kv_ref[:, 256 * h:256 * h + 128]
        kcat[h, :, 128:256] = jnp.where(_mla_half(h), kp, jnp.zeros_like(kp))


def _mla_fwd(q, kv, kpe, n, tq):
    B, L, _ = q.shape

    def body(q_ref, kv_ref, kpe_ref, o_ref, lse_ref, kcat):
        i = pl.program_id(1)

        @pl.when(i == 0)
        def _():
            _mla_fill_keys(kcat, kv_ref, kpe_ref)

        def tile(keys):
            for h in range(4):
                s = _dot_nt(_mla_query(q_ref, h), kcat[h, keys, :]) * MLA_SCALE
                m = jnp.max(s, axis=1, keepdims=True)
                e = jnp.exp(s - m)
                l = jnp.sum(e, axis=1, keepdims=True)
                p = (e * (1.0 / l)).astype(BF16)
                o_ref[:, 128 * h:128 * h + 128] = _dot(p, kv_ref[keys, 256 * h + 128:256 * h + 256]).astype(o_ref.dtype)
                lse_ref[:, h:h + 1] = m + jnp.log(l)

        pl.when(i < n // tq)(functools.partial(tile, slice(0, L)))
        pl.when(i >= n // tq)(functools.partial(tile, slice(n, L)))

    return _pallas(
        body, name="mla_fwd", grid=(B, L // tq),
        in_specs=[_tile_spec(tq, 768), _full_spec(L, 1024), _full_spec(L, 128)],
        out_specs=[_tile_spec(tq, 512), _tile_spec(tq, 4)],
        out_shape=[_sds((B, L, 512), BF16), _sds((B, L, 4), F32)],
        scratch_shapes=[pltpu.VMEM((4, L, 256), BF16)],
        compiler_params=_params(("parallel", "arbitrary")),
    )(q, kv, kpe)


def _mla_bwd(q, kv, kpe, lse, do, n, tq):
    B, L, _ = q.shape

    def body(q_ref, kv_ref, kpe_ref, lse_ref, do_ref, dq_ref, dkv_ref, dkpe_ref, kcat):
        i = pl.program_id(1)

        @pl.when(i == 0)
        def _():
            dkv_ref[...] = jnp.zeros_like(dkv_ref)
            dkpe_ref[...] = jnp.zeros_like(dkpe_ref)
            _mla_fill_keys(kcat, kv_ref, kpe_ref)

        def tile(keys):
            rope_pair = None
            for h in range(4):
                qc, kc = _mla_query(q_ref, h), kcat[h, keys, :]
                v = kv_ref[keys, 256 * h + 128:256 * h + 256]
                p = jnp.exp(_dot_nt(qc, kc) * MLA_SCALE - lse_ref[:, h:h + 1])
                doh = do_ref[:, 128 * h:128 * h + 128].astype(BF16)
                dp = _dot_nt(doh, v)
                delta = jnp.sum(p * dp, axis=1, keepdims=True)
                ds = (p * (dp - delta) * MLA_SCALE).astype(BF16)
                dqc = _dot(ds, kc)
                dq_ref[:, 128 * h:128 * h + 128] = dqc[:, 0:128]
                if h % 2 == 0:
                    rope_pair = dqc[:, 128:256]
                else:
                    dq_ref[:, 512 + 128 * (h // 2):640 + 128 * (h // 2)] = jnp.where(_mla_half(h), dqc[:, 128:256], rope_pair)
                dkc = _dot_tn(ds, qc)
                dkv_ref[keys, 256 * h:256 * h + 128] += dkc[:, 0:128]
                dkpe_ref[keys, :] += dkc[:, 128:256]
                dkv_ref[keys, 256 * h + 128:256 * h + 256] += _dot_tn(p.astype(BF16), doh)

        pl.when(i < n // tq)(functools.partial(tile, slice(0, L)))
        pl.when(i >= n // tq)(functools.partial(tile, slice(n, L)))

    return _pallas(
        body, name="mla_bwd", grid=(B, L // tq),
        in_specs=[_tile_spec(tq, 768), _full_spec(L, 1024), _full_spec(L, 128), _tile_spec(tq, 4),
                  pl.BlockSpec((None, tq, 512), lambda b, i: (b, i, 0))],
        out_specs=[_tile_spec(tq, 768), _full_spec(L, 1024), _full_spec(L, 128)],
        out_shape=[_sds((B, L, 768), F32), _sds((B, L, 1024), F32), _sds((B, L, 128), F32)],
        scratch_shapes=[pltpu.VMEM((4, L, 256), BF16)],
        compiler_params=_params(("parallel", "arbitrary")),
    )(q, kv, kpe, lse, do)


def _swa_window(i, tq, n):
    W = min(tq + 2 * SWA_WINDOW, n)
    lo = jnp.clip(i * tq - SWA_WINDOW, 0, n - W)
    return pl.multiple_of(lo, 128), W


def _swa_mask(i, lo, tq, W, n):
    qpos = i * tq + lax.broadcasted_iota(jnp.int32, (tq, 1), 0)
    kpos = lo + lax.broadcasted_iota(jnp.int32, (1, W), 1)
    return (jnp.abs(qpos - kpos) <= SWA_WINDOW) & (qpos < n)


def _swa_fill(rep, src_ref):
    for g in range(2):
        rep[:, 256 * g:256 * g + 256] = jnp.concatenate([src_ref[:, 64 * g:64 * g + 64]] * 4, axis=1)


def _swa_head_rows(a):
    lane = lax.broadcasted_iota(jnp.int32, (1, 256), 1)
    return jnp.concatenate([jnp.where(lane // 64 == h, a, jnp.zeros_like(a)) for h in range(4)], axis=0)


def _swa_fold_rows(a, tq):
    lane = lax.broadcasted_iota(jnp.int32, (1, 256), 1)
    out = jnp.where(lane // 64 == 0, a[0:tq], 0.0)
    for h in range(1, 4):
        out = jnp.where(lane // 64 == h, a[h * tq:(h + 1) * tq], out)
    return out


def _swa_scores(i, g, tq, n, L, q_ref, krep, sink_ref):
    lo, W = _swa_window(i, tq, n)
    mask = jnp.concatenate([_swa_mask(i, lo, tq, W, n)] * 4, axis=0)
    qs = _swa_head_rows(q_ref[:, 256 * g:256 * g + 256])
    kl, kc = krep[pl.ds(lo, W), 256 * g:256 * g + 256], krep[n:L, 256 * g:256 * g + 256]
    s1 = jnp.where(mask, _dot_nt(qs, kl) * SWA_SCALE, NEG_INF)
    s2 = _dot_nt(qs, kc) * SWA_SCALE
    sk = jnp.concatenate([jnp.broadcast_to(sink_ref[0:1, 4 * g + h:4 * g + h + 1], (tq, 1)) for h in range(4)], axis=0)
    return qs, kl, kc, s1, s2, sk, lo, W


def _swa_fwd(q, k, v, sink, n, tq):
    B, L, _ = q.shape

    def body(q_ref, k_ref, v_ref, sink_ref, o_ref, lse_ref, krep, vrep):
        i = pl.program_id(1)

        @pl.when(i == 0)
        def _():
            _swa_fill(krep, k_ref)
            _swa_fill(vrep, v_ref)

        for g in range(2):
            _, _, _, s1, s2, sk, lo, W = _swa_scores(i, g, tq, n, L, q_ref, krep, sink_ref)
            vl, vc = vrep[pl.ds(lo, W), 256 * g:256 * g + 256], vrep[n:L, 256 * g:256 * g + 256]
            m = jnp.maximum(jnp.maximum(jnp.max(s1, axis=1, keepdims=True), jnp.max(s2, axis=1, keepdims=True)), sk)
            e1, e2 = jnp.exp(s1 - m), jnp.exp(s2 - m)
            l = jnp.sum(e1, axis=1, keepdims=True) + jnp.sum(e2, axis=1, keepdims=True) + jnp.exp(sk - m)
            r = 1.0 / l
            o = _dot((e1 * r).astype(BF16), vl) + _dot((e2 * r).astype(BF16), vc)
            o_ref[:, 256 * g:256 * g + 256] = _swa_fold_rows(o, tq).astype(o_ref.dtype)
            lse = m + jnp.log(l)
            for h in range(4):
                lse_ref[:, 4 * g + h:4 * g + h + 1] = lse[h * tq:(h + 1) * tq]

    return _pallas(
        body, name="swa_fwd", grid=(B, L // tq),
        in_specs=[_tile_spec(tq, 512), _full_spec(L, 128), _full_spec(L, 128), pl.BlockSpec((1, 8), lambda b, i: (0, 0))],
        out_specs=[_tile_spec(tq, 512), _tile_spec(tq, 8)],
        out_shape=[_sds((B, L, 512), BF16), _sds((B, L, 8), F32)],
        scratch_shapes=[pltpu.VMEM((L, 512), BF16), pltpu.VMEM((L, 512), BF16)],
        compiler_params=_params(("parallel", "arbitrary")),
    )(q, k, v, sink)


def _swa_bwd(q, k, v, sink, lse, do, n, tq):
    B, L, _ = q.shape

    def body(q_ref, k_ref, v_ref, sink_ref, lse_ref, do_ref, dq_ref, dk_ref, dv_ref, dsink_ref):
        i = pl.program_id(1)

        @pl.when(i == 0)
        def _():
            dk_ref[...] = jnp.zeros_like(dk_ref)
            dv_ref[...] = jnp.zeros_like(dv_ref)
            dsink_ref[...] = jnp.zeros_like(dsink_ref)

        lo, W = _swa_window(i, tq, n)
        mask = _swa_mask(i, lo, tq, W, n)
        kl, vl = k_ref[pl.ds(lo, W), :], v_ref[pl.ds(lo, W), :]
        kc, vc = k_ref[n:L, :], v_ref[n:L, :]
        for h in range(8):
            g = h // 4
            cols = slice(64 * g, 64 * g + 64)
            qh = q_ref[:, 64 * h:64 * h + 64]
            lse_h = lse_ref[:, h:h + 1]
            s1 = jnp.where(mask, _dot_nt(qh, kl[:, cols]) * SWA_SCALE, NEG_INF)
            s2 = _dot_nt(qh, kc[:, cols]) * SWA_SCALE
            p1, p2 = jnp.exp(s1 - lse_h), jnp.exp(s2 - lse_h)
            ps = jnp.exp(sink_ref[0:1, h:h + 1] - lse_h)
            doh = do_ref[:, 64 * h:64 * h + 64].astype(BF16)
            dp1, dp2 = _dot_nt(doh, vl[:, cols]), _dot_nt(doh, vc[:, cols])
            delta = jnp.sum(p1 * dp1, axis=1, keepdims=True) + jnp.sum(p2 * dp2, axis=1, keepdims=True)
            ds1 = (p1 * (dp1 - delta) * SWA_SCALE).astype(BF16)
            ds2 = (p2 * (dp2 - delta) * SWA_SCALE).astype(BF16)
            dq_ref[:, 64 * h:64 * h + 64] = _dot(ds1, kl[:, cols]) + _dot(ds2, kc[:, cols])
            dk_ref[pl.ds(lo, W), cols] += _dot_tn(ds1, qh)
            dk_ref[n:L, cols] += _dot_tn(ds2, qh)
            dv_ref[pl.ds(lo, W), cols] += _dot_tn(p1.astype(BF16), doh)
            dv_ref[n:L, cols] += _dot_tn(p2.astype(BF16), doh)
            dsink_ref[0:1, h:h + 1] += jnp.sum(-ps * delta, axis=0, keepdims=True)

    return _pallas(
        body, name="swa_bwd", grid=(B, L // tq),
        in_specs=[_tile_spec(tq, 512), _full_spec(L, 128), _full_spec(L, 128), pl.BlockSpec((1, 8), lambda b, i: (0, 0)),
                  _tile_spec(tq, 8), pl.BlockSpec((None, tq, 512), lambda b, i: (b, i, 1))],
        out_specs=[_tile_spec(tq, 512), _full_spec(L, 128), _full_spec(L, 128),
                   pl.BlockSpec((None, 1, 8), lambda b, i: (b, 0, 0))],
        out_shape=[_sds((B, L, 512), F32), _sds((B, L, 128), F32), _sds((B, L, 128), F32), _sds((B, 1, 8), F32)],
        compiler_params=_params(("parallel", "arbitrary")),
    )(q, k, v, sink, lse, do)


def _ret_decay(i, tq, L, n, ctx_tile):
    qi = i * tq + lax.broadcasted_iota(jnp.int32, (tq, 1), 0)
    kc = lax.broadcasted_iota(jnp.int32, (1, L - n), 1)
    d_hb = (n + kc - qi).astype(F32)
    if ctx_tile:
        return None, ((qi - L) - (kc - (L - n))).astype(F32), d_hb
    d_x = (qi - lax.broadcasted_iota(jnp.int32, (1, n), 1)).astype(F32)
    return d_x, (qi - (kc - (L - n))).astype(F32), d_hb


def _ret_head(a, h):
    lane = lax.broadcasted_iota(jnp.int32, (1, a.shape[1]), 1)
    return jnp.where(lane // 64 == h, a, jnp.zeros_like(a))


def _ret_mask(dist, lg):
    return jnp.where(dist >= 0.0, jnp.exp(lg * jnp.maximum(dist, 0.0)), 0.0)


def _ret_weights(n, L, d_x, d_hf, d_hb, lg_f, lg_b, with_grad):
    m_f, m_b = _ret_mask(d_hf, lg_f), _ret_mask(d_hb, lg_b)
    ctx = (slice(n, L), m_f + m_b)
    if with_grad:
        ctx += (m_f * jnp.maximum(d_hf, 0.0), m_b * jnp.maximum(d_hb, 0.0))
    if d_x is None:
        return [ctx]
    e_x = jnp.exp(jnp.where(d_x >= 0.0, lg_f, -lg_b) * d_x)
    lat = (slice(0, n), jnp.where(d_x == 0.0, 2.0 * e_x, e_x))
    if with_grad:
        lat += (e_x * jnp.maximum(d_x, 0.0), e_x * jnp.maximum(-d_x, 0.0))
    return [lat, ctx]


def _ret_fwd(q, k, v, lg, n, tq):
    B, L, _ = q.shape

    def body(q_ref, k_ref, v_ref, lg_ref, y_ref):
        i = pl.program_id(1)

        def tile(ctx_tile):
            dist = _ret_decay(i, tq, L, n, ctx_tile)
            for h in range(4):
                qh = _ret_head(q_ref[...], h)
                y = None
                for rows, dec_r in _ret_weights(n, L, *dist, lg_ref[0:1, h:h + 1], lg_ref[1:2, h:h + 1], False):
                    a = _dot_nt(qh, k_ref[rows, :]) * dec_r
                    part = _dot(a.astype(BF16), v_ref[rows, 128 * h:128 * h + 128])
                    y = part if y is None else y + part
                y_ref[:, 128 * h:128 * h + 128] = y

        pl.when(i < n // tq)(functools.partial(tile, False))
        pl.when(i >= n // tq)(functools.partial(tile, True))

    return _pallas(
        body, name="ret_fwd", grid=(B, L // tq),
        in_specs=[_tile_spec(tq, 256), _full_spec(L, 256), _full_spec(L, 512), pl.BlockSpec((2, 4), lambda b, i: (0, 0))],
        out_specs=_tile_spec(tq, 512), out_shape=_sds((B, L, 512), F32),
        compiler_params=_params(("parallel", "arbitrary")),
    )(q, k, v, lg)


def _ret_bwd(q, k, v, lg, dy, n, tq):
    B, L, _ = q.shape

    def body(q_ref, k_ref, v_ref, lg_ref, dy_ref, dq_ref, dk_ref, dv_ref, dlg_ref):
        i = pl.program_id(1)

        @pl.when(i == 0)
        def _():
            dk_ref[...] = jnp.zeros_like(dk_ref)
            dv_ref[...] = jnp.zeros_like(dv_ref)
            dlg_ref[...] = jnp.zeros_like(dlg_ref)

        def total(a):
            return jnp.sum(jnp.sum(a, axis=1, keepdims=True), axis=0, keepdims=True)

        def tile(ctx_tile):
            dist = _ret_decay(i, tq, L, n, ctx_tile)
            dq = None
            for h in range(4):
                vc = slice(128 * h, 128 * h + 128)
                qh = _ret_head(q_ref[...], h)
                dyh = dy_ref[:, vc].astype(BF16)
                dqh = None
                for rows, dec_r, wf_r, wb_r in _ret_weights(n, L, *dist, lg_ref[0:1, h:h + 1], lg_ref[1:2, h:h + 1], True):
                    k_all, vh = k_ref[rows, :], v_ref[rows, vc]
                    s = _dot_nt(qh, k_all)
                    gr = _dot_nt(dyh, vh)
                    ds = (gr * dec_r).astype(BF16)
                    part = _dot(ds, k_all)
                    dqh = part if dqh is None else dqh + part
                    dk_ref[rows, :] += _dot_tn(ds, qh)
                    dv_ref[rows, vc] += _dot_tn((s * dec_r).astype(BF16), dyh)
                    gs = gr * s
                    dlg_ref[0:1, h:h + 1] += total(gs * wf_r)
                    dlg_ref[1:2, h:h + 1] += total(gs * wb_r)
                dqh = _ret_head(dqh, h)
                dq = dqh if dq is None else dq + dqh
            dq_ref[...] = dq

        pl.when(i < n // tq)(functools.partial(tile, False))
        pl.when(i >= n // tq)(functools.partial(tile, True))

    return _pallas(
        body, name="ret_bwd", grid=(B, L // tq),
        in_specs=[_tile_spec(tq, 256), _full_spec(L, 256), _full_spec(L, 512), pl.BlockSpec((2, 4), lambda b, i: (0, 0)),
                  _tile_spec(tq, 512)],
        out_specs=[_tile_spec(tq, 256), _full_spec(L, 256), _full_spec(L, 512),
                   pl.BlockSpec((None, 2, 4), lambda b, i: (b, 0, 0))],
        out_shape=[_sds((B, L, 256), F32), _sds((B, L, 256), F32), _sds((B, L, 512), F32), _sds((B, 2, 4), F32)],
        compiler_params=_params(("parallel", "arbitrary")),
    )(q, k, v, lg, dy)


def _loss_head(s, target, g, n, tm):
    B, L, D = s.shape
    nx = n // tm

    def body(s_ref, t_ref, g_ref, ds_ref, dg_ref, loss_ref):
        b, i = pl.program_id(0), pl.program_id(1)

        @pl.when((b == 0) & (i == 0))
        def _():
            dg_ref[...] = jnp.zeros_like(dg_ref)
            loss_ref[...] = jnp.zeros_like(loss_ref)

        @pl.when(i < nx)
        def _():
            y, vjp = jax.vjp(_rms, s_ref[...], g_ref[...])
            err = y - t_ref[...]
            d_s, d_g = vjp(err * (1.0 / D))
            ds_ref[...] = d_s
            dg_ref[...] += d_g
            part = jnp.sum(jnp.sum(err * err, axis=1, keepdims=True), axis=0, keepdims=True) * (0.5 / D)
            loss_ref[...] += jnp.broadcast_to(part, loss_ref.shape)

        @pl.when(i >= nx)
        def _():
            ds_ref[...] = jnp.zeros_like(ds_ref)

    return _pallas(
        body, name="loss_head", grid=(B, L // tm),
        in_specs=[pl.BlockSpec((None, tm, D), lambda b, i: (b, i, 0)),
                  pl.BlockSpec((None, tm, D), lambda b, i: (b, jnp.minimum(i, nx - 1), 0)),
                  pl.BlockSpec((1, D), lambda b, i: (0, 0))],
        out_specs=[pl.BlockSpec((None, tm, D), lambda b, i: (b, i, 0)), pl.BlockSpec((1, D), lambda b, i: (0, 0)),
                   pl.BlockSpec((1, 128), lambda b, i: (0, 0))],
        out_shape=[_sds((B, L, D), F32), _sds((1, D), F32), _sds((1, 128), F32)],
        compiler_params=_params(("arbitrary", "arbitrary")),
    )(s, target, g)


def _ada_fwd(c_all, ada_w, ada_b):
    NL, D, Ns = ada_w.shape
    R = c_all.shape[0]

    def body(c_ref, w_ref, b_ref, o_ref):
        cond = jax.nn.silu(c_ref[...]).astype(BF16)
        o_ref[...] = _dot(cond, w_ref[...].astype(BF16)) + b_ref[...]

    return _pallas(
        body, name="ada_fwd", grid=(NL,),
        in_specs=[pl.BlockSpec((R, D), lambda l: (0, 0)), pl.BlockSpec((None, D, Ns), lambda l: (l, 0, 0)),
                  pl.BlockSpec((None, 1, Ns), lambda l: (l, 0, 0))],
        out_specs=pl.BlockSpec((None, R, Ns), lambda l: (l, 0, 0)), out_shape=_sds((NL, R, Ns), F32),
        compiler_params=_params(("parallel",)),
    )(c_all, ada_w, ada_b)


def _ada_bwd(c_all, ada_w, dmods):
    NL, D, Ns = ada_w.shape
    R = c_all.shape[0]

    def body(c_ref, w_ref, dm_ref, dw_ref, dc_ref):
        cond = jax.nn.silu(c_ref[...]).astype(BF16)
        dm = dm_ref[...].astype(BF16)
        dw_ref[...] = _dot_tn(cond, dm)
        dc_ref[...] = _dot_nt(dm, w_ref[...].astype(BF16))

    return _pallas(
        body, name="ada_bwd", grid=(NL,),
        in_specs=[pl.BlockSpec((R, D), lambda l: (0, 0)), pl.BlockSpec((None, D, Ns), lambda l: (l, 0, 0)),
                  pl.BlockSpec((None, R, Ns), lambda l: (l, 0, 0))],
        out_specs=[pl.BlockSpec((None, D, Ns), lambda l: (l, 0, 0)), pl.BlockSpec((None, R, D), lambda l: (l, 0, 0))],
        out_shape=[_sds((NL, D, Ns), F32), _sds((NL, R, D), F32)],
        compiler_params=_params(("parallel",)),
    )(c_all, ada_w, dmods)


def _my_index():
    return 4 * lax.axis_index("x") + 2 * lax.axis_index("y") + lax.axis_index("c")


def _peer(k):
    x, y, c = lax.axis_index("x"), lax.axis_index("y"), lax.axis_index("c")
    kx, ky, kc = (k >> 2) & 1, (k >> 1) & 1, k & 1
    px, py, pc = (x + kx) % 2, (y + ky) % 2, (c + kc) % 2
    return (px, py, pc), 4 * px + 2 * py + pc


def _all_gather(name, shards):
    na = len(shards)
    hbm = pl.BlockSpec(memory_space=pl.ANY)

    def body(*refs):
        in_refs, out_refs = refs[:na], refs[na:2 * na]
        send_sems, recv_sems, local_sems = refs[2 * na:]
        me = _my_index()
        sib_id, sib = _peer(1)
        chips = [_peer(k) for k in (4, 2, 6)]
        sib_chips = [4 * px + 2 * py + (1 - pc) for (px, py, pc), _ in chips]

        def copy(a, k, slot, to, src=None):
            dst = out_refs[a].at[slot]
            return pltpu.make_async_remote_copy(
                src_ref=dst if src is None else src, dst_ref=dst, send_sem=send_sems.at[a, k],
                recv_sem=recv_sems.at[a, k], device_id=to, device_id_type=MESH)

        first, passed, mine = [], [], []
        for a in range(na):
            cp = pltpu.make_async_copy(in_refs[a], out_refs[a].at[me], local_sems.at[a])
            cp.start()
            mine.append(cp)
            first.append(copy(a, 0, me, sib_id, src=in_refs[a]))
            first += [copy(a, 1 + j, me, pid, src=in_refs[a]) for j, (pid, _) in enumerate(chips)]
        for cp in first:
            cp.start()
        for a in range(na):
            for j, (pid, pidx) in enumerate(chips):
                copy(a, 1 + j, pidx, pid).wait_recv()
                fwd = copy(a, 4 + j, pidx, sib_id)
                fwd.start()
                passed.append(fwd)
        for a in range(na):
            copy(a, 0, sib, sib_id).wait_recv()
            for j in range(3):
                copy(a, 4 + j, sib_chips[j], sib_id).wait_recv()
        for cp in first + passed:
            cp.wait_send()
        for cp in mine:
            cp.wait()

    return _pallas(
        body, name=name, in_specs=[hbm] * na, out_specs=[hbm] * na,
        out_shape=[_sds((N_DEV,) + s.shape, s.dtype) for s in shards],
        scratch_shapes=[pltpu.SemaphoreType.DMA((na, 7)), pltpu.SemaphoreType.DMA((na, 7)),
                        pltpu.SemaphoreType.DMA((na,))],
    )(*shards)


def _all_to_all(name, parts):
    na = len(parts)
    hbm = pl.BlockSpec(memory_space=pl.ANY)

    def body(*refs):
        in_refs, out_refs = refs[:na], refs[na:2 * na]
        send_sems, recv_sems, local_sems = refs[2 * na:]
        me = _my_index()
        copies = []
        for a in range(na):
            cp = pltpu.make_async_copy(in_refs[a].at[me], out_refs[a].at[me], local_sems.at[a])
            cp.start()
            copies.append(cp)
            for k in range(1, N_DEV):
                pid, pidx = _peer(k)
                cp = pltpu.make_async_remote_copy(
                    src_ref=in_refs[a].at[pidx], dst_ref=out_refs[a].at[me], send_sem=send_sems.at[a, k - 1],
                    recv_sem=recv_sems.at[a, k - 1], device_id=pid, device_id_type=MESH)
                cp.start()
                copies.append(cp)
        for cp in copies:
            cp.wait()

    return _pallas(
        body, name=name, in_specs=[hbm] * na, out_specs=[hbm] * na,
        out_shape=[_sds(p.shape, p.dtype) for p in parts],
        scratch_shapes=[pltpu.SemaphoreType.DMA((na, 7)), pltpu.SemaphoreType.DMA((na, 7)),
                        pltpu.SemaphoreType.DMA((na,))],
    )(*parts)


_HBM = pl.BlockSpec(memory_space=pltpu.HBM)
_SEM = pl.BlockSpec(memory_space=pltpu.SEMAPHORE)
_DATAFLOW = pltpu.SideEffectType.DATAFLOW_SIDE_EFFECTING


def _exchange_start(name, srcs, slotted, after=()):
    na = len(srcs)
    lands = [lax.empty((N_DEV,) + (s.shape[1:] if slotted else s.shape), s.dtype) for s in srcs]

    def body(*refs):
        src_refs, land_refs = refs[:na], refs[na:2 * na]
        outs = refs[2 * na + len(after):]
        send_sems, recv_sems, token = outs[:na], outs[na:2 * na], outs[4 * na]
        me = _my_index()
        for a in range(na):
            for k in range(1, N_DEV):
                pid, pidx = _peer(k)
                pltpu.make_async_remote_copy(
                    src_ref=src_refs[a].at[pidx] if slotted else src_refs[a], dst_ref=land_refs[a].at[me],
                    send_sem=send_sems[a], recv_sem=recv_sems[a], device_id=pid, device_id_type=MESH).start()
        token[...] = jnp.zeros_like(token)

    ops = [pltpu.with_memory_space_constraint(a, pltpu.HBM) for a in list(srcs) + lands]
    outs = _pallas(
        body, name=name,
        out_shape=[pltpu.SemaphoreType.DMA(())] * (2 * na) + [pltpu.HBM(a.shape, a.dtype) for a in ops]
        + [_sds((8, 128), F32)],
        in_specs=[_HBM] * (2 * na) + [pl.BlockSpec(memory_space=pl.ANY)] * len(after),
        out_specs=[_SEM] * (2 * na) + [_HBM] * (2 * na) + [pl.BlockSpec(memory_space=pltpu.VMEM)],
        input_output_aliases={a: 2 * na + a for a in range(2 * na)},
        compiler_params=pltpu.CompilerParams(has_side_effects=_DATAFLOW),
    )(*ops, *after)
    return (na, outs[:2 * na], outs[2 * na:4 * na]), outs[4 * na]


def _exchange_wait(name, handle, after):
    na, sems, thru = handle

    def body(*refs):
        land_refs = refs[na:2 * na]
        send_sems, recv_sems = refs[2 * na:3 * na], refs[3 * na:4 * na]
        me_id = (lax.axis_index("x"), lax.axis_index("y"), lax.axis_index("c"))
        for a in range(na):
            seven = land_refs[a].at[pl.ds(0, N_DEV - 1)]
            drain = pltpu.make_async_remote_copy(src_ref=seven, dst_ref=seven, send_sem=send_sems[a],
                                                 recv_sem=recv_sems[a], device_id=me_id, device_id_type=MESH)
            drain.wait_send()
            drain.wait_recv()

    outs = _pallas(
        body, name=name, out_shape=[pltpu.HBM(a.shape, a.dtype) for a in thru],
        in_specs=[_HBM] * (2 * na) + [_SEM] * (2 * na) + [pl.BlockSpec(memory_space=pl.ANY)],
        out_specs=[_HBM] * (2 * na), input_output_aliases={a: a for a in range(2 * na)},
        compiler_params=pltpu.CompilerParams(has_side_effects=_DATAFLOW),
    )(*thru, *sems, after)
    return outs[na:]


def _with_own(landed, own, slot_axis=0):
    me = _my_index()
    slot = lax.broadcasted_iota(jnp.int32, landed.shape, slot_axis)
    return jnp.where(slot == me, own, landed)


def _rope_tables(n, L, width):
    t = jnp.arange(n)
    row = (t // GRID_W).astype(F32)
    col = (t % GRID_W).astype(F32)
    n_freq = ROPE_DIM // 4
    freqs = ROPE_THETA ** (-jnp.arange(n_freq, dtype=F32) / n_freq)
    ang = jnp.concatenate([row[:, None] * freqs, col[:, None] * freqs], axis=-1)
    cos, sin = jnp.cos(ang), jnp.sin(ang)
    cos2 = jnp.concatenate([cos, cos], axis=-1)
    sin2 = jnp.concatenate([-sin, sin], axis=-1)
    cos2 = jnp.concatenate([cos2, jnp.ones((L - n, ROPE_DIM), F32)], axis=0)
    sin2 = jnp.concatenate([sin2, jnp.zeros((L - n, ROPE_DIM), F32)], axis=0)
    reps = width // ROPE_DIM
    return jnp.tile(cos2, (1, reps)), jnp.tile(sin2, (1, reps))


def _ab_perm(w):
    return jnp.concatenate([w[:, 0:256], w[:, 320:1600], w[:, 256:320], w[:, 256:320]], axis=1)


def _ab_unperm(g):
    rope_key = (g[:, 1536:1600].astype(F32) + g[:, 1600:1664].astype(F32)).astype(g.dtype)
    return jnp.concatenate([g[:, 0:256], rope_key, g[:, 256:1536]], axis=1)


def _wq_perm(w):
    return jnp.concatenate([w[:, 192 * h:192 * h + 128] for h in range(4)]
                           + [w[:, 192 * h + 128:192 * h + 192] for h in range(4)], axis=1)


def _wq_unperm(g):
    return jnp.concatenate([g[:, sl] for h in range(4)
                            for sl in (slice(128 * h, 128 * h + 128), slice(512 + 64 * h, 576 + 64 * h))], axis=1)


def _flat(a):
    return a.reshape(-1, a.shape[-1])


def _layer_weights(full, p):
    NL, NE, NO = len(full["ffn_in"]), len(full["ab_in"]), len(full["cd_in"])
    groups = range(4)

    def each(f, mats):
        return [None if w is None else f(w) for w in mats]

    return dict(
        norm_mix=[p["norm_mix"][l][None] for l in range(NL)], norm_ffn=[p["norm_ffn"][l][None] for l in range(NL)],
        norm_final=p["norm_final"][None],
        ffn_in=each(_ffn_interleave, full["ffn_in"]), ffn_out=list(full["ffn_out"]),
        ab_in=each(_ab_perm, full["ab_in"]), ab_out=list(full["ab_out"]),
        wq=each(_wq_perm, full["mla_wq_b"]), wkv=list(full["mla_wkv_b"]),
        kv_norm=[p["mla_kv_norm"][j][None] for j in range(NE)], q_norm=[p["mla_q_norm"][j][None] for j in range(NE)],
        v_norm=[[p["cmlp_v_norm"][j][None, 128 * g:128 * g + 128] for g in groups] for j in range(NE)],
        ws=[[p["cmlp_ws"][j, g] for g in groups] for j in range(NE)],
        bs=[[p["cmlp_bs"][j, g][:, None] for g in groups] for j in range(NE)],
        cd_in=list(full["cd_in"]), cd_out=list(full["cd_out"]),
        lg=[jnp.stack([jax.nn.log_sigmoid(p["ret_decay_fwd"][j]), jax.nn.log_sigmoid(p["ret_decay_bwd"][j])])
            for j in range(NO)],
        sink=[p["swa_sink"][j][None] for j in range(NO)],
        ret_norm=[[p["ret_norm"][j][None, 128 * g:128 * g + 128] for g in groups] for j in range(NO)],
    )


def _local_step(x, ctx, target, mods, W, later_weights=None, early_grads=None):
    B, n, D = x.shape
    m = ctx.shape[1]
    L = n + m
    NL = mods.shape[0]
    tm = min(256, m)
    tq = min(256, m)
    cos512, sin512 = _rope_tables(n, L, 512)
    s = jnp.concatenate([x, ctx], axis=1)
    saved = []

    def rows(name, f, arrays, pieces, samp, samp_pieces, glob, out_arrays, out_pieces, tile=tm):
        return _Rows(name, f, B, L, n, tile, arrays, pieces, samp, samp_pieces, glob, out_arrays, out_pieces)

    def full(width, start=0):
        return [(0, start, width)]

    for l in range(NL):
        j = l // 2
        even = l % 2 == 0
        md = mods[l]
        if l == 1 and later_weights is not None:
            W = later_weights(1, s)
        r = {}
        if l == 0:
            r["pre1"] = rows("pre_mix0", _f_pre, [s], full(D), [md], [(0, 0), (0, 1)], [W["norm_mix"][0]], [(D, BF16)], full(D))
            (xn,) = r["pre1"].fwd()
        r["xn"] = xn
        if even:
            z = _mm(f"ab_in{l}", _flat(xn), W["ab_in"][j]).reshape(B, L, 1664)
            pieces = [(0, 0, 256), (0, 256, 256), (0, 1536, 128)]
            pieces += [(0, 512 + 128 * g, 128) for g in range(4)] + [(0, 1024 + 128 * g, 128) for g in range(4)]
            pieces += [(1, 0, 128), (2, 0, 128)]
            glob = [W["kv_norm"][j], W["q_norm"][j]] + W["v_norm"][j] + W["ws"][j] + W["bs"][j]
            abp = rows(f"ab_pre{l}", _f_ab_pre, [z, cos512, sin512], pieces, [], [], glob,
                       [(256, BF16), (256, BF16), (128, BF16), (512, BF16)],
                       [(0, 0, 256), (1, 0, 256), (2, 0, 128)] + [(3, 128 * g, 128) for g in range(4)], tile=128)
            kvn, qn, kpe, cm = abp.fwd()
            kv = _mm(f"wkv{l}", _flat(kvn), W["wkv"][j], out_dtype=BF16).reshape(B, L, 1024)
            q0 = _mm(f"wq{l}", _flat(qn), W["wq"][j]).reshape(B, L, 768)
            qrp = rows(f"q_rope{l}", _f_q_rope, [q0, cos512, sin512], [(0, 0, 512), (0, 512, 256), (1, 0, 256), (2, 0, 256)],
                       [], [], [], [(768, BF16)], [(0, 0, 512), (0, 512, 256)])
            (q,) = qrp.fwd()
            o, lse = _mla_fwd(q, kv, kpe, n, tq)
            y = _mm(f"ab_out_a{l}", _flat(o), W["ab_out"][j][:512])
            y = _mm(f"ab_out_b{l}", _flat(cm), W["ab_out"][j][512:], add=y).reshape(B, L, D)
            r.update(z=z, abp=abp, kvn=kvn, qn=qn, kpe=kpe, cm=cm, kv=kv, qrp=qrp, q=q, o=o, lse=lse)
        else:
            z = _mm(f"cd_in{l}", _flat(xn), W["cd_in"][j]).reshape(B, L, 2304)
            pieces = [(0, 0, 256), (0, 256, 512), (0, 768, 128), (0, 896, 128), (0, 1024, 256), (0, 1280, 512), (0, 1792, 512)]
            pieces += [(1, 0, 256), (2, 0, 256), (1, 0, 128), (2, 0, 128), (1, 0, 512), (2, 0, 512)]
            cdp = rows(f"cd_pre{l}", _f_cd_pre, [z, cos512, sin512], pieces, [], [], [],
                       [(256, BF16), (256, BF16), (512, BF16), (512, BF16), (128, BF16), (128, BF16), (512, F32)],
                       [(k_, 0, w_) for k_, w_ in enumerate((256, 256, 512, 512, 128, 128, 512))])
            rq, rk, rv, sq, sk, sv, rg = cdp.fwd()
            yret = _ret_fwd(rq, rk, rv, W["lg"][j], n, tq)
            osw, lse = _swa_fwd(sq, sk, sv, W["sink"][j], n, tq)
            mrg = rows(f"cd_merge{l}", _f_cd_merge, [yret, rg],
                       [(0, 128 * g, 128) for g in range(4)] + [(1, 128 * g, 128) for g in range(4)], [], [],
                       W["ret_norm"][j], [(512, BF16)], [(0, 128 * g, 128) for g in range(4)])
            (yr,) = mrg.fwd()
            y = _mm(f"cd_out_a{l}", _flat(yr), W["cd_out"][j][:512])
            y = _mm(f"cd_out_b{l}", _flat(osw), W["cd_out"][j][512:], add=y).reshape(B, L, D)
            r.update(z=z, cdp=cdp, rq=rq, rk=rk, rv=rv, sq=sq, sk=sk, sv=sv, rg=rg, yret=yret, osw=osw, lse=lse,
                     mrg=mrg, yr=yr)
        two, outs2 = [(0, 0, D), (1, 0, D)], [(D, F32), (D, BF16)]
        r["mix_out"] = rows(f"mix_res_pre{l}", _f_res_pre, [s, y], two, [md], [(0, 2), (0, 3), (0, 4)],
                            [W["norm_ffn"][l]], outs2, two)
        s1, xn2 = r["mix_out"].fwd()
        if l == 0 and later_weights is not None:
            W = later_weights(0, s1)
        y2 = _ffn_fwd(f"ffn{l}", _flat(xn2), W["ffn_in"][l], W["ffn_out"][l]).reshape(B, L, D)
        if l < NL - 1:
            r["ffn_out"] = rows(f"ffn_res_pre{l}", _f_res_pre, [s1, y2], two, [md, mods[l + 1]], [(0, 5), (1, 0), (1, 1)],
                                [W["norm_mix"][l + 1]], outs2, two)
            s, xn = r["ffn_out"].fwd()
        else:
            r["ffn_out"] = rows(f"res_ffn{l}", _f_res, [s1, y2], two, [md], [(0, 5)], [], [(D, F32)], full(D))
            (s,) = r["ffn_out"].fwd()
        r["xn2"] = xn2
        saved.append(r)

    ds, d_norm_final, loss = _loss_head(s, target, W["norm_final"], n, tm)

    G = {k: [None] * len(v) for k, v in W.items() if isinstance(v, list)}
    G["norm_final"] = d_norm_final
    dm = [[None] * 6 for _ in range(NL)]
    dxn_next = None
    for l in reversed(range(NL)):
        j = l // 2
        even = l % 2 == 0
        r = saved[l]
        zero = early_grads(0, G) if l == 0 and early_grads is not None else None
        if l == NL - 1:
            (ds1, dy2), (dm[l][5],), _ = r["ffn_out"].bwd([ds], {0: F32, 1: BF16}, unread=(0,))
        else:
            smp = None if zero is None else [mods[l] + zero, mods[l + 1] + zero]
            (ds1, dy2), (dm[l][5], dm[l + 1][0], dm[l + 1][1]), (G["norm_mix"][l + 1],) = r["ffn_out"].bwd(
                [ds, dxn_next], {0: F32, 1: BF16}, grad_glob=(0,), samp=smp)
        dy2f, xn2f = _flat(dy2), _flat(r["xn2"])
        hid, dz2 = _ffn_mid_bwd(f"ffn_mid{l}", xn2f, dy2f, W["ffn_in"][l], W["ffn_out"][l])
        G["ffn_out"][l] = _mm(f"g_ffn_out{l}", hid, dy2f, ta=True, out_dtype=BF16)
        G["ffn_in"][l] = _mm(f"g_ffn_in{l}", xn2f, dz2, ta=True, out_dtype=BF16)
        dxn2 = _mm(f"d_xn2{l}", dz2, W["ffn_in"][l], tb=True).reshape(B, L, D)
        zero = early_grads(1, G) if l == 0 and early_grads is not None else None
        smp = None if zero is None else [mods[l] + zero]
        (ds0, dy), (dm[l][2], dm[l][3], dm[l][4]), (G["norm_ffn"][l],) = r["mix_out"].bwd(
            [ds1, dxn2], {0: F32, 1: BF16}, grad_glob=(0,), samp=smp)
        dyf = _flat(dy)
        if even:
            w_out = W["ab_out"][j]
            dcat = _mm(f"d_cat{l}", dyf, w_out, tb=True).reshape(B, L, -1)
            G["ab_out"][l // 2] = jnp.concatenate(
                [_mm(f"g_ab_out_a{l}", _flat(r["o"]), dyf, ta=True, out_dtype=BF16),
                 _mm(f"g_ab_out_b{l}", _flat(r["cm"]), dyf, ta=True, out_dtype=BF16)], axis=0)
            dq, dkv, dkpe = _mla_bwd(r["q"], r["kv"], r["kpe"], r["lse"], dcat, n, tq)
            (dq0,), _, _ = r["qrp"].bwd([dq], {0: BF16}, unread=(0,))
            dq0f, dkvf = _flat(dq0), _flat(dkv)
            G["wq"][j] = _mm(f"g_wq{l}", _flat(r["qn"]), dq0f, ta=True, out_dtype=BF16)
            G["wkv"][j] = _mm(f"g_wkv{l}", _flat(r["kvn"]), dkvf, ta=True, out_dtype=BF16)
            dqn = _mm(f"d_qn{l}", dq0f, W["wq"][j], tb=True).reshape(B, L, 256)
            dkvn = _mm(f"d_kvn{l}", dkvf, W["wkv"][j], tb=True).reshape(B, L, 256)
            (dz,), _, gg = r["abp"].bwd([dkvn, dqn, dkpe, (dcat, 512)], {0: BF16}, grad_glob=tuple(range(14)))
            G["kv_norm"][j], G["q_norm"][j] = gg[0], gg[1]
            G["v_norm"][j], G["ws"][j], G["bs"][j] = list(gg[2:6]), list(gg[6:10]), list(gg[10:14])
            w_in, key = W["ab_in"][j], "ab_in"
        else:
            w_out = W["cd_out"][j]
            dcat = _mm(f"d_cat{l}", dyf, w_out, tb=True).reshape(B, L, -1)
            G["cd_out"][j] = jnp.concatenate(
                [_mm(f"g_cd_out_a{l}", _flat(r["yr"]), dyf, ta=True, out_dtype=BF16),
                 _mm(f"g_cd_out_b{l}", _flat(r["osw"]), dyf, ta=True, out_dtype=BF16)], axis=0)
            (dyret, drg), _, gg = r["mrg"].bwd([(dcat, 0)], {0: F32, 1: F32}, grad_glob=(0, 1, 2, 3))
            G["ret_norm"][j] = list(gg)
            drq, drk, drv, dlg = _ret_bwd(r["rq"], r["rk"], r["rv"], W["lg"][j], dyret, n, tq)
            dsq, dsk, dsv, dsink = _swa_bwd(r["sq"], r["sk"], r["sv"], W["sink"][j], r["lse"], dcat, n, tq)
            G["lg"][j], G["sink"][j] = dlg, dsink
            (dz,), _, _ = r["cdp"].bwd([drq, drk, drv, dsq, dsk, dsv, drg], {0: BF16}, unread=(0,))
            w_in, key = W["cd_in"][j], "cd_in"
        dzf = _flat(dz)
        G[key][j] = _mm(f"g_{key}{l}", _flat(r["xn"]), dzf, ta=True, out_dtype=BF16)
        dxn = _mm(f"d_xn{l}", dzf, w_in, tb=True).reshape(B, L, D)
        if l == 0:
            (ds,), (dm[0][0], dm[0][1]), (G["norm_mix"][0],) = r["pre1"].bwd([dxn], {0: F32}, grad_glob=(0,),
                                                                            add={0: ds0}, samp=smp)
        else:
            ds, dxn_next = ds0, dxn
    dmods = jnp.stack([jnp.concatenate(d, axis=2) for d in dm])
    return loss, ds[:, :n], dmods, G


def kernel(x, c, ctx, c_ctx, ada_w, ada_b, norm_mix, norm_ffn, norm_final, ffn_in, ffn_out, ab_in, ab_out, mla_q_norm, mla_kv_norm, mla_wq_b, mla_wkv_b, cmlp_v_norm, cmlp_ws, cmlp_bs, cd_in, cd_out, ret_decay_fwd, ret_decay_bwd, ret_norm, swa_sink, loss_target, m_c_ctx, m_ada_w, m_ada_b, m_norm_mix, m_norm_ffn, m_norm_final, m_ffn_in, m_ffn_out, m_ab_in, m_ab_out, m_mla_q_norm, m_mla_kv_norm, m_mla_wq_b, m_mla_wkv_b, m_cmlp_v_norm, m_cmlp_ws, m_cmlp_bs, m_cd_in, m_cd_out, m_ret_decay_fwd, m_ret_decay_bwd, m_ret_norm, m_swa_sink, v_c_ctx, v_ada_w, v_ada_b, v_norm_mix, v_norm_ffn, v_norm_final, v_ffn_in, v_ffn_out, v_ab_in, v_ab_out, v_mla_q_norm, v_mla_kv_norm, v_mla_wq_b, v_mla_wkv_b, v_cmlp_v_norm, v_cmlp_ws, v_cmlp_bs, v_cd_in, v_cd_out, v_ret_decay_fwd, v_ret_decay_bwd, v_ret_norm, v_swa_sink):
    B, n, D = x.shape
    NL = ada_w.shape[0]
    NE, NO = ab_in.shape[0], cd_in.shape[0]
    me = _my_index()
    weights = dict(c_ctx=c_ctx, ada_w=ada_w, ada_b=ada_b, norm_mix=norm_mix, norm_ffn=norm_ffn, norm_final=norm_final,
                   ffn_in=ffn_in, ffn_out=ffn_out, ab_in=ab_in, ab_out=ab_out, mla_q_norm=mla_q_norm,
                   mla_kv_norm=mla_kv_norm, mla_wq_b=mla_wq_b, mla_wkv_b=mla_wkv_b, cmlp_v_norm=cmlp_v_norm,
                   cmlp_ws=cmlp_ws, cmlp_bs=cmlp_bs, cd_in=cd_in, cd_out=cd_out, ret_decay_fwd=ret_decay_fwd,
                   ret_decay_bwd=ret_decay_bwd, ret_norm=ret_norm, swa_sink=swa_sink)
    moments_m = dict(c_ctx=m_c_ctx, ada_w=m_ada_w, ada_b=m_ada_b, norm_mix=m_norm_mix, norm_ffn=m_norm_ffn,
                     norm_final=m_norm_final, ffn_in=m_ffn_in, ffn_out=m_ffn_out, ab_in=m_ab_in, ab_out=m_ab_out,
                     mla_q_norm=m_mla_q_norm, mla_kv_norm=m_mla_kv_norm, mla_wq_b=m_mla_wq_b, mla_wkv_b=m_mla_wkv_b,
                     cmlp_v_norm=m_cmlp_v_norm, cmlp_ws=m_cmlp_ws, cmlp_bs=m_cmlp_bs, cd_in=m_cd_in, cd_out=m_cd_out,
                     ret_decay_fwd=m_ret_decay_fwd, ret_decay_bwd=m_ret_decay_bwd, ret_norm=m_ret_norm,
                     swa_sink=m_swa_sink)
    moments_v = dict(c_ctx=v_c_ctx, ada_w=v_ada_w, ada_b=v_ada_b, norm_mix=v_norm_mix, norm_ffn=v_norm_ffn,
                     norm_final=v_norm_final, ffn_in=v_ffn_in, ffn_out=v_ffn_out, ab_in=v_ab_in, ab_out=v_ab_out,
                     mla_q_norm=v_mla_q_norm, mla_kv_norm=v_mla_kv_norm, mla_wq_b=v_mla_wq_b, mla_wkv_b=v_mla_wkv_b,
                     cmlp_v_norm=v_cmlp_v_norm, cmlp_ws=v_cmlp_ws, cmlp_bs=v_cmlp_bs, cd_in=v_cd_in, cd_out=v_cd_out,
                     ret_decay_fwd=v_ret_decay_fwd, ret_decay_bwd=v_ret_decay_bwd, ret_norm=v_ret_norm,
                     swa_sink=v_swa_sink)
    order = list(weights)

    Ns = ada_w.shape[2]
    (c_g,) = _all_gather("gather_c", [c])
    R = N_DEV * B + 8
    c_all = jnp.concatenate([c_g.reshape(N_DEV * B, D), jnp.broadcast_to(c_ctx[None], (8, D))], axis=0)
    ada_b_mine = lax.dynamic_slice_in_dim(ada_b, me * Ns, Ns, axis=1)[:, None, :]
    mods_shard = _ada_fwd(c_all, ada_w, ada_b_mine)
    (mods_g,) = _all_gather("gather_mods", [mods_shard])
    mods_full = jnp.transpose(mods_g, (1, 2, 0, 3)).reshape(NL, R, 6, D)
    mx = lax.dynamic_slice_in_dim(mods_full, me * B, B, axis=1)
    mh = jnp.broadcast_to(mods_full[:, N_DEV * B][:, None], (NL, B, 6, D))
    mods = jnp.stack([mx, mh], axis=2)

    big = ["ffn_in", "ffn_out", "ab_in", "ab_out", "cd_in", "cd_out", "mla_wq_b", "mla_wkv_b"]
    col_sharded = {"ffn_in", "ab_in", "cd_in", "mla_wq_b", "mla_wkv_b"}
    shards = {k: _to_bf16("cast_" + k, weights[k]) for k in big}
    first = {k: 0 if k.startswith("cd_") else 1 for k in big}
    a_keys = [k for k in big if first[k] and not k.startswith("ffn_")]
    b_keys = ["ffn_in", "ffn_out"]
    early = _all_gather("gather_wA", [shards[k][:1] for k in a_keys])
    (rn_g,) = _all_gather("gather_ret_norm", [ret_norm])
    small_first = [mods, rn_g] + list(early)
    wb_handle, wb_token = _exchange_start("gather_wB_start", [shards[k][:1] for k in b_keys], False, small_first)
    wc_handle, wc_token = _exchange_start("gather_wC_start", [shards[k][first[k]:] for k in big], False,
                                          small_first + [wb_token])
    mods = mods + (wb_token[0, 0] + wc_token[0, 0])

    def unshard(k, g):
        if k in col_sharded:
            f = jnp.transpose(g, (1, 2, 0, 3)).reshape(g.shape[1], g.shape[2], -1)
        else:
            f = jnp.transpose(g, (1, 0, 2, 3)).reshape(g.shape[1], -1, g.shape[3])
        return [f[i] for i in range(f.shape[0])]

    rn_full = jnp.transpose(rn_g, (1, 0, 2)).reshape(NO, -1)
    small_p = dict(weights, ret_norm=rn_full)
    full0 = {k: [None] * weights[k].shape[0] for k in big}
    for k, g in zip(a_keys, early):
        full0[k][:1] = unshard(k, g)

    def later_weights(stage, newest):
        if stage == 0:
            landed = _exchange_wait("gather_wB_wait", wb_handle, newest)
            for k, land in zip(b_keys, landed):
                full0[k][:1] = unshard(k, _with_own(land, shards[k][:1]))
        else:
            landed = _exchange_wait("gather_wC_wait", wc_handle, newest)
            for k, land in zip(big, landed):
                full0[k][first[k]:] = unshard(k, _with_own(land, shards[k][first[k]:]))
        return _layer_weights(full0, small_p)

    def to_slots(k, gl):
        g = jnp.stack(gl)
        if k == "ffn_in":
            half = N_DEV // 2
            g = g.reshape(g.shape[0], 2, g.shape[2], half, -1)
            return jnp.transpose(g, (1, 3, 0, 2, 4)).reshape(N_DEV, g.shape[0], g.shape[2], -1)
        if k in col_sharded:
            return jnp.transpose(g.reshape(g.shape[0], g.shape[1], N_DEV, -1), (2, 0, 1, 3))
        return jnp.transpose(g.reshape(g.shape[0], N_DEV, -1, g.shape[2]), (1, 0, 2, 3))

    def big_grads(G):
        return dict(ffn_in=[g if g is None else _ffn_deinterleave(g) for g in G["ffn_in"]], ffn_out=G["ffn_out"],
                    ab_in=[g if g is None else _ab_unperm(g) for g in G["ab_in"]],
                    ab_out=G["ab_out"], cd_in=G["cd_in"], cd_out=G["cd_out"],
                    mla_wq_b=[g if g is None else _wq_unperm(g) for g in G["wq"]], mla_wkv_b=G["wkv"])

    sent = {}

    def early_grads(stage, G):
        parts = big_grads(G)
        if stage == 0:
            srcs = [to_slots(k, parts[k][first[k]:]) for k in big]
            handle, g_token = _exchange_start("scatter_gC_start", srcs, slotted=True)
        else:
            srcs = [to_slots(k, parts[k][:1]) for k in b_keys]
            handle, g_token = _exchange_start("scatter_gB_start", srcs, slotted=True)
        sent[stage] = (handle, srcs)
        return g_token[0, 0]

    loss_part, grad_x, dmods, G = _local_step(x, ctx, loss_target, mods, _layer_weights(full0, small_p),
                                              later_weights, early_grads)

    dmx = dmods[:, :, 0].reshape(NL, B, 6 * D)
    dmh = jnp.sum(dmods[:, :, 1], axis=1).reshape(NL, 1, 6 * D)
    (dm_g,) = _all_gather("gather_dmods", [jnp.concatenate([dmx, dmh], axis=1)])
    dmx_all = jnp.transpose(dm_g[:, :, :B], (1, 0, 2, 3)).reshape(NL, N_DEV * B, 6 * D)
    dmh_all = jnp.sum(dm_g[:, :, B], axis=0)
    dm_rows = jnp.concatenate([dmx_all, dmh_all[:, None], jnp.zeros((NL, 7, 6 * D), F32)], axis=1)
    g_ada_b = jnp.sum(dm_rows, axis=1)
    dm_mine = lax.dynamic_slice_in_dim(dm_rows, me * Ns, Ns, axis=2)
    g_ada_w, dcond = _ada_bwd(c_all, ada_w, dm_mine)
    sg = jax.nn.sigmoid(c_ctx)
    d_c_ctx_part = jnp.sum(dcond[:, N_DEV * B], axis=0) * (sg * (1.0 + c_ctx * (1.0 - sg)))

    def cat(parts):
        return jnp.concatenate([p.reshape(-1) for p in parts])

    dlg = jnp.stack([jnp.sum(G["lg"][j], axis=0) for j in range(NO)])
    sig_f, sig_b = jax.nn.sigmoid(-ret_decay_fwd), jax.nn.sigmoid(-ret_decay_bwd)
    small = dict(
        loss=loss_part[0, 0:1],
        c_ctx=d_c_ctx_part,
        norm_mix=cat(G["norm_mix"]), norm_ffn=cat(G["norm_ffn"]), norm_final=G["norm_final"].reshape(-1),
        mla_q_norm=cat(G["q_norm"]), mla_kv_norm=cat(G["kv_norm"]),
        cmlp_v_norm=cat([cat(G["v_norm"][j]) for j in range(NE)]),
        cmlp_ws=cat([jnp.stack(G["ws"][j]) for j in range(NE)]),
        cmlp_bs=cat([jnp.stack([b_[:, 0] for b_ in G["bs"][j]]) for j in range(NE)]),
        ret_decay_fwd=(dlg[:, 0] * sig_f).reshape(-1), ret_decay_bwd=(dlg[:, 1] * sig_b).reshape(-1),
        ret_norm=cat([cat(G["ret_norm"][j]) for j in range(NO)]),
        swa_sink=cat([jnp.sum(G["sink"][j], axis=0) for j in range(NO)]),
    )
    small_keys = list(small)
    sizes = [small[k].shape[0] for k in small_keys]
    total = sum(sizes)
    padded = -(-total // 2048) * 2048
    packed = jnp.concatenate([small[k] for k in small_keys] + [jnp.zeros((padded - total,), F32)]).reshape(-1, 128)
    (small_g,) = _all_gather("gather_small", [packed])

    parts = big_grads(G)
    landed0 = dict(zip(a_keys, _all_to_all("scatter_gA", [to_slots(k, parts[k][:1]) for k in a_keys])))
    for k, land, src in zip(b_keys, _exchange_wait("scatter_gB_wait", sent[1][0], grad_x), sent[1][1]):
        landed0[k] = _with_own(land, src)
    landed = []
    for k, land, src in zip(big, _exchange_wait("scatter_gC_wait", sent[0][0], grad_x), sent[0][1]):
        rest = _with_own(land, src)
        landed.append(jnp.concatenate([landed0[k], rest], axis=1) if first[k] else rest)

    grads, deltas, new_m, new_v = {}, {}, {}, {}
    for k, land in zip(big, landed):
        grads[k], deltas[k], new_m[k], new_v[k] = _adamw_from_slots("adamw_" + k, weights[k], moments_m[k], moments_v[k], land)
    deltas["ada_w"], new_m["ada_w"], new_v["ada_w"] = [
        o.reshape(ada_w.shape) for o in _adamw("adamw_ada_w", _flat(ada_w), _flat(g_ada_w), _flat(m_ada_w), _flat(v_ada_w))]
    grads["ada_w"] = g_ada_w

    sums_only = ("loss", "ret_norm")

    def packed_of(src, fill):
        vals = [jnp.full((sizes[i],), fill, F32) if k in sums_only else src[k].reshape(-1)
                for i, k in enumerate(small_keys)]
        return jnp.concatenate(vals + [jnp.full((padded - total,), fill, F32)]).reshape(-1, 128)

    w_p, m_p, v_p = packed_of(weights, 0.0), packed_of(moments_m, 0.0), packed_of(moments_v, 1.0)

    def f_small(w_, m_, v_, land_):
        g = _sum_slots(land_)
        return (g,) + _adamw_math(w_, g, m_, v_)

    g_p, d_p, nm_p, nv_p = _ew("adamw_small", f_small, [w_p, m_p, v_p, small_g], [F32] * 4)
    offs = np.cumsum([0] + sizes)
    for i, k in enumerate(small_keys):
        sl = slice(int(offs[i]), int(offs[i + 1]))
        if k == "loss":
            loss = g_p.reshape(-1)[int(offs[i])]
        elif k == "ret_norm":
            g_full = g_p.reshape(-1)[sl].reshape(NO, -1)
            g_mine = lax.dynamic_slice_in_dim(g_full, me * ret_norm.shape[1], ret_norm.shape[1], axis=1)
            d_, m_, v_ = _adamw("adamw_ret_norm", *[jnp.pad(a, ((0, 8 - NO), (0, 128 - a.shape[1])), constant_values=cv)
                                                     for a, cv in ((ret_norm, 0.0), (g_mine, 0.0), (m_ret_norm, 0.0), (v_ret_norm, 1.0))])
            grads[k] = g_mine
            deltas[k], new_m[k], new_v[k] = [a[:NO, :ret_norm.shape[1]] for a in (d_, m_, v_)]
        else:
            shp = weights[k].shape
            grads[k], deltas[k], new_m[k], new_v[k] = [a.reshape(-1)[sl].reshape(shp) for a in (g_p, d_p, nm_p, nv_p)]
    pad_b = lambda a, cv=0.0: jnp.pad(a, ((0, 8 - NL), (0, 0)), constant_values=cv)
    d_, m_, v_ = _adamw("adamw_ada_b", pad_b(ada_b), pad_b(g_ada_b), pad_b(m_ada_b), pad_b(v_ada_b, 1.0))
    grads["ada_b"] = g_ada_b
    deltas["ada_b"], new_m["ada_b"], new_v["ada_b"] = d_[:NL], m_[:NL], v_[:NL]

    return (loss, grad_x, *[grads[k] for k in order], *[deltas[k] for k in order],
            *[new_m[k] for k in order], *[new_v[k] for k in order])
```

```python
import functools

import numpy as np
import jax
import jax.numpy as jnp
from jax import lax
from jax.experimental import pallas as pl
from jax.experimental.pallas import tpu as pltpu

F32 = jnp.float32
BF16 = jnp.bfloat16
EPS = 1e-6
NEG_INF = -1e30
N_DEV = 8
GRID_W = 64
ROPE_THETA = 10000.0
ROPE_DIM = 64
SWA_WINDOW = 128
MLA_SCALE = (128 + 64) ** -0.5
SWA_SCALE = 64 ** -0.5
RET_K_SCALE = 64 ** -0.5
ADAM_LR, ADAM_B1, ADAM_B2, ADAM_EPS, ADAM_WD, ADAM_STEP = 0.001, 0.9, 0.999, 1e-08, 0.01, 10
V7X_VMEM_LIMIT = 56 * 1024 * 1024
MESH = pl.DeviceIdType.MESH


def _pallas(body, **kw):
    return pl.pallas_call(body, **kw)


def _params(sem=None):
    return pltpu.CompilerParams(dimension_semantics=sem, vmem_limit_bytes=V7X_VMEM_LIMIT)


def _tile(n, cap, align):
    best = None
    for t in range(align, min(n, cap) + 1, align):
        if n % t == 0:
            best = t
    return n if best is None else best


def _sds(shape, dtype):
    return jax.ShapeDtypeStruct(tuple(shape), dtype)


def _ew(name, f, ins, out_dtypes, cap_elems=131072):
    R, C = ins[0].shape[-2:]
    tr = _tile(R, max(16, cap_elems // C), 16)
    n_in = len(ins)

    def spec(a):
        if a.ndim == 2:
            return pl.BlockSpec((tr, C), lambda i: (i, 0))
        return pl.BlockSpec((a.shape[0], tr, C), lambda i: (0, i, 0))

    def body(*refs):
        outs = f(*[r[...] for r in refs[:n_in]])
        for r, o in zip(refs[n_in:], outs):
            r[...] = o.astype(r.dtype)

    return _pallas(
        body, name=name, grid=(R // tr,), in_specs=[spec(a) for a in ins],
        out_specs=[pl.BlockSpec((tr, C), lambda i: (i, 0)) for _ in out_dtypes],
        out_shape=[_sds((R, C), d) for d in out_dtypes], compiler_params=_params(("parallel",)),
    )(*ins)


def _to_bf16(name, w):
    w2 = w.reshape(-1, w.shape[-1])
    return _ew(name, lambda v: (v,), [w2], [BF16])[0].reshape(w.shape)


def _adamw_math(w, g, m, v):
    m = ADAM_B1 * m + (1.0 - ADAM_B1) * g
    v = ADAM_B2 * v + (1.0 - ADAM_B2) * (g * g)
    m_hat = m / (1.0 - ADAM_B1 ** ADAM_STEP)
    v_hat = v / (1.0 - ADAM_B2 ** ADAM_STEP)
    delta = -ADAM_LR * (m_hat / (jnp.sqrt(v_hat) + ADAM_EPS) + ADAM_WD * w)
    return delta, m, v


def _sum_slots(land):
    g = land[0].astype(F32)
    for s in range(1, land.shape[0]):
        g = g + land[s].astype(F32)
    return g


def _adamw_from_slots(name, w, m, v, lands):
    nl, K, C = w.shape
    tr = _tile(K, max(16, 65536 // C), 16)
    per = K // tr
    starts = np.cumsum([0] + [ld.shape[1] * per for ld in lands])
    ng = len(lands)

    def land_spec(g):
        lo, hi = int(starts[g]), int(starts[g + 1])
        return pl.BlockSpec((N_DEV, tr, C), lambda t: (0, jnp.clip(t, lo, hi - 1) - lo, 0))

    def body(*refs):
        w_ref, m_ref, v_ref = refs[:3]
        land_refs, out_refs = refs[3:3 + ng], refs[3 + ng:]
        t = pl.program_id(0)
        for g in range(ng):
            @pl.when((t >= int(starts[g])) & (t < int(starts[g + 1])))
            def _(g=g):
                grad = _sum_slots(land_refs[g][...])
                for r, o in zip(out_refs, (grad,) + _adamw_math(w_ref[...], grad, m_ref[...], v_ref[...])):
                    r[...] = o

    row = pl.BlockSpec((tr, C), lambda t: (t, 0))
    outs = _pallas(
        body, name=name, grid=(nl * per,), in_specs=[row] * 3 + [land_spec(g) for g in range(ng)],
        out_specs=[row] * 4, out_shape=[_sds((nl * K, C), F32)] * 4, compiler_params=_params(("parallel",)),
    )(w.reshape(-1, C), m.reshape(-1, C), v.reshape(-1, C), *[ld.reshape(N_DEV, -1, C) for ld in lands])
    return [o.reshape(w.shape) for o in outs]


def _adamw(name, w, g, m, v):
    outs = _ew(name, lambda w_, g_, m_, v_: _adamw_math(w_, g_, m_, v_), [w, g, m, v], [F32] * 3)
    return outs


def _mm(name, a, b, ta=False, tb=False, out_dtype=F32, add=None):
    M, K = (a.shape[1], a.shape[0]) if ta else a.shape
    N = b.shape[0] if tb else b.shape[1]
    tm = _tile(M, 1408, 128)
    tn = _tile(N, 1024, 128)
    if tn < 256 and N <= 2432:
        tn = N
    tk = _tile(K, 1792, 128)
    nk = K // tk
    a_spec = pl.BlockSpec((tk, tm), lambda i, j, k: (k, i)) if ta else pl.BlockSpec((tm, tk), lambda i, j, k: (i, k))
    b_spec = pl.BlockSpec((tn, tk), lambda i, j, k: (j, k)) if tb else pl.BlockSpec((tk, tn), lambda i, j, k: (k, j))
    o_spec = pl.BlockSpec((tm, tn), lambda i, j, k: (i, j))
    dims = (((0 if ta else 1,), (1 if tb else 0,)), ((), ()))
    has_add = add is not None

    def product(a_ref, b_ref):
        return lax.dot_general(a_ref[...].astype(BF16), b_ref[...].astype(BF16), dims, preferred_element_type=F32)

    def body_single(*refs):
        acc = product(refs[0], refs[1])
        if has_add:
            acc = acc + refs[2][...]
        refs[-1][...] = acc.astype(refs[-1].dtype)

    def body(*refs):
        a_ref, b_ref = refs[0], refs[1]
        add_ref = refs[2] if has_add else None
        o_ref, acc = refs[-2], refs[-1]
        k = pl.program_id(2)

        @pl.when(k == 0)
        def _():
            acc[...] = add_ref[...] if has_add else jnp.zeros_like(acc)

        acc[...] += product(a_ref, b_ref)

        @pl.when(k == nk - 1)
        def _():
            o_ref[...] = acc[...].astype(o_ref.dtype)

    ins = [a, b] + ([add] if has_add else [])
    specs = [a_spec, b_spec] + ([o_spec] if has_add else [])
    return _pallas(
        body_single if nk == 1 else body, name=name, grid=(M // tm, N // tn, nk), in_specs=specs, out_specs=o_spec,
        out_shape=_sds((M, N), out_dtype), scratch_shapes=[] if nk == 1 else [pltpu.VMEM((tm, tn), F32)],
        compiler_params=_params(("parallel", "parallel", "arbitrary")),
    )(*ins)


FFN_TILE = 256


def _ffn_interleave(w):
    D, F2 = w.shape
    nj = F2 // (2 * FFN_TILE)

    def body(a_ref, b_ref, o_ref):
        o_ref[:, :FFN_TILE] = a_ref[...]
        o_ref[:, FFN_TILE:] = b_ref[...]

    return _pallas(
        body, name="ffn_interleave", grid=(nj,),
        in_specs=[pl.BlockSpec((D, FFN_TILE), lambda j: (0, j)), pl.BlockSpec((D, FFN_TILE), lambda j: (0, j + nj))],
        out_specs=pl.BlockSpec((D, 2 * FFN_TILE), lambda j: (0, j)), out_shape=_sds((D, F2), w.dtype),
        compiler_params=_params(("parallel",)),
    )(w, w)


def _ffn_deinterleave(w):
    D, F2 = w.shape
    nj = F2 // (2 * FFN_TILE)

    def body(w_ref, o_ref):
        o_ref[0] = w_ref[:, :FFN_TILE]
        o_ref[1] = w_ref[:, FFN_TILE:]

    return _pallas(
        body, name="ffn_deinterleave", grid=(nj,), in_specs=[pl.BlockSpec((D, 2 * FFN_TILE), lambda j: (0, j))],
        out_specs=pl.BlockSpec((2, D, FFN_TILE), lambda j: (0, 0, j)), out_shape=_sds((2, D, F2 // 2), w.dtype),
        compiler_params=_params(("parallel",)),
    )(w)


def _ffn_specs(M, D, F):
    tm = _tile(M, 1024, 128)
    x_spec = pl.BlockSpec((tm, D), lambda i, j: (i, 0))
    wi_spec = pl.BlockSpec((D, 2 * FFN_TILE), lambda i, j: (0, j))
    wo_spec = pl.BlockSpec((FFN_TILE, D), lambda i, j: (j, 0))
    return tm, F // FFN_TILE, x_spec, wi_spec, wo_spec


def _ffn_fwd(name, xn, w_in, w_out):
    M, D = xn.shape
    tm, nj, x_spec, wi_spec, wo_spec = _ffn_specs(M, D, w_out.shape[0])

    def body(x_ref, wi_ref, wo_ref, y_ref):
        j = pl.program_id(1)
        z = _dot(x_ref[...], wi_ref[...])
        hid = (jax.nn.silu(z[:, :FFN_TILE]) * z[:, FFN_TILE:]).astype(BF16)
        part = _dot(hid, wo_ref[...])

        @pl.when(j == 0)
        def _():
            y_ref[...] = part

        @pl.when(j > 0)
        def _():
            y_ref[...] += part

    return _pallas(
        body, name=name, grid=(M // tm, nj), in_specs=[x_spec, wi_spec, wo_spec], out_specs=x_spec,
        out_shape=_sds((M, D), F32), compiler_params=_params(("parallel", "arbitrary")),
    )(xn, w_in, w_out)


def _ffn_mid_bwd(name, xn, dy, w_in, w_out):
    M, D = xn.shape
    F = w_out.shape[0]
    tm, nj, x_spec, wi_spec, wo_spec = _ffn_specs(M, D, F)

    def body(x_ref, dy_ref, wi_ref, wo_ref, h_ref, dz_ref):
        z = _dot(x_ref[...], wi_ref[...])
        a, b = z[:, :FFN_TILE], z[:, FFN_TILE:]
        dh = _dot_nt(dy_ref[...], wo_ref[...])
        sig = jax.nn.sigmoid(a)
        act = a * sig
        h_ref[...] = (act * b).astype(BF16)
        dz_ref[:, :FFN_TILE] = (dh * b * (sig * (1.0 + a * (1.0 - sig)))).astype(BF16)
        dz_ref[:, FFN_TILE:] = (dh * act).astype(BF16)

    return _pallas(
        body, name=name, grid=(M // tm, nj), in_specs=[x_spec, x_spec, wi_spec, wo_spec],
        out_specs=[pl.BlockSpec((tm, FFN_TILE), lambda i, j: (i, j)),
                   pl.BlockSpec((tm, 2 * FFN_TILE), lambda i, j: (i, j))],
        out_shape=[_sds((M, F), BF16), _sds((M, 2 * F), BF16)],
        compiler_params=_params(("parallel", "parallel")),
    )(xn, dy, w_in, w_out)


class _Rows:
    def __init__(self, name, f, B, L, n, tm, arrays, pieces, samp, samp_pieces, glob, out_arrays, out_pieces):
        self.name, self.f, self.B, self.L, self.n, self.tm = name, f, B, L, n, tm
        self.arrays, self.pieces, self.samp, self.samp_pieces, self.glob = arrays, pieces, samp, samp_pieces, glob
        self.out_arrays, self.out_pieces = out_arrays, out_pieces
        self.nx = n // tm
        self.grid = (B, L // tm)

    def _row_spec(self, C, batched=True):
        tm = self.tm
        if batched:
            return pl.BlockSpec((None, tm, C), lambda b, i: (b, i, 0))
        return pl.BlockSpec((tm, C), lambda b, i: (i, 0))

    def _in_specs(self, unread=()):
        nx, tm = self.nx, self.tm
        specs = [self._row_spec(a.shape[-1], a.ndim == 3) for a in self.arrays]
        for ai in unread:
            specs[ai] = pl.BlockSpec((None, tm, self.arrays[ai].shape[-1]), lambda b, i: (0, 0, 0))
        specs += [pl.BlockSpec((None, None) + s.shape[2:], lambda b, i: (b, i // nx, 0, 0)) for s in self.samp]
        specs += [pl.BlockSpec(g.shape, lambda b, i, nd=g.ndim: (0,) * nd) for g in self.glob]
        return specs

    def _load(self, a_refs, s_refs, g_refs):
        args = [a_refs[ai][:, cs:cs + cw].astype(F32) for ai, cs, cw in self.pieces]
        args += [s_refs[si][r:r + 1, :].astype(F32) for si, r in self.samp_pieces]
        args += [g[...].astype(F32) for g in g_refs]
        return args

    def fwd(self):
        na, ns, ng = len(self.arrays), len(self.samp), len(self.glob)

        def body(*refs):
            a_refs, s_refs, g_refs = refs[:na], refs[na:na + ns], refs[na + ns:na + ns + ng]
            o_refs = refs[na + ns + ng:]
            outs = self.f(*self._load(a_refs, s_refs, g_refs))
            for (oi, cs, cw), o in zip(self.out_pieces, outs):
                o_refs[oi][:, cs:cs + cw] = o.astype(o_refs[oi].dtype)

        return _pallas(
            body, name=self.name + "_fwd", grid=self.grid, in_specs=self._in_specs(),
            out_specs=[self._row_spec(C) for C, _ in self.out_arrays],
            out_shape=[_sds((self.B, self.L, C), d) for C, d in self.out_arrays],
            compiler_params=_params(("parallel", "parallel")),
        )(*self.arrays, *self.samp, *self.glob)

    def bwd(self, cts, grad_arrays, grad_glob=(), add=None, samp=None, unread=()):
        samp = self.samp if samp is None else samp
        na, ns, ng, nc = len(self.arrays), len(self.samp), len(self.glob), len(cts)
        ct_off = [c[1] if isinstance(c, tuple) else 0 for c in cts]
        cts = [c[0] if isinstance(c, tuple) else c for c in cts]
        add = add or {}
        add_keys = list(add)
        g_idx = list(grad_arrays)
        nx, B = self.nx, self.B
        n_in = na + ns + ng + nc + len(add_keys)
        n_pieces, n_sp = len(self.pieces), len(self.samp_pieces)

        def body(*refs):
            a_refs, s_refs, g_refs = refs[:na], refs[na:na + ns], refs[na + ns:na + ns + ng]
            c_refs = refs[na + ns + ng:na + ns + ng + nc]
            add_refs = refs[na + ns + ng + nc:n_in]
            d_refs = refs[n_in:n_in + len(g_idx)]
            ds_refs = refs[n_in + len(g_idx):n_in + len(g_idx) + n_sp]
            dg_refs = refs[n_in + len(g_idx) + n_sp:]
            b, i = pl.program_id(0), pl.program_id(1)
            args = self._load(a_refs, s_refs, g_refs)
            _, vjp = jax.vjp(lambda *xs: tuple(self.f(*xs)), *args)
            grads = vjp(tuple(c_refs[oi][:, ct_off[oi] + cs:ct_off[oi] + cs + cw].astype(F32)
                              for oi, cs, cw in self.out_pieces))
            for k, ai in enumerate(g_idx):
                covered = sum(cw for pa, _, cw in self.pieces if pa == ai)
                if covered < self.arrays[ai].shape[-1]:
                    d_refs[k][...] = jnp.zeros_like(d_refs[k])
                for (pa, cs, cw), gr in zip(self.pieces, grads[:n_pieces]):
                    if pa == ai:
                        if ai in add:
                            gr = gr + add_refs[add_keys.index(ai)][:, cs:cs + cw]
                        d_refs[k][:, cs:cs + cw] = gr.astype(d_refs[k].dtype)

            @pl.when((i == 0) | (i == nx))
            def _():
                for r in ds_refs:
                    r[...] = jnp.zeros_like(r)

            for r, gr in zip(ds_refs, grads[n_pieces:n_pieces + n_sp]):
                r[...] += gr

            @pl.when((b == 0) & (i == 0))
            def _():
                for r in dg_refs:
                    r[...] = jnp.zeros_like(r)

            for r, gi in zip(dg_refs, grad_glob):
                r[...] += grads[n_pieces + n_sp + gi]

        in_specs = self._in_specs(unread) + [self._row_spec(c.shape[-1]) for c in cts]
        in_specs += [self._row_spec(add[k].shape[-1]) for k in add_keys]
        out_specs = [self._row_spec(self.arrays[ai].shape[-1]) for ai in g_idx]
        out_shape = [_sds(self.arrays[ai].shape, grad_arrays[ai]) for ai in g_idx]
        for si, _ in self.samp_pieces:
            C = self.samp[si].shape[-1]
            out_specs.append(pl.BlockSpec((None, None, 1, C), lambda b, i: (b, i // nx, 0, 0)))
            out_shape.append(_sds((B, 2, 1, C), F32))
        for gi in grad_glob:
            g = self.glob[gi]
            out_specs.append(pl.BlockSpec(g.shape, lambda b, i, nd=g.ndim: (0,) * nd))
            out_shape.append(_sds(g.shape, F32))
        outs = _pallas(
            body, name=self.name + "_bwd", grid=self.grid, in_specs=in_specs, out_specs=out_specs,
            out_shape=out_shape, compiler_params=_params(("arbitrary", "arbitrary")),
        )(*self.arrays, *samp, *self.glob, *cts, *[add[k] for k in add_keys])
        ng_ = len(g_idx)
        return outs[:ng_], outs[ng_:ng_ + n_sp], outs[ng_ + n_sp:]


def _rms(x, g):
    return x * lax.rsqrt(jnp.mean(x * x, axis=-1, keepdims=True) + EPS) * g


@jax.custom_vjp
def _swap_halves(x):
    w = x.shape[-1]
    lane = lax.broadcasted_iota(jnp.int32, x.shape, x.ndim - 1)
    up = pltpu.roll(x, w - ROPE_DIM // 2, x.ndim - 1)
    down = pltpu.roll(x, ROPE_DIM // 2, x.ndim - 1)
    return jnp.where(lane % ROPE_DIM < ROPE_DIM // 2, up, down)


_swap_halves.defvjp(lambda x: (_swap_halves(x), None), lambda _, ct: (_swap_halves(ct),))


@jax.custom_vjp
def _bdot(a, b):
    return jnp.dot(a.astype(BF16), b.astype(BF16), preferred_element_type=F32)


_bdot.defvjp(lambda a, b: (_bdot(a, b), (a, b)),
             lambda res, ct: (_dot_nt(ct.astype(BF16), res[1].astype(BF16)), _dot_tn(res[0].astype(BF16), ct.astype(BF16))))


def _rope(x, cos2, sin2):
    return x * cos2 + _swap_halves(x) * sin2


def _f_pre(s, shift, scale, g):
    return (_rms(s, g) * (1.0 + scale) + shift,)


def _f_res(s, y, gate):
    return (s + gate * y,)


def _f_res_pre(s, y, gate, shift, scale, g):
    s1 = s + gate * y
    return s1, _rms(s1, g) * (1.0 + scale) + shift


def _f_ab_pre(*args):
    kv_lat, q_lat, kpe = args[0:3]
    us, vs = args[3:7], args[7:11]
    cos2, sin2 = args[11:13]
    kv_norm, q_norm = args[13:15]
    vns, wss, bss = args[15:19], args[19:23], args[23:27]
    outs = [_rms(kv_lat, kv_norm), _rms(q_lat, q_norm), _rope(kpe, cos2, sin2)]
    for u, v, vn, ws, bs in zip(us, vs, vns, wss, bss):
        vg = _rms(jax.nn.gelu(v), vn)
        mixed = _bdot(ws, vg) + bs
        outs.append(jax.nn.gelu(u) * mixed)
    return tuple(outs)


def _f_q_rope(qn, qr, cos2, sin2):
    return qn, _rope(qr, cos2, sin2)


def _f_cd_pre(rk, rv, sk, sv, rq, rg, sq, c256, s256, c128, s128, c512, s512):
    return (_rope(rq, c256, s256), _rope(rk * RET_K_SCALE, c256, s256), rv,
            _rope(sq, c512, s512), _rope(sk, c128, s128), sv, rg)


def _f_cd_merge(*args):
    ys, rgs, rns = args[0:4], args[4:8], args[8:12]
    return tuple(_rms(y, rn) * jax.nn.silu(rg) for y, rg, rn in zip(ys, rgs, rns))


def _dot_nt(a, b):
    return lax.dot_general(a, b, (((1,), (1,)), ((), ())), preferred_element_type=F32)


def _dot_tn(a, b):
    return lax.dot_general(a, b, (((0,), (0,)), ((), ())), preferred_element_type=F32)


def _dot(a, b):
    return jnp.dot(a, b, preferred_element_type=F32)


def _tile_spec(tq, C):
    return pl.BlockSpec((None, tq, C), lambda b, i: (b, i, 0))


def _full_spec(L, C):
    return pl.BlockSpec((None, L, C), lambda b, i: (b, 0, 0))


def _mla_half(h):
    lane = lax.broadcasted_iota(jnp.int32, (1, 128), 1)
    return (lane < 64) if h % 2 == 0 else (lane >= 64)


def _mla_query(q_ref, h):
    pair = q_ref[:, 512 + 128 * (h // 2):640 + 128 * (h // 2)]
    return jnp.concatenate([q_ref[:, 128 * h:128 * h + 128], jnp.where(_mla_half(h), pair, jnp.zeros_like(pair))], axis=1)


def _mla_fill_keys(kcat, kv_ref, kpe_ref):
    kp = kpe_ref[...]
    for h in range(4):
        kcat[h, :, 0:128] = kv_ref[:, 256 * h:256 * h + 128]
        kcat[h, :, 128:256] = jnp.where(_mla_half(h), kp, jnp.zeros_like(kp))


def _mla_fwd(q, kv, kpe, n, tq):
    B, L, _ = q.shape

    def body(q_ref, kv_ref, kpe_ref, o_ref, lse_ref, kcat):
        i = pl.program_id(1)

        @pl.when(i == 0)
        def _():
            _mla_fill_keys(kcat, kv_ref, kpe_ref)

        def tile(keys):
            for h in range(4):
                s = _dot_nt(_mla_query(q_ref, h), kcat[h, keys, :]) * MLA_SCALE
                m = jnp.max(s, axis=1, keepdims=True)
                e = jnp.exp(s - m)
                l = jnp.sum(e, axis=1, keepdims=True)
                p = (e * (1.0 / l)).astype(BF16)
                o_ref[:, 128 * h:128 * h + 128] = _dot(p, kv_ref[keys, 256 * h + 128:256 * h + 256]).astype(o_ref.dtype)
                lse_ref[:, h:h + 1] = m + jnp.log(l)

        pl.when(i < n // tq)(functools.partial(tile, slice(0, L)))
        pl.when(i >= n // tq)(functools.partial(tile, slice(n, L)))

    return _pallas(
        body, name="mla_fwd", grid=(B, L // tq),
        in_specs=[_tile_spec(tq, 768), _full_spec(L, 1024), _full_spec(L, 128)],
        out_specs=[_tile_spec(tq, 512), _tile_spec(tq, 4)],
        out_shape=[_sds((B, L, 512), BF16), _sds((B, L, 4), F32)],
        scratch_shapes=[pltpu.VMEM((4, L, 256), BF16)],
        compiler_params=_params(("parallel", "arbitrary")),
    )(q, kv, kpe)


def _mla_bwd(q, kv, kpe, lse, do, n, tq):
    B, L, _ = q.shape

    def body(q_ref, kv_ref, kpe_ref, lse_ref, do_ref, dq_ref, dkv_ref, dkpe_ref, kcat):
        i = pl.program_id(1)

        @pl.when(i == 0)
        def _():
            dkv_ref[...] = jnp.zeros_like(dkv_ref)
            dkpe_ref[...] = jnp.zeros_like(dkpe_ref)
            _mla_fill_keys(kcat, kv_ref, kpe_ref)

        def tile(keys):
            rope_pair = None
            for h in range(4):
                qc, kc = _mla_query(q_ref, h), kcat[h, keys, :]
                v = kv_ref[keys, 256 * h + 128:256 * h + 256]
                p = jnp.exp(_dot_nt(qc, kc) * MLA_SCALE - lse_ref[:, h:h + 1])
                doh = do_ref[:, 128 * h:128 * h + 128].astype(BF16)
                dp = _dot_nt(doh, v)
                delta = jnp.sum(p * dp, axis=1, keepdims=True)
                ds = (p * (dp - delta) * MLA_SCALE).astype(BF16)
                dqc = _dot(ds, kc)
                dq_ref[:, 128 * h:128 * h + 128] = dqc[:, 0:128]
                if h % 2 == 0:
                    rope_pair = dqc[:, 128:256]
                else:
                    dq_ref[:, 512 + 128 * (h // 2):640 + 128 * (h // 2)] = jnp.where(_mla_half(h), dqc[:, 128:256], rope_pair)
                dkc = _dot_tn(ds, qc)
                dkv_ref[keys, 256 * h:256 * h + 128] += dkc[:, 0:128]
                dkpe_ref[keys, :] += dkc[:, 128:256]
                dkv_ref[keys, 256 * h + 128:256 * h + 256] += _dot_tn(p.astype(BF16), doh)

        pl.when(i < n // tq)(functools.partial(tile, slice(0, L)))
        pl.when(i >= n // tq)(functools.partial(tile, slice(n, L)))

    return _pallas(
        body, name="mla_bwd", grid=(B, L // tq),
        in_specs=[_tile_spec(tq, 768), _full_spec(L, 1024), _full_spec(L, 128), _tile_spec(tq, 4),
                  pl.BlockSpec((None, tq, 512), lambda b, i: (b, i, 0))],
        out_specs=[_tile_spec(tq, 768), _full_spec(L, 1024), _full_spec(L, 128)],
        out_shape=[_sds((B, L, 768), F32), _sds((B, L, 1024), F32), _sds((B, L, 128), F32)],
        scratch_shapes=[pltpu.VMEM((4, L, 256), BF16)],
        compiler_params=_params(("parallel", "arbitrary")),
    )(q, kv, kpe, lse, do)


def _swa_window(i, tq, n):
    W = min(tq + 2 * SWA_WINDOW, n)
    lo = jnp.clip(i * tq - SWA_WINDOW, 0, n - W)
    return pl.multiple_of(lo, 128), W


def _swa_mask(i, lo, tq, W, n):
    qpos = i * tq + lax.broadcasted_iota(jnp.int32, (tq, 1), 0)
    kpos = lo + lax.broadcasted_iota(jnp.int32, (1, W), 1)
    return (jnp.abs(qpos - kpos) <= SWA_WINDOW) & (qpos < n)


def _swa_fill(rep, src_ref):
    for g in range(2):
        rep[:, 256 * g:256 * g + 256] = jnp.concatenate([src_ref[:, 64 * g:64 * g + 64]] * 4, axis=1)


def _swa_head_rows(a):
    lane = lax.broadcasted_iota(jnp.int32, (1, 256), 1)
    return jnp.concatenate([jnp.where(lane // 64 == h, a, jnp.zeros_like(a)) for h in range(4)], axis=0)


def _swa_fold_rows(a, tq):
    lane = lax.broadcasted_iota(jnp.int32, (1, 256), 1)
    out = jnp.where(lane // 64 == 0, a[0:tq], 0.0)
    for h in range(1, 4):
        out = jnp.where(lane // 64 == h, a[h * tq:(h + 1) * tq], out)
    return out


def _swa_scores(i, g, tq, n, L, q_ref, krep, sink_ref):
    lo, W = _swa_window(i, tq, n)
    mask = jnp.concatenate([_swa_mask(i, lo, tq, W, n)] * 4, axis=0)
    qs = _swa_head_rows(q_ref[:, 256 * g:256 * g + 256])
    kl, kc = krep[pl.ds(lo, W), 256 * g:256 * g + 256], krep[n:L, 256 * g:256 * g + 256]
    s1 = jnp.where(mask, _dot_nt(qs, kl) * SWA_SCALE, NEG_INF)
    s2 = _dot_nt(qs, kc) * SWA_SCALE
    sk = jnp.concatenate([jnp.broadcast_to(sink_ref[0:1, 4 * g + h:4 * g + h + 1], (tq, 1)) for h in range(4)], axis=0)
    return qs, kl, kc, s1, s2, sk, lo, W


def _swa_fwd(q, k, v, sink, n, tq):
    B, L, _ = q.shape

    def body(q_ref, k_ref, v_ref, sink_ref, o_ref, lse_ref, krep, vrep):
        i = pl.program_id(1)

        @pl.when(i == 0)
        def _():
            _swa_fill(krep, k_ref)
            _swa_fill(vrep, v_ref)

        for g in range(2):
            _, _, _, s1, s2, sk, lo, W = _swa_scores(i, g, tq, n, L, q_ref, krep, sink_ref)
            vl, vc = vrep[pl.ds(lo, W), 256 * g:256 * g + 256], vrep[n:L, 256 * g:256 * g + 256]
            m = jnp.maximum(jnp.maximum(jnp.max(s1, axis=1, keepdims=True), jnp.max(s2, axis=1, keepdims=True)), sk)
            e1, e2 = jnp.exp(s1 - m), jnp.exp(s2 - m)
            l = jnp.sum(e1, axis=1, keepdims=True) + jnp.sum(e2, axis=1, keepdims=True) + jnp.exp(sk - m)
            r = 1.0 / l
            o = _dot((e1 * r).astype(BF16), vl) + _dot((e2 * r).astype(BF16), vc)
            o_ref[:, 256 * g:256 * g + 256] = _swa_fold_rows(o, tq).astype(o_ref.dtype)
            lse = m + jnp.log(l)
            for h in range(4):
                lse_ref[:, 4 * g + h:4 * g + h + 1] = lse[h * tq:(h + 1) * tq]

    return _pallas(
        body, name="swa_fwd", grid=(B, L // tq),
        in_specs=[_tile_spec(tq, 512), _full_spec(L, 128), _full_spec(L, 128), pl.BlockSpec((1, 8), lambda b, i: (0, 0))],
        out_specs=[_tile_spec(tq, 512), _tile_spec(tq, 8)],
        out_shape=[_sds((B, L, 512), BF16), _sds((B, L, 8), F32)],
        scratch_shapes=[pltpu.VMEM((L, 512), BF16), pltpu.VMEM((L, 512), BF16)],
        compiler_params=_params(("parallel", "arbitrary")),
    )(q, k, v, sink)


def _swa_bwd(q, k, v, sink, lse, do, n, tq):
    B, L, _ = q.shape

    def body(q_ref, k_ref, v_ref, sink_ref, lse_ref, do_ref, dq_ref, dk_ref, dv_ref, dsink_ref):
        i = pl.program_id(1)

        @pl.when(i == 0)
        def _():
            dk_ref[...] = jnp.zeros_like(dk_ref)
            dv_ref[...] = jnp.zeros_like(dv_ref)
            dsink_ref[...] = jnp.zeros_like(dsink_ref)

        lo, W = _swa_window(i, tq, n)
        mask = _swa_mask(i, lo, tq, W, n)
        kl, vl = k_ref[pl.ds(lo, W), :], v_ref[pl.ds(lo, W), :]
        kc, vc = k_ref[n:L, :], v_ref[n:L, :]
        for h in range(8):
            g = h // 4
            cols = slice(64 * g, 64 * g + 64)
            qh = q_ref[:, 64 * h:64 * h + 64]
            lse_h = lse_ref[:, h:h + 1]
            s1 = jnp.where(mask, _dot_nt(qh, kl[:, cols]) * SWA_SCALE, NEG_INF)
            s2 = _dot_nt(qh, kc[:, cols]) * SWA_SCALE
            p1, p2 = jnp.exp(s1 - lse_h), jnp.exp(s2 - lse_h)
            ps = jnp.exp(sink_ref[0:1, h:h + 1] - lse_h)
            doh = do_ref[:, 64 * h:64 * h + 64].astype(BF16)
            dp1, dp2 = _dot_nt(doh, vl[:, cols]), _dot_nt(doh, vc[:, cols])
            delta = jnp.sum(p1 * dp1, axis=1, keepdims=True) + jnp.sum(p2 * dp2, axis=1, keepdims=True)
            ds1 = (p1 * (dp1 - delta) * SWA_SCALE).astype(BF16)
            ds2 = (p2 * (dp2 - delta) * SWA_SCALE).astype(BF16)
            dq_ref[:, 64 * h:64 * h + 64] = _dot(ds1, kl[:, cols]) + _dot(ds2, kc[:, cols])
            dk_ref[pl.ds(lo, W), cols] += _dot_tn(ds1, qh)
            dk_ref[n:L, cols] += _dot_tn(ds2, qh)
            dv_ref[pl.ds(lo, W), cols] += _dot_tn(p1.astype(BF16), doh)
            dv_ref[n:L, cols] += _dot_tn(p2.astype(BF16), doh)
            dsink_ref[0:1, h:h + 1] += jnp.sum(-ps * delta, axis=0, keepdims=True)

    return _pallas(
        body, name="swa_bwd", grid=(B, L // tq),
        in_specs=[_tile_spec(tq, 512), _full_spec(L, 128), _full_spec(L, 128), pl.BlockSpec((1, 8), lambda b, i: (0, 0)),
                  _tile_spec(tq, 8), pl.BlockSpec((None, tq, 512), lambda b, i: (b, i, 1))],
        out_specs=[_tile_spec(tq, 512), _full_spec(L, 128), _full_spec(L, 128),
                   pl.BlockSpec((None, 1, 8), lambda b, i: (b, 0, 0))],
        out_shape=[_sds((B, L, 512), F32), _sds((B, L, 128), F32), _sds((B, L, 128), F32), _sds((B, 1, 8), F32)],
        compiler_params=_params(("parallel", "arbitrary")),
    )(q, k, v, sink, lse, do)


def _ret_decay(i, tq, L, n, ctx_tile):
    qi = i * tq + lax.broadcasted_iota(jnp.int32, (tq, 1), 0)
    kc = lax.broadcasted_iota(jnp.int32, (1, L - n), 1)
    d_hb = (n + kc - qi).astype(F32)
    if ctx_tile:
        return None, ((qi - L) - (kc - (L - n))).astype(F32), d_hb
    d_x = (qi - lax.broadcasted_iota(jnp.int32, (1, n), 1)).astype(F32)
    return d_x, (qi - (kc - (L - n))).astype(F32), d_hb


def _ret_head(a, h):
    lane = lax.broadcasted_iota(jnp.int32, (1, a.shape[1]), 1)
    return jnp.where(lane // 64 == h, a, jnp.zeros_like(a))


def _ret_mask(dist, lg):
    return jnp.where(dist >= 0.0, jnp.exp(lg * jnp.maximum(dist, 0.0)), 0.0)


def _ret_weights(n, L, d_x, d_hf, d_hb, lg_f, lg_b, with_grad):
    m_f, m_b = _ret_mask(d_hf, lg_f), _ret_mask(d_hb, lg_b)
    ctx = (slice(n, L), m_f + m_b)
    if with_grad:
        ctx += (m_f * jnp.maximum(d_hf, 0.0), m_b * jnp.maximum(d_hb, 0.0))
    if d_x is None:
        return [ctx]
    e_x = jnp.exp(jnp.where(d_x >= 0.0, lg_f, -lg_b) * d_x)
    lat = (slice(0, n), jnp.where(d_x == 0.0, 2.0 * e_x, e_x))
    if with_grad:
        lat += (e_x * jnp.maximum(d_x, 0.0), e_x * jnp.maximum(-d_x, 0.0))
    return [lat, ctx]


def _ret_fwd(q, k, v, lg, n, tq):
    B, L, _ = q.shape

    def body(q_ref, k_ref, v_ref, lg_ref, y_ref):
        i = pl.program_id(1)

        def tile(ctx_tile):
            dist = _ret_decay(i, tq, L, n, ctx_tile)
            for h in range(4):
                qh = _ret_head(q_ref[...], h)
                y = None
                for rows, dec_r in _ret_weights(n, L, *dist, lg_ref[0:1, h:h + 1], lg_ref[1:2, h:h + 1], False):
                    a = _dot_nt(qh, k_ref[rows, :]) * dec_r
                    part = _dot(a.astype(BF16), v_ref[rows, 128 * h:128 * h + 128])
                    y = part if y is None else y + part
                y_ref[:, 128 * h:128 * h + 128] = y

        pl.when(i < n // tq)(functools.partial(tile, False))
        pl.when(i >= n // tq)(functools.partial(tile, True))

    return _pallas(
        body, name="ret_fwd", grid=(B, L // tq),
        in_specs=[_tile_spec(tq, 256), _full_spec(L, 256), _full_spec(L, 512), pl.BlockSpec((2, 4), lambda b, i: (0, 0))],
        out_specs=_tile_spec(tq, 512), out_shape=_sds((B, L, 512), F32),
        compiler_params=_params(("parallel", "arbitrary")),
    )(q, k, v, lg)


def _ret_bwd(q, k, v, lg, dy, n, tq):
    B, L, _ = q.shape

    def body(q_ref, k_ref, v_ref, lg_ref, dy_ref, dq_ref, dk_ref, dv_ref, dlg_ref):
        i = pl.program_id(1)

        @pl.when(i == 0)
        def _():
            dk_ref[...] = jnp.zeros_like(dk_ref)
            dv_ref[...] = jnp.zeros_like(dv_ref)
            dlg_ref[...] = jnp.zeros_like(dlg_ref)

        def total(a):
            return jnp.sum(jnp.sum(a, axis=1, keepdims=True), axis=0, keepdims=True)

        def tile(ctx_tile):
            dist = _ret_decay(i, tq, L, n, ctx_tile)
            dq = None
            for h in range(4):
                vc = slice(128 * h, 128 * h + 128)
                qh = _ret_head(q_ref[...], h)
                dyh = dy_ref[:, vc].astype(BF16)
                dqh = None
                for rows, dec_r, wf_r, wb_r in _ret_weights(n, L, *dist, lg_ref[0:1, h:h + 1], lg_ref[1:2, h:h + 1], True):
                    k_all, vh = k_ref[rows, :], v_ref[rows, vc]
                    s = _dot_nt(qh, k_all)
                    gr = _dot_nt(dyh, vh)
                    ds = (gr * dec_r).astype(BF16)
                    part = _dot(ds, k_all)
                    dqh = part if dqh is None else dqh + part
                    dk_ref[rows, :] += _dot_tn(ds, qh)
                    dv_ref[rows, vc] += _dot_tn((s * dec_r).astype(BF16), dyh)
                    gs = gr * s
                    dlg_ref[0:1, h:h + 1] += total(gs * wf_r)
                    dlg_ref[1:2, h:h + 1] += total(gs * wb_r)
                dqh = _ret_head(dqh, h)
                dq = dqh if dq is None else dq + dqh
            dq_ref[...] = dq

        pl.when(i < n // tq)(functools.partial(tile, False))
        pl.when(i >= n // tq)(functools.partial(tile, True))

    return _pallas(
        body, name="ret_bwd", grid=(B, L // tq),
        in_specs=[_tile_spec(tq, 256), _full_spec(L, 256), _full_spec(L, 512), pl.BlockSpec((2, 4), lambda b, i: (0, 0)),
                  _tile_spec(tq, 512)],
        out_specs=[_tile_spec(tq, 256), _full_spec(L, 256), _full_spec(L, 512),
                   pl.BlockSpec((None, 2, 4), lambda b, i: (b, 0, 0))],
        out_shape=[_sds((B, L, 256), F32), _sds((B, L, 256), F32), _sds((B, L, 512), F32), _sds((B, 2, 4), F32)],
        compiler_params=_params(("parallel", "arbitrary")),
    )(q, k, v, lg, dy)


def _loss_head(s, target, g, n, tm):
    B, L, D = s.shape
    nx = n // tm

    def body(s_ref, t_ref, g_ref, ds_ref, dg_ref, loss_ref):
        b, i = pl.program_id(0), pl.program_id(1)

        @pl.when((b == 0) & (i == 0))
        def _():
            dg_ref[...] = jnp.zeros_like(dg_ref)
            loss_ref[...] = jnp.zeros_like(loss_ref)

        @pl.when(i < nx)
        def _():
            y, vjp = jax.vjp(_rms, s_ref[...], g_ref[...])
            err = y - t_ref[...]
            d_s, d_g = vjp(err * (1.0 / D))
            ds_ref[...] = d_s
            dg_ref[...] += d_g
            part = jnp.sum(jnp.sum(err * err, axis=1, keepdims=True), axis=0, keepdims=True) * (0.5 / D)
            loss_ref[...] += jnp.broadcast_to(part, loss_ref.shape)

        @pl.when(i >= nx)
        def _():
            ds_ref[...] = jnp.zeros_like(ds_ref)

    return _pallas(
        body, name="loss_head", grid=(B, L // tm),
        in_specs=[pl.BlockSpec((None, tm, D), lambda b, i: (b, i, 0)),
                  pl.BlockSpec((None, tm, D), lambda b, i: (b, jnp.minimum(i, nx - 1), 0)),
                  pl.BlockSpec((1, D), lambda b, i: (0, 0))],
        out_specs=[pl.BlockSpec((None, tm, D), lambda b, i: (b, i, 0)), pl.BlockSpec((1, D), lambda b, i: (0, 0)),
                   pl.BlockSpec((1, 128), lambda b, i: (0, 0))],
        out_shape=[_sds((B, L, D), F32), _sds((1, D), F32), _sds((1, 128), F32)],
        compiler_params=_params(("arbitrary", "arbitrary")),
    )(s, target, g)


def _ada_fwd(c_all, ada_w, ada_b):
    NL, D, Ns = ada_w.shape
    R = c_all.shape[0]

    def body(c_ref, w_ref, b_ref, o_ref):
        cond = jax.nn.silu(c_ref[...]).astype(BF16)
        o_ref[...] = _dot(cond, w_ref[...].astype(BF16)) + b_ref[...]

    return _pallas(
        body, name="ada_fwd", grid=(NL,),
        in_specs=[pl.BlockSpec((R, D), lambda l: (0, 0)), pl.BlockSpec((None, D, Ns), lambda l: (l, 0, 0)),
                  pl.BlockSpec((None, 1, Ns), lambda l: (l, 0, 0))],
        out_specs=pl.BlockSpec((None, R, Ns), lambda l: (l, 0, 0)), out_shape=_sds((NL, R, Ns), F32),
        compiler_params=_params(("parallel",)),
    )(c_all, ada_w, ada_b)


def _ada_bwd(c_all, ada_w, dmods):
    NL, D, Ns = ada_w.shape
    R = c_all.shape[0]

    def body(c_ref, w_ref, dm_ref, dw_ref, dc_ref):
        cond = jax.nn.silu(c_ref[...]).astype(BF16)
        dm = dm_ref[...].astype(BF16)
        dw_ref[...] = _dot_tn(cond, dm)
        dc_ref[...] = _dot_nt(dm, w_ref[...].astype(BF16))

    return _pallas(
        body, name="ada_bwd", grid=(NL,),
        in_specs=[pl.BlockSpec((R, D), lambda l: (0, 0)), pl.BlockSpec((None, D, Ns), lambda l: (l, 0, 0)),
                  pl.BlockSpec((None, R, Ns), lambda l: (l, 0, 0))],
        out_specs=[pl.BlockSpec((None, D, Ns), lambda l: (l, 0, 0)), pl.BlockSpec((None, R, D), lambda l: (l, 0, 0))],
        out_shape=[_sds((NL, D, Ns), F32), _sds((NL, R, D), F32)],
        compiler_params=_params(("parallel",)),
    )(c_all, ada_w, dmods)


def _my_index():
    return 4 * lax.axis_index("x") + 2 * lax.axis_index("y") + lax.axis_index("c")


def _peer(k):
    x, y, c = lax.axis_index("x"), lax.axis_index("y"), lax.axis_index("c")
    kx, ky, kc = (k >> 2) & 1, (k >> 1) & 1, k & 1
    px, py, pc = (x + kx) % 2, (y + ky) % 2, (c + kc) % 2
    return (px, py, pc), 4 * px + 2 * py + pc


def _all_gather(name, shards):
    na = len(shards)
    hbm = pl.BlockSpec(memory_space=pl.ANY)

    def body(*refs):
        in_refs, out_refs = refs[:na], refs[na:2 * na]
        send_sems, recv_sems, local_sems = refs[2 * na:]
        me = _my_index()
        sib_id, sib = _peer(1)
        chips = [_peer(k) for k in (4, 2, 6)]
        sib_chips = [4 * px + 2 * py + (1 - pc) for (px, py, pc), _ in chips]

        def copy(a, k, slot, to, src=None):
            dst = out_refs[a].at[slot]
            return pltpu.make_async_remote_copy(
                src_ref=dst if src is None else src, dst_ref=dst, send_sem=send_sems.at[a, k],
                recv_sem=recv_sems.at[a, k], device_id=to, device_id_type=MESH)

        first, passed, mine = [], [], []
        for a in range(na):
            cp = pltpu.make_async_copy(in_refs[a], out_refs[a].at[me], local_sems.at[a])
            cp.start()
            mine.append(cp)
            first.append(copy(a, 0, me, sib_id, src=in_refs[a]))
            first += [copy(a, 1 + j, me, pid, src=in_refs[a]) for j, (pid, _) in enumerate(chips)]
        for cp in first:
            cp.start()
        for a in range(na):
            for j, (pid, pidx) in enumerate(chips):
                copy(a, 1 + j, pidx, pid).wait_recv()
                fwd = copy(a, 4 + j, pidx, sib_id)
                fwd.start()
                passed.append(fwd)
        for a in range(na):
            copy(a, 0, sib, sib_id).wait_recv()
            for j in range(3):
                copy(a, 4 + j, sib_chips[j], sib_id).wait_recv()
        for cp in first + passed:
            cp.wait_send()
        for cp in mine:
            cp.wait()

    return _pallas(
        body, name=name, in_specs=[hbm] * na, out_specs=[hbm] * na,
        out_shape=[_sds((N_DEV,) + s.shape, s.dtype) for s in shards],
        scratch_shapes=[pltpu.SemaphoreType.DMA((na, 7)), pltpu.SemaphoreType.DMA((na, 7)),
                        pltpu.SemaphoreType.DMA((na,))],
    )(*shards)


def _all_to_all(name, parts):
    na = len(parts)
    hbm = pl.BlockSpec(memory_space=pl.ANY)

    def body(*refs):
        in_refs, out_refs = refs[:na], refs[na:2 * na]
        send_sems, recv_sems, local_sems = refs[2 * na:]
        me = _my_index()
        copies = []
        for a in range(na):
            cp = pltpu.make_async_copy(in_refs[a].at[me], out_refs[a].at[me], local_sems.at[a])
            cp.start()
            copies.append(cp)
            for k in range(1, N_DEV):
                pid, pidx = _peer(k)
                cp = pltpu.make_async_remote_copy(
                    src_ref=in_refs[a].at[pidx], dst_ref=out_refs[a].at[me], send_sem=send_sems.at[a, k - 1],
                    recv_sem=recv_sems.at[a, k - 1], device_id=pid, device_id_type=MESH)
                cp.start()
                copies.append(cp)
        for cp in copies:
            cp.wait()

    return _pallas(
        body, name=name, in_specs=[hbm] * na, out_specs=[hbm] * na,
        out_shape=[_sds(p.shape, p.dtype) for p in parts],
        scratch_shapes=[pltpu.SemaphoreType.DMA((na, 7)), pltpu.SemaphoreType.DMA((na, 7)),
                        pltpu.SemaphoreType.DMA((na,))],
    )(*parts)


_HBM = pl.BlockSpec(memory_space=pltpu.HBM)
_SEM = pl.BlockSpec(memory_space=pltpu.SEMAPHORE)
_DATAFLOW = pltpu.SideEffectType.DATAFLOW_SIDE_EFFECTING


def _exchange_start(name, srcs, slotted, after=()):
    na = len(srcs)
    lands = [lax.empty((N_DEV,) + (s.shape[1:] if slotted else s.shape), s.dtype) for s in srcs]

    def body(*refs):
        src_refs, land_refs = refs[:na], refs[na:2 * na]
        outs = refs[2 * na + len(after):]
        send_sems, recv_sems, token = outs[:na], outs[na:2 * na], outs[4 * na]
        me = _my_index()
        for a in range(na):
            for k in range(1, N_DEV):
                pid, pidx = _peer(k)
                pltpu.make_async_remote_copy(
                    src_ref=src_refs[a].at[pidx] if slotted else src_refs[a], dst_ref=land_refs[a].at[me],
                    send_sem=send_sems[a], recv_sem=recv_sems[a], device_id=pid, device_id_type=MESH).start()
        token[...] = jnp.zeros_like(token)

    ops = [pltpu.with_memory_space_constraint(a, pltpu.HBM) for a in list(srcs) + lands]
    outs = _pallas(
        body, name=name,
        out_shape=[pltpu.SemaphoreType.DMA(())] * (2 * na) + [pltpu.HBM(a.shape, a.dtype) for a in ops]
        + [_sds((8, 128), F32)],
        in_specs=[_HBM] * (2 * na) + [pl.BlockSpec(memory_space=pl.ANY)] * len(after),
        out_specs=[_SEM] * (2 * na) + [_HBM] * (2 * na) + [pl.BlockSpec(memory_space=pltpu.VMEM)],
        input_output_aliases={a: 2 * na + a for a in range(2 * na)},
        compiler_params=pltpu.CompilerParams(has_side_effects=_DATAFLOW),
    )(*ops, *after)
    return (na, outs[:2 * na], outs[2 * na:4 * na]), outs[4 * na]


def _exchange_wait(name, handle, after):
    na, sems, thru = handle

    def body(*refs):
        land_refs = refs[na:2 * na]
        send_sems, recv_sems = refs[2 * na:3 * na], refs[3 * na:4 * na]
        me_id = (lax.axis_index("x"), lax.axis_index("y"), lax.axis_index("c"))
        for a in range(na):
            seven = land_refs[a].at[pl.ds(0, N_DEV - 1)]
            drain = pltpu.make_async_remote_copy(src_ref=seven, dst_ref=seven, send_sem=send_sems[a],
                                                 recv_sem=recv_sems[a], device_id=me_id, device_id_type=MESH)
            drain.wait_send()
            drain.wait_recv()

    outs = _pallas(
        body, name=name, out_shape=[pltpu.HBM(a.shape, a.dtype) for a in thru],
        in_specs=[_HBM] * (2 * na) + [_SEM] * (2 * na) + [pl.BlockSpec(memory_space=pl.ANY)],
        out_specs=[_HBM] * (2 * na), input_output_aliases={a: a for a in range(2 * na)},
        compiler_params=pltpu.CompilerParams(has_side_effects=_DATAFLOW),
    )(*thru, *sems, after)
    return outs[na:]


def _fill_own(name, landed, owns, slotted):
    na = len(landed)
    hbm = pl.BlockSpec(memory_space=pl.ANY)

    def body(*refs):
        own_refs, land_refs, sems = refs[na:2 * na], refs[2 * na:3 * na], refs[3 * na]
        me = _my_index()
        copies = [pltpu.make_async_copy(own_refs[a].at[me] if slotted else own_refs[a], land_refs[a].at[me], sems.at[a])
                  for a in range(na)]
        for cp in copies:
            cp.start()
        for cp in copies:
            cp.wait()

    return _pallas(
        body, name=name, in_specs=[hbm] * (2 * na), out_specs=[hbm] * na,
        out_shape=[_sds(a.shape, a.dtype) for a in landed], input_output_aliases={a: a for a in range(na)},
        scratch_shapes=[pltpu.SemaphoreType.DMA((na,))],
    )(*landed, *owns)


def _rope_tables(n, L, width):
    t = jnp.arange(n)
    row = (t // GRID_W).astype(F32)
    col = (t % GRID_W).astype(F32)
    n_freq = ROPE_DIM // 4
    freqs = ROPE_THETA ** (-jnp.arange(n_freq, dtype=F32) / n_freq)
    ang = jnp.concatenate([row[:, None] * freqs, col[:, None] * freqs], axis=-1)
    cos, sin = jnp.cos(ang), jnp.sin(ang)
    cos2 = jnp.concatenate([cos, cos], axis=-1)
    sin2 = jnp.concatenate([-sin, sin], axis=-1)
    cos2 = jnp.concatenate([cos2, jnp.ones((L - n, ROPE_DIM), F32)], axis=0)
    sin2 = jnp.concatenate([sin2, jnp.zeros((L - n, ROPE_DIM), F32)], axis=0)
    reps = width // ROPE_DIM
    return jnp.tile(cos2, (1, reps)), jnp.tile(sin2, (1, reps))


def _ab_perm(w):
    return jnp.concatenate([w[:, 0:256], w[:, 320:1600], w[:, 256:320], w[:, 256:320]], axis=1)


def _ab_unperm(g):
    rope_key = (g[:, 1536:1600].astype(F32) + g[:, 1600:1664].astype(F32)).astype(g.dtype)
    return jnp.concatenate([g[:, 0:256], rope_key, g[:, 256:1536]], axis=1)


def _wq_perm(w):
    return jnp.concatenate([w[:, 192 * h:192 * h + 128] for h in range(4)]
                           + [w[:, 192 * h + 128:192 * h + 192] for h in range(4)], axis=1)


def _wq_unperm(g):
    return jnp.concatenate([g[:, sl] for h in range(4)
                            for sl in (slice(128 * h, 128 * h + 128), slice(512 + 64 * h, 576 + 64 * h))], axis=1)


def _flat(a):
    return a.reshape(-1, a.shape[-1])


def _layer_weights(full, p):
    NL, NE, NO = len(full["ffn_in"]), len(full["ab_in"]), len(full["cd_in"])
    groups = range(4)

    def each(f, mats):
        return [None if w is None else f(w) for w in mats]

    return dict(
        norm_mix=[p["norm_mix"][l][None] for l in range(NL)], norm_ffn=[p["norm_ffn"][l][None] for l in range(NL)],
        norm_final=p["norm_final"][None],
        ffn_in=each(_ffn_interleave, full["ffn_in"]), ffn_out=list(full["ffn_out"]),
        ab_in=each(_ab_perm, full["ab_in"]), ab_out=list(full["ab_out"]),
        wq=each(_wq_perm, full["mla_wq_b"]), wkv=list(full["mla_wkv_b"]),
        kv_norm=[p["mla_kv_norm"][j][None] for j in range(NE)], q_norm=[p["mla_q_norm"][j][None] for j in range(NE)],
        v_norm=[[p["cmlp_v_norm"][j][None, 128 * g:128 * g + 128] for g in groups] for j in range(NE)],
        ws=[[p["cmlp_ws"][j, g] for g in groups] for j in range(NE)],
        bs=[[p["cmlp_bs"][j, g][:, None] for g in groups] for j in range(NE)],
        cd_in=list(full["cd_in"]), cd_out=list(full["cd_out"]),
        lg=[jnp.stack([jax.nn.log_sigmoid(p["ret_decay_fwd"][j]), jax.nn.log_sigmoid(p["ret_decay_bwd"][j])])
            for j in range(NO)],
        sink=[p["swa_sink"][j][None] for j in range(NO)],
        ret_norm=[[p["ret_norm"][j][None, 128 * g:128 * g + 128] for g in groups] for j in range(NO)],
    )


def _local_step(x, ctx, target, mods, W, later_weights=None, early_grads=None):
    B, n, D = x.shape
    m = ctx.shape[1]
    L = n + m
    NL = mods.shape[0]
    tm = min(256, m)
    tq = min(256, m)
    cos512, sin512 = _rope_tables(n, L, 512)
    s = jnp.concatenate([x, ctx], axis=1)
    saved = []

    def rows(name, f, arrays, pieces, samp, samp_pieces, glob, out_arrays, out_pieces, tile=tm):
        return _Rows(name, f, B, L, n, tile, arrays, pieces, samp, samp_pieces, glob, out_arrays, out_pieces)

    def full(width, start=0):
        return [(0, start, width)]

    for l in range(NL):
        j = l // 2
        even = l % 2 == 0
        md = mods[l]
        if l == 1 and later_weights is not None:
            W = later_weights(1, s)
        r = {}
        if l == 0:
            r["pre1"] = rows("pre_mix0", _f_pre, [s], full(D), [md], [(0, 0), (0, 1)], [W["norm_mix"][0]], [(D, BF16)], full(D))
            (xn,) = r["pre1"].fwd()
        r["xn"] = xn
        if even:
            z = _mm(f"ab_in{l}", _flat(xn), W["ab_in"][j]).reshape(B, L, 1664)
            pieces = [(0, 0, 256), (0, 256, 256), (0, 1536, 128)]
            pieces += [(0, 512 + 128 * g, 128) for g in range(4)] + [(0, 1024 + 128 * g, 128) for g in range(4)]
            pieces += [(1, 0, 128), (2, 0, 128)]
            glob = [W["kv_norm"][j], W["q_norm"][j]] + W["v_norm"][j] + W["ws"][j] + W["bs"][j]
            abp = rows(f"ab_pre{l}", _f_ab_pre, [z, cos512, sin512], pieces, [], [], glob,
                       [(256, BF16), (256, BF16), (128, BF16), (512, BF16)],
                       [(0, 0, 256), (1, 0, 256), (2, 0, 128)] + [(3, 128 * g, 128) for g in range(4)], tile=128)
            kvn, qn, kpe, cm = abp.fwd()
            kv = _mm(f"wkv{l}", _flat(kvn), W["wkv"][j], out_dtype=BF16).reshape(B, L, 1024)
            q0 = _mm(f"wq{l}", _flat(qn), W["wq"][j]).reshape(B, L, 768)
            qrp = rows(f"q_rope{l}", _f_q_rope, [q0, cos512, sin512], [(0, 0, 512), (0, 512, 256), (1, 0, 256), (2, 0, 256)],
                       [], [], [], [(768, BF16)], [(0, 0, 512), (0, 512, 256)])
            (q,) = qrp.fwd()
            o, lse = _mla_fwd(q, kv, kpe, n, tq)
            y = _mm(f"ab_out_a{l}", _flat(o), W["ab_out"][j][:512])
            y = _mm(f"ab_out_b{l}", _flat(cm), W["ab_out"][j][512:], add=y).reshape(B, L, D)
            r.update(z=z, abp=abp, kvn=kvn, qn=qn, kpe=kpe, cm=cm, kv=kv, qrp=qrp, q=q, o=o, lse=lse)
        else:
            z = _mm(f"cd_in{l}", _flat(xn), W["cd_in"][j]).reshape(B, L, 2304)
            pieces = [(0, 0, 256), (0, 256, 512), (0, 768, 128), (0, 896, 128), (0, 1024, 256), (0, 1280, 512), (0, 1792, 512)]
            pieces += [(1, 0, 256), (2, 0, 256), (1, 0, 128), (2, 0, 128), (1, 0, 512), (2, 0, 512)]
            cdp = rows(f"cd_pre{l}", _f_cd_pre, [z, cos512, sin512], pieces, [], [], [],
                       [(256, BF16), (256, BF16), (512, BF16), (512, BF16), (128, BF16), (128, BF16), (512, F32)],
                       [(k_, 0, w_) for k_, w_ in enumerate((256, 256, 512, 512, 128, 128, 512))])
            rq, rk, rv, sq, sk, sv, rg = cdp.fwd()
            yret = _ret_fwd(rq, rk, rv, W["lg"][j], n, tq)
            osw, lse = _swa_fwd(sq, sk, sv, W["sink"][j], n, tq)
            mrg = rows(f"cd_merge{l}", _f_cd_merge, [yret, rg],
                       [(0, 128 * g, 128) for g in range(4)] + [(1, 128 * g, 128) for g in range(4)], [], [],
                       W["ret_norm"][j], [(512, BF16)], [(0, 128 * g, 128) for g in range(4)])
            (yr,) = mrg.fwd()
            y = _mm(f"cd_out_a{l}", _flat(yr), W["cd_out"][j][:512])
            y = _mm(f"cd_out_b{l}", _flat(osw), W["cd_out"][j][512:], add=y).reshape(B, L, D)
            r.update(z=z, cdp=cdp, rq=rq, rk=rk, rv=rv, sq=sq, sk=sk, sv=sv, rg=rg, yret=yret, osw=osw, lse=lse,
                     mrg=mrg, yr=yr)
        two, outs2 = [(0, 0, D), (1, 0, D)], [(D, F32), (D, BF16)]
        r["mix_out"] = rows(f"mix_res_pre{l}", _f_res_pre, [s, y], two, [md], [(0, 2), (0, 3), (0, 4)],
                            [W["norm_ffn"][l]], outs2, two)
        s1, xn2 = r["mix_out"].fwd()
        if l == 0 and later_weights is not None:
            W = later_weights(0, s1)
        y2 = _ffn_fwd(f"ffn{l}", _flat(xn2), W["ffn_in"][l], W["ffn_out"][l]).reshape(B, L, D)
        if l < NL - 1:
            r["ffn_out"] = rows(f"ffn_res_pre{l}", _f_res_pre, [s1, y2], two, [md, mods[l + 1]], [(0, 5), (1, 0), (1, 1)],
                                [W["norm_mix"][l + 1]], outs2, two)
            s, xn = r["ffn_out"].fwd()
        else:
            r["ffn_out"] = rows(f"res_ffn{l}", _f_res, [s1, y2], two, [md], [(0, 5)], [], [(D, F32)], full(D))
            (s,) = r["ffn_out"].fwd()
        r["xn2"] = xn2
        saved.append(r)

    ds, d_norm_final, loss = _loss_head(s, target, W["norm_final"], n, tm)

    G = {k: [None] * len(v) for k, v in W.items() if isinstance(v, list)}
    G["norm_final"] = d_norm_final
    dm = [[None] * 6 for _ in range(NL)]
    dxn_next = None
    for l in reversed(range(NL)):
        j = l // 2
        even = l % 2 == 0
        r = saved[l]
        zero = early_grads(0, G) if l == 0 and early_grads is not None else None
        if l == NL - 1:
            (ds1, dy2), (dm[l][5],), _ = r["ffn_out"].bwd([ds], {0: F32, 1: BF16}, unread=(0,))
        else:
            smp = None if zero is None else [mods[l] + zero, mods[l + 1] + zero]
            (ds1, dy2), (dm[l][5], dm[l + 1][0], dm[l + 1][1]), (G["norm_mix"][l + 1],) = r["ffn_out"].bwd(
                [ds, dxn_next], {0: F32, 1: BF16}, grad_glob=(0,), samp=smp)
        dy2f, xn2f = _flat(dy2), _flat(r["xn2"])
        hid, dz2 = _ffn_mid_bwd(f"ffn_mid{l}", xn2f, dy2f, W["ffn_in"][l], W["ffn_out"][l])
        G["ffn_out"][l] = _mm(f"g_ffn_out{l}", hid, dy2f, ta=True, out_dtype=BF16)
        G["ffn_in"][l] = _mm(f"g_ffn_in{l}", xn2f, dz2, ta=True, out_dtype=BF16)
        dxn2 = _mm(f"d_xn2{l}", dz2, W["ffn_in"][l], tb=True).reshape(B, L, D)
        zero = early_grads(1, G) if l == 0 and early_grads is not None else None
        smp = None if zero is None else [mods[l] + zero]
        (ds0, dy), (dm[l][2], dm[l][3], dm[l][4]), (G["norm_ffn"][l],) = r["mix_out"].bwd(
            [ds1, dxn2], {0: F32, 1: BF16}, grad_glob=(0,), samp=smp)
        dyf = _flat(dy)
        if even:
            w_out = W["ab_out"][j]
            dcat = _mm(f"d_cat{l}", dyf, w_out, tb=True).reshape(B, L, -1)
            G["ab_out"][l // 2] = jnp.concatenate(
                [_mm(f"g_ab_out_a{l}", _flat(r["o"]), dyf, ta=True, out_dtype=BF16),
                 _mm(f"g_ab_out_b{l}", _flat(r["cm"]), dyf, ta=True, out_dtype=BF16)], axis=0)
            dq, dkv, dkpe = _mla_bwd(r["q"], r["kv"], r["kpe"], r["lse"], dcat, n, tq)
            (dq0,), _, _ = r["qrp"].bwd([dq], {0: BF16}, unread=(0,))
            dq0f, dkvf = _flat(dq0), _flat(dkv)
            G["wq"][j] = _mm(f"g_wq{l}", _flat(r["qn"]), dq0f, ta=True, out_dtype=BF16)
            G["wkv"][j] = _mm(f"g_wkv{l}", _flat(r["kvn"]), dkvf, ta=True, out_dtype=BF16)
            dqn = _mm(f"d_qn{l}", dq0f, W["wq"][j], tb=True).reshape(B, L, 256)
            dkvn = _mm(f"d_kvn{l}", dkvf, W["wkv"][j], tb=True).reshape(B, L, 256)
            (dz,), _, gg = r["abp"].bwd([dkvn, dqn, dkpe, (dcat, 512)], {0: BF16}, grad_glob=tuple(range(14)))
            G["kv_norm"][j], G["q_norm"][j] = gg[0], gg[1]
            G["v_norm"][j], G["ws"][j], G["bs"][j] = list(gg[2:6]), list(gg[6:10]), list(gg[10:14])
            w_in, key = W["ab_in"][j], "ab_in"
        else:
            w_out = W["cd_out"][j]
            dcat = _mm(f"d_cat{l}", dyf, w_out, tb=True).reshape(B, L, -1)
            G["cd_out"][j] = jnp.concatenate(
                [_mm(f"g_cd_out_a{l}", _flat(r["yr"]), dyf, ta=True, out_dtype=BF16),
                 _mm(f"g_cd_out_b{l}", _flat(r["osw"]), dyf, ta=True, out_dtype=BF16)], axis=0)
            (dyret, drg), _, gg = r["mrg"].bwd([(dcat, 0)], {0: F32, 1: F32}, grad_glob=(0, 1, 2, 3))
            G["ret_norm"][j] = list(gg)
            drq, drk, drv, dlg = _ret_bwd(r["rq"], r["rk"], r["rv"], W["lg"][j], dyret, n, tq)
            dsq, dsk, dsv, dsink = _swa_bwd(r["sq"], r["sk"], r["sv"], W["sink"][j], r["lse"], dcat, n, tq)
            G["lg"][j], G["sink"][j] = dlg, dsink
            (dz,), _, _ = r["cdp"].bwd([drq, drk, drv, dsq, dsk, dsv, drg], {0: BF16}, unread=(0,))
            w_in, key = W["cd_in"][j], "cd_in"
        dzf = _flat(dz)
        G[key][j] = _mm(f"g_{key}{l}", _flat(r["xn"]), dzf, ta=True, out_dtype=BF16)
        dxn = _mm(f"d_xn{l}", dzf, w_in, tb=True).reshape(B, L, D)
        if l == 0:
            (ds,), (dm[0][0], dm[0][1]), (G["norm_mix"][0],) = r["pre1"].bwd([dxn], {0: F32}, grad_glob=(0,),
                                                                            add={0: ds0}, samp=smp)
        else:
            ds, dxn_next = ds0, dxn
    dmods = jnp.stack([jnp.concatenate(d, axis=2) for d in dm])
    return loss, ds[:, :n], dmods, G


def kernel(x, c, ctx, c_ctx, ada_w, ada_b, norm_mix, norm_ffn, norm_final, ffn_in, ffn_out, ab_in, ab_out, mla_q_norm, mla_kv_norm, mla_wq_b, mla_wkv_b, cmlp_v_norm, cmlp_ws, cmlp_bs, cd_in, cd_out, ret_decay_fwd, ret_decay_bwd, ret_norm, swa_sink, loss_target, m_c_ctx, m_ada_w, m_ada_b, m_norm_mix, m_norm_ffn, m_norm_final, m_ffn_in, m_ffn_out, m_ab_in, m_ab_out, m_mla_q_norm, m_mla_kv_norm, m_mla_wq_b, m_mla_wkv_b, m_cmlp_v_norm, m_cmlp_ws, m_cmlp_bs, m_cd_in, m_cd_out, m_ret_decay_fwd, m_ret_decay_bwd, m_ret_norm, m_swa_sink, v_c_ctx, v_ada_w, v_ada_b, v_norm_mix, v_norm_ffn, v_norm_final, v_ffn_in, v_ffn_out, v_ab_in, v_ab_out, v_mla_q_norm, v_mla_kv_norm, v_mla_wq_b, v_mla_wkv_b, v_cmlp_v_norm, v_cmlp_ws, v_cmlp_bs, v_cd_in, v_cd_out, v_ret_decay_fwd, v_ret_decay_bwd, v_ret_norm, v_swa_sink):
    B, n, D = x.shape
    NL = ada_w.shape[0]
    NE, NO = ab_in.shape[0], cd_in.shape[0]
    me = _my_index()
    weights = dict(c_ctx=c_ctx, ada_w=ada_w, ada_b=ada_b, norm_mix=norm_mix, norm_ffn=norm_ffn, norm_final=norm_final,
                   ffn_in=ffn_in, ffn_out=ffn_out, ab_in=ab_in, ab_out=ab_out, mla_q_norm=mla_q_norm,
                   mla_kv_norm=mla_kv_norm, mla_wq_b=mla_wq_b, mla_wkv_b=mla_wkv_b, cmlp_v_norm=cmlp_v_norm,
                   cmlp_ws=cmlp_ws, cmlp_bs=cmlp_bs, cd_in=cd_in, cd_out=cd_out, ret_decay_fwd=ret_decay_fwd,
                   ret_decay_bwd=ret_decay_bwd, ret_norm=ret_norm, swa_sink=swa_sink)
    moments_m = dict(c_ctx=m_c_ctx, ada_w=m_ada_w, ada_b=m_ada_b, norm_mix=m_norm_mix, norm_ffn=m_norm_ffn,
                     norm_final=m_norm_final, ffn_in=m_ffn_in, ffn_out=m_ffn_out, ab_in=m_ab_in, ab_out=m_ab_out,
                     mla_q_norm=m_mla_q_norm, mla_kv_norm=m_mla_kv_norm, mla_wq_b=m_mla_wq_b, mla_wkv_b=m_mla_wkv_b,
                     cmlp_v_norm=m_cmlp_v_norm, cmlp_ws=m_cmlp_ws, cmlp_bs=m_cmlp_bs, cd_in=m_cd_in, cd_out=m_cd_out,
                     ret_decay_fwd=m_ret_decay_fwd, ret_decay_bwd=m_ret_decay_bwd, ret_norm=m_ret_norm,
                     swa_sink=m_swa_sink)
    moments_v = dict(c_ctx=v_c_ctx, ada_w=v_ada_w, ada_b=v_ada_b, norm_mix=v_norm_mix, norm_ffn=v_norm_ffn,
                     norm_final=v_norm_final, ffn_in=v_ffn_in, ffn_out=v_ffn_out, ab_in=v_ab_in, ab_out=v_ab_out,
                     mla_q_norm=v_mla_q_norm, mla_kv_norm=v_mla_kv_norm, mla_wq_b=v_mla_wq_b, mla_wkv_b=v_mla_wkv_b,
                     cmlp_v_norm=v_cmlp_v_norm, cmlp_ws=v_cmlp_ws, cmlp_bs=v_cmlp_bs, cd_in=v_cd_in, cd_out=v_cd_out,
                     ret_decay_fwd=v_ret_decay_fwd, ret_decay_bwd=v_ret_decay_bwd, ret_norm=v_ret_norm,
                     swa_sink=v_swa_sink)
    order = list(weights)

    Ns = ada_w.shape[2]
    (c_g,) = _all_gather("gather_c", [c])
    R = N_DEV * B + 8
    c_all = jnp.concatenate([c_g.reshape(N_DEV * B, D), jnp.broadcast_to(c_ctx[None], (8, D))], axis=0)
    ada_b_mine = lax.dynamic_slice_in_dim(ada_b, me * Ns, Ns, axis=1)[:, None, :]
    mods_shard = _ada_fwd(c_all, ada_w, ada_b_mine)
    (mods_g,) = _all_gather("gather_mods", [mods_shard])
    mods_full = jnp.transpose(mods_g, (1, 2, 0, 3)).reshape(NL, R, 6, D)
    mx = lax.dynamic_slice_in_dim(mods_full, me * B, B, axis=1)
    mh = jnp.broadcast_to(mods_full[:, N_DEV * B][:, None], (NL, B, 6, D))
    mods = jnp.stack([mx, mh], axis=2)

    big = ["ffn_in", "ffn_out", "ab_in", "ab_out", "cd_in", "cd_out", "mla_wq_b", "mla_wkv_b"]
    col_sharded = {"ffn_in", "ab_in", "cd_in", "mla_wq_b", "mla_wkv_b"}
    shards = {k: _to_bf16("cast_" + k, weights[k]) for k in big}
    first = {k: 0 if k.startswith("cd_") else 1 for k in big}
    a_keys = [k for k in big if first[k] and not k.startswith("ffn_")]
    b_keys = ["ffn_in", "ffn_out"]
    early = _all_gather("gather_wA", [shards[k][:1] for k in a_keys])
    (rn_g,) = _all_gather("gather_ret_norm", [ret_norm])
    small_first = [mods, rn_g] + list(early)
    wb_handle, wb_token = _exchange_start("gather_wB_start", [shards[k][:1] for k in b_keys], False, small_first)
    wc_handle, wc_token = _exchange_start("gather_wC_start", [shards[k][first[k]:] for k in big], False,
                                          small_first + [wb_token])
    mods = mods + (wb_token[0, 0] + wc_token[0, 0])

    def unshard(k, g):
        if k in col_sharded:
            f = jnp.transpose(g, (1, 2, 0, 3)).reshape(g.shape[1], g.shape[2], -1)
        else:
            f = jnp.transpose(g, (1, 0, 2, 3)).reshape(g.shape[1], -1, g.shape[3])
        return [f[i] for i in range(f.shape[0])]

    rn_full = jnp.transpose(rn_g, (1, 0, 2)).reshape(NO, -1)
    small_p = dict(weights, ret_norm=rn_full)
    full0 = {k: [None] * weights[k].shape[0] for k in big}
    for k, g in zip(a_keys, early):
        full0[k][:1] = unshard(k, g)

    def later_weights(stage, newest):
        if stage == 0:
            landed = _exchange_wait("gather_wB_wait", wb_handle, newest)
            landed = _fill_own("own_wB", landed, [shards[k][:1] for k in b_keys], False)
            for k, land in zip(b_keys, landed):
                full0[k][:1] = unshard(k, land)
        else:
            landed = _exchange_wait("gather_wC_wait", wc_handle, newest)
            landed = _fill_own("own_wC", landed, [shards[k][first[k]:] for k in big], False)
            for k, land in zip(big, landed):
                full0[k][first[k]:] = unshard(k, land)
        return _layer_weights(full0, small_p)

    def to_slots(k, gl):
        g = jnp.stack(gl)
        if k == "ffn_in":
            half = N_DEV // 2
            g = g.reshape(g.shape[0], 2, g.shape[2], half, -1)
            return jnp.transpose(g, (1, 3, 0, 2, 4)).reshape(N_DEV, g.shape[0], g.shape[2], -1)
        if k in col_sharded:
            return jnp.transpose(g.reshape(g.shape[0], g.shape[1], N_DEV, -1), (2, 0, 1, 3))
        return jnp.transpose(g.reshape(g.shape[0], N_DEV, -1, g.shape[2]), (1, 0, 2, 3))

    def big_grads(G):
        return dict(ffn_in=[g if g is None else _ffn_deinterleave(g) for g in G["ffn_in"]], ffn_out=G["ffn_out"],
                    ab_in=[g if g is None else _ab_unperm(g) for g in G["ab_in"]],
                    ab_out=G["ab_out"], cd_in=G["cd_in"], cd_out=G["cd_out"],
                    mla_wq_b=[g if g is None else _wq_unperm(g) for g in G["wq"]], mla_wkv_b=G["wkv"])

    sent = {}

    def early_grads(stage, G):
        parts = big_grads(G)
        if stage == 0:
            srcs = [to_slots(k, parts[k][first[k]:]) for k in big]
            handle, g_token = _exchange_start("scatter_gC_start", srcs, slotted=True)
        else:
            srcs = [to_slots(k, parts[k][:1]) for k in b_keys]
            handle, g_token = _exchange_start("scatter_gB_start", srcs, slotted=True)
        sent[stage] = (handle, srcs)
        return g_token[0, 0]

    loss_part, grad_x, dmods, G = _local_step(x, ctx, loss_target, mods, _layer_weights(full0, small_p),
                                              later_weights, early_grads)

    dmx = dmods[:, :, 0].reshape(NL, B, 6 * D)
    dmh = jnp.sum(dmods[:, :, 1], axis=1).reshape(NL, 1, 6 * D)
    (dm_g,) = _all_gather("gather_dmods", [jnp.concatenate([dmx, dmh], axis=1)])
    dmx_all = jnp.transpose(dm_g[:, :, :B], (1, 0, 2, 3)).reshape(NL, N_DEV * B, 6 * D)
    dmh_all = jnp.sum(dm_g[:, :, B], axis=0)
    dm_rows = jnp.concatenate([dmx_all, dmh_all[:, None], jnp.zeros((NL, 7, 6 * D), F32)], axis=1)
    g_ada_b = jnp.sum(dm_rows, axis=1)
    dm_mine = lax.dynamic_slice_in_dim(dm_rows, me * Ns, Ns, axis=2)
    g_ada_w, dcond = _ada_bwd(c_all, ada_w, dm_mine)
    sg = jax.nn.sigmoid(c_ctx)
    d_c_ctx_part = jnp.sum(dcond[:, N_DEV * B], axis=0) * (sg * (1.0 + c_ctx * (1.0 - sg)))

    def cat(parts):
        return jnp.concatenate([p.reshape(-1) for p in parts])

    dlg = jnp.stack([jnp.sum(G["lg"][j], axis=0) for j in range(NO)])
    sig_f, sig_b = jax.nn.sigmoid(-ret_decay_fwd), jax.nn.sigmoid(-ret_decay_bwd)
    small = dict(
        loss=loss_part[0, 0:1],
        c_ctx=d_c_ctx_part,
        norm_mix=cat(G["norm_mix"]), norm_ffn=cat(G["norm_ffn"]), norm_final=G["norm_final"].reshape(-1),
        mla_q_norm=cat(G["q_norm"]), mla_kv_norm=cat(G["kv_norm"]),
        cmlp_v_norm=cat([cat(G["v_norm"][j]) for j in range(NE)]),
        cmlp_ws=cat([jnp.stack(G["ws"][j]) for j in range(NE)]),
        cmlp_bs=cat([jnp.stack([b_[:, 0] for b_ in G["bs"][j]]) for j in range(NE)]),
        ret_decay_fwd=(dlg[:, 0] * sig_f).reshape(-1), ret_decay_bwd=(dlg[:, 1] * sig_b).reshape(-1),
        ret_norm=cat([cat(G["ret_norm"][j]) for j in range(NO)]),
        swa_sink=cat([jnp.sum(G["sink"][j], axis=0) for j in range(NO)]),
    )
    small_keys = list(small)
    sizes = [small[k].shape[0] for k in small_keys]
    total = sum(sizes)
    padded = -(-total // 2048) * 2048
    packed = jnp.concatenate([small[k] for k in small_keys] + [jnp.zeros((padded - total,), F32)]).reshape(-1, 128)
    (small_g,) = _all_gather("gather_small", [packed])

    parts = big_grads(G)
    landed0 = dict(zip(a_keys, _all_to_all("scatter_gA", [to_slots(k, parts[k][:1]) for k in a_keys])))
    landed_b = _fill_own("own_gB", _exchange_wait("scatter_gB_wait", sent[1][0], grad_x), sent[1][1], True)
    landed0.update(zip(b_keys, landed_b))
    landed_c = _fill_own("own_gC", _exchange_wait("scatter_gC_wait", sent[0][0], grad_x), sent[0][1], True)
    landed = [[landed0[k], rest] if first[k] else [rest] for k, rest in zip(big, landed_c)]

    grads, deltas, new_m, new_v = {}, {}, {}, {}
    for k, land in zip(big, landed):
        grads[k], deltas[k], new_m[k], new_v[k] = _adamw_from_slots("adamw_" + k, weights[k], moments_m[k], moments_v[k], land)
    deltas["ada_w"], new_m["ada_w"], new_v["ada_w"] = [
        o.reshape(ada_w.shape) for o in _adamw("adamw_ada_w", _flat(ada_w), _flat(g_ada_w), _flat(m_ada_w), _flat(v_ada_w))]
    grads["ada_w"] = g_ada_w

    sums_only = ("loss", "ret_norm")

    def packed_of(src, fill):
        vals = [jnp.full((sizes[i],), fill, F32) if k in sums_only else src[k].reshape(-1)
                for i, k in enumerate(small_keys)]
        return jnp.concatenate(vals + [jnp.full((padded - total,), fill, F32)]).reshape(-1, 128)

    w_p, m_p, v_p = packed_of(weights, 0.0), packed_of(moments_m, 0.0), packed_of(moments_v, 1.0)

    def f_small(w_, m_, v_, land_):
        g = _sum_slots(land_)
        return (g,) + _adamw_math(w_, g, m_, v_)

    g_p, d_p, nm_p, nv_p = _ew("adamw_small", f_small, [w_p, m_p, v_p, small_g], [F32] * 4)
    offs = np.cumsum([0] + sizes)
    for i, k in enumerate(small_keys):
        sl = slice(int(offs[i]), int(offs[i + 1]))
        if k == "loss":
            loss = g_p.reshape(-1)[int(offs[i])]
        elif k == "ret_norm":
            g_full = g_p.reshape(-1)[sl].reshape(NO, -1)
            g_mine = lax.dynamic_slice_in_dim(g_full, me * ret_norm.shape[1], ret_norm.shape[1], axis=1)
            d_, m_, v_ = _adamw("adamw_ret_norm", *[jnp.pad(a, ((0, 8 - NO), (0, 128 - a.shape[1])), constant_values=cv)
                                                     for a, cv in ((ret_norm, 0.0), (g_mine, 0.0), (m_ret_norm, 0.0), (v_ret_norm, 1.0))])
            grads[k] = g_mine
            deltas[k], new_m[k], new_v[k] = [a[:NO, :ret_norm.shape[1]] for a in (d_, m_, v_)]
        else:
            shp = weights[k].shape
            grads[k], deltas[k], new_m[k], new_v[k] = [a.reshape(-1)[sl].reshape(shp) for a in (g_p, d_p, nm_p, nv_p)]
    pad_b = lambda a, cv=0.0: jnp.pad(a, ((0, 8 - NL), (0, 0)), constant_values=cv)
    d_, m_, v_ = _adamw("adamw_ada_b", pad_b(ada_b), pad_b(g_ada_b), pad_b(m_ada_b), pad_b(v_ada_b, 1.0))
    grads["ada_b"] = g_ada_b
    deltas["ada_b"], new_m["ada_b"], new_v["ada_b"] = d_[:NL], m_[:NL], v_[:NL]

    return (loss, grad_x, *[grads[k] for k in order], *[deltas[k] for k in order],
            *[new_m[k] for k in order], *[new_v[k] for k in order])
```

```python
import functools

import numpy as np
import jax
import jax.numpy as jnp
from jax import lax
from jax.experimental import pallas as pl
from jax.experimental.pallas import tpu as pltpu

F32 = jnp.float32
BF16 = jnp.bfloat16
EPS = 1e-6
NEG_INF = -1e30
N_DEV = 8
GRID_W = 64
ROPE_THETA = 10000.0
ROPE_DIM = 64
SWA_WINDOW = 128
MLA_SCALE = (128 + 64) ** -0.5
SWA_SCALE = 64 ** -0.5
RET_K_SCALE = 64 ** -0.5
ADAM_LR, ADAM_B1, ADAM_B2, ADAM_EPS, ADAM_WD, ADAM_STEP = 0.001, 0.9, 0.999, 1e-08, 0.01, 10
V7X_VMEM_LIMIT = 56 * 1024 * 1024
MESH = pl.DeviceIdType.MESH


def _pallas(body, **kw):
    return pl.pallas_call(body, **kw)


def _params(sem=None):
    return pltpu.CompilerParams(dimension_semantics=sem, vmem_limit_bytes=V7X_VMEM_LIMIT)


def _tile(n, cap, align):
    best = None
    for t in range(align, min(n, cap) + 1, align):
        if n % t == 0:
            best = t
    return n if best is None else best


def _sds(shape, dtype):
    return jax.ShapeDtypeStruct(tuple(shape), dtype)


def _ew(name, f, ins, out_dtypes, cap_elems=131072):
    R, C = ins[0].shape[-2:]
    tr = _tile(R, max(16, cap_elems // C), 16)
    n_in = len(ins)

    def spec(a):
        if a.ndim == 2:
            return pl.BlockSpec((tr, C), lambda i: (i, 0))
        return pl.BlockSpec((a.shape[0], tr, C), lambda i: (0, i, 0))

    def body(*refs):
        outs = f(*[r[...] for r in refs[:n_in]])
        for r, o in zip(refs[n_in:], outs):
            r[...] = o.astype(r.dtype)

    return _pallas(
        body, name=name, grid=(R // tr,), in_specs=[spec(a) for a in ins],
        out_specs=[pl.BlockSpec((tr, C), lambda i: (i, 0)) for _ in out_dtypes],
        out_shape=[_sds((R, C), d) for d in out_dtypes], compiler_params=_params(("parallel",)),
    )(*ins)


def _to_bf16(name, w):
    w2 = w.reshape(-1, w.shape[-1])
    return _ew(name, lambda v: (v,), [w2], [BF16])[0].reshape(w.shape)


def _adamw_math(w, g, m, v):
    m = ADAM_B1 * m + (1.0 - ADAM_B1) * g
    v = ADAM_B2 * v + (1.0 - ADAM_B2) * (g * g)
    m_hat = m / (1.0 - ADAM_B1 ** ADAM_STEP)
    v_hat = v / (1.0 - ADAM_B2 ** ADAM_STEP)
    delta = -ADAM_LR * (m_hat / (jnp.sqrt(v_hat) + ADAM_EPS) + ADAM_WD * w)
    return delta, m, v


def _sum_slots(land):
    g = land[0].astype(F32)
    for s in range(1, land.shape[0]):
        g = g + land[s].astype(F32)
    return g


def _adamw_from_slots(name, w, m, v, lands):
    nl, K, C = w.shape
    tr = _tile(K, max(16, 65536 // C), 16)
    per = K // tr
    starts = np.cumsum([0] + [ld.shape[1] * per for ld in lands])
    ng = len(lands)

    def land_spec(g):
        lo, hi = int(starts[g]), int(starts[g + 1])
        return pl.BlockSpec((N_DEV, tr, C), lambda t: (0, jnp.clip(t, lo, hi - 1) - lo, 0))

    def body(*refs):
        w_ref, m_ref, v_ref = refs[:3]
        land_refs, out_refs = refs[3:3 + ng], refs[3 + ng:]
        t = pl.program_id(0)
        for g in range(ng):
            @pl.when((t >= int(starts[g])) & (t < int(starts[g + 1])))
            def _(g=g):
                grad = _sum_slots(land_refs[g][...])
                for r, o in zip(out_refs, (grad,) + _adamw_math(w_ref[...], grad, m_ref[...], v_ref[...])):
                    r[...] = o

    row = pl.BlockSpec((tr, C), lambda t: (t, 0))
    outs = _pallas(
        body, name=name, grid=(nl * per,), in_specs=[row] * 3 + [land_spec(g) for g in range(ng)],
        out_specs=[row] * 4, out_shape=[_sds((nl * K, C), F32)] * 4, compiler_params=_params(("parallel",)),
    )(w.reshape(-1, C), m.reshape(-1, C), v.reshape(-1, C), *[ld.reshape(N_DEV, -1, C) for ld in lands])
    return [o.reshape(w.shape) for o in outs]


def _adamw(name, w, g, m, v):
    outs = _ew(name, lambda w_, g_, m_, v_: _adamw_math(w_, g_, m_, v_), [w, g, m, v], [F32] * 3)
    return outs


def _mm(name, a, b, ta=False, tb=False, out_dtype=F32, add=None):
    M, K = (a.shape[1], a.shape[0]) if ta else a.shape
    N = b.shape[0] if tb else b.shape[1]
    tm = _tile(M, 1408, 128)
    tn = _tile(N, 1024, 128)
    if tn < 256 and N <= 2432:
        tn = N
    tk = _tile(K, 1792, 128)
    nk = K // tk
    a_spec = pl.BlockSpec((tk, tm), lambda i, j, k: (k, i)) if ta else pl.BlockSpec((tm, tk), lambda i, j, k: (i, k))
    b_spec = pl.BlockSpec((tn, tk), lambda i, j, k: (j, k)) if tb else pl.BlockSpec((tk, tn), lambda i, j, k: (k, j))
    o_spec = pl.BlockSpec((tm, tn), lambda i, j, k: (i, j))
    dims = (((0 if ta else 1,), (1 if tb else 0,)), ((), ()))
    has_add = add is not None

    def product(a_ref, b_ref):
        return lax.dot_general(a_ref[...].astype(BF16), b_ref[...].astype(BF16), dims, preferred_element_type=F32)

    def body_single(*refs):
        acc = product(refs[0], refs[1])
        if has_add:
            acc = acc + refs[2][...]
        refs[-1][...] = acc.astype(refs[-1].dtype)

    def body(*refs):
        a_ref, b_ref = refs[0], refs[1]
        add_ref = refs[2] if has_add else None
        o_ref, acc = refs[-2], refs[-1]
        k = pl.program_id(2)

        @pl.when(k == 0)
        def _():
            acc[...] = add_ref[...] if has_add else jnp.zeros_like(acc)

        acc[...] += product(a_ref, b_ref)

        @pl.when(k == nk - 1)
        def _():
            o_ref[...] = acc[...].astype(o_ref.dtype)

    ins = [a, b] + ([add] if has_add else [])
    specs = [a_spec, b_spec] + ([o_spec] if has_add else [])
    return _pallas(
        body_single if nk == 1 else body, name=name, grid=(M // tm, N // tn, nk), in_specs=specs, out_specs=o_spec,
        out_shape=_sds((M, N), out_dtype), scratch_shapes=[] if nk == 1 else [pltpu.VMEM((tm, tn), F32)],
        compiler_params=_params(("parallel", "parallel", "arbitrary")),
    )(*ins)


FFN_TILE = 256


def _ffn_interleave(w):
    D, F2 = w.shape
    nj = F2 // (2 * FFN_TILE)

    def body(a_ref, b_ref, o_ref):
        o_ref[:, :FFN_TILE] = a_ref[...]
        o_ref[:, FFN_TILE:] = b_ref[...]

    return _pallas(
        body, name="ffn_interleave", grid=(nj,),
        in_specs=[pl.BlockSpec((D, FFN_TILE), lambda j: (0, j)), pl.BlockSpec((D, FFN_TILE), lambda j: (0, j + nj))],
        out_specs=pl.BlockSpec((D, 2 * FFN_TILE), lambda j: (0, j)), out_shape=_sds((D, F2), w.dtype),
        compiler_params=_params(("parallel",)),
    )(w, w)


def _ffn_deinterleave(w):
    D, F2 = w.shape
    nj = F2 // (2 * FFN_TILE)

    def body(w_ref, o_ref):
        o_ref[0] = w_ref[:, :FFN_TILE]
        o_ref[1] = w_ref[:, FFN_TILE:]

    return _pallas(
        body, name="ffn_deinterleave", grid=(nj,), in_specs=[pl.BlockSpec((D, 2 * FFN_TILE), lambda j: (0, j))],
        out_specs=pl.BlockSpec((2, D, FFN_TILE), lambda j: (0, 0, j)), out_shape=_sds((2, D, F2 // 2), w.dtype),
        compiler_params=_params(("parallel",)),
    )(w)


def _ffn_specs(M, D, F):
    tm = _tile(M, 1024, 128)
    x_spec = pl.BlockSpec((tm, D), lambda i, j: (i, 0))
    wi_spec = pl.BlockSpec((D, 2 * FFN_TILE), lambda i, j: (0, j))
    wo_spec = pl.BlockSpec((FFN_TILE, D), lambda i, j: (j, 0))
    return tm, F // FFN_TILE, x_spec, wi_spec, wo_spec


def _ffn_fwd(name, xn, w_in, w_out):
    M, D = xn.shape
    tm, nj, x_spec, wi_spec, wo_spec = _ffn_specs(M, D, w_out.shape[0])

    def body(x_ref, wi_ref, wo_ref, y_ref):
        j = pl.program_id(1)
        z = _dot(x_ref[...], wi_ref[...])
        hid = (jax.nn.silu(z[:, :FFN_TILE]) * z[:, FFN_TILE:]).astype(BF16)
        part = _dot(hid, wo_ref[...])

        @pl.when(j == 0)
        def _():
            y_ref[...] = part

        @pl.when(j > 0)
        def _():
            y_ref[...] += part

    return _pallas(
        body, name=name, grid=(M // tm, nj), in_specs=[x_spec, wi_spec, wo_spec], out_specs=x_spec,
        out_shape=_sds((M, D), F32), compiler_params=_params(("parallel", "arbitrary")),
    )(xn, w_in, w_out)


def _ffn_mid_bwd(name, xn, dy, w_in, w_out):
    M, D = xn.shape
    F = w_out.shape[0]
    tm, nj, x_spec, wi_spec, wo_spec = _ffn_specs(M, D, F)

    def body(x_ref, dy_ref, wi_ref, wo_ref, h_ref, dz_ref):
        z = _dot(x_ref[...], wi_ref[...])
        a, b = z[:, :FFN_TILE], z[:, FFN_TILE:]
        dh = _dot_nt(dy_ref[...], wo_ref[...])
        sig = jax.nn.sigmoid(a)
        act = a * sig
        h_ref[...] = (act * b).astype(BF16)
        dz_ref[:, :FFN_TILE] = (dh * b * (sig * (1.0 + a * (1.0 - sig)))).astype(BF16)
        dz_ref[:, FFN_TILE:] = (dh * act).astype(BF16)

    return _pallas(
        body, name=name, grid=(M // tm, nj), in_specs=[x_spec, x_spec, wi_spec, wo_spec],
        out_specs=[pl.BlockSpec((tm, FFN_TILE), lambda i, j: (i, j)),
                   pl.BlockSpec((tm, 2 * FFN_TILE), lambda i, j: (i, j))],
        out_shape=[_sds((M, F), BF16), _sds((M, 2 * F), BF16)],
        compiler_params=_params(("parallel", "parallel")),
    )(xn, dy, w_in, w_out)


class _Rows:
    def __init__(self, name, f, B, L, n, tm, arrays, pieces, samp, samp_pieces, glob, out_arrays, out_pieces):
        self.name, self.f, self.B, self.L, self.n, self.tm = name, f, B, L, n, tm
        self.arrays, self.pieces, self.samp, self.samp_pieces, self.glob = arrays, pieces, samp, samp_pieces, glob
        self.out_arrays, self.out_pieces = out_arrays, out_pieces
        self.nx = n // tm
        self.grid = (B, L // tm)

    def _row_spec(self, C, batched=True):
        tm = self.tm
        if batched:
            return pl.BlockSpec((None, tm, C), lambda b, i: (b, i, 0))
        return pl.BlockSpec((tm, C), lambda b, i: (i, 0))

    def _in_specs(self, unread=()):
        nx, tm = self.nx, self.tm
        specs = [self._row_spec(a.shape[-1], a.ndim == 3) for a in self.arrays]
        for ai in unread:
            specs[ai] = pl.BlockSpec((None, tm, self.arrays[ai].shape[-1]), lambda b, i: (0, 0, 0))
        specs += [pl.BlockSpec((None, None) + s.shape[2:], lambda b, i: (b, i // nx, 0, 0)) for s in self.samp]
        specs += [pl.BlockSpec(g.shape, lambda b, i, nd=g.ndim: (0,) * nd) for g in self.glob]
        return specs

    def _load(self, a_refs, s_refs, g_refs):
        args = [a_refs[ai][:, cs:cs + cw].astype(F32) for ai, cs, cw in self.pieces]
        args += [s_refs[si][r:r + 1, :].astype(F32) for si, r in self.samp_pieces]
        args += [g[...].astype(F32) for g in g_refs]
        return args

    def fwd(self):
        na, ns, ng = len(self.arrays), len(self.samp), len(self.glob)

        def body(*refs):
            a_refs, s_refs, g_refs = refs[:na], refs[na:na + ns], refs[na + ns:na + ns + ng]
            o_refs = refs[na + ns + ng:]
            outs = self.f(*self._load(a_refs, s_refs, g_refs))
            for (oi, cs, cw), o in zip(self.out_pieces, outs):
                o_refs[oi][:, cs:cs + cw] = o.astype(o_refs[oi].dtype)

        return _pallas(
            body, name=self.name + "_fwd", grid=self.grid, in_specs=self._in_specs(),
            out_specs=[self._row_spec(C) for C, _ in self.out_arrays],
            out_shape=[_sds((self.B, self.L, C), d) for C, d in self.out_arrays],
            compiler_params=_params(("parallel", "parallel")),
        )(*self.arrays, *self.samp, *self.glob)

    def bwd(self, cts, grad_arrays, grad_glob=(), add=None, samp=None, unread=()):
        samp = self.samp if samp is None else samp
        na, ns, ng, nc = len(self.arrays), len(self.samp), len(self.glob), len(cts)
        ct_off = [c[1] if isinstance(c, tuple) else 0 for c in cts]
        cts = [c[0] if isinstance(c, tuple) else c for c in cts]
        add = add or {}
        add_keys = list(add)
        g_idx = list(grad_arrays)
        nx, B = self.nx, self.B
        n_in = na + ns + ng + nc + len(add_keys)
        n_pieces, n_sp = len(self.pieces), len(self.samp_pieces)

        def body(*refs):
            a_refs, s_refs, g_refs = refs[:na], refs[na:na + ns], refs[na + ns:na + ns + ng]
            c_refs = refs[na + ns + ng:na + ns + ng + nc]
            add_refs = refs[na + ns + ng + nc:n_in]
            d_refs = refs[n_in:n_in + len(g_idx)]
            ds_refs = refs[n_in + len(g_idx):n_in + len(g_idx) + n_sp]
            dg_refs = refs[n_in + len(g_idx) + n_sp:]
            b, i = pl.program_id(0), pl.program_id(1)
            args = self._load(a_refs, s_refs, g_refs)
            _, vjp = jax.vjp(lambda *xs: tuple(self.f(*xs)), *args)
            grads = vjp(tuple(c_refs[oi][:, ct_off[oi] + cs:ct_off[oi] + cs + cw].astype(F32)
                              for oi, cs, cw in self.out_pieces))
            for k, ai in enumerate(g_idx):
                covered = sum(cw for pa, _, cw in self.pieces if pa == ai)
                if covered < self.arrays[ai].shape[-1]:
                    d_refs[k][...] = jnp.zeros_like(d_refs[k])
                for (pa, cs, cw), gr in zip(self.pieces, grads[:n_pieces]):
                    if pa == ai:
                        if ai in add:
                            gr = gr + add_refs[add_keys.index(ai)][:, cs:cs + cw]
                        d_refs[k][:, cs:cs + cw] = gr.astype(d_refs[k].dtype)

            @pl.when((i == 0) | (i == nx))
            def _():
                for r in ds_refs:
                    r[...] = jnp.zeros_like(r)

            for r, gr in zip(ds_refs, grads[n_pieces:n_pieces + n_sp]):
                r[...] += gr

            @pl.when((b == 0) & (i == 0))
            def _():
                for r in dg_refs:
                    r[...] = jnp.zeros_like(r)

            for r, gi in zip(dg_refs, grad_glob):
                r[...] += grads[n_pieces + n_sp + gi]

        in_specs = self._in_specs(unread) + [self._row_spec(c.shape[-1]) for c in cts]
        in_specs += [self._row_spec(add[k].shape[-1]) for k in add_keys]
        out_specs = [self._row_spec(self.arrays[ai].shape[-1]) for ai in g_idx]
        out_shape = [_sds(self.arrays[ai].shape, grad_arrays[ai]) for ai in g_idx]
        for si, _ in self.samp_pieces:
            C = self.samp[si].shape[-1]
            out_specs.append(pl.BlockSpec((None, None, 1, C), lambda b, i: (b, i // nx, 0, 0)))
            out_shape.append(_sds((B, 2, 1, C), F32))
        for gi in grad_glob:
            g = self.glob[gi]
            out_specs.append(pl.BlockSpec(g.shape, lambda b, i, nd=g.ndim: (0,) * nd))
            out_shape.append(_sds(g.shape, F32))
        outs = _pallas(
            body, name=self.name + "_bwd", grid=self.grid, in_specs=in_specs, out_specs=out_specs,
            out_shape=out_shape, compiler_params=_params(("arbitrary", "arbitrary")),
        )(*self.arrays, *samp, *self.glob, *cts, *[add[k] for k in add_keys])
        ng_ = len(g_idx)
        return outs[:ng_], outs[ng_:ng_ + n_sp], outs[ng_ + n_sp:]


def _rms(x, g):
    return x * lax.rsqrt(jnp.mean(x * x, axis=-1, keepdims=True) + EPS) * g


@jax.custom_vjp
def _swap_halves(x):
    w = x.shape[-1]
    lane = lax.broadcasted_iota(jnp.int32, x.shape, x.ndim - 1)
    up = pltpu.roll(x, w - ROPE_DIM // 2, x.ndim - 1)
    down = pltpu.roll(x, ROPE_DIM // 2, x.ndim - 1)
    return jnp.where(lane % ROPE_DIM < ROPE_DIM // 2, up, down)


_swap_halves.defvjp(lambda x: (_swap_halves(x), None), lambda _, ct: (_swap_halves(ct),))


@jax.custom_vjp
def _bdot(a, b):
    return jnp.dot(a.astype(BF16), b.astype(BF16), preferred_element_type=F32)


_bdot.defvjp(lambda a, b: (_bdot(a, b), (a, b)),
             lambda res, ct: (_dot_nt(ct.astype(BF16), res[1].astype(BF16)), _dot_tn(res[0].astype(BF16), ct.astype(BF16))))


def _rope(x, cos2, sin2):
    return x * cos2 + _swap_halves(x) * sin2


def _f_pre(s, shift, scale, g):
    return (_rms(s, g) * (1.0 + scale) + shift,)


def _f_res(s, y, gate):
    return (s + gate * y,)


def _f_res_pre(s, y, gate, shift, scale, g):
    s1 = s + gate * y
    return s1, _rms(s1, g) * (1.0 + scale) + shift


def _f_ab_pre(*args):
    kv_lat, q_lat, kpe = args[0:3]
    us, vs = args[3:7], args[7:11]
    cos2, sin2 = args[11:13]
    kv_norm, q_norm = args[13:15]
    vns, wss, bss = args[15:19], args[19:23], args[23:27]
    outs = [_rms(kv_lat, kv_norm), _rms(q_lat, q_norm), _rope(kpe, cos2, sin2)]
    for u, v, vn, ws, bs in zip(us, vs, vns, wss, bss):
        vg = _rms(jax.nn.gelu(v), vn)
        mixed = _bdot(ws, vg) + bs
        outs.append(jax.nn.gelu(u) * mixed)
    return tuple(outs)


def _f_q_rope(qn, qr, cos2, sin2):
    return qn, _rope(qr, cos2, sin2)


def _f_cd_pre(rk, rv, sk, sv, rq, rg, sq, c256, s256, c128, s128, c512, s512):
    return (_rope(rq, c256, s256), _rope(rk * RET_K_SCALE, c256, s256), rv,
            _rope(sq, c512, s512), _rope(sk, c128, s128), sv, rg)


def _f_cd_merge(*args):
    ys, rgs, rns = args[0:4], args[4:8], args[8:12]
    return tuple(_rms(y, rn) * jax.nn.silu(rg) for y, rg, rn in zip(ys, rgs, rns))


def _dot_nt(a, b):
    return lax.dot_general(a, b, (((1,), (1,)), ((), ())), preferred_element_type=F32)


def _dot_tn(a, b):
    return lax.dot_general(a, b, (((0,), (0,)), ((), ())), preferred_element_type=F32)


def _dot(a, b):
    return jnp.dot(a, b, preferred_element_type=F32)


def _tile_spec(tq, C):
    return pl.BlockSpec((None, tq, C), lambda b, i: (b, i, 0))


def _full_spec(L, C):
    return pl.BlockSpec((None, L, C), lambda b, i: (b, 0, 0))


def _mla_half(h):
    lane = lax.broadcasted_iota(jnp.int32, (1, 128), 1)
    return (lane < 64) if h % 2 == 0 else (lane >= 64)


def _mla_query(q_ref, h):
    pair = q_ref[:, 512 + 128 * (h // 2):640 + 128 * (h // 2)]
    return jnp.concatenate([q_ref[:, 128 * h:128 * h + 128], jnp.where(_mla_half(h), pair, jnp.zeros_like(pair))], axis=1)


def _mla_fill_keys(kcat, kv_ref, kpe_ref):
    kp = kpe_ref[...]
    for h in range(4):
        kcat[h, :, 0:128] = kv_ref[:, 256 * h:256 * h + 128]
        kcat[h, :, 128:256] = jnp.where(_mla_half(h), kp, jnp.zeros_like(kp))


def _mla_fwd(q, kv, kpe, n, tq):
    B, L, _ = q.shape

    def body(q_ref, kv_ref, kpe_ref, o_ref, lse_ref, kcat):
        i = pl.program_id(1)

        @pl.when(i == 0)
        def _():
            _mla_fill_keys(kcat, kv_ref, kpe_ref)

        def tile(keys):
            for h in range(4):
                s = _dot_nt(_mla_query(q_ref, h), kcat[h, keys, :]) * MLA_SCALE
                m = jnp.max(s, axis=1, keepdims=True)
                e = jnp.exp(s - m)
                l = jnp.sum(e, axis=1, keepdims=True)
                p = (e * (1.0 / l)).astype(BF16)
                o_ref[:, 128 * h:128 * h + 128] = _dot(p, kv_ref[keys, 256 * h + 128:256 * h + 256]).astype(o_ref.dtype)
                lse_ref[:, h:h + 1] = m + jnp.log(l)

        pl.when(i < n // tq)(functools.partial(tile, slice(0, L)))
        pl.when(i >= n // tq)(functools.partial(tile, slice(n, L)))

    return _pallas(
        body, name="mla_fwd", grid=(B, L // tq),
        in_specs=[_tile_spec(tq, 768), _full_spec(L, 1024), _full_spec(L, 128)],
        out_specs=[_tile_spec(tq, 512), _tile_spec(tq, 4)],
        out_shape=[_sds((B, L, 512), BF16), _sds((B, L, 4), F32)],
        scratch_shapes=[pltpu.VMEM((4, L, 256), BF16)],
        compiler_params=_params(("parallel", "arbitrary")),
    )(q, kv, kpe)


def _mla_bwd(q, kv, kpe, lse, do, n, tq):
    B, L, _ = q.shape

    def body(q_ref, kv_ref, kpe_ref, lse_ref, do_ref, dq_ref, dkv_ref, dkpe_ref, kcat):
        i = pl.program_id(1)

        @pl.when(i == 0)
        def _():
            dkv_ref[...] = jnp.zeros_like(dkv_ref)
            dkpe_ref[...] = jnp.zeros_like(dkpe_ref)
            _mla_fill_keys(kcat, kv_ref, kpe_ref)

        def tile(keys):
            rope_pair = None
            for h in range(4):
                qc, kc = _mla_query(q_ref, h), kcat[h, keys, :]
                v = kv_ref[keys, 256 * h + 128:256 * h + 256]
                p = jnp.exp(_dot_nt(qc, kc) * MLA_SCALE - lse_ref[:, h:h + 1])
                doh = do_ref[:, 128 * h:128 * h + 128].astype(BF16)
                dp = _dot_nt(doh, v)
                delta = jnp.sum(p * dp, axis=1, keepdims=True)
                ds = (p * (dp - delta) * MLA_SCALE).astype(BF16)
                dqc = _dot(ds, kc)
                dq_ref[:, 128 * h:128 * h + 128] = dqc[:, 0:128]
                if h % 2 == 0:
                    rope_pair = dqc[:, 128:256]
                else:
                    dq_ref[:, 512 + 128 * (h // 2):640 + 128 * (h // 2)] = jnp.where(_mla_half(h), dqc[:, 128:256], rope_pair)
                dkc = _dot_tn(ds, qc)
                dkv_ref[keys, 256 * h:256 * h + 128] += dkc[:, 0:128]
                dkpe_ref[keys, :] += dkc[:, 128:256]
                dkv_ref[keys, 256 * h + 128:256 * h + 256] += _dot_tn(p.astype(BF16), doh)

        pl.when(i < n // tq)(functools.partial(tile, slice(0, L)))
        pl.when(i >= n // tq)(functools.partial(tile, slice(n, L)))

    return _pallas(
        body, name="mla_bwd", grid=(B, L // tq),
        in_specs=[_tile_spec(tq, 768), _full_spec(L, 1024), _full_spec(L, 128), _tile_spec(tq, 4),
                  pl.BlockSpec((None, tq, 512), lambda b, i: (b, i, 0))],
        out_specs=[_tile_spec(tq, 768), _full_spec(L, 1024), _full_spec(L, 128)],
        out_shape=[_sds((B, L, 768), F32), _sds((B, L, 1024), F32), _sds((B, L, 128), F32)],
        scratch_shapes=[pltpu.VMEM((4, L, 256), BF16)],
        compiler_params=_params(("parallel", "arbitrary")),
    )(q, kv, kpe, lse, do)


def _swa_window(i, tq, n):
    W = min(tq + 2 * SWA_WINDOW, n)
    lo = jnp.clip(i * tq - SWA_WINDOW, 0, n - W)
    return pl.multiple_of(lo, 128), W


def _swa_mask(i, lo, tq, W, n):
    qpos = i * tq + lax.broadcasted_iota(jnp.int32, (tq, 1), 0)
    kpos = lo + lax.broadcasted_iota(jnp.int32, (1, W), 1)
    return (jnp.abs(qpos - kpos) <= SWA_WINDOW) & (qpos < n)


def _swa_fill(rep, src_ref):
    for g in range(2):
        rep[:, 256 * g:256 * g + 256] = jnp.concatenate([src_ref[:, 64 * g:64 * g + 64]] * 4, axis=1)


def _swa_head_rows(a):
    lane = lax.broadcasted_iota(jnp.int32, (1, 256), 1)
    return jnp.concatenate([jnp.where(lane // 64 == h, a, jnp.zeros_like(a)) for h in range(4)], axis=0)


def _swa_fold_rows(a, tq):
    lane = lax.broadcasted_iota(jnp.int32, (1, 256), 1)
    out = jnp.where(lane // 64 == 0, a[0:tq], 0.0)
    for h in range(1, 4):
        out = jnp.where(lane // 64 == h, a[h * tq:(h + 1) * tq], out)
    return out


def _swa_fwd(q, k, v, sink, n, tq):
    B, L, _ = q.shape

    def body(q_ref, k_ref, v_ref, sink_ref, o_ref, lse_ref, krep, vrep):
        i = pl.program_id(1)

        @pl.when(i == 0)
        def _():
            _swa_fill(krep, k_ref)
            _swa_fill(vrep, v_ref)

        def tile(band):
            for g in range(2):
                gc = slice(256 * g, 256 * g + 256)
                qs = _swa_head_rows(q_ref[:, gc])
                sk = jnp.concatenate([jnp.broadcast_to(sink_ref[0:1, 4 * g + h:4 * g + h + 1], (tq, 1))
                                      for h in range(4)], axis=0)
                s2 = _dot_nt(qs, krep[n:L, gc]) * SWA_SCALE
                m = jnp.maximum(jnp.max(s2, axis=1, keepdims=True), sk)
                if band:
                    lo, W = _swa_window(i, tq, n)
                    mask = jnp.concatenate([_swa_mask(i, lo, tq, W, n)] * 4, axis=0)
                    s1 = jnp.where(mask, _dot_nt(qs, krep[pl.ds(lo, W), gc]) * SWA_SCALE, NEG_INF)
                    m = jnp.maximum(m, jnp.max(s1, axis=1, keepdims=True))
                e2 = jnp.exp(s2 - m)
                l = jnp.sum(e2, axis=1, keepdims=True) + jnp.exp(sk - m)
                if band:
                    e1 = jnp.exp(s1 - m)
                    l = l + jnp.sum(e1, axis=1, keepdims=True)
                r = 1.0 / l
                o = _dot((e2 * r).astype(BF16), vrep[n:L, gc])
                if band:
                    o = o + _dot((e1 * r).astype(BF16), vrep[pl.ds(lo, W), gc])
                o_ref[:, gc] = _swa_fold_rows(o, tq).astype(o_ref.dtype)
                lse = m + jnp.log(l)
                for h in range(4):
                    lse_ref[:, 4 * g + h:4 * g + h + 1] = lse[h * tq:(h + 1) * tq]

        pl.when(i < n // tq)(functools.partial(tile, True))
        pl.when(i >= n // tq)(functools.partial(tile, False))

    return _pallas(
        body, name="swa_fwd", grid=(B, L // tq),
        in_specs=[_tile_spec(tq, 512), _full_spec(L, 128), _full_spec(L, 128), pl.BlockSpec((1, 8), lambda b, i: (0, 0))],
        out_specs=[_tile_spec(tq, 512), _tile_spec(tq, 8)],
        out_shape=[_sds((B, L, 512), BF16), _sds((B, L, 8), F32)],
        scratch_shapes=[pltpu.VMEM((L, 512), BF16), pltpu.VMEM((L, 512), BF16)],
        compiler_params=_params(("parallel", "arbitrary")),
    )(q, k, v, sink)


def _swa_bwd(q, k, v, sink, lse, do, n, tq):
    B, L, _ = q.shape

    def body(q_ref, k_ref, v_ref, sink_ref, lse_ref, do_ref, dq_ref, dk_ref, dv_ref, dsink_ref):
        i = pl.program_id(1)

        @pl.when(i == 0)
        def _():
            dk_ref[...] = jnp.zeros_like(dk_ref)
            dv_ref[...] = jnp.zeros_like(dv_ref)
            dsink_ref[...] = jnp.zeros_like(dsink_ref)

        def tile(band):
            kc, vc = k_ref[n:L, :], v_ref[n:L, :]
            if band:
                lo, W = _swa_window(i, tq, n)
                mask = _swa_mask(i, lo, tq, W, n)
                kl, vl = k_ref[pl.ds(lo, W), :], v_ref[pl.ds(lo, W), :]
            for h in range(8):
                g = h // 4
                cols = slice(64 * g, 64 * g + 64)
                qh = q_ref[:, 64 * h:64 * h + 64]
                lse_h = lse_ref[:, h:h + 1]
                doh = do_ref[:, 64 * h:64 * h + 64].astype(BF16)
                p2 = jnp.exp(_dot_nt(qh, kc[:, cols]) * SWA_SCALE - lse_h)
                ps = jnp.exp(sink_ref[0:1, h:h + 1] - lse_h)
                dp2 = _dot_nt(doh, vc[:, cols])
                delta = jnp.sum(p2 * dp2, axis=1, keepdims=True)
                if band:
                    p1 = jnp.exp(jnp.where(mask, _dot_nt(qh, kl[:, cols]) * SWA_SCALE, NEG_INF) - lse_h)
                    dp1 = _dot_nt(doh, vl[:, cols])
                    delta = delta + jnp.sum(p1 * dp1, axis=1, keepdims=True)
                ds2 = (p2 * (dp2 - delta) * SWA_SCALE).astype(BF16)
                dq = _dot(ds2, kc[:, cols])
                dk_ref[n:L, cols] += _dot_tn(ds2, qh)
                dv_ref[n:L, cols] += _dot_tn(p2.astype(BF16), doh)
                if band:
                    ds1 = (p1 * (dp1 - delta) * SWA_SCALE).astype(BF16)
                    dq = dq + _dot(ds1, kl[:, cols])
                    dk_ref[pl.ds(lo, W), cols] += _dot_tn(ds1, qh)
                    dv_ref[pl.ds(lo, W), cols] += _dot_tn(p1.astype(BF16), doh)
                dq_ref[:, 64 * h:64 * h + 64] = dq
                dsink_ref[0:1, h:h + 1] += jnp.sum(-ps * delta, axis=0, keepdims=True)

        pl.when(i < n // tq)(functools.partial(tile, True))
        pl.when(i >= n // tq)(functools.partial(tile, False))

    return _pallas(
        body, name="swa_bwd", grid=(B, L // tq),
        in_specs=[_tile_spec(tq, 512), _full_spec(L, 128), _full_spec(L, 128), pl.BlockSpec((1, 8), lambda b, i: (0, 0)),
                  _tile_spec(tq, 8), pl.BlockSpec((None, tq, 512), lambda b, i: (b, i, 1))],
        out_specs=[_tile_spec(tq, 512), _full_spec(L, 128), _full_spec(L, 128),
                   pl.BlockSpec((None, 1, 8), lambda b, i: (b, 0, 0))],
        out_shape=[_sds((B, L, 512), F32), _sds((B, L, 128), F32), _sds((B, L, 128), F32), _sds((B, 1, 8), F32)],
        compiler_params=_params(("parallel", "arbitrary")),
    )(q, k, v, sink, lse, do)


def _ret_decay(i, tq, L, n, ctx_tile):
    qi = i * tq + lax.broadcasted_iota(jnp.int32, (tq, 1), 0)
    kc = lax.broadcasted_iota(jnp.int32, (1, L - n), 1)
    d_hb = (n + kc - qi).astype(F32)
    if ctx_tile:
        return None, ((qi - L) - (kc - (L - n))).astype(F32), d_hb
    d_x = (qi - lax.broadcasted_iota(jnp.int32, (1, n), 1)).astype(F32)
    return d_x, (qi - (kc - (L - n))).astype(F32), d_hb


def _ret_head(a, h):
    lane = lax.broadcasted_iota(jnp.int32, (1, a.shape[1]), 1)
    return jnp.where(lane // 64 == h, a, jnp.zeros_like(a))


def _ret_mask(dist, lg):
    return jnp.where(dist >= 0.0, jnp.exp(lg * jnp.maximum(dist, 0.0)), 0.0)


def _ret_weights(n, L, d_x, d_hf, d_hb, lg_f, lg_b, with_grad):
    m_f, m_b = _ret_mask(d_hf, lg_f), _ret_mask(d_hb, lg_b)
    ctx = (slice(n, L), m_f + m_b)
    if with_grad:
        ctx += (m_f * jnp.maximum(d_hf, 0.0), m_b * jnp.maximum(d_hb, 0.0))
    if d_x is None:
        return [ctx]
    e_x = jnp.exp(jnp.where(d_x >= 0.0, lg_f, -lg_b) * d_x)
    lat = (slice(0, n), jnp.where(d_x == 0.0, 2.0 * e_x, e_x))
    if with_grad:
        lat += (e_x * jnp.maximum(d_x, 0.0), e_x * jnp.maximum(-d_x, 0.0))
    return [lat, ctx]


def _ret_fwd(q, k, v, lg, n, tq):
    B, L, _ = q.shape

    def body(q_ref, k_ref, v_ref, lg_ref, y_ref):
        i = pl.program_id(1)

        def tile(ctx_tile):
            dist = _ret_decay(i, tq, L, n, ctx_tile)
            for h in range(4):
                qh = _ret_head(q_ref[...], h)
                y = None
                for rows, dec_r in _ret_weights(n, L, *dist, lg_ref[0:1, h:h + 1], lg_ref[1:2, h:h + 1], False):
                    a = _dot_nt(qh, k_ref[rows, :]) * dec_r
                    part = _dot(a.astype(BF16), v_ref[rows, 128 * h:128 * h + 128])
                    y = part if y is None else y + part
                y_ref[:, 128 * h:128 * h + 128] = y

        pl.when(i < n // tq)(functools.partial(tile, False))
        pl.when(i >= n // tq)(functools.partial(tile, True))

    return _pallas(
        body, name="ret_fwd", grid=(B, L // tq),
        in_specs=[_tile_spec(tq, 256), _full_spec(L, 256), _full_spec(L, 512), pl.BlockSpec((2, 4), lambda b, i: (0, 0))],
        out_specs=_tile_spec(tq, 512), out_shape=_sds((B, L, 512), F32),
        compiler_params=_params(("parallel", "arbitrary")),
    )(q, k, v, lg)


def _ret_bwd(q, k, v, lg, dy, n, tq):
    B, L, _ = q.shape

    def body(q_ref, k_ref, v_ref, lg_ref, dy_ref, dq_ref, dk_ref, dv_ref, dlg_ref):
        i = pl.program_id(1)

        @pl.when(i == 0)
        def _():
            dk_ref[...] = jnp.zeros_like(dk_ref)
            dv_ref[...] = jnp.zeros_like(dv_ref)
            dlg_ref[...] = jnp.zeros_like(dlg_ref)

        def total(a):
            return jnp.sum(jnp.sum(a, axis=1, keepdims=True), axis=0, keepdims=True)

        def tile(ctx_tile):
            dist = _ret_decay(i, tq, L, n, ctx_tile)
            dq = None
            for h in range(4):
                vc = slice(128 * h, 128 * h + 128)
                qh = _ret_head(q_ref[...], h)
                dyh = dy_ref[:, vc].astype(BF16)
                dqh = None
                for rows, dec_r, wf_r, wb_r in _ret_weights(n, L, *dist, lg_ref[0:1, h:h + 1], lg_ref[1:2, h:h + 1], True):
                    k_all, vh = k_ref[rows, :], v_ref[rows, vc]
                    s = _dot_nt(qh, k_all)
                    gr = _dot_nt(dyh, vh)
                    ds = (gr * dec_r).astype(BF16)
                    part = _dot(ds, k_all)
                    dqh = part if dqh is None else dqh + part
                    dk_ref[rows, :] += _dot_tn(ds, qh)
                    dv_ref[rows, vc] += _dot_tn((s * dec_r).astype(BF16), dyh)
                    gs = gr * s
                    dlg_ref[0:1, h:h + 1] += total(gs * wf_r)
                    dlg_ref[1:2, h:h + 1] += total(gs * wb_r)
                dqh = _ret_head(dqh, h)
                dq = dqh if dq is None else dq + dqh
            dq_ref[...] = dq

        pl.when(i < n // tq)(functools.partial(tile, False))
        pl.when(i >= n // tq)(functools.partial(tile, True))

    return _pallas(
        body, name="ret_bwd", grid=(B, L // tq),
        in_specs=[_tile_spec(tq, 256), _full_spec(L, 256), _full_spec(L, 512), pl.BlockSpec((2, 4), lambda b, i: (0, 0)),
                  _tile_spec(tq, 512)],
        out_specs=[_tile_spec(tq, 256), _full_spec(L, 256), _full_spec(L, 512),
                   pl.BlockSpec((None, 2, 4), lambda b, i: (b, 0, 0))],
        out_shape=[_sds((B, L, 256), F32), _sds((B, L, 256), F32), _sds((B, L, 512), F32), _sds((B, 2, 4), F32)],
        compiler_params=_params(("parallel", "arbitrary")),
    )(q, k, v, lg, dy)


def _loss_head(s, target, g, n, tm):
    B, L, D = s.shape
    nx = n // tm

    def body(s_ref, t_ref, g_ref, ds_ref, dg_ref, loss_ref):
        b, i = pl.program_id(0), pl.program_id(1)

        @pl.when((b == 0) & (i == 0))
        def _():
            dg_ref[...] = jnp.zeros_like(dg_ref)
            loss_ref[...] = jnp.zeros_like(loss_ref)

        @pl.when(i < nx)
        def _():
            y, vjp = jax.vjp(_rms, s_ref[...], g_ref[...])
            err = y - t_ref[...]
            d_s, d_g = vjp(err * (1.0 / D))
            ds_ref[...] = d_s
            dg_ref[...] += d_g
            part = jnp.sum(jnp.sum(err * err, axis=1, keepdims=True), axis=0, keepdims=True) * (0.5 / D)
            loss_ref[...] += jnp.broadcast_to(part, loss_ref.shape)

        @pl.when(i >= nx)
        def _():
            ds_ref[...] = jnp.zeros_like(ds_ref)

    return _pallas(
        body, name="loss_head", grid=(B, L // tm),
        in_specs=[pl.BlockSpec((None, tm, D), lambda b, i: (b, i, 0)),
                  pl.BlockSpec((None, tm, D), lambda b, i: (b, jnp.minimum(i, nx - 1), 0)),
                  pl.BlockSpec((1, D), lambda b, i: (0, 0))],
        out_specs=[pl.BlockSpec((None, tm, D), lambda b, i: (b, i, 0)), pl.BlockSpec((1, D), lambda b, i: (0, 0)),
                   pl.BlockSpec((1, 128), lambda b, i: (0, 0))],
        out_shape=[_sds((B, L, D), F32), _sds((1, D), F32), _sds((1, 128), F32)],
        compiler_params=_params(("arbitrary", "arbitrary")),
    )(s, target, g)


def _ada_fwd(c_all, ada_w, ada_b):
    NL, D, Ns = ada_w.shape
    R = c_all.shape[0]

    def body(c_ref, w_ref, b_ref, o_ref):
        cond = jax.nn.silu(c_ref[...]).astype(BF16)
        o_ref[...] = _dot(cond, w_ref[...].astype(BF16)) + b_ref[...]

    return _pallas(
        body, name="ada_fwd", grid=(NL,),
        in_specs=[pl.BlockSpec((R, D), lambda l: (0, 0)), pl.BlockSpec((None, D, Ns), lambda l: (l, 0, 0)),
                  pl.BlockSpec((None, 1, Ns), lambda l: (l, 0, 0))],
        out_specs=pl.BlockSpec((None, R, Ns), lambda l: (l, 0, 0)), out_shape=_sds((NL, R, Ns), F32),
        compiler_params=_params(("parallel",)),
    )(c_all, ada_w, ada_b)


def _ada_bwd(c_all, ada_w, dmods):
    NL, D, Ns = ada_w.shape
    R = c_all.shape[0]

    def body(c_ref, w_ref, dm_ref, dw_ref, dc_ref):
        cond = jax.nn.silu(c_ref[...]).astype(BF16)
        dm = dm_ref[...].astype(BF16)
        dw_ref[...] = _dot_tn(cond, dm)
        dc_ref[...] = _dot_nt(dm, w_ref[...].astype(BF16))

    return _pallas(
        body, name="ada_bwd", grid=(NL,),
        in_specs=[pl.BlockSpec((R, D), lambda l: (0, 0)), pl.BlockSpec((None, D, Ns), lambda l: (l, 0, 0)),
                  pl.BlockSpec((None, R, Ns), lambda l: (l, 0, 0))],
        out_specs=[pl.BlockSpec((None, D, Ns), lambda l: (l, 0, 0)), pl.BlockSpec((None, R, D), lambda l: (l, 0, 0))],
        out_shape=[_sds((NL, D, Ns), F32), _sds((NL, R, D), F32)],
        compiler_params=_params(("parallel",)),
    )(c_all, ada_w, dmods)


def _my_index():
    return 4 * lax.axis_index("x") + 2 * lax.axis_index("y") + lax.axis_index("c")


def _peer(k):
    x, y, c = lax.axis_index("x"), lax.axis_index("y"), lax.axis_index("c")
    kx, ky, kc = (k >> 2) & 1, (k >> 1) & 1, k & 1
    px, py, pc = (x + kx) % 2, (y + ky) % 2, (c + kc) % 2
    return (px, py, pc), 4 * px + 2 * py + pc


def _all_gather(name, shards):
    na = len(shards)
    hbm = pl.BlockSpec(memory_space=pl.ANY)

    def body(*refs):
        in_refs, out_refs = refs[:na], refs[na:2 * na]
        send_sems, recv_sems, local_sems = refs[2 * na:]
        me = _my_index()
        sib_id, sib = _peer(1)
        chips = [_peer(k) for k in (4, 2, 6)]
        sib_chips = [4 * px + 2 * py + (1 - pc) for (px, py, pc), _ in chips]

        def copy(a, k, slot, to, src=None):
            dst = out_refs[a].at[slot]
            return pltpu.make_async_remote_copy(
                src_ref=dst if src is None else src, dst_ref=dst, send_sem=send_sems.at[a, k],
                recv_sem=recv_sems.at[a, k], device_id=to, device_id_type=MESH)

        first, passed, mine = [], [], []
        for a in range(na):
            cp = pltpu.make_async_copy(in_refs[a], out_refs[a].at[me], local_sems.at[a])
            cp.start()
            mine.append(cp)
            first.append(copy(a, 0, me, sib_id, src=in_refs[a]))
            first += [copy(a, 1 + j, me, pid, src=in_refs[a]) for j, (pid, _) in enumerate(chips)]
        for cp in first:
            cp.start()
        for a in range(na):
            for j, (pid, pidx) in enumerate(chips):
                copy(a, 1 + j, pidx, pid).wait_recv()
                fwd = copy(a, 4 + j, pidx, sib_id)
                fwd.start()
                passed.append(fwd)
        for a in range(na):
            copy(a, 0, sib, sib_id).wait_recv()
            for j in range(3):
                copy(a, 4 + j, sib_chips[j], sib_id).wait_recv()
        for cp in first + passed:
            cp.wait_send()
        for cp in mine:
            cp.wait()

    return _pallas(
        body, name=name, in_specs=[hbm] * na, out_specs=[hbm] * na,
        out_shape=[_sds((N_DEV,) + s.shape, s.dtype) for s in shards],
        scratch_shapes=[pltpu.SemaphoreType.DMA((na, 7)), pltpu.SemaphoreType.DMA((na, 7)),
                        pltpu.SemaphoreType.DMA((na,))],
    )(*shards)


def _all_to_all(name, parts):
    na = len(parts)
    hbm = pl.BlockSpec(memory_space=pl.ANY)

    def body(*refs):
        in_refs, out_refs = refs[:na], refs[na:2 * na]
        send_sems, recv_sems, local_sems = refs[2 * na:]
        me = _my_index()
        copies = []
        for a in range(na):
            cp = pltpu.make_async_copy(in_refs[a].at[me], out_refs[a].at[me], local_sems.at[a])
            cp.start()
            copies.append(cp)
            for k in range(1, N_DEV):
                pid, pidx = _peer(k)
                cp = pltpu.make_async_remote_copy(
                    src_ref=in_refs[a].at[pidx], dst_ref=out_refs[a].at[me], send_sem=send_sems.at[a, k - 1],
                    recv_sem=recv_sems.at[a, k - 1], device_id=pid, device_id_type=MESH)
                cp.start()
                copies.append(cp)
        for cp in copies:
            cp.wait()

    return _pallas(
        body, name=name, in_specs=[hbm] * na, out_specs=[hbm] * na,
        out_shape=[_sds(p.shape, p.dtype) for p in parts],
        scratch_shapes=[pltpu.SemaphoreType.DMA((na, 7)), pltpu.SemaphoreType.DMA((na, 7)),
                        pltpu.SemaphoreType.DMA((na,))],
    )(*parts)


_HBM = pl.BlockSpec(memory_space=pltpu.HBM)
_SEM = pl.BlockSpec(memory_space=pltpu.SEMAPHORE)
_DATAFLOW = pltpu.SideEffectType.DATAFLOW_SIDE_EFFECTING


def _exchange_start(name, srcs, slotted, after=()):
    na = len(srcs)
    lands = [lax.empty((N_DEV,) + (s.shape[1:] if slotted else s.shape), s.dtype) for s in srcs]

    def body(*refs):
        src_refs, land_refs = refs[:na], refs[na:2 * na]
        outs = refs[2 * na + len(after):]
        send_sems, recv_sems, token = outs[:na], outs[na:2 * na], outs[4 * na]
        me = _my_index()
        for a in range(na):
            for k in range(1, N_DEV):
                pid, pidx = _peer(k)
                pltpu.make_async_remote_copy(
                    src_ref=src_refs[a].at[pidx] if slotted else src_refs[a], dst_ref=land_refs[a].at[me],
                    send_sem=send_sems[a], recv_sem=recv_sems[a], device_id=pid, device_id_type=MESH).start()
        token[...] = jnp.zeros_like(token)

    ops = [pltpu.with_memory_space_constraint(a, pltpu.HBM) for a in list(srcs) + lands]
    outs = _pallas(
        body, name=name,
        out_shape=[pltpu.SemaphoreType.DMA(())] * (2 * na) + [pltpu.HBM(a.shape, a.dtype) for a in ops]
        + [_sds((8, 128), F32)],
        in_specs=[_HBM] * (2 * na) + [pl.BlockSpec(memory_space=pl.ANY)] * len(after),
        out_specs=[_SEM] * (2 * na) + [_HBM] * (2 * na) + [pl.BlockSpec(memory_space=pltpu.VMEM)],
        input_output_aliases={a: 2 * na + a for a in range(2 * na)},
        compiler_params=pltpu.CompilerParams(has_side_effects=_DATAFLOW),
    )(*ops, *after)
    return (na, outs[:2 * na], outs[2 * na:4 * na]), outs[4 * na]


def _exchange_wait(name, handle, after):
    na, sems, thru = handle

    def body(*refs):
        land_refs = refs[na:2 * na]
        send_sems, recv_sems = refs[2 * na:3 * na], refs[3 * na:4 * na]
        me_id = (lax.axis_index("x"), lax.axis_index("y"), lax.axis_index("c"))
        for a in range(na):
            seven = land_refs[a].at[pl.ds(0, N_DEV - 1)]
            drain = pltpu.make_async_remote_copy(src_ref=seven, dst_ref=seven, send_sem=send_sems[a],
                                                 recv_sem=recv_sems[a], device_id=me_id, device_id_type=MESH)
            drain.wait_send()
            drain.wait_recv()

    outs = _pallas(
        body, name=name, out_shape=[pltpu.HBM(a.shape, a.dtype) for a in thru],
        in_specs=[_HBM] * (2 * na) + [_SEM] * (2 * na) + [pl.BlockSpec(memory_space=pl.ANY)],
        out_specs=[_HBM] * (2 * na), input_output_aliases={a: a for a in range(2 * na)},
        compiler_params=pltpu.CompilerParams(has_side_effects=_DATAFLOW),
    )(*thru, *sems, after)
    return outs[na:]


def _fill_own(name, landed, owns, slotted):
    me = _my_index()
    return [jnp.where(lax.broadcasted_iota(jnp.int32, land.shape, 0) == me, own, land)
            for land, own in zip(landed, owns)]


def _rope_tables(n, L, width):
    t = jnp.arange(n)
    row = (t // GRID_W).astype(F32)
    col = (t % GRID_W).astype(F32)
    n_freq = ROPE_DIM // 4
    freqs = ROPE_THETA ** (-jnp.arange(n_freq, dtype=F32) / n_freq)
    ang = jnp.concatenate([row[:, None] * freqs, col[:, None] * freqs], axis=-1)
    cos, sin = jnp.cos(ang), jnp.sin(ang)
    cos2 = jnp.concatenate([cos, cos], axis=-1)
    sin2 = jnp.concatenate([-sin, sin], axis=-1)
    cos2 = jnp.concatenate([cos2, jnp.ones((L - n, ROPE_DIM), F32)], axis=0)
    sin2 = jnp.concatenate([sin2, jnp.zeros((L - n, ROPE_DIM), F32)], axis=0)
    reps = width // ROPE_DIM
    return jnp.tile(cos2, (1, reps)), jnp.tile(sin2, (1, reps))


def _ab_perm(w):
    return jnp.concatenate([w[:, 0:256], w[:, 320:1600], w[:, 256:320], w[:, 256:320]], axis=1)


def _ab_unperm(g):
    rope_key = (g[:, 1536:1600].astype(F32) + g[:, 1600:1664].astype(F32)).astype(g.dtype)
    return jnp.concatenate([g[:, 0:256], rope_key, g[:, 256:1536]], axis=1)


def _wq_perm(w):
    return jnp.concatenate([w[:, 192 * h:192 * h + 128] for h in range(4)]
                           + [w[:, 192 * h + 128:192 * h + 192] for h in range(4)], axis=1)


def _wq_unperm(g):
    return jnp.concatenate([g[:, sl] for h in range(4)
                            for sl in (slice(128 * h, 128 * h + 128), slice(512 + 64 * h, 576 + 64 * h))], axis=1)


def _flat(a):
    return a.reshape(-1, a.shape[-1])


def _layer_weights(full, p):
    NL, NE, NO = len(full["ffn_in"]), len(full["ab_in"]), len(full["cd_in"])
    groups = range(4)

    def each(f, mats):
        return [None if w is None else f(w) for w in mats]

    return dict(
        norm_mix=[p["norm_mix"][l][None] for l in range(NL)], norm_ffn=[p["norm_ffn"][l][None] for l in range(NL)],
        norm_final=p["norm_final"][None],
        ffn_in=each(_ffn_interleave, full["ffn_in"]), ffn_out=list(full["ffn_out"]),
        ab_in=each(_ab_perm, full["ab_in"]), ab_out=list(full["ab_out"]),
        wq=each(_wq_perm, full["mla_wq_b"]), wkv=list(full["mla_wkv_b"]),
        kv_norm=[p["mla_kv_norm"][j][None] for j in range(NE)], q_norm=[p["mla_q_norm"][j][None] for j in range(NE)],
        v_norm=[[p["cmlp_v_norm"][j][None, 128 * g:128 * g + 128] for g in groups] for j in range(NE)],
        ws=[[p["cmlp_ws"][j, g] for g in groups] for j in range(NE)],
        bs=[[p["cmlp_bs"][j, g][:, None] for g in groups] for j in range(NE)],
        cd_in=list(full["cd_in"]), cd_out=list(full["cd_out"]),
        lg=[jnp.stack([jax.nn.log_sigmoid(p["ret_decay_fwd"][j]), jax.nn.log_sigmoid(p["ret_decay_bwd"][j])])
            for j in range(NO)],
        sink=[p["swa_sink"][j][None] for j in range(NO)],
        ret_norm=[[p["ret_norm"][j][None, 128 * g:128 * g + 128] for g in groups] for j in range(NO)],
    )


def _local_step(x, ctx, target, mods, W, later_weights=None, early_grads=None):
    B, n, D = x.shape
    m = ctx.shape[1]
    L = n + m
    NL = mods.shape[0]
    tm = min(256, m)
    tq = min(256, m)
    cos512, sin512 = _rope_tables(n, L, 512)
    s = jnp.concatenate([x, ctx], axis=1)
    saved = []

    def rows(name, f, arrays, pieces, samp, samp_pieces, glob, out_arrays, out_pieces, tile=tm):
        return _Rows(name, f, B, L, n, tile, arrays, pieces, samp, samp_pieces, glob, out_arrays, out_pieces)

    def full(width, start=0):
        return [(0, start, width)]

    for l in range(NL):
        j = l // 2
        even = l % 2 == 0
        md = mods[l]
        if l == 1 and later_weights is not None:
            W = later_weights(1, s)
        r = {}
        if l == 0:
            r["pre1"] = rows("pre_mix0", _f_pre, [s], full(D), [md], [(0, 0), (0, 1)], [W["norm_mix"][0]], [(D, BF16)], full(D))
            (xn,) = r["pre1"].fwd()
        r["xn"] = xn
        if even:
            z = _mm(f"ab_in{l}", _flat(xn), W["ab_in"][j]).reshape(B, L, 1664)
            pieces = [(0, 0, 256), (0, 256, 256), (0, 1536, 128)]
            pieces += [(0, 512 + 128 * g, 128) for g in range(4)] + [(0, 1024 + 128 * g, 128) for g in range(4)]
            pieces += [(1, 0, 128), (2, 0, 128)]
            glob = [W["kv_norm"][j], W["q_norm"][j]] + W["v_norm"][j] + W["ws"][j] + W["bs"][j]
            abp = rows(f"ab_pre{l}", _f_ab_pre, [z, cos512, sin512], pieces, [], [], glob,
                       [(256, BF16), (256, BF16), (128, BF16), (512, BF16)],
                       [(0, 0, 256), (1, 0, 256), (2, 0, 128)] + [(3, 128 * g, 128) for g in range(4)], tile=128)
            kvn, qn, kpe, cm = abp.fwd()
            kv = _mm(f"wkv{l}", _flat(kvn), W["wkv"][j], out_dtype=BF16).reshape(B, L, 1024)
            q0 = _mm(f"wq{l}", _flat(qn), W["wq"][j]).reshape(B, L, 768)
            qrp = rows(f"q_rope{l}", _f_q_rope, [q0, cos512, sin512], [(0, 0, 512), (0, 512, 256), (1, 0, 256), (2, 0, 256)],
                       [], [], [], [(768, BF16)], [(0, 0, 512), (0, 512, 256)])
            (q,) = qrp.fwd()
            o, lse = _mla_fwd(q, kv, kpe, n, tq)
            y = _mm(f"ab_out_a{l}", _flat(o), W["ab_out"][j][:512])
            y = _mm(f"ab_out_b{l}", _flat(cm), W["ab_out"][j][512:], add=y).reshape(B, L, D)
            r.update(z=z, abp=abp, kvn=kvn, qn=qn, kpe=kpe, cm=cm, kv=kv, qrp=qrp, q=q, o=o, lse=lse)
        else:
            z = _mm(f"cd_in{l}", _flat(xn), W["cd_in"][j]).reshape(B, L, 2304)
            pieces = [(0, 0, 256), (0, 256, 512), (0, 768, 128), (0, 896, 128), (0, 1024, 256), (0, 1280, 512), (0, 1792, 512)]
            pieces += [(1, 0, 256), (2, 0, 256), (1, 0, 128), (2, 0, 128), (1, 0, 512), (2, 0, 512)]
            cdp = rows(f"cd_pre{l}", _f_cd_pre, [z, cos512, sin512], pieces, [], [], [],
                       [(256, BF16), (256, BF16), (512, BF16), (512, BF16), (128, BF16), (128, BF16), (512, F32)],
                       [(k_, 0, w_) for k_, w_ in enumerate((256, 256, 512, 512, 128, 128, 512))])
            rq, rk, rv, sq, sk, sv, rg = cdp.fwd()
            yret = _ret_fwd(rq, rk, rv, W["lg"][j], n, tq)
            osw, lse = _swa_fwd(sq, sk, sv, W["sink"][j], n, tq)
            mrg = rows(f"cd_merge{l}", _f_cd_merge, [yret, rg],
                       [(0, 128 * g, 128) for g in range(4)] + [(1, 128 * g, 128) for g in range(4)], [], [],
                       W["ret_norm"][j], [(512, BF16)], [(0, 128 * g, 128) for g in range(4)])
            (yr,) = mrg.fwd()
            y = _mm(f"cd_out_a{l}", _flat(yr), W["cd_out"][j][:512])
            y = _mm(f"cd_out_b{l}", _flat(osw), W["cd_out"][j][512:], add=y).reshape(B, L, D)
            r.update(z=z, cdp=cdp, rq=rq, rk=rk, rv=rv, sq=sq, sk=sk, sv=sv, rg=rg, yret=yret, osw=osw, lse=lse,
                     mrg=mrg, yr=yr)
        two, outs2 = [(0, 0, D), (1, 0, D)], [(D, F32), (D, BF16)]
        r["mix_out"] = rows(f"mix_res_pre{l}", _f_res_pre, [s, y], two, [md], [(0, 2), (0, 3), (0, 4)],
                            [W["norm_ffn"][l]], outs2, two)
        s1, xn2 = r["mix_out"].fwd()
        if l == 0 and later_weights is not None:
            W = later_weights(0, s1)
        y2 = _ffn_fwd(f"ffn{l}", _flat(xn2), W["ffn_in"][l], W["ffn_out"][l]).reshape(B, L, D)
        if l < NL - 1:
            r["ffn_out"] = rows(f"ffn_res_pre{l}", _f_res_pre, [s1, y2], two, [md, mods[l + 1]], [(0, 5), (1, 0), (1, 1)],
                                [W["norm_mix"][l + 1]], outs2, two)
            s, xn = r["ffn_out"].fwd()
        else:
            r["ffn_out"] = rows(f"res_ffn{l}", _f_res, [s1, y2], two, [md], [(0, 5)], [], [(D, F32)], full(D))
            (s,) = r["ffn_out"].fwd()
        r["xn2"] = xn2
        saved.append(r)

    ds, d_norm_final, loss = _loss_head(s, target, W["norm_final"], n, tm)

    G = {k: [None] * len(v) for k, v in W.items() if isinstance(v, list)}
    G["norm_final"] = d_norm_final
    dm = [[None] * 6 for _ in range(NL)]
    dxn_next = None
    for l in reversed(range(NL)):
        j = l // 2
        even = l % 2 == 0
        r = saved[l]
        zero = early_grads(0, G) if l == 0 and early_grads is not None else None
        if l == NL - 1:
            (ds1, dy2), (dm[l][5],), _ = r["ffn_out"].bwd([ds], {0: F32, 1: BF16}, unread=(0,))
        else:
            smp = None if zero is None else [mods[l] + zero, mods[l + 1] + zero]
            (ds1, dy2), (dm[l][5], dm[l + 1][0], dm[l + 1][1]), (G["norm_mix"][l + 1],) = r["ffn_out"].bwd(
                [ds, dxn_next], {0: F32, 1: BF16}, grad_glob=(0,), samp=smp)
        dy2f, xn2f = _flat(dy2), _flat(r["xn2"])
        hid, dz2 = _ffn_mid_bwd(f"ffn_mid{l}", xn2f, dy2f, W["ffn_in"][l], W["ffn_out"][l])
        G["ffn_out"][l] = _mm(f"g_ffn_out{l}", hid, dy2f, ta=True, out_dtype=BF16)
        G["ffn_in"][l] = _mm(f"g_ffn_in{l}", xn2f, dz2, ta=True, out_dtype=BF16)
        dxn2 = _mm(f"d_xn2{l}", dz2, W["ffn_in"][l], tb=True).reshape(B, L, D)
        zero = early_grads(1, G) if l == 0 and early_grads is not None else None
        smp = None if zero is None else [mods[l] + zero]
        (ds0, dy), (dm[l][2], dm[l][3], dm[l][4]), (G["norm_ffn"][l],) = r["mix_out"].bwd(
            [ds1, dxn2], {0: F32, 1: BF16}, grad_glob=(0,), samp=smp)
        dyf = _flat(dy)
        if even:
            w_out = W["ab_out"][j]
            dcat = _mm(f"d_cat{l}", dyf, w_out, tb=True).reshape(B, L, -1)
            G["ab_out"][l // 2] = jnp.concatenate(
                [_mm(f"g_ab_out_a{l}", _flat(r["o"]), dyf, ta=True, out_dtype=BF16),
                 _mm(f"g_ab_out_b{l}", _flat(r["cm"]), dyf, ta=True, out_dtype=BF16)], axis=0)
            dq, dkv, dkpe = _mla_bwd(r["q"], r["kv"], r["kpe"], r["lse"], dcat, n, tq)
            (dq0,), _, _ = r["qrp"].bwd([dq], {0: BF16}, unread=(0,))
            dq0f, dkvf = _flat(dq0), _flat(dkv)
            G["wq"][j] = _mm(f"g_wq{l}", _flat(r["qn"]), dq0f, ta=True, out_dtype=BF16)
            G["wkv"][j] = _mm(f"g_wkv{l}", _flat(r["kvn"]), dkvf, ta=True, out_dtype=BF16)
            dqn = _mm(f"d_qn{l}", dq0f, W["wq"][j], tb=True).reshape(B, L, 256)
            dkvn = _mm(f"d_kvn{l}", dkvf, W["wkv"][j], tb=True).reshape(B, L, 256)
            (dz,), _, gg = r["abp"].bwd([dkvn, dqn, dkpe, (dcat, 512)], {0: BF16}, grad_glob=tuple(range(14)))
            G["kv_norm"][j], G["q_norm"][j] = gg[0], gg[1]
            G["v_norm"][j], G["ws"][j], G["bs"][j] = list(gg[2:6]), list(gg[6:10]), list(gg[10:14])
            w_in, key = W["ab_in"][j], "ab_in"
        else:
            w_out = W["cd_out"][j]
            dcat = _mm(f"d_cat{l}", dyf, w_out, tb=True).reshape(B, L, -1)
            G["cd_out"][j] = jnp.concatenate(
                [_mm(f"g_cd_out_a{l}", _flat(r["yr"]), dyf, ta=True, out_dtype=BF16),
                 _mm(f"g_cd_out_b{l}", _flat(r["osw"]), dyf, ta=True, out_dtype=BF16)], axis=0)
            (dyret, drg), _, gg = r["mrg"].bwd([(dcat, 0)], {0: F32, 1: F32}, grad_glob=(0, 1, 2, 3))
            G["ret_norm"][j] = list(gg)
            drq, drk, drv, dlg = _ret_bwd(r["rq"], r["rk"], r["rv"], W["lg"][j], dyret, n, tq)
            dsq, dsk, dsv, dsink = _swa_bwd(r["sq"], r["sk"], r["sv"], W["sink"][j], r["lse"], dcat, n, tq)
            G["lg"][j], G["sink"][j] = dlg, dsink
            (dz,), _, _ = r["cdp"].bwd([drq, drk, drv, dsq, dsk, dsv, drg], {0: BF16}, unread=(0,))
            w_in, key = W["cd_in"][j], "cd_in"
        dzf = _flat(dz)
        G[key][j] = _mm(f"g_{key}{l}", _flat(r["xn"]), dzf, ta=True, out_dtype=BF16)
        dxn = _mm(f"d_xn{l}", dzf, w_in, tb=True).reshape(B, L, D)
        if l == 0:
            (ds,), (dm[0][0], dm[0][1]), (G["norm_mix"][0],) = r["pre1"].bwd([dxn], {0: F32}, grad_glob=(0,),
                                                                            add={0: ds0}, samp=smp)
        else:
            ds, dxn_next = ds0, dxn
    dmods = jnp.stack([jnp.concatenate(d, axis=2) for d in dm])
    return loss, ds[:, :n], dmods, G


def kernel(x, c, ctx, c_ctx, ada_w, ada_b, norm_mix, norm_ffn, norm_final, ffn_in, ffn_out, ab_in, ab_out, mla_q_norm, mla_kv_norm, mla_wq_b, mla_wkv_b, cmlp_v_norm, cmlp_ws, cmlp_bs, cd_in, cd_out, ret_decay_fwd, ret_decay_bwd, ret_norm, swa_sink, loss_target, m_c_ctx, m_ada_w, m_ada_b, m_norm_mix, m_norm_ffn, m_norm_final, m_ffn_in, m_ffn_out, m_ab_in, m_ab_out, m_mla_q_norm, m_mla_kv_norm, m_mla_wq_b, m_mla_wkv_b, m_cmlp_v_norm, m_cmlp_ws, m_cmlp_bs, m_cd_in, m_cd_out, m_ret_decay_fwd, m_ret_decay_bwd, m_ret_norm, m_swa_sink, v_c_ctx, v_ada_w, v_ada_b, v_norm_mix, v_norm_ffn, v_norm_final, v_ffn_in, v_ffn_out, v_ab_in, v_ab_out, v_mla_q_norm, v_mla_kv_norm, v_mla_wq_b, v_mla_wkv_b, v_cmlp_v_norm, v_cmlp_ws, v_cmlp_bs, v_cd_in, v_cd_out, v_ret_decay_fwd, v_ret_decay_bwd, v_ret_norm, v_swa_sink):
    B, n, D = x.shape
    NL = ada_w.shape[0]
    NE, NO = ab_in.shape[0], cd_in.shape[0]
    me = _my_index()
    weights = dict(c_ctx=c_ctx, ada_w=ada_w, ada_b=ada_b, norm_mix=norm_mix, norm_ffn=norm_ffn, norm_final=norm_final,
                   ffn_in=ffn_in, ffn_out=ffn_out, ab_in=ab_in, ab_out=ab_out, mla_q_norm=mla_q_norm,
                   mla_kv_norm=mla_kv_norm, mla_wq_b=mla_wq_b, mla_wkv_b=mla_wkv_b, cmlp_v_norm=cmlp_v_norm,
                   cmlp_ws=cmlp_ws, cmlp_bs=cmlp_bs, cd_in=cd_in, cd_out=cd_out, ret_decay_fwd=ret_decay_fwd,
                   ret_decay_bwd=ret_decay_bwd, ret_norm=ret_norm, swa_sink=swa_sink)
    moments_m = dict(c_ctx=m_c_ctx, ada_w=m_ada_w, ada_b=m_ada_b, norm_mix=m_norm_mix, norm_ffn=m_norm_ffn,
                     norm_final=m_norm_final, ffn_in=m_ffn_in, ffn_out=m_ffn_out, ab_in=m_ab_in, ab_out=m_ab_out,
                     mla_q_norm=m_mla_q_norm, mla_kv_norm=m_mla_kv_norm, mla_wq_b=m_mla_wq_b, mla_wkv_b=m_mla_wkv_b,
                     cmlp_v_norm=m_cmlp_v_norm, cmlp_ws=m_cmlp_ws, cmlp_bs=m_cmlp_bs, cd_in=m_cd_in, cd_out=m_cd_out,
                     ret_decay_fwd=m_ret_decay_fwd, ret_decay_bwd=m_ret_decay_bwd, ret_norm=m_ret_norm,
                     swa_sink=m_swa_sink)
    moments_v = dict(c_ctx=v_c_ctx, ada_w=v_ada_w, ada_b=v_ada_b, norm_mix=v_norm_mix, norm_ffn=v_norm_ffn,
                     norm_final=v_norm_final, ffn_in=v_ffn_in, ffn_out=v_ffn_out, ab_in=v_ab_in, ab_out=v_ab_out,
                     mla_q_norm=v_mla_q_norm, mla_kv_norm=v_mla_kv_norm, mla_wq_b=v_mla_wq_b, mla_wkv_b=v_mla_wkv_b,
                     cmlp_v_norm=v_cmlp_v_norm, cmlp_ws=v_cmlp_ws, cmlp_bs=v_cmlp_bs, cd_in=v_cd_in, cd_out=v_cd_out,
                     ret_decay_fwd=v_ret_decay_fwd, ret_decay_bwd=v_ret_decay_bwd, ret_norm=v_ret_norm,
                     swa_sink=v_swa_sink)
    order = list(weights)

    Ns = ada_w.shape[2]
    (c_g,) = _all_gather("gather_c", [c])
    R = N_DEV * B + 8
    c_all = jnp.concatenate([c_g.reshape(N_DEV * B, D), jnp.broadcast_to(c_ctx[None], (8, D))], axis=0)
    ada_b_mine = lax.dynamic_slice_in_dim(ada_b, me * Ns, Ns, axis=1)[:, None, :]
    mods_shard = _ada_fwd(c_all, ada_w, ada_b_mine)
    (mods_g,) = _all_gather("gather_mods", [mods_shard])
    mods_full = jnp.transpose(mods_g, (1, 2, 0, 3)).reshape(NL, R, 6, D)
    mx = lax.dynamic_slice_in_dim(mods_full, me * B, B, axis=1)
    mh = jnp.broadcast_to(mods_full[:, N_DEV * B][:, None], (NL, B, 6, D))
    mods = jnp.stack([mx, mh], axis=2)

    big = ["ffn_in", "ffn_out", "ab_in", "ab_out", "cd_in", "cd_out", "mla_wq_b", "mla_wkv_b"]
    col_sharded = {"ffn_in", "ab_in", "cd_in", "mla_wq_b", "mla_wkv_b"}
    shards = {k: _to_bf16("cast_" + k, weights[k]) for k in big}
    first = {k: 0 if k.startswith("cd_") else 1 for k in big}
    a_keys = [k for k in big if first[k] and not k.startswith("ffn_")]
    b_keys = ["ffn_in", "ffn_out"]
    early = _all_gather("gather_wA", [shards[k][:1] for k in a_keys])
    (rn_g,) = _all_gather("gather_ret_norm", [ret_norm])
    small_first = [mods, rn_g] + list(early)
    wb_handle, wb_token = _exchange_start("gather_wB_start", [shards[k][:1] for k in b_keys], False, small_first)
    wc_handle, wc_token = _exchange_start("gather_wC_start", [shards[k][first[k]:] for k in big], False,
                                          small_first + [wb_token])
    mods = mods + (wb_token[0, 0] + wc_token[0, 0])

    def unshard(k, g):
        if k in col_sharded:
            f = jnp.transpose(g, (1, 2, 0, 3)).reshape(g.shape[1], g.shape[2], -1)
        else:
            f = jnp.transpose(g, (1, 0, 2, 3)).reshape(g.shape[1], -1, g.shape[3])
        return [f[i] for i in range(f.shape[0])]

    rn_full = jnp.transpose(rn_g, (1, 0, 2)).reshape(NO, -1)
    small_p = dict(weights, ret_norm=rn_full)
    full0 = {k: [None] * weights[k].shape[0] for k in big}
    for k, g in zip(a_keys, early):
        full0[k][:1] = unshard(k, g)

    def later_weights(stage, newest):
        if stage == 0:
            landed = _exchange_wait("gather_wB_wait", wb_handle, newest)
            landed = _fill_own("own_wB", landed, [shards[k][:1] for k in b_keys], False)
            for k, land in zip(b_keys, landed):
                full0[k][:1] = unshard(k, land)
        else:
            landed = _exchange_wait("gather_wC_wait", wc_handle, newest)
            landed = _fill_own("own_wC", landed, [shards[k][first[k]:] for k in big], False)
            for k, land in zip(big, landed):
                full0[k][first[k]:] = unshard(k, land)
        return _layer_weights(full0, small_p)

    def to_slots(k, gl):
        g = jnp.stack(gl)
        if k == "ffn_in":
            half = N_DEV // 2
            g = g.reshape(g.shape[0], 2, g.shape[2], half, -1)
            return jnp.transpose(g, (1, 3, 0, 2, 4)).reshape(N_DEV, g.shape[0], g.shape[2], -1)
        if k in col_sharded:
            return jnp.transpose(g.reshape(g.shape[0], g.shape[1], N_DEV, -1), (2, 0, 1, 3))
        return jnp.transpose(g.reshape(g.shape[0], N_DEV, -1, g.shape[2]), (1, 0, 2, 3))

    def big_grads(G):
        return dict(ffn_in=[g if g is None else _ffn_deinterleave(g) for g in G["ffn_in"]], ffn_out=G["ffn_out"],
                    ab_in=[g if g is None else _ab_unperm(g) for g in G["ab_in"]],
                    ab_out=G["ab_out"], cd_in=G["cd_in"], cd_out=G["cd_out"],
                    mla_wq_b=[g if g is None else _wq_unperm(g) for g in G["wq"]], mla_wkv_b=G["wkv"])

    sent = {}

    def early_grads(stage, G):
        parts = big_grads(G)
        if stage == 0:
            srcs = [to_slots(k, parts[k][first[k]:]) for k in big]
            handle, g_token = _exchange_start("scatter_gC_start", srcs, slotted=True)
        else:
            srcs = [to_slots(k, parts[k][:1]) for k in b_keys]
            handle, g_token = _exchange_start("scatter_gB_start", srcs, slotted=True)
        sent[stage] = (handle, srcs)
        return g_token[0, 0]

    loss_part, grad_x, dmods, G = _local_step(x, ctx, loss_target, mods, _layer_weights(full0, small_p),
                                              later_weights, early_grads)

    dmx = dmods[:, :, 0].reshape(NL, B, 6 * D)
    dmh = jnp.sum(dmods[:, :, 1], axis=1).reshape(NL, 1, 6 * D)
    (dm_g,) = _all_gather("gather_dmods", [jnp.concatenate([dmx, dmh], axis=1)])
    dmx_all = jnp.transpose(dm_g[:, :, :B], (1, 0, 2, 3)).reshape(NL, N_DEV * B, 6 * D)
    dmh_all = jnp.sum(dm_g[:, :, B], axis=0)
    dm_rows = jnp.concatenate([dmx_all, dmh_all[:, None], jnp.zeros((NL, 7, 6 * D), F32)], axis=1)
    g_ada_b = jnp.sum(dm_rows, axis=1)
    dm_mine = lax.dynamic_slice_in_dim(dm_rows, me * Ns, Ns, axis=2)
    g_ada_w, dcond = _ada_bwd(c_all, ada_w, dm_mine)
    sg = jax.nn.sigmoid(c_ctx)
    d_c_ctx_part = jnp.sum(dcond[:, N_DEV * B], axis=0) * (sg * (1.0 + c_ctx * (1.0 - sg)))

    def cat(parts):
        return jnp.concatenate([p.reshape(-1) for p in parts])

    dlg = jnp.stack([jnp.sum(G["lg"][j], axis=0) for j in range(NO)])
    sig_f, sig_b = jax.nn.sigmoid(-ret_decay_fwd), jax.nn.sigmoid(-ret_decay_bwd)
    small = dict(
        loss=loss_part[0, 0:1],
        c_ctx=d_c_ctx_part,
        norm_mix=cat(G["norm_mix"]), norm_ffn=cat(G["norm_ffn"]), norm_final=G["norm_final"].reshape(-1),
        mla_q_norm=cat(G["q_norm"]), mla_kv_norm=cat(G["kv_norm"]),
        cmlp_v_norm=cat([cat(G["v_norm"][j]) for j in range(NE)]),
        cmlp_ws=cat([jnp.stack(G["ws"][j]) for j in range(NE)]),
        cmlp_bs=cat([jnp.stack([b_[:, 0] for b_ in G["bs"][j]]) for j in range(NE)]),
        ret_decay_fwd=(dlg[:, 0] * sig_f).reshape(-1), ret_decay_bwd=(dlg[:, 1] * sig_b).reshape(-1),
        ret_norm=cat([cat(G["ret_norm"][j]) for j in range(NO)]),
        swa_sink=cat([jnp.sum(G["sink"][j], axis=0) for j in range(NO)]),
    )
    small_keys = list(small)
    sizes = [small[k].shape[0] for k in small_keys]
    total = sum(sizes)
    padded = -(-total // 2048) * 2048
    packed = jnp.concatenate([small[k] for k in small_keys] + [jnp.zeros((padded - total,), F32)]).reshape(-1, 128)
    (small_g,) = _all_gather("gather_small", [packed])

    parts = big_grads(G)
    landed0 = dict(zip(a_keys, _all_to_all("scatter_gA", [to_slots(k, parts[k][:1]) for k in a_keys])))
    landed_b = _fill_own("own_gB", _exchange_wait("scatter_gB_wait", sent[1][0], grad_x), sent[1][1], True)
    landed0.update(zip(b_keys, landed_b))
    landed_c = _fill_own("own_gC", _exchange_wait("scatter_gC_wait", sent[0][0], grad_x), sent[0][1], True)
    landed = [[landed0[k], rest] if first[k] else [rest] for k, rest in zip(big, landed_c)]

    grads, deltas, new_m, new_v = {}, {}, {}, {}
    for k, land in zip(big, landed):
        grads[k], deltas[k], new_m[k], new_v[k] = _adamw_from_slots("adamw_" + k, weights[k], moments_m[k], moments_v[k], land)
    deltas["ada_w"], new_m["ada_w"], new_v["ada_w"] = [
        o.reshape(ada_w.shape) for o in _adamw("adamw_ada_w", _flat(ada_w), _flat(g_ada_w), _flat(m_ada_w), _flat(v_ada_w))]
    grads["ada_w"] = g_ada_w

    sums_only = ("loss", "ret_norm")

    def packed_of(src, fill):
        vals = [jnp.full((sizes[i],), fill, F32) if k in sums_only else src[k].reshape(-1)
                for i, k in enumerate(small_keys)]
        return jnp.concatenate(vals + [jnp.full((padded - total,), fill, F32)]).reshape(-1, 128)

    w_p, m_p, v_p = packed_of(weights, 0.0), packed_of(moments_m, 0.0), packed_of(moments_v, 1.0)

    def f_small(w_, m_, v_, land_):
        g = _sum_slots(land_)
        return (g,) + _adamw_math(w_, g, m_, v_)

    g_p, d_p, nm_p, nv_p = _ew("adamw_small", f_small, [w_p, m_p, v_p, small_g], [F32] * 4)
    offs = np.cumsum([0] + sizes)
    for i, k in enumerate(small_keys):
        sl = slice(int(offs[i]), int(offs[i + 1]))
        if k == "loss":
            loss = g_p.reshape(-1)[int(offs[i])]
        elif k == "ret_norm":
            g_full = g_p.reshape(-1)[sl].reshape(NO, -1)
            g_mine = lax.dynamic_slice_in_dim(g_full, me * ret_norm.shape[1], ret_norm.shape[1], axis=1)
            d_, m_, v_ = _adamw("adamw_ret_norm", *[jnp.pad(a, ((0, 8 - NO), (0, 128 - a.shape[1])), constant_values=cv)
                                                     for a, cv in ((ret_norm, 0.0), (g_mine, 0.0), (m_ret_norm, 0.0), (v_ret_norm, 1.0))])
            grads[k] = g_mine
            deltas[k], new_m[k], new_v[k] = [a[:NO, :ret_norm.shape[1]] for a in (d_, m_, v_)]
        else:
            shp = weights[k].shape
            grads[k], deltas[k], new_m[k], new_v[k] = [a.reshape(-1)[sl].reshape(shp) for a in (g_p, d_p, nm_p, nv_p)]
    pad_b = lambda a, cv=0.0: jnp.pad(a, ((0, 8 - NL), (0, 0)), constant_values=cv)
    d_, m_, v_ = _adamw("adamw_ada_b", pad_b(ada_b), pad_b(g_ada_b), pad_b(m_ada_b), pad_b(v_ada_b, 1.0))
    grads["ada_b"] = g_ada_b
    deltas["ada_b"], new_m["ada_b"], new_v["ada_b"] = d_[:NL], m_[:NL], v_[:NL]

    return (loss, grad_x, *[grads[k] for k in order], *[deltas[k] for k in order],
            *[new_m[k] for k in order], *[new_v[k] for k in order])
```

```python
import functools

import numpy as np
import jax
import jax.numpy as jnp
from jax import lax
from jax.experimental import pallas as pl
from jax.experimental.pallas import tpu as pltpu

F32 = jnp.float32
BF16 = jnp.bfloat16
EPS = 1e-6
NEG_INF = -1e30
N_DEV = 8
GRID_W = 64
ROPE_THETA = 10000.0
ROPE_DIM = 64
SWA_WINDOW = 128
MLA_SCALE = (128 + 64) ** -0.5
SWA_SCALE = 64 ** -0.5
RET_K_SCALE = 64 ** -0.5
ADAM_LR, ADAM_B1, ADAM_B2, ADAM_EPS, ADAM_WD, ADAM_STEP = 0.001, 0.9, 0.999, 1e-08, 0.01, 10
V7X_VMEM_LIMIT = 56 * 1024 * 1024
MESH = pl.DeviceIdType.MESH


def _pallas(body, **kw):
    return pl.pallas_call(body, **kw)


def _params(sem=None):
    return pltpu.CompilerParams(dimension_semantics=sem, vmem_limit_bytes=V7X_VMEM_LIMIT)


def _tile(n, cap, align):
    best = None
    for t in range(align, min(n, cap) + 1, align):
        if n % t == 0:
            best = t
    return n if best is None else best


def _sds(shape, dtype):
    return jax.ShapeDtypeStruct(tuple(shape), dtype)


def _ew(name, f, ins, out_dtypes, cap_elems=131072):
    R, C = ins[0].shape[-2:]
    tr = _tile(R, max(16, cap_elems // C), 16)
    n_in = len(ins)

    def spec(a):
        if a.ndim == 2:
            return pl.BlockSpec((tr, C), lambda i: (i, 0))
        return pl.BlockSpec((a.shape[0], tr, C), lambda i: (0, i, 0))

    def body(*refs):
        outs = f(*[r[...] for r in refs[:n_in]])
        for r, o in zip(refs[n_in:], outs):
            r[...] = o.astype(r.dtype)

    return _pallas(
        body, name=name, grid=(R // tr,), in_specs=[spec(a) for a in ins],
        out_specs=[pl.BlockSpec((tr, C), lambda i: (i, 0)) for _ in out_dtypes],
        out_shape=[_sds((R, C), d) for d in out_dtypes], compiler_params=_params(("parallel",)),
    )(*ins)


def _to_bf16(name, w):
    w2 = w.reshape(-1, w.shape[-1])
    return _ew(name, lambda v: (v,), [w2], [BF16])[0].reshape(w.shape)


def _adamw_math(w, g, m, v):
    m = ADAM_B1 * m + (1.0 - ADAM_B1) * g
    v = ADAM_B2 * v + (1.0 - ADAM_B2) * (g * g)
    m_hat = m / (1.0 - ADAM_B1 ** ADAM_STEP)
    v_hat = v / (1.0 - ADAM_B2 ** ADAM_STEP)
    delta = -ADAM_LR * (m_hat / (jnp.sqrt(v_hat) + ADAM_EPS) + ADAM_WD * w)
    return delta, m, v


def _sum_slots(land):
    g = land[0].astype(F32)
    for s in range(1, land.shape[0]):
        g = g + land[s].astype(F32)
    return g


def _adamw_from_slots(name, w, m, v, lands):
    nl, K, C = w.shape
    tr = _tile(K, max(16, 65536 // C), 16)
    per = K // tr
    starts = np.cumsum([0] + [ld.shape[1] * per for ld in lands])
    ng = len(lands)

    def land_spec(g):
        lo, hi = int(starts[g]), int(starts[g + 1])
        return pl.BlockSpec((N_DEV, tr, C), lambda t: (0, jnp.clip(t, lo, hi - 1) - lo, 0))

    def body(*refs):
        w_ref, m_ref, v_ref = refs[:3]
        land_refs, out_refs = refs[3:3 + ng], refs[3 + ng:]
        t = pl.program_id(0)
        for g in range(ng):
            @pl.when((t >= int(starts[g])) & (t < int(starts[g + 1])))
            def _(g=g):
                grad = _sum_slots(land_refs[g][...])
                for r, o in zip(out_refs, (grad,) + _adamw_math(w_ref[...], grad, m_ref[...], v_ref[...])):
                    r[...] = o

    row = pl.BlockSpec((tr, C), lambda t: (t, 0))
    outs = _pallas(
        body, name=name, grid=(nl * per,), in_specs=[row] * 3 + [land_spec(g) for g in range(ng)],
        out_specs=[row] * 4, out_shape=[_sds((nl * K, C), F32)] * 4, compiler_params=_params(("parallel",)),
    )(w.reshape(-1, C), m.reshape(-1, C), v.reshape(-1, C), *[ld.reshape(N_DEV, -1, C) for ld in lands])
    return [o.reshape(w.shape) for o in outs]


def _adamw(name, w, g, m, v):
    outs = _ew(name, lambda w_, g_, m_, v_: _adamw_math(w_, g_, m_, v_), [w, g, m, v], [F32] * 3)
    return outs


def _mm(name, a, b, ta=False, tb=False, out_dtype=F32, add=None):
    M, K = (a.shape[1], a.shape[0]) if ta else a.shape
    N = b.shape[0] if tb else b.shape[1]
    tm = _tile(M, 1408, 128)
    tn = _tile(N, 1024, 128)
    if tn < 256 and N <= 2432:
        tn = N
    tk = _tile(K, 1792, 128)
    nk = K // tk
    a_spec = pl.BlockSpec((tk, tm), lambda i, j, k: (k, i)) if ta else pl.BlockSpec((tm, tk), lambda i, j, k: (i, k))
    b_spec = pl.BlockSpec((tn, tk), lambda i, j, k: (j, k)) if tb else pl.BlockSpec((tk, tn), lambda i, j, k: (k, j))
    o_spec = pl.BlockSpec((tm, tn), lambda i, j, k: (i, j))
    dims = (((0 if ta else 1,), (1 if tb else 0,)), ((), ()))
    has_add = add is not None

    def product(a_ref, b_ref):
        return lax.dot_general(a_ref[...].astype(BF16), b_ref[...].astype(BF16), dims, preferred_element_type=F32)

    def body_single(*refs):
        acc = product(refs[0], refs[1])
        if has_add:
            acc = acc + refs[2][...]
        refs[-1][...] = acc.astype(refs[-1].dtype)

    def body(*refs):
        a_ref, b_ref = refs[0], refs[1]
        add_ref = refs[2] if has_add else None
        o_ref, acc = refs[-2], refs[-1]
        k = pl.program_id(2)

        @pl.when(k == 0)
        def _():
            acc[...] = add_ref[...] if has_add else jnp.zeros_like(acc)

        acc[...] += product(a_ref, b_ref)

        @pl.when(k == nk - 1)
        def _():
            o_ref[...] = acc[...].astype(o_ref.dtype)

    ins = [a, b] + ([add] if has_add else [])
    specs = [a_spec, b_spec] + ([o_spec] if has_add else [])
    return _pallas(
        body_single if nk == 1 else body, name=name, grid=(M // tm, N // tn, nk), in_specs=specs, out_specs=o_spec,
        out_shape=_sds((M, N), out_dtype), scratch_shapes=[] if nk == 1 else [pltpu.VMEM((tm, tn), F32)],
        compiler_params=_params(("parallel", "parallel", "arbitrary")),
    )(*ins)


def _mm_pair(name, a1, a2, b, out_dtype):
    M, Kh = a1.shape
    N = b.shape[1]
    tm = _tile(M, 1408, 128)

    def body(a1_ref, a2_ref, b1_ref, b2_ref, o_ref):
        o_ref[...] = (_dot(a1_ref[...].astype(BF16), b1_ref[...].astype(BF16))
                      + _dot(a2_ref[...].astype(BF16), b2_ref[...].astype(BF16))).astype(o_ref.dtype)

    a_spec = pl.BlockSpec((tm, Kh), lambda i: (i, 0))
    return _pallas(
        body, name=name, grid=(M // tm,),
        in_specs=[a_spec, a_spec, pl.BlockSpec((Kh, N), lambda i: (0, 0)), pl.BlockSpec((Kh, N), lambda i: (1, 0))],
        out_specs=pl.BlockSpec((tm, N), lambda i: (i, 0)), out_shape=_sds((M, N), out_dtype),
        compiler_params=_params(("parallel",)),
    )(a1, a2, b, b)


FFN_TILE = 256


def _ffn_interleave(w):
    D, F2 = w.shape
    nj = F2 // (2 * FFN_TILE)

    def body(a_ref, b_ref, o_ref):
        o_ref[:, :FFN_TILE] = a_ref[...]
        o_ref[:, FFN_TILE:] = b_ref[...]

    return _pallas(
        body, name="ffn_interleave", grid=(nj,),
        in_specs=[pl.BlockSpec((D, FFN_TILE), lambda j: (0, j)), pl.BlockSpec((D, FFN_TILE), lambda j: (0, j + nj))],
        out_specs=pl.BlockSpec((D, 2 * FFN_TILE), lambda j: (0, j)), out_shape=_sds((D, F2), w.dtype),
        compiler_params=_params(("parallel",)),
    )(w, w)


def _ffn_deinterleave(w):
    D, F2 = w.shape
    nj = F2 // (2 * FFN_TILE)

    def body(w_ref, o_ref):
        o_ref[0] = w_ref[:, :FFN_TILE]
        o_ref[1] = w_ref[:, FFN_TILE:]

    return _pallas(
        body, name="ffn_deinterleave", grid=(nj,), in_specs=[pl.BlockSpec((D, 2 * FFN_TILE), lambda j: (0, j))],
        out_specs=pl.BlockSpec((2, D, FFN_TILE), lambda j: (0, 0, j)), out_shape=_sds((2, D, F2 // 2), w.dtype),
        compiler_params=_params(("parallel",)),
    )(w)


def _ffn_specs(M, D, F):
    tm = _tile(M, 1024, 128)
    x_spec = pl.BlockSpec((tm, D), lambda i, j: (i, 0))
    wi_spec = pl.BlockSpec((D, 2 * FFN_TILE), lambda i, j: (0, j))
    wo_spec = pl.BlockSpec((FFN_TILE, D), lambda i, j: (j, 0))
    return tm, F // FFN_TILE, x_spec, wi_spec, wo_spec


def _ffn_fwd(name, xn, w_in, w_out):
    M, D = xn.shape
    tm, nj, x_spec, wi_spec, wo_spec = _ffn_specs(M, D, w_out.shape[0])

    def body(x_ref, wi_ref, wo_ref, y_ref, acc):
        j = pl.program_id(1)
        z = _dot(x_ref[...], wi_ref[...])
        hid = (jax.nn.silu(z[:, :FFN_TILE]) * z[:, FFN_TILE:]).astype(BF16)
        part = _dot(hid, wo_ref[...])

        @pl.when(j == 0)
        def _():
            acc[...] = part

        @pl.when(j > 0)
        def _():
            acc[...] += part

        @pl.when(j == nj - 1)
        def _():
            y_ref[...] = acc[...].astype(y_ref.dtype)

    return _pallas(
        body, name=name, grid=(M // tm, nj), in_specs=[x_spec, wi_spec, wo_spec], out_specs=x_spec,
        out_shape=_sds((M, D), BF16), scratch_shapes=[pltpu.VMEM((tm, D), F32)],
        compiler_params=_params(("parallel", "arbitrary")),
    )(xn, w_in, w_out)


def _ffn_mid_bwd(name, xn, dy, w_in, w_out):
    M, D = xn.shape
    F = w_out.shape[0]
    tm, nj, x_spec, wi_spec, wo_spec = _ffn_specs(M, D, F)

    def body(x_ref, dy_ref, wi_ref, wo_ref, h_ref, dz_ref):
        z = _dot(x_ref[...], wi_ref[...])
        a, b = z[:, :FFN_TILE], z[:, FFN_TILE:]
        dh = _dot_nt(dy_ref[...], wo_ref[...])
        sig = jax.nn.sigmoid(a)
        act = a * sig
        h_ref[...] = (act * b).astype(BF16)
        dz_ref[:, :FFN_TILE] = (dh * b * (sig * (1.0 + a * (1.0 - sig)))).astype(BF16)
        dz_ref[:, FFN_TILE:] = (dh * act).astype(BF16)

    return _pallas(
        body, name=name, grid=(M // tm, nj), in_specs=[x_spec, x_spec, wi_spec, wo_spec],
        out_specs=[pl.BlockSpec((tm, FFN_TILE), lambda i, j: (i, j)),
                   pl.BlockSpec((tm, 2 * FFN_TILE), lambda i, j: (i, j))],
        out_shape=[_sds((M, F), BF16), _sds((M, 2 * F), BF16)],
        compiler_params=_params(("parallel", "parallel")),
    )(xn, dy, w_in, w_out)


class _Rows:
    def __init__(self, name, f, B, L, n, tm, arrays, pieces, samp, samp_pieces, glob, out_arrays, out_pieces):
        self.name, self.f, self.B, self.L, self.n, self.tm = name, f, B, L, n, tm
        self.arrays, self.pieces, self.samp, self.samp_pieces, self.glob = arrays, pieces, samp, samp_pieces, glob
        self.out_arrays, self.out_pieces = out_arrays, out_pieces
        self.nx = n // tm
        self.grid = (B, L // tm)

    def _row_spec(self, C, batched=True):
        tm = self.tm
        if batched:
            return pl.BlockSpec((None, tm, C), lambda b, i: (b, i, 0))
        return pl.BlockSpec((tm, C), lambda b, i: (i, 0))

    def _in_specs(self, unread=()):
        nx, tm = self.nx, self.tm
        specs = [self._row_spec(a.shape[-1], a.ndim == 3) for a in self.arrays]
        for ai in unread:
            specs[ai] = pl.BlockSpec((None, tm, self.arrays[ai].shape[-1]), lambda b, i: (0, 0, 0))
        specs += [pl.BlockSpec((None, None) + s.shape[2:], lambda b, i: (b, i // nx, 0, 0)) for s in self.samp]
        specs += [pl.BlockSpec(g.shape, lambda b, i, nd=g.ndim: (0,) * nd) for g in self.glob]
        return specs

    def _load(self, a_refs, s_refs, g_refs):
        args = [a_refs[ai][:, cs:cs + cw].astype(F32) for ai, cs, cw in self.pieces]
        args += [s_refs[si][r:r + 1, :].astype(F32) for si, r in self.samp_pieces]
        args += [g[...].astype(F32) for g in g_refs]
        return args

    def fwd(self):
        na, ns, ng = len(self.arrays), len(self.samp), len(self.glob)

        def body(*refs):
            a_refs, s_refs, g_refs = refs[:na], refs[na:na + ns], refs[na + ns:na + ns + ng]
            o_refs = refs[na + ns + ng:]
            outs = self.f(*self._load(a_refs, s_refs, g_refs))
            for (oi, cs, cw), o in zip(self.out_pieces, outs):
                o_refs[oi][:, cs:cs + cw] = o.astype(o_refs[oi].dtype)

        return _pallas(
            body, name=self.name + "_fwd", grid=self.grid, in_specs=self._in_specs(),
            out_specs=[self._row_spec(C) for C, _ in self.out_arrays],
            out_shape=[_sds((self.B, self.L, C), d) for C, d in self.out_arrays],
            compiler_params=_params(("parallel", "parallel")),
        )(*self.arrays, *self.samp, *self.glob)

    def bwd(self, cts, grad_arrays, grad_glob=(), add=None, samp=None, unread=()):
        samp = self.samp if samp is None else samp
        na, ns, ng, nc = len(self.arrays), len(self.samp), len(self.glob), len(cts)
        ct_off = [c[1] if isinstance(c, tuple) else 0 for c in cts]
        cts = [c[0] if isinstance(c, tuple) else c for c in cts]
        add = add or {}
        add_keys = list(add)
        g_idx = list(grad_arrays)
        nx, B = self.nx, self.B
        n_in = na + ns + ng + nc + len(add_keys)
        n_pieces, n_sp = len(self.pieces), len(self.samp_pieces)

        def body(*refs):
            a_refs, s_refs, g_refs = refs[:na], refs[na:na + ns], refs[na + ns:na + ns + ng]
            c_refs = refs[na + ns + ng:na + ns + ng + nc]
            add_refs = refs[na + ns + ng + nc:n_in]
            d_refs = refs[n_in:n_in + len(g_idx)]
            ds_refs = refs[n_in + len(g_idx):n_in + len(g_idx) + n_sp]
            dg_refs = refs[n_in + len(g_idx) + n_sp:]
            b, i = pl.program_id(0), pl.program_id(1)
            args = self._load(a_refs, s_refs, g_refs)
            _, vjp = jax.vjp(lambda *xs: tuple(self.f(*xs)), *args)
            grads = vjp(tuple(c_refs[oi][:, ct_off[oi] + cs:ct_off[oi] + cs + cw].astype(F32)
                              for oi, cs, cw in self.out_pieces))
            for k, ai in enumerate(g_idx):
                covered = sum(cw for pa, _, cw in self.pieces if pa == ai)
                if covered < self.arrays[ai].shape[-1]:
                    d_refs[k][...] = jnp.zeros_like(d_refs[k])
                for (pa, cs, cw), gr in zip(self.pieces, grads[:n_pieces]):
                    if pa == ai:
                        if ai in add:
                            gr = gr + add_refs[add_keys.index(ai)][:, cs:cs + cw]
                        d_refs[k][:, cs:cs + cw] = gr.astype(d_refs[k].dtype)

            @pl.when((i == 0) | (i == nx))
            def _():
                for r in ds_refs:
                    r[...] = jnp.zeros_like(r)

            for r, gr in zip(ds_refs, grads[n_pieces:n_pieces + n_sp]):
                r[...] += gr

            @pl.when((b == 0) & (i == 0))
            def _():
                for r in dg_refs:
                    r[...] = jnp.zeros_like(r)

            for r, gi in zip(dg_refs, grad_glob):
                r[...] += grads[n_pieces + n_sp + gi]

        in_specs = self._in_specs(unread) + [self._row_spec(c.shape[-1]) for c in cts]
        in_specs += [self._row_spec(add[k].shape[-1]) for k in add_keys]
        out_specs = [self._row_spec(self.arrays[ai].shape[-1]) for ai in g_idx]
        out_shape = [_sds(self.arrays[ai].shape, grad_arrays[ai]) for ai in g_idx]
        for si, _ in self.samp_pieces:
            C = self.samp[si].shape[-1]
            out_specs.append(pl.BlockSpec((None, None, 1, C), lambda b, i: (b, i // nx, 0, 0)))
            out_shape.append(_sds((B, 2, 1, C), F32))
        for gi in grad_glob:
            g = self.glob[gi]
            out_specs.append(pl.BlockSpec(g.shape, lambda b, i, nd=g.ndim: (0,) * nd))
            out_shape.append(_sds(g.shape, F32))
        outs = _pallas(
            body, name=self.name + "_bwd", grid=self.grid, in_specs=in_specs, out_specs=out_specs,
            out_shape=out_shape, compiler_params=_params(("arbitrary", "arbitrary")),
        )(*self.arrays, *samp, *self.glob, *cts, *[add[k] for k in add_keys])
        ng_ = len(g_idx)
        return outs[:ng_], outs[ng_:ng_ + n_sp], outs[ng_ + n_sp:]


def _rms(x, g):
    return x * lax.rsqrt(jnp.mean(x * x, axis=-1, keepdims=True) + EPS) * g


@jax.custom_vjp
def _swap_halves(x):
    w = x.shape[-1]
    lane = lax.broadcasted_iota(jnp.int32, x.shape, x.ndim - 1)
    up = pltpu.roll(x, w - ROPE_DIM // 2, x.ndim - 1)
    down = pltpu.roll(x, ROPE_DIM // 2, x.ndim - 1)
    return jnp.where(lane % ROPE_DIM < ROPE_DIM // 2, up, down)


_swap_halves.defvjp(lambda x: (_swap_halves(x), None), lambda _, ct: (_swap_halves(ct),))


@jax.custom_vjp
def _bdot(a, b):
    return jnp.dot(a.astype(BF16), b.astype(BF16), preferred_element_type=F32)


_bdot.defvjp(lambda a, b: (_bdot(a, b), (a, b)),
             lambda res, ct: (_dot_nt(ct.astype(BF16), res[1].astype(BF16)), _dot_tn(res[0].astype(BF16), ct.astype(BF16))))


def _rope(x, cos2, sin2):
    return x * cos2 + _swap_halves(x) * sin2


def _f_pre(s, shift, scale, g):
    return (_rms(s, g) * (1.0 + scale) + shift,)


def _f_res(s, y, gate):
    return (s + gate * y,)


def _f_res_pre(s, y, gate, shift, scale, g):
    s1 = s + gate * y
    return s1, _rms(s1, g) * (1.0 + scale) + shift


def _f_ab_pre(*args):
    kv_lat, q_lat, kpe = args[0:3]
    us, vs = args[3:7], args[7:11]
    cos2, sin2 = args[11:13]
    kv_norm, q_norm = args[13:15]
    vns, wss, bss = args[15:19], args[19:23], args[23:27]
    outs = [_rms(kv_lat, kv_norm), _rms(q_lat, q_norm), _rope(kpe, cos2, sin2)]
    for u, v, vn, ws, bs in zip(us, vs, vns, wss, bss):
        vg = _rms(jax.nn.gelu(v), vn)
        mixed = _bdot(ws, vg) + bs
        outs.append(jax.nn.gelu(u) * mixed)
    return tuple(outs)


def _f_q_rope(qn, qr, cos2, sin2):
    return qn, _rope(qr, cos2, sin2)


def _f_cd_pre(rk, rv, sk, sv, rq, rg, sq, c256, s256, c128, s128, c512, s512):
    return (_rope(rq, c256, s256), _rope(rk * RET_K_SCALE, c256, s256), rv,
            _rope(sq, c512, s512), _rope(sk, c128, s128), sv, rg)


def _f_cd_merge(*args):
    ys, rgs, rns = args[0:4], args[4:8], args[8:12]
    return tuple(_rms(y, rn) * jax.nn.silu(rg) for y, rg, rn in zip(ys, rgs, rns))


def _dot_nt(a, b):
    return lax.dot_general(a, b, (((1,), (1,)), ((), ())), preferred_element_type=F32)


def _dot_tn(a, b):
    return lax.dot_general(a, b, (((0,), (0,)), ((), ())), preferred_element_type=F32)


def _dot(a, b):
    return jnp.dot(a, b, preferred_element_type=F32)


def _tile_spec(tq, C):
    return pl.BlockSpec((None, tq, C), lambda b, i: (b, i, 0))


def _full_spec(L, C):
    return pl.BlockSpec((None, L, C), lambda b, i: (b, 0, 0))


def _mla_half(h):
    lane = lax.broadcasted_iota(jnp.int32, (1, 128), 1)
    return (lane < 64) if h % 2 == 0 else (lane >= 64)


def _mla_query(q_ref, h):
    pair = q_ref[:, 512 + 128 * (h // 2):640 + 128 * (h // 2)]
    return jnp.concatenate([q_ref[:, 128 * h:128 * h + 128], jnp.where(_mla_half(h), pair, jnp.zeros_like(pair))], axis=1)


def _mla_fill_keys(kcat, kv_ref, kpe_ref):
    kp = kpe_ref[...]
    for h in range(4):
        kcat[h, :, 0:128] = kv_ref[:, 256 * h:256 * h + 128]
        kcat[h, :, 128:256] = jnp.where(_mla_half(h), kp, jnp.zeros_like(kp))


def _mla_fwd(q, kv, kpe, n, tq):
    B, L, _ = q.shape

    def body(q_ref, kv_ref, kpe_ref, o_ref, lse_ref, kcat):
        i = pl.program_id(1)

        @pl.when(i == 0)
        def _():
            _mla_fill_keys(kcat, kv_ref, kpe_ref)

        def tile(keys):
            for h in range(4):
                s = _dot_nt(_mla_query(q_ref, h), kcat[h, keys, :]) * MLA_SCALE
                m = jnp.max(s, axis=1, keepdims=True)
                e = jnp.exp(s - m)
                l = jnp.sum(e, axis=1, keepdims=True)
                p = (e * (1.0 / l)).astype(BF16)
                o_ref[:, 128 * h:128 * h + 128] = _dot(p, kv_ref[keys, 256 * h + 128:256 * h + 256]).astype(o_ref.dtype)
                lse_ref[:, h:h + 1] = m + jnp.log(l)

        pl.when(i < n // tq)(functools.partial(tile, slice(0, L)))
        pl.when(i >= n // tq)(functools.partial(tile, slice(n, L)))

    return _pallas(
        body, name="mla_fwd", grid=(B, L // tq),
        in_specs=[_tile_spec(tq, 768), _full_spec(L, 1024), _full_spec(L, 128)],
        out_specs=[_tile_spec(tq, 512), _tile_spec(tq, 4)],
        out_shape=[_sds((B, L, 512), BF16), _sds((B, L, 4), F32)],
        scratch_shapes=[pltpu.VMEM((4, L, 256), BF16)],
        compiler_params=_params(("parallel", "arbitrary")),
    )(q, kv, kpe)


def _mla_bwd(q, kv, kpe, lse, do, n, tq):
    B, L, _ = q.shape

    def body(q_ref, kv_ref, kpe_ref, lse_ref, do_ref, dq_ref, dkv_ref, dkpe_ref, kcat):
        i = pl.program_id(1)

        @pl.when(i == 0)
        def _():
            dkv_ref[...] = jnp.zeros_like(dkv_ref)
            dkpe_ref[...] = jnp.zeros_like(dkpe_ref)
            _mla_fill_keys(kcat, kv_ref, kpe_ref)

        def tile(keys):
            rope_pair = None
            for h in range(4):
                qc, kc = _mla_query(q_ref, h), kcat[h, keys, :]
                v = kv_ref[keys, 256 * h + 128:256 * h + 256]
                p = jnp.exp(_dot_nt(qc, kc) * MLA_SCALE - lse_ref[:, h:h + 1])
                doh = do_ref[:, 128 * h:128 * h + 128].astype(BF16)
                dp = _dot_nt(doh, v)
                delta = jnp.sum(p * dp, axis=1, keepdims=True)
                ds = (p * (dp - delta) * MLA_SCALE).astype(BF16)
                dqc = _dot(ds, kc)
                dq_ref[:, 128 * h:128 * h + 128] = dqc[:, 0:128]
                if h % 2 == 0:
                    rope_pair = dqc[:, 128:256]
                else:
                    dq_ref[:, 512 + 128 * (h // 2):640 + 128 * (h // 2)] = jnp.where(_mla_half(h), dqc[:, 128:256], rope_pair)
                dkc = _dot_tn(ds, qc)
                dkv_ref[keys, 256 * h:256 * h + 128] += dkc[:, 0:128]
                dkpe_ref[keys, :] += dkc[:, 128:256]
                dkv_ref[keys, 256 * h + 128:256 * h + 256] += _dot_tn(p.astype(BF16), doh)

        pl.when(i < n // tq)(functools.partial(tile, slice(0, L)))
        pl.when(i >= n // tq)(functools.partial(tile, slice(n, L)))

    return _pallas(
        body, name="mla_bwd", grid=(B, L // tq),
        in_specs=[_tile_spec(tq, 768), _full_spec(L, 1024), _full_spec(L, 128), _tile_spec(tq, 4),
                  pl.BlockSpec((None, tq, 512), lambda b, i: (b, i, 0))],
        out_specs=[_tile_spec(tq, 768), _full_spec(L, 1024), _full_spec(L, 128)],
        out_shape=[_sds((B, L, 768), F32), _sds((B, L, 1024), F32), _sds((B, L, 128), F32)],
        scratch_shapes=[pltpu.VMEM((4, L, 256), BF16)],
        compiler_params=_params(("parallel", "arbitrary")),
    )(q, kv, kpe, lse, do)


def _swa_window(i, tq, n):
    W = min(tq + 2 * SWA_WINDOW, n)
    lo = jnp.clip(i * tq - SWA_WINDOW, 0, n - W)
    return pl.multiple_of(lo, 128), W


def _swa_mask(i, lo, tq, W, n):
    qpos = i * tq + lax.broadcasted_iota(jnp.int32, (tq, 1), 0)
    kpos = lo + lax.broadcasted_iota(jnp.int32, (1, W), 1)
    return (jnp.abs(qpos - kpos) <= SWA_WINDOW) & (qpos < n)


def _swa_fill(rep, src_ref):
    for g in range(2):
        rep[:, 256 * g:256 * g + 256] = jnp.concatenate([src_ref[:, 64 * g:64 * g + 64]] * 4, axis=1)


def _swa_head_rows(a):
    lane = lax.broadcasted_iota(jnp.int32, (1, 256), 1)
    return jnp.concatenate([jnp.where(lane // 64 == h, a, jnp.zeros_like(a)) for h in range(4)], axis=0)


def _swa_fold_rows(a, tq):
    lane = lax.broadcasted_iota(jnp.int32, (1, 256), 1)
    out = jnp.where(lane // 64 == 0, a[0:tq], 0.0)
    for h in range(1, 4):
        out = jnp.where(lane // 64 == h, a[h * tq:(h + 1) * tq], out)
    return out


def _swa_fwd(q, k, v, sink, n, tq):
    B, L, _ = q.shape

    def body(q_ref, k_ref, v_ref, sink_ref, o_ref, lse_ref, krep, vrep):
        i = pl.program_id(1)

        @pl.when(i == 0)
        def _():
            _swa_fill(krep, k_ref)
            _swa_fill(vrep, v_ref)

        def tile(band):
            for g in range(2):
                gc = slice(256 * g, 256 * g + 256)
                qs = _swa_head_rows(q_ref[:, gc])
                sk = jnp.concatenate([jnp.broadcast_to(sink_ref[0:1, 4 * g + h:4 * g + h + 1], (tq, 1))
                                      for h in range(4)], axis=0)
                s2 = _dot_nt(qs, krep[n:L, gc]) * SWA_SCALE
                m = jnp.maximum(jnp.max(s2, axis=1, keepdims=True), sk)
                if band:
                    lo, W = _swa_window(i, tq, n)
                    mask = jnp.concatenate([_swa_mask(i, lo, tq, W, n)] * 4, axis=0)
                    s1 = jnp.where(mask, _dot_nt(qs, krep[pl.ds(lo, W), gc]) * SWA_SCALE, NEG_INF)
                    m = jnp.maximum(m, jnp.max(s1, axis=1, keepdims=True))
                e2 = jnp.exp(s2 - m)
                l = jnp.sum(e2, axis=1, keepdims=True) + jnp.exp(sk - m)
                if band:
                    e1 = jnp.exp(s1 - m)
                    l = l + jnp.sum(e1, axis=1, keepdims=True)
                r = 1.0 / l
                o = _dot((e2 * r).astype(BF16), vrep[n:L, gc])
                if band:
                    o = o + _dot((e1 * r).astype(BF16), vrep[pl.ds(lo, W), gc])
                o_ref[:, gc] = _swa_fold_rows(o, tq).astype(o_ref.dtype)
                lse = m + jnp.log(l)
                for h in range(4):
                    lse_ref[:, 4 * g + h:4 * g + h + 1] = lse[h * tq:(h + 1) * tq]

        pl.when(i < n // tq)(functools.partial(tile, True))
        pl.when(i >= n // tq)(functools.partial(tile, False))

    return _pallas(
        body, name="swa_fwd", grid=(B, L // tq),
        in_specs=[_tile_spec(tq, 512), _full_spec(L, 128), _full_spec(L, 128), pl.BlockSpec((1, 8), lambda b, i: (0, 0))],
        out_specs=[_tile_spec(tq, 512), _tile_spec(tq, 8)],
        out_shape=[_sds((B, L, 512), BF16), _sds((B, L, 8), F32)],
        scratch_shapes=[pltpu.VMEM((L, 512), BF16), pltpu.VMEM((L, 512), BF16)],
        compiler_params=_params(("parallel", "arbitrary")),
    )(q, k, v, sink)


def _swa_bwd(q, k, v, sink, lse, do, n, tq):
    B, L, _ = q.shape

    def body(q_ref, k_ref, v_ref, sink_ref, lse_ref, do_ref, dq_ref, dk_ref, dv_ref, dsink_ref):
        i = pl.program_id(1)

        @pl.when(i == 0)
        def _():
            dk_ref[...] = jnp.zeros_like(dk_ref)
            dv_ref[...] = jnp.zeros_like(dv_ref)
            dsink_ref[...] = jnp.zeros_like(dsink_ref)

        def tile(band):
            kc, vc = k_ref[n:L, :], v_ref[n:L, :]
            if band:
                lo, W = _swa_window(i, tq, n)
                mask = _swa_mask(i, lo, tq, W, n)
                kl, vl = k_ref[pl.ds(lo, W), :], v_ref[pl.ds(lo, W), :]
            for h in range(8):
                g = h // 4
                cols = slice(64 * g, 64 * g + 64)
                qh = q_ref[:, 64 * h:64 * h + 64]
                lse_h = lse_ref[:, h:h + 1]
                doh = do_ref[:, 64 * h:64 * h + 64].astype(BF16)
                p2 = jnp.exp(_dot_nt(qh, kc[:, cols]) * SWA_SCALE - lse_h)
                ps = jnp.exp(sink_ref[0:1, h:h + 1] - lse_h)
                dp2 = _dot_nt(doh, vc[:, cols])
                delta = jnp.sum(p2 * dp2, axis=1, keepdims=True)
                if band:
                    p1 = jnp.exp(jnp.where(mask, _dot_nt(qh, kl[:, cols]) * SWA_SCALE, NEG_INF) - lse_h)
                    dp1 = _dot_nt(doh, vl[:, cols])
                    delta = delta + jnp.sum(p1 * dp1, axis=1, keepdims=True)
                ds2 = (p2 * (dp2 - delta) * SWA_SCALE).astype(BF16)
                dq = _dot(ds2, kc[:, cols])
                dk_ref[n:L, cols] += _dot_tn(ds2, qh)
                dv_ref[n:L, cols] += _dot_tn(p2.astype(BF16), doh)
                if band:
                    ds1 = (p1 * (dp1 - delta) * SWA_SCALE).astype(BF16)
                    dq = dq + _dot(ds1, kl[:, cols])
                    dk_ref[pl.ds(lo, W), cols] += _dot_tn(ds1, qh)
                    dv_ref[pl.ds(lo, W), cols] += _dot_tn(p1.astype(BF16), doh)
                dq_ref[:, 64 * h:64 * h + 64] = dq
                dsink_ref[0:1, h:h + 1] += jnp.sum(-ps * delta, axis=0, keepdims=True)

        pl.when(i < n // tq)(functools.partial(tile, True))
        pl.when(i >= n // tq)(functools.partial(tile, False))

    return _pallas(
        body, name="swa_bwd", grid=(B, L // tq),
        in_specs=[_tile_spec(tq, 512), _full_spec(L, 128), _full_spec(L, 128), pl.BlockSpec((1, 8), lambda b, i: (0, 0)),
                  _tile_spec(tq, 8), pl.BlockSpec((None, tq, 512), lambda b, i: (b, i, 1))],
        out_specs=[_tile_spec(tq, 512), _full_spec(L, 128), _full_spec(L, 128),
                   pl.BlockSpec((None, 1, 8), lambda b, i: (b, 0, 0))],
        out_shape=[_sds((B, L, 512), F32), _sds((B, L, 128), F32), _sds((B, L, 128), F32), _sds((B, 1, 8), F32)],
        compiler_params=_params(("parallel", "arbitrary")),
    )(q, k, v, sink, lse, do)


def _ret_decay(i, tq, L, n, ctx_tile):
    qi = i * tq + lax.broadcasted_iota(jnp.int32, (tq, 1), 0)
    kc = lax.broadcasted_iota(jnp.int32, (1, L - n), 1)
    d_hb = (n + kc - qi).astype(F32)
    if ctx_tile:
        return None, ((qi - L) - (kc - (L - n))).astype(F32), d_hb
    d_x = (qi - lax.broadcasted_iota(jnp.int32, (1, n), 1)).astype(F32)
    return d_x, (qi - (kc - (L - n))).astype(F32), d_hb


def _ret_head(a, h):
    lane = lax.broadcasted_iota(jnp.int32, (1, a.shape[1]), 1)
    return jnp.where(lane // 64 == h, a, jnp.zeros_like(a))


def _ret_mask(dist, lg):
    return jnp.where(dist >= 0.0, jnp.exp(lg * jnp.maximum(dist, 0.0)), 0.0)


def _ret_weights(n, L, d_x, d_hf, d_hb, lg_f, lg_b, with_grad):
    m_f, m_b = _ret_mask(d_hf, lg_f), _ret_mask(d_hb, lg_b)
    ctx = (slice(n, L), m_f + m_b)
    if with_grad:
        ctx += (m_f * jnp.maximum(d_hf, 0.0), m_b * jnp.maximum(d_hb, 0.0))
    if d_x is None:
        return [ctx]
    e_x = jnp.exp(jnp.where(d_x >= 0.0, lg_f, -lg_b) * d_x)
    lat = (slice(0, n), jnp.where(d_x == 0.0, 2.0 * e_x, e_x))
    if with_grad:
        lat += (e_x * jnp.maximum(d_x, 0.0), e_x * jnp.maximum(-d_x, 0.0))
    return [lat, ctx]


def _ret_fwd(q, k, v, lg, n, tq):
    B, L, _ = q.shape

    def body(q_ref, k_ref, v_ref, lg_ref, y_ref):
        i = pl.program_id(1)

        def tile(ctx_tile):
            dist = _ret_decay(i, tq, L, n, ctx_tile)
            for h in range(4):
                qh = _ret_head(q_ref[...], h)
                y = None
                for rows, dec_r in _ret_weights(n, L, *dist, lg_ref[0:1, h:h + 1], lg_ref[1:2, h:h + 1], False):
                    a = _dot_nt(qh, k_ref[rows, :]) * dec_r
                    part = _dot(a.astype(BF16), v_ref[rows, 128 * h:128 * h + 128])
                    y = part if y is None else y + part
                y_ref[:, 128 * h:128 * h + 128] = y

        pl.when(i < n // tq)(functools.partial(tile, False))
        pl.when(i >= n // tq)(functools.partial(tile, True))

    return _pallas(
        body, name="ret_fwd", grid=(B, L // tq),
        in_specs=[_tile_spec(tq, 256), _full_spec(L, 256), _full_spec(L, 512), pl.BlockSpec((2, 4), lambda b, i: (0, 0))],
        out_specs=_tile_spec(tq, 512), out_shape=_sds((B, L, 512), F32),
        compiler_params=_params(("parallel", "arbitrary")),
    )(q, k, v, lg)


def _ret_bwd(q, k, v, lg, dy, n, tq):
    B, L, _ = q.shape

    def body(q_ref, k_ref, v_ref, lg_ref, dy_ref, dq_ref, dk_ref, dv_ref, dlg_ref):
        i = pl.program_id(1)

        @pl.when(i == 0)
        def _():
            dk_ref[...] = jnp.zeros_like(dk_ref)
            dv_ref[...] = jnp.zeros_like(dv_ref)
            dlg_ref[...] = jnp.zeros_like(dlg_ref)

        def total(a):
            return jnp.sum(jnp.sum(a, axis=1, keepdims=True), axis=0, keepdims=True)

        def tile(ctx_tile):
            dist = _ret_decay(i, tq, L, n, ctx_tile)
            dq = None
            for h in range(4):
                vc = slice(128 * h, 128 * h + 128)
                qh = _ret_head(q_ref[...], h)
                dyh = dy_ref[:, vc].astype(BF16)
                dqh = None
                for rows, dec_r, wf_r, wb_r in _ret_weights(n, L, *dist, lg_ref[0:1, h:h + 1], lg_ref[1:2, h:h + 1], True):
                    k_all, vh = k_ref[rows, :], v_ref[rows, vc]
                    s = _dot_nt(qh, k_all)
                    gr = _dot_nt(dyh, vh)
                    ds = (gr * dec_r).astype(BF16)
                    part = _dot(ds, k_all)
                    dqh = part if dqh is None else dqh + part
                    dk_ref[rows, :] += _dot_tn(ds, qh)
                    dv_ref[rows, vc] += _dot_tn((s * dec_r).astype(BF16), dyh)
                    gs = gr * s
                    dlg_ref[0:1, h:h + 1] += total(gs * wf_r)
                    dlg_ref[1:2, h:h + 1] += total(gs * wb_r)
                dqh = _ret_head(dqh, h)
                dq = dqh if dq is None else dq + dqh
            dq_ref[...] = dq

        pl.when(i < n // tq)(functools.partial(tile, False))
        pl.when(i >= n // tq)(functools.partial(tile, True))

    return _pallas(
        body, name="ret_bwd", grid=(B, L // tq),
        in_specs=[_tile_spec(tq, 256), _full_spec(L, 256), _full_spec(L, 512), pl.BlockSpec((2, 4), lambda b, i: (0, 0)),
                  _tile_spec(tq, 512)],
        out_specs=[_tile_spec(tq, 256), _full_spec(L, 256), _full_spec(L, 512),
                   pl.BlockSpec((None, 2, 4), lambda b, i: (b, 0, 0))],
        out_shape=[_sds((B, L, 256), F32), _sds((B, L, 256), F32), _sds((B, L, 512), F32), _sds((B, 2, 4), F32)],
        compiler_params=_params(("parallel", "arbitrary")),
    )(q, k, v, lg, dy)


def _loss_head(s, target, g, n, tm):
    B, L, D = s.shape
    nx = n // tm

    def body(s_ref, t_ref, g_ref, ds_ref, dg_ref, loss_ref):
        b, i = pl.program_id(0), pl.program_id(1)

        @pl.when((b == 0) & (i == 0))
        def _():
            dg_ref[...] = jnp.zeros_like(dg_ref)
            loss_ref[...] = jnp.zeros_like(loss_ref)

        @pl.when(i < nx)
        def _():
            y, vjp = jax.vjp(_rms, s_ref[...], g_ref[...])
            err = y - t_ref[...]
            d_s, d_g = vjp(err * (1.0 / D))
            ds_ref[...] = d_s
            dg_ref[...] += d_g
            part = jnp.sum(jnp.sum(err * err, axis=1, keepdims=True), axis=0, keepdims=True) * (0.5 / D)
            loss_ref[...] += jnp.broadcast_to(part, loss_ref.shape)

        @pl.when(i >= nx)
        def _():
            ds_ref[...] = jnp.zeros_like(ds_ref)

    return _pallas(
        body, name="loss_head", grid=(B, L // tm),
        in_specs=[pl.BlockSpec((None, tm, D), lambda b, i: (b, i, 0)),
                  pl.BlockSpec((None, tm, D), lambda b, i: (b, jnp.minimum(i, nx - 1), 0)),
                  pl.BlockSpec((1, D), lambda b, i: (0, 0))],
        out_specs=[pl.BlockSpec((None, tm, D), lambda b, i: (b, i, 0)), pl.BlockSpec((1, D), lambda b, i: (0, 0)),
                   pl.BlockSpec((1, 128), lambda b, i: (0, 0))],
        out_shape=[_sds((B, L, D), F32), _sds((1, D), F32), _sds((1, 128), F32)],
        compiler_params=_params(("arbitrary", "arbitrary")),
    )(s, target, g)


def _ada_fwd(c_all, ada_w, ada_b):
    NL, D, Ns = ada_w.shape
    R = c_all.shape[0]

    def body(c_ref, w_ref, b_ref, o_ref):
        cond = jax.nn.silu(c_ref[...]).astype(BF16)
        o_ref[...] = _dot(cond, w_ref[...].astype(BF16)) + b_ref[...]

    return _pallas(
        body, name="ada_fwd", grid=(NL,),
        in_specs=[pl.BlockSpec((R, D), lambda l: (0, 0)), pl.BlockSpec((None, D, Ns), lambda l: (l, 0, 0)),
                  pl.BlockSpec((None, 1, Ns), lambda l: (l, 0, 0))],
        out_specs=pl.BlockSpec((None, R, Ns), lambda l: (l, 0, 0)), out_shape=_sds((NL, R, Ns), F32),
        compiler_params=_params(("parallel",)),
    )(c_all, ada_w, ada_b)


def _ada_bwd(c_all, ada_w, dmods):
    NL, D, Ns = ada_w.shape
    R = c_all.shape[0]

    def body(c_ref, w_ref, dm_ref, dw_ref, dc_ref):
        cond = jax.nn.silu(c_ref[...]).astype(BF16)
        dm = dm_ref[...].astype(BF16)
        dw_ref[...] = _dot_tn(cond, dm)
        dc_ref[...] = _dot_nt(dm, w_ref[...].astype(BF16))

    return _pallas(
        body, name="ada_bwd", grid=(NL,),
        in_specs=[pl.BlockSpec((R, D), lambda l: (0, 0)), pl.BlockSpec((None, D, Ns), lambda l: (l, 0, 0)),
                  pl.BlockSpec((None, R, Ns), lambda l: (l, 0, 0))],
        out_specs=[pl.BlockSpec((None, D, Ns), lambda l: (l, 0, 0)), pl.BlockSpec((None, R, D), lambda l: (l, 0, 0))],
        out_shape=[_sds((NL, D, Ns), F32), _sds((NL, R, D), F32)],
        compiler_params=_params(("parallel",)),
    )(c_all, ada_w, dmods)


def _my_index():
    return 4 * lax.axis_index("x") + 2 * lax.axis_index("y") + lax.axis_index("c")


def _peer(k):
    x, y, c = lax.axis_index("x"), lax.axis_index("y"), lax.axis_index("c")
    kx, ky, kc = (k >> 2) & 1, (k >> 1) & 1, k & 1
    px, py, pc = (x + kx) % 2, (y + ky) % 2, (c + kc) % 2
    return (px, py, pc), 4 * px + 2 * py + pc


def _all_gather(name, shards):
    na = len(shards)
    hbm = pl.BlockSpec(memory_space=pl.ANY)

    def body(*refs):
        in_refs, out_refs = refs[:na], refs[na:2 * na]
        send_sems, recv_sems, local_sems = refs[2 * na:]
        me = _my_index()
        sib_id, sib = _peer(1)
        chips = [_peer(k) for k in (4, 2, 6)]
        sib_chips = [4 * px + 2 * py + (1 - pc) for (px, py, pc), _ in chips]

        def copy(a, k, slot, to, src=None):
            dst = out_refs[a].at[slot]
            return pltpu.make_async_remote_copy(
                src_ref=dst if src is None else src, dst_ref=dst, send_sem=send_sems.at[a, k],
                recv_sem=recv_sems.at[a, k], device_id=to, device_id_type=MESH)

        first, passed, mine = [], [], []
        for a in range(na):
            cp = pltpu.make_async_copy(in_refs[a], out_refs[a].at[me], local_sems.at[a])
            cp.start()
            mine.append(cp)
            first.append(copy(a, 0, me, sib_id, src=in_refs[a]))
            first += [copy(a, 1 + j, me, pid, src=in_refs[a]) for j, (pid, _) in enumerate(chips)]
        for cp in first:
            cp.start()
        for a in range(na):
            for j, (pid, pidx) in enumerate(chips):
                copy(a, 1 + j, pidx, pid).wait_recv()
                fwd = copy(a, 4 + j, pidx, sib_id)
                fwd.start()
                passed.append(fwd)
        for a in range(na):
            copy(a, 0, sib, sib_id).wait_recv()
            for j in range(3):
                copy(a, 4 + j, sib_chips[j], sib_id).wait_recv()
        for cp in first + passed:
            cp.wait_send()
        for cp in mine:
            cp.wait()

    return _pallas(
        body, name=name, in_specs=[hbm] * na, out_specs=[hbm] * na,
        out_shape=[_sds((N_DEV,) + s.shape, s.dtype) for s in shards],
        scratch_shapes=[pltpu.SemaphoreType.DMA((na, 7)), pltpu.SemaphoreType.DMA((na, 7)),
                        pltpu.SemaphoreType.DMA((na,))],
    )(*shards)


def _all_to_all(name, parts):
    na = len(parts)
    hbm = pl.BlockSpec(memory_space=pl.ANY)

    def body(*refs):
        in_refs, out_refs = refs[:na], refs[na:2 * na]
        send_sems, recv_sems, local_sems = refs[2 * na:]
        me = _my_index()
        copies = []
        for a in range(na):
            cp = pltpu.make_async_copy(in_refs[a].at[me], out_refs[a].at[me], local_sems.at[a])
            cp.start()
            copies.append(cp)
            for k in range(1, N_DEV):
                pid, pidx = _peer(k)
                cp = pltpu.make_async_remote_copy(
                    src_ref=in_refs[a].at[pidx], dst_ref=out_refs[a].at[me], send_sem=send_sems.at[a, k - 1],
                    recv_sem=recv_sems.at[a, k - 1], device_id=pid, device_id_type=MESH)
                cp.start()
                copies.append(cp)
        for cp in copies:
            cp.wait()

    return _pallas(
        body, name=name, in_specs=[hbm] * na, out_specs=[hbm] * na,
        out_shape=[_sds(p.shape, p.dtype) for p in parts],
        scratch_shapes=[pltpu.SemaphoreType.DMA((na, 7)), pltpu.SemaphoreType.DMA((na, 7)),
                        pltpu.SemaphoreType.DMA((na,))],
    )(*parts)


_HBM = pl.BlockSpec(memory_space=pltpu.HBM)
_SEM = pl.BlockSpec(memory_space=pltpu.SEMAPHORE)
_DATAFLOW = pltpu.SideEffectType.DATAFLOW_SIDE_EFFECTING


def _exchange_start(name, srcs, slotted, after=()):
    na = len(srcs)
    lands = [lax.empty((N_DEV,) + (s.shape[1:] if slotted else s.shape), s.dtype) for s in srcs]

    def body(*refs):
        src_refs, land_refs = refs[:na], refs[na:2 * na]
        outs = refs[2 * na + len(after):]
        send_sems, recv_sems, token = outs[:na], outs[na:2 * na], outs[4 * na]
        me = _my_index()
        for a in range(na):
            for k in range(1, N_DEV):
                pid, pidx = _peer(k)
                pltpu.make_async_remote_copy(
                    src_ref=src_refs[a].at[pidx] if slotted else src_refs[a], dst_ref=land_refs[a].at[me],
                    send_sem=send_sems[a], recv_sem=recv_sems[a], device_id=pid, device_id_type=MESH).start()
        token[...] = jnp.zeros_like(token)

    ops = [pltpu.with_memory_space_constraint(a, pltpu.HBM) for a in list(srcs) + lands]
    outs = _pallas(
        body, name=name,
        out_shape=[pltpu.SemaphoreType.DMA(())] * (2 * na) + [pltpu.HBM(a.shape, a.dtype) for a in ops]
        + [_sds((8, 128), F32)],
        in_specs=[_HBM] * (2 * na) + [pl.BlockSpec(memory_space=pl.ANY)] * len(after),
        out_specs=[_SEM] * (2 * na) + [_HBM] * (2 * na) + [pl.BlockSpec(memory_space=pltpu.VMEM)],
        input_output_aliases={a: 2 * na + a for a in range(2 * na)},
        compiler_params=pltpu.CompilerParams(has_side_effects=_DATAFLOW),
    )(*ops, *after)
    return (na, outs[:2 * na], outs[2 * na:4 * na]), outs[4 * na]


def _exchange_wait(name, handle, after):
    na, sems, thru = handle

    def body(*refs):
        land_refs = refs[na:2 * na]
        send_sems, recv_sems = refs[2 * na:3 * na], refs[3 * na:4 * na]
        me_id = (lax.axis_index("x"), lax.axis_index("y"), lax.axis_index("c"))
        for a in range(na):
            seven = land_refs[a].at[pl.ds(0, N_DEV - 1)]
            drain = pltpu.make_async_remote_copy(src_ref=seven, dst_ref=seven, send_sem=send_sems[a],
                                                 recv_sem=recv_sems[a], device_id=me_id, device_id_type=MESH)
            drain.wait_send()
            drain.wait_recv()

    outs = _pallas(
        body, name=name, out_shape=[pltpu.HBM(a.shape, a.dtype) for a in thru],
        in_specs=[_HBM] * (2 * na) + [_SEM] * (2 * na) + [pl.BlockSpec(memory_space=pl.ANY)],
        out_specs=[_HBM] * (2 * na), input_output_aliases={a: a for a in range(2 * na)},
        compiler_params=pltpu.CompilerParams(has_side_effects=_DATAFLOW),
    )(*thru, *sems, after)
    return outs[na:]


def _fill_own(name, landed, owns, slotted):
    me = _my_index()
    return [jnp.where(lax.broadcasted_iota(jnp.int32, land.shape, 0) == me, own, land)
            for land, own in zip(landed, owns)]


def _rope_tables(n, L, width):
    t = jnp.arange(n)
    row = (t // GRID_W).astype(F32)
    col = (t % GRID_W).astype(F32)
    n_freq = ROPE_DIM // 4
    freqs = ROPE_THETA ** (-jnp.arange(n_freq, dtype=F32) / n_freq)
    ang = jnp.concatenate([row[:, None] * freqs, col[:, None] * freqs], axis=-1)
    cos, sin = jnp.cos(ang), jnp.sin(ang)
    cos2 = jnp.concatenate([cos, cos], axis=-1)
    sin2 = jnp.concatenate([-sin, sin], axis=-1)
    cos2 = jnp.concatenate([cos2, jnp.ones((L - n, ROPE_DIM), F32)], axis=0)
    sin2 = jnp.concatenate([sin2, jnp.zeros((L - n, ROPE_DIM), F32)], axis=0)
    reps = width // ROPE_DIM
    return jnp.tile(cos2, (1, reps)), jnp.tile(sin2, (1, reps))


def _ab_perm(w):
    return jnp.concatenate([w[:, 0:256], w[:, 320:1600], w[:, 256:320], w[:, 256:320]], axis=1)


def _ab_unperm(g):
    rope_key = (g[:, 1536:1600].astype(F32) + g[:, 1600:1664].astype(F32)).astype(g.dtype)
    return jnp.concatenate([g[:, 0:256], rope_key, g[:, 256:1536]], axis=1)


def _wq_perm(w):
    return jnp.concatenate([w[:, 192 * h:192 * h + 128] for h in range(4)]
                           + [w[:, 192 * h + 128:192 * h + 192] for h in range(4)], axis=1)


def _wq_unperm(g):
    return jnp.concatenate([g[:, sl] for h in range(4)
                            for sl in (slice(128 * h, 128 * h + 128), slice(512 + 64 * h, 576 + 64 * h))], axis=1)


def _flat(a):
    return a.reshape(-1, a.shape[-1])


def _layer_weights(full, p):
    NL, NE, NO = len(full["ffn_in"]), len(full["ab_in"]), len(full["cd_in"])
    groups = range(4)

    def each(f, mats):
        return [None if w is None else f(w) for w in mats]

    return dict(
        norm_mix=[p["norm_mix"][l][None] for l in range(NL)], norm_ffn=[p["norm_ffn"][l][None] for l in range(NL)],
        norm_final=p["norm_final"][None],
        ffn_in=each(_ffn_interleave, full["ffn_in"]), ffn_out=list(full["ffn_out"]),
        ab_in=each(_ab_perm, full["ab_in"]), ab_out=list(full["ab_out"]),
        wq=each(_wq_perm, full["mla_wq_b"]), wkv=list(full["mla_wkv_b"]),
        kv_norm=[p["mla_kv_norm"][j][None] for j in range(NE)], q_norm=[p["mla_q_norm"][j][None] for j in range(NE)],
        v_norm=[[p["cmlp_v_norm"][j][None, 128 * g:128 * g + 128] for g in groups] for j in range(NE)],
        ws=[[p["cmlp_ws"][j, g] for g in groups] for j in range(NE)],
        bs=[[p["cmlp_bs"][j, g][:, None] for g in groups] for j in range(NE)],
        cd_in=list(full["cd_in"]), cd_out=list(full["cd_out"]),
        lg=[jnp.stack([jax.nn.log_sigmoid(p["ret_decay_fwd"][j]), jax.nn.log_sigmoid(p["ret_decay_bwd"][j])])
            for j in range(NO)],
        sink=[p["swa_sink"][j][None] for j in range(NO)],
        ret_norm=[[p["ret_norm"][j][None, 128 * g:128 * g + 128] for g in groups] for j in range(NO)],
    )


def _local_step(x, ctx, target, mods, W, later_weights=None, early_grads=None):
    B, n, D = x.shape
    m = ctx.shape[1]
    L = n + m
    NL = mods.shape[0]
    tm = min(256, m)
    tq = min(256, m)
    cos512, sin512 = _rope_tables(n, L, 512)
    s = jnp.concatenate([x, ctx], axis=1)
    saved = []

    def rows(name, f, arrays, pieces, samp, samp_pieces, glob, out_arrays, out_pieces, tile=tm):
        return _Rows(name, f, B, L, n, tile, arrays, pieces, samp, samp_pieces, glob, out_arrays, out_pieces)

    def full(width, start=0):
        return [(0, start, width)]

    for l in range(NL):
        j = l // 2
        even = l % 2 == 0
        md = mods[l]
        if l == 1 and later_weights is not None:
            W = later_weights(1, s)
        r = {}
        if l == 0:
            r["pre1"] = rows("pre_mix0", _f_pre, [s], full(D), [md], [(0, 0), (0, 1)], [W["norm_mix"][0]], [(D, BF16)], full(D))
            (xn,) = r["pre1"].fwd()
        r["xn"] = xn
        if even:
            z = _mm(f"ab_in{l}", _flat(xn), W["ab_in"][j]).reshape(B, L, 1664)
            pieces = [(0, 0, 256), (0, 256, 256), (0, 1536, 128)]
            pieces += [(0, 512 + 128 * g, 128) for g in range(4)] + [(0, 1024 + 128 * g, 128) for g in range(4)]
            pieces += [(1, 0, 128), (2, 0, 128)]
            glob = [W["kv_norm"][j], W["q_norm"][j]] + W["v_norm"][j] + W["ws"][j] + W["bs"][j]
            abp = rows(f"ab_pre{l}", _f_ab_pre, [z, cos512, sin512], pieces, [], [], glob,
                       [(256, BF16), (256, BF16), (128, BF16), (512, BF16)],
                       [(0, 0, 256), (1, 0, 256), (2, 0, 128)] + [(3, 128 * g, 128) for g in range(4)], tile=128)
            kvn, qn, kpe, cm = abp.fwd()
            kv = _mm(f"wkv{l}", _flat(kvn), W["wkv"][j], out_dtype=BF16).reshape(B, L, 1024)
            q0 = _mm(f"wq{l}", _flat(qn), W["wq"][j]).reshape(B, L, 768)
            qrp = rows(f"q_rope{l}", _f_q_rope, [q0, cos512, sin512], [(0, 0, 512), (0, 512, 256), (1, 0, 256), (2, 0, 256)],
                       [], [], [], [(768, BF16)], [(0, 0, 512), (0, 512, 256)])
            (q,) = qrp.fwd()
            o, lse = _mla_fwd(q, kv, kpe, n, tq)
            y = _mm_pair(f"ab_out{l}", _flat(o), _flat(cm), W["ab_out"][j], BF16).reshape(B, L, D)
            r.update(z=z, abp=abp, kvn=kvn, qn=qn, kpe=kpe, cm=cm, kv=kv, qrp=qrp, q=q, o=o, lse=lse)
        else:
            z = _mm(f"cd_in{l}", _flat(xn), W["cd_in"][j]).reshape(B, L, 2304)
            pieces = [(0, 0, 256), (0, 256, 512), (0, 768, 128), (0, 896, 128), (0, 1024, 256), (0, 1280, 512), (0, 1792, 512)]
            pieces += [(1, 0, 256), (2, 0, 256), (1, 0, 128), (2, 0, 128), (1, 0, 512), (2, 0, 512)]
            cdp = rows(f"cd_pre{l}", _f_cd_pre, [z, cos512, sin512], pieces, [], [], [],
                       [(256, BF16), (256, BF16), (512, BF16), (512, BF16), (128, BF16), (128, BF16), (512, F32)],
                       [(k_, 0, w_) for k_, w_ in enumerate((256, 256, 512, 512, 128, 128, 512))])
            rq, rk, rv, sq, sk, sv, rg = cdp.fwd()
            yret = _ret_fwd(rq, rk, rv, W["lg"][j], n, tq)
            osw, lse = _swa_fwd(sq, sk, sv, W["sink"][j], n, tq)
            mrg = rows(f"cd_merge{l}", _f_cd_merge, [yret, rg],
                       [(0, 128 * g, 128) for g in range(4)] + [(1, 128 * g, 128) for g in range(4)], [], [],
                       W["ret_norm"][j], [(512, BF16)], [(0, 128 * g, 128) for g in range(4)])
            (yr,) = mrg.fwd()
            y = _mm_pair(f"cd_out{l}", _flat(yr), _flat(osw), W["cd_out"][j], BF16).reshape(B, L, D)
            r.update(z=z, cdp=cdp, rq=rq, rk=rk, rv=rv, sq=sq, sk=sk, sv=sv, rg=rg, yret=yret, osw=osw, lse=lse,
                     mrg=mrg, yr=yr)
        two, outs2 = [(0, 0, D), (1, 0, D)], [(D, F32), (D, BF16)]
        r["mix_out"] = rows(f"mix_res_pre{l}", _f_res_pre, [s, y], two, [md], [(0, 2), (0, 3), (0, 4)],
                            [W["norm_ffn"][l]], outs2, two)
        s1, xn2 = r["mix_out"].fwd()
        if l == 0 and later_weights is not None:
            W = later_weights(0, s1)
        y2 = _ffn_fwd(f"ffn{l}", _flat(xn2), W["ffn_in"][l], W["ffn_out"][l]).reshape(B, L, D)
        if l < NL - 1:
            r["ffn_out"] = rows(f"ffn_res_pre{l}", _f_res_pre, [s1, y2], two, [md, mods[l + 1]], [(0, 5), (1, 0), (1, 1)],
                                [W["norm_mix"][l + 1]], outs2, two)
            s, xn = r["ffn_out"].fwd()
        else:
            r["ffn_out"] = rows(f"res_ffn{l}", _f_res, [s1, y2], two, [md], [(0, 5)], [], [(D, F32)], full(D))
            (s,) = r["ffn_out"].fwd()
        r["xn2"] = xn2
        saved.append(r)

    ds, d_norm_final, loss = _loss_head(s, target, W["norm_final"], n, tm)

    G = {k: [None] * len(v) for k, v in W.items() if isinstance(v, list)}
    G["norm_final"] = d_norm_final
    dm = [[None] * 6 for _ in range(NL)]
    dxn_next = None
    for l in reversed(range(NL)):
        j = l // 2
        even = l % 2 == 0
        r = saved[l]
        zero = early_grads(0, G) if l == 0 and early_grads is not None else None
        if l == NL - 1:
            (ds1, dy2), (dm[l][5],), _ = r["ffn_out"].bwd([ds], {0: F32, 1: BF16}, unread=(0,))
        else:
            smp = None if zero is None else [mods[l] + zero, mods[l + 1] + zero]
            (ds1, dy2), (dm[l][5], dm[l + 1][0], dm[l + 1][1]), (G["norm_mix"][l + 1],) = r["ffn_out"].bwd(
                [ds, dxn_next], {0: F32, 1: BF16}, grad_glob=(0,), samp=smp)
        dy2f, xn2f = _flat(dy2), _flat(r["xn2"])
        hid, dz2 = _ffn_mid_bwd(f"ffn_mid{l}", xn2f, dy2f, W["ffn_in"][l], W["ffn_out"][l])
        G["ffn_out"][l] = _mm(f"g_ffn_out{l}", hid, dy2f, ta=True, out_dtype=BF16)
        G["ffn_in"][l] = _mm(f"g_ffn_in{l}", xn2f, dz2, ta=True, out_dtype=BF16)
        dxn2 = _mm(f"d_xn2{l}", dz2, W["ffn_in"][l], tb=True, out_dtype=BF16).reshape(B, L, D)
        zero = early_grads(1, G) if l == 0 and early_grads is not None else None
        smp = None if zero is None else [mods[l] + zero]
        (ds0, dy), (dm[l][2], dm[l][3], dm[l][4]), (G["norm_ffn"][l],) = r["mix_out"].bwd(
            [ds1, dxn2], {0: F32, 1: BF16}, grad_glob=(0,), samp=smp)
        dyf = _flat(dy)
        if even:
            w_out = W["ab_out"][j]
            dcat = _mm(f"d_cat{l}", dyf, w_out, tb=True, out_dtype=BF16).reshape(B, L, -1)
            G["ab_out"][l // 2] = jnp.concatenate(
                [_mm(f"g_ab_out_a{l}", _flat(r["o"]), dyf, ta=True, out_dtype=BF16),
                 _mm(f"g_ab_out_b{l}", _flat(r["cm"]), dyf, ta=True, out_dtype=BF16)], axis=0)
            dq, dkv, dkpe = _mla_bwd(r["q"], r["kv"], r["kpe"], r["lse"], dcat, n, tq)
            (dq0,), _, _ = r["qrp"].bwd([dq], {0: BF16}, unread=(0,))
            dq0f, dkvf = _flat(dq0), _flat(dkv)
            G["wq"][j] = _mm(f"g_wq{l}", _flat(r["qn"]), dq0f, ta=True, out_dtype=BF16)
            G["wkv"][j] = _mm(f"g_wkv{l}", _flat(r["kvn"]), dkvf, ta=True, out_dtype=BF16)
            dqn = _mm(f"d_qn{l}", dq0f, W["wq"][j], tb=True).reshape(B, L, 256)
            dkvn = _mm(f"d_kvn{l}", dkvf, W["wkv"][j], tb=True).reshape(B, L, 256)
            (dz,), _, gg = r["abp"].bwd([dkvn, dqn, dkpe, (dcat, 512)], {0: BF16}, grad_glob=tuple(range(14)))
            G["kv_norm"][j], G["q_norm"][j] = gg[0], gg[1]
            G["v_norm"][j], G["ws"][j], G["bs"][j] = list(gg[2:6]), list(gg[6:10]), list(gg[10:14])
            w_in, key = W["ab_in"][j], "ab_in"
        else:
            w_out = W["cd_out"][j]
            dcat = _mm(f"d_cat{l}", dyf, w_out, tb=True, out_dtype=BF16).reshape(B, L, -1)
            G["cd_out"][j] = jnp.concatenate(
                [_mm(f"g_cd_out_a{l}", _flat(r["yr"]), dyf, ta=True, out_dtype=BF16),
                 _mm(f"g_cd_out_b{l}", _flat(r["osw"]), dyf, ta=True, out_dtype=BF16)], axis=0)
            (dyret, drg), _, gg = r["mrg"].bwd([(dcat, 0)], {0: F32, 1: F32}, grad_glob=(0, 1, 2, 3))
            G["ret_norm"][j] = list(gg)
            drq, drk, drv, dlg = _ret_bwd(r["rq"], r["rk"], r["rv"], W["lg"][j], dyret, n, tq)
            dsq, dsk, dsv, dsink = _swa_bwd(r["sq"], r["sk"], r["sv"], W["sink"][j], r["lse"], dcat, n, tq)
            G["lg"][j], G["sink"][j] = dlg, dsink
            (dz,), _, _ = r["cdp"].bwd([drq, drk, drv, dsq, dsk, dsv, drg], {0: BF16}, unread=(0,))
            w_in, key = W["cd_in"][j], "cd_in"
        dzf = _flat(dz)
        G[key][j] = _mm(f"g_{key}{l}", _flat(r["xn"]), dzf, ta=True, out_dtype=BF16)
        dxn = _mm(f"d_xn{l}", dzf, w_in, tb=True, out_dtype=BF16).reshape(B, L, D)
        if l == 0:
            (ds,), (dm[0][0], dm[0][1]), (G["norm_mix"][0],) = r["pre1"].bwd([dxn], {0: F32}, grad_glob=(0,),
                                                                            add={0: ds0}, samp=smp)
        else:
            ds, dxn_next = ds0, dxn
    dmods = jnp.stack([jnp.concatenate(d, axis=2) for d in dm])
    return loss, ds[:, :n], dmods, G


def kernel(x, c, ctx, c_ctx, ada_w, ada_b, norm_mix, norm_ffn, norm_final, ffn_in, ffn_out, ab_in, ab_out, mla_q_norm, mla_kv_norm, mla_wq_b, mla_wkv_b, cmlp_v_norm, cmlp_ws, cmlp_bs, cd_in, cd_out, ret_decay_fwd, ret_decay_bwd, ret_norm, swa_sink, loss_target, m_c_ctx, m_ada_w, m_ada_b, m_norm_mix, m_norm_ffn, m_norm_final, m_ffn_in, m_ffn_out, m_ab_in, m_ab_out, m_mla_q_norm, m_mla_kv_norm, m_mla_wq_b, m_mla_wkv_b, m_cmlp_v_norm, m_cmlp_ws, m_cmlp_bs, m_cd_in, m_cd_out, m_ret_decay_fwd, m_ret_decay_bwd, m_ret_norm, m_swa_sink, v_c_ctx, v_ada_w, v_ada_b, v_norm_mix, v_norm_ffn, v_norm_final, v_ffn_in, v_ffn_out, v_ab_in, v_ab_out, v_mla_q_norm, v_mla_kv_norm, v_mla_wq_b, v_mla_wkv_b, v_cmlp_v_norm, v_cmlp_ws, v_cmlp_bs, v_cd_in, v_cd_out, v_ret_decay_fwd, v_ret_decay_bwd, v_ret_norm, v_swa_sink):
    B, n, D = x.shape
    NL = ada_w.shape[0]
    NE, NO = ab_in.shape[0], cd_in.shape[0]
    me = _my_index()
    weights = dict(c_ctx=c_ctx, ada_w=ada_w, ada_b=ada_b, norm_mix=norm_mix, norm_ffn=norm_ffn, norm_final=norm_final,
                   ffn_in=ffn_in, ffn_out=ffn_out, ab_in=ab_in, ab_out=ab_out, mla_q_norm=mla_q_norm,
                   mla_kv_norm=mla_kv_norm, mla_wq_b=mla_wq_b, mla_wkv_b=mla_wkv_b, cmlp_v_norm=cmlp_v_norm,
                   cmlp_ws=cmlp_ws, cmlp_bs=cmlp_bs, cd_in=cd_in, cd_out=cd_out, ret_decay_fwd=ret_decay_fwd,
                   ret_decay_bwd=ret_decay_bwd, ret_norm=ret_norm, swa_sink=swa_sink)
    moments_m = dict(c_ctx=m_c_ctx, ada_w=m_ada_w, ada_b=m_ada_b, norm_mix=m_norm_mix, norm_ffn=m_norm_ffn,
                     norm_final=m_norm_final, ffn_in=m_ffn_in, ffn_out=m_ffn_out, ab_in=m_ab_in, ab_out=m_ab_out,
                     mla_q_norm=m_mla_q_norm, mla_kv_norm=m_mla_kv_norm, mla_wq_b=m_mla_wq_b, mla_wkv_b=m_mla_wkv_b,
                     cmlp_v_norm=m_cmlp_v_norm, cmlp_ws=m_cmlp_ws, cmlp_bs=m_cmlp_bs, cd_in=m_cd_in, cd_out=m_cd_out,
                     ret_decay_fwd=m_ret_decay_fwd, ret_decay_bwd=m_ret_decay_bwd, ret_norm=m_ret_norm,
                     swa_sink=m_swa_sink)
    moments_v = dict(c_ctx=v_c_ctx, ada_w=v_ada_w, ada_b=v_ada_b, norm_mix=v_norm_mix, norm_ffn=v_norm_ffn,
                     norm_final=v_norm_final, ffn_in=v_ffn_in, ffn_out=v_ffn_out, ab_in=v_ab_in, ab_out=v_ab_out,
                     mla_q_norm=v_mla_q_norm, mla_kv_norm=v_mla_kv_norm, mla_wq_b=v_mla_wq_b, mla_wkv_b=v_mla_wkv_b,
                     cmlp_v_norm=v_cmlp_v_norm, cmlp_ws=v_cmlp_ws, cmlp_bs=v_cmlp_bs, cd_in=v_cd_in, cd_out=v_cd_out,
                     ret_decay_fwd=v_ret_decay_fwd, ret_decay_bwd=v_ret_decay_bwd, ret_norm=v_ret_norm,
                     swa_sink=v_swa_sink)
    order = list(weights)

    Ns = ada_w.shape[2]
    (c_g,) = _all_gather("gather_c", [c])
    R = N_DEV * B + 8
    c_all = jnp.concatenate([c_g.reshape(N_DEV * B, D), jnp.broadcast_to(c_ctx[None], (8, D))], axis=0)
    ada_b_mine = lax.dynamic_slice_in_dim(ada_b, me * Ns, Ns, axis=1)[:, None, :]
    mods_shard = _ada_fwd(c_all, ada_w, ada_b_mine)
    (mods_g,) = _all_gather("gather_mods", [mods_shard])
    mods_full = jnp.transpose(mods_g, (1, 2, 0, 3)).reshape(NL, R, 6, D)
    mx = lax.dynamic_slice_in_dim(mods_full, me * B, B, axis=1)
    mh = jnp.broadcast_to(mods_full[:, N_DEV * B][:, None], (NL, B, 6, D))
    mods = jnp.stack([mx, mh], axis=2)

    big = ["ffn_in", "ffn_out", "ab_in", "ab_out", "cd_in", "cd_out", "mla_wq_b", "mla_wkv_b"]
    col_sharded = {"ffn_in", "ab_in", "cd_in", "mla_wq_b", "mla_wkv_b"}
    shards = {k: _to_bf16("cast_" + k, weights[k]) for k in big}
    first = {k: 0 if k.startswith("cd_") else 1 for k in big}
    a_keys = [k for k in big if first[k] and not k.startswith("ffn_")]
    b_keys = ["ffn_in", "ffn_out"]
    early = _all_gather("gather_wA", [shards[k][:1] for k in a_keys])
    (rn_g,) = _all_gather("gather_ret_norm", [ret_norm])
    small_first = [mods, rn_g] + list(early)
    wb_handle, wb_token = _exchange_start("gather_wB_start", [shards[k][:1] for k in b_keys], False, small_first)
    wc_handle, wc_token = _exchange_start("gather_wC_start", [shards[k][first[k]:] for k in big], False,
                                          small_first + [wb_token])
    mods = mods + (wb_token[0, 0] + wc_token[0, 0])

    def unshard(k, g):
        if k in col_sharded:
            f = jnp.transpose(g, (1, 2, 0, 3)).reshape(g.shape[1], g.shape[2], -1)
        else:
            f = jnp.transpose(g, (1, 0, 2, 3)).reshape(g.shape[1], -1, g.shape[3])
        return [f[i] for i in range(f.shape[0])]

    rn_full = jnp.transpose(rn_g, (1, 0, 2)).reshape(NO, -1)
    small_p = dict(weights, ret_norm=rn_full)
    full0 = {k: [None] * weights[k].shape[0] for k in big}
    for k, g in zip(a_keys, early):
        full0[k][:1] = unshard(k, g)

    def later_weights(stage, newest):
        if stage == 0:
            landed = _exchange_wait("gather_wB_wait", wb_handle, newest)
            landed = _fill_own("own_wB", landed, [shards[k][:1] for k in b_keys], False)
            for k, land in zip(b_keys, landed):
                full0[k][:1] = unshard(k, land)
        else:
            landed = _exchange_wait("gather_wC_wait", wc_handle, newest)
            landed = _fill_own("own_wC", landed, [shards[k][first[k]:] for k in big], False)
            for k, land in zip(big, landed):
                full0[k][first[k]:] = unshard(k, land)
        return _layer_weights(full0, small_p)

    def to_slots(k, gl):
        g = jnp.stack(gl)
        if k == "ffn_in":
            half = N_DEV // 2
            g = g.reshape(g.shape[0], 2, g.shape[2], half, -1)
            return jnp.transpose(g, (1, 3, 0, 2, 4)).reshape(N_DEV, g.shape[0], g.shape[2], -1)
        if k in col_sharded:
            return jnp.transpose(g.reshape(g.shape[0], g.shape[1], N_DEV, -1), (2, 0, 1, 3))
        return jnp.transpose(g.reshape(g.shape[0], N_DEV, -1, g.shape[2]), (1, 0, 2, 3))

    def big_grads(G):
        return dict(ffn_in=[g if g is None else _ffn_deinterleave(g) for g in G["ffn_in"]], ffn_out=G["ffn_out"],
                    ab_in=[g if g is None else _ab_unperm(g) for g in G["ab_in"]],
                    ab_out=G["ab_out"], cd_in=G["cd_in"], cd_out=G["cd_out"],
                    mla_wq_b=[g if g is None else _wq_unperm(g) for g in G["wq"]], mla_wkv_b=G["wkv"])

    sent = {}

    def early_grads(stage, G):
        parts = big_grads(G)
        if stage == 0:
            srcs = [to_slots(k, parts[k][first[k]:]) for k in big]
            handle, g_token = _exchange_start("scatter_gC_start", srcs, slotted=True)
        else:
            srcs = [to_slots(k, parts[k][:1]) for k in b_keys]
            handle, g_token = _exchange_start("scatter_gB_start", srcs, slotted=True)
        sent[stage] = (handle, srcs)
        return g_token[0, 0]

    loss_part, grad_x, dmods, G = _local_step(x, ctx, loss_target, mods, _layer_weights(full0, small_p),
                                              later_weights, early_grads)

    dmx = dmods[:, :, 0].reshape(NL, B, 6 * D)
    dmh = jnp.sum(dmods[:, :, 1], axis=1).reshape(NL, 1, 6 * D)
    (dm_g,) = _all_gather("gather_dmods", [jnp.concatenate([dmx, dmh], axis=1)])
    dmx_all = jnp.transpose(dm_g[:, :, :B], (1, 0, 2, 3)).reshape(NL, N_DEV * B, 6 * D)
    dmh_all = jnp.sum(dm_g[:, :, B], axis=0)
    dm_rows = jnp.concatenate([dmx_all, dmh_all[:, None], jnp.zeros((NL, 7, 6 * D), F32)], axis=1)
    g_ada_b = jnp.sum(dm_rows, axis=1)
    dm_mine = lax.dynamic_slice_in_dim(dm_rows, me * Ns, Ns, axis=2)
    g_ada_w, dcond = _ada_bwd(c_all, ada_w, dm_mine)
    sg = jax.nn.sigmoid(c_ctx)
    d_c_ctx_part = jnp.sum(dcond[:, N_DEV * B], axis=0) * (sg * (1.0 + c_ctx * (1.0 - sg)))

    def cat(parts):
        return jnp.concatenate([p.reshape(-1) for p in parts])

    dlg = jnp.stack([jnp.sum(G["lg"][j], axis=0) for j in range(NO)])
    sig_f, sig_b = jax.nn.sigmoid(-ret_decay_fwd), jax.nn.sigmoid(-ret_decay_bwd)
    small = dict(
        loss=loss_part[0, 0:1],
        c_ctx=d_c_ctx_part,
        norm_mix=cat(G["norm_mix"]), norm_ffn=cat(G["norm_ffn"]), norm_final=G["norm_final"].reshape(-1),
        mla_q_norm=cat(G["q_norm"]), mla_kv_norm=cat(G["kv_norm"]),
        cmlp_v_norm=cat([cat(G["v_norm"][j]) for j in range(NE)]),
        cmlp_ws=cat([jnp.stack(G["ws"][j]) for j in range(NE)]),
        cmlp_bs=cat([jnp.stack([b_[:, 0] for b_ in G["bs"][j]]) for j in range(NE)]),
        ret_decay_fwd=(dlg[:, 0] * sig_f).reshape(-1), ret_decay_bwd=(dlg[:, 1] * sig_b).reshape(-1),
        ret_norm=cat([cat(G["ret_norm"][j]) for j in range(NO)]),
        swa_sink=cat([jnp.sum(G["sink"][j], axis=0) for j in range(NO)]),
    )
    small_keys = list(small)
    sizes = [small[k].shape[0] for k in small_keys]
    total = sum(sizes)
    padded = -(-total // 2048) * 2048
    packed = jnp.concatenate([small[k] for k in small_keys] + [jnp.zeros((padded - total,), F32)]).reshape(-1, 128)
    (small_g,) = _all_gather("gather_small", [packed])

    parts = big_grads(G)
    landed0 = dict(zip(a_keys, _all_to_all("scatter_gA", [to_slots(k, parts[k][:1]) for k in a_keys])))
    landed_b = _fill_own("own_gB", _exchange_wait("scatter_gB_wait", sent[1][0], grad_x), sent[1][1], True)
    landed0.update(zip(b_keys, landed_b))
    landed_c = _fill_own("own_gC", _exchange_wait("scatter_gC_wait", sent[0][0], grad_x), sent[0][1], True)
    landed = [[landed0[k], rest] if first[k] else [rest] for k, rest in zip(big, landed_c)]

    grads, deltas, new_m, new_v = {}, {}, {}, {}
    for k, land in zip(big, landed):
        grads[k], deltas[k], new_m[k], new_v[k] = _adamw_from_slots("adamw_" + k, weights[k], moments_m[k], moments_v[k], land)
    deltas["ada_w"], new_m["ada_w"], new_v["ada_w"] = [
        o.reshape(ada_w.shape) for o in _adamw("adamw_ada_w", _flat(ada_w), _flat(g_ada_w), _flat(m_ada_w), _flat(v_ada_w))]
    grads["ada_w"] = g_ada_w

    sums_only = ("loss", "ret_norm")

    def packed_of(src, fill):
        vals = [jnp.full((sizes[i],), fill, F32) if k in sums_only else src[k].reshape(-1)
                for i, k in enumerate(small_keys)]
        return jnp.concatenate(vals + [jnp.full((padded - total,), fill, F32)]).reshape(-1, 128)

    w_p, m_p, v_p = packed_of(weights, 0.0), packed_of(moments_m, 0.0), packed_of(moments_v, 1.0)

    def f_small(w_, m_, v_, land_):
        g = _sum_slots(land_)
        return (g,) + _adamw_math(w_, g, m_, v_)

    g_p, d_p, nm_p, nv_p = _ew("adamw_small", f_small, [w_p, m_p, v_p, small_g], [F32] * 4)
    offs = np.cumsum([0] + sizes)
    for i, k in enumerate(small_keys):
        sl = slice(int(offs[i]), int(offs[i + 1]))
        if k == "loss":
            loss = g_p.reshape(-1)[int(offs[i])]
        elif k == "ret_norm":
            g_full = g_p.reshape(-1)[sl].reshape(NO, -1)
            g_mine = lax.dynamic_slice_in_dim(g_full, me * ret_norm.shape[1], ret_norm.shape[1], axis=1)
            d_, m_, v_ = _adamw("adamw_ret_norm", *[jnp.pad(a, ((0, 8 - NO), (0, 128 - a.shape[1])), constant_values=cv)
                                                     for a, cv in ((ret_norm, 0.0), (g_mine, 0.0), (m_ret_norm, 0.0), (v_ret_norm, 1.0))])
            grads[k] = g_mine
            deltas[k], new_m[k], new_v[k] = [a[:NO, :ret_norm.shape[1]] for a in (d_, m_, v_)]
        else:
            shp = weights[k].shape
            grads[k], deltas[k], new_m[k], new_v[k] = [a.reshape(-1)[sl].reshape(shp) for a in (g_p, d_p, nm_p, nv_p)]
    pad_b = lambda a, cv=0.0: jnp.pad(a, ((0, 8 - NL), (0, 0)), constant_values=cv)
    d_, m_, v_ = _adamw("adamw_ada_b", pad_b(ada_b), pad_b(g_ada_b), pad_b(m_ada_b), pad_b(v_ada_b, 1.0))
    grads["ada_b"] = g_ada_b
    deltas["ada_b"], new_m["ada_b"], new_v["ada_b"] = d_[:NL], m_[:NL], v_[:NL]

    return (loss, grad_x, *[grads[k] for k in order], *[deltas[k] for k in order],
            *[new_m[k] for k in order], *[new_v[k] for k in order])
```

```python
import functools

import numpy as np
import jax
import jax.numpy as jnp
from jax import lax
from jax.experimental import pallas as pl
from jax.experimental.pallas import tpu as pltpu

F32 = jnp.float32
BF16 = jnp.bfloat16
EPS = 1e-6
NEG_INF = -1e30
N_DEV = 8
GRID_W = 64
ROPE_THETA = 10000.0
ROPE_DIM = 64
SWA_WINDOW = 128
MLA_SCALE = (128 + 64) ** -0.5
SWA_SCALE = 64 ** -0.5
RET_K_SCALE = 64 ** -0.5
ADAM_LR, ADAM_B1, ADAM_B2, ADAM_EPS, ADAM_WD, ADAM_STEP = 0.001, 0.9, 0.999, 1e-08, 0.01, 10
V7X_VMEM_LIMIT = 56 * 1024 * 1024
MESH = pl.DeviceIdType.MESH


def _pallas(body, **kw):
    return pl.pallas_call(body, **kw)


def _params(sem=None):
    return pltpu.CompilerParams(dimension_semantics=sem, vmem_limit_bytes=V7X_VMEM_LIMIT)


def _tile(n, cap, align):
    best = None
    for t in range(align, min(n, cap) + 1, align):
        if n % t == 0:
            best = t
    return n if best is None else best


def _sds(shape, dtype):
    return jax.ShapeDtypeStruct(tuple(shape), dtype)


def _ew(name, f, ins, out_dtypes, cap_elems=131072):
    R, C = ins[0].shape[-2:]
    tr = _tile(R, max(16, cap_elems // C), 16)
    n_in = len(ins)

    def spec(a):
        if a.ndim == 2:
            return pl.BlockSpec((tr, C), lambda i: (i, 0))
        return pl.BlockSpec((a.shape[0], tr, C), lambda i: (0, i, 0))

    def body(*refs):
        outs = f(*[r[...] for r in refs[:n_in]])
        for r, o in zip(refs[n_in:], outs):
            r[...] = o.astype(r.dtype)

    return _pallas(
        body, name=name, grid=(R // tr,), in_specs=[spec(a) for a in ins],
        out_specs=[pl.BlockSpec((tr, C), lambda i: (i, 0)) for _ in out_dtypes],
        out_shape=[_sds((R, C), d) for d in out_dtypes], compiler_params=_params(("parallel",)),
    )(*ins)


def _to_bf16(name, w):
    w2 = w.reshape(-1, w.shape[-1])
    return _ew(name, lambda v: (v,), [w2], [BF16])[0].reshape(w.shape)


def _adamw_math(w, g, m, v):
    m = ADAM_B1 * m + (1.0 - ADAM_B1) * g
    v = ADAM_B2 * v + (1.0 - ADAM_B2) * (g * g)
    m_hat = m / (1.0 - ADAM_B1 ** ADAM_STEP)
    v_hat = v / (1.0 - ADAM_B2 ** ADAM_STEP)
    delta = -ADAM_LR * (m_hat / (jnp.sqrt(v_hat) + ADAM_EPS) + ADAM_WD * w)
    return delta, m, v


def _sum_slots(land):
    g = land[0].astype(F32)
    for s in range(1, land.shape[0]):
        g = g + land[s].astype(F32)
    return g


def _adamw_from_slots(name, w, m, v, lands):
    nl, K, C = w.shape
    tr = _tile(K, max(16, 65536 // C), 16)
    per = K // tr
    starts = np.cumsum([0] + [ld.shape[1] * per for ld in lands])
    ng = len(lands)

    def land_spec(g):
        lo, hi = int(starts[g]), int(starts[g + 1])
        return pl.BlockSpec((N_DEV, tr, C), lambda t: (0, jnp.clip(t, lo, hi - 1) - lo, 0))

    def body(*refs):
        w_ref, m_ref, v_ref = refs[:3]
        land_refs, out_refs = refs[3:3 + ng], refs[3 + ng:]
        t = pl.program_id(0)
        for g in range(ng):
            @pl.when((t >= int(starts[g])) & (t < int(starts[g + 1])))
            def _(g=g):
                grad = _sum_slots(land_refs[g][...])
                for r, o in zip(out_refs, (grad,) + _adamw_math(w_ref[...], grad, m_ref[...], v_ref[...])):
                    r[...] = o

    row = pl.BlockSpec((tr, C), lambda t: (t, 0))
    outs = _pallas(
        body, name=name, grid=(nl * per,), in_specs=[row] * 3 + [land_spec(g) for g in range(ng)],
        out_specs=[row] * 4, out_shape=[_sds((nl * K, C), F32)] * 4, compiler_params=_params(("parallel",)),
    )(w.reshape(-1, C), m.reshape(-1, C), v.reshape(-1, C), *[ld.reshape(N_DEV, -1, C) for ld in lands])
    return [o.reshape(w.shape) for o in outs]


def _adamw(name, w, g, m, v):
    outs = _ew(name, lambda w_, g_, m_, v_: _adamw_math(w_, g_, m_, v_), [w, g, m, v], [F32] * 3)
    return outs


def _mm(name, a, b, ta=False, tb=False, out_dtype=F32, add=None):
    M, K = (a.shape[1], a.shape[0]) if ta else a.shape
    N = b.shape[0] if tb else b.shape[1]
    tm = _tile(M, 1408, 128)
    tn = _tile(N, 1024, 128)
    if tn < 256 and N <= 2432:
        tn = N
    tk = _tile(K, 1792, 128)
    nk = K // tk
    a_spec = pl.BlockSpec((tk, tm), lambda i, j, k: (k, i)) if ta else pl.BlockSpec((tm, tk), lambda i, j, k: (i, k))
    b_spec = pl.BlockSpec((tn, tk), lambda i, j, k: (j, k)) if tb else pl.BlockSpec((tk, tn), lambda i, j, k: (k, j))
    o_spec = pl.BlockSpec((tm, tn), lambda i, j, k: (i, j))
    dims = (((0 if ta else 1,), (1 if tb else 0,)), ((), ()))
    has_add = add is not None

    def product(a_ref, b_ref):
        return lax.dot_general(a_ref[...].astype(BF16), b_ref[...].astype(BF16), dims, preferred_element_type=F32)

    def body_single(*refs):
        acc = product(refs[0], refs[1])
        if has_add:
            acc = acc + refs[2][...]
        refs[-1][...] = acc.astype(refs[-1].dtype)

    def body(*refs):
        a_ref, b_ref = refs[0], refs[1]
        add_ref = refs[2] if has_add else None
        o_ref, acc = refs[-2], refs[-1]
        k = pl.program_id(2)

        @pl.when(k == 0)
        def _():
            acc[...] = add_ref[...] if has_add else jnp.zeros_like(acc)

        acc[...] += product(a_ref, b_ref)

        @pl.when(k == nk - 1)
        def _():
            o_ref[...] = acc[...].astype(o_ref.dtype)

    ins = [a, b] + ([add] if has_add else [])
    specs = [a_spec, b_spec] + ([o_spec] if has_add else [])
    return _pallas(
        body_single if nk == 1 else body, name=name, grid=(M // tm, N // tn, nk), in_specs=specs, out_specs=o_spec,
        out_shape=_sds((M, N), out_dtype), scratch_shapes=[] if nk == 1 else [pltpu.VMEM((tm, tn), F32)],
        compiler_params=_params(("parallel", "parallel", "arbitrary")),
    )(*ins)


def _mm_pair(name, a1, a2, b, out_dtype):
    M, Kh = a1.shape
    N = b.shape[1]
    tm = _tile(M, 1408, 128)

    def body(a1_ref, a2_ref, b1_ref, b2_ref, o_ref):
        o_ref[...] = (_dot(a1_ref[...].astype(BF16), b1_ref[...].astype(BF16))
                      + _dot(a2_ref[...].astype(BF16), b2_ref[...].astype(BF16))).astype(o_ref.dtype)

    a_spec = pl.BlockSpec((tm, Kh), lambda i: (i, 0))
    return _pallas(
        body, name=name, grid=(M // tm,),
        in_specs=[a_spec, a_spec, pl.BlockSpec((Kh, N), lambda i: (0, 0)), pl.BlockSpec((Kh, N), lambda i: (1, 0))],
        out_specs=pl.BlockSpec((tm, N), lambda i: (i, 0)), out_shape=_sds((M, N), out_dtype),
        compiler_params=_params(("parallel",)),
    )(a1, a2, b, b)


def _ffn_tile(F):
    return F // 2 if (F // 2) % 128 == 0 else F


def _ffn_interleave(w):
    D, F2 = w.shape
    T = _ffn_tile(F2 // 2)
    nj = F2 // (2 * T)

    def body(a_ref, b_ref, o_ref):
        o_ref[:, :T] = a_ref[...]
        o_ref[:, T:] = b_ref[...]

    return _pallas(
        body, name="ffn_interleave", grid=(nj,),
        in_specs=[pl.BlockSpec((D, T), lambda j: (0, j)), pl.BlockSpec((D, T), lambda j: (0, j + nj))],
        out_specs=pl.BlockSpec((D, 2 * T), lambda j: (0, j)), out_shape=_sds((D, F2), w.dtype),
        compiler_params=_params(("parallel",)),
    )(w, w)


def _ffn_deinterleave(w):
    D, F2 = w.shape
    T = _ffn_tile(F2 // 2)
    nj = F2 // (2 * T)

    def body(w_ref, o_ref):
        o_ref[0] = w_ref[:, :T]
        o_ref[1] = w_ref[:, T:]

    return _pallas(
        body, name="ffn_deinterleave", grid=(nj,), in_specs=[pl.BlockSpec((D, 2 * T), lambda j: (0, j))],
        out_specs=pl.BlockSpec((2, D, T), lambda j: (0, 0, j)), out_shape=_sds((2, D, F2 // 2), w.dtype),
        compiler_params=_params(("parallel",)),
    )(w)


def _ffn_specs(M, D, F):
    T = _ffn_tile(F)
    tm = _tile(M, 512, 128)
    x_spec = pl.BlockSpec((tm, D), lambda i, j: (i, 0))
    wi_spec = pl.BlockSpec((D, 2 * T), lambda i, j: (0, j))
    wo_spec = pl.BlockSpec((T, D), lambda i, j: (j, 0))
    return T, tm, F // T, x_spec, wi_spec, wo_spec


def _ffn_fwd(name, xn, w_in, w_out):
    M, D = xn.shape
    T, tm, nj, x_spec, wi_spec, wo_spec = _ffn_specs(M, D, w_out.shape[0])

    def body(x_ref, wi_ref, wo_ref, y_ref, acc):
        j = pl.program_id(1)
        z = _dot(x_ref[...], wi_ref[...])
        hid = (jax.nn.silu(z[:, :T]) * z[:, T:]).astype(BF16)
        part = _dot(hid, wo_ref[...])

        @pl.when(j == 0)
        def _():
            acc[...] = part

        @pl.when(j > 0)
        def _():
            acc[...] += part

        @pl.when(j == nj - 1)
        def _():
            y_ref[...] = acc[...].astype(y_ref.dtype)

    return _pallas(
        body, name=name, grid=(M // tm, nj), in_specs=[x_spec, wi_spec, wo_spec], out_specs=x_spec,
        out_shape=_sds((M, D), BF16), scratch_shapes=[pltpu.VMEM((tm, D), F32)],
        compiler_params=_params(("parallel", "arbitrary")),
    )(xn, w_in, w_out)


def _ffn_mid_bwd(name, xn, dy, w_in, w_out):
    M, D = xn.shape
    F = w_out.shape[0]
    T, tm, nj, x_spec, wi_spec, wo_spec = _ffn_specs(M, D, F)

    def body(x_ref, dy_ref, wi_ref, wo_ref, h_ref, dz_ref):
        z = _dot(x_ref[...], wi_ref[...])
        a, b = z[:, :T], z[:, T:]
        dh = _dot_nt(dy_ref[...], wo_ref[...])
        sig = jax.nn.sigmoid(a)
        act = a * sig
        h_ref[...] = (act * b).astype(BF16)
        dz_ref[:, :T] = (dh * b * (sig * (1.0 + a * (1.0 - sig)))).astype(BF16)
        dz_ref[:, T:] = (dh * act).astype(BF16)

    return _pallas(
        body, name=name, grid=(M // tm, nj), in_specs=[x_spec, x_spec, wi_spec, wo_spec],
        out_specs=[pl.BlockSpec((tm, T), lambda i, j: (i, j)),
                   pl.BlockSpec((tm, 2 * T), lambda i, j: (i, j))],
        out_shape=[_sds((M, F), BF16), _sds((M, 2 * F), BF16)],
        compiler_params=_params(("parallel", "parallel")),
    )(xn, dy, w_in, w_out)


class _Rows:
    def __init__(self, name, f, B, L, n, tm, arrays, pieces, samp, samp_pieces, glob, out_arrays, out_pieces):
        self.name, self.f, self.B, self.L, self.n, self.tm = name, f, B, L, n, tm
        self.arrays, self.pieces, self.samp, self.samp_pieces, self.glob = arrays, pieces, samp, samp_pieces, glob
        self.out_arrays, self.out_pieces = out_arrays, out_pieces
        self.nx = n // tm
        self.grid = (B, L // tm)

    def _row_spec(self, C, batched=True):
        tm = self.tm
        if batched:
            return pl.BlockSpec((None, tm, C), lambda b, i: (b, i, 0))
        return pl.BlockSpec((tm, C), lambda b, i: (i, 0))

    def _in_specs(self, unread=()):
        nx, tm = self.nx, self.tm
        specs = [self._row_spec(a.shape[-1], a.ndim == 3) for a in self.arrays]
        for ai in unread:
            specs[ai] = pl.BlockSpec((None, tm, self.arrays[ai].shape[-1]), lambda b, i: (0, 0, 0))
        specs += [pl.BlockSpec((None, None) + s.shape[2:], lambda b, i: (b, i // nx, 0, 0)) for s in self.samp]
        specs += [pl.BlockSpec(g.shape, lambda b, i, nd=g.ndim: (0,) * nd) for g in self.glob]
        return specs

    def _load(self, a_refs, s_refs, g_refs):
        args = [a_refs[ai][:, cs:cs + cw].astype(F32) for ai, cs, cw in self.pieces]
        args += [s_refs[si][r:r + 1, :].astype(F32) for si, r in self.samp_pieces]
        args += [g[...].astype(F32) for g in g_refs]
        return args

    def fwd(self):
        na, ns, ng = len(self.arrays), len(self.samp), len(self.glob)

        def body(*refs):
            a_refs, s_refs, g_refs = refs[:na], refs[na:na + ns], refs[na + ns:na + ns + ng]
            o_refs = refs[na + ns + ng:]
            outs = self.f(*self._load(a_refs, s_refs, g_refs))
            for (oi, cs, cw), o in zip(self.out_pieces, outs):
                o_refs[oi][:, cs:cs + cw] = o.astype(o_refs[oi].dtype)

        return _pallas(
            body, name=self.name + "_fwd", grid=self.grid, in_specs=self._in_specs(),
            out_specs=[self._row_spec(C) for C, _ in self.out_arrays],
            out_shape=[_sds((self.B, self.L, C), d) for C, d in self.out_arrays],
            compiler_params=_params(("parallel", "parallel")),
        )(*self.arrays, *self.samp, *self.glob)

    def bwd(self, cts, grad_arrays, grad_glob=(), add=None, samp=None, unread=()):
        samp = self.samp if samp is None else samp
        na, ns, ng, nc = len(self.arrays), len(self.samp), len(self.glob), len(cts)
        ct_off = [c[1] if isinstance(c, tuple) else 0 for c in cts]
        cts = [c[0] if isinstance(c, tuple) else c for c in cts]
        add = add or {}
        add_keys = list(add)
        g_idx = list(grad_arrays)
        nx, B = self.nx, self.B
        n_in = na + ns + ng + nc + len(add_keys)
        n_pieces, n_sp = len(self.pieces), len(self.samp_pieces)

        def body(*refs):
            a_refs, s_refs, g_refs = refs[:na], refs[na:na + ns], refs[na + ns:na + ns + ng]
            c_refs = refs[na + ns + ng:na + ns + ng + nc]
            add_refs = refs[na + ns + ng + nc:n_in]
            d_refs = refs[n_in:n_in + len(g_idx)]
            ds_refs = refs[n_in + len(g_idx):n_in + len(g_idx) + n_sp]
            dg_refs = refs[n_in + len(g_idx) + n_sp:]
            b, i = pl.program_id(0), pl.program_id(1)
            args = self._load(a_refs, s_refs, g_refs)
            _, vjp = jax.vjp(lambda *xs: tuple(self.f(*xs)), *args)
            grads = vjp(tuple(c_refs[oi][:, ct_off[oi] + cs:ct_off[oi] + cs + cw].astype(F32)
                              for oi, cs, cw in self.out_pieces))
            for k, ai in enumerate(g_idx):
                covered = sum(cw for pa, _, cw in self.pieces if pa == ai)
                if covered < self.arrays[ai].shape[-1]:
                    d_refs[k][...] = jnp.zeros_like(d_refs[k])
                for (pa, cs, cw), gr in zip(self.pieces, grads[:n_pieces]):
                    if pa == ai:
                        if ai in add:
                            gr = gr + add_refs[add_keys.index(ai)][:, cs:cs + cw]
                        d_refs[k][:, cs:cs + cw] = gr.astype(d_refs[k].dtype)

            @pl.when((i == 0) | (i == nx))
            def _():
                for r in ds_refs:
                    r[...] = jnp.zeros_like(r)

            for r, gr in zip(ds_refs, grads[n_pieces:n_pieces + n_sp]):
                r[...] += gr

            @pl.when((b == 0) & (i == 0))
            def _():
                for r in dg_refs:
                    r[...] = jnp.zeros_like(r)

            for r, gi in zip(dg_refs, grad_glob):
                r[...] += grads[n_pieces + n_sp + gi]

        in_specs = self._in_specs(unread) + [self._row_spec(c.shape[-1]) for c in cts]
        in_specs += [self._row_spec(add[k].shape[-1]) for k in add_keys]
        out_specs = [self._row_spec(self.arrays[ai].shape[-1]) for ai in g_idx]
        out_shape = [_sds(self.arrays[ai].shape, grad_arrays[ai]) for ai in g_idx]
        for si, _ in self.samp_pieces:
            C = self.samp[si].shape[-1]
            out_specs.append(pl.BlockSpec((None, None, 1, C), lambda b, i: (b, i // nx, 0, 0)))
            out_shape.append(_sds((B, 2, 1, C), F32))
        for gi in grad_glob:
            g = self.glob[gi]
            out_specs.append(pl.BlockSpec(g.shape, lambda b, i, nd=g.ndim: (0,) * nd))
            out_shape.append(_sds(g.shape, F32))
        outs = _pallas(
            body, name=self.name + "_bwd", grid=self.grid, in_specs=in_specs, out_specs=out_specs,
            out_shape=out_shape, compiler_params=_params(("arbitrary", "arbitrary")),
        )(*self.arrays, *samp, *self.glob, *cts, *[add[k] for k in add_keys])
        ng_ = len(g_idx)
        return outs[:ng_], outs[ng_:ng_ + n_sp], outs[ng_ + n_sp:]


def _rms(x, g):
    return x * lax.rsqrt(jnp.mean(x * x, axis=-1, keepdims=True) + EPS) * g


@jax.custom_vjp
def _swap_halves(x):
    w = x.shape[-1]
    lane = lax.broadcasted_iota(jnp.int32, x.shape, x.ndim - 1)
    up = pltpu.roll(x, w - ROPE_DIM // 2, x.ndim - 1)
    down = pltpu.roll(x, ROPE_DIM // 2, x.ndim - 1)
    return jnp.where(lane % ROPE_DIM < ROPE_DIM // 2, up, down)


_swap_halves.defvjp(lambda x: (_swap_halves(x), None), lambda _, ct: (_swap_halves(ct),))


@jax.custom_vjp
def _bdot(a, b):
    return jnp.dot(a.astype(BF16), b.astype(BF16), preferred_element_type=F32)


_bdot.defvjp(lambda a, b: (_bdot(a, b), (a, b)),
             lambda res, ct: (_dot_nt(ct.astype(BF16), res[1].astype(BF16)), _dot_tn(res[0].astype(BF16), ct.astype(BF16))))


def _rope(x, cos2, sin2):
    return x * cos2 + _swap_halves(x) * sin2


def _f_pre(s, shift, scale, g):
    return (_rms(s, g) * (1.0 + scale) + shift,)


def _f_res(s, y, gate):
    return (s + gate * y,)


def _f_res_pre(s, y, gate, shift, scale, g):
    s1 = s + gate * y
    return s1, _rms(s1, g) * (1.0 + scale) + shift


def _f_ab_pre(*args):
    kv_lat, q_lat, kpe = args[0:3]
    us, vs = args[3:7], args[7:11]
    cos2, sin2 = args[11:13]
    kv_norm, q_norm = args[13:15]
    vns, wss, bss = args[15:19], args[19:23], args[23:27]
    outs = [_rms(kv_lat, kv_norm), _rms(q_lat, q_norm), _rope(kpe, cos2, sin2)]
    for u, v, vn, ws, bs in zip(us, vs, vns, wss, bss):
        vg = _rms(jax.nn.gelu(v), vn)
        mixed = _bdot(ws, vg) + bs
        outs.append(jax.nn.gelu(u) * mixed)
    return tuple(outs)


def _f_q_rope(qn, qr, cos2, sin2):
    return qn, _rope(qr, cos2, sin2)


def _f_cd_pre(rk, rv, sk, sv, rq, rg, sq, c256, s256, c128, s128, c512, s512):
    return (_rope(rq, c256, s256), _rope(rk * RET_K_SCALE, c256, s256), rv,
            _rope(sq, c512, s512), _rope(sk, c128, s128), sv, rg)


def _f_cd_merge(*args):
    ys, rgs, rns = args[0:4], args[4:8], args[8:12]
    return tuple(_rms(y, rn) * jax.nn.silu(rg) for y, rg, rn in zip(ys, rgs, rns))


def _dot_nt(a, b):
    return lax.dot_general(a, b, (((1,), (1,)), ((), ())), preferred_element_type=F32)


def _dot_tn(a, b):
    return lax.dot_general(a, b, (((0,), (0,)), ((), ())), preferred_element_type=F32)


def _dot(a, b):
    return jnp.dot(a, b, preferred_element_type=F32)


def _tile_spec(tq, C):
    return pl.BlockSpec((None, tq, C), lambda b, i: (b, i, 0))


def _full_spec(L, C):
    return pl.BlockSpec((None, L, C), lambda b, i: (b, 0, 0))


def _mla_half(h):
    lane = lax.broadcasted_iota(jnp.int32, (1, 128), 1)
    return (lane < 64) if h % 2 == 0 else (lane >= 64)


def _mla_query(q_ref, h):
    pair = q_ref[:, 512 + 128 * (h // 2):640 + 128 * (h // 2)]
    return jnp.concatenate([q_ref[:, 128 * h:128 * h + 128], jnp.where(_mla_half(h), pair, jnp.zeros_like(pair))], axis=1)


def _mla_fill_keys(kcat, kv_ref, kpe_ref):
    kp = kpe_ref[...]
    for h in range(4):
        kcat[h, :, 0:128] = kv_ref[:, 256 * h:256 * h + 128]
        kcat[h, :, 128:256] = jnp.where(_mla_half(h), kp, jnp.zeros_like(kp))


def _mla_fwd(q, kv, kpe, n, tq):
    B, L, _ = q.shape

    def body(q_ref, kv_ref, kpe_ref, o_ref, lse_ref, kcat):
        i = pl.program_id(1)

        @pl.when(i == 0)
        def _():
            _mla_fill_keys(kcat, kv_ref, kpe_ref)

        def tile(keys):
            for h in range(4):
                s = _dot_nt(_mla_query(q_ref, h), kcat[h, keys, :]) * MLA_SCALE
                m = jnp.max(s, axis=1, keepdims=True)
                e = jnp.exp(s - m)
                l = jnp.sum(e, axis=1, keepdims=True)
                p = (e * (1.0 / l)).astype(BF16)
                o_ref[:, 128 * h:128 * h + 128] = _dot(p, kv_ref[keys, 256 * h + 128:256 * h + 256]).astype(o_ref.dtype)
                lse_ref[:, h:h + 1] = m + jnp.log(l)

        pl.when(i < n // tq)(functools.partial(tile, slice(0, L)))
        pl.when(i >= n // tq)(functools.partial(tile, slice(n, L)))

    return _pallas(
        body, name="mla_fwd", grid=(B, L // tq),
        in_specs=[_tile_spec(tq, 768), _full_spec(L, 1024), _full_spec(L, 128)],
        out_specs=[_tile_spec(tq, 512), _tile_spec(tq, 4)],
        out_shape=[_sds((B, L, 512), BF16), _sds((B, L, 4), F32)],
        scratch_shapes=[pltpu.VMEM((4, L, 256), BF16)],
        compiler_params=_params(("parallel", "arbitrary")),
    )(q, kv, kpe)


def _mla_bwd(q, kv, kpe, lse, do, n, tq):
    B, L, _ = q.shape

    def body(q_ref, kv_ref, kpe_ref, lse_ref, do_ref, dq_ref, dkv_ref, dkpe_ref, kcat):
        i = pl.program_id(1)

        @pl.when(i == 0)
        def _():
            dkv_ref[...] = jnp.zeros_like(dkv_ref)
            dkpe_ref[...] = jnp.zeros_like(dkpe_ref)
            _mla_fill_keys(kcat, kv_ref, kpe_ref)

        def tile(keys):
            rope_pair = None
            for h in range(4):
                qc, kc = _mla_query(q_ref, h), kcat[h, keys, :]
                v = kv_ref[keys, 256 * h + 128:256 * h + 256]
                p = jnp.exp(_dot_nt(qc, kc) * MLA_SCALE - lse_ref[:, h:h + 1])
                doh = do_ref[:, 128 * h:128 * h + 128].astype(BF16)
                dp = _dot_nt(doh, v)
                delta = jnp.sum(p * dp, axis=1, keepdims=True)
                ds = (p * (dp - delta) * MLA_SCALE).astype(BF16)
                dqc = _dot(ds, kc)
                dq_ref[:, 128 * h:128 * h + 128] = dqc[:, 0:128]
                if h % 2 == 0:
                    rope_pair = dqc[:, 128:256]
                else:
                    dq_ref[:, 512 + 128 * (h // 2):640 + 128 * (h // 2)] = jnp.where(_mla_half(h), dqc[:, 128:256], rope_pair)
                dkc = _dot_tn(ds, qc)
                dkv_ref[keys, 256 * h:256 * h + 128] += dkc[:, 0:128]
                dkpe_ref[keys, :] += dkc[:, 128:256]
                dkv_ref[keys, 256 * h + 128:256 * h + 256] += _dot_tn(p.astype(BF16), doh)

        pl.when(i < n // tq)(functools.partial(tile, slice(0, L)))
        pl.when(i >= n // tq)(functools.partial(tile, slice(n, L)))

    return _pallas(
        body, name="mla_bwd", grid=(B, L // tq),
        in_specs=[_tile_spec(tq, 768), _full_spec(L, 1024), _full_spec(L, 128), _tile_spec(tq, 4),
                  pl.BlockSpec((None, tq, 512), lambda b, i: (b, i, 0))],
        out_specs=[_tile_spec(tq, 768), _full_spec(L, 1024), _full_spec(L, 128)],
        out_shape=[_sds((B, L, 768), F32), _sds((B, L, 1024), F32), _sds((B, L, 128), F32)],
        scratch_shapes=[pltpu.VMEM((4, L, 256), BF16)],
        compiler_params=_params(("parallel", "arbitrary")),
    )(q, kv, kpe, lse, do)


def _swa_window(i, tq, n):
    W = min(tq + 2 * SWA_WINDOW, n)
    lo = jnp.clip(i * tq - SWA_WINDOW, 0, n - W)
    return pl.multiple_of(lo, 128), W


def _swa_mask(i, lo, tq, W, n):
    qpos = i * tq + lax.broadcasted_iota(jnp.int32, (tq, 1), 0)
    kpos = lo + lax.broadcasted_iota(jnp.int32, (1, W), 1)
    return (jnp.abs(qpos - kpos) <= SWA_WINDOW) & (qpos < n)


def _swa_fill(rep, src_ref):
    for g in range(2):
        rep[:, 256 * g:256 * g + 256] = jnp.concatenate([src_ref[:, 64 * g:64 * g + 64]] * 4, axis=1)


def _swa_head_rows(a):
    lane = lax.broadcasted_iota(jnp.int32, (1, 256), 1)
    return jnp.concatenate([jnp.where(lane // 64 == h, a, jnp.zeros_like(a)) for h in range(4)], axis=0)


def _swa_fold_rows(a, tq):
    lane = lax.broadcasted_iota(jnp.int32, (1, 256), 1)
    out = jnp.where(lane // 64 == 0, a[0:tq], 0.0)
    for h in range(1, 4):
        out = jnp.where(lane // 64 == h, a[h * tq:(h + 1) * tq], out)
    return out


def _swa_fwd(q, k, v, sink, n, tq):
    B, L, _ = q.shape

    def body(q_ref, k_ref, v_ref, sink_ref, o_ref, lse_ref, krep, vrep):
        i = pl.program_id(1)

        @pl.when(i == 0)
        def _():
            _swa_fill(krep, k_ref)
            _swa_fill(vrep, v_ref)

        def tile(band):
            for g in range(2):
                gc = slice(256 * g, 256 * g + 256)
                qs = _swa_head_rows(q_ref[:, gc])
                sk = jnp.concatenate([jnp.broadcast_to(sink_ref[0:1, 4 * g + h:4 * g + h + 1], (tq, 1))
                                      for h in range(4)], axis=0)
                s2 = _dot_nt(qs, krep[n:L, gc]) * SWA_SCALE
                m = jnp.maximum(jnp.max(s2, axis=1, keepdims=True), sk)
                if band:
                    lo, W = _swa_window(i, tq, n)
                    mask = jnp.concatenate([_swa_mask(i, lo, tq, W, n)] * 4, axis=0)
                    s1 = jnp.where(mask, _dot_nt(qs, krep[pl.ds(lo, W), gc]) * SWA_SCALE, NEG_INF)
                    m = jnp.maximum(m, jnp.max(s1, axis=1, keepdims=True))
                e2 = jnp.exp(s2 - m)
                l = jnp.sum(e2, axis=1, keepdims=True) + jnp.exp(sk - m)
                if band:
                    e1 = jnp.exp(s1 - m)
                    l = l + jnp.sum(e1, axis=1, keepdims=True)
                r = 1.0 / l
                o = _dot((e2 * r).astype(BF16), vrep[n:L, gc])
                if band:
                    o = o + _dot((e1 * r).astype(BF16), vrep[pl.ds(lo, W), gc])
                o_ref[:, gc] = _swa_fold_rows(o, tq).astype(o_ref.dtype)
                lse = m + jnp.log(l)
                for h in range(4):
                    lse_ref[:, 4 * g + h:4 * g + h + 1] = lse[h * tq:(h + 1) * tq]

        pl.when(i < n // tq)(functools.partial(tile, True))
        pl.when(i >= n // tq)(functools.partial(tile, False))

    return _pallas(
        body, name="swa_fwd", grid=(B, L // tq),
        in_specs=[_tile_spec(tq, 512), _full_spec(L, 128), _full_spec(L, 128), pl.BlockSpec((1, 8), lambda b, i: (0, 0))],
        out_specs=[_tile_spec(tq, 512), _tile_spec(tq, 8)],
        out_shape=[_sds((B, L, 512), BF16), _sds((B, L, 8), F32)],
        scratch_shapes=[pltpu.VMEM((L, 512), BF16), pltpu.VMEM((L, 512), BF16)],
        compiler_params=_params(("parallel", "arbitrary")),
    )(q, k, v, sink)


def _swa_bwd(q, k, v, sink, lse, do, n, tq):
    B, L, _ = q.shape

    def body(q_ref, k_ref, v_ref, sink_ref, lse_ref, do_ref, dq_ref, dk_ref, dv_ref, dsink_ref):
        i = pl.program_id(1)

        @pl.when(i == 0)
        def _():
            dk_ref[...] = jnp.zeros_like(dk_ref)
            dv_ref[...] = jnp.zeros_like(dv_ref)
            dsink_ref[...] = jnp.zeros_like(dsink_ref)

        def tile(band):
            kc, vc = k_ref[n:L, :], v_ref[n:L, :]
            if band:
                lo, W = _swa_window(i, tq, n)
                mask = _swa_mask(i, lo, tq, W, n)
                kl, vl = k_ref[pl.ds(lo, W), :], v_ref[pl.ds(lo, W), :]
            for h in range(8):
                g = h // 4
                cols = slice(64 * g, 64 * g + 64)
                qh = q_ref[:, 64 * h:64 * h + 64]
                lse_h = lse_ref[:, h:h + 1]
                doh = do_ref[:, 64 * h:64 * h + 64].astype(BF16)
                p2 = jnp.exp(_dot_nt(qh, kc[:, cols]) * SWA_SCALE - lse_h)
                ps = jnp.exp(sink_ref[0:1, h:h + 1] - lse_h)
                dp2 = _dot_nt(doh, vc[:, cols])
                delta = jnp.sum(p2 * dp2, axis=1, keepdims=True)
                if band:
                    p1 = jnp.exp(jnp.where(mask, _dot_nt(qh, kl[:, cols]) * SWA_SCALE, NEG_INF) - lse_h)
                    dp1 = _dot_nt(doh, vl[:, cols])
                    delta = delta + jnp.sum(p1 * dp1, axis=1, keepdims=True)
                ds2 = (p2 * (dp2 - delta) * SWA_SCALE).astype(BF16)
                dq = _dot(ds2, kc[:, cols])
                dk_ref[n:L, cols] += _dot_tn(ds2, qh)
                dv_ref[n:L, cols] += _dot_tn(p2.astype(BF16), doh)
                if band:
                    ds1 = (p1 * (dp1 - delta) * SWA_SCALE).astype(BF16)
                    dq = dq + _dot(ds1, kl[:, cols])
                    dk_ref[pl.ds(lo, W), cols] += _dot_tn(ds1, qh)
                    dv_ref[pl.ds(lo, W), cols] += _dot_tn(p1.astype(BF16), doh)
                dq_ref[:, 64 * h:64 * h + 64] = dq
                dsink_ref[0:1, h:h + 1] += jnp.sum(-ps * delta, axis=0, keepdims=True)

        pl.when(i < n // tq)(functools.partial(tile, True))
        pl.when(i >= n // tq)(functools.partial(tile, False))

    return _pallas(
        body, name="swa_bwd", grid=(B, L // tq),
        in_specs=[_tile_spec(tq, 512), _full_spec(L, 128), _full_spec(L, 128), pl.BlockSpec((1, 8), lambda b, i: (0, 0)),
                  _tile_spec(tq, 8), pl.BlockSpec((None, tq, 512), lambda b, i: (b, i, 1))],
        out_specs=[_tile_spec(tq, 512), _full_spec(L, 128), _full_spec(L, 128),
                   pl.BlockSpec((None, 1, 8), lambda b, i: (b, 0, 0))],
        out_shape=[_sds((B, L, 512), F32), _sds((B, L, 128), F32), _sds((B, L, 128), F32), _sds((B, 1, 8), F32)],
        compiler_params=_params(("parallel", "arbitrary")),
    )(q, k, v, sink, lse, do)


def _ret_decay(i, tq, L, n, ctx_tile):
    qi = i * tq + lax.broadcasted_iota(jnp.int32, (tq, 1), 0)
    kc = lax.broadcasted_iota(jnp.int32, (1, L - n), 1)
    d_hb = (n + kc - qi).astype(F32)
    if ctx_tile:
        return None, ((qi - L) - (kc - (L - n))).astype(F32), d_hb
    d_x = (qi - lax.broadcasted_iota(jnp.int32, (1, n), 1)).astype(F32)
    return d_x, (qi - (kc - (L - n))).astype(F32), d_hb


def _ret_head(a, h):
    lane = lax.broadcasted_iota(jnp.int32, (1, a.shape[1]), 1)
    return jnp.where(lane // 64 == h, a, jnp.zeros_like(a))


def _ret_mask(dist, lg):
    return jnp.where(dist >= 0.0, jnp.exp(lg * jnp.maximum(dist, 0.0)), 0.0)


def _ret_weights(n, L, d_x, d_hf, d_hb, lg_f, lg_b, with_grad):
    m_f, m_b = _ret_mask(d_hf, lg_f), _ret_mask(d_hb, lg_b)
    ctx = (slice(n, L), m_f + m_b)
    if with_grad:
        ctx += (m_f * jnp.maximum(d_hf, 0.0), m_b * jnp.maximum(d_hb, 0.0))
    if d_x is None:
        return [ctx]
    e_x = jnp.exp(jnp.where(d_x >= 0.0, lg_f, -lg_b) * d_x)
    lat = (slice(0, n), jnp.where(d_x == 0.0, 2.0 * e_x, e_x))
    if with_grad:
        lat += (e_x * jnp.maximum(d_x, 0.0), e_x * jnp.maximum(-d_x, 0.0))
    return [lat, ctx]


def _ret_fwd(q, k, v, lg, n, tq):
    B, L, _ = q.shape

    def body(q_ref, k_ref, v_ref, lg_ref, y_ref):
        i = pl.program_id(1)

        def tile(ctx_tile):
            dist = _ret_decay(i, tq, L, n, ctx_tile)
            for h in range(4):
                qh = _ret_head(q_ref[...], h)
                y = None
                for rows, dec_r in _ret_weights(n, L, *dist, lg_ref[0:1, h:h + 1], lg_ref[1:2, h:h + 1], False):
                    a = _dot_nt(qh, k_ref[rows, :]) * dec_r
                    part = _dot(a.astype(BF16), v_ref[rows, 128 * h:128 * h + 128])
                    y = part if y is None else y + part
                y_ref[:, 128 * h:128 * h + 128] = y

        pl.when(i < n // tq)(functools.partial(tile, False))
        pl.when(i >= n // tq)(functools.partial(tile, True))

    return _pallas(
        body, name="ret_fwd", grid=(B, L // tq),
        in_specs=[_tile_spec(tq, 256), _full_spec(L, 256), _full_spec(L, 512), pl.BlockSpec((2, 4), lambda b, i: (0, 0))],
        out_specs=_tile_spec(tq, 512), out_shape=_sds((B, L, 512), F32),
        compiler_params=_params(("parallel", "arbitrary")),
    )(q, k, v, lg)


def _ret_bwd(q, k, v, lg, dy, n, tq):
    B, L, _ = q.shape

    def body(q_ref, k_ref, v_ref, lg_ref, dy_ref, dq_ref, dk_ref, dv_ref, dlg_ref):
        i = pl.program_id(1)

        @pl.when(i == 0)
        def _():
            dk_ref[...] = jnp.zeros_like(dk_ref)
            dv_ref[...] = jnp.zeros_like(dv_ref)
            dlg_ref[...] = jnp.zeros_like(dlg_ref)

        def total(a):
            return jnp.sum(jnp.sum(a, axis=1, keepdims=True), axis=0, keepdims=True)

        def tile(ctx_tile):
            dist = _ret_decay(i, tq, L, n, ctx_tile)
            dq = None
            for h in range(4):
                vc = slice(128 * h, 128 * h + 128)
                qh = _ret_head(q_ref[...], h)
                dyh = dy_ref[:, vc].astype(BF16)
                dqh = None
                for rows, dec_r, wf_r, wb_r in _ret_weights(n, L, *dist, lg_ref[0:1, h:h + 1], lg_ref[1:2, h:h + 1], True):
                    k_all, vh = k_ref[rows, :], v_ref[rows, vc]
                    s = _dot_nt(qh, k_all)
                    gr = _dot_nt(dyh, vh)
                    ds = (gr * dec_r).astype(BF16)
                    part = _dot(ds, k_all)
                    dqh = part if dqh is None else dqh + part
                    dk_ref[rows, :] += _dot_tn(ds, qh)
                    dv_ref[rows, vc] += _dot_tn((s * dec_r).astype(BF16), dyh)
                    gs = gr * s
                    dlg_ref[0:1, h:h + 1] += total(gs * wf_r)
                    dlg_ref[1:2, h:h + 1] += total(gs * wb_r)
                dqh = _ret_head(dqh, h)
                dq = dqh if dq is None else dq + dqh
            dq_ref[...] = dq

        pl.when(i < n // tq)(functools.partial(tile, False))
        pl.when(i >= n // tq)(functools.partial(tile, True))

    return _pallas(
        body, name="ret_bwd", grid=(B, L // tq),
        in_specs=[_tile_spec(tq, 256), _full_spec(L, 256), _full_spec(L, 512), pl.BlockSpec((2, 4), lambda b, i: (0, 0)),
                  _tile_spec(tq, 512)],
        out_specs=[_tile_spec(tq, 256), _full_spec(L, 256), _full_spec(L, 512),
                   pl.BlockSpec((None, 2, 4), lambda b, i: (b, 0, 0))],
        out_shape=[_sds((B, L, 256), F32), _sds((B, L, 256), F32), _sds((B, L, 512), F32), _sds((B, 2, 4), F32)],
        compiler_params=_params(("parallel", "arbitrary")),
    )(q, k, v, lg, dy)


def _loss_head(s, target, g, n, tm):
    B, L, D = s.shape
    nx = n // tm

    def body(s_ref, t_ref, g_ref, ds_ref, dg_ref, loss_ref):
        b, i = pl.program_id(0), pl.program_id(1)

        @pl.when((b == 0) & (i == 0))
        def _():
            dg_ref[...] = jnp.zeros_like(dg_ref)
            loss_ref[...] = jnp.zeros_like(loss_ref)

        @pl.when(i < nx)
        def _():
            y, vjp = jax.vjp(_rms, s_ref[...], g_ref[...])
            err = y - t_ref[...]
            d_s, d_g = vjp(err * (1.0 / D))
            ds_ref[...] = d_s
            dg_ref[...] += d_g
            part = jnp.sum(jnp.sum(err * err, axis=1, keepdims=True), axis=0, keepdims=True) * (0.5 / D)
            loss_ref[...] += jnp.broadcast_to(part, loss_ref.shape)

        @pl.when(i >= nx)
        def _():
            ds_ref[...] = jnp.zeros_like(ds_ref)

    return _pallas(
        body, name="loss_head", grid=(B, L // tm),
        in_specs=[pl.BlockSpec((None, tm, D), lambda b, i: (b, i, 0)),
                  pl.BlockSpec((None, tm, D), lambda b, i: (b, jnp.minimum(i, nx - 1), 0)),
                  pl.BlockSpec((1, D), lambda b, i: (0, 0))],
        out_specs=[pl.BlockSpec((None, tm, D), lambda b, i: (b, i, 0)), pl.BlockSpec((1, D), lambda b, i: (0, 0)),
                   pl.BlockSpec((1, 128), lambda b, i: (0, 0))],
        out_shape=[_sds((B, L, D), F32), _sds((1, D), F32), _sds((1, 128), F32)],
        compiler_params=_params(("arbitrary", "arbitrary")),
    )(s, target, g)


def _ada_fwd(c_all, ada_w, ada_b):
    NL, D, Ns = ada_w.shape
    R = c_all.shape[0]

    def body(c_ref, w_ref, b_ref, o_ref):
        cond = jax.nn.silu(c_ref[...]).astype(BF16)
        o_ref[...] = _dot(cond, w_ref[...].astype(BF16)) + b_ref[...]

    return _pallas(
        body, name="ada_fwd", grid=(NL,),
        in_specs=[pl.BlockSpec((R, D), lambda l: (0, 0)), pl.BlockSpec((None, D, Ns), lambda l: (l, 0, 0)),
                  pl.BlockSpec((None, 1, Ns), lambda l: (l, 0, 0))],
        out_specs=pl.BlockSpec((None, R, Ns), lambda l: (l, 0, 0)), out_shape=_sds((NL, R, Ns), F32),
        compiler_params=_params(("parallel",)),
    )(c_all, ada_w, ada_b)


def _ada_bwd(c_all, ada_w, dmods):
    NL, D, Ns = ada_w.shape
    R = c_all.shape[0]

    def body(c_ref, w_ref, dm_ref, dw_ref, dc_ref):
        cond = jax.nn.silu(c_ref[...]).astype(BF16)
        dm = dm_ref[...].astype(BF16)
        dw_ref[...] = _dot_tn(cond, dm)
        dc_ref[...] = _dot_nt(dm, w_ref[...].astype(BF16))

    return _pallas(
        body, name="ada_bwd", grid=(NL,),
        in_specs=[pl.BlockSpec((R, D), lambda l: (0, 0)), pl.BlockSpec((None, D, Ns), lambda l: (l, 0, 0)),
                  pl.BlockSpec((None, R, Ns), lambda l: (l, 0, 0))],
        out_specs=[pl.BlockSpec((None, D, Ns), lambda l: (l, 0, 0)), pl.BlockSpec((None, R, D), lambda l: (l, 0, 0))],
        out_shape=[_sds((NL, D, Ns), F32), _sds((NL, R, D), F32)],
        compiler_params=_params(("parallel",)),
    )(c_all, ada_w, dmods)


def _my_index():
    return 4 * lax.axis_index("x") + 2 * lax.axis_index("y") + lax.axis_index("c")


def _peer(k):
    x, y, c = lax.axis_index("x"), lax.axis_index("y"), lax.axis_index("c")
    kx, ky, kc = (k >> 2) & 1, (k >> 1) & 1, k & 1
    px, py, pc = (x + kx) % 2, (y + ky) % 2, (c + kc) % 2
    return (px, py, pc), 4 * px + 2 * py + pc


def _all_gather(name, shards):
    na = len(shards)
    hbm = pl.BlockSpec(memory_space=pl.ANY)

    def body(*refs):
        in_refs, out_refs = refs[:na], refs[na:2 * na]
        send_sems, recv_sems, local_sems = refs[2 * na:]
        me = _my_index()
        sib_id, sib = _peer(1)
        chips = [_peer(k) for k in (4, 2, 6)]
        sib_chips = [4 * px + 2 * py + (1 - pc) for (px, py, pc), _ in chips]

        def copy(a, k, slot, to, src=None):
            dst = out_refs[a].at[slot]
            return pltpu.make_async_remote_copy(
                src_ref=dst if src is None else src, dst_ref=dst, send_sem=send_sems.at[a, k],
                recv_sem=recv_sems.at[a, k], device_id=to, device_id_type=MESH)

        first, passed, mine = [], [], []
        for a in range(na):
            cp = pltpu.make_async_copy(in_refs[a], out_refs[a].at[me], local_sems.at[a])
            cp.start()
            mine.append(cp)
            first.append(copy(a, 0, me, sib_id, src=in_refs[a]))
            first += [copy(a, 1 + j, me, pid, src=in_refs[a]) for j, (pid, _) in enumerate(chips)]
        for cp in first:
            cp.start()
        for a in range(na):
            for j, (pid, pidx) in enumerate(chips):
                copy(a, 1 + j, pidx, pid).wait_recv()
                fwd = copy(a, 4 + j, pidx, sib_id)
                fwd.start()
                passed.append(fwd)
        for a in range(na):
            copy(a, 0, sib, sib_id).wait_recv()
            for j in range(3):
                copy(a, 4 + j, sib_chips[j], sib_id).wait_recv()
        for cp in first + passed:
            cp.wait_send()
        for cp in mine:
            cp.wait()

    return _pallas(
        body, name=name, in_specs=[hbm] * na, out_specs=[hbm] * na,
        out_shape=[_sds((N_DEV,) + s.shape, s.dtype) for s in shards],
        scratch_shapes=[pltpu.SemaphoreType.DMA((na, 7)), pltpu.SemaphoreType.DMA((na, 7)),
                        pltpu.SemaphoreType.DMA((na,))],
    )(*shards)


_HBM = pl.BlockSpec(memory_space=pltpu.HBM)
_SEM = pl.BlockSpec(memory_space=pltpu.SEMAPHORE)
_DATAFLOW = pltpu.SideEffectType.DATAFLOW_SIDE_EFFECTING


def _exchange_start(name, srcs, slotted, after=()):
    na = len(srcs)
    lands = [lax.empty((N_DEV,) + (s.shape[1:] if slotted else s.shape), s.dtype) for s in srcs]

    def body(*refs):
        src_refs, land_refs = refs[:na], refs[na:2 * na]
        outs = refs[2 * na + len(after):]
        send_sems, recv_sems, token = outs[:na], outs[na:2 * na], outs[4 * na]
        me = _my_index()
        for a in range(na):
            for k in range(1, N_DEV):
                pid, pidx = _peer(k)
                pltpu.make_async_remote_copy(
                    src_ref=src_refs[a].at[pidx] if slotted else src_refs[a], dst_ref=land_refs[a].at[me],
                    send_sem=send_sems[a], recv_sem=recv_sems[a], device_id=pid, device_id_type=MESH).start()
        token[...] = jnp.zeros_like(token)

    ops = [pltpu.with_memory_space_constraint(a, pltpu.HBM) for a in list(srcs) + lands]
    outs = _pallas(
        body, name=name,
        out_shape=[pltpu.SemaphoreType.DMA(())] * (2 * na) + [pltpu.HBM(a.shape, a.dtype) for a in ops]
        + [_sds((8, 128), F32)],
        in_specs=[_HBM] * (2 * na) + [pl.BlockSpec(memory_space=pl.ANY)] * len(after),
        out_specs=[_SEM] * (2 * na) + [_HBM] * (2 * na) + [pl.BlockSpec(memory_space=pltpu.VMEM)],
        input_output_aliases={a: 2 * na + a for a in range(2 * na)},
        compiler_params=pltpu.CompilerParams(has_side_effects=_DATAFLOW),
    )(*ops, *after)
    return (na, outs[:2 * na], outs[2 * na:4 * na]), outs[4 * na]


def _exchange_wait(name, handle, after):
    na, sems, thru = handle

    def body(*refs):
        land_refs = refs[na:2 * na]
        send_sems, recv_sems = refs[2 * na:3 * na], refs[3 * na:4 * na]
        me_id = (lax.axis_index("x"), lax.axis_index("y"), lax.axis_index("c"))
        for a in range(na):
            seven = land_refs[a].at[pl.ds(0, N_DEV - 1)]
            drain = pltpu.make_async_remote_copy(src_ref=seven, dst_ref=seven, send_sem=send_sems[a],
                                                 recv_sem=recv_sems[a], device_id=me_id, device_id_type=MESH)
            drain.wait_send()
            drain.wait_recv()

    outs = _pallas(
        body, name=name, out_shape=[pltpu.HBM(a.shape, a.dtype) for a in thru],
        in_specs=[_HBM] * (2 * na) + [_SEM] * (2 * na) + [pl.BlockSpec(memory_space=pl.ANY)],
        out_specs=[_HBM] * (2 * na), input_output_aliases={a: a for a in range(2 * na)},
        compiler_params=pltpu.CompilerParams(has_side_effects=_DATAFLOW),
    )(*thru, *sems, after)
    return outs[na:]


def _fill_own(name, landed, owns, slotted):
    me = _my_index()
    return [jnp.where(lax.broadcasted_iota(jnp.int32, land.shape, 0) == me, own, land)
            for land, own in zip(landed, owns)]


def _rope_tables(n, L, width):
    t = jnp.arange(n)
    row = (t // GRID_W).astype(F32)
    col = (t % GRID_W).astype(F32)
    n_freq = ROPE_DIM // 4
    freqs = ROPE_THETA ** (-jnp.arange(n_freq, dtype=F32) / n_freq)
    ang = jnp.concatenate([row[:, None] * freqs, col[:, None] * freqs], axis=-1)
    cos, sin = jnp.cos(ang), jnp.sin(ang)
    cos2 = jnp.concatenate([cos, cos], axis=-1)
    sin2 = jnp.concatenate([-sin, sin], axis=-1)
    cos2 = jnp.concatenate([cos2, jnp.ones((L - n, ROPE_DIM), F32)], axis=0)
    sin2 = jnp.concatenate([sin2, jnp.zeros((L - n, ROPE_DIM), F32)], axis=0)
    reps = width // ROPE_DIM
    return jnp.tile(cos2, (1, reps)), jnp.tile(sin2, (1, reps))


def _ab_perm(w):
    return jnp.concatenate([w[:, 0:256], w[:, 320:1600], w[:, 256:320], w[:, 256:320]], axis=1)


def _ab_unperm(g):
    rope_key = (g[:, 1536:1600].astype(F32) + g[:, 1600:1664].astype(F32)).astype(g.dtype)
    return jnp.concatenate([g[:, 0:256], rope_key, g[:, 256:1536]], axis=1)


def _wq_perm(w):
    return jnp.concatenate([w[:, 192 * h:192 * h + 128] for h in range(4)]
                           + [w[:, 192 * h + 128:192 * h + 192] for h in range(4)], axis=1)


def _wq_unperm(g):
    return jnp.concatenate([g[:, sl] for h in range(4)
                            for sl in (slice(128 * h, 128 * h + 128), slice(512 + 64 * h, 576 + 64 * h))], axis=1)


def _flat(a):
    return a.reshape(-1, a.shape[-1])


def _layer_weights(full, p):
    NL, NE, NO = len(full["ffn_in"]), len(full["ab_in"]), len(full["cd_in"])
    groups = range(4)

    def each(f, mats):
        return [None if w is None else f(w) for w in mats]

    return dict(
        norm_mix=[p["norm_mix"][l][None] for l in range(NL)], norm_ffn=[p["norm_ffn"][l][None] for l in range(NL)],
        norm_final=p["norm_final"][None],
        ffn_in=each(_ffn_interleave, full["ffn_in"]), ffn_out=list(full["ffn_out"]),
        ab_in=each(_ab_perm, full["ab_in"]), ab_out=list(full["ab_out"]),
        wq=each(_wq_perm, full["mla_wq_b"]), wkv=list(full["mla_wkv_b"]),
        kv_norm=[p["mla_kv_norm"][j][None] for j in range(NE)], q_norm=[p["mla_q_norm"][j][None] for j in range(NE)],
        v_norm=[[p["cmlp_v_norm"][j][None, 128 * g:128 * g + 128] for g in groups] for j in range(NE)],
        ws=[[p["cmlp_ws"][j, g] for g in groups] for j in range(NE)],
        bs=[[p["cmlp_bs"][j, g][:, None] for g in groups] for j in range(NE)],
        cd_in=list(full["cd_in"]), cd_out=list(full["cd_out"]),
        lg=[jnp.stack([jax.nn.log_sigmoid(p["ret_decay_fwd"][j]), jax.nn.log_sigmoid(p["ret_decay_bwd"][j])])
            for j in range(NO)],
        sink=[p["swa_sink"][j][None] for j in range(NO)],
        ret_norm=[[p["ret_norm"][j][None, 128 * g:128 * g + 128] for g in groups] for j in range(NO)],
    )


def _local_step(x, ctx, target, mods, W, later_weights=None, early_grads=None):
    B, n, D = x.shape
    m = ctx.shape[1]
    L = n + m
    NL = mods.shape[0]
    tm = min(256, m)
    tq = min(256, m)
    cos512, sin512 = _rope_tables(n, L, 512)
    s = jnp.concatenate([x, ctx], axis=1)
    saved = []

    def rows(name, f, arrays, pieces, samp, samp_pieces, glob, out_arrays, out_pieces, tile=tm):
        return _Rows(name, f, B, L, n, tile, arrays, pieces, samp, samp_pieces, glob, out_arrays, out_pieces)

    def full(width, start=0):
        return [(0, start, width)]

    for l in range(NL):
        j = l // 2
        even = l % 2 == 0
        md = mods[l]
        if l == 1 and later_weights is not None:
            W = later_weights(1, s)
        r = {}
        if l == 0:
            r["pre1"] = rows("pre_mix0", _f_pre, [s], full(D), [md], [(0, 0), (0, 1)], [W["norm_mix"][0]], [(D, BF16)], full(D))
            (xn,) = r["pre1"].fwd()
        r["xn"] = xn
        if even:
            z = _mm(f"ab_in{l}", _flat(xn), W["ab_in"][j]).reshape(B, L, 1664)
            pieces = [(0, 0, 256), (0, 256, 256), (0, 1536, 128)]
            pieces += [(0, 512 + 128 * g, 128) for g in range(4)] + [(0, 1024 + 128 * g, 128) for g in range(4)]
            pieces += [(1, 0, 128), (2, 0, 128)]
            glob = [W["kv_norm"][j], W["q_norm"][j]] + W["v_norm"][j] + W["ws"][j] + W["bs"][j]
            abp = rows(f"ab_pre{l}", _f_ab_pre, [z, cos512, sin512], pieces, [], [], glob,
                       [(256, BF16), (256, BF16), (128, BF16), (512, BF16)],
                       [(0, 0, 256), (1, 0, 256), (2, 0, 128)] + [(3, 128 * g, 128) for g in range(4)], tile=128)
            kvn, qn, kpe, cm = abp.fwd()
            kv = _mm(f"wkv{l}", _flat(kvn), W["wkv"][j], out_dtype=BF16).reshape(B, L, 1024)
            q0 = _mm(f"wq{l}", _flat(qn), W["wq"][j]).reshape(B, L, 768)
            qrp = rows(f"q_rope{l}", _f_q_rope, [q0, cos512, sin512], [(0, 0, 512), (0, 512, 256), (1, 0, 256), (2, 0, 256)],
                       [], [], [], [(768, BF16)], [(0, 0, 512), (0, 512, 256)])
            (q,) = qrp.fwd()
            o, lse = _mla_fwd(q, kv, kpe, n, tq)
            y = _mm_pair(f"ab_out{l}", _flat(o), _flat(cm), W["ab_out"][j], BF16).reshape(B, L, D)
            r.update(z=z, abp=abp, kvn=kvn, qn=qn, kpe=kpe, cm=cm, kv=kv, qrp=qrp, q=q, o=o, lse=lse)
        else:
            z = _mm(f"cd_in{l}", _flat(xn), W["cd_in"][j]).reshape(B, L, 2304)
            pieces = [(0, 0, 256), (0, 256, 512), (0, 768, 128), (0, 896, 128), (0, 1024, 256), (0, 1280, 512), (0, 1792, 512)]
            pieces += [(1, 0, 256), (2, 0, 256), (1, 0, 128), (2, 0, 128), (1, 0, 512), (2, 0, 512)]
            cdp = rows(f"cd_pre{l}", _f_cd_pre, [z, cos512, sin512], pieces, [], [], [],
                       [(256, BF16), (256, BF16), (512, BF16), (512, BF16), (128, BF16), (128, BF16), (512, F32)],
                       [(k_, 0, w_) for k_, w_ in enumerate((256, 256, 512, 512, 128, 128, 512))])
            rq, rk, rv, sq, sk, sv, rg = cdp.fwd()
            yret = _ret_fwd(rq, rk, rv, W["lg"][j], n, tq)
            osw, lse = _swa_fwd(sq, sk, sv, W["sink"][j], n, tq)
            mrg = rows(f"cd_merge{l}", _f_cd_merge, [yret, rg],
                       [(0, 128 * g, 128) for g in range(4)] + [(1, 128 * g, 128) for g in range(4)], [], [],
                       W["ret_norm"][j], [(512, BF16)], [(0, 128 * g, 128) for g in range(4)])
            (yr,) = mrg.fwd()
            y = _mm_pair(f"cd_out{l}", _flat(yr), _flat(osw), W["cd_out"][j], BF16).reshape(B, L, D)
            r.update(z=z, cdp=cdp, rq=rq, rk=rk, rv=rv, sq=sq, sk=sk, sv=sv, rg=rg, yret=yret, osw=osw, lse=lse,
                     mrg=mrg, yr=yr)
        two, outs2 = [(0, 0, D), (1, 0, D)], [(D, F32), (D, BF16)]
        r["mix_out"] = rows(f"mix_res_pre{l}", _f_res_pre, [s, y], two, [md], [(0, 2), (0, 3), (0, 4)],
                            [W["norm_ffn"][l]], outs2, two)
        s1, xn2 = r["mix_out"].fwd()
        if l == 0 and later_weights is not None:
            W = later_weights(0, s1)
        y2 = _ffn_fwd(f"ffn{l}", _flat(xn2), W["ffn_in"][l], W["ffn_out"][l]).reshape(B, L, D)
        if l < NL - 1:
            r["ffn_out"] = rows(f"ffn_res_pre{l}", _f_res_pre, [s1, y2], two, [md, mods[l + 1]], [(0, 5), (1, 0), (1, 1)],
                                [W["norm_mix"][l + 1]], outs2, two)
            s, xn = r["ffn_out"].fwd()
        else:
            r["ffn_out"] = rows(f"res_ffn{l}", _f_res, [s1, y2], two, [md], [(0, 5)], [], [(D, F32)], full(D))
            (s,) = r["ffn_out"].fwd()
        r["xn2"] = xn2
        saved.append(r)

    ds, d_norm_final, loss = _loss_head(s, target, W["norm_final"], n, tm)

    G = {k: [None] * len(v) for k, v in W.items() if isinstance(v, list)}
    G["norm_final"] = d_norm_final
    dm = [[None] * 6 for _ in range(NL)]
    dxn_next = None
    for l in reversed(range(NL)):
        j = l // 2
        even = l % 2 == 0
        r = saved[l]
        zero = early_grads(0, G) if l == 0 and early_grads is not None else None
        if l == NL - 1:
            (ds1, dy2), (dm[l][5],), _ = r["ffn_out"].bwd([ds], {0: F32, 1: BF16}, unread=(0,))
        else:
            smp = None if zero is None else [mods[l] + zero, mods[l + 1] + zero]
            (ds1, dy2), (dm[l][5], dm[l + 1][0], dm[l + 1][1]), (G["norm_mix"][l + 1],) = r["ffn_out"].bwd(
                [ds, dxn_next], {0: F32, 1: BF16}, grad_glob=(0,), samp=smp)
        dy2f, xn2f = _flat(dy2), _flat(r["xn2"])
        hid, dz2 = _ffn_mid_bwd(f"ffn_mid{l}", xn2f, dy2f, W["ffn_in"][l], W["ffn_out"][l])
        G["ffn_out"][l] = _mm(f"g_ffn_out{l}", hid, dy2f, ta=True, out_dtype=BF16)
        G["ffn_in"][l] = _mm(f"g_ffn_in{l}", xn2f, dz2, ta=True, out_dtype=BF16)
        dxn2 = _mm(f"d_xn2{l}", dz2, W["ffn_in"][l], tb=True, out_dtype=BF16).reshape(B, L, D)
        zero = early_grads(1, G) if l == 0 and early_grads is not None else None
        smp = None if zero is None else [mods[l] + zero]
        (ds0, dy), (dm[l][2], dm[l][3], dm[l][4]), (G["norm_ffn"][l],) = r["mix_out"].bwd(
            [ds1, dxn2], {0: F32, 1: BF16}, grad_glob=(0,), samp=smp)
        dyf = _flat(dy)
        if even:
            w_out = W["ab_out"][j]
            dcat = _mm(f"d_cat{l}", dyf, w_out, tb=True, out_dtype=BF16).reshape(B, L, -1)
            G["ab_out"][l // 2] = jnp.concatenate(
                [_mm(f"g_ab_out_a{l}", _flat(r["o"]), dyf, ta=True, out_dtype=BF16),
                 _mm(f"g_ab_out_b{l}", _flat(r["cm"]), dyf, ta=True, out_dtype=BF16)], axis=0)
            dq, dkv, dkpe = _mla_bwd(r["q"], r["kv"], r["kpe"], r["lse"], dcat, n, tq)
            (dq0,), _, _ = r["qrp"].bwd([dq], {0: BF16}, unread=(0,))
            dq0f, dkvf = _flat(dq0), _flat(dkv)
            G["wq"][j] = _mm(f"g_wq{l}", _flat(r["qn"]), dq0f, ta=True, out_dtype=BF16)
            G["wkv"][j] = _mm(f"g_wkv{l}", _flat(r["kvn"]), dkvf, ta=True, out_dtype=BF16)
            dqn = _mm(f"d_qn{l}", dq0f, W["wq"][j], tb=True).reshape(B, L, 256)
            dkvn = _mm(f"d_kvn{l}", dkvf, W["wkv"][j], tb=True).reshape(B, L, 256)
            (dz,), _, gg = r["abp"].bwd([dkvn, dqn, dkpe, (dcat, 512)], {0: BF16}, grad_glob=tuple(range(14)))
            G["kv_norm"][j], G["q_norm"][j] = gg[0], gg[1]
            G["v_norm"][j], G["ws"][j], G["bs"][j] = list(gg[2:6]), list(gg[6:10]), list(gg[10:14])
            w_in, key = W["ab_in"][j], "ab_in"
        else:
            w_out = W["cd_out"][j]
            dcat = _mm(f"d_cat{l}", dyf, w_out, tb=True, out_dtype=BF16).reshape(B, L, -1)
            G["cd_out"][j] = jnp.concatenate(
                [_mm(f"g_cd_out_a{l}", _flat(r["yr"]), dyf, ta=True, out_dtype=BF16),
                 _mm(f"g_cd_out_b{l}", _flat(r["osw"]), dyf, ta=True, out_dtype=BF16)], axis=0)
            (dyret, drg), _, gg = r["mrg"].bwd([(dcat, 0)], {0: F32, 1: F32}, grad_glob=(0, 1, 2, 3))
            G["ret_norm"][j] = list(gg)
            drq, drk, drv, dlg = _ret_bwd(r["rq"], r["rk"], r["rv"], W["lg"][j], dyret, n, tq)
            dsq, dsk, dsv, dsink = _swa_bwd(r["sq"], r["sk"], r["sv"], W["sink"][j], r["lse"], dcat, n, tq)
            G["lg"][j], G["sink"][j] = dlg, dsink
            (dz,), _, _ = r["cdp"].bwd([drq, drk, drv, dsq, dsk, dsv, drg], {0: BF16}, unread=(0,))
            w_in, key = W["cd_in"][j], "cd_in"
        dzf = _flat(dz)
        G[key][j] = _mm(f"g_{key}{l}", _flat(r["xn"]), dzf, ta=True, out_dtype=BF16)
        dxn = _mm(f"d_xn{l}", dzf, w_in, tb=True, out_dtype=BF16).reshape(B, L, D)
        if l == 0 and early_grads is not None:
            smp = [mods[0] + early_grads(2, G)]
        if l == 0:
            (ds,), (dm[0][0], dm[0][1]), (G["norm_mix"][0],) = r["pre1"].bwd([dxn], {0: F32}, grad_glob=(0,),
                                                                            add={0: ds0}, samp=smp)
        else:
            ds, dxn_next = ds0, dxn
    dmods = jnp.stack([jnp.concatenate(d, axis=2) for d in dm])
    return loss, ds[:, :n], dmods, G


def kernel(x, c, ctx, c_ctx, ada_w, ada_b, norm_mix, norm_ffn, norm_final, ffn_in, ffn_out, ab_in, ab_out, mla_q_norm, mla_kv_norm, mla_wq_b, mla_wkv_b, cmlp_v_norm, cmlp_ws, cmlp_bs, cd_in, cd_out, ret_decay_fwd, ret_decay_bwd, ret_norm, swa_sink, loss_target, m_c_ctx, m_ada_w, m_ada_b, m_norm_mix, m_norm_ffn, m_norm_final, m_ffn_in, m_ffn_out, m_ab_in, m_ab_out, m_mla_q_norm, m_mla_kv_norm, m_mla_wq_b, m_mla_wkv_b, m_cmlp_v_norm, m_cmlp_ws, m_cmlp_bs, m_cd_in, m_cd_out, m_ret_decay_fwd, m_ret_decay_bwd, m_ret_norm, m_swa_sink, v_c_ctx, v_ada_w, v_ada_b, v_norm_mix, v_norm_ffn, v_norm_final, v_ffn_in, v_ffn_out, v_ab_in, v_ab_out, v_mla_q_norm, v_mla_kv_norm, v_mla_wq_b, v_mla_wkv_b, v_cmlp_v_norm, v_cmlp_ws, v_cmlp_bs, v_cd_in, v_cd_out, v_ret_decay_fwd, v_ret_decay_bwd, v_ret_norm, v_swa_sink):
    B, n, D = x.shape
    NL = ada_w.shape[0]
    NE, NO = ab_in.shape[0], cd_in.shape[0]
    me = _my_index()
    weights = dict(c_ctx=c_ctx, ada_w=ada_w, ada_b=ada_b, norm_mix=norm_mix, norm_ffn=norm_ffn, norm_final=norm_final,
                   ffn_in=ffn_in, ffn_out=ffn_out, ab_in=ab_in, ab_out=ab_out, mla_q_norm=mla_q_norm,
                   mla_kv_norm=mla_kv_norm, mla_wq_b=mla_wq_b, mla_wkv_b=mla_wkv_b, cmlp_v_norm=cmlp_v_norm,
                   cmlp_ws=cmlp_ws, cmlp_bs=cmlp_bs, cd_in=cd_in, cd_out=cd_out, ret_decay_fwd=ret_decay_fwd,
                   ret_decay_bwd=ret_decay_bwd, ret_norm=ret_norm, swa_sink=swa_sink)
    moments_m = dict(c_ctx=m_c_ctx, ada_w=m_ada_w, ada_b=m_ada_b, norm_mix=m_norm_mix, norm_ffn=m_norm_ffn,
                     norm_final=m_norm_final, ffn_in=m_ffn_in, ffn_out=m_ffn_out, ab_in=m_ab_in, ab_out=m_ab_out,
                     mla_q_norm=m_mla_q_norm, mla_kv_norm=m_mla_kv_norm, mla_wq_b=m_mla_wq_b, mla_wkv_b=m_mla_wkv_b,
                     cmlp_v_norm=m_cmlp_v_norm, cmlp_ws=m_cmlp_ws, cmlp_bs=m_cmlp_bs, cd_in=m_cd_in, cd_out=m_cd_out,
                     ret_decay_fwd=m_ret_decay_fwd, ret_decay_bwd=m_ret_decay_bwd, ret_norm=m_ret_norm,
                     swa_sink=m_swa_sink)
    moments_v = dict(c_ctx=v_c_ctx, ada_w=v_ada_w, ada_b=v_ada_b, norm_mix=v_norm_mix, norm_ffn=v_norm_ffn,
                     norm_final=v_norm_final, ffn_in=v_ffn_in, ffn_out=v_ffn_out, ab_in=v_ab_in, ab_out=v_ab_out,
                     mla_q_norm=v_mla_q_norm, mla_kv_norm=v_mla_kv_norm, mla_wq_b=v_mla_wq_b, mla_wkv_b=v_mla_wkv_b,
                     cmlp_v_norm=v_cmlp_v_norm, cmlp_ws=v_cmlp_ws, cmlp_bs=v_cmlp_bs, cd_in=v_cd_in, cd_out=v_cd_out,
                     ret_decay_fwd=v_ret_decay_fwd, ret_decay_bwd=v_ret_decay_bwd, ret_norm=v_ret_norm,
                     swa_sink=v_swa_sink)
    order = list(weights)

    Ns = ada_w.shape[2]
    (c_g,) = _all_gather("gather_c", [c])
    R = N_DEV * B + 8
    c_all = jnp.concatenate([c_g.reshape(N_DEV * B, D), jnp.broadcast_to(c_ctx[None], (8, D))], axis=0)
    ada_b_mine = lax.dynamic_slice_in_dim(ada_b, me * Ns, Ns, axis=1)[:, None, :]
    mods_shard = _ada_fwd(c_all, ada_w, ada_b_mine)
    (mods_g,) = _all_gather("gather_mods", [mods_shard])
    mods_full = jnp.transpose(mods_g, (1, 2, 0, 3)).reshape(NL, R, 6, D)
    mx = lax.dynamic_slice_in_dim(mods_full, me * B, B, axis=1)
    mh = jnp.broadcast_to(mods_full[:, N_DEV * B][:, None], (NL, B, 6, D))
    mods = jnp.stack([mx, mh], axis=2)

    big = ["ffn_in", "ffn_out", "ab_in", "ab_out", "cd_in", "cd_out", "mla_wq_b", "mla_wkv_b"]
    col_sharded = {"ffn_in", "ab_in", "cd_in", "mla_wq_b", "mla_wkv_b"}
    shards = {k: _to_bf16("cast_" + k, weights[k]) for k in big}
    first = {k: 0 if k.startswith("cd_") else 1 for k in big}
    a_keys = [k for k in big if first[k] and not k.startswith("ffn_")]
    b_keys = ["ffn_in", "ffn_out"]
    early = _all_gather("gather_wA", [shards[k][:1] for k in a_keys])
    (rn_g,) = _all_gather("gather_ret_norm", [ret_norm])
    small_first = [mods, rn_g] + list(early)
    wb_handle, wb_token = _exchange_start("gather_wB_start", [shards[k][:1] for k in b_keys], False, small_first)
    wc_handle, wc_token = _exchange_start("gather_wC_start", [shards[k][first[k]:] for k in big], False,
                                          small_first + [wb_token])
    mods = mods + (wb_token[0, 0] + wc_token[0, 0])

    def unshard(k, g):
        if k in col_sharded:
            f = jnp.transpose(g, (1, 2, 0, 3)).reshape(g.shape[1], g.shape[2], -1)
        else:
            f = jnp.transpose(g, (1, 0, 2, 3)).reshape(g.shape[1], -1, g.shape[3])
        return [f[i] for i in range(f.shape[0])]

    rn_full = jnp.transpose(rn_g, (1, 0, 2)).reshape(NO, -1)
    small_p = dict(weights, ret_norm=rn_full)
    full0 = {k: [None] * weights[k].shape[0] for k in big}
    for k, g in zip(a_keys, early):
        full0[k][:1] = unshard(k, g)

    def later_weights(stage, newest):
        if stage == 0:
            landed = _exchange_wait("gather_wB_wait", wb_handle, newest)
            landed = _fill_own("own_wB", landed, [shards[k][:1] for k in b_keys], False)
            for k, land in zip(b_keys, landed):
                full0[k][:1] = unshard(k, land)
        else:
            landed = _exchange_wait("gather_wC_wait", wc_handle, newest)
            landed = _fill_own("own_wC", landed, [shards[k][first[k]:] for k in big], False)
            for k, land in zip(big, landed):
                full0[k][first[k]:] = unshard(k, land)
        return _layer_weights(full0, small_p)

    def to_slots(k, gl):
        g = jnp.stack(gl)
        if k == "ffn_in":
            half = N_DEV // 2
            g = g.reshape(g.shape[0], 2, g.shape[2], half, -1)
            return jnp.transpose(g, (1, 3, 0, 2, 4)).reshape(N_DEV, g.shape[0], g.shape[2], -1)
        if k in col_sharded:
            return jnp.transpose(g.reshape(g.shape[0], g.shape[1], N_DEV, -1), (2, 0, 1, 3))
        return jnp.transpose(g.reshape(g.shape[0], N_DEV, -1, g.shape[2]), (1, 0, 2, 3))

    def big_grads(G):
        return dict(ffn_in=[g if g is None else _ffn_deinterleave(g) for g in G["ffn_in"]], ffn_out=G["ffn_out"],
                    ab_in=[g if g is None else _ab_unperm(g) for g in G["ab_in"]],
                    ab_out=G["ab_out"], cd_in=G["cd_in"], cd_out=G["cd_out"],
                    mla_wq_b=[g if g is None else _wq_unperm(g) for g in G["wq"]], mla_wkv_b=G["wkv"])

    sent = {}

    def early_grads(stage, G):
        parts = big_grads(G)
        if stage == 0:
            srcs = [to_slots(k, parts[k][first[k]:]) for k in big]
            handle, g_token = _exchange_start("scatter_gC_start", srcs, slotted=True)
        elif stage == 1:
            srcs = [to_slots(k, parts[k][:1]) for k in b_keys]
            handle, g_token = _exchange_start("scatter_gB_start", srcs, slotted=True)
        else:
            srcs = [to_slots(k, parts[k][:1]) for k in a_keys]
            handle, g_token = _exchange_start("scatter_gA_start", srcs, slotted=True)
        sent[stage] = (handle, srcs)
        return g_token[0, 0]

    loss_part, grad_x, dmods, G = _local_step(x, ctx, loss_target, mods, _layer_weights(full0, small_p),
                                              later_weights, early_grads)

    dmx = dmods[:, :, 0].reshape(NL, B, 6 * D)
    dmh = jnp.sum(dmods[:, :, 1], axis=1).reshape(NL, 1, 6 * D)
    (dm_g,) = _all_gather("gather_dmods", [jnp.concatenate([dmx, dmh], axis=1)])
    dmx_all = jnp.transpose(dm_g[:, :, :B], (1, 0, 2, 3)).reshape(NL, N_DEV * B, 6 * D)
    dmh_all = jnp.sum(dm_g[:, :, B], axis=0)
    dm_rows = jnp.concatenate([dmx_all, dmh_all[:, None], jnp.zeros((NL, 7, 6 * D), F32)], axis=1)
    g_ada_b = jnp.sum(dm_rows, axis=1)
    dm_mine = lax.dynamic_slice_in_dim(dm_rows, me * Ns, Ns, axis=2)
    g_ada_w, dcond = _ada_bwd(c_all, ada_w, dm_mine)
    sg = jax.nn.sigmoid(c_ctx)
    d_c_ctx_part = jnp.sum(dcond[:, N_DEV * B], axis=0) * (sg * (1.0 + c_ctx * (1.0 - sg)))

    def cat(parts):
        return jnp.concatenate([p.reshape(-1) for p in parts])

    dlg = jnp.stack([jnp.sum(G["lg"][j], axis=0) for j in range(NO)])
    sig_f, sig_b = jax.nn.sigmoid(-ret_decay_fwd), jax.nn.sigmoid(-ret_decay_bwd)
    small = dict(
        loss=loss_part[0, 0:1],
        c_ctx=d_c_ctx_part,
        norm_mix=cat(G["norm_mix"]), norm_ffn=cat(G["norm_ffn"]), norm_final=G["norm_final"].reshape(-1),
        mla_q_norm=cat(G["q_norm"]), mla_kv_norm=cat(G["kv_norm"]),
        cmlp_v_norm=cat([cat(G["v_norm"][j]) for j in range(NE)]),
        cmlp_ws=cat([jnp.stack(G["ws"][j]) for j in range(NE)]),
        cmlp_bs=cat([jnp.stack([b_[:, 0] for b_ in G["bs"][j]]) for j in range(NE)]),
        ret_decay_fwd=(dlg[:, 0] * sig_f).reshape(-1), ret_decay_bwd=(dlg[:, 1] * sig_b).reshape(-1),
        ret_norm=cat([cat(G["ret_norm"][j]) for j in range(NO)]),
        swa_sink=cat([jnp.sum(G["sink"][j], axis=0) for j in range(NO)]),
    )
    small_keys = list(small)
    sizes = [small[k].shape[0] for k in small_keys]
    total = sum(sizes)
    padded = -(-total // 2048) * 2048
    packed = jnp.concatenate([small[k] for k in small_keys] + [jnp.zeros((padded - total,), F32)]).reshape(-1, 128)
    (small_g,) = _all_gather("gather_small", [packed])

    landed_a = _fill_own("own_gA", _exchange_wait("scatter_gA_wait", sent[2][0], small_g), sent[2][1], True)
    landed0 = dict(zip(a_keys, landed_a))
    landed_b = _fill_own("own_gB", _exchange_wait("scatter_gB_wait", sent[1][0], grad_x), sent[1][1], True)
    landed0.update(zip(b_keys, landed_b))
    landed_c = _fill_own("own_gC", _exchange_wait("scatter_gC_wait", sent[0][0], grad_x), sent[0][1], True)
    landed = [[landed0[k], rest] if first[k] else [rest] for k, rest in zip(big, landed_c)]

    grads, deltas, new_m, new_v = {}, {}, {}, {}
    for k, land in zip(big, landed):
        grads[k], deltas[k], new_m[k], new_v[k] = _adamw_from_slots("adamw_" + k, weights[k], moments_m[k], moments_v[k], land)
    deltas["ada_w"], new_m["ada_w"], new_v["ada_w"] = [
        o.reshape(ada_w.shape) for o in _adamw("adamw_ada_w", _flat(ada_w), _flat(g_ada_w), _flat(m_ada_w), _flat(v_ada_w))]
    grads["ada_w"] = g_ada_w

    sums_only = ("loss", "ret_norm")

    def packed_of(src, fill):
        vals = [jnp.full((sizes[i],), fill, F32) if k in sums_only else src[k].reshape(-1)
                for i, k in enumerate(small_keys)]
        return jnp.concatenate(vals + [jnp.full((padded - total,), fill, F32)]).reshape(-1, 128)

    w_p, m_p, v_p = packed_of(weights, 0.0), packed_of(moments_m, 0.0), packed_of(moments_v, 1.0)

    def f_small(w_, m_, v_, land_):
        g = _sum_slots(land_)
        return (g,) + _adamw_math(w_, g, m_, v_)

    g_p, d_p, nm_p, nv_p = _ew("adamw_small", f_small, [w_p, m_p, v_p, small_g], [F32] * 4)
    offs = np.cumsum([0] + sizes)
    for i, k in enumerate(small_keys):
        sl = slice(int(offs[i]), int(offs[i + 1]))
        if k == "loss":
            loss = g_p.reshape(-1)[int(offs[i])]
        elif k == "ret_norm":
            g_full = g_p.reshape(-1)[sl].reshape(NO, -1)
            g_mine = lax.dynamic_slice_in_dim(g_full, me * ret_norm.shape[1], ret_norm.shape[1], axis=1)
            d_, m_, v_ = _adamw("adamw_ret_norm", *[jnp.pad(a, ((0, 8 - NO), (0, 128 - a.shape[1])), constant_values=cv)
                                                     for a, cv in ((ret_norm, 0.0), (g_mine, 0.0), (m_ret_norm, 0.0), (v_ret_norm, 1.0))])
            grads[k] = g_mine
            deltas[k], new_m[k], new_v[k] = [a[:NO, :ret_norm.shape[1]] for a in (d_, m_, v_)]
        else:
            shp = weights[k].shape
            grads[k], deltas[k], new_m[k], new_v[k] = [a.reshape(-1)[sl].reshape(shp) for a in (g_p, d_p, nm_p, nv_p)]
    pad_b = lambda a, cv=0.0: jnp.pad(a, ((0, 8 - NL), (0, 0)), constant_values=cv)
    d_, m_, v_ = _adamw("adamw_ada_b", pad_b(ada_b), pad_b(g_ada_b), pad_b(m_ada_b), pad_b(v_ada_b, 1.0))
    grads["ada_b"] = g_ada_b
    deltas["ada_b"], new_m["ada_b"], new_v["ada_b"] = d_[:NL], m_[:NL], v_[:NL]

    return (loss, grad_x, *[grads[k] for k in order], *[deltas[k] for k in order],
            *[new_m[k] for k in order], *[new_v[k] for k in order])
```

```python
import functools

import numpy as np
import jax
import jax.numpy as jnp
from jax import lax
from jax.experimental import pallas as pl
from jax.experimental.pallas import tpu as pltpu

F32 = jnp.float32
BF16 = jnp.bfloat16
EPS = 1e-6
NEG_INF = -1e30
N_DEV = 8
GRID_W = 64
ROPE_THETA = 10000.0
ROPE_DIM = 64
SWA_WINDOW = 128
MLA_SCALE = (128 + 64) ** -0.5
SWA_SCALE = 64 ** -0.5
RET_K_SCALE = 64 ** -0.5
ADAM_LR, ADAM_B1, ADAM_B2, ADAM_EPS, ADAM_WD, ADAM_STEP = 0.001, 0.9, 0.999, 1e-08, 0.01, 10
V7X_VMEM_LIMIT = 56 * 1024 * 1024
ROW_BLOCK_BUDGET = 24 * 1024 * 1024
MESH = pl.DeviceIdType.MESH


def _pallas(body, **kw):
    return pl.pallas_call(body, **kw)


def _params(sem=None):
    return pltpu.CompilerParams(dimension_semantics=sem, vmem_limit_bytes=V7X_VMEM_LIMIT)


def _tile(n, cap, align):
    best = None
    for t in range(align, min(n, cap) + 1, align):
        if n % t == 0:
            best = t
    return n if best is None else best


def _sds(shape, dtype):
    return jax.ShapeDtypeStruct(tuple(shape), dtype)


def _ew(name, f, ins, out_dtypes, cap_elems=131072):
    R, C = ins[0].shape[-2:]
    tr = _tile(R, max(16, cap_elems // C), 16)
    n_in = len(ins)

    def spec(a):
        if a.ndim == 2:
            return pl.BlockSpec((tr, C), lambda i: (i, 0))
        return pl.BlockSpec((a.shape[0], tr, C), lambda i: (0, i, 0))

    def body(*refs):
        outs = f(*[r[...] for r in refs[:n_in]])
        for r, o in zip(refs[n_in:], outs):
            r[...] = o.astype(r.dtype)

    return _pallas(
        body, name=name, grid=(R // tr,), in_specs=[spec(a) for a in ins],
        out_specs=[pl.BlockSpec((tr, C), lambda i: (i, 0)) for _ in out_dtypes],
        out_shape=[_sds((R, C), d) for d in out_dtypes], compiler_params=_params(("parallel",)),
    )(*ins)


def _to_bf16(name, w):
    w2 = w.reshape(-1, w.shape[-1])
    return _ew(name, lambda v: (v,), [w2], [BF16])[0].reshape(w.shape)


def _adamw_math(w, g, m, v):
    m = ADAM_B1 * m + (1.0 - ADAM_B1) * g
    v = ADAM_B2 * v + (1.0 - ADAM_B2) * (g * g)
    m_hat = m / (1.0 - ADAM_B1 ** ADAM_STEP)
    v_hat = v / (1.0 - ADAM_B2 ** ADAM_STEP)
    delta = -ADAM_LR * (m_hat / (jnp.sqrt(v_hat) + ADAM_EPS) + ADAM_WD * w)
    return delta, m, v


def _sum_slots(land):
    g = land[0].astype(F32)
    for s in range(1, land.shape[0]):
        g = g + land[s].astype(F32)
    return g


def _adamw_from_slots(name, w, m, v, lands):
    nl, K, C = w.shape
    tr = _tile(K, max(16, 65536 // C), 16)
    per = K // tr
    starts = np.cumsum([0] + [ld.shape[1] * per for ld in lands])
    ng = len(lands)

    def land_spec(g):
        lo, hi = int(starts[g]), int(starts[g + 1])
        return pl.BlockSpec((N_DEV, tr, C), lambda t: (0, jnp.clip(t, lo, hi - 1) - lo, 0))

    def body(*refs):
        w_ref, m_ref, v_ref = refs[:3]
        land_refs, out_refs = refs[3:3 + ng], refs[3 + ng:]
        t = pl.program_id(0)
        for g in range(ng):
            @pl.when((t >= int(starts[g])) & (t < int(starts[g + 1])))
            def _(g=g):
                grad = _sum_slots(land_refs[g][...])
                for r, o in zip(out_refs, (grad,) + _adamw_math(w_ref[...], grad, m_ref[...], v_ref[...])):
                    r[...] = o

    row = pl.BlockSpec((tr, C), lambda t: (t, 0))
    outs = _pallas(
        body, name=name, grid=(nl * per,), in_specs=[row] * 3 + [land_spec(g) for g in range(ng)],
        out_specs=[row] * 4, out_shape=[_sds((nl * K, C), F32)] * 4, compiler_params=_params(("parallel",)),
    )(w.reshape(-1, C), m.reshape(-1, C), v.reshape(-1, C), *[ld.reshape(N_DEV, -1, C) for ld in lands])
    return [o.reshape(w.shape) for o in outs]


def _adamw(name, w, g, m, v):
    outs = _ew(name, lambda w_, g_, m_, v_: _adamw_math(w_, g_, m_, v_), [w, g, m, v], [F32] * 3)
    return outs


def _mm(name, a, b, ta=False, tb=False, out_dtype=F32, add=None):
    M, K = (a.shape[1], a.shape[0]) if ta else a.shape
    N = b.shape[0] if tb else b.shape[1]
    tm = _tile(M, 1408, 128)
    tn = _tile(N, 1024, 128)
    if tn < 256 and N <= 2432:
        tn = N
    tk = _tile(K, 3072, 128)
    nk = K // tk
    a_spec = pl.BlockSpec((tk, tm), lambda i, j, k: (k, i)) if ta else pl.BlockSpec((tm, tk), lambda i, j, k: (i, k))
    b_spec = pl.BlockSpec((tn, tk), lambda i, j, k: (j, k)) if tb else pl.BlockSpec((tk, tn), lambda i, j, k: (k, j))
    o_spec = pl.BlockSpec((tm, tn), lambda i, j, k: (i, j))
    dims = (((0 if ta else 1,), (1 if tb else 0,)), ((), ()))
    has_add = add is not None

    def product(a_ref, b_ref):
        return lax.dot_general(a_ref[...].astype(BF16), b_ref[...].astype(BF16), dims, preferred_element_type=F32)

    def body_single(*refs):
        acc = product(refs[0], refs[1])
        if has_add:
            acc = acc + refs[2][...]
        refs[-1][...] = acc.astype(refs[-1].dtype)

    def body(*refs):
        a_ref, b_ref = refs[0], refs[1]
        add_ref = refs[2] if has_add else None
        o_ref, acc = refs[-2], refs[-1]
        k = pl.program_id(2)

        @pl.when(k == 0)
        def _():
            acc[...] = add_ref[...] if has_add else jnp.zeros_like(acc)

        acc[...] += product(a_ref, b_ref)

        @pl.when(k == nk - 1)
        def _():
            o_ref[...] = acc[...].astype(o_ref.dtype)

    ins = [a, b] + ([add] if has_add else [])
    specs = [a_spec, b_spec] + ([o_spec] if has_add else [])
    return _pallas(
        body_single if nk == 1 else body, name=name, grid=(M // tm, N // tn, nk), in_specs=specs, out_specs=o_spec,
        out_shape=_sds((M, N), out_dtype), scratch_shapes=[] if nk == 1 else [pltpu.VMEM((tm, tn), F32)],
        compiler_params=_params(("parallel", "parallel", "arbitrary")),
    )(*ins)


def _mm_pair(name, a1, a2, b, out_dtype):
    M, Kh = a1.shape
    N = b.shape[1]
    tm = _tile(M, 1408, 128)

    def body(a1_ref, a2_ref, b1_ref, b2_ref, o_ref):
        o_ref[...] = (_dot(a1_ref[...].astype(BF16), b1_ref[...].astype(BF16))
                      + _dot(a2_ref[...].astype(BF16), b2_ref[...].astype(BF16))).astype(o_ref.dtype)

    a_spec = pl.BlockSpec((tm, Kh), lambda i: (i, 0))
    return _pallas(
        body, name=name, grid=(M // tm,),
        in_specs=[a_spec, a_spec, pl.BlockSpec((Kh, N), lambda i: (0, 0)), pl.BlockSpec((Kh, N), lambda i: (1, 0))],
        out_specs=pl.BlockSpec((tm, N), lambda i: (i, 0)), out_shape=_sds((M, N), out_dtype),
        compiler_params=_params(("parallel",)),
    )(a1, a2, b, b)


def _ffn_tile(F):
    return F // 2 if (F // 2) % 128 == 0 else F


def _ffn_interleave(w):
    D, F2 = w.shape
    T = _ffn_tile(F2 // 2)
    nj = F2 // (2 * T)

    def body(a_ref, b_ref, o_ref):
        o_ref[:, :T] = a_ref[...]
        o_ref[:, T:] = b_ref[...]

    return _pallas(
        body, name="ffn_interleave", grid=(nj,),
        in_specs=[pl.BlockSpec((D, T), lambda j: (0, j)), pl.BlockSpec((D, T), lambda j: (0, j + nj))],
        out_specs=pl.BlockSpec((D, 2 * T), lambda j: (0, j)), out_shape=_sds((D, F2), w.dtype),
        compiler_params=_params(("parallel",)),
    )(w, w)


def _ffn_deinterleave(w):
    D, F2 = w.shape
    T = _ffn_tile(F2 // 2)
    nj = F2 // (2 * T)

    def body(w_ref, o_ref):
        o_ref[0] = w_ref[:, :T]
        o_ref[1] = w_ref[:, T:]

    return _pallas(
        body, name="ffn_deinterleave", grid=(nj,), in_specs=[pl.BlockSpec((D, 2 * T), lambda j: (0, j))],
        out_specs=pl.BlockSpec((2, D, T), lambda j: (0, 0, j)), out_shape=_sds((2, D, F2 // 2), w.dtype),
        compiler_params=_params(("parallel",)),
    )(w)


def _ffn_specs(M, D, F):
    T = _ffn_tile(F)
    tm = _tile(M, 512, 128)
    x_spec = pl.BlockSpec((tm, D), lambda i, j: (i, 0))
    wi_spec = pl.BlockSpec((D, 2 * T), lambda i, j: (0, j))
    wo_spec = pl.BlockSpec((T, D), lambda i, j: (j, 0))
    return T, tm, F // T, x_spec, wi_spec, wo_spec


def _ffn_fwd(name, xn, w_in, w_out):
    M, D = xn.shape
    T, tm, nj, x_spec, wi_spec, wo_spec = _ffn_specs(M, D, w_out.shape[0])

    def body(x_ref, wi_ref, wo_ref, y_ref, acc):
        j = pl.program_id(1)
        z = _dot(x_ref[...], wi_ref[...])
        hid = (jax.nn.silu(z[:, :T]) * z[:, T:]).astype(BF16)
        part = _dot(hid, wo_ref[...])

        @pl.when(j == 0)
        def _():
            acc[...] = part

        @pl.when(j > 0)
        def _():
            acc[...] += part

        @pl.when(j == nj - 1)
        def _():
            y_ref[...] = acc[...].astype(y_ref.dtype)

    return _pallas(
        body, name=name, grid=(M // tm, nj), in_specs=[x_spec, wi_spec, wo_spec], out_specs=x_spec,
        out_shape=_sds((M, D), BF16), scratch_shapes=[pltpu.VMEM((tm, D), F32)],
        compiler_params=_params(("parallel", "arbitrary")),
    )(xn, w_in, w_out)


def _ffn_mid_bwd(name, xn, dy, w_in, w_out):
    M, D = xn.shape
    F = w_out.shape[0]
    T, tm, nj, x_spec, wi_spec, wo_spec = _ffn_specs(M, D, F)

    def body(x_ref, dy_ref, wi_ref, wo_ref, h_ref, dz_ref):
        z = _dot(x_ref[...], wi_ref[...])
        a, b = z[:, :T], z[:, T:]
        dh = _dot_nt(dy_ref[...], wo_ref[...])
        sig = jax.nn.sigmoid(a)
        act = a * sig
        h_ref[...] = (act * b).astype(BF16)
        dz_ref[:, :T] = (dh * b * (sig * (1.0 + a * (1.0 - sig)))).astype(BF16)
        dz_ref[:, T:] = (dh * act).astype(BF16)

    return _pallas(
        body, name=name, grid=(M // tm, nj), in_specs=[x_spec, x_spec, wi_spec, wo_spec],
        out_specs=[pl.BlockSpec((tm, T), lambda i, j: (i, j)),
                   pl.BlockSpec((tm, 2 * T), lambda i, j: (i, j))],
        out_shape=[_sds((M, F), BF16), _sds((M, 2 * F), BF16)],
        compiler_params=_params(("parallel", "parallel")),
    )(xn, dy, w_in, w_out)


class _Rows:
    def __init__(self, name, f, B, L, n, tm, arrays, pieces, samp, samp_pieces, glob, out_arrays, out_pieces):
        self.name, self.f, self.B, self.L, self.n, self.tm = name, f, B, L, n, tm
        self.arrays, self.pieces, self.samp, self.samp_pieces, self.glob = arrays, pieces, samp, samp_pieces, glob
        self.out_arrays, self.out_pieces = out_arrays, out_pieces
        self.nx = n // tm

    def _samples_per_step(self, row_bytes):
        for bb in (4, 2, 1):
            if self.B % bb == 0 and 2 * bb * self.tm * row_bytes <= ROW_BLOCK_BUDGET:
                return bb
        return 1

    def _row_spec(self, bb, C, batched=True):
        tm = self.tm
        if batched:
            return pl.BlockSpec((bb, tm, C), lambda g, i: (g, i, 0))
        return pl.BlockSpec((tm, C), lambda g, i: (i, 0))

    def _in_specs(self, bb, unread=()):
        nx, tm = self.nx, self.tm
        specs = [self._row_spec(bb, a.shape[-1], a.ndim == 3) for a in self.arrays]
        for ai in unread:
            specs[ai] = pl.BlockSpec((bb, tm, self.arrays[ai].shape[-1]), lambda g, i: (0, 0, 0))
        specs += [pl.BlockSpec((bb, None) + s.shape[2:], lambda g, i: (g, i // nx, 0, 0)) for s in self.samp]
        specs += [pl.BlockSpec(g.shape, lambda g_, i, nd=g.ndim: (0,) * nd) for g in self.glob]
        return specs

    def _load(self, b, a_refs, s_refs, g_refs):
        args = [(a_refs[ai][b, :, cs:cs + cw] if self.arrays[ai].ndim == 3 else a_refs[ai][:, cs:cs + cw]).astype(F32)
                for ai, cs, cw in self.pieces]
        args += [s_refs[si][b, r:r + 1, :].astype(F32) for si, r in self.samp_pieces]
        args += [g[...].astype(F32) for g in g_refs]
        return args

    @staticmethod
    def _row_bytes(arrays):
        return sum(a.shape[-1] * a.dtype.itemsize for a in arrays if a.ndim == 3)

    def fwd(self):
        na, ns, ng = len(self.arrays), len(self.samp), len(self.glob)
        bb = self._samples_per_step(self._row_bytes(self.arrays)
                                    + sum(C * jnp.dtype(d).itemsize for C, d in self.out_arrays))

        def body(*refs):
            a_refs, s_refs, g_refs = refs[:na], refs[na:na + ns], refs[na + ns:na + ns + ng]
            o_refs = refs[na + ns + ng:]
            for b in range(bb):
                outs = self.f(*self._load(b, a_refs, s_refs, g_refs))
                for (oi, cs, cw), o in zip(self.out_pieces, outs):
                    o_refs[oi][b, :, cs:cs + cw] = o.astype(o_refs[oi].dtype)

        return _pallas(
            body, name=self.name + "_fwd", grid=(self.B // bb, self.L // self.tm), in_specs=self._in_specs(bb),
            out_specs=[self._row_spec(bb, C) for C, _ in self.out_arrays],
            out_shape=[_sds((self.B, self.L, C), d) for C, d in self.out_arrays],
            compiler_params=_params(("parallel", "parallel")),
        )(*self.arrays, *self.samp, *self.glob)

    def bwd(self, cts, grad_arrays, grad_glob=(), add=None, samp=None, unread=()):
        samp = self.samp if samp is None else samp
        na, ns, ng, nc = len(self.arrays), len(self.samp), len(self.glob), len(cts)
        ct_off = [c[1] if isinstance(c, tuple) else 0 for c in cts]
        cts = [c[0] if isinstance(c, tuple) else c for c in cts]
        add = add or {}
        add_keys = list(add)
        g_idx = list(grad_arrays)
        nx, B = self.nx, self.B
        n_in = na + ns + ng + nc + len(add_keys)
        n_pieces, n_sp = len(self.pieces), len(self.samp_pieces)
        streamed = list(self.arrays) + cts + [add[k] for k in add_keys]
        bb = self._samples_per_step(self._row_bytes(streamed) + sum(
            self.arrays[ai].shape[-1] * jnp.dtype(grad_arrays[ai]).itemsize for ai in g_idx))

        def body(*refs):
            a_refs, s_refs, g_refs = refs[:na], refs[na:na + ns], refs[na + ns:na + ns + ng]
            c_refs = refs[na + ns + ng:na + ns + ng + nc]
            add_refs = refs[na + ns + ng + nc:n_in]
            d_refs = refs[n_in:n_in + len(g_idx)]
            ds_refs = refs[n_in + len(g_idx):n_in + len(g_idx) + n_sp]
            dg_refs = refs[n_in + len(g_idx) + n_sp:]
            g_step, i = pl.program_id(0), pl.program_id(1)

            @pl.when((i == 0) | (i == nx))
            def _():
                for r in ds_refs:
                    r[...] = jnp.zeros_like(r)

            @pl.when((g_step == 0) & (i == 0))
            def _():
                for r in dg_refs:
                    r[...] = jnp.zeros_like(r)

            for b in range(bb):
                args = self._load(b, a_refs, s_refs, g_refs)
                _, vjp = jax.vjp(lambda *xs: tuple(self.f(*xs)), *args)
                grads = vjp(tuple(c_refs[oi][b, :, ct_off[oi] + cs:ct_off[oi] + cs + cw].astype(F32)
                                  for oi, cs, cw in self.out_pieces))
                for k, ai in enumerate(g_idx):
                    covered = sum(cw for pa, _, cw in self.pieces if pa == ai)
                    if covered < self.arrays[ai].shape[-1]:
                        d_refs[k][b] = jnp.zeros(d_refs[k].shape[1:], d_refs[k].dtype)
                    for (pa, cs, cw), gr in zip(self.pieces, grads[:n_pieces]):
                        if pa == ai:
                            if ai in add:
                                gr = gr + add_refs[add_keys.index(ai)][b, :, cs:cs + cw]
                            d_refs[k][b, :, cs:cs + cw] = gr.astype(d_refs[k].dtype)
                for r, gr in zip(ds_refs, grads[n_pieces:n_pieces + n_sp]):
                    r[b] += gr
                for r, gi in zip(dg_refs, grad_glob):
                    r[...] += grads[n_pieces + n_sp + gi]

        in_specs = self._in_specs(bb, unread) + [self._row_spec(bb, c.shape[-1]) for c in cts]
        in_specs += [self._row_spec(bb, add[k].shape[-1]) for k in add_keys]
        out_specs = [self._row_spec(bb, self.arrays[ai].shape[-1]) for ai in g_idx]
        out_shape = [_sds(self.arrays[ai].shape, grad_arrays[ai]) for ai in g_idx]
        for si, _ in self.samp_pieces:
            C = self.samp[si].shape[-1]
            out_specs.append(pl.BlockSpec((bb, None, 1, C), lambda g, i: (g, i // nx, 0, 0)))
            out_shape.append(_sds((B, 2, 1, C), F32))
        for gi in grad_glob:
            g = self.glob[gi]
            out_specs.append(pl.BlockSpec(g.shape, lambda g_, i, nd=g.ndim: (0,) * nd))
            out_shape.append(_sds(g.shape, F32))
        outs = _pallas(
            body, name=self.name + "_bwd", grid=(B // bb, self.L // self.tm), in_specs=in_specs, out_specs=out_specs,
            out_shape=out_shape, compiler_params=_params(("arbitrary", "arbitrary")),
        )(*self.arrays, *samp, *self.glob, *cts, *[add[k] for k in add_keys])
        ng_ = len(g_idx)
        return outs[:ng_], outs[ng_:ng_ + n_sp], outs[ng_ + n_sp:]


def _rms(x, g):
    return x * lax.rsqrt(jnp.mean(x * x, axis=-1, keepdims=True) + EPS) * g


@jax.custom_vjp
def _swap_halves(x):
    w = x.shape[-1]
    lane = lax.broadcasted_iota(jnp.int32, x.shape, x.ndim - 1)
    up = pltpu.roll(x, w - ROPE_DIM // 2, x.ndim - 1)
    down = pltpu.roll(x, ROPE_DIM // 2, x.ndim - 1)
    return jnp.where(lane % ROPE_DIM < ROPE_DIM // 2, up, down)


_swap_halves.defvjp(lambda x: (_swap_halves(x), None), lambda _, ct: (_swap_halves(ct),))


@jax.custom_vjp
def _bdot(a, b):
    return jnp.dot(a.astype(BF16), b.astype(BF16), preferred_element_type=F32)


_bdot.defvjp(lambda a, b: (_bdot(a, b), (a, b)),
             lambda res, ct: (_dot_nt(ct.astype(BF16), res[1].astype(BF16)), _dot_tn(res[0].astype(BF16), ct.astype(BF16))))


def _rope(x, cos2, sin2):
    return x * cos2 + _swap_halves(x) * sin2


def _f_pre(s, shift, scale, g):
    return (_rms(s, g) * (1.0 + scale) + shift,)


def _f_res(s, y, gate):
    return (s + gate * y,)


def _f_res_pre(s, y, gate, shift, scale, g):
    s1 = s + gate * y
    return s1, _rms(s1, g) * (1.0 + scale) + shift


def _f_ab_pre(*args):
    kv_lat, q_lat, kpe = args[0:3]
    us, vs = args[3:7], args[7:11]
    cos2, sin2 = args[11:13]
    kv_norm, q_norm = args[13:15]
    vns, wss, bss = args[15:19], args[19:23], args[23:27]
    outs = [_rms(kv_lat, kv_norm), _rms(q_lat, q_norm), _rope(kpe, cos2, sin2)]
    for u, v, vn, ws, bs in zip(us, vs, vns, wss, bss):
        vg = _rms(jax.nn.gelu(v), vn)
        mixed = _bdot(ws, vg) + bs
        outs.append(jax.nn.gelu(u) * mixed)
    return tuple(outs)


def _f_q_rope(qn, qr, cos2, sin2):
    return qn, _rope(qr, cos2, sin2)


def _f_cd_pre(rk, rv, sk, sv, rq, rg, sq, c256, s256, c128, s128, c512, s512):
    return (_rope(rq, c256, s256), _rope(rk * RET_K_SCALE, c256, s256), rv,
            _rope(sq, c512, s512), _rope(sk, c128, s128), sv, rg)


def _f_cd_merge(*args):
    ys, rgs, rns = args[0:4], args[4:8], args[8:12]
    return tuple(_rms(y, rn) * jax.nn.silu(rg) for y, rg, rn in zip(ys, rgs, rns))


def _dot_nt(a, b):
    return lax.dot_general(a, b, (((1,), (1,)), ((), ())), preferred_element_type=F32)


def _dot_tn(a, b):
    return lax.dot_general(a, b, (((0,), (0,)), ((), ())), preferred_element_type=F32)


def _dot(a, b):
    return jnp.dot(a, b, preferred_element_type=F32)


def _tile_spec(tq, C):
    return pl.BlockSpec((None, tq, C), lambda b, i: (b, i, 0))


def _full_spec(L, C):
    return pl.BlockSpec((None, L, C), lambda b, i: (b, 0, 0))


def _mla_half(h):
    lane = lax.broadcasted_iota(jnp.int32, (1, 128), 1)
    return (lane < 64) if h % 2 == 0 else (lane >= 64)


def _mla_query(q_ref, h):
    pair = q_ref[:, 512 + 128 * (h // 2):640 + 128 * (h // 2)]
    return jnp.concatenate([q_ref[:, 128 * h:128 * h + 128], jnp.where(_mla_half(h), pair, jnp.zeros_like(pair))], axis=1)


def _mla_fill_keys(kcat, kv_ref, kpe_ref):
    kp = kpe_ref[...]
    for h in range(4):
        kcat[h, :, 0:128] = kv_ref[:, 256 * h:256 * h + 128]
        kcat[h, :, 128:256] = jnp.where(_mla_half(h), kp, jnp.zeros_like(kp))


def _mla_fwd(q, kv, kpe, n, tq):
    B, L, _ = q.shape

    def body(q_ref, kv_ref, kpe_ref, o_ref, lse_ref, kcat):
        i = pl.program_id(1)

        @pl.when(i == 0)
        def _():
            _mla_fill_keys(kcat, kv_ref, kpe_ref)

        def tile(keys):
            for h in range(4):
                s = _dot_nt(_mla_query(q_ref, h), kcat[h, keys, :]) * MLA_SCALE
                m = jnp.max(s, axis=1, keepdims=True)
                e = jnp.exp(s - m)
                l = jnp.sum(e, axis=1, keepdims=True)
                p = (e * (1.0 / l)).astype(BF16)
                o_ref[:, 128 * h:128 * h + 128] = _dot(p, kv_ref[keys, 256 * h + 128:256 * h + 256]).astype(o_ref.dtype)
                lse_ref[:, h:h + 1] = m + jnp.log(l)

        pl.when(i < n // tq)(functools.partial(tile, slice(0, L)))
        pl.when(i >= n // tq)(functools.partial(tile, slice(n, L)))

    return _pallas(
        body, name="mla_fwd", grid=(B, L // tq),
        in_specs=[_tile_spec(tq, 768), _full_spec(L, 1024), _full_spec(L, 128)],
        out_specs=[_tile_spec(tq, 512), _tile_spec(tq, 4)],
        out_shape=[_sds((B, L, 512), BF16), _sds((B, L, 4), F32)],
        scratch_shapes=[pltpu.VMEM((4, L, 256), BF16)],
        compiler_params=_params(("parallel", "arbitrary")),
    )(q, kv, kpe)


def _mla_bwd(q, kv, kpe, lse, do, n, tq):
    B, L, _ = q.shape

    def body(q_ref, kv_ref, kpe_ref, lse_ref, do_ref, dq_ref, dkv_ref, dkpe_ref, kcat):
        i = pl.program_id(1)

        @pl.when(i == 0)
        def _():
            dkv_ref[...] = jnp.zeros_like(dkv_ref)
            dkpe_ref[...] = jnp.zeros_like(dkpe_ref)
            _mla_fill_keys(kcat, kv_ref, kpe_ref)

        def tile(keys):
            rope_pair = None
            for h in range(4):
                qc, kc = _mla_query(q_ref, h), kcat[h, keys, :]
                v = kv_ref[keys, 256 * h + 128:256 * h + 256]
                p = jnp.exp(_dot_nt(qc, kc) * MLA_SCALE - lse_ref[:, h:h + 1])
                doh = do_ref[:, 128 * h:128 * h + 128].astype(BF16)
                dp = _dot_nt(doh, v)
                delta = jnp.sum(p * dp, axis=1, keepdims=True)
                ds = (p * (dp - delta) * MLA_SCALE).astype(BF16)
                dqc = _dot(ds, kc)
                dq_ref[:, 128 * h:128 * h + 128] = dqc[:, 0:128]
                if h % 2 == 0:
                    rope_pair = dqc[:, 128:256]
                else:
                    dq_ref[:, 512 + 128 * (h // 2):640 + 128 * (h // 2)] = jnp.where(_mla_half(h), dqc[:, 128:256], rope_pair)
                dkc = _dot_tn(ds, qc)
                dkv_ref[keys, 256 * h:256 * h + 128] += dkc[:, 0:128]
                dkpe_ref[keys, :] += dkc[:, 128:256]
                dkv_ref[keys, 256 * h + 128:256 * h + 256] += _dot_tn(p.astype(BF16), doh)

        pl.when(i < n // tq)(functools.partial(tile, slice(0, L)))
        pl.when(i >= n // tq)(functools.partial(tile, slice(n, L)))

    return _pallas(
        body, name="mla_bwd", grid=(B, L // tq),
        in_specs=[_tile_spec(tq, 768), _full_spec(L, 1024), _full_spec(L, 128), _tile_spec(tq, 4),
                  pl.BlockSpec((None, tq, 512), lambda b, i: (b, i, 0))],
        out_specs=[_tile_spec(tq, 768), _full_spec(L, 1024), _full_spec(L, 128)],
        out_shape=[_sds((B, L, 768), F32), _sds((B, L, 1024), F32), _sds((B, L, 128), F32)],
        scratch_shapes=[pltpu.VMEM((4, L, 256), BF16)],
        compiler_params=_params(("parallel", "arbitrary")),
    )(q, kv, kpe, lse, do)


def _swa_window(i, tq, n):
    W = min(tq + 2 * SWA_WINDOW, n)
    lo = jnp.clip(i * tq - SWA_WINDOW, 0, n - W)
    return pl.multiple_of(lo, 128), W


def _swa_mask(i, lo, tq, W, n):
    qpos = i * tq + lax.broadcasted_iota(jnp.int32, (tq, 1), 0)
    kpos = lo + lax.broadcasted_iota(jnp.int32, (1, W), 1)
    return (jnp.abs(qpos - kpos) <= SWA_WINDOW) & (qpos < n)


def _swa_fill(rep, src_ref):
    for g in range(2):
        rep[:, 256 * g:256 * g + 256] = jnp.concatenate([src_ref[:, 64 * g:64 * g + 64]] * 4, axis=1)


def _swa_head_rows(a):
    lane = lax.broadcasted_iota(jnp.int32, (1, 256), 1)
    return jnp.concatenate([jnp.where(lane // 64 == h, a, jnp.zeros_like(a)) for h in range(4)], axis=0)


def _swa_fold_rows(a, tq):
    lane = lax.broadcasted_iota(jnp.int32, (1, 256), 1)
    out = jnp.where(lane // 64 == 0, a[0:tq], 0.0)
    for h in range(1, 4):
        out = jnp.where(lane // 64 == h, a[h * tq:(h + 1) * tq], out)
    return out


def _swa_fwd(q, k, v, sink, n, tq):
    B, L, _ = q.shape

    def body(q_ref, k_ref, v_ref, sink_ref, o_ref, lse_ref, krep, vrep):
        i = pl.program_id(1)

        @pl.when(i == 0)
        def _():
            _swa_fill(krep, k_ref)
            _swa_fill(vrep, v_ref)

        def tile(band):
            for g in range(2):
                gc = slice(256 * g, 256 * g + 256)
                qs = _swa_head_rows(q_ref[:, gc])
                sk = jnp.concatenate([jnp.broadcast_to(sink_ref[0:1, 4 * g + h:4 * g + h + 1], (tq, 1))
                                      for h in range(4)], axis=0)
                s2 = _dot_nt(qs, krep[n:L, gc]) * SWA_SCALE
                m = jnp.maximum(jnp.max(s2, axis=1, keepdims=True), sk)
                if band:
                    lo, W = _swa_window(i, tq, n)
                    mask = jnp.concatenate([_swa_mask(i, lo, tq, W, n)] * 4, axis=0)
                    s1 = jnp.where(mask, _dot_nt(qs, krep[pl.ds(lo, W), gc]) * SWA_SCALE, NEG_INF)
                    m = jnp.maximum(m, jnp.max(s1, axis=1, keepdims=True))
                e2 = jnp.exp(s2 - m)
                l = jnp.sum(e2, axis=1, keepdims=True) + jnp.exp(sk - m)
                if band:
                    e1 = jnp.exp(s1 - m)
                    l = l + jnp.sum(e1, axis=1, keepdims=True)
                r = 1.0 / l
                o = _dot((e2 * r).astype(BF16), vrep[n:L, gc])
                if band:
                    o = o + _dot((e1 * r).astype(BF16), vrep[pl.ds(lo, W), gc])
                o_ref[:, gc] = _swa_fold_rows(o, tq).astype(o_ref.dtype)
                lse = m + jnp.log(l)
                for h in range(4):
                    lse_ref[:, 4 * g + h:4 * g + h + 1] = lse[h * tq:(h + 1) * tq]

        pl.when(i < n // tq)(functools.partial(tile, True))
        pl.when(i >= n // tq)(functools.partial(tile, False))

    return _pallas(
        body, name="swa_fwd", grid=(B, L // tq),
        in_specs=[_tile_spec(tq, 512), _full_spec(L, 128), _full_spec(L, 128), pl.BlockSpec((1, 8), lambda b, i: (0, 0))],
        out_specs=[_tile_spec(tq, 512), _tile_spec(tq, 8)],
        out_shape=[_sds((B, L, 512), BF16), _sds((B, L, 8), F32)],
        scratch_shapes=[pltpu.VMEM((L, 512), BF16), pltpu.VMEM((L, 512), BF16)],
        compiler_params=_params(("parallel", "arbitrary")),
    )(q, k, v, sink)


def _swa_bwd(q, k, v, sink, lse, do, n, tq):
    B, L, _ = q.shape

    def body(q_ref, k_ref, v_ref, sink_ref, lse_ref, do_ref, dq_ref, dk_ref, dv_ref, dsink_ref):
        i = pl.program_id(1)

        @pl.when(i == 0)
        def _():
            dk_ref[...] = jnp.zeros_like(dk_ref)
            dv_ref[...] = jnp.zeros_like(dv_ref)
            dsink_ref[...] = jnp.zeros_like(dsink_ref)

        def tile(band):
            kc, vc = k_ref[n:L, :], v_ref[n:L, :]
            if band:
                lo, W = _swa_window(i, tq, n)
                mask = _swa_mask(i, lo, tq, W, n)
                kl, vl = k_ref[pl.ds(lo, W), :], v_ref[pl.ds(lo, W), :]
            for h in range(8):
                g = h // 4
                cols = slice(64 * g, 64 * g + 64)
                qh = q_ref[:, 64 * h:64 * h + 64]
                lse_h = lse_ref[:, h:h + 1]
                doh = do_ref[:, 64 * h:64 * h + 64].astype(BF16)
                p2 = jnp.exp(_dot_nt(qh, kc[:, cols]) * SWA_SCALE - lse_h)
                ps = jnp.exp(sink_ref[0:1, h:h + 1] - lse_h)
                dp2 = _dot_nt(doh, vc[:, cols])
                delta = jnp.sum(p2 * dp2, axis=1, keepdims=True)
                if band:
                    p1 = jnp.exp(jnp.where(mask, _dot_nt(qh, kl[:, cols]) * SWA_SCALE, NEG_INF) - lse_h)
                    dp1 = _dot_nt(doh, vl[:, cols])
                    delta = delta + jnp.sum(p1 * dp1, axis=1, keepdims=True)
                ds2 = (p2 * (dp2 - delta) * SWA_SCALE).astype(BF16)
                dq = _dot(ds2, kc[:, cols])
                dk_ref[n:L, cols] += _dot_tn(ds2, qh)
                dv_ref[n:L, cols] += _dot_tn(p2.astype(BF16), doh)
                if band:
                    ds1 = (p1 * (dp1 - delta) * SWA_SCALE).astype(BF16)
                    dq = dq + _dot(ds1, kl[:, cols])
                    dk_ref[pl.ds(lo, W), cols] += _dot_tn(ds1, qh)
                    dv_ref[pl.ds(lo, W), cols] += _dot_tn(p1.astype(BF16), doh)
                dq_ref[:, 64 * h:64 * h + 64] = dq
                dsink_ref[0:1, h:h + 1] += jnp.sum(-ps * delta, axis=0, keepdims=True)

        pl.when(i < n // tq)(functools.partial(tile, True))
        pl.when(i >= n // tq)(functools.partial(tile, False))

    return _pallas(
        body, name="swa_bwd", grid=(B, L // tq),
        in_specs=[_tile_spec(tq, 512), _full_spec(L, 128), _full_spec(L, 128), pl.BlockSpec((1, 8), lambda b, i: (0, 0)),
                  _tile_spec(tq, 8), pl.BlockSpec((None, tq, 512), lambda b, i: (b, i, 1))],
        out_specs=[_tile_spec(tq, 512), _full_spec(L, 128), _full_spec(L, 128),
                   pl.BlockSpec((None, 1, 8), lambda b, i: (b, 0, 0))],
        out_shape=[_sds((B, L, 512), F32), _sds((B, L, 128), F32), _sds((B, L, 128), F32), _sds((B, 1, 8), F32)],
        compiler_params=_params(("parallel", "arbitrary")),
    )(q, k, v, sink, lse, do)


def _ret_decay(i, tq, L, n, ctx_tile):
    qi = i * tq + lax.broadcasted_iota(jnp.int32, (tq, 1), 0)
    kc = lax.broadcasted_iota(jnp.int32, (1, L - n), 1)
    d_hb = (n + kc - qi).astype(F32)
    if ctx_tile:
        return None, ((qi - L) - (kc - (L - n))).astype(F32), d_hb
    d_x = (qi - lax.broadcasted_iota(jnp.int32, (1, n), 1)).astype(F32)
    return d_x, (qi - (kc - (L - n))).astype(F32), d_hb


def _ret_head(a, h):
    lane = lax.broadcasted_iota(jnp.int32, (1, a.shape[1]), 1)
    return jnp.where(lane // 64 == h, a, jnp.zeros_like(a))


def _ret_mask(dist, lg):
    return jnp.where(dist >= 0.0, jnp.exp(lg * jnp.maximum(dist, 0.0)), 0.0)


def _ret_weights(n, L, d_x, d_hf, d_hb, lg_f, lg_b, with_grad):
    m_f, m_b = _ret_mask(d_hf, lg_f), _ret_mask(d_hb, lg_b)
    ctx = (slice(n, L), m_f + m_b)
    if with_grad:
        ctx += (m_f * jnp.maximum(d_hf, 0.0), m_b * jnp.maximum(d_hb, 0.0))
    if d_x is None:
        return [ctx]
    e_x = jnp.exp(jnp.where(d_x >= 0.0, lg_f, -lg_b) * d_x)
    lat = (slice(0, n), jnp.where(d_x == 0.0, 2.0 * e_x, e_x))
    if with_grad:
        lat += (e_x * jnp.maximum(d_x, 0.0), e_x * jnp.maximum(-d_x, 0.0))
    return [lat, ctx]


def _ret_fwd(q, k, v, lg, n, tq):
    B, L, _ = q.shape

    def body(q_ref, k_ref, v_ref, lg_ref, y_ref):
        i = pl.program_id(1)

        def tile(ctx_tile):
            dist = _ret_decay(i, tq, L, n, ctx_tile)
            for h in range(4):
                qh = _ret_head(q_ref[...], h)
                y = None
                for rows, dec_r in _ret_weights(n, L, *dist, lg_ref[0:1, h:h + 1], lg_ref[1:2, h:h + 1], False):
                    a = _dot_nt(qh, k_ref[rows, :]) * dec_r
                    part = _dot(a.astype(BF16), v_ref[rows, 128 * h:128 * h + 128])
                    y = part if y is None else y + part
                y_ref[:, 128 * h:128 * h + 128] = y

        pl.when(i < n // tq)(functools.partial(tile, False))
        pl.when(i >= n // tq)(functools.partial(tile, True))

    return _pallas(
        body, name="ret_fwd", grid=(B, L // tq),
        in_specs=[_tile_spec(tq, 256), _full_spec(L, 256), _full_spec(L, 512), pl.BlockSpec((2, 4), lambda b, i: (0, 0))],
        out_specs=_tile_spec(tq, 512), out_shape=_sds((B, L, 512), F32),
        compiler_params=_params(("parallel", "arbitrary")),
    )(q, k, v, lg)


def _ret_bwd(q, k, v, lg, dy, n, tq):
    B, L, _ = q.shape

    def body(q_ref, k_ref, v_ref, lg_ref, dy_ref, dq_ref, dk_ref, dv_ref, dlg_ref):
        i = pl.program_id(1)

        @pl.when(i == 0)
        def _():
            dk_ref[...] = jnp.zeros_like(dk_ref)
            dv_ref[...] = jnp.zeros_like(dv_ref)
            dlg_ref[...] = jnp.zeros_like(dlg_ref)

        def total(a):
            return jnp.sum(jnp.sum(a, axis=1, keepdims=True), axis=0, keepdims=True)

        def tile(ctx_tile):
            dist = _ret_decay(i, tq, L, n, ctx_tile)
            dq = None
            for h in range(4):
                vc = slice(128 * h, 128 * h + 128)
                qh = _ret_head(q_ref[...], h)
                dyh = dy_ref[:, vc].astype(BF16)
                dqh = None
                for rows, dec_r, wf_r, wb_r in _ret_weights(n, L, *dist, lg_ref[0:1, h:h + 1], lg_ref[1:2, h:h + 1], True):
                    k_all, vh = k_ref[rows, :], v_ref[rows, vc]
                    s = _dot_nt(qh, k_all)
                    gr = _dot_nt(dyh, vh)
                    ds = (gr * dec_r).astype(BF16)
                    part = _dot(ds, k_all)
                    dqh = part if dqh is None else dqh + part
                    dk_ref[rows, :] += _dot_tn(ds, qh)
                    dv_ref[rows, vc] += _dot_tn((s * dec_r).astype(BF16), dyh)
                    gs = gr * s
                    dlg_ref[0:1, h:h + 1] += total(gs * wf_r)
                    dlg_ref[1:2, h:h + 1] += total(gs * wb_r)
                dqh = _ret_head(dqh, h)
                dq = dqh if dq is None else dq + dqh
            dq_ref[...] = dq

        pl.when(i < n // tq)(functools.partial(tile, False))
        pl.when(i >= n // tq)(functools.partial(tile, True))

    return _pallas(
        body, name="ret_bwd", grid=(B, L // tq),
        in_specs=[_tile_spec(tq, 256), _full_spec(L, 256), _full_spec(L, 512), pl.BlockSpec((2, 4), lambda b, i: (0, 0)),
                  _tile_spec(tq, 512)],
        out_specs=[_tile_spec(tq, 256), _full_spec(L, 256), _full_spec(L, 512),
                   pl.BlockSpec((None, 2, 4), lambda b, i: (b, 0, 0))],
        out_shape=[_sds((B, L, 256), F32), _sds((B, L, 256), F32), _sds((B, L, 512), F32), _sds((B, 2, 4), F32)],
        compiler_params=_params(("parallel", "arbitrary")),
    )(q, k, v, lg, dy)


def _loss_head(s, target, g, n, tm):
    B, L, D = s.shape
    nx = n // tm

    def body(s_ref, t_ref, g_ref, ds_ref, dg_ref, loss_ref):
        b, i = pl.program_id(0), pl.program_id(1)

        @pl.when((b == 0) & (i == 0))
        def _():
            dg_ref[...] = jnp.zeros_like(dg_ref)
            loss_ref[...] = jnp.zeros_like(loss_ref)

        @pl.when(i < nx)
        def _():
            y, vjp = jax.vjp(_rms, s_ref[...], g_ref[...])
            err = y - t_ref[...]
            d_s, d_g = vjp(err * (1.0 / D))
            ds_ref[...] = d_s
            dg_ref[...] += d_g
            part = jnp.sum(jnp.sum(err * err, axis=1, keepdims=True), axis=0, keepdims=True) * (0.5 / D)
            loss_ref[...] += jnp.broadcast_to(part, loss_ref.shape)

        @pl.when(i >= nx)
        def _():
            ds_ref[...] = jnp.zeros_like(ds_ref)

    return _pallas(
        body, name="loss_head", grid=(B, L // tm),
        in_specs=[pl.BlockSpec((None, tm, D), lambda b, i: (b, i, 0)),
                  pl.BlockSpec((None, tm, D), lambda b, i: (b, jnp.minimum(i, nx - 1), 0)),
                  pl.BlockSpec((1, D), lambda b, i: (0, 0))],
        out_specs=[pl.BlockSpec((None, tm, D), lambda b, i: (b, i, 0)), pl.BlockSpec((1, D), lambda b, i: (0, 0)),
                   pl.BlockSpec((1, 128), lambda b, i: (0, 0))],
        out_shape=[_sds((B, L, D), F32), _sds((1, D), F32), _sds((1, 128), F32)],
        compiler_params=_params(("arbitrary", "arbitrary")),
    )(s, target, g)


def _ada_fwd(c_all, ada_w, ada_b):
    NL, D, Ns = ada_w.shape
    R = c_all.shape[0]

    def body(c_ref, w_ref, b_ref, o_ref):
        cond = jax.nn.silu(c_ref[...]).astype(BF16)
        o_ref[...] = _dot(cond, w_ref[...].astype(BF16)) + b_ref[...]

    return _pallas(
        body, name="ada_fwd", grid=(NL,),
        in_specs=[pl.BlockSpec((R, D), lambda l: (0, 0)), pl.BlockSpec((None, D, Ns), lambda l: (l, 0, 0)),
                  pl.BlockSpec((None, 1, Ns), lambda l: (l, 0, 0))],
        out_specs=pl.BlockSpec((None, R, Ns), lambda l: (l, 0, 0)), out_shape=_sds((NL, R, Ns), F32),
        compiler_params=_params(("parallel",)),
    )(c_all, ada_w, ada_b)


def _ada_bwd(c_all, ada_w, dmods):
    NL, D, Ns = ada_w.shape
    R = c_all.shape[0]

    def body(c_ref, w_ref, dm_ref, dw_ref, dc_ref):
        cond = jax.nn.silu(c_ref[...]).astype(BF16)
        dm = dm_ref[...].astype(BF16)
        dw_ref[...] = _dot_tn(cond, dm)
        dc_ref[...] = _dot_nt(dm, w_ref[...].astype(BF16))

    return _pallas(
        body, name="ada_bwd", grid=(NL,),
        in_specs=[pl.BlockSpec((R, D), lambda l: (0, 0)), pl.BlockSpec((None, D, Ns), lambda l: (l, 0, 0)),
                  pl.BlockSpec((None, R, Ns), lambda l: (l, 0, 0))],
        out_specs=[pl.BlockSpec((None, D, Ns), lambda l: (l, 0, 0)), pl.BlockSpec((None, R, D), lambda l: (l, 0, 0))],
        out_shape=[_sds((NL, D, Ns), F32), _sds((NL, R, D), F32)],
        compiler_params=_params(("parallel",)),
    )(c_all, ada_w, dmods)


def _my_index():
    return 4 * lax.axis_index("x") + 2 * lax.axis_index("y") + lax.axis_index("c")


def _peer(k):
    x, y, c = lax.axis_index("x"), lax.axis_index("y"), lax.axis_index("c")
    kx, ky, kc = (k >> 2) & 1, (k >> 1) & 1, k & 1
    px, py, pc = (x + kx) % 2, (y + ky) % 2, (c + kc) % 2
    return (px, py, pc), 4 * px + 2 * py + pc


def _all_gather(name, shards):
    na = len(shards)
    hbm = pl.BlockSpec(memory_space=pl.ANY)

    def body(*refs):
        in_refs, out_refs = refs[:na], refs[na:2 * na]
        send_sems, recv_sems, local_sems = refs[2 * na:]
        me = _my_index()
        sib_id, sib = _peer(1)
        chips = [_peer(k) for k in (4, 2, 6)]
        sib_chips = [4 * px + 2 * py + (1 - pc) for (px, py, pc), _ in chips]

        def copy(a, k, slot, to, src=None):
            dst = out_refs[a].at[slot]
            return pltpu.make_async_remote_copy(
                src_ref=dst if src is None else src, dst_ref=dst, send_sem=send_sems.at[a, k],
                recv_sem=recv_sems.at[a, k], device_id=to, device_id_type=MESH)

        first, passed, mine = [], [], []
        for a in range(na):
            cp = pltpu.make_async_copy(in_refs[a], out_refs[a].at[me], local_sems.at[a])
            cp.start()
            mine.append(cp)
            first.append(copy(a, 0, me, sib_id, src=in_refs[a]))
            first += [copy(a, 1 + j, me, pid, src=in_refs[a]) for j, (pid, _) in enumerate(chips)]
        for cp in first:
            cp.start()
        for a in range(na):
            for j, (pid, pidx) in enumerate(chips):
                copy(a, 1 + j, pidx, pid).wait_recv()
                fwd = copy(a, 4 + j, pidx, sib_id)
                fwd.start()
                passed.append(fwd)
        for a in range(na):
            copy(a, 0, sib, sib_id).wait_recv()
            for j in range(3):
                copy(a, 4 + j, sib_chips[j], sib_id).wait_recv()
        for cp in first + passed:
            cp.wait_send()
        for cp in mine:
            cp.wait()

    return _pallas(
        body, name=name, in_specs=[hbm] * na, out_specs=[hbm] * na,
        out_shape=[_sds((N_DEV,) + s.shape, s.dtype) for s in shards],
        scratch_shapes=[pltpu.SemaphoreType.DMA((na, 7)), pltpu.SemaphoreType.DMA((na, 7)),
                        pltpu.SemaphoreType.DMA((na,))],
    )(*shards)


_HBM = pl.BlockSpec(memory_space=pltpu.HBM)
_SEM = pl.BlockSpec(memory_space=pltpu.SEMAPHORE)
_DATAFLOW = pltpu.SideEffectType.DATAFLOW_SIDE_EFFECTING


def _exchange_start(name, srcs, slotted, after=()):
    na = len(srcs)
    lands = [lax.empty((N_DEV,) + (s.shape[1:] if slotted else s.shape), s.dtype) for s in srcs]

    def body(*refs):
        src_refs, land_refs = refs[:na], refs[na:2 * na]
        outs = refs[2 * na + len(after):]
        send_sems, recv_sems, token = outs[:na], outs[na:2 * na], outs[4 * na]
        me = _my_index()
        for a in range(na):
            for k in range(1, N_DEV):
                pid, pidx = _peer(k)
                pltpu.make_async_remote_copy(
                    src_ref=src_refs[a].at[pidx] if slotted else src_refs[a], dst_ref=land_refs[a].at[me],
                    send_sem=send_sems[a], recv_sem=recv_sems[a], device_id=pid, device_id_type=MESH).start()
        token[...] = jnp.zeros_like(token)

    ops = [pltpu.with_memory_space_constraint(a, pltpu.HBM) for a in list(srcs) + lands]
    outs = _pallas(
        body, name=name,
        out_shape=[pltpu.SemaphoreType.DMA(())] * (2 * na) + [pltpu.HBM(a.shape, a.dtype) for a in ops]
        + [_sds((8, 128), F32)],
        in_specs=[_HBM] * (2 * na) + [pl.BlockSpec(memory_space=pl.ANY)] * len(after),
        out_specs=[_SEM] * (2 * na) + [_HBM] * (2 * na) + [pl.BlockSpec(memory_space=pltpu.VMEM)],
        input_output_aliases={a: 2 * na + a for a in range(2 * na)},
        compiler_params=pltpu.CompilerParams(has_side_effects=_DATAFLOW),
    )(*ops, *after)
    return (na, outs[:2 * na], outs[2 * na:4 * na]), outs[4 * na]


def _exchange_wait(name, handle, after):
    na, sems, thru = handle

    def body(*refs):
        land_refs = refs[na:2 * na]
        send_sems, recv_sems = refs[2 * na:3 * na], refs[3 * na:4 * na]
        me_id = (lax.axis_index("x"), lax.axis_index("y"), lax.axis_index("c"))
        for a in range(na):
            seven = land_refs[a].at[pl.ds(0, N_DEV - 1)]
            drain = pltpu.make_async_remote_copy(src_ref=seven, dst_ref=seven, send_sem=send_sems[a],
                                                 recv_sem=recv_sems[a], device_id=me_id, device_id_type=MESH)
            drain.wait_send()
            drain.wait_recv()

    outs = _pallas(
        body, name=name, out_shape=[pltpu.HBM(a.shape, a.dtype) for a in thru],
        in_specs=[_HBM] * (2 * na) + [_SEM] * (2 * na) + [pl.BlockSpec(memory_space=pl.ANY)],
        out_specs=[_HBM] * (2 * na), input_output_aliases={a: a for a in range(2 * na)},
        compiler_params=pltpu.CompilerParams(has_side_effects=_DATAFLOW),
    )(*thru, *sems, after)
    return outs[na:]


def _fill_own(name, landed, owns, slotted):
    me = _my_index()
    return [jnp.where(lax.broadcasted_iota(jnp.int32, land.shape, 0) == me, own, land)
            for land, own in zip(landed, owns)]


def _rope_tables(n, L, width):
    t = jnp.arange(n)
    row = (t // GRID_W).astype(F32)
    col = (t % GRID_W).astype(F32)
    n_freq = ROPE_DIM // 4
    freqs = ROPE_THETA ** (-jnp.arange(n_freq, dtype=F32) / n_freq)
    ang = jnp.concatenate([row[:, None] * freqs, col[:, None] * freqs], axis=-1)
    cos, sin = jnp.cos(ang), jnp.sin(ang)
    cos2 = jnp.concatenate([cos, cos], axis=-1)
    sin2 = jnp.concatenate([-sin, sin], axis=-1)
    cos2 = jnp.concatenate([cos2, jnp.ones((L - n, ROPE_DIM), F32)], axis=0)
    sin2 = jnp.concatenate([sin2, jnp.zeros((L - n, ROPE_DIM), F32)], axis=0)
    reps = width // ROPE_DIM
    return jnp.tile(cos2, (1, reps)), jnp.tile(sin2, (1, reps))


def _ab_perm(w):
    return jnp.concatenate([w[:, 0:256], w[:, 320:1600], w[:, 256:320], w[:, 256:320]], axis=1)


def _ab_unperm(g):
    rope_key = (g[:, 1536:1600].astype(F32) + g[:, 1600:1664].astype(F32)).astype(g.dtype)
    return jnp.concatenate([g[:, 0:256], rope_key, g[:, 256:1536]], axis=1)


def _wq_perm(w):
    return jnp.concatenate([w[:, 192 * h:192 * h + 128] for h in range(4)]
                           + [w[:, 192 * h + 128:192 * h + 192] for h in range(4)], axis=1)


def _wq_unperm(g):
    return jnp.concatenate([g[:, sl] for h in range(4)
                            for sl in (slice(128 * h, 128 * h + 128), slice(512 + 64 * h, 576 + 64 * h))], axis=1)


def _flat(a):
    return a.reshape(-1, a.shape[-1])


def _layer_weights(full, p):
    NL, NE, NO = len(full["ffn_in"]), len(full["ab_in"]), len(full["cd_in"])
    groups = range(4)

    def each(f, mats):
        return [None if w is None else f(w) for w in mats]

    return dict(
        norm_mix=[p["norm_mix"][l][None] for l in range(NL)], norm_ffn=[p["norm_ffn"][l][None] for l in range(NL)],
        norm_final=p["norm_final"][None],
        ffn_in=each(_ffn_interleave, full["ffn_in"]), ffn_out=list(full["ffn_out"]),
        ab_in=each(_ab_perm, full["ab_in"]), ab_out=list(full["ab_out"]),
        wq=each(_wq_perm, full["mla_wq_b"]), wkv=list(full["mla_wkv_b"]),
        kv_norm=[p["mla_kv_norm"][j][None] for j in range(NE)], q_norm=[p["mla_q_norm"][j][None] for j in range(NE)],
        v_norm=[[p["cmlp_v_norm"][j][None, 128 * g:128 * g + 128] for g in groups] for j in range(NE)],
        ws=[[p["cmlp_ws"][j, g] for g in groups] for j in range(NE)],
        bs=[[p["cmlp_bs"][j, g][:, None] for g in groups] for j in range(NE)],
        cd_in=list(full["cd_in"]), cd_out=list(full["cd_out"]),
        lg=[jnp.stack([jax.nn.log_sigmoid(p["ret_decay_fwd"][j]), jax.nn.log_sigmoid(p["ret_decay_bwd"][j])])
            for j in range(NO)],
        sink=[p["swa_sink"][j][None] for j in range(NO)],
        ret_norm=[[p["ret_norm"][j][None, 128 * g:128 * g + 128] for g in groups] for j in range(NO)],
    )


def _local_step(x, ctx, target, mods, W, later_weights=None, early_grads=None):
    B, n, D = x.shape
    m = ctx.shape[1]
    L = n + m
    NL = mods.shape[0]
    tm = min(256, m)
    tq = min(256, m)
    cos512, sin512 = _rope_tables(n, L, 512)
    s = jnp.concatenate([x, ctx], axis=1)
    saved = []

    def rows(name, f, arrays, pieces, samp, samp_pieces, glob, out_arrays, out_pieces, tile=tm):
        return _Rows(name, f, B, L, n, tile, arrays, pieces, samp, samp_pieces, glob, out_arrays, out_pieces)

    def full(width, start=0):
        return [(0, start, width)]

    for l in range(NL):
        j = l // 2
        even = l % 2 == 0
        md = mods[l]
        if l == 1 and later_weights is not None:
            W = later_weights(1, s)
        r = {}
        if l == 0:
            r["pre1"] = rows("pre_mix0", _f_pre, [s], full(D), [md], [(0, 0), (0, 1)], [W["norm_mix"][0]], [(D, BF16)], full(D))
            (xn,) = r["pre1"].fwd()
        r["xn"] = xn
        if even:
            z = _mm(f"ab_in{l}", _flat(xn), W["ab_in"][j]).reshape(B, L, 1664)
            pieces = [(0, 0, 256), (0, 256, 256), (0, 1536, 128)]
            pieces += [(0, 512 + 128 * g, 128) for g in range(4)] + [(0, 1024 + 128 * g, 128) for g in range(4)]
            pieces += [(1, 0, 128), (2, 0, 128)]
            glob = [W["kv_norm"][j], W["q_norm"][j]] + W["v_norm"][j] + W["ws"][j] + W["bs"][j]
            abp = rows(f"ab_pre{l}", _f_ab_pre, [z, cos512, sin512], pieces, [], [], glob,
                       [(256, BF16), (256, BF16), (128, BF16), (512, BF16)],
                       [(0, 0, 256), (1, 0, 256), (2, 0, 128)] + [(3, 128 * g, 128) for g in range(4)], tile=128)
            kvn, qn, kpe, cm = abp.fwd()
            kv = _mm(f"wkv{l}", _flat(kvn), W["wkv"][j], out_dtype=BF16).reshape(B, L, 1024)
            q0 = _mm(f"wq{l}", _flat(qn), W["wq"][j]).reshape(B, L, 768)
            qrp = rows(f"q_rope{l}", _f_q_rope, [q0, cos512, sin512], [(0, 0, 512), (0, 512, 256), (1, 0, 256), (2, 0, 256)],
                       [], [], [], [(768, BF16)], [(0, 0, 512), (0, 512, 256)])
            (q,) = qrp.fwd()
            o, lse = _mla_fwd(q, kv, kpe, n, tq)
            y = _mm_pair(f"ab_out{l}", _flat(o), _flat(cm), W["ab_out"][j], BF16).reshape(B, L, D)
            r.update(z=z, abp=abp, kvn=kvn, qn=qn, kpe=kpe, cm=cm, kv=kv, qrp=qrp, q=q, o=o, lse=lse)
        else:
            z = _mm(f"cd_in{l}", _flat(xn), W["cd_in"][j]).reshape(B, L, 2304)
            pieces = [(0, 0, 256), (0, 256, 512), (0, 768, 128), (0, 896, 128), (0, 1024, 256), (0, 1280, 512), (0, 1792, 512)]
            pieces += [(1, 0, 256), (2, 0, 256), (1, 0, 128), (2, 0, 128), (1, 0, 512), (2, 0, 512)]
            cdp = rows(f"cd_pre{l}", _f_cd_pre, [z, cos512, sin512], pieces, [], [], [],
                       [(256, BF16), (256, BF16), (512, BF16), (512, BF16), (128, BF16), (128, BF16), (512, F32)],
                       [(k_, 0, w_) for k_, w_ in enumerate((256, 256, 512, 512, 128, 128, 512))])
            rq, rk, rv, sq, sk, sv, rg = cdp.fwd()
            yret = _ret_fwd(rq, rk, rv, W["lg"][j], n, tq)
            osw, lse = _swa_fwd(sq, sk, sv, W["sink"][j], n, tq)
            mrg = rows(f"cd_merge{l}", _f_cd_merge, [yret, rg],
                       [(0, 128 * g, 128) for g in range(4)] + [(1, 128 * g, 128) for g in range(4)], [], [],
                       W["ret_norm"][j], [(512, BF16)], [(0, 128 * g, 128) for g in range(4)])
            (yr,) = mrg.fwd()
            y = _mm_pair(f"cd_out{l}", _flat(yr), _flat(osw), W["cd_out"][j], BF16).reshape(B, L, D)
            r.update(z=z, cdp=cdp, rq=rq, rk=rk, rv=rv, sq=sq, sk=sk, sv=sv, rg=rg, yret=yret, osw=osw, lse=lse,
                     mrg=mrg, yr=yr)
        two, outs2 = [(0, 0, D), (1, 0, D)], [(D, F32), (D, BF16)]
        r["mix_out"] = rows(f"mix_res_pre{l}", _f_res_pre, [s, y], two, [md], [(0, 2), (0, 3), (0, 4)],
                            [W["norm_ffn"][l]], outs2, two)
        s1, xn2 = r["mix_out"].fwd()
        if l == 0 and later_weights is not None:
            W = later_weights(0, s1)
        y2 = _ffn_fwd(f"ffn{l}", _flat(xn2), W["ffn_in"][l], W["ffn_out"][l]).reshape(B, L, D)
        if l < NL - 1:
            r["ffn_out"] = rows(f"ffn_res_pre{l}", _f_res_pre, [s1, y2], two, [md, mods[l + 1]], [(0, 5), (1, 0), (1, 1)],
                                [W["norm_mix"][l + 1]], outs2, two)
            s, xn = r["ffn_out"].fwd()
        else:
            r["ffn_out"] = rows(f"res_ffn{l}", _f_res, [s1, y2], two, [md], [(0, 5)], [], [(D, F32)], full(D))
            (s,) = r["ffn_out"].fwd()
        r["xn2"] = xn2
        saved.append(r)

    ds, d_norm_final, loss = _loss_head(s, target, W["norm_final"], n, tm)

    G = {k: [None] * len(v) for k, v in W.items() if isinstance(v, list)}
    G["norm_final"] = d_norm_final
    dm = [[None] * 6 for _ in range(NL)]
    dxn_next = None
    for l in reversed(range(NL)):
        j = l // 2
        even = l % 2 == 0
        r = saved[l]
        zero = early_grads(0, G) if l == 0 and early_grads is not None else None
        if l == NL - 1:
            (ds1, dy2), (dm[l][5],), _ = r["ffn_out"].bwd([ds], {0: F32, 1: BF16}, unread=(0,))
        else:
            smp = None if zero is None else [mods[l] + zero, mods[l + 1] + zero]
            (ds1, dy2), (dm[l][5], dm[l + 1][0], dm[l + 1][1]), (G["norm_mix"][l + 1],) = r["ffn_out"].bwd(
                [ds, dxn_next], {0: F32, 1: BF16}, grad_glob=(0,), samp=smp)
        dy2f, xn2f = _flat(dy2), _flat(r["xn2"])
        hid, dz2 = _ffn_mid_bwd(f"ffn_mid{l}", xn2f, dy2f, W["ffn_in"][l], W["ffn_out"][l])
        G["ffn_out"][l] = _mm(f"g_ffn_out{l}", hid, dy2f, ta=True, out_dtype=BF16)
        G["ffn_in"][l] = _mm(f"g_ffn_in{l}", xn2f, dz2, ta=True, out_dtype=BF16)
        dxn2 = _mm(f"d_xn2{l}", dz2, W["ffn_in"][l], tb=True, out_dtype=BF16).reshape(B, L, D)
        zero = early_grads(1, G) if l == 0 and early_grads is not None else None
        smp = None if zero is None else [mods[l] + zero]
        (ds0, dy), (dm[l][2], dm[l][3], dm[l][4]), (G["norm_ffn"][l],) = r["mix_out"].bwd(
            [ds1, dxn2], {0: F32, 1: BF16}, grad_glob=(0,), samp=smp)
        dyf = _flat(dy)
        if even:
            w_out = W["ab_out"][j]
            dcat = _mm(f"d_cat{l}", dyf, w_out, tb=True, out_dtype=BF16).reshape(B, L, -1)
            G["ab_out"][l // 2] = jnp.concatenate(
                [_mm(f"g_ab_out_a{l}", _flat(r["o"]), dyf, ta=True, out_dtype=BF16),
                 _mm(f"g_ab_out_b{l}", _flat(r["cm"]), dyf, ta=True, out_dtype=BF16)], axis=0)
            dq, dkv, dkpe = _mla_bwd(r["q"], r["kv"], r["kpe"], r["lse"], dcat, n, tq)
            (dq0,), _, _ = r["qrp"].bwd([dq], {0: BF16}, unread=(0,))
            dq0f, dkvf = _flat(dq0), _flat(dkv)
            G["wq"][j] = _mm(f"g_wq{l}", _flat(r["qn"]), dq0f, ta=True, out_dtype=BF16)
            G["wkv"][j] = _mm(f"g_wkv{l}", _flat(r["kvn"]), dkvf, ta=True, out_dtype=BF16)
            dqn = _mm(f"d_qn{l}", dq0f, W["wq"][j], tb=True).reshape(B, L, 256)
            dkvn = _mm(f"d_kvn{l}", dkvf, W["wkv"][j], tb=True).reshape(B, L, 256)
            (dz,), _, gg = r["abp"].bwd([dkvn, dqn, dkpe, (dcat, 512)], {0: BF16}, grad_glob=tuple(range(14)))
            G["kv_norm"][j], G["q_norm"][j] = gg[0], gg[1]
            G["v_norm"][j], G["ws"][j], G["bs"][j] = list(gg[2:6]), list(gg[6:10]), list(gg[10:14])
            w_in, key = W["ab_in"][j], "ab_in"
        else:
            w_out = W["cd_out"][j]
            dcat = _mm(f"d_cat{l}", dyf, w_out, tb=True, out_dtype=BF16).reshape(B, L, -1)
            G["cd_out"][j] = jnp.concatenate(
                [_mm(f"g_cd_out_a{l}", _flat(r["yr"]), dyf, ta=True, out_dtype=BF16),
                 _mm(f"g_cd_out_b{l}", _flat(r["osw"]), dyf, ta=True, out_dtype=BF16)], axis=0)
            (dyret, drg), _, gg = r["mrg"].bwd([(dcat, 0)], {0: F32, 1: F32}, grad_glob=(0, 1, 2, 3))
            G["ret_norm"][j] = list(gg)
            drq, drk, drv, dlg = _ret_bwd(r["rq"], r["rk"], r["rv"], W["lg"][j], dyret, n, tq)
            dsq, dsk, dsv, dsink = _swa_bwd(r["sq"], r["sk"], r["sv"], W["sink"][j], r["lse"], dcat, n, tq)
            G["lg"][j], G["sink"][j] = dlg, dsink
            (dz,), _, _ = r["cdp"].bwd([drq, drk, drv, dsq, dsk, dsv, drg], {0: BF16}, unread=(0,))
            w_in, key = W["cd_in"][j], "cd_in"
        dzf = _flat(dz)
        G[key][j] = _mm(f"g_{key}{l}", _flat(r["xn"]), dzf, ta=True, out_dtype=BF16)
        dxn = _mm(f"d_xn{l}", dzf, w_in, tb=True, out_dtype=BF16).reshape(B, L, D)
        if l == 0 and early_grads is not None:
            smp = [mods[0] + early_grads(2, G)]
        if l == 0:
            (ds,), (dm[0][0], dm[0][1]), (G["norm_mix"][0],) = r["pre1"].bwd([dxn], {0: F32}, grad_glob=(0,),
                                                                            add={0: ds0}, samp=smp)
        else:
            ds, dxn_next = ds0, dxn
    dmods = jnp.stack([jnp.concatenate(d, axis=2) for d in dm])
    return loss, ds[:, :n], dmods, G


def kernel(x, c, ctx, c_ctx, ada_w, ada_b, norm_mix, norm_ffn, norm_final, ffn_in, ffn_out, ab_in, ab_out, mla_q_norm, mla_kv_norm, mla_wq_b, mla_wkv_b, cmlp_v_norm, cmlp_ws, cmlp_bs, cd_in, cd_out, ret_decay_fwd, ret_decay_bwd, ret_norm, swa_sink, loss_target, m_c_ctx, m_ada_w, m_ada_b, m_norm_mix, m_norm_ffn, m_norm_final, m_ffn_in, m_ffn_out, m_ab_in, m_ab_out, m_mla_q_norm, m_mla_kv_norm, m_mla_wq_b, m_mla_wkv_b, m_cmlp_v_norm, m_cmlp_ws, m_cmlp_bs, m_cd_in, m_cd_out, m_ret_decay_fwd, m_ret_decay_bwd, m_ret_norm, m_swa_sink, v_c_ctx, v_ada_w, v_ada_b, v_norm_mix, v_norm_ffn, v_norm_final, v_ffn_in, v_ffn_out, v_ab_in, v_ab_out, v_mla_q_norm, v_mla_kv_norm, v_mla_wq_b, v_mla_wkv_b, v_cmlp_v_norm, v_cmlp_ws, v_cmlp_bs, v_cd_in, v_cd_out, v_ret_decay_fwd, v_ret_decay_bwd, v_ret_norm, v_swa_sink):
    B, n, D = x.shape
    NL = ada_w.shape[0]
    NE, NO = ab_in.shape[0], cd_in.shape[0]
    me = _my_index()
    weights = dict(c_ctx=c_ctx, ada_w=ada_w, ada_b=ada_b, norm_mix=norm_mix, norm_ffn=norm_ffn, norm_final=norm_final,
                   ffn_in=ffn_in, ffn_out=ffn_out, ab_in=ab_in, ab_out=ab_out, mla_q_norm=mla_q_norm,
                   mla_kv_norm=mla_kv_norm, mla_wq_b=mla_wq_b, mla_wkv_b=mla_wkv_b, cmlp_v_norm=cmlp_v_norm,
                   cmlp_ws=cmlp_ws, cmlp_bs=cmlp_bs, cd_in=cd_in, cd_out=cd_out, ret_decay_fwd=ret_decay_fwd,
                   ret_decay_bwd=ret_decay_bwd, ret_norm=ret_norm, swa_sink=swa_sink)
    moments_m = dict(c_ctx=m_c_ctx, ada_w=m_ada_w, ada_b=m_ada_b, norm_mix=m_norm_mix, norm_ffn=m_norm_ffn,
                     norm_final=m_norm_final, ffn_in=m_ffn_in, ffn_out=m_ffn_out, ab_in=m_ab_in, ab_out=m_ab_out,
                     mla_q_norm=m_mla_q_norm, mla_kv_norm=m_mla_kv_norm, mla_wq_b=m_mla_wq_b, mla_wkv_b=m_mla_wkv_b,
                     cmlp_v_norm=m_cmlp_v_norm, cmlp_ws=m_cmlp_ws, cmlp_bs=m_cmlp_bs, cd_in=m_cd_in, cd_out=m_cd_out,
                     ret_decay_fwd=m_ret_decay_fwd, ret_decay_bwd=m_ret_decay_bwd, ret_norm=m_ret_norm,
                     swa_sink=m_swa_sink)
    moments_v = dict(c_ctx=v_c_ctx, ada_w=v_ada_w, ada_b=v_ada_b, norm_mix=v_norm_mix, norm_ffn=v_norm_ffn,
                     norm_final=v_norm_final, ffn_in=v_ffn_in, ffn_out=v_ffn_out, ab_in=v_ab_in, ab_out=v_ab_out,
                     mla_q_norm=v_mla_q_norm, mla_kv_norm=v_mla_kv_norm, mla_wq_b=v_mla_wq_b, mla_wkv_b=v_mla_wkv_b,
                     cmlp_v_norm=v_cmlp_v_norm, cmlp_ws=v_cmlp_ws, cmlp_bs=v_cmlp_bs, cd_in=v_cd_in, cd_out=v_cd_out,
                     ret_decay_fwd=v_ret_decay_fwd, ret_decay_bwd=v_ret_decay_bwd, ret_norm=v_ret_norm,
                     swa_sink=v_swa_sink)
    order = list(weights)

    Ns = ada_w.shape[2]
    (c_g,) = _all_gather("gather_c", [c])
    R = N_DEV * B + 8
    c_all = jnp.concatenate([c_g.reshape(N_DEV * B, D), jnp.broadcast_to(c_ctx[None], (8, D))], axis=0)
    ada_b_mine = lax.dynamic_slice_in_dim(ada_b, me * Ns, Ns, axis=1)[:, None, :]
    mods_shard = _ada_fwd(c_all, ada_w, ada_b_mine)
    (mods_g,) = _all_gather("gather_mods", [mods_shard])
    mods_full = jnp.transpose(mods_g, (1, 2, 0, 3)).reshape(NL, R, 6, D)
    mx = lax.dynamic_slice_in_dim(mods_full, me * B, B, axis=1)
    mh = jnp.broadcast_to(mods_full[:, N_DEV * B][:, None], (NL, B, 6, D))
    mods = jnp.stack([mx, mh], axis=2)

    big = ["ffn_in", "ffn_out", "ab_in", "ab_out", "cd_in", "cd_out", "mla_wq_b", "mla_wkv_b"]
    col_sharded = {"ffn_in", "ab_in", "cd_in", "mla_wq_b", "mla_wkv_b"}
    shards = {k: _to_bf16("cast_" + k, weights[k]) for k in big}
    first = {k: 0 if k.startswith("cd_") else 1 for k in big}
    a_keys = [k for k in big if first[k] and not k.startswith("ffn_")]
    b_keys = ["ffn_in", "ffn_out"]
    early = _all_gather("gather_wA", [shards[k][:1] for k in a_keys])
    (rn_g,) = _all_gather("gather_ret_norm", [ret_norm])
    small_first = [mods, rn_g] + list(early)
    wb_handle, wb_token = _exchange_start("gather_wB_start", [shards[k][:1] for k in b_keys], False, small_first)
    wc_handle, wc_token = _exchange_start("gather_wC_start", [shards[k][first[k]:] for k in big], False,
                                          small_first + [wb_token])
    mods = mods + (wb_token[0, 0] + wc_token[0, 0])

    def unshard(k, g):
        if k in col_sharded:
            f = jnp.transpose(g, (1, 2, 0, 3)).reshape(g.shape[1], g.shape[2], -1)
        else:
            f = jnp.transpose(g, (1, 0, 2, 3)).reshape(g.shape[1], -1, g.shape[3])
        return [f[i] for i in range(f.shape[0])]

    rn_full = jnp.transpose(rn_g, (1, 0, 2)).reshape(NO, -1)
    small_p = dict(weights, ret_norm=rn_full)
    full0 = {k: [None] * weights[k].shape[0] for k in big}
    for k, g in zip(a_keys, early):
        full0[k][:1] = unshard(k, g)

    def later_weights(stage, newest):
        if stage == 0:
            landed = _exchange_wait("gather_wB_wait", wb_handle, newest)
            landed = _fill_own("own_wB", landed, [shards[k][:1] for k in b_keys], False)
            for k, land in zip(b_keys, landed):
                full0[k][:1] = unshard(k, land)
        else:
            landed = _exchange_wait("gather_wC_wait", wc_handle, newest)
            landed = _fill_own("own_wC", landed, [shards[k][first[k]:] for k in big], False)
            for k, land in zip(big, landed):
                full0[k][first[k]:] = unshard(k, land)
        return _layer_weights(full0, small_p)

    def to_slots(k, gl):
        g = jnp.stack(gl)
        if k == "ffn_in":
            half = N_DEV // 2
            g = g.reshape(g.shape[0], 2, g.shape[2], half, -1)
            return jnp.transpose(g, (1, 3, 0, 2, 4)).reshape(N_DEV, g.shape[0], g.shape[2], -1)
        if k in col_sharded:
            return jnp.transpose(g.reshape(g.shape[0], g.shape[1], N_DEV, -1), (2, 0, 1, 3))
        return jnp.transpose(g.reshape(g.shape[0], N_DEV, -1, g.shape[2]), (1, 0, 2, 3))

    def big_grads(G):
        return dict(ffn_in=[g if g is None else _ffn_deinterleave(g) for g in G["ffn_in"]], ffn_out=G["ffn_out"],
                    ab_in=[g if g is None else _ab_unperm(g) for g in G["ab_in"]],
                    ab_out=G["ab_out"], cd_in=G["cd_in"], cd_out=G["cd_out"],
                    mla_wq_b=[g if g is None else _wq_unperm(g) for g in G["wq"]], mla_wkv_b=G["wkv"])

    sent = {}

    def early_grads(stage, G):
        parts = big_grads(G)
        if stage == 0:
            srcs = [to_slots(k, parts[k][first[k]:]) for k in big]
            handle, g_token = _exchange_start("scatter_gC_start", srcs, slotted=True)
        elif stage == 1:
            srcs = [to_slots(k, parts[k][:1]) for k in b_keys]
            handle, g_token = _exchange_start("scatter_gB_start", srcs, slotted=True)
        else:
            srcs = [to_slots(k, parts[k][:1]) for k in a_keys]
            handle, g_token = _exchange_start("scatter_gA_start", srcs, slotted=True)
        sent[stage] = (handle, srcs)
        return g_token[0, 0]

    loss_part, grad_x, dmods, G = _local_step(x, ctx, loss_target, mods, _layer_weights(full0, small_p),
                                              later_weights, early_grads)

    dmx = dmods[:, :, 0].reshape(NL, B, 6 * D)
    dmh = jnp.sum(dmods[:, :, 1], axis=1).reshape(NL, 1, 6 * D)
    (dm_g,) = _all_gather("gather_dmods", [jnp.concatenate([dmx, dmh], axis=1)])
    dmx_all = jnp.transpose(dm_g[:, :, :B], (1, 0, 2, 3)).reshape(NL, N_DEV * B, 6 * D)
    dmh_all = jnp.sum(dm_g[:, :, B], axis=0)
    dm_rows = jnp.concatenate([dmx_all, dmh_all[:, None], jnp.zeros((NL, 7, 6 * D), F32)], axis=1)
    g_ada_b = jnp.sum(dm_rows, axis=1)
    dm_mine = lax.dynamic_slice_in_dim(dm_rows, me * Ns, Ns, axis=2)
    g_ada_w, dcond = _ada_bwd(c_all, ada_w, dm_mine)
    sg = jax.nn.sigmoid(c_ctx)
    d_c_ctx_part = jnp.sum(dcond[:, N_DEV * B], axis=0) * (sg * (1.0 + c_ctx * (1.0 - sg)))

    def cat(parts):
        return jnp.concatenate([p.reshape(-1) for p in parts])

    dlg = jnp.stack([jnp.sum(G["lg"][j], axis=0) for j in range(NO)])
    sig_f, sig_b = jax.nn.sigmoid(-ret_decay_fwd), jax.nn.sigmoid(-ret_decay_bwd)
    small = dict(
        loss=loss_part[0, 0:1],
        c_ctx=d_c_ctx_part,
        norm_mix=cat(G["norm_mix"]), norm_ffn=cat(G["norm_ffn"]), norm_final=G["norm_final"].reshape(-1),
        mla_q_norm=cat(G["q_norm"]), mla_kv_norm=cat(G["kv_norm"]),
        cmlp_v_norm=cat([cat(G["v_norm"][j]) for j in range(NE)]),
        cmlp_ws=cat([jnp.stack(G["ws"][j]) for j in range(NE)]),
        cmlp_bs=cat([jnp.stack([b_[:, 0] for b_ in G["bs"][j]]) for j in range(NE)]),
        ret_decay_fwd=(dlg[:, 0] * sig_f).reshape(-1), ret_decay_bwd=(dlg[:, 1] * sig_b).reshape(-1),
        ret_norm=cat([cat(G["ret_norm"][j]) for j in range(NO)]),
        swa_sink=cat([jnp.sum(G["sink"][j], axis=0) for j in range(NO)]),
    )
    small_keys = list(small)
    sizes = [small[k].shape[0] for k in small_keys]
    total = sum(sizes)
    padded = -(-total // 2048) * 2048
    packed = jnp.concatenate([small[k] for k in small_keys] + [jnp.zeros((padded - total,), F32)]).reshape(-1, 128)
    (small_g,) = _all_gather("gather_small", [packed])

    landed_a = _fill_own("own_gA", _exchange_wait("scatter_gA_wait", sent[2][0], small_g), sent[2][1], True)
    landed0 = dict(zip(a_keys, landed_a))
    landed_b = _fill_own("own_gB", _exchange_wait("scatter_gB_wait", sent[1][0], grad_x), sent[1][1], True)
    landed0.update(zip(b_keys, landed_b))
    landed_c = _fill_own("own_gC", _exchange_wait("scatter_gC_wait", sent[0][0], grad_x), sent[0][1], True)
    landed = [[landed0[k], rest] if first[k] else [rest] for k, rest in zip(big, landed_c)]

    grads, deltas, new_m, new_v = {}, {}, {}, {}
    for k, land in zip(big, landed):
        grads[k], deltas[k], new_m[k], new_v[k] = _adamw_from_slots("adamw_" + k, weights[k], moments_m[k], moments_v[k], land)
    deltas["ada_w"], new_m["ada_w"], new_v["ada_w"] = [
        o.reshape(ada_w.shape) for o in _adamw("adamw_ada_w", _flat(ada_w), _flat(g_ada_w), _flat(m_ada_w), _flat(v_ada_w))]
    grads["ada_w"] = g_ada_w

    sums_only = ("loss", "ret_norm")

    def packed_of(src, fill):
        vals = [jnp.full((sizes[i],), fill, F32) if k in sums_only else src[k].reshape(-1)
                for i, k in enumerate(small_keys)]
        return jnp.concatenate(vals + [jnp.full((padded - total,), fill, F32)]).reshape(-1, 128)

    w_p, m_p, v_p = packed_of(weights, 0.0), packed_of(moments_m, 0.0), packed_of(moments_v, 1.0)

    def f_small(w_, m_, v_, land_):
        g = _sum_slots(land_)
        return (g,) + _adamw_math(w_, g, m_, v_)

    g_p, d_p, nm_p, nv_p = _ew("adamw_small", f_small, [w_p, m_p, v_p, small_g], [F32] * 4)
    offs = np.cumsum([0] + sizes)
    for i, k in enumerate(small_keys):
        sl = slice(int(offs[i]), int(offs[i + 1]))
        if k == "loss":
            loss = g_p.reshape(-1)[int(offs[i])]
        elif k == "ret_norm":
            g_full = g_p.reshape(-1)[sl].reshape(NO, -1)
            g_mine = lax.dynamic_slice_in_dim(g_full, me * ret_norm.shape[1], ret_norm.shape[1], axis=1)
            d_, m_, v_ = _adamw("adamw_ret_norm", *[jnp.pad(a, ((0, 8 - NO), (0, 128 - a.shape[1])), constant_values=cv)
                                                     for a, cv in ((ret_norm, 0.0), (g_mine, 0.0), (m_ret_norm, 0.0), (v_ret_norm, 1.0))])
            grads[k] = g_mine
            deltas[k], new_m[k], new_v[k] = [a[:NO, :ret_norm.shape[1]] for a in (d_, m_, v_)]
        else:
            shp = weights[k].shape
            grads[k], deltas[k], new_m[k], new_v[k] = [a.reshape(-1)[sl].reshape(shp) for a in (g_p, d_p, nm_p, nv_p)]
    pad_b = lambda a, cv=0.0: jnp.pad(a, ((0, 8 - NL), (0, 0)), constant_values=cv)
    d_, m_, v_ = _adamw("adamw_ada_b", pad_b(ada_b), pad_b(g_ada_b), pad_b(m_ada_b), pad_b(v_ada_b, 1.0))
    grads["ada_b"] = g_ada_b
    deltas["ada_b"], new_m["ada_b"], new_v["ada_b"] = d_[:NL], m_[:NL], v_[:NL]

    return (loss, grad_x, *[grads[k] for k in order], *[deltas[k] for k in order],
            *[new_m[k] for k in order], *[new_v[k] for k in order])
```

```python
import functools

import numpy as np
import jax
import jax.numpy as jnp
from jax import lax
from jax.experimental import pallas as pl
from jax.experimental.pallas import tpu as pltpu

F32 = jnp.float32
BF16 = jnp.bfloat16
EPS = 1e-6
NEG_INF = -1e30
N_DEV = 8
GRID_W = 64
ROPE_THETA = 10000.0
ROPE_DIM = 64
SWA_WINDOW = 128
MLA_SCALE = (128 + 64) ** -0.5
SWA_SCALE = 64 ** -0.5
RET_K_SCALE = 64 ** -0.5
ADAM_LR, ADAM_B1, ADAM_B2, ADAM_EPS, ADAM_WD, ADAM_STEP = 0.001, 0.9, 0.999, 1e-08, 0.01, 10
V7X_VMEM_LIMIT = 56 * 1024 * 1024
ROW_BLOCK_BUDGET = 24 * 1024 * 1024
MESH = pl.DeviceIdType.MESH


def _pallas(body, **kw):
    return pl.pallas_call(body, **kw)


def _params(sem=None):
    return pltpu.CompilerParams(dimension_semantics=sem, vmem_limit_bytes=V7X_VMEM_LIMIT)


def _tile(n, cap, align):
    best = None
    for t in range(align, min(n, cap) + 1, align):
        if n % t == 0:
            best = t
    return n if best is None else best


def _sds(shape, dtype):
    return jax.ShapeDtypeStruct(tuple(shape), dtype)


def _ew(name, f, ins, out_dtypes, cap_elems=131072):
    R, C = ins[0].shape[-2:]
    tr = _tile(R, max(16, cap_elems // C), 16)
    n_in = len(ins)

    def spec(a):
        if a.ndim == 2:
            return pl.BlockSpec((tr, C), lambda i: (i, 0))
        return pl.BlockSpec((a.shape[0], tr, C), lambda i: (0, i, 0))

    def body(*refs):
        outs = f(*[r[...] for r in refs[:n_in]])
        for r, o in zip(refs[n_in:], outs):
            r[...] = o.astype(r.dtype)

    return _pallas(
        body, name=name, grid=(R // tr,), in_specs=[spec(a) for a in ins],
        out_specs=[pl.BlockSpec((tr, C), lambda i: (i, 0)) for _ in out_dtypes],
        out_shape=[_sds((R, C), d) for d in out_dtypes], compiler_params=_params(("parallel",)),
    )(*ins)


def _to_bf16(name, w):
    w2 = w.reshape(-1, w.shape[-1])
    return _ew(name, lambda v: (v,), [w2], [BF16])[0].reshape(w.shape)


def _adamw_math(w, g, m, v):
    m = ADAM_B1 * m + (1.0 - ADAM_B1) * g
    v = ADAM_B2 * v + (1.0 - ADAM_B2) * (g * g)
    m_hat = m / (1.0 - ADAM_B1 ** ADAM_STEP)
    v_hat = v / (1.0 - ADAM_B2 ** ADAM_STEP)
    delta = -ADAM_LR * (m_hat / (jnp.sqrt(v_hat) + ADAM_EPS) + ADAM_WD * w)
    return delta, m, v


def _sum_slots(land):
    g = land[0].astype(F32)
    for s in range(1, land.shape[0]):
        g = g + land[s].astype(F32)
    return g


def _adamw_from_slots(name, w, m, v, lands):
    nl, K, C = w.shape
    tr = _tile(K, max(16, 65536 // C), 16)
    per = K // tr
    starts = np.cumsum([0] + [ld.shape[1] * per for ld in lands])
    ng = len(lands)

    def land_spec(g):
        lo, hi = int(starts[g]), int(starts[g + 1])
        return pl.BlockSpec((N_DEV, tr, C), lambda t: (0, jnp.clip(t, lo, hi - 1) - lo, 0))

    def body(*refs):
        w_ref, m_ref, v_ref = refs[:3]
        land_refs, out_refs = refs[3:3 + ng], refs[3 + ng:]
        t = pl.program_id(0)
        for g in range(ng):
            @pl.when((t >= int(starts[g])) & (t < int(starts[g + 1])))
            def _(g=g):
                grad = _sum_slots(land_refs[g][...])
                for r, o in zip(out_refs, (grad,) + _adamw_math(w_ref[...], grad, m_ref[...], v_ref[...])):
                    r[...] = o

    row = pl.BlockSpec((tr, C), lambda t: (t, 0))
    outs = _pallas(
        body, name=name, grid=(nl * per,), in_specs=[row] * 3 + [land_spec(g) for g in range(ng)],
        out_specs=[row] * 4, out_shape=[_sds((nl * K, C), F32)] * 4, compiler_params=_params(("parallel",)),
    )(w.reshape(-1, C), m.reshape(-1, C), v.reshape(-1, C), *[ld.reshape(N_DEV, -1, C) for ld in lands])
    return [o.reshape(w.shape) for o in outs]


def _adamw(name, w, g, m, v):
    outs = _ew(name, lambda w_, g_, m_, v_: _adamw_math(w_, g_, m_, v_), [w, g, m, v], [F32] * 3)
    return outs


def _mm(name, a, b, ta=False, tb=False, out_dtype=F32, add=None):
    M, K = (a.shape[1], a.shape[0]) if ta else a.shape
    N = b.shape[0] if tb else b.shape[1]
    tm = _tile(M, 1408, 128)
    tn = _tile(N, 1024, 128)
    if tn < 256 and N <= 2432:
        tn = N
    tk = _tile(K, 3072, 128)
    nk = K // tk
    a_spec = pl.BlockSpec((tk, tm), lambda i, j, k: (k, i)) if ta else pl.BlockSpec((tm, tk), lambda i, j, k: (i, k))
    b_spec = pl.BlockSpec((tn, tk), lambda i, j, k: (j, k)) if tb else pl.BlockSpec((tk, tn), lambda i, j, k: (k, j))
    o_spec = pl.BlockSpec((tm, tn), lambda i, j, k: (i, j))
    dims = (((0 if ta else 1,), (1 if tb else 0,)), ((), ()))
    has_add = add is not None

    def product(a_ref, b_ref):
        return lax.dot_general(a_ref[...].astype(BF16), b_ref[...].astype(BF16), dims, preferred_element_type=F32)

    def body_single(*refs):
        acc = product(refs[0], refs[1])
        if has_add:
            acc = acc + refs[2][...]
        refs[-1][...] = acc.astype(refs[-1].dtype)

    def body(*refs):
        a_ref, b_ref = refs[0], refs[1]
        add_ref = refs[2] if has_add else None
        o_ref, acc = refs[-2], refs[-1]
        k = pl.program_id(2)

        @pl.when(k == 0)
        def _():
            acc[...] = add_ref[...] if has_add else jnp.zeros_like(acc)

        acc[...] += product(a_ref, b_ref)

        @pl.when(k == nk - 1)
        def _():
            o_ref[...] = acc[...].astype(o_ref.dtype)

    ins = [a, b] + ([add] if has_add else [])
    specs = [a_spec, b_spec] + ([o_spec] if has_add else [])
    return _pallas(
        body_single if nk == 1 else body, name=name, grid=(M // tm, N // tn, nk), in_specs=specs, out_specs=o_spec,
        out_shape=_sds((M, N), out_dtype), scratch_shapes=[] if nk == 1 else [pltpu.VMEM((tm, tn), F32)],
        compiler_params=_params(("parallel", "parallel", "arbitrary")),
    )(*ins)


def _mm_pair(name, a1, a2, b, out_dtype):
    M, Kh = a1.shape
    N = b.shape[1]
    tm = _tile(M, 1408, 128)

    def body(a1_ref, a2_ref, b1_ref, b2_ref, o_ref):
        o_ref[...] = (_dot(a1_ref[...].astype(BF16), b1_ref[...].astype(BF16))
                      + _dot(a2_ref[...].astype(BF16), b2_ref[...].astype(BF16))).astype(o_ref.dtype)

    a_spec = pl.BlockSpec((tm, Kh), lambda i: (i, 0))
    return _pallas(
        body, name=name, grid=(M // tm,),
        in_specs=[a_spec, a_spec, pl.BlockSpec((Kh, N), lambda i: (0, 0)), pl.BlockSpec((Kh, N), lambda i: (1, 0))],
        out_specs=pl.BlockSpec((tm, N), lambda i: (i, 0)), out_shape=_sds((M, N), out_dtype),
        compiler_params=_params(("parallel",)),
    )(a1, a2, b, b)


def _ffn_tile(F):
    return F // 2 if (F // 2) % 128 == 0 else F


def _ffn_interleave(w):
    D, F2 = w.shape
    T = _ffn_tile(F2 // 2)
    nj = F2 // (2 * T)

    def body(a_ref, b_ref, o_ref):
        o_ref[:, :T] = a_ref[...]
        o_ref[:, T:] = b_ref[...]

    return _pallas(
        body, name="ffn_interleave", grid=(nj,),
        in_specs=[pl.BlockSpec((D, T), lambda j: (0, j)), pl.BlockSpec((D, T), lambda j: (0, j + nj))],
        out_specs=pl.BlockSpec((D, 2 * T), lambda j: (0, j)), out_shape=_sds((D, F2), w.dtype),
        compiler_params=_params(("parallel",)),
    )(w, w)


def _ffn_deinterleave(w):
    D, F2 = w.shape
    T = _ffn_tile(F2 // 2)
    nj = F2 // (2 * T)

    def body(w_ref, o_ref):
        o_ref[0] = w_ref[:, :T]
        o_ref[1] = w_ref[:, T:]

    return _pallas(
        body, name="ffn_deinterleave", grid=(nj,), in_specs=[pl.BlockSpec((D, 2 * T), lambda j: (0, j))],
        out_specs=pl.BlockSpec((2, D, T), lambda j: (0, 0, j)), out_shape=_sds((2, D, F2 // 2), w.dtype),
        compiler_params=_params(("parallel",)),
    )(w)


def _ffn_specs(M, D, F):
    T = _ffn_tile(F)
    tm = _tile(M, 512, 128)
    x_spec = pl.BlockSpec((tm, D), lambda i, j: (i, 0))
    wi_spec = pl.BlockSpec((D, 2 * T), lambda i, j: (0, j))
    wo_spec = pl.BlockSpec((T, D), lambda i, j: (j, 0))
    return T, tm, F // T, x_spec, wi_spec, wo_spec


def _ffn_fwd(name, xn, w_in, w_out):
    M, D = xn.shape
    T, tm, nj, x_spec, wi_spec, wo_spec = _ffn_specs(M, D, w_out.shape[0])

    def body(x_ref, wi_ref, wo_ref, y_ref, acc):
        j = pl.program_id(1)
        z = _dot(x_ref[...], wi_ref[...])
        hid = (jax.nn.silu(z[:, :T]) * z[:, T:]).astype(BF16)
        part = _dot(hid, wo_ref[...])

        @pl.when(j == 0)
        def _():
            acc[...] = part

        @pl.when(j > 0)
        def _():
            acc[...] += part

        @pl.when(j == nj - 1)
        def _():
            y_ref[...] = acc[...].astype(y_ref.dtype)

    return _pallas(
        body, name=name, grid=(M // tm, nj), in_specs=[x_spec, wi_spec, wo_spec], out_specs=x_spec,
        out_shape=_sds((M, D), BF16), scratch_shapes=[pltpu.VMEM((tm, D), F32)],
        compiler_params=_params(("parallel", "arbitrary")),
    )(xn, w_in, w_out)


def _ffn_mid_bwd(name, xn, dy, w_in, w_out):
    M, D = xn.shape
    F = w_out.shape[0]
    T, tm, nj, x_spec, wi_spec, wo_spec = _ffn_specs(M, D, F)

    def body(x_ref, dy_ref, wi_ref, wo_ref, h_ref, dz_ref):
        z = _dot(x_ref[...], wi_ref[...])
        a, b = z[:, :T], z[:, T:]
        dh = _dot_nt(dy_ref[...], wo_ref[...])
        sig = jax.nn.sigmoid(a)
        act = a * sig
        h_ref[...] = (act * b).astype(BF16)
        dz_ref[:, :T] = (dh * b * (sig * (1.0 + a * (1.0 - sig)))).astype(BF16)
        dz_ref[:, T:] = (dh * act).astype(BF16)

    return _pallas(
        body, name=name, grid=(M // tm, nj), in_specs=[x_spec, x_spec, wi_spec, wo_spec],
        out_specs=[pl.BlockSpec((tm, T), lambda i, j: (i, j)),
                   pl.BlockSpec((tm, 2 * T), lambda i, j: (i, j))],
        out_shape=[_sds((M, F), BF16), _sds((M, 2 * F), BF16)],
        compiler_params=_params(("parallel", "parallel")),
    )(xn, dy, w_in, w_out)


class _Rows:
    def __init__(self, name, f, B, L, n, tm, arrays, pieces, samp, samp_pieces, glob, out_arrays, out_pieces):
        self.name, self.f, self.B, self.L, self.n, self.tm = name, f, B, L, n, tm
        self.arrays, self.pieces, self.samp, self.samp_pieces, self.glob = arrays, pieces, samp, samp_pieces, glob
        self.out_arrays, self.out_pieces = out_arrays, out_pieces
        self.nx = n // tm

    def _samples_per_step(self, row_bytes):
        for bb in (4, 2, 1):
            if self.B % bb == 0 and 2 * bb * self.tm * row_bytes <= ROW_BLOCK_BUDGET:
                return bb
        return 1

    def _row_spec(self, bb, C, batched=True):
        tm = self.tm
        if batched:
            return pl.BlockSpec((bb, tm, C), lambda g, i: (g, i, 0))
        return pl.BlockSpec((tm, C), lambda g, i: (i, 0))

    def _in_specs(self, bb, unread=()):
        nx, tm = self.nx, self.tm
        specs = [self._row_spec(bb, a.shape[-1], a.ndim == 3) for a in self.arrays]
        for ai in unread:
            specs[ai] = pl.BlockSpec((bb, tm, self.arrays[ai].shape[-1]), lambda g, i: (0, 0, 0))
        specs += [pl.BlockSpec((bb, None) + s.shape[2:], lambda g, i: (g, i // nx, 0, 0)) for s in self.samp]
        specs += [pl.BlockSpec(g.shape, lambda g_, i, nd=g.ndim: (0,) * nd) for g in self.glob]
        return specs

    def _load(self, b, a_refs, s_refs, g_refs):
        args = [(a_refs[ai][b, :, cs:cs + cw] if self.arrays[ai].ndim == 3 else a_refs[ai][:, cs:cs + cw]).astype(F32)
                for ai, cs, cw in self.pieces]
        args += [s_refs[si][b, r:r + 1, :].astype(F32) for si, r in self.samp_pieces]
        args += [g[...].astype(F32) for g in g_refs]
        return args

    @staticmethod
    def _row_bytes(arrays):
        return sum(a.shape[-1] * a.dtype.itemsize for a in arrays if a.ndim == 3)

    def fwd(self):
        na, ns, ng = len(self.arrays), len(self.samp), len(self.glob)
        bb = self._samples_per_step(self._row_bytes(self.arrays)
                                    + sum(C * jnp.dtype(d).itemsize for C, d in self.out_arrays))

        def body(*refs):
            a_refs, s_refs, g_refs = refs[:na], refs[na:na + ns], refs[na + ns:na + ns + ng]
            o_refs = refs[na + ns + ng:]
            for b in range(bb):
                outs = self.f(*self._load(b, a_refs, s_refs, g_refs))
                for (oi, cs, cw), o in zip(self.out_pieces, outs):
                    o_refs[oi][b, :, cs:cs + cw] = o.astype(o_refs[oi].dtype)

        return _pallas(
            body, name=self.name + "_fwd", grid=(self.B // bb, self.L // self.tm), in_specs=self._in_specs(bb),
            out_specs=[self._row_spec(bb, C) for C, _ in self.out_arrays],
            out_shape=[_sds((self.B, self.L, C), d) for C, d in self.out_arrays],
            compiler_params=_params(("parallel", "parallel")),
        )(*self.arrays, *self.samp, *self.glob)

    def bwd(self, cts, grad_arrays, grad_glob=(), add=None, samp=None, unread=()):
        samp = self.samp if samp is None else samp
        na, ns, ng, nc = len(self.arrays), len(self.samp), len(self.glob), len(cts)
        ct_off = [c[1] if isinstance(c, tuple) else 0 for c in cts]
        cts = [c[0] if isinstance(c, tuple) else c for c in cts]
        add = add or {}
        add_keys = list(add)
        g_idx = list(grad_arrays)
        nx, B = self.nx, self.B
        n_in = na + ns + ng + nc + len(add_keys)
        n_pieces, n_sp = len(self.pieces), len(self.samp_pieces)
        streamed = list(self.arrays) + cts + [add[k] for k in add_keys]
        bb = self._samples_per_step(self._row_bytes(streamed) + sum(
            self.arrays[ai].shape[-1] * jnp.dtype(grad_arrays[ai]).itemsize for ai in g_idx))

        def body(*refs):
            a_refs, s_refs, g_refs = refs[:na], refs[na:na + ns], refs[na + ns:na + ns + ng]
            c_refs = refs[na + ns + ng:na + ns + ng + nc]
            add_refs = refs[na + ns + ng + nc:n_in]
            d_refs = refs[n_in:n_in + len(g_idx)]
            ds_refs = refs[n_in + len(g_idx):n_in + len(g_idx) + n_sp]
            dg_refs = refs[n_in + len(g_idx) + n_sp:]
            g_step, i = pl.program_id(0), pl.program_id(1)

            @pl.when((i == 0) | (i == nx))
            def _():
                for r in ds_refs:
                    r[...] = jnp.zeros_like(r)

            @pl.when((g_step == 0) & (i == 0))
            def _():
                for r in dg_refs:
                    r[...] = jnp.zeros_like(r)

            for b in range(bb):
                args = self._load(b, a_refs, s_refs, g_refs)
                _, vjp = jax.vjp(lambda *xs: tuple(self.f(*xs)), *args)
                grads = vjp(tuple(c_refs[oi][b, :, ct_off[oi] + cs:ct_off[oi] + cs + cw].astype(F32)
                                  for oi, cs, cw in self.out_pieces))
                for k, ai in enumerate(g_idx):
                    covered = sum(cw for pa, _, cw in self.pieces if pa == ai)
                    if covered < self.arrays[ai].shape[-1]:
                        d_refs[k][b] = jnp.zeros(d_refs[k].shape[1:], d_refs[k].dtype)
                    for (pa, cs, cw), gr in zip(self.pieces, grads[:n_pieces]):
                        if pa == ai:
                            if ai in add:
                                gr = gr + add_refs[add_keys.index(ai)][b, :, cs:cs + cw]
                            d_refs[k][b, :, cs:cs + cw] = gr.astype(d_refs[k].dtype)
                for r, gr in zip(ds_refs, grads[n_pieces:n_pieces + n_sp]):
                    r[b] += gr
                for r, gi in zip(dg_refs, grad_glob):
                    r[...] += grads[n_pieces + n_sp + gi]

        in_specs = self._in_specs(bb, unread) + [self._row_spec(bb, c.shape[-1]) for c in cts]
        in_specs += [self._row_spec(bb, add[k].shape[-1]) for k in add_keys]
        out_specs = [self._row_spec(bb, self.arrays[ai].shape[-1]) for ai in g_idx]
        out_shape = [_sds(self.arrays[ai].shape, grad_arrays[ai]) for ai in g_idx]
        for si, _ in self.samp_pieces:
            C = self.samp[si].shape[-1]
            out_specs.append(pl.BlockSpec((bb, None, 1, C), lambda g, i: (g, i // nx, 0, 0)))
            out_shape.append(_sds((B, 2, 1, C), F32))
        for gi in grad_glob:
            g = self.glob[gi]
            out_specs.append(pl.BlockSpec(g.shape, lambda g_, i, nd=g.ndim: (0,) * nd))
            out_shape.append(_sds(g.shape, F32))
        outs = _pallas(
            body, name=self.name + "_bwd", grid=(B // bb, self.L // self.tm), in_specs=in_specs, out_specs=out_specs,
            out_shape=out_shape, compiler_params=_params(("arbitrary", "arbitrary")),
        )(*self.arrays, *samp, *self.glob, *cts, *[add[k] for k in add_keys])
        ng_ = len(g_idx)
        return outs[:ng_], outs[ng_:ng_ + n_sp], outs[ng_ + n_sp:]


def _rms(x, g):
    return x * lax.rsqrt(jnp.mean(x * x, axis=-1, keepdims=True) + EPS) * g


@jax.custom_vjp
def _swap_halves(x):
    w = x.shape[-1]
    lane = lax.broadcasted_iota(jnp.int32, x.shape, x.ndim - 1)
    up = pltpu.roll(x, w - ROPE_DIM // 2, x.ndim - 1)
    down = pltpu.roll(x, ROPE_DIM // 2, x.ndim - 1)
    return jnp.where(lane % ROPE_DIM < ROPE_DIM // 2, up, down)


_swap_halves.defvjp(lambda x: (_swap_halves(x), None), lambda _, ct: (_swap_halves(ct),))


@jax.custom_vjp
def _bdot(a, b):
    return jnp.dot(a.astype(BF16), b.astype(BF16), preferred_element_type=F32)


_bdot.defvjp(lambda a, b: (_bdot(a, b), (a, b)),
             lambda res, ct: (_dot_nt(ct.astype(BF16), res[1].astype(BF16)), _dot_tn(res[0].astype(BF16), ct.astype(BF16))))


def _rope(x, cos2, sin2):
    return x * cos2 + _swap_halves(x) * sin2


def _f_pre(s, shift, scale, g):
    return (_rms(s, g) * (1.0 + scale) + shift,)


def _f_res(s, y, gate):
    return (s + gate * y,)


def _f_res_pre(s, y, gate, shift, scale, g):
    s1 = s + gate * y
    return s1, _rms(s1, g) * (1.0 + scale) + shift


def _f_ab_pre(*args):
    kv_lat, q_lat, kpe = args[0:3]
    us, vs = args[3:7], args[7:11]
    cos2, sin2 = args[11:13]
    kv_norm, q_norm = args[13:15]
    vns, wss, bss = args[15:19], args[19:23], args[23:27]
    outs = [_rms(kv_lat, kv_norm), _rms(q_lat, q_norm), _rope(kpe, cos2, sin2)]
    for u, v, vn, ws, bs in zip(us, vs, vns, wss, bss):
        vg = _rms(jax.nn.gelu(v), vn)
        mixed = _bdot(ws, vg) + bs
        outs.append(jax.nn.gelu(u) * mixed)
    return tuple(outs)


def _f_q_rope(qn, qr, cos2, sin2):
    return qn, _rope(qr, cos2, sin2)


def _f_cd_pre(rk, rv, sk, sv, rq, rg, sq, c256, s256, c128, s128, c512, s512):
    return (_rope(rq, c256, s256), _rope(rk * RET_K_SCALE, c256, s256), rv,
            _rope(sq, c512, s512), _rope(sk, c128, s128), sv, rg)


def _f_cd_merge(*args):
    ys, rgs, rns = args[0:4], args[4:8], args[8:12]
    return tuple(_rms(y, rn) * jax.nn.silu(rg) for y, rg, rn in zip(ys, rgs, rns))


def _dot_nt(a, b):
    return lax.dot_general(a, b, (((1,), (1,)), ((), ())), preferred_element_type=F32)


def _dot_tn(a, b):
    return lax.dot_general(a, b, (((0,), (0,)), ((), ())), preferred_element_type=F32)


def _dot(a, b):
    return jnp.dot(a, b, preferred_element_type=F32)


def _tile_spec(tq, C):
    return pl.BlockSpec((None, tq, C), lambda b, i: (b, i, 0))


def _full_spec(L, C):
    return pl.BlockSpec((None, L, C), lambda b, i: (b, 0, 0))


def _mla_half(h):
    lane = lax.broadcasted_iota(jnp.int32, (1, 128), 1)
    return (lane < 64) if h % 2 == 0 else (lane >= 64)


def _mla_query(q_ref, h):
    pair = q_ref[:, 512 + 128 * (h // 2):640 + 128 * (h // 2)]
    return jnp.concatenate([q_ref[:, 128 * h:128 * h + 128], jnp.where(_mla_half(h), pair, jnp.zeros_like(pair))], axis=1)


def _mla_fill_keys(kcat, kv_ref, kpe_ref):
    kp = kpe_ref[...]
    for h in range(4):
        kcat[h, :, 0:128] = kv_ref[:, 256 * h:256 * h + 128]
        kcat[h, :, 128:256] = jnp.where(_mla_half(h), kp, jnp.zeros_like(kp))


def _mla_fwd(q, kv, kpe, n, tq):
    B, L, _ = q.shape

    def body(q_ref, kv_ref, kpe_ref, o_ref, lse_ref, kcat):
        i = pl.program_id(1)

        @pl.when(i == 0)
        def _():
            _mla_fill_keys(kcat, kv_ref, kpe_ref)

        def tile(keys):
            for h in range(4):
                s = _dot_nt(_mla_query(q_ref, h), kcat[h, keys, :]) * MLA_SCALE
                m = jnp.max(s, axis=1, keepdims=True)
                e = jnp.exp(s - m)
                l = jnp.sum(e, axis=1, keepdims=True)
                p = (e * (1.0 / l)).astype(BF16)
                o_ref[:, 128 * h:128 * h + 128] = _dot(p, kv_ref[keys, 256 * h + 128:256 * h + 256]).astype(o_ref.dtype)
                lse_ref[:, h:h + 1] = m + jnp.log(l)

        pl.when(i < n // tq)(functools.partial(tile, slice(0, L)))
        pl.when(i >= n // tq)(functools.partial(tile, slice(n, L)))

    return _pallas(
        body, name="mla_fwd", grid=(B, L // tq),
        in_specs=[_tile_spec(tq, 768), _full_spec(L, 1024), _full_spec(L, 128)],
        out_specs=[_tile_spec(tq, 512), _tile_spec(tq, 4)],
        out_shape=[_sds((B, L, 512), BF16), _sds((B, L, 4), F32)],
        scratch_shapes=[pltpu.VMEM((4, L, 256), BF16)],
        compiler_params=_params(("parallel", "arbitrary")),
    )(q, kv, kpe)


def _mla_bwd(q, kv, kpe, lse, do, n, tq):
    B, L, _ = q.shape

    def body(q_ref, kv_ref, kpe_ref, lse_ref, do_ref, dq_ref, dkv_ref, dkpe_ref, kcat):
        i = pl.program_id(1)

        @pl.when(i == 0)
        def _():
            dkv_ref[...] = jnp.zeros_like(dkv_ref)
            dkpe_ref[...] = jnp.zeros_like(dkpe_ref)
            _mla_fill_keys(kcat, kv_ref, kpe_ref)

        def tile(keys):
            rope_pair = None
            for h in range(4):
                qc, kc = _mla_query(q_ref, h), kcat[h, keys, :]
                v = kv_ref[keys, 256 * h + 128:256 * h + 256]
                p = jnp.exp(_dot_nt(qc, kc) * MLA_SCALE - lse_ref[:, h:h + 1])
                doh = do_ref[:, 128 * h:128 * h + 128].astype(BF16)
                dp = _dot_nt(doh, v)
                delta = jnp.sum(p * dp, axis=1, keepdims=True)
                ds = (p * (dp - delta) * MLA_SCALE).astype(BF16)
                dqc = _dot(ds, kc)
                dq_ref[:, 128 * h:128 * h + 128] = dqc[:, 0:128]
                if h % 2 == 0:
                    rope_pair = dqc[:, 128:256]
                else:
                    dq_ref[:, 512 + 128 * (h // 2):640 + 128 * (h // 2)] = jnp.where(_mla_half(h), dqc[:, 128:256], rope_pair)
                dkc = _dot_tn(ds, qc)
                dkv_ref[keys, 256 * h:256 * h + 128] += dkc[:, 0:128]
                dkpe_ref[keys, :] += dkc[:, 128:256]
                dkv_ref[keys, 256 * h + 128:256 * h + 256] += _dot_tn(p.astype(BF16), doh)

        pl.when(i < n // tq)(functools.partial(tile, slice(0, L)))
        pl.when(i >= n // tq)(functools.partial(tile, slice(n, L)))

    return _pallas(
        body, name="mla_bwd", grid=(B, L // tq),
        in_specs=[_tile_spec(tq, 768), _full_spec(L, 1024), _full_spec(L, 128), _tile_spec(tq, 4),
                  pl.BlockSpec((None, tq, 512), lambda b, i: (b, i, 0))],
        out_specs=[_tile_spec(tq, 768), _full_spec(L, 1024), _full_spec(L, 128)],
        out_shape=[_sds((B, L, 768), F32), _sds((B, L, 1024), F32), _sds((B, L, 128), F32)],
        scratch_shapes=[pltpu.VMEM((4, L, 256), BF16)],
        compiler_params=_params(("parallel", "arbitrary")),
    )(q, kv, kpe, lse, do)


def _swa_window(i, tq, n):
    W = min(tq + 2 * SWA_WINDOW, n)
    lo = jnp.clip(i * tq - SWA_WINDOW, 0, n - W)
    return pl.multiple_of(lo, 128), W


def _swa_mask(i, lo, tq, W, n):
    qpos = i * tq + lax.broadcasted_iota(jnp.int32, (tq, 1), 0)
    kpos = lo + lax.broadcasted_iota(jnp.int32, (1, W), 1)
    return (jnp.abs(qpos - kpos) <= SWA_WINDOW) & (qpos < n)


def _swa_fill(rep, src_ref):
    for g in range(2):
        rep[:, 256 * g:256 * g + 256] = jnp.concatenate([src_ref[:, 64 * g:64 * g + 64]] * 4, axis=1)


def _swa_head_rows(a):
    lane = lax.broadcasted_iota(jnp.int32, (1, 256), 1)
    return jnp.concatenate([jnp.where(lane // 64 == h, a, jnp.zeros_like(a)) for h in range(4)], axis=0)


def _swa_fold_rows(a, tq):
    lane = lax.broadcasted_iota(jnp.int32, (1, 256), 1)
    out = jnp.where(lane // 64 == 0, a[0:tq], 0.0)
    for h in range(1, 4):
        out = jnp.where(lane // 64 == h, a[h * tq:(h + 1) * tq], out)
    return out


def _swa_fwd(q, k, v, sink, n, tq):
    B, L, _ = q.shape

    def body(q_ref, k_ref, v_ref, sink_ref, o_ref, lse_ref, krep, vrep):
        i = pl.program_id(1)

        @pl.when(i == 0)
        def _():
            _swa_fill(krep, k_ref)
            _swa_fill(vrep, v_ref)

        def tile(band):
            for g in range(2):
                gc = slice(256 * g, 256 * g + 256)
                qs = _swa_head_rows(q_ref[:, gc])
                sk = jnp.concatenate([jnp.broadcast_to(sink_ref[0:1, 4 * g + h:4 * g + h + 1], (tq, 1))
                                      for h in range(4)], axis=0)
                s2 = _dot_nt(qs, krep[n:L, gc]) * SWA_SCALE
                m = jnp.maximum(jnp.max(s2, axis=1, keepdims=True), sk)
                if band:
                    lo, W = _swa_window(i, tq, n)
                    mask = jnp.concatenate([_swa_mask(i, lo, tq, W, n)] * 4, axis=0)
                    s1 = jnp.where(mask, _dot_nt(qs, krep[pl.ds(lo, W), gc]) * SWA_SCALE, NEG_INF)
                    m = jnp.maximum(m, jnp.max(s1, axis=1, keepdims=True))
                e2 = jnp.exp(s2 - m)
                l = jnp.sum(e2, axis=1, keepdims=True) + jnp.exp(sk - m)
                if band:
                    e1 = jnp.exp(s1 - m)
                    l = l + jnp.sum(e1, axis=1, keepdims=True)
                r = 1.0 / l
                o = _dot((e2 * r).astype(BF16), vrep[n:L, gc])
                if band:
                    o = o + _dot((e1 * r).astype(BF16), vrep[pl.ds(lo, W), gc])
                o_ref[:, gc] = _swa_fold_rows(o, tq).astype(o_ref.dtype)
                lse = m + jnp.log(l)
                for h in range(4):
                    lse_ref[:, 4 * g + h:4 * g + h + 1] = lse[h * tq:(h + 1) * tq]

        pl.when(i < n // tq)(functools.partial(tile, True))
        pl.when(i >= n // tq)(functools.partial(tile, False))

    return _pallas(
        body, name="swa_fwd", grid=(B, L // tq),
        in_specs=[_tile_spec(tq, 512), _full_spec(L, 128), _full_spec(L, 128), pl.BlockSpec((1, 8), lambda b, i: (0, 0))],
        out_specs=[_tile_spec(tq, 512), _tile_spec(tq, 8)],
        out_shape=[_sds((B, L, 512), BF16), _sds((B, L, 8), F32)],
        scratch_shapes=[pltpu.VMEM((L, 512), BF16), pltpu.VMEM((L, 512), BF16)],
        compiler_params=_params(("parallel", "arbitrary")),
    )(q, k, v, sink)


def _swa_bwd(q, k, v, sink, lse, do, n, tq):
    B, L, _ = q.shape

    def body(q_ref, k_ref, v_ref, sink_ref, lse_ref, do_ref, dq_ref, dk_ref, dv_ref, dsink_ref):
        i = pl.program_id(1)

        @pl.when(i == 0)
        def _():
            dk_ref[...] = jnp.zeros_like(dk_ref)
            dv_ref[...] = jnp.zeros_like(dv_ref)
            dsink_ref[...] = jnp.zeros_like(dsink_ref)

        def tile(band):
            kc, vc = k_ref[n:L, :], v_ref[n:L, :]
            if band:
                lo, W = _swa_window(i, tq, n)
                mask = _swa_mask(i, lo, tq, W, n)
                kl, vl = k_ref[pl.ds(lo, W), :], v_ref[pl.ds(lo, W), :]
            for h in range(8):
                g = h // 4
                cols = slice(64 * g, 64 * g + 64)
                qh = q_ref[:, 64 * h:64 * h + 64]
                lse_h = lse_ref[:, h:h + 1]
                doh = do_ref[:, 64 * h:64 * h + 64].astype(BF16)
                p2 = jnp.exp(_dot_nt(qh, kc[:, cols]) * SWA_SCALE - lse_h)
                ps = jnp.exp(sink_ref[0:1, h:h + 1] - lse_h)
                dp2 = _dot_nt(doh, vc[:, cols])
                delta = jnp.sum(p2 * dp2, axis=1, keepdims=True)
                if band:
                    p1 = jnp.exp(jnp.where(mask, _dot_nt(qh, kl[:, cols]) * SWA_SCALE, NEG_INF) - lse_h)
                    dp1 = _dot_nt(doh, vl[:, cols])
                    delta = delta + jnp.sum(p1 * dp1, axis=1, keepdims=True)
                ds2 = (p2 * (dp2 - delta) * SWA_SCALE).astype(BF16)
                dq = _dot(ds2, kc[:, cols])
                dk_ref[n:L, cols] += _dot_tn(ds2, qh)
                dv_ref[n:L, cols] += _dot_tn(p2.astype(BF16), doh)
                if band:
                    ds1 = (p1 * (dp1 - delta) * SWA_SCALE).astype(BF16)
                    dq = dq + _dot(ds1, kl[:, cols])
                    dk_ref[pl.ds(lo, W), cols] += _dot_tn(ds1, qh)
                    dv_ref[pl.ds(lo, W), cols] += _dot_tn(p1.astype(BF16), doh)
                dq_ref[:, 64 * h:64 * h + 64] = dq
                dsink_ref[0:1, h:h + 1] += jnp.sum(-ps * delta, axis=0, keepdims=True)

        pl.when(i < n // tq)(functools.partial(tile, True))
        pl.when(i >= n // tq)(functools.partial(tile, False))

    return _pallas(
        body, name="swa_bwd", grid=(B, L // tq),
        in_specs=[_tile_spec(tq, 512), _full_spec(L, 128), _full_spec(L, 128), pl.BlockSpec((1, 8), lambda b, i: (0, 0)),
                  _tile_spec(tq, 8), pl.BlockSpec((None, tq, 512), lambda b, i: (b, i, 1))],
        out_specs=[_tile_spec(tq, 512), _full_spec(L, 128), _full_spec(L, 128),
                   pl.BlockSpec((None, 1, 8), lambda b, i: (b, 0, 0))],
        out_shape=[_sds((B, L, 512), F32), _sds((B, L, 128), F32), _sds((B, L, 128), F32), _sds((B, 1, 8), F32)],
        compiler_params=_params(("parallel", "arbitrary")),
    )(q, k, v, sink, lse, do)


def _ret_decay(i, tq, L, n, ctx_tile):
    qi = i * tq + lax.broadcasted_iota(jnp.int32, (tq, 1), 0)
    kc = lax.broadcasted_iota(jnp.int32, (1, L - n), 1)
    d_hb = (n + kc - qi).astype(F32)
    if ctx_tile:
        return None, ((qi - L) - (kc - (L - n))).astype(F32), d_hb
    d_x = (qi - lax.broadcasted_iota(jnp.int32, (1, n), 1)).astype(F32)
    return d_x, (qi - (kc - (L - n))).astype(F32), d_hb


def _ret_head(a, h):
    lane = lax.broadcasted_iota(jnp.int32, (1, a.shape[1]), 1)
    return jnp.where(lane // 64 == h, a, jnp.zeros_like(a))


def _ret_mask(dist, lg):
    return jnp.where(dist >= 0.0, jnp.exp(lg * jnp.maximum(dist, 0.0)), 0.0)


def _ret_weights(n, L, d_x, d_hf, d_hb, lg_f, lg_b, with_grad):
    m_f, m_b = _ret_mask(d_hf, lg_f), _ret_mask(d_hb, lg_b)
    ctx = (slice(n, L), m_f + m_b)
    if with_grad:
        ctx += (m_f * jnp.maximum(d_hf, 0.0), m_b * jnp.maximum(d_hb, 0.0))
    if d_x is None:
        return [ctx]
    e_x = jnp.exp(jnp.where(d_x >= 0.0, lg_f, -lg_b) * d_x)
    lat = (slice(0, n), jnp.where(d_x == 0.0, 2.0 * e_x, e_x))
    if with_grad:
        lat += (e_x * jnp.maximum(d_x, 0.0), e_x * jnp.maximum(-d_x, 0.0))
    return [lat, ctx]


def _ret_fwd(q, k, v, lg, n, tq):
    B, L, _ = q.shape

    def body(q_ref, k_ref, v_ref, lg_ref, y_ref):
        i = pl.program_id(1)

        def tile(ctx_tile):
            dist = _ret_decay(i, tq, L, n, ctx_tile)
            for h in range(4):
                qh = _ret_head(q_ref[...], h)
                y = None
                for rows, dec_r in _ret_weights(n, L, *dist, lg_ref[0:1, h:h + 1], lg_ref[1:2, h:h + 1], False):
                    a = _dot_nt(qh, k_ref[rows, :]) * dec_r
                    part = _dot(a.astype(BF16), v_ref[rows, 128 * h:128 * h + 128])
                    y = part if y is None else y + part
                y_ref[:, 128 * h:128 * h + 128] = y

        pl.when(i < n // tq)(functools.partial(tile, False))
        pl.when(i >= n // tq)(functools.partial(tile, True))

    return _pallas(
        body, name="ret_fwd", grid=(B, L // tq),
        in_specs=[_tile_spec(tq, 256), _full_spec(L, 256), _full_spec(L, 512), pl.BlockSpec((2, 4), lambda b, i: (0, 0))],
        out_specs=_tile_spec(tq, 512), out_shape=_sds((B, L, 512), F32),
        compiler_params=_params(("parallel", "arbitrary")),
    )(q, k, v, lg)


def _ret_bwd(q, k, v, lg, dy, n, tq):
    B, L, _ = q.shape

    def body(q_ref, k_ref, v_ref, lg_ref, dy_ref, dq_ref, dk_ref, dv_ref, dlg_ref):
        i = pl.program_id(1)

        @pl.when(i == 0)
        def _():
            dk_ref[...] = jnp.zeros_like(dk_ref)
            dv_ref[...] = jnp.zeros_like(dv_ref)
            dlg_ref[...] = jnp.zeros_like(dlg_ref)

        def total(a):
            return jnp.sum(jnp.sum(a, axis=1, keepdims=True), axis=0, keepdims=True)

        def tile(ctx_tile):
            dist = _ret_decay(i, tq, L, n, ctx_tile)
            dq = None
            for h in range(4):
                vc = slice(128 * h, 128 * h + 128)
                qh = _ret_head(q_ref[...], h)
                dyh = dy_ref[:, vc].astype(BF16)
                dqh = None
                for rows, dec_r, wf_r, wb_r in _ret_weights(n, L, *dist, lg_ref[0:1, h:h + 1], lg_ref[1:2, h:h + 1], True):
                    k_all, vh = k_ref[rows, :], v_ref[rows, vc]
                    s = _dot_nt(qh, k_all)
                    gr = _dot_nt(dyh, vh)
                    ds = (gr * dec_r).astype(BF16)
                    part = _dot(ds, k_all)
                    dqh = part if dqh is None else dqh + part
                    dk_ref[rows, :] += _dot_tn(ds, qh)
                    dv_ref[rows, vc] += _dot_tn((s * dec_r).astype(BF16), dyh)
                    gs = gr * s
                    dlg_ref[0:1, h:h + 1] += total(gs * wf_r)
                    dlg_ref[1:2, h:h + 1] += total(gs * wb_r)
                dqh = _ret_head(dqh, h)
                dq = dqh if dq is None else dq + dqh
            dq_ref[...] = dq

        pl.when(i < n // tq)(functools.partial(tile, False))
        pl.when(i >= n // tq)(functools.partial(tile, True))

    return _pallas(
        body, name="ret_bwd", grid=(B, L // tq),
        in_specs=[_tile_spec(tq, 256), _full_spec(L, 256), _full_spec(L, 512), pl.BlockSpec((2, 4), lambda b, i: (0, 0)),
                  _tile_spec(tq, 512)],
        out_specs=[_tile_spec(tq, 256), _full_spec(L, 256), _full_spec(L, 512),
                   pl.BlockSpec((None, 2, 4), lambda b, i: (b, 0, 0))],
        out_shape=[_sds((B, L, 256), F32), _sds((B, L, 256), F32), _sds((B, L, 512), F32), _sds((B, 2, 4), F32)],
        compiler_params=_params(("parallel", "arbitrary")),
    )(q, k, v, lg, dy)


def _loss_head(s, target, g, n, tm):
    B, L, D = s.shape
    nx = n // tm

    def body(s_ref, t_ref, g_ref, ds_ref, dg_ref, loss_ref):
        b, i = pl.program_id(0), pl.program_id(1)

        @pl.when((b == 0) & (i == 0))
        def _():
            dg_ref[...] = jnp.zeros_like(dg_ref)
            loss_ref[...] = jnp.zeros_like(loss_ref)

        @pl.when(i < nx)
        def _():
            y, vjp = jax.vjp(_rms, s_ref[...], g_ref[...])
            err = y - t_ref[...]
            d_s, d_g = vjp(err * (1.0 / D))
            ds_ref[...] = d_s
            dg_ref[...] += d_g
            part = jnp.sum(jnp.sum(err * err, axis=1, keepdims=True), axis=0, keepdims=True) * (0.5 / D)
            loss_ref[...] += jnp.broadcast_to(part, loss_ref.shape)

        @pl.when(i >= nx)
        def _():
            ds_ref[...] = jnp.zeros_like(ds_ref)

    return _pallas(
        body, name="loss_head", grid=(B, L // tm),
        in_specs=[pl.BlockSpec((None, tm, D), lambda b, i: (b, i, 0)),
                  pl.BlockSpec((None, tm, D), lambda b, i: (b, jnp.minimum(i, nx - 1), 0)),
                  pl.BlockSpec((1, D), lambda b, i: (0, 0))],
        out_specs=[pl.BlockSpec((None, tm, D), lambda b, i: (b, i, 0)), pl.BlockSpec((1, D), lambda b, i: (0, 0)),
                   pl.BlockSpec((1, 128), lambda b, i: (0, 0))],
        out_shape=[_sds((B, L, D), F32), _sds((1, D), F32), _sds((1, 128), F32)],
        compiler_params=_params(("arbitrary", "arbitrary")),
    )(s, target, g)


def _ada_fwd(c_all, ada_w, ada_b):
    NL, D, Ns = ada_w.shape
    R = c_all.shape[0]

    def body(c_ref, w_ref, b_ref, o_ref):
        cond = jax.nn.silu(c_ref[...]).astype(BF16)
        o_ref[...] = _dot(cond, w_ref[...].astype(BF16)) + b_ref[...]

    return _pallas(
        body, name="ada_fwd", grid=(NL,),
        in_specs=[pl.BlockSpec((R, D), lambda l: (0, 0)), pl.BlockSpec((None, D, Ns), lambda l: (l, 0, 0)),
                  pl.BlockSpec((None, 1, Ns), lambda l: (l, 0, 0))],
        out_specs=pl.BlockSpec((None, R, Ns), lambda l: (l, 0, 0)), out_shape=_sds((NL, R, Ns), F32),
        compiler_params=_params(("parallel",)),
    )(c_all, ada_w, ada_b)


def _ada_bwd(c_all, ada_w, dmods):
    NL, D, Ns = ada_w.shape
    R = c_all.shape[0]

    def body(c_ref, w_ref, dm_ref, dw_ref, dc_ref):
        cond = jax.nn.silu(c_ref[...]).astype(BF16)
        dm = dm_ref[...].astype(BF16)
        dw_ref[...] = _dot_tn(cond, dm)
        dc_ref[...] = _dot_nt(dm, w_ref[...].astype(BF16))

    return _pallas(
        body, name="ada_bwd", grid=(NL,),
        in_specs=[pl.BlockSpec((R, D), lambda l: (0, 0)), pl.BlockSpec((None, D, Ns), lambda l: (l, 0, 0)),
                  pl.BlockSpec((None, R, Ns), lambda l: (l, 0, 0))],
        out_specs=[pl.BlockSpec((None, D, Ns), lambda l: (l, 0, 0)), pl.BlockSpec((None, R, D), lambda l: (l, 0, 0))],
        out_shape=[_sds((NL, D, Ns), F32), _sds((NL, R, D), F32)],
        compiler_params=_params(("parallel",)),
    )(c_all, ada_w, dmods)


def _my_index():
    return 4 * lax.axis_index("x") + 2 * lax.axis_index("y") + lax.axis_index("c")


def _peer(k):
    x, y, c = lax.axis_index("x"), lax.axis_index("y"), lax.axis_index("c")
    kx, ky, kc = (k >> 2) & 1, (k >> 1) & 1, k & 1
    px, py, pc = (x + kx) % 2, (y + ky) % 2, (c + kc) % 2
    return (px, py, pc), 4 * px + 2 * py + pc


def _all_gather(name, shards):
    na = len(shards)
    hbm = pl.BlockSpec(memory_space=pl.ANY)

    def body(*refs):
        in_refs, out_refs = refs[:na], refs[na:2 * na]
        send_sems, recv_sems, local_sems = refs[2 * na:]
        me = _my_index()
        sib_id, sib = _peer(1)
        chips = [_peer(k) for k in (4, 2, 6)]
        sib_chips = [4 * px + 2 * py + (1 - pc) for (px, py, pc), _ in chips]

        def copy(a, k, slot, to, src=None):
            dst = out_refs[a].at[slot]
            return pltpu.make_async_remote_copy(
                src_ref=dst if src is None else src, dst_ref=dst, send_sem=send_sems.at[a, k],
                recv_sem=recv_sems.at[a, k], device_id=to, device_id_type=MESH)

        first, passed, mine = [], [], []
        for a in range(na):
            cp = pltpu.make_async_copy(in_refs[a], out_refs[a].at[me], local_sems.at[a])
            cp.start()
            mine.append(cp)
            first.append(copy(a, 0, me, sib_id, src=in_refs[a]))
            first += [copy(a, 1 + j, me, pid, src=in_refs[a]) for j, (pid, _) in enumerate(chips)]
        for cp in first:
            cp.start()
        for a in range(na):
            for j, (pid, pidx) in enumerate(chips):
                copy(a, 1 + j, pidx, pid).wait_recv()
                fwd = copy(a, 4 + j, pidx, sib_id)
                fwd.start()
                passed.append(fwd)
        for a in range(na):
            copy(a, 0, sib, sib_id).wait_recv()
            for j in range(3):
                copy(a, 4 + j, sib_chips[j], sib_id).wait_recv()
        for cp in first + passed:
            cp.wait_send()
        for cp in mine:
            cp.wait()

    return _pallas(
        body, name=name, in_specs=[hbm] * na, out_specs=[hbm] * na,
        out_shape=[_sds((N_DEV,) + s.shape, s.dtype) for s in shards],
        scratch_shapes=[pltpu.SemaphoreType.DMA((na, 7)), pltpu.SemaphoreType.DMA((na, 7)),
                        pltpu.SemaphoreType.DMA((na,))],
    )(*shards)


_HBM = pl.BlockSpec(memory_space=pltpu.HBM)
_SEM = pl.BlockSpec(memory_space=pltpu.SEMAPHORE)
_DATAFLOW = pltpu.SideEffectType.DATAFLOW_SIDE_EFFECTING


def _exchange_start(name, srcs, slotted, after=()):
    na = len(srcs)
    lands = [lax.empty((N_DEV,) + (s.shape[1:] if slotted else s.shape), s.dtype) for s in srcs]

    def body(*refs):
        src_refs, land_refs = refs[:na], refs[na:2 * na]
        outs = refs[2 * na + len(after):]
        send_sems, recv_sems, token = outs[:na], outs[na:2 * na], outs[4 * na]
        me = _my_index()
        for a in range(na):
            for k in range(1, N_DEV):
                pid, pidx = _peer(k)
                pltpu.make_async_remote_copy(
                    src_ref=src_refs[a].at[pidx] if slotted else src_refs[a], dst_ref=land_refs[a].at[me],
                    send_sem=send_sems[a], recv_sem=recv_sems[a], device_id=pid, device_id_type=MESH).start()
        token[...] = jnp.zeros_like(token)

    ops = [pltpu.with_memory_space_constraint(a, pltpu.HBM) for a in list(srcs) + lands]
    outs = _pallas(
        body, name=name,
        out_shape=[pltpu.SemaphoreType.DMA(())] * (2 * na) + [pltpu.HBM(a.shape, a.dtype) for a in ops]
        + [_sds((8, 128), F32)],
        in_specs=[_HBM] * (2 * na) + [pl.BlockSpec(memory_space=pl.ANY)] * len(after),
        out_specs=[_SEM] * (2 * na) + [_HBM] * (2 * na) + [pl.BlockSpec(memory_space=pltpu.VMEM)],
        input_output_aliases={a: 2 * na + a for a in range(2 * na)},
        compiler_params=pltpu.CompilerParams(has_side_effects=_DATAFLOW),
    )(*ops, *after)
    return (na, outs[:2 * na], outs[2 * na:4 * na]), outs[4 * na]


def _exchange_wait(name, handle, after):
    na, sems, thru = handle

    def body(*refs):
        land_refs = refs[na:2 * na]
        send_sems, recv_sems = refs[2 * na:3 * na], refs[3 * na:4 * na]
        me_id = (lax.axis_index("x"), lax.axis_index("y"), lax.axis_index("c"))
        for a in range(na):
            seven = land_refs[a].at[pl.ds(0, N_DEV - 1)]
            drain = pltpu.make_async_remote_copy(src_ref=seven, dst_ref=seven, send_sem=send_sems[a],
                                                 recv_sem=recv_sems[a], device_id=me_id, device_id_type=MESH)
            drain.wait_send()
            drain.wait_recv()

    outs = _pallas(
        body, name=name, out_shape=[pltpu.HBM(a.shape, a.dtype) for a in thru],
        in_specs=[_HBM] * (2 * na) + [_SEM] * (2 * na) + [pl.BlockSpec(memory_space=pl.ANY)],
        out_specs=[_HBM] * (2 * na), input_output_aliases={a: a for a in range(2 * na)},
        compiler_params=pltpu.CompilerParams(has_side_effects=_DATAFLOW),
    )(*thru, *sems, after)
    return outs[na:]


def _fill_own(name, landed, owns, slotted):
    me = _my_index()
    return [jnp.where(lax.broadcasted_iota(jnp.int32, land.shape, 0) == me, own, land)
            for land, own in zip(landed, owns)]


def _rope_tables(n, L, width):
    t = jnp.arange(n)
    row = (t // GRID_W).astype(F32)
    col = (t % GRID_W).astype(F32)
    n_freq = ROPE_DIM // 4
    freqs = ROPE_THETA ** (-jnp.arange(n_freq, dtype=F32) / n_freq)
    ang = jnp.concatenate([row[:, None] * freqs, col[:, None] * freqs], axis=-1)
    cos, sin = jnp.cos(ang), jnp.sin(ang)
    cos2 = jnp.concatenate([cos, cos], axis=-1)
    sin2 = jnp.concatenate([-sin, sin], axis=-1)
    cos2 = jnp.concatenate([cos2, jnp.ones((L - n, ROPE_DIM), F32)], axis=0)
    sin2 = jnp.concatenate([sin2, jnp.zeros((L - n, ROPE_DIM), F32)], axis=0)
    reps = width // ROPE_DIM
    return jnp.tile(cos2, (1, reps)), jnp.tile(sin2, (1, reps))


def _ab_perm(w):
    return jnp.concatenate([w[:, 0:256], w[:, 320:1600], w[:, 256:320], w[:, 256:320]], axis=1)


def _ab_unperm(g):
    rope_key = (g[:, 1536:1600].astype(F32) + g[:, 1600:1664].astype(F32)).astype(g.dtype)
    return jnp.concatenate([g[:, 0:256], rope_key, g[:, 256:1536]], axis=1)


def _wq_perm(w):
    return jnp.concatenate([w[:, 192 * h:192 * h + 128] for h in range(4)]
                           + [w[:, 192 * h + 128:192 * h + 192] for h in range(4)], axis=1)


def _wq_unperm(g):
    return jnp.concatenate([g[:, sl] for h in range(4)
                            for sl in (slice(128 * h, 128 * h + 128), slice(512 + 64 * h, 576 + 64 * h))], axis=1)


def _flat(a):
    return a.reshape(-1, a.shape[-1])


def _layer_weights(full, p):
    NL, NE, NO = len(full["ffn_in"]), len(full["ab_in"]), len(full["cd_in"])
    groups = range(4)

    def each(f, mats):
        return [None if w is None else f(w) for w in mats]

    return dict(
        norm_mix=[p["norm_mix"][l][None] for l in range(NL)], norm_ffn=[p["norm_ffn"][l][None] for l in range(NL)],
        norm_final=p["norm_final"][None],
        ffn_in=each(_ffn_interleave, full["ffn_in"]), ffn_out=list(full["ffn_out"]),
        ab_in=each(_ab_perm, full["ab_in"]), ab_out=list(full["ab_out"]),
        wq=each(_wq_perm, full["mla_wq_b"]), wkv=list(full["mla_wkv_b"]),
        kv_norm=[p["mla_kv_norm"][j][None] for j in range(NE)], q_norm=[p["mla_q_norm"][j][None] for j in range(NE)],
        v_norm=[[p["cmlp_v_norm"][j][None, 128 * g:128 * g + 128] for g in groups] for j in range(NE)],
        ws=[[p["cmlp_ws"][j, g] for g in groups] for j in range(NE)],
        bs=[[p["cmlp_bs"][j, g][:, None] for g in groups] for j in range(NE)],
        cd_in=list(full["cd_in"]), cd_out=list(full["cd_out"]),
        lg=[jnp.stack([jax.nn.log_sigmoid(p["ret_decay_fwd"][j]), jax.nn.log_sigmoid(p["ret_decay_bwd"][j])])
            for j in range(NO)],
        sink=[p["swa_sink"][j][None] for j in range(NO)],
        ret_norm=[[p["ret_norm"][j][None, 128 * g:128 * g + 128] for g in groups] for j in range(NO)],
    )


def _local_step(x, ctx, target, mods, W, later_weights=None, early_grads=None):
    B, n, D = x.shape
    m = ctx.shape[1]
    L = n + m
    NL = mods.shape[0]
    tm = min(256, m)
    tq = min(256, m)
    cos512, sin512 = _rope_tables(n, L, 512)
    s = jnp.concatenate([x, ctx], axis=1)
    saved = []

    def rows(name, f, arrays, pieces, samp, samp_pieces, glob, out_arrays, out_pieces, tile=tm):
        return _Rows(name, f, B, L, n, tile, arrays, pieces, samp, samp_pieces, glob, out_arrays, out_pieces)

    def full(width, start=0):
        return [(0, start, width)]

    for l in range(NL):
        j = l // 2
        even = l % 2 == 0
        md = mods[l]
        if l in (1, 2) and later_weights is not None:
            W = later_weights(l, s)
        r = {}
        if l == 0:
            r["pre1"] = rows("pre_mix0", _f_pre, [s], full(D), [md], [(0, 0), (0, 1)], [W["norm_mix"][0]], [(D, BF16)], full(D))
            (xn,) = r["pre1"].fwd()
        r["xn"] = xn
        if even:
            z = _mm(f"ab_in{l}", _flat(xn), W["ab_in"][j]).reshape(B, L, 1664)
            pieces = [(0, 0, 256), (0, 256, 256), (0, 1536, 128)]
            pieces += [(0, 512 + 128 * g, 128) for g in range(4)] + [(0, 1024 + 128 * g, 128) for g in range(4)]
            pieces += [(1, 0, 128), (2, 0, 128)]
            glob = [W["kv_norm"][j], W["q_norm"][j]] + W["v_norm"][j] + W["ws"][j] + W["bs"][j]
            abp = rows(f"ab_pre{l}", _f_ab_pre, [z, cos512, sin512], pieces, [], [], glob,
                       [(256, BF16), (256, BF16), (128, BF16), (512, BF16)],
                       [(0, 0, 256), (1, 0, 256), (2, 0, 128)] + [(3, 128 * g, 128) for g in range(4)], tile=128)
            kvn, qn, kpe, cm = abp.fwd()
            kv = _mm(f"wkv{l}", _flat(kvn), W["wkv"][j], out_dtype=BF16).reshape(B, L, 1024)
            q0 = _mm(f"wq{l}", _flat(qn), W["wq"][j]).reshape(B, L, 768)
            qrp = rows(f"q_rope{l}", _f_q_rope, [q0, cos512, sin512], [(0, 0, 512), (0, 512, 256), (1, 0, 256), (2, 0, 256)],
                       [], [], [], [(768, BF16)], [(0, 0, 512), (0, 512, 256)])
            (q,) = qrp.fwd()
            o, lse = _mla_fwd(q, kv, kpe, n, tq)
            y = _mm_pair(f"ab_out{l}", _flat(o), _flat(cm), W["ab_out"][j], BF16).reshape(B, L, D)
            r.update(z=z, abp=abp, kvn=kvn, qn=qn, kpe=kpe, cm=cm, kv=kv, qrp=qrp, q=q, o=o, lse=lse)
        else:
            z = _mm(f"cd_in{l}", _flat(xn), W["cd_in"][j]).reshape(B, L, 2304)
            pieces = [(0, 0, 256), (0, 256, 512), (0, 768, 128), (0, 896, 128), (0, 1024, 256), (0, 1280, 512), (0, 1792, 512)]
            pieces += [(1, 0, 256), (2, 0, 256), (1, 0, 128), (2, 0, 128), (1, 0, 512), (2, 0, 512)]
            cdp = rows(f"cd_pre{l}", _f_cd_pre, [z, cos512, sin512], pieces, [], [], [],
                       [(256, BF16), (256, BF16), (512, BF16), (512, BF16), (128, BF16), (128, BF16), (512, F32)],
                       [(k_, 0, w_) for k_, w_ in enumerate((256, 256, 512, 512, 128, 128, 512))])
            rq, rk, rv, sq, sk, sv, rg = cdp.fwd()
            yret = _ret_fwd(rq, rk, rv, W["lg"][j], n, tq)
            osw, lse = _swa_fwd(sq, sk, sv, W["sink"][j], n, tq)
            mrg = rows(f"cd_merge{l}", _f_cd_merge, [yret, rg],
                       [(0, 128 * g, 128) for g in range(4)] + [(1, 128 * g, 128) for g in range(4)], [], [],
                       W["ret_norm"][j], [(512, BF16)], [(0, 128 * g, 128) for g in range(4)])
            (yr,) = mrg.fwd()
            y = _mm_pair(f"cd_out{l}", _flat(yr), _flat(osw), W["cd_out"][j], BF16).reshape(B, L, D)
            r.update(z=z, cdp=cdp, rq=rq, rk=rk, rv=rv, sq=sq, sk=sk, sv=sv, rg=rg, yret=yret, osw=osw, lse=lse,
                     mrg=mrg, yr=yr)
        two, outs2 = [(0, 0, D), (1, 0, D)], [(D, F32), (D, BF16)]
        r["mix_out"] = rows(f"mix_res_pre{l}", _f_res_pre, [s, y], two, [md], [(0, 2), (0, 3), (0, 4)],
                            [W["norm_ffn"][l]], outs2, two)
        s1, xn2 = r["mix_out"].fwd()
        if l == 0 and later_weights is not None:
            W = later_weights(0, s1)
        y2 = _ffn_fwd(f"ffn{l}", _flat(xn2), W["ffn_in"][l], W["ffn_out"][l]).reshape(B, L, D)
        if l < NL - 1:
            r["ffn_out"] = rows(f"ffn_res_pre{l}", _f_res_pre, [s1, y2], two, [md, mods[l + 1]], [(0, 5), (1, 0), (1, 1)],
                                [W["norm_mix"][l + 1]], outs2, two)
            s, xn = r["ffn_out"].fwd()
        else:
            r["ffn_out"] = rows(f"res_ffn{l}", _f_res, [s1, y2], two, [md], [(0, 5)], [], [(D, F32)], full(D))
            (s,) = r["ffn_out"].fwd()
        r["xn2"] = xn2
        saved.append(r)

    ds, d_norm_final, loss = _loss_head(s, target, W["norm_final"], n, tm)

    G = {k: [None] * len(v) for k, v in W.items() if isinstance(v, list)}
    G["norm_final"] = d_norm_final
    dm = [[None] * 6 for _ in range(NL)]
    dxn_next = None
    for l in reversed(range(NL)):
        j = l // 2
        even = l % 2 == 0
        r = saved[l]
        zero = early_grads(0, G) if l == 0 and early_grads is not None else None
        if l == NL - 1:
            (ds1, dy2), (dm[l][5],), _ = r["ffn_out"].bwd([ds], {0: F32, 1: BF16}, unread=(0,))
        else:
            smp = None if zero is None else [mods[l] + zero, mods[l + 1] + zero]
            (ds1, dy2), (dm[l][5], dm[l + 1][0], dm[l + 1][1]), (G["norm_mix"][l + 1],) = r["ffn_out"].bwd(
                [ds, dxn_next], {0: F32, 1: BF16}, grad_glob=(0,), samp=smp)
        dy2f, xn2f = _flat(dy2), _flat(r["xn2"])
        hid, dz2 = _ffn_mid_bwd(f"ffn_mid{l}", xn2f, dy2f, W["ffn_in"][l], W["ffn_out"][l])
        G["ffn_out"][l] = _mm(f"g_ffn_out{l}", hid, dy2f, ta=True, out_dtype=BF16)
        G["ffn_in"][l] = _mm(f"g_ffn_in{l}", xn2f, dz2, ta=True, out_dtype=BF16)
        dxn2 = _mm(f"d_xn2{l}", dz2, W["ffn_in"][l], tb=True, out_dtype=BF16).reshape(B, L, D)
        zero = early_grads(1, G) if l == 0 and early_grads is not None else None
        smp = None if zero is None else [mods[l] + zero]
        (ds0, dy), (dm[l][2], dm[l][3], dm[l][4]), (G["norm_ffn"][l],) = r["mix_out"].bwd(
            [ds1, dxn2], {0: F32, 1: BF16}, grad_glob=(0,), samp=smp)
        dyf = _flat(dy)
        if even:
            w_out = W["ab_out"][j]
            dcat = _mm(f"d_cat{l}", dyf, w_out, tb=True, out_dtype=BF16).reshape(B, L, -1)
            G["ab_out"][l // 2] = jnp.concatenate(
                [_mm(f"g_ab_out_a{l}", _flat(r["o"]), dyf, ta=True, out_dtype=BF16),
                 _mm(f"g_ab_out_b{l}", _flat(r["cm"]), dyf, ta=True, out_dtype=BF16)], axis=0)
            dq, dkv, dkpe = _mla_bwd(r["q"], r["kv"], r["kpe"], r["lse"], dcat, n, tq)
            (dq0,), _, _ = r["qrp"].bwd([dq], {0: BF16}, unread=(0,))
            dq0f, dkvf = _flat(dq0), _flat(dkv)
            G["wq"][j] = _mm(f"g_wq{l}", _flat(r["qn"]), dq0f, ta=True, out_dtype=BF16)
            G["wkv"][j] = _mm(f"g_wkv{l}", _flat(r["kvn"]), dkvf, ta=True, out_dtype=BF16)
            dqn = _mm(f"d_qn{l}", dq0f, W["wq"][j], tb=True).reshape(B, L, 256)
            dkvn = _mm(f"d_kvn{l}", dkvf, W["wkv"][j], tb=True).reshape(B, L, 256)
            (dz,), _, gg = r["abp"].bwd([dkvn, dqn, dkpe, (dcat, 512)], {0: BF16}, grad_glob=tuple(range(14)))
            G["kv_norm"][j], G["q_norm"][j] = gg[0], gg[1]
            G["v_norm"][j], G["ws"][j], G["bs"][j] = list(gg[2:6]), list(gg[6:10]), list(gg[10:14])
            w_in, key = W["ab_in"][j], "ab_in"
        else:
            w_out = W["cd_out"][j]
            dcat = _mm(f"d_cat{l}", dyf, w_out, tb=True, out_dtype=BF16).reshape(B, L, -1)
            G["cd_out"][j] = jnp.concatenate(
                [_mm(f"g_cd_out_a{l}", _flat(r["yr"]), dyf, ta=True, out_dtype=BF16),
                 _mm(f"g_cd_out_b{l}", _flat(r["osw"]), dyf, ta=True, out_dtype=BF16)], axis=0)
            (dyret, drg), _, gg = r["mrg"].bwd([(dcat, 0)], {0: F32, 1: F32}, grad_glob=(0, 1, 2, 3))
            G["ret_norm"][j] = list(gg)
            drq, drk, drv, dlg = _ret_bwd(r["rq"], r["rk"], r["rv"], W["lg"][j], dyret, n, tq)
            dsq, dsk, dsv, dsink = _swa_bwd(r["sq"], r["sk"], r["sv"], W["sink"][j], r["lse"], dcat, n, tq)
            G["lg"][j], G["sink"][j] = dlg, dsink
            (dz,), _, _ = r["cdp"].bwd([drq, drk, drv, dsq, dsk, dsv, drg], {0: BF16}, unread=(0,))
            w_in, key = W["cd_in"][j], "cd_in"
        dzf = _flat(dz)
        G[key][j] = _mm(f"g_{key}{l}", _flat(r["xn"]), dzf, ta=True, out_dtype=BF16)
        dxn = _mm(f"d_xn{l}", dzf, w_in, tb=True, out_dtype=BF16).reshape(B, L, D)
        if l == 0 and early_grads is not None:
            smp = [mods[0] + early_grads(2, G)]
        if l == 0:
            (ds,), (dm[0][0], dm[0][1]), (G["norm_mix"][0],) = r["pre1"].bwd([dxn], {0: F32}, grad_glob=(0,),
                                                                            add={0: ds0}, samp=smp)
        else:
            ds, dxn_next = ds0, dxn
    dmods = jnp.stack([jnp.concatenate(d, axis=2) for d in dm])
    return loss, ds[:, :n], dmods, G


def kernel(x, c, ctx, c_ctx, ada_w, ada_b, norm_mix, norm_ffn, norm_final, ffn_in, ffn_out, ab_in, ab_out, mla_q_norm, mla_kv_norm, mla_wq_b, mla_wkv_b, cmlp_v_norm, cmlp_ws, cmlp_bs, cd_in, cd_out, ret_decay_fwd, ret_decay_bwd, ret_norm, swa_sink, loss_target, m_c_ctx, m_ada_w, m_ada_b, m_norm_mix, m_norm_ffn, m_norm_final, m_ffn_in, m_ffn_out, m_ab_in, m_ab_out, m_mla_q_norm, m_mla_kv_norm, m_mla_wq_b, m_mla_wkv_b, m_cmlp_v_norm, m_cmlp_ws, m_cmlp_bs, m_cd_in, m_cd_out, m_ret_decay_fwd, m_ret_decay_bwd, m_ret_norm, m_swa_sink, v_c_ctx, v_ada_w, v_ada_b, v_norm_mix, v_norm_ffn, v_norm_final, v_ffn_in, v_ffn_out, v_ab_in, v_ab_out, v_mla_q_norm, v_mla_kv_norm, v_mla_wq_b, v_mla_wkv_b, v_cmlp_v_norm, v_cmlp_ws, v_cmlp_bs, v_cd_in, v_cd_out, v_ret_decay_fwd, v_ret_decay_bwd, v_ret_norm, v_swa_sink):
    B, n, D = x.shape
    NL = ada_w.shape[0]
    NE, NO = ab_in.shape[0], cd_in.shape[0]
    me = _my_index()
    weights = dict(c_ctx=c_ctx, ada_w=ada_w, ada_b=ada_b, norm_mix=norm_mix, norm_ffn=norm_ffn, norm_final=norm_final,
                   ffn_in=ffn_in, ffn_out=ffn_out, ab_in=ab_in, ab_out=ab_out, mla_q_norm=mla_q_norm,
                   mla_kv_norm=mla_kv_norm, mla_wq_b=mla_wq_b, mla_wkv_b=mla_wkv_b, cmlp_v_norm=cmlp_v_norm,
                   cmlp_ws=cmlp_ws, cmlp_bs=cmlp_bs, cd_in=cd_in, cd_out=cd_out, ret_decay_fwd=ret_decay_fwd,
                   ret_decay_bwd=ret_decay_bwd, ret_norm=ret_norm, swa_sink=swa_sink)
    moments_m = dict(c_ctx=m_c_ctx, ada_w=m_ada_w, ada_b=m_ada_b, norm_mix=m_norm_mix, norm_ffn=m_norm_ffn,
                     norm_final=m_norm_final, ffn_in=m_ffn_in, ffn_out=m_ffn_out, ab_in=m_ab_in, ab_out=m_ab_out,
                     mla_q_norm=m_mla_q_norm, mla_kv_norm=m_mla_kv_norm, mla_wq_b=m_mla_wq_b, mla_wkv_b=m_mla_wkv_b,
                     cmlp_v_norm=m_cmlp_v_norm, cmlp_ws=m_cmlp_ws, cmlp_bs=m_cmlp_bs, cd_in=m_cd_in, cd_out=m_cd_out,
                     ret_decay_fwd=m_ret_decay_fwd, ret_decay_bwd=m_ret_decay_bwd, ret_norm=m_ret_norm,
                     swa_sink=m_swa_sink)
    moments_v = dict(c_ctx=v_c_ctx, ada_w=v_ada_w, ada_b=v_ada_b, norm_mix=v_norm_mix, norm_ffn=v_norm_ffn,
                     norm_final=v_norm_final, ffn_in=v_ffn_in, ffn_out=v_ffn_out, ab_in=v_ab_in, ab_out=v_ab_out,
                     mla_q_norm=v_mla_q_norm, mla_kv_norm=v_mla_kv_norm, mla_wq_b=v_mla_wq_b, mla_wkv_b=v_mla_wkv_b,
                     cmlp_v_norm=v_cmlp_v_norm, cmlp_ws=v_cmlp_ws, cmlp_bs=v_cmlp_bs, cd_in=v_cd_in, cd_out=v_cd_out,
                     ret_decay_fwd=v_ret_decay_fwd, ret_decay_bwd=v_ret_decay_bwd, ret_norm=v_ret_norm,
                     swa_sink=v_swa_sink)
    order = list(weights)

    Ns = ada_w.shape[2]
    (c_g,) = _all_gather("gather_c", [c])
    R = N_DEV * B + 8
    c_all = jnp.concatenate([c_g.reshape(N_DEV * B, D), jnp.broadcast_to(c_ctx[None], (8, D))], axis=0)
    ada_b_mine = lax.dynamic_slice_in_dim(ada_b, me * Ns, Ns, axis=1)[:, None, :]
    mods_shard = _ada_fwd(c_all, ada_w, ada_b_mine)
    (mods_g,) = _all_gather("gather_mods", [mods_shard])
    mods_full = jnp.transpose(mods_g, (1, 2, 0, 3)).reshape(NL, R, 6, D)
    mx = lax.dynamic_slice_in_dim(mods_full, me * B, B, axis=1)
    mh = jnp.broadcast_to(mods_full[:, N_DEV * B][:, None], (NL, B, 6, D))
    mods = jnp.stack([mx, mh], axis=2)

    big = ["ffn_in", "ffn_out", "ab_in", "ab_out", "cd_in", "cd_out", "mla_wq_b", "mla_wkv_b"]
    col_sharded = {"ffn_in", "ab_in", "cd_in", "mla_wq_b", "mla_wkv_b"}
    shards = {k: _to_bf16("cast_" + k, weights[k]) for k in big}
    first = {k: 0 if k.startswith("cd_") else 1 for k in big}
    a_keys = [k for k in big if first[k] and not k.startswith("ffn_")]
    b_keys = ["ffn_in", "ffn_out"]
    early = _all_gather("gather_wA", [shards[k][:1] for k in a_keys])
    (rn_g,) = _all_gather("gather_ret_norm", [ret_norm])
    small_first = [mods, rn_g] + list(early)
    w_groups = [
        ("B", {k: (0, 1) for k in b_keys}),
        ("C1", {"ffn_in": (1, 2), "ffn_out": (1, 2), "cd_in": (0, 1), "cd_out": (0, 1)}),
        ("C2", {k: (2, NL) if k.startswith("ffn_") else (1, weights[k].shape[0]) for k in big}),
    ]
    w_handles, w_after = [], small_first
    for gname, members in w_groups:
        handle, token = _exchange_start(f"gather_w{gname}_start", [shards[k][lo:hi] for k, (lo, hi) in members.items()],
                                        False, w_after)
        w_handles.append(handle)
        w_after = w_after + [token]
        mods = mods + token[0, 0]

    def unshard(k, g):
        if k in col_sharded:
            f = jnp.transpose(g, (1, 2, 0, 3)).reshape(g.shape[1], g.shape[2], -1)
        else:
            f = jnp.transpose(g, (1, 0, 2, 3)).reshape(g.shape[1], -1, g.shape[3])
        return [f[i] for i in range(f.shape[0])]

    rn_full = jnp.transpose(rn_g, (1, 0, 2)).reshape(NO, -1)
    small_p = dict(weights, ret_norm=rn_full)
    full0 = {k: [None] * weights[k].shape[0] for k in big}
    for k, g in zip(a_keys, early):
        full0[k][:1] = unshard(k, g)

    def later_weights(stage, newest):
        gname, members = w_groups[stage]
        landed = _exchange_wait(f"gather_w{gname}_wait", w_handles[stage], newest)
        landed = _fill_own(f"own_w{gname}", landed, [shards[k][lo:hi] for k, (lo, hi) in members.items()], False)
        for (k, (lo, hi)), land in zip(members.items(), landed):
            full0[k][lo:hi] = unshard(k, land)
        return _layer_weights(full0, small_p)

    def to_slots(k, gl):
        g = jnp.stack(gl)
        if k == "ffn_in":
            half = N_DEV // 2
            g = g.reshape(g.shape[0], 2, g.shape[2], half, -1)
            return jnp.transpose(g, (1, 3, 0, 2, 4)).reshape(N_DEV, g.shape[0], g.shape[2], -1)
        if k in col_sharded:
            return jnp.transpose(g.reshape(g.shape[0], g.shape[1], N_DEV, -1), (2, 0, 1, 3))
        return jnp.transpose(g.reshape(g.shape[0], N_DEV, -1, g.shape[2]), (1, 0, 2, 3))

    def big_grads(G):
        return dict(ffn_in=[g if g is None else _ffn_deinterleave(g) for g in G["ffn_in"]], ffn_out=G["ffn_out"],
                    ab_in=[g if g is None else _ab_unperm(g) for g in G["ab_in"]],
                    ab_out=G["ab_out"], cd_in=G["cd_in"], cd_out=G["cd_out"],
                    mla_wq_b=[g if g is None else _wq_unperm(g) for g in G["wq"]], mla_wkv_b=G["wkv"])

    sent = {}

    def early_grads(stage, G):
        parts = big_grads(G)
        if stage == 0:
            srcs = [to_slots(k, parts[k][first[k]:]) for k in big]
            handle, g_token = _exchange_start("scatter_gC_start", srcs, slotted=True)
        elif stage == 1:
            srcs = [to_slots(k, parts[k][:1]) for k in b_keys]
            handle, g_token = _exchange_start("scatter_gB_start", srcs, slotted=True)
        else:
            srcs = [to_slots(k, parts[k][:1]) for k in a_keys]
            handle, g_token = _exchange_start("scatter_gA_start", srcs, slotted=True)
        sent[stage] = (handle, srcs)
        return g_token[0, 0]

    loss_part, grad_x, dmods, G = _local_step(x, ctx, loss_target, mods, _layer_weights(full0, small_p),
                                              later_weights, early_grads)

    dmx = dmods[:, :, 0].reshape(NL, B, 6 * D)
    dmh = jnp.sum(dmods[:, :, 1], axis=1).reshape(NL, 1, 6 * D)
    (dm_g,) = _all_gather("gather_dmods", [jnp.concatenate([dmx, dmh], axis=1)])
    dmx_all = jnp.transpose(dm_g[:, :, :B], (1, 0, 2, 3)).reshape(NL, N_DEV * B, 6 * D)
    dmh_all = jnp.sum(dm_g[:, :, B], axis=0)
    dm_rows = jnp.concatenate([dmx_all, dmh_all[:, None], jnp.zeros((NL, 7, 6 * D), F32)], axis=1)
    g_ada_b = jnp.sum(dm_rows, axis=1)
    dm_mine = lax.dynamic_slice_in_dim(dm_rows, me * Ns, Ns, axis=2)
    g_ada_w, dcond = _ada_bwd(c_all, ada_w, dm_mine)
    sg = jax.nn.sigmoid(c_ctx)
    d_c_ctx_part = jnp.sum(dcond[:, N_DEV * B], axis=0) * (sg * (1.0 + c_ctx * (1.0 - sg)))

    def cat(parts):
        return jnp.concatenate([p.reshape(-1) for p in parts])

    dlg = jnp.stack([jnp.sum(G["lg"][j], axis=0) for j in range(NO)])
    sig_f, sig_b = jax.nn.sigmoid(-ret_decay_fwd), jax.nn.sigmoid(-ret_decay_bwd)
    small = dict(
        loss=loss_part[0, 0:1],
        c_ctx=d_c_ctx_part,
        norm_mix=cat(G["norm_mix"]), norm_ffn=cat(G["norm_ffn"]), norm_final=G["norm_final"].reshape(-1),
        mla_q_norm=cat(G["q_norm"]), mla_kv_norm=cat(G["kv_norm"]),
        cmlp_v_norm=cat([cat(G["v_norm"][j]) for j in range(NE)]),
        cmlp_ws=cat([jnp.stack(G["ws"][j]) for j in range(NE)]),
        cmlp_bs=cat([jnp.stack([b_[:, 0] for b_ in G["bs"][j]]) for j in range(NE)]),
        ret_decay_fwd=(dlg[:, 0] * sig_f).reshape(-1), ret_decay_bwd=(dlg[:, 1] * sig_b).reshape(-1),
        ret_norm=cat([cat(G["ret_norm"][j]) for j in range(NO)]),
        swa_sink=cat([jnp.sum(G["sink"][j], axis=0) for j in range(NO)]),
    )
    small_keys = list(small)
    sizes = [small[k].shape[0] for k in small_keys]
    total = sum(sizes)
    padded = -(-total // 2048) * 2048
    packed = jnp.concatenate([small[k] for k in small_keys] + [jnp.zeros((padded - total,), F32)]).reshape(-1, 128)
    (small_g,) = _all_gather("gather_small", [packed])

    landed_a = _fill_own("own_gA", _exchange_wait("scatter_gA_wait", sent[2][0], small_g), sent[2][1], True)
    landed0 = dict(zip(a_keys, landed_a))
    landed_b = _fill_own("own_gB", _exchange_wait("scatter_gB_wait", sent[1][0], grad_x), sent[1][1], True)
    landed0.update(zip(b_keys, landed_b))
    landed_c = _fill_own("own_gC", _exchange_wait("scatter_gC_wait", sent[0][0], grad_x), sent[0][1], True)
    landed = [[landed0[k], rest] if first[k] else [rest] for k, rest in zip(big, landed_c)]

    grads, deltas, new_m, new_v = {}, {}, {}, {}
    for k, land in zip(big, landed):
        grads[k], deltas[k], new_m[k], new_v[k] = _adamw_from_slots("adamw_" + k, weights[k], moments_m[k], moments_v[k], land)
    deltas["ada_w"], new_m["ada_w"], new_v["ada_w"] = [
        o.reshape(ada_w.shape) for o in _adamw("adamw_ada_w", _flat(ada_w), _flat(g_ada_w), _flat(m_ada_w), _flat(v_ada_w))]
    grads["ada_w"] = g_ada_w

    sums_only = ("loss", "ret_norm")

    def packed_of(src, fill):
        vals = [jnp.full((sizes[i],), fill, F32) if k in sums_only else src[k].reshape(-1)
                for i, k in enumerate(small_keys)]
        return jnp.concatenate(vals + [jnp.full((padded - total,), fill, F32)]).reshape(-1, 128)

    w_p, m_p, v_p = packed_of(weights, 0.0), packed_of(moments_m, 0.0), packed_of(moments_v, 1.0)

    def f_small(w_, m_, v_, land_):
        g = _sum_slots(land_)
        return (g,) + _adamw_math(w_, g, m_, v_)

    g_p, d_p, nm_p, nv_p = _ew("adamw_small", f_small, [w_p, m_p, v_p, small_g], [F32] * 4)
    offs = np.cumsum([0] + sizes)
    for i, k in enumerate(small_keys):
        sl = slice(int(offs[i]), int(offs[i + 1]))
        if k == "loss":
            loss = g_p.reshape(-1)[int(offs[i])]
        elif k == "ret_norm":
            g_full = g_p.reshape(-1)[sl].reshape(NO, -1)
            g_mine = lax.dynamic_slice_in_dim(g_full, me * ret_norm.shape[1], ret_norm.shape[1], axis=1)
            d_, m_, v_ = _adamw("adamw_ret_norm", *[jnp.pad(a, ((0, 8 - NO), (0, 128 - a.shape[1])), constant_values=cv)
                                                     for a, cv in ((ret_norm, 0.0), (g_mine, 0.0), (m_ret_norm, 0.0), (v_ret_norm, 1.0))])
            grads[k] = g_mine
            deltas[k], new_m[k], new_v[k] = [a[:NO, :ret_norm.shape[1]] for a in (d_, m_, v_)]
        else:
            shp = weights[k].shape
            grads[k], deltas[k], new_m[k], new_v[k] = [a.reshape(-1)[sl].reshape(shp) for a in (g_p, d_p, nm_p, nv_p)]
    pad_b = lambda a, cv=0.0: jnp.pad(a, ((0, 8 - NL), (0, 0)), constant_values=cv)
    d_, m_, v_ = _adamw("adamw_ada_b", pad_b(ada_b), pad_b(g_ada_b), pad_b(m_ada_b), pad_b(v_ada_b, 1.0))
    grads["ada_b"] = g_ada_b
    deltas["ada_b"], new_m["ada_b"], new_v["ada_b"] = d_[:NL], m_[:NL], v_[:NL]

    return (loss, grad_x, *[grads[k] for k in order], *[deltas[k] for k in order],
            *[new_m[k] for k in order], *[new_v[k] for k in order])
```

```python
import functools

import numpy as np
import jax
import jax.numpy as jnp
from jax import lax
from jax.experimental import pallas as pl
from jax.experimental.pallas import tpu as pltpu

F32 = jnp.float32
BF16 = jnp.bfloat16
EPS = 1e-6
NEG_INF = -1e30
N_DEV = 8
GRID_W = 64
ROPE_THETA = 10000.0
ROPE_DIM = 64
SWA_WINDOW = 128
MLA_SCALE = (128 + 64) ** -0.5
SWA_SCALE = 64 ** -0.5
RET_K_SCALE = 64 ** -0.5
ADAM_LR, ADAM_B1, ADAM_B2, ADAM_EPS, ADAM_WD, ADAM_STEP = 0.001, 0.9, 0.999, 1e-08, 0.01, 10
V7X_VMEM_LIMIT = 56 * 1024 * 1024
ROW_BLOCK_BUDGET = 24 * 1024 * 1024
MESH = pl.DeviceIdType.MESH


def _pallas(body, **kw):
    return pl.pallas_call(body, **kw)


def _params(sem=None):
    return pltpu.CompilerParams(dimension_semantics=sem, vmem_limit_bytes=V7X_VMEM_LIMIT)


def _tile(n, cap, align):
    best = None
    for t in range(align, min(n, cap) + 1, align):
        if n % t == 0:
            best = t
    return n if best is None else best


def _sds(shape, dtype):
    return jax.ShapeDtypeStruct(tuple(shape), dtype)


def _ew(name, f, ins, out_dtypes, cap_elems=131072):
    R, C = ins[0].shape[-2:]
    tr = _tile(R, max(16, cap_elems // C), 16)
    n_in = len(ins)

    def spec(a):
        if a.ndim == 2:
            return pl.BlockSpec((tr, C), lambda i: (i, 0))
        return pl.BlockSpec((a.shape[0], tr, C), lambda i: (0, i, 0))

    def body(*refs):
        outs = f(*[r[...] for r in refs[:n_in]])
        for r, o in zip(refs[n_in:], outs):
            r[...] = o.astype(r.dtype)

    return _pallas(
        body, name=name, grid=(R // tr,), in_specs=[spec(a) for a in ins],
        out_specs=[pl.BlockSpec((tr, C), lambda i: (i, 0)) for _ in out_dtypes],
        out_shape=[_sds((R, C), d) for d in out_dtypes], compiler_params=_params(("parallel",)),
    )(*ins)


def _to_bf16(name, w):
    w2 = w.reshape(-1, w.shape[-1])
    return _ew(name, lambda v: (v,), [w2], [BF16])[0].reshape(w.shape)


def _adamw_math(w, g, m, v):
    m = ADAM_B1 * m + (1.0 - ADAM_B1) * g
    v = ADAM_B2 * v + (1.0 - ADAM_B2) * (g * g)
    m_hat = m / (1.0 - ADAM_B1 ** ADAM_STEP)
    v_hat = v / (1.0 - ADAM_B2 ** ADAM_STEP)
    delta = -ADAM_LR * (m_hat / (jnp.sqrt(v_hat) + ADAM_EPS) + ADAM_WD * w)
    return delta, m, v


def _sum_slots(land):
    g = land[0].astype(F32)
    for s in range(1, land.shape[0]):
        g = g + land[s].astype(F32)
    return g


def _adamw_from_slots(name, w, m, v, lands):
    nl, K, C = w.shape
    tr = _tile(K, max(16, 65536 // C), 16)
    per = K // tr
    starts = np.cumsum([0] + [ld.shape[1] * per for ld in lands])
    ng = len(lands)

    def land_spec(g):
        lo, hi = int(starts[g]), int(starts[g + 1])
        return pl.BlockSpec((N_DEV, tr, C), lambda t: (0, jnp.clip(t, lo, hi - 1) - lo, 0))

    def body(*refs):
        w_ref, m_ref, v_ref = refs[:3]
        land_refs, out_refs = refs[3:3 + ng], refs[3 + ng:]
        t = pl.program_id(0)
        for g in range(ng):
            @pl.when((t >= int(starts[g])) & (t < int(starts[g + 1])))
            def _(g=g):
                grad = _sum_slots(land_refs[g][...])
                for r, o in zip(out_refs, (grad,) + _adamw_math(w_ref[...], grad, m_ref[...], v_ref[...])):
                    r[...] = o

    row = pl.BlockSpec((tr, C), lambda t: (t, 0))
    outs = _pallas(
        body, name=name, grid=(nl * per,), in_specs=[row] * 3 + [land_spec(g) for g in range(ng)],
        out_specs=[row] * 4, out_shape=[_sds((nl * K, C), F32)] * 4, compiler_params=_params(("parallel",)),
    )(w.reshape(-1, C), m.reshape(-1, C), v.reshape(-1, C), *[ld.reshape(N_DEV, -1, C) for ld in lands])
    return [o.reshape(w.shape) for o in outs]


def _adamw(name, w, g, m, v):
    outs = _ew(name, lambda w_, g_, m_, v_: _adamw_math(w_, g_, m_, v_), [w, g, m, v], [F32] * 3)
    return outs


def _mm(name, a, b, ta=False, tb=False, out_dtype=F32, add=None):
    M, K = (a.shape[1], a.shape[0]) if ta else a.shape
    N = b.shape[0] if tb else b.shape[1]
    tm = _tile(M, 1408, 128)
    tn = _tile(N, 1024, 128)
    if tn < 256 and N <= 2432:
        tn = N
    tk = _tile(K, 3072, 128)
    nk = K // tk
    a_spec = pl.BlockSpec((tk, tm), lambda i, j, k: (k, i)) if ta else pl.BlockSpec((tm, tk), lambda i, j, k: (i, k))
    b_spec = pl.BlockSpec((tn, tk), lambda i, j, k: (j, k)) if tb else pl.BlockSpec((tk, tn), lambda i, j, k: (k, j))
    o_spec = pl.BlockSpec((tm, tn), lambda i, j, k: (i, j))
    dims = (((0 if ta else 1,), (1 if tb else 0,)), ((), ()))
    has_add = add is not None

    def product(a_ref, b_ref):
        return lax.dot_general(a_ref[...].astype(BF16), b_ref[...].astype(BF16), dims, preferred_element_type=F32)

    def body_single(*refs):
        acc = product(refs[0], refs[1])
        if has_add:
            acc = acc + refs[2][...]
        refs[-1][...] = acc.astype(refs[-1].dtype)

    def body(*refs):
        a_ref, b_ref = refs[0], refs[1]
        add_ref = refs[2] if has_add else None
        o_ref, acc = refs[-2], refs[-1]
        k = pl.program_id(2)

        @pl.when(k == 0)
        def _():
            acc[...] = add_ref[...] if has_add else jnp.zeros_like(acc)

        acc[...] += product(a_ref, b_ref)

        @pl.when(k == nk - 1)
        def _():
            o_ref[...] = acc[...].astype(o_ref.dtype)

    ins = [a, b] + ([add] if has_add else [])
    specs = [a_spec, b_spec] + ([o_spec] if has_add else [])
    return _pallas(
        body_single if nk == 1 else body, name=name, grid=(M // tm, N // tn, nk), in_specs=specs, out_specs=o_spec,
        out_shape=_sds((M, N), out_dtype), scratch_shapes=[] if nk == 1 else [pltpu.VMEM((tm, tn), F32)],
        compiler_params=_params(("parallel", "parallel", "arbitrary")),
    )(*ins)


def _mm_pair(name, a1, a2, b, out_dtype):
    M, Kh = a1.shape
    N = b.shape[1]
    tm = _tile(M, 1408, 128)

    def body(a1_ref, a2_ref, b1_ref, b2_ref, o_ref):
        o_ref[...] = (_dot(a1_ref[...].astype(BF16), b1_ref[...].astype(BF16))
                      + _dot(a2_ref[...].astype(BF16), b2_ref[...].astype(BF16))).astype(o_ref.dtype)

    a_spec = pl.BlockSpec((tm, Kh), lambda i: (i, 0))
    return _pallas(
        body, name=name, grid=(M // tm,),
        in_specs=[a_spec, a_spec, pl.BlockSpec((Kh, N), lambda i: (0, 0)), pl.BlockSpec((Kh, N), lambda i: (1, 0))],
        out_specs=pl.BlockSpec((tm, N), lambda i: (i, 0)), out_shape=_sds((M, N), out_dtype),
        compiler_params=_params(("parallel",)),
    )(a1, a2, b, b)


def _ffn_tile(F):
    return F // 2 if (F // 2) % 128 == 0 else F


def _ffn_interleave(w):
    D, F2 = w.shape
    T = _ffn_tile(F2 // 2)
    nj = F2 // (2 * T)

    def body(a_ref, b_ref, o_ref):
        o_ref[:, :T] = a_ref[...]
        o_ref[:, T:] = b_ref[...]

    return _pallas(
        body, name="ffn_interleave", grid=(nj,),
        in_specs=[pl.BlockSpec((D, T), lambda j: (0, j)), pl.BlockSpec((D, T), lambda j: (0, j + nj))],
        out_specs=pl.BlockSpec((D, 2 * T), lambda j: (0, j)), out_shape=_sds((D, F2), w.dtype),
        compiler_params=_params(("parallel",)),
    )(w, w)


def _ffn_deinterleave(w):
    D, F2 = w.shape
    T = _ffn_tile(F2 // 2)
    nj = F2 // (2 * T)

    def body(w_ref, o_ref):
        o_ref[0] = w_ref[:, :T]
        o_ref[1] = w_ref[:, T:]

    return _pallas(
        body, name="ffn_deinterleave", grid=(nj,), in_specs=[pl.BlockSpec((D, 2 * T), lambda j: (0, j))],
        out_specs=pl.BlockSpec((2, D, T), lambda j: (0, 0, j)), out_shape=_sds((2, D, F2 // 2), w.dtype),
        compiler_params=_params(("parallel",)),
    )(w)


def _ffn_specs(M, D, F):
    T = _ffn_tile(F)
    tm = _tile(M, 512, 128)
    x_spec = pl.BlockSpec((tm, D), lambda i, j: (i, 0))
    wi_spec = pl.BlockSpec((D, 2 * T), lambda i, j: (0, j))
    wo_spec = pl.BlockSpec((T, D), lambda i, j: (j, 0))
    return T, tm, F // T, x_spec, wi_spec, wo_spec


def _ffn_fwd(name, xn, w_in, w_out):
    M, D = xn.shape
    T, tm, nj, x_spec, wi_spec, wo_spec = _ffn_specs(M, D, w_out.shape[0])

    def body(x_ref, wi_ref, wo_ref, y_ref, acc):
        j = pl.program_id(1)
        z = _dot(x_ref[...], wi_ref[...])
        hid = (jax.nn.silu(z[:, :T]) * z[:, T:]).astype(BF16)
        part = _dot(hid, wo_ref[...])

        @pl.when(j == 0)
        def _():
            acc[...] = part

        @pl.when(j > 0)
        def _():
            acc[...] += part

        @pl.when(j == nj - 1)
        def _():
            y_ref[...] = acc[...].astype(y_ref.dtype)

    return _pallas(
        body, name=name, grid=(M // tm, nj), in_specs=[x_spec, wi_spec, wo_spec], out_specs=x_spec,
        out_shape=_sds((M, D), BF16), scratch_shapes=[pltpu.VMEM((tm, D), F32)],
        compiler_params=_params(("parallel", "arbitrary")),
    )(xn, w_in, w_out)


def _ffn_mid_bwd(name, xn, dy, w_in, w_out):
    M, D = xn.shape
    F = w_out.shape[0]
    T, tm, nj, x_spec, wi_spec, wo_spec = _ffn_specs(M, D, F)

    def body(x_ref, dy_ref, wi_ref, wo_ref, h_ref, dz_ref):
        z = _dot(x_ref[...], wi_ref[...])
        a, b = z[:, :T], z[:, T:]
        dh = _dot_nt(dy_ref[...], wo_ref[...])
        sig = jax.nn.sigmoid(a)
        act = a * sig
        h_ref[...] = (act * b).astype(BF16)
        dz_ref[:, :T] = (dh * b * (sig * (1.0 + a * (1.0 - sig)))).astype(BF16)
        dz_ref[:, T:] = (dh * act).astype(BF16)

    return _pallas(
        body, name=name, grid=(M // tm, nj), in_specs=[x_spec, x_spec, wi_spec, wo_spec],
        out_specs=[pl.BlockSpec((tm, T), lambda i, j: (i, j)),
                   pl.BlockSpec((tm, 2 * T), lambda i, j: (i, j))],
        out_shape=[_sds((M, F), BF16), _sds((M, 2 * F), BF16)],
        compiler_params=_params(("parallel", "parallel")),
    )(xn, dy, w_in, w_out)


class _Rows:
    def __init__(self, name, f, B, L, n, tm, arrays, pieces, samp, samp_pieces, glob, out_arrays, out_pieces):
        self.name, self.f, self.B, self.L, self.n, self.tm = name, f, B, L, n, tm
        self.arrays, self.pieces, self.samp, self.samp_pieces, self.glob = arrays, pieces, samp, samp_pieces, glob
        self.out_arrays, self.out_pieces = out_arrays, out_pieces
        self.nx = n // tm

    def _samples_per_step(self, row_bytes):
        for bb in (4, 2, 1):
            if self.B % bb == 0 and 2 * bb * self.tm * row_bytes <= ROW_BLOCK_BUDGET:
                return bb
        return 1

    def _row_spec(self, bb, C, batched=True):
        tm = self.tm
        if batched:
            return pl.BlockSpec((bb, tm, C), lambda g, i: (g, i, 0))
        return pl.BlockSpec((tm, C), lambda g, i: (i, 0))

    def _in_specs(self, bb, unread=()):
        nx, tm = self.nx, self.tm
        specs = [self._row_spec(bb, a.shape[-1], a.ndim == 3) for a in self.arrays]
        for ai in unread:
            specs[ai] = pl.BlockSpec((bb, tm, self.arrays[ai].shape[-1]), lambda g, i: (0, 0, 0))
        specs += [pl.BlockSpec((bb, None) + s.shape[2:], lambda g, i: (g, i // nx, 0, 0)) for s in self.samp]
        specs += [pl.BlockSpec(g.shape, lambda g_, i, nd=g.ndim: (0,) * nd) for g in self.glob]
        return specs

    def _load(self, b, a_refs, s_refs, g_refs):
        args = [(a_refs[ai][b, :, cs:cs + cw] if self.arrays[ai].ndim == 3 else a_refs[ai][:, cs:cs + cw]).astype(F32)
                for ai, cs, cw in self.pieces]
        args += [s_refs[si][b, r:r + 1, :].astype(F32) for si, r in self.samp_pieces]
        args += [g[...].astype(F32) for g in g_refs]
        return args

    @staticmethod
    def _row_bytes(arrays):
        return sum(a.shape[-1] * a.dtype.itemsize for a in arrays if a.ndim == 3)

    def fwd(self):
        na, ns, ng = len(self.arrays), len(self.samp), len(self.glob)
        bb = self._samples_per_step(self._row_bytes(self.arrays)
                                    + sum(C * jnp.dtype(d).itemsize for C, d in self.out_arrays))

        def body(*refs):
            a_refs, s_refs, g_refs = refs[:na], refs[na:na + ns], refs[na + ns:na + ns + ng]
            o_refs = refs[na + ns + ng:]
            for b in range(bb):
                outs = self.f(*self._load(b, a_refs, s_refs, g_refs))
                for (oi, cs, cw), o in zip(self.out_pieces, outs):
                    o_refs[oi][b, :, cs:cs + cw] = o.astype(o_refs[oi].dtype)

        return _pallas(
            body, name=self.name + "_fwd", grid=(self.B // bb, self.L // self.tm), in_specs=self._in_specs(bb),
            out_specs=[self._row_spec(bb, C) for C, _ in self.out_arrays],
            out_shape=[_sds((self.B, self.L, C), d) for C, d in self.out_arrays],
            compiler_params=_params(("parallel", "parallel")),
        )(*self.arrays, *self.samp, *self.glob)

    def bwd(self, cts, grad_arrays, grad_glob=(), add=None, samp=None, unread=()):
        samp = self.samp if samp is None else samp
        na, ns, ng, nc = len(self.arrays), len(self.samp), len(self.glob), len(cts)
        ct_off = [c[1] if isinstance(c, tuple) else 0 for c in cts]
        cts = [c[0] if isinstance(c, tuple) else c for c in cts]
        add = add or {}
        add_keys = list(add)
        g_idx = list(grad_arrays)
        nx, B = self.nx, self.B
        n_in = na + ns + ng + nc + len(add_keys)
        n_pieces, n_sp = len(self.pieces), len(self.samp_pieces)
        streamed = list(self.arrays) + cts + [add[k] for k in add_keys]
        bb = self._samples_per_step(self._row_bytes(streamed) + sum(
            self.arrays[ai].shape[-1] * jnp.dtype(grad_arrays[ai]).itemsize for ai in g_idx))

        def body(*refs):
            a_refs, s_refs, g_refs = refs[:na], refs[na:na + ns], refs[na + ns:na + ns + ng]
            c_refs = refs[na + ns + ng:na + ns + ng + nc]
            add_refs = refs[na + ns + ng + nc:n_in]
            d_refs = refs[n_in:n_in + len(g_idx)]
            ds_refs = refs[n_in + len(g_idx):n_in + len(g_idx) + n_sp]
            dg_refs = refs[n_in + len(g_idx) + n_sp:]
            g_step, i = pl.program_id(0), pl.program_id(1)

            @pl.when((i == 0) | (i == nx))
            def _():
                for r in ds_refs:
                    r[...] = jnp.zeros_like(r)

            @pl.when((g_step == 0) & (i == 0))
            def _():
                for r in dg_refs:
                    r[...] = jnp.zeros_like(r)

            for b in range(bb):
                args = self._load(b, a_refs, s_refs, g_refs)
                _, vjp = jax.vjp(lambda *xs: tuple(self.f(*xs)), *args)
                grads = vjp(tuple(c_refs[oi][b, :, ct_off[oi] + cs:ct_off[oi] + cs + cw].astype(F32)
                                  for oi, cs, cw in self.out_pieces))
                for k, ai in enumerate(g_idx):
                    covered = sum(cw for pa, _, cw in self.pieces if pa == ai)
                    if covered < self.arrays[ai].shape[-1]:
                        d_refs[k][b] = jnp.zeros(d_refs[k].shape[1:], d_refs[k].dtype)
                    for (pa, cs, cw), gr in zip(self.pieces, grads[:n_pieces]):
                        if pa == ai:
                            if ai in add:
                                gr = gr + add_refs[add_keys.index(ai)][b, :, cs:cs + cw]
                            d_refs[k][b, :, cs:cs + cw] = gr.astype(d_refs[k].dtype)
                for r, gr in zip(ds_refs, grads[n_pieces:n_pieces + n_sp]):
                    r[b] += gr
                for r, gi in zip(dg_refs, grad_glob):
                    r[...] += grads[n_pieces + n_sp + gi]

        in_specs = self._in_specs(bb, unread) + [self._row_spec(bb, c.shape[-1]) for c in cts]
        in_specs += [self._row_spec(bb, add[k].shape[-1]) for k in add_keys]
        out_specs = [self._row_spec(bb, self.arrays[ai].shape[-1]) for ai in g_idx]
        out_shape = [_sds(self.arrays[ai].shape, grad_arrays[ai]) for ai in g_idx]
        for si, _ in self.samp_pieces:
            C = self.samp[si].shape[-1]
            out_specs.append(pl.BlockSpec((bb, None, 1, C), lambda g, i: (g, i // nx, 0, 0)))
            out_shape.append(_sds((B, 2, 1, C), F32))
        for gi in grad_glob:
            g = self.glob[gi]
            out_specs.append(pl.BlockSpec(g.shape, lambda g_, i, nd=g.ndim: (0,) * nd))
            out_shape.append(_sds(g.shape, F32))
        outs = _pallas(
            body, name=self.name + "_bwd", grid=(B // bb, self.L // self.tm), in_specs=in_specs, out_specs=out_specs,
            out_shape=out_shape, compiler_params=_params(("arbitrary", "arbitrary")),
        )(*self.arrays, *samp, *self.glob, *cts, *[add[k] for k in add_keys])
        ng_ = len(g_idx)
        return outs[:ng_], outs[ng_:ng_ + n_sp], outs[ng_ + n_sp:]


@jax.custom_vjp
def _rms(x, g):
    return x * lax.rsqrt(jnp.mean(x * x, axis=-1, keepdims=True) + EPS) * g


def _rms_fwd(x, g):
    r = lax.rsqrt(jnp.mean(x * x, axis=-1, keepdims=True) + EPS)
    xh = x * r
    return xh * g, (xh, r, g)


def _rms_bwd(res, dy):
    xh, r, g = res
    dxh = dy * g
    dx = r * (dxh - xh * jnp.mean(dxh * xh, axis=-1, keepdims=True))
    return dx, jnp.sum(dy * xh, axis=0, keepdims=True)


_rms.defvjp(_rms_fwd, _rms_bwd)


@jax.custom_vjp
def _swap_halves(x):
    w = x.shape[-1]
    lane = lax.broadcasted_iota(jnp.int32, x.shape, x.ndim - 1)
    up = pltpu.roll(x, w - ROPE_DIM // 2, x.ndim - 1)
    down = pltpu.roll(x, ROPE_DIM // 2, x.ndim - 1)
    return jnp.where(lane % ROPE_DIM < ROPE_DIM // 2, up, down)


_swap_halves.defvjp(lambda x: (_swap_halves(x), None), lambda _, ct: (_swap_halves(ct),))


@jax.custom_vjp
def _bdot(a, b):
    return jnp.dot(a.astype(BF16), b.astype(BF16), preferred_element_type=F32)


_bdot.defvjp(lambda a, b: (_bdot(a, b), (a, b)),
             lambda res, ct: (_dot_nt(ct.astype(BF16), res[1].astype(BF16)), _dot_tn(res[0].astype(BF16), ct.astype(BF16))))


def _rope(x, cos2, sin2):
    return x * cos2 + _swap_halves(x) * sin2


def _f_pre(s, shift, scale, g):
    return (_rms(s, g) * (1.0 + scale) + shift,)


def _f_res(s, y, gate):
    return (s + gate * y,)


def _f_res_pre(s, y, gate, shift, scale, g):
    s1 = s + gate * y
    return s1, _rms(s1, g) * (1.0 + scale) + shift


def _f_ab_pre(*args):
    kv_lat, q_lat, kpe = args[0:3]
    us, vs = args[3:7], args[7:11]
    cos2, sin2 = args[11:13]
    kv_norm, q_norm = args[13:15]
    vns, wss, bss = args[15:19], args[19:23], args[23:27]
    outs = [_rms(kv_lat, kv_norm), _rms(q_lat, q_norm), _rope(kpe, cos2, sin2)]
    for u, v, vn, ws, bs in zip(us, vs, vns, wss, bss):
        vg = _rms(jax.nn.gelu(v), vn)
        mixed = _bdot(ws, vg) + bs
        outs.append(jax.nn.gelu(u) * mixed)
    return tuple(outs)


def _f_q_rope(qn, qr, cos2, sin2):
    return qn, _rope(qr, cos2, sin2)


def _f_cd_pre(rk, rv, sk, sv, rq, rg, sq, c256, s256, c128, s128, c512, s512):
    return (_rope(rq, c256, s256), _rope(rk * RET_K_SCALE, c256, s256), rv,
            _rope(sq, c512, s512), _rope(sk, c128, s128), sv, rg)


def _f_cd_merge(*args):
    ys, rgs, rns = args[0:4], args[4:8], args[8:12]
    return tuple(_rms(y, rn) * jax.nn.silu(rg) for y, rg, rn in zip(ys, rgs, rns))


def _dot_nt(a, b):
    return lax.dot_general(a, b, (((1,), (1,)), ((), ())), preferred_element_type=F32)


def _dot_tn(a, b):
    return lax.dot_general(a, b, (((0,), (0,)), ((), ())), preferred_element_type=F32)


def _dot(a, b):
    return jnp.dot(a, b, preferred_element_type=F32)


def _tile_spec(tq, C):
    return pl.BlockSpec((None, tq, C), lambda b, i: (b, i, 0))


def _full_spec(L, C):
    return pl.BlockSpec((None, L, C), lambda b, i: (b, 0, 0))


def _mla_half(h):
    lane = lax.broadcasted_iota(jnp.int32, (1, 128), 1)
    return (lane < 64) if h % 2 == 0 else (lane >= 64)


def _mla_query(q_ref, h):
    pair = q_ref[:, 512 + 128 * (h // 2):640 + 128 * (h // 2)]
    return jnp.concatenate([q_ref[:, 128 * h:128 * h + 128], jnp.where(_mla_half(h), pair, jnp.zeros_like(pair))], axis=1)


def _mla_fill_keys(kcat, kv_ref, kpe_ref):
    kp = kpe_ref[...]
    for h in range(4):
        kcat[h, :, 0:128] = kv_ref[:, 256 * h:256 * h + 128]
        kcat[h, :, 128:256] = jnp.where(_mla_half(h), kp, jnp.zeros_like(kp))


def _mla_fwd(q, kv, kpe, n, tq):
    B, L, _ = q.shape

    def body(q_ref, kv_ref, kpe_ref, o_ref, lse_ref, kcat):
        i = pl.program_id(1)

        @pl.when(i == 0)
        def _():
            _mla_fill_keys(kcat, kv_ref, kpe_ref)

        def tile(keys):
            for h in range(4):
                s = _dot_nt(_mla_query(q_ref, h), kcat[h, keys, :]) * MLA_SCALE
                m = jnp.max(s, axis=1, keepdims=True)
                e = jnp.exp(s - m)
                l = jnp.sum(e, axis=1, keepdims=True)
                p = (e * (1.0 / l)).astype(BF16)
                o_ref[:, 128 * h:128 * h + 128] = _dot(p, kv_ref[keys, 256 * h + 128:256 * h + 256]).astype(o_ref.dtype)
                lse_ref[:, h:h + 1] = m + jnp.log(l)

        pl.when(i < n // tq)(functools.partial(tile, slice(0, L)))
        pl.when(i >= n // tq)(functools.partial(tile, slice(n, L)))

    return _pallas(
        body, name="mla_fwd", grid=(B, L // tq),
        in_specs=[_tile_spec(tq, 768), _full_spec(L, 1024), _full_spec(L, 128)],
        out_specs=[_tile_spec(tq, 512), _tile_spec(tq, 4)],
        out_shape=[_sds((B, L, 512), BF16), _sds((B, L, 4), F32)],
        scratch_shapes=[pltpu.VMEM((4, L, 256), BF16)],
        compiler_params=_params(("parallel", "arbitrary")),
    )(q, kv, kpe)


def _mla_bwd(q, kv, kpe, lse, do, n, tq):
    B, L, _ = q.shape

    def body(q_ref, kv_ref, kpe_ref, lse_ref, do_ref, dq_ref, dkv_ref, dkpe_ref, kcat):
        i = pl.program_id(1)

        @pl.when(i == 0)
        def _():
            dkv_ref[...] = jnp.zeros_like(dkv_ref)
            dkpe_ref[...] = jnp.zeros_like(dkpe_ref)
            _mla_fill_keys(kcat, kv_ref, kpe_ref)

        def tile(keys):
            rope_pair = None
            for h in range(4):
                qc, kc = _mla_query(q_ref, h), kcat[h, keys, :]
                v = kv_ref[keys, 256 * h + 128:256 * h + 256]
                p = jnp.exp(_dot_nt(qc, kc) * MLA_SCALE - lse_ref[:, h:h + 1])
                doh = do_ref[:, 128 * h:128 * h + 128].astype(BF16)
                dp = _dot_nt(doh, v)
                delta = jnp.sum(p * dp, axis=1, keepdims=True)
                ds = (p * (dp - delta) * MLA_SCALE).astype(BF16)
                dqc = _dot(ds, kc)
                dq_ref[:, 128 * h:128 * h + 128] = dqc[:, 0:128]
                if h % 2 == 0:
                    rope_pair = dqc[:, 128:256]
                else:
                    dq_ref[:, 512 + 128 * (h // 2):640 + 128 * (h // 2)] = jnp.where(_mla_half(h), dqc[:, 128:256], rope_pair)
                dkc = _dot_tn(ds, qc)
                dkv_ref[keys, 256 * h:256 * h + 128] += dkc[:, 0:128]
                dkpe_ref[keys, :] += dkc[:, 128:256]
                dkv_ref[keys, 256 * h + 128:256 * h + 256] += _dot_tn(p.astype(BF16), doh)

        pl.when(i < n // tq)(functools.partial(tile, slice(0, L)))
        pl.when(i >= n // tq)(functools.partial(tile, slice(n, L)))

    return _pallas(
        body, name="mla_bwd", grid=(B, L // tq),
        in_specs=[_tile_spec(tq, 768), _full_spec(L, 1024), _full_spec(L, 128), _tile_spec(tq, 4),
                  pl.BlockSpec((None, tq, 512), lambda b, i: (b, i, 0))],
        out_specs=[_tile_spec(tq, 768), _full_spec(L, 1024), _full_spec(L, 128)],
        out_shape=[_sds((B, L, 768), F32), _sds((B, L, 1024), F32), _sds((B, L, 128), F32)],
        scratch_shapes=[pltpu.VMEM((4, L, 256), BF16)],
        compiler_params=_params(("parallel", "arbitrary")),
    )(q, kv, kpe, lse, do)


def _swa_window(i, tq, n):
    W = min(tq + 2 * SWA_WINDOW, n)
    lo = jnp.clip(i * tq - SWA_WINDOW, 0, n - W)
    return pl.multiple_of(lo, 128), W


def _swa_mask(i, lo, tq, W, n):
    qpos = i * tq + lax.broadcasted_iota(jnp.int32, (tq, 1), 0)
    kpos = lo + lax.broadcasted_iota(jnp.int32, (1, W), 1)
    return (jnp.abs(qpos - kpos) <= SWA_WINDOW) & (qpos < n)


def _swa_fill(rep, src_ref):
    for g in range(2):
        rep[:, 256 * g:256 * g + 256] = jnp.concatenate([src_ref[:, 64 * g:64 * g + 64]] * 4, axis=1)


def _swa_head_rows(a):
    lane = lax.broadcasted_iota(jnp.int32, (1, 256), 1)
    return jnp.concatenate([jnp.where(lane // 64 == h, a, jnp.zeros_like(a)) for h in range(4)], axis=0)


def _swa_fold_rows(a, tq):
    lane = lax.broadcasted_iota(jnp.int32, (1, 256), 1)
    out = jnp.where(lane // 64 == 0, a[0:tq], 0.0)
    for h in range(1, 4):
        out = jnp.where(lane // 64 == h, a[h * tq:(h + 1) * tq], out)
    return out


def _swa_fwd(q, k, v, sink, n, tq):
    B, L, _ = q.shape

    def body(q_ref, k_ref, v_ref, sink_ref, o_ref, lse_ref, krep, vrep):
        i = pl.program_id(1)

        @pl.when(i == 0)
        def _():
            _swa_fill(krep, k_ref)
            _swa_fill(vrep, v_ref)

        def tile(band):
            for g in range(2):
                gc = slice(256 * g, 256 * g + 256)
                qs = _swa_head_rows(q_ref[:, gc])
                sk = jnp.concatenate([jnp.broadcast_to(sink_ref[0:1, 4 * g + h:4 * g + h + 1], (tq, 1))
                                      for h in range(4)], axis=0)
                s2 = _dot_nt(qs, krep[n:L, gc]) * SWA_SCALE
                m = jnp.maximum(jnp.max(s2, axis=1, keepdims=True), sk)
                if band:
                    lo, W = _swa_window(i, tq, n)
                    mask = jnp.concatenate([_swa_mask(i, lo, tq, W, n)] * 4, axis=0)
                    s1 = jnp.where(mask, _dot_nt(qs, krep[pl.ds(lo, W), gc]) * SWA_SCALE, NEG_INF)
                    m = jnp.maximum(m, jnp.max(s1, axis=1, keepdims=True))
                e2 = jnp.exp(s2 - m)
                l = jnp.sum(e2, axis=1, keepdims=True) + jnp.exp(sk - m)
                if band:
                    e1 = jnp.exp(s1 - m)
                    l = l + jnp.sum(e1, axis=1, keepdims=True)
                r = 1.0 / l
                o = _dot((e2 * r).astype(BF16), vrep[n:L, gc])
                if band:
                    o = o + _dot((e1 * r).astype(BF16), vrep[pl.ds(lo, W), gc])
                o_ref[:, gc] = _swa_fold_rows(o, tq).astype(o_ref.dtype)
                lse = m + jnp.log(l)
                for h in range(4):
                    lse_ref[:, 4 * g + h:4 * g + h + 1] = lse[h * tq:(h + 1) * tq]

        pl.when(i < n // tq)(functools.partial(tile, True))
        pl.when(i >= n // tq)(functools.partial(tile, False))

    return _pallas(
        body, name="swa_fwd", grid=(B, L // tq),
        in_specs=[_tile_spec(tq, 512), _full_spec(L, 128), _full_spec(L, 128), pl.BlockSpec((1, 8), lambda b, i: (0, 0))],
        out_specs=[_tile_spec(tq, 512), _tile_spec(tq, 8)],
        out_shape=[_sds((B, L, 512), BF16), _sds((B, L, 8), F32)],
        scratch_shapes=[pltpu.VMEM((L, 512), BF16), pltpu.VMEM((L, 512), BF16)],
        compiler_params=_params(("parallel", "arbitrary")),
    )(q, k, v, sink)


def _swa_bwd(q, k, v, sink, lse, do, n, tq):
    B, L, _ = q.shape

    def body(q_ref, k_ref, v_ref, sink_ref, lse_ref, do_ref, dq_ref, dk_ref, dv_ref, dsink_ref):
        i = pl.program_id(1)

        @pl.when(i == 0)
        def _():
            dk_ref[...] = jnp.zeros_like(dk_ref)
            dv_ref[...] = jnp.zeros_like(dv_ref)
            dsink_ref[...] = jnp.zeros_like(dsink_ref)

        def tile(band):
            kc, vc = k_ref[n:L, :], v_ref[n:L, :]
            if band:
                lo, W = _swa_window(i, tq, n)
                mask = _swa_mask(i, lo, tq, W, n)
                kl, vl = k_ref[pl.ds(lo, W), :], v_ref[pl.ds(lo, W), :]
            for h in range(8):
                g = h // 4
                cols = slice(64 * g, 64 * g + 64)
                qh = q_ref[:, 64 * h:64 * h + 64]
                lse_h = lse_ref[:, h:h + 1]
                doh = do_ref[:, 64 * h:64 * h + 64].astype(BF16)
                p2 = jnp.exp(_dot_nt(qh, kc[:, cols]) * SWA_SCALE - lse_h)
                ps = jnp.exp(sink_ref[0:1, h:h + 1] - lse_h)
                dp2 = _dot_nt(doh, vc[:, cols])
                delta = jnp.sum(p2 * dp2, axis=1, keepdims=True)
                if band:
                    p1 = jnp.exp(jnp.where(mask, _dot_nt(qh, kl[:, cols]) * SWA_SCALE, NEG_INF) - lse_h)
                    dp1 = _dot_nt(doh, vl[:, cols])
                    delta = delta + jnp.sum(p1 * dp1, axis=1, keepdims=True)
                ds2 = (p2 * (dp2 - delta) * SWA_SCALE).astype(BF16)
                dq = _dot(ds2, kc[:, cols])
                dk_ref[n:L, cols] += _dot_tn(ds2, qh)
                dv_ref[n:L, cols] += _dot_tn(p2.astype(BF16), doh)
                if band:
                    ds1 = (p1 * (dp1 - delta) * SWA_SCALE).astype(BF16)
                    dq = dq + _dot(ds1, kl[:, cols])
                    dk_ref[pl.ds(lo, W), cols] += _dot_tn(ds1, qh)
                    dv_ref[pl.ds(lo, W), cols] += _dot_tn(p1.astype(BF16), doh)
                dq_ref[:, 64 * h:64 * h + 64] = dq
                dsink_ref[0:1, h:h + 1] += jnp.sum(-ps * delta, axis=0, keepdims=True)

        pl.when(i < n // tq)(functools.partial(tile, True))
        pl.when(i >= n // tq)(functools.partial(tile, False))

    return _pallas(
        body, name="swa_bwd", grid=(B, L // tq),
        in_specs=[_tile_spec(tq, 512), _full_spec(L, 128), _full_spec(L, 128), pl.BlockSpec((1, 8), lambda b, i: (0, 0)),
                  _tile_spec(tq, 8), pl.BlockSpec((None, tq, 512), lambda b, i: (b, i, 1))],
        out_specs=[_tile_spec(tq, 512), _full_spec(L, 128), _full_spec(L, 128),
                   pl.BlockSpec((None, 1, 8), lambda b, i: (b, 0, 0))],
        out_shape=[_sds((B, L, 512), F32), _sds((B, L, 128), F32), _sds((B, L, 128), F32), _sds((B, 1, 8), F32)],
        compiler_params=_params(("parallel", "arbitrary")),
    )(q, k, v, sink, lse, do)


def _ret_decay(i, tq, L, n, ctx_tile):
    qi = i * tq + lax.broadcasted_iota(jnp.int32, (tq, 1), 0)
    kc = lax.broadcasted_iota(jnp.int32, (1, L - n), 1)
    d_hb = (n + kc - qi).astype(F32)
    if ctx_tile:
        return None, ((qi - L) - (kc - (L - n))).astype(F32), d_hb
    d_x = (qi - lax.broadcasted_iota(jnp.int32, (1, n), 1)).astype(F32)
    return d_x, (qi - (kc - (L - n))).astype(F32), d_hb


def _ret_head(a, h):
    lane = lax.broadcasted_iota(jnp.int32, (1, a.shape[1]), 1)
    return jnp.where(lane // 64 == h, a, jnp.zeros_like(a))


def _ret_mask(dist, lg):
    return jnp.where(dist >= 0.0, jnp.exp(lg * jnp.maximum(dist, 0.0)), 0.0)


def _ret_weights(n, L, d_x, d_hf, d_hb, lg_f, lg_b, with_grad):
    m_f, m_b = _ret_mask(d_hf, lg_f), _ret_mask(d_hb, lg_b)
    ctx = (slice(n, L), m_f + m_b)
    if with_grad:
        ctx += (m_f * jnp.maximum(d_hf, 0.0), m_b * jnp.maximum(d_hb, 0.0))
    if d_x is None:
        return [ctx]
    e_x = jnp.exp(jnp.where(d_x >= 0.0, lg_f, -lg_b) * d_x)
    lat = (slice(0, n), jnp.where(d_x == 0.0, 2.0 * e_x, e_x))
    if with_grad:
        lat += (e_x * jnp.maximum(d_x, 0.0), e_x * jnp.maximum(-d_x, 0.0))
    return [lat, ctx]


def _ret_fwd(q, k, v, lg, n, tq):
    B, L, _ = q.shape

    def body(q_ref, k_ref, v_ref, lg_ref, y_ref):
        i = pl.program_id(1)

        def tile(ctx_tile):
            dist = _ret_decay(i, tq, L, n, ctx_tile)
            for h in range(4):
                qh = _ret_head(q_ref[...], h)
                y = None
                for rows, dec_r in _ret_weights(n, L, *dist, lg_ref[0:1, h:h + 1], lg_ref[1:2, h:h + 1], False):
                    a = _dot_nt(qh, k_ref[rows, :]) * dec_r
                    part = _dot(a.astype(BF16), v_ref[rows, 128 * h:128 * h + 128])
                    y = part if y is None else y + part
                y_ref[:, 128 * h:128 * h + 128] = y

        pl.when(i < n // tq)(functools.partial(tile, False))
        pl.when(i >= n // tq)(functools.partial(tile, True))

    return _pallas(
        body, name="ret_fwd", grid=(B, L // tq),
        in_specs=[_tile_spec(tq, 256), _full_spec(L, 256), _full_spec(L, 512), pl.BlockSpec((2, 4), lambda b, i: (0, 0))],
        out_specs=_tile_spec(tq, 512), out_shape=_sds((B, L, 512), F32),
        compiler_params=_params(("parallel", "arbitrary")),
    )(q, k, v, lg)


def _ret_bwd(q, k, v, lg, dy, n, tq):
    B, L, _ = q.shape

    def body(q_ref, k_ref, v_ref, lg_ref, dy_ref, dq_ref, dk_ref, dv_ref, dlg_ref):
        i = pl.program_id(1)

        @pl.when(i == 0)
        def _():
            dk_ref[...] = jnp.zeros_like(dk_ref)
            dv_ref[...] = jnp.zeros_like(dv_ref)
            dlg_ref[...] = jnp.zeros_like(dlg_ref)

        def total(a):
            return jnp.sum(jnp.sum(a, axis=1, keepdims=True), axis=0, keepdims=True)

        def tile(ctx_tile):
            dist = _ret_decay(i, tq, L, n, ctx_tile)
            dq = None
            for h in range(4):
                vc = slice(128 * h, 128 * h + 128)
                qh = _ret_head(q_ref[...], h)
                dyh = dy_ref[:, vc].astype(BF16)
                dqh = None
                for rows, dec_r, wf_r, wb_r in _ret_weights(n, L, *dist, lg_ref[0:1, h:h + 1], lg_ref[1:2, h:h + 1], True):
                    k_all, vh = k_ref[rows, :], v_ref[rows, vc]
                    s = _dot_nt(qh, k_all)
                    gr = _dot_nt(dyh, vh)
                    ds = (gr * dec_r).astype(BF16)
                    part = _dot(ds, k_all)
                    dqh = part if dqh is None else dqh + part
                    dk_ref[rows, :] += _dot_tn(ds, qh)
                    dv_ref[rows, vc] += _dot_tn((s * dec_r).astype(BF16), dyh)
                    gs = gr * s
                    dlg_ref[0:1, h:h + 1] += total(gs * wf_r)
                    dlg_ref[1:2, h:h + 1] += total(gs * wb_r)
                dqh = _ret_head(dqh, h)
                dq = dqh if dq is None else dq + dqh
            dq_ref[...] = dq

        pl.when(i < n // tq)(functools.partial(tile, False))
        pl.when(i >= n // tq)(functools.partial(tile, True))

    return _pallas(
        body, name="ret_bwd", grid=(B, L // tq),
        in_specs=[_tile_spec(tq, 256), _full_spec(L, 256), _full_spec(L, 512), pl.BlockSpec((2, 4), lambda b, i: (0, 0)),
                  _tile_spec(tq, 512)],
        out_specs=[_tile_spec(tq, 256), _full_spec(L, 256), _full_spec(L, 512),
                   pl.BlockSpec((None, 2, 4), lambda b, i: (b, 0, 0))],
        out_shape=[_sds((B, L, 256), F32), _sds((B, L, 256), F32), _sds((B, L, 512), F32), _sds((B, 2, 4), F32)],
        compiler_params=_params(("parallel", "arbitrary")),
    )(q, k, v, lg, dy)


def _loss_head(s, target, g, n, tm):
    B, L, D = s.shape
    nx = n // tm

    def body(s_ref, t_ref, g_ref, ds_ref, dg_ref, loss_ref):
        b, i = pl.program_id(0), pl.program_id(1)

        @pl.when((b == 0) & (i == 0))
        def _():
            dg_ref[...] = jnp.zeros_like(dg_ref)
            loss_ref[...] = jnp.zeros_like(loss_ref)

        @pl.when(i < nx)
        def _():
            y, vjp = jax.vjp(_rms, s_ref[...], g_ref[...])
            err = y - t_ref[...]
            d_s, d_g = vjp(err * (1.0 / D))
            ds_ref[...] = d_s
            dg_ref[...] += d_g
            part = jnp.sum(jnp.sum(err * err, axis=1, keepdims=True), axis=0, keepdims=True) * (0.5 / D)
            loss_ref[...] += jnp.broadcast_to(part, loss_ref.shape)

        @pl.when(i >= nx)
        def _():
            ds_ref[...] = jnp.zeros_like(ds_ref)

    return _pallas(
        body, name="loss_head", grid=(B, L // tm),
        in_specs=[pl.BlockSpec((None, tm, D), lambda b, i: (b, i, 0)),
                  pl.BlockSpec((None, tm, D), lambda b, i: (b, jnp.minimum(i, nx - 1), 0)),
                  pl.BlockSpec((1, D), lambda b, i: (0, 0))],
        out_specs=[pl.BlockSpec((None, tm, D), lambda b, i: (b, i, 0)), pl.BlockSpec((1, D), lambda b, i: (0, 0)),
                   pl.BlockSpec((1, 128), lambda b, i: (0, 0))],
        out_shape=[_sds((B, L, D), F32), _sds((1, D), F32), _sds((1, 128), F32)],
        compiler_params=_params(("arbitrary", "arbitrary")),
    )(s, target, g)


def _ada_fwd(c_all, ada_w, ada_b):
    NL, D, Ns = ada_w.shape
    R = c_all.shape[0]

    def body(c_ref, w_ref, b_ref, o_ref):
        cond = jax.nn.silu(c_ref[...]).astype(BF16)
        o_ref[...] = _dot(cond, w_ref[...].astype(BF16)) + b_ref[...]

    return _pallas(
        body, name="ada_fwd", grid=(NL,),
        in_specs=[pl.BlockSpec((R, D), lambda l: (0, 0)), pl.BlockSpec((None, D, Ns), lambda l: (l, 0, 0)),
                  pl.BlockSpec((None, 1, Ns), lambda l: (l, 0, 0))],
        out_specs=pl.BlockSpec((None, R, Ns), lambda l: (l, 0, 0)), out_shape=_sds((NL, R, Ns), F32),
        compiler_params=_params(("parallel",)),
    )(c_all, ada_w, ada_b)


def _ada_bwd(c_all, ada_w, dmods):
    NL, D, Ns = ada_w.shape
    R = c_all.shape[0]

    def body(c_ref, w_ref, dm_ref, dw_ref, dc_ref):
        cond = jax.nn.silu(c_ref[...]).astype(BF16)
        dm = dm_ref[...].astype(BF16)
        dw_ref[...] = _dot_tn(cond, dm)
        dc_ref[...] = _dot_nt(dm, w_ref[...].astype(BF16))

    return _pallas(
        body, name="ada_bwd", grid=(NL,),
        in_specs=[pl.BlockSpec((R, D), lambda l: (0, 0)), pl.BlockSpec((None, D, Ns), lambda l: (l, 0, 0)),
                  pl.BlockSpec((None, R, Ns), lambda l: (l, 0, 0))],
        out_specs=[pl.BlockSpec((None, D, Ns), lambda l: (l, 0, 0)), pl.BlockSpec((None, R, D), lambda l: (l, 0, 0))],
        out_shape=[_sds((NL, D, Ns), F32), _sds((NL, R, D), F32)],
        compiler_params=_params(("parallel",)),
    )(c_all, ada_w, dmods)


def _my_index():
    return 4 * lax.axis_index("x") + 2 * lax.axis_index("y") + lax.axis_index("c")


def _peer(k):
    x, y, c = lax.axis_index("x"), lax.axis_index("y"), lax.axis_index("c")
    kx, ky, kc = (k >> 2) & 1, (k >> 1) & 1, k & 1
    px, py, pc = (x + kx) % 2, (y + ky) % 2, (c + kc) % 2
    return (px, py, pc), 4 * px + 2 * py + pc


def _all_gather(name, shards):
    na = len(shards)
    hbm = pl.BlockSpec(memory_space=pl.ANY)

    def body(*refs):
        in_refs, out_refs = refs[:na], refs[na:2 * na]
        send_sems, recv_sems, local_sems = refs[2 * na:]
        me = _my_index()
        sib_id, sib = _peer(1)
        chips = [_peer(k) for k in (4, 2, 6)]
        sib_chips = [4 * px + 2 * py + (1 - pc) for (px, py, pc), _ in chips]

        def copy(a, k, slot, to, src=None):
            dst = out_refs[a].at[slot]
            return pltpu.make_async_remote_copy(
                src_ref=dst if src is None else src, dst_ref=dst, send_sem=send_sems.at[a, k],
                recv_sem=recv_sems.at[a, k], device_id=to, device_id_type=MESH)

        first, passed, mine = [], [], []
        for a in range(na):
            cp = pltpu.make_async_copy(in_refs[a], out_refs[a].at[me], local_sems.at[a])
            cp.start()
            mine.append(cp)
            first.append(copy(a, 0, me, sib_id, src=in_refs[a]))
            first += [copy(a, 1 + j, me, pid, src=in_refs[a]) for j, (pid, _) in enumerate(chips)]
        for cp in first:
            cp.start()
        for a in range(na):
            for j, (pid, pidx) in enumerate(chips):
                copy(a, 1 + j, pidx, pid).wait_recv()
                fwd = copy(a, 4 + j, pidx, sib_id)
                fwd.start()
                passed.append(fwd)
        for a in range(na):
            copy(a, 0, sib, sib_id).wait_recv()
            for j in range(3):
                copy(a, 4 + j, sib_chips[j], sib_id).wait_recv()
        for cp in first + passed:
            cp.wait_send()
        for cp in mine:
            cp.wait()

    return _pallas(
        body, name=name, in_specs=[hbm] * na, out_specs=[hbm] * na,
        out_shape=[_sds((N_DEV,) + s.shape, s.dtype) for s in shards],
        scratch_shapes=[pltpu.SemaphoreType.DMA((na, 7)), pltpu.SemaphoreType.DMA((na, 7)),
                        pltpu.SemaphoreType.DMA((na,))],
    )(*shards)


_HBM = pl.BlockSpec(memory_space=pltpu.HBM)
_SEM = pl.BlockSpec(memory_space=pltpu.SEMAPHORE)
_DATAFLOW = pltpu.SideEffectType.DATAFLOW_SIDE_EFFECTING


def _exchange_start(name, srcs, slotted, after=()):
    na = len(srcs)
    lands = [lax.empty((N_DEV,) + (s.shape[1:] if slotted else s.shape), s.dtype) for s in srcs]

    def body(*refs):
        src_refs, land_refs = refs[:na], refs[na:2 * na]
        outs = refs[2 * na + len(after):]
        send_sems, recv_sems, token = outs[:na], outs[na:2 * na], outs[4 * na]
        me = _my_index()
        for a in range(na):
            for k in range(1, N_DEV):
                pid, pidx = _peer(k)
                pltpu.make_async_remote_copy(
                    src_ref=src_refs[a].at[pidx] if slotted else src_refs[a], dst_ref=land_refs[a].at[me],
                    send_sem=send_sems[a], recv_sem=recv_sems[a], device_id=pid, device_id_type=MESH).start()
        token[...] = jnp.zeros_like(token)

    ops = [pltpu.with_memory_space_constraint(a, pltpu.HBM) for a in list(srcs) + lands]
    outs = _pallas(
        body, name=name,
        out_shape=[pltpu.SemaphoreType.DMA(())] * (2 * na) + [pltpu.HBM(a.shape, a.dtype) for a in ops]
        + [_sds((8, 128), F32)],
        in_specs=[_HBM] * (2 * na) + [pl.BlockSpec(memory_space=pl.ANY)] * len(after),
        out_specs=[_SEM] * (2 * na) + [_HBM] * (2 * na) + [pl.BlockSpec(memory_space=pltpu.VMEM)],
        input_output_aliases={a: 2 * na + a for a in range(2 * na)},
        compiler_params=pltpu.CompilerParams(has_side_effects=_DATAFLOW),
    )(*ops, *after)
    return (na, outs[:2 * na], outs[2 * na:4 * na]), outs[4 * na]


def _exchange_wait(name, handle, after):
    na, sems, thru = handle

    def body(*refs):
        land_refs = refs[na:2 * na]
        send_sems, recv_sems = refs[2 * na:3 * na], refs[3 * na:4 * na]
        me_id = (lax.axis_index("x"), lax.axis_index("y"), lax.axis_index("c"))
        for a in range(na):
            seven = land_refs[a].at[pl.ds(0, N_DEV - 1)]
            drain = pltpu.make_async_remote_copy(src_ref=seven, dst_ref=seven, send_sem=send_sems[a],
                                                 recv_sem=recv_sems[a], device_id=me_id, device_id_type=MESH)
            drain.wait_send()
            drain.wait_recv()

    outs = _pallas(
        body, name=name, out_shape=[pltpu.HBM(a.shape, a.dtype) for a in thru],
        in_specs=[_HBM] * (2 * na) + [_SEM] * (2 * na) + [pl.BlockSpec(memory_space=pl.ANY)],
        out_specs=[_HBM] * (2 * na), input_output_aliases={a: a for a in range(2 * na)},
        compiler_params=pltpu.CompilerParams(has_side_effects=_DATAFLOW),
    )(*thru, *sems, after)
    return outs[na:]


def _fill_own(name, landed, owns, slotted):
    me = _my_index()
    return [jnp.where(lax.broadcasted_iota(jnp.int32, land.shape, 0) == me, own, land)
            for land, own in zip(landed, owns)]


def _rope_tables(n, L, width):
    t = jnp.arange(n)
    row = (t // GRID_W).astype(F32)
    col = (t % GRID_W).astype(F32)
    n_freq = ROPE_DIM // 4
    freqs = ROPE_THETA ** (-jnp.arange(n_freq, dtype=F32) / n_freq)
    ang = jnp.concatenate([row[:, None] * freqs, col[:, None] * freqs], axis=-1)
    cos, sin = jnp.cos(ang), jnp.sin(ang)
    cos2 = jnp.concatenate([cos, cos], axis=-1)
    sin2 = jnp.concatenate([-sin, sin], axis=-1)
    cos2 = jnp.concatenate([cos2, jnp.ones((L - n, ROPE_DIM), F32)], axis=0)
    sin2 = jnp.concatenate([sin2, jnp.zeros((L - n, ROPE_DIM), F32)], axis=0)
    reps = width // ROPE_DIM
    return jnp.tile(cos2, (1, reps)), jnp.tile(sin2, (1, reps))


def _ab_perm(w):
    return jnp.concatenate([w[:, 0:256], w[:, 320:1600], w[:, 256:320], w[:, 256:320]], axis=1)


def _ab_unperm(g):
    rope_key = (g[:, 1536:1600].astype(F32) + g[:, 1600:1664].astype(F32)).astype(g.dtype)
    return jnp.concatenate([g[:, 0:256], rope_key, g[:, 256:1536]], axis=1)


def _wq_perm(w):
    return jnp.concatenate([w[:, 192 * h:192 * h + 128] for h in range(4)]
                           + [w[:, 192 * h + 128:192 * h + 192] for h in range(4)], axis=1)


def _wq_unperm(g):
    return jnp.concatenate([g[:, sl] for h in range(4)
                            for sl in (slice(128 * h, 128 * h + 128), slice(512 + 64 * h, 576 + 64 * h))], axis=1)


def _flat(a):
    return a.reshape(-1, a.shape[-1])


def _layer_weights(full, p):
    NL, NE, NO = len(full["ffn_in"]), len(full["ab_in"]), len(full["cd_in"])
    groups = range(4)

    def each(f, mats):
        return [None if w is None else f(w) for w in mats]

    return dict(
        norm_mix=[p["norm_mix"][l][None] for l in range(NL)], norm_ffn=[p["norm_ffn"][l][None] for l in range(NL)],
        norm_final=p["norm_final"][None],
        ffn_in=each(_ffn_interleave, full["ffn_in"]), ffn_out=list(full["ffn_out"]),
        ab_in=each(_ab_perm, full["ab_in"]), ab_out=list(full["ab_out"]),
        wq=each(_wq_perm, full["mla_wq_b"]), wkv=list(full["mla_wkv_b"]),
        kv_norm=[p["mla_kv_norm"][j][None] for j in range(NE)], q_norm=[p["mla_q_norm"][j][None] for j in range(NE)],
        v_norm=[[p["cmlp_v_norm"][j][None, 128 * g:128 * g + 128] for g in groups] for j in range(NE)],
        ws=[[p["cmlp_ws"][j, g] for g in groups] for j in range(NE)],
        bs=[[p["cmlp_bs"][j, g][:, None] for g in groups] for j in range(NE)],
        cd_in=list(full["cd_in"]), cd_out=list(full["cd_out"]),
        lg=[jnp.stack([jax.nn.log_sigmoid(p["ret_decay_fwd"][j]), jax.nn.log_sigmoid(p["ret_decay_bwd"][j])])
            for j in range(NO)],
        sink=[p["swa_sink"][j][None] for j in range(NO)],
        ret_norm=[[p["ret_norm"][j][None, 128 * g:128 * g + 128] for g in groups] for j in range(NO)],
    )


def _local_step(x, ctx, target, mods, W, later_weights=None, early_grads=None):
    B, n, D = x.shape
    m = ctx.shape[1]
    L = n + m
    NL = mods.shape[0]
    tm = min(256, m)
    tq = min(256, m)
    cos512, sin512 = _rope_tables(n, L, 512)
    s = jnp.concatenate([x, ctx], axis=1)
    saved = []

    def rows(name, f, arrays, pieces, samp, samp_pieces, glob, out_arrays, out_pieces, tile=tm):
        return _Rows(name, f, B, L, n, tile, arrays, pieces, samp, samp_pieces, glob, out_arrays, out_pieces)

    def full(width, start=0):
        return [(0, start, width)]

    for l in range(NL):
        j = l // 2
        even = l % 2 == 0
        md = mods[l]
        if l in (1, 2) and later_weights is not None:
            W = later_weights(l, s)
        r = {}
        if l == 0:
            r["pre1"] = rows("pre_mix0", _f_pre, [s], full(D), [md], [(0, 0), (0, 1)], [W["norm_mix"][0]], [(D, BF16)], full(D))
            (xn,) = r["pre1"].fwd()
        r["xn"] = xn
        if even:
            z = _mm(f"ab_in{l}", _flat(xn), W["ab_in"][j]).reshape(B, L, 1664)
            pieces = [(0, 0, 256), (0, 256, 256), (0, 1536, 128)]
            pieces += [(0, 512 + 128 * g, 128) for g in range(4)] + [(0, 1024 + 128 * g, 128) for g in range(4)]
            pieces += [(1, 0, 128), (2, 0, 128)]
            glob = [W["kv_norm"][j], W["q_norm"][j]] + W["v_norm"][j] + W["ws"][j] + W["bs"][j]
            abp = rows(f"ab_pre{l}", _f_ab_pre, [z, cos512, sin512], pieces, [], [], glob,
                       [(256, BF16), (256, BF16), (128, BF16), (512, BF16)],
                       [(0, 0, 256), (1, 0, 256), (2, 0, 128)] + [(3, 128 * g, 128) for g in range(4)], tile=128)
            kvn, qn, kpe, cm = abp.fwd()
            kv = _mm(f"wkv{l}", _flat(kvn), W["wkv"][j], out_dtype=BF16).reshape(B, L, 1024)
            q0 = _mm(f"wq{l}", _flat(qn), W["wq"][j]).reshape(B, L, 768)
            qrp = rows(f"q_rope{l}", _f_q_rope, [q0, cos512, sin512], [(0, 0, 512), (0, 512, 256), (1, 0, 256), (2, 0, 256)],
                       [], [], [], [(768, BF16)], [(0, 0, 512), (0, 512, 256)])
            (q,) = qrp.fwd()
            o, lse = _mla_fwd(q, kv, kpe, n, tq)
            y = _mm_pair(f"ab_out{l}", _flat(o), _flat(cm), W["ab_out"][j], BF16).reshape(B, L, D)
            r.update(z=z, abp=abp, kvn=kvn, qn=qn, kpe=kpe, cm=cm, kv=kv, qrp=qrp, q=q, o=o, lse=lse)
        else:
            z = _mm(f"cd_in{l}", _flat(xn), W["cd_in"][j]).reshape(B, L, 2304)
            pieces = [(0, 0, 256), (0, 256, 512), (0, 768, 128), (0, 896, 128), (0, 1024, 256), (0, 1280, 512), (0, 1792, 512)]
            pieces += [(1, 0, 256), (2, 0, 256), (1, 0, 128), (2, 0, 128), (1, 0, 512), (2, 0, 512)]
            cdp = rows(f"cd_pre{l}", _f_cd_pre, [z, cos512, sin512], pieces, [], [], [],
                       [(256, BF16), (256, BF16), (512, BF16), (512, BF16), (128, BF16), (128, BF16), (512, F32)],
                       [(k_, 0, w_) for k_, w_ in enumerate((256, 256, 512, 512, 128, 128, 512))])
            rq, rk, rv, sq, sk, sv, rg = cdp.fwd()
            yret = _ret_fwd(rq, rk, rv, W["lg"][j], n, tq)
            osw, lse = _swa_fwd(sq, sk, sv, W["sink"][j], n, tq)
            mrg = rows(f"cd_merge{l}", _f_cd_merge, [yret, rg],
                       [(0, 128 * g, 128) for g in range(4)] + [(1, 128 * g, 128) for g in range(4)], [], [],
                       W["ret_norm"][j], [(512, BF16)], [(0, 128 * g, 128) for g in range(4)])
            (yr,) = mrg.fwd()
            y = _mm_pair(f"cd_out{l}", _flat(yr), _flat(osw), W["cd_out"][j], BF16).reshape(B, L, D)
            r.update(z=z, cdp=cdp, rq=rq, rk=rk, rv=rv, sq=sq, sk=sk, sv=sv, rg=rg, yret=yret, osw=osw, lse=lse,
                     mrg=mrg, yr=yr)
        two, outs2 = [(0, 0, D), (1, 0, D)], [(D, F32), (D, BF16)]
        r["mix_out"] = rows(f"mix_res_pre{l}", _f_res_pre, [s, y], two, [md], [(0, 2), (0, 3), (0, 4)],
                            [W["norm_ffn"][l]], outs2, two)
        s1, xn2 = r["mix_out"].fwd()
        if l == 0 and later_weights is not None:
            W = later_weights(0, s1)
        y2 = _ffn_fwd(f"ffn{l}", _flat(xn2), W["ffn_in"][l], W["ffn_out"][l]).reshape(B, L, D)
        if l < NL - 1:
            r["ffn_out"] = rows(f"ffn_res_pre{l}", _f_res_pre, [s1, y2], two, [md, mods[l + 1]], [(0, 5), (1, 0), (1, 1)],
                                [W["norm_mix"][l + 1]], outs2, two)
            s, xn = r["ffn_out"].fwd()
        else:
            r["ffn_out"] = rows(f"res_ffn{l}", _f_res, [s1, y2], two, [md], [(0, 5)], [], [(D, F32)], full(D))
            (s,) = r["ffn_out"].fwd()
        r["xn2"] = xn2
        saved.append(r)

    ds, d_norm_final, loss = _loss_head(s, target, W["norm_final"], n, tm)

    G = {k: [None] * len(v) for k, v in W.items() if isinstance(v, list)}
    G["norm_final"] = d_norm_final
    dm = [[None] * 6 for _ in range(NL)]
    dxn_next = None
    for l in reversed(range(NL)):
        j = l // 2
        even = l % 2 == 0
        r = saved[l]
        zero = early_grads(0, G) if l == 0 and early_grads is not None else None
        if l == NL - 1:
            (ds1, dy2), (dm[l][5],), _ = r["ffn_out"].bwd([ds], {0: F32, 1: BF16}, unread=(0,))
        else:
            smp = None if zero is None else [mods[l] + zero, mods[l + 1] + zero]
            (ds1, dy2), (dm[l][5], dm[l + 1][0], dm[l + 1][1]), (G["norm_mix"][l + 1],) = r["ffn_out"].bwd(
                [ds, dxn_next], {0: F32, 1: BF16}, grad_glob=(0,), samp=smp)
        dy2f, xn2f = _flat(dy2), _flat(r["xn2"])
        hid, dz2 = _ffn_mid_bwd(f"ffn_mid{l}", xn2f, dy2f, W["ffn_in"][l], W["ffn_out"][l])
        G["ffn_out"][l] = _mm(f"g_ffn_out{l}", hid, dy2f, ta=True, out_dtype=BF16)
        G["ffn_in"][l] = _mm(f"g_ffn_in{l}", xn2f, dz2, ta=True, out_dtype=BF16)
        dxn2 = _mm(f"d_xn2{l}", dz2, W["ffn_in"][l], tb=True, out_dtype=BF16).reshape(B, L, D)
        zero = early_grads(1, G) if l == 0 and early_grads is not None else None
        smp = None if zero is None else [mods[l] + zero]
        (ds0, dy), (dm[l][2], dm[l][3], dm[l][4]), (G["norm_ffn"][l],) = r["mix_out"].bwd(
            [ds1, dxn2], {0: F32, 1: BF16}, grad_glob=(0,), samp=smp)
        dyf = _flat(dy)
        if even:
            w_out = W["ab_out"][j]
            dcat = _mm(f"d_cat{l}", dyf, w_out, tb=True, out_dtype=BF16).reshape(B, L, -1)
            G["ab_out"][l // 2] = jnp.concatenate(
                [_mm(f"g_ab_out_a{l}", _flat(r["o"]), dyf, ta=True, out_dtype=BF16),
                 _mm(f"g_ab_out_b{l}", _flat(r["cm"]), dyf, ta=True, out_dtype=BF16)], axis=0)
            dq, dkv, dkpe = _mla_bwd(r["q"], r["kv"], r["kpe"], r["lse"], dcat, n, tq)
            (dq0,), _, _ = r["qrp"].bwd([dq], {0: BF16}, unread=(0,))
            dq0f, dkvf = _flat(dq0), _flat(dkv)
            G["wq"][j] = _mm(f"g_wq{l}", _flat(r["qn"]), dq0f, ta=True, out_dtype=BF16)
            G["wkv"][j] = _mm(f"g_wkv{l}", _flat(r["kvn"]), dkvf, ta=True, out_dtype=BF16)
            dqn = _mm(f"d_qn{l}", dq0f, W["wq"][j], tb=True).reshape(B, L, 256)
            dkvn = _mm(f"d_kvn{l}", dkvf, W["wkv"][j], tb=True).reshape(B, L, 256)
            (dz,), _, gg = r["abp"].bwd([dkvn, dqn, dkpe, (dcat, 512)], {0: BF16}, grad_glob=tuple(range(14)))
            G["kv_norm"][j], G["q_norm"][j] = gg[0], gg[1]
            G["v_norm"][j], G["ws"][j], G["bs"][j] = list(gg[2:6]), list(gg[6:10]), list(gg[10:14])
            w_in, key = W["ab_in"][j], "ab_in"
        else:
            w_out = W["cd_out"][j]
            dcat = _mm(f"d_cat{l}", dyf, w_out, tb=True, out_dtype=BF16).reshape(B, L, -1)
            G["cd_out"][j] = jnp.concatenate(
                [_mm(f"g_cd_out_a{l}", _flat(r["yr"]), dyf, ta=True, out_dtype=BF16),
                 _mm(f"g_cd_out_b{l}", _flat(r["osw"]), dyf, ta=True, out_dtype=BF16)], axis=0)
            (dyret, drg), _, gg = r["mrg"].bwd([(dcat, 0)], {0: F32, 1: F32}, grad_glob=(0, 1, 2, 3))
            G["ret_norm"][j] = list(gg)
            drq, drk, drv, dlg = _ret_bwd(r["rq"], r["rk"], r["rv"], W["lg"][j], dyret, n, tq)
            dsq, dsk, dsv, dsink = _swa_bwd(r["sq"], r["sk"], r["sv"], W["sink"][j], r["lse"], dcat, n, tq)
            G["lg"][j], G["sink"][j] = dlg, dsink
            (dz,), _, _ = r["cdp"].bwd([drq, drk, drv, dsq, dsk, dsv, drg], {0: BF16}, unread=(0,))
            w_in, key = W["cd_in"][j], "cd_in"
        dzf = _flat(dz)
        G[key][j] = _mm(f"g_{key}{l}", _flat(r["xn"]), dzf, ta=True, out_dtype=BF16)
        dxn = _mm(f"d_xn{l}", dzf, w_in, tb=True, out_dtype=BF16).reshape(B, L, D)
        if l == 0 and early_grads is not None:
            smp = [mods[0] + early_grads(2, G)]
        if l == 0:
            (ds,), (dm[0][0], dm[0][1]), (G["norm_mix"][0],) = r["pre1"].bwd([dxn], {0: F32}, grad_glob=(0,),
                                                                            add={0: ds0}, samp=smp)
        else:
            ds, dxn_next = ds0, dxn
    dmods = jnp.stack([jnp.concatenate(d, axis=2) for d in dm])
    return loss, ds[:, :n], dmods, G


def kernel(x, c, ctx, c_ctx, ada_w, ada_b, norm_mix, norm_ffn, norm_final, ffn_in, ffn_out, ab_in, ab_out, mla_q_norm, mla_kv_norm, mla_wq_b, mla_wkv_b, cmlp_v_norm, cmlp_ws, cmlp_bs, cd_in, cd_out, ret_decay_fwd, ret_decay_bwd, ret_norm, swa_sink, loss_target, m_c_ctx, m_ada_w, m_ada_b, m_norm_mix, m_norm_ffn, m_norm_final, m_ffn_in, m_ffn_out, m_ab_in, m_ab_out, m_mla_q_norm, m_mla_kv_norm, m_mla_wq_b, m_mla_wkv_b, m_cmlp_v_norm, m_cmlp_ws, m_cmlp_bs, m_cd_in, m_cd_out, m_ret_decay_fwd, m_ret_decay_bwd, m_ret_norm, m_swa_sink, v_c_ctx, v_ada_w, v_ada_b, v_norm_mix, v_norm_ffn, v_norm_final, v_ffn_in, v_ffn_out, v_ab_in, v_ab_out, v_mla_q_norm, v_mla_kv_norm, v_mla_wq_b, v_mla_wkv_b, v_cmlp_v_norm, v_cmlp_ws, v_cmlp_bs, v_cd_in, v_cd_out, v_ret_decay_fwd, v_ret_decay_bwd, v_ret_norm, v_swa_sink):
    B, n, D = x.shape
    NL = ada_w.shape[0]
    NE, NO = ab_in.shape[0], cd_in.shape[0]
    me = _my_index()
    weights = dict(c_ctx=c_ctx, ada_w=ada_w, ada_b=ada_b, norm_mix=norm_mix, norm_ffn=norm_ffn, norm_final=norm_final,
                   ffn_in=ffn_in, ffn_out=ffn_out, ab_in=ab_in, ab_out=ab_out, mla_q_norm=mla_q_norm,
                   mla_kv_norm=mla_kv_norm, mla_wq_b=mla_wq_b, mla_wkv_b=mla_wkv_b, cmlp_v_norm=cmlp_v_norm,
                   cmlp_ws=cmlp_ws, cmlp_bs=cmlp_bs, cd_in=cd_in, cd_out=cd_out, ret_decay_fwd=ret_decay_fwd,
                   ret_decay_bwd=ret_decay_bwd, ret_norm=ret_norm, swa_sink=swa_sink)
    moments_m = dict(c_ctx=m_c_ctx, ada_w=m_ada_w, ada_b=m_ada_b, norm_mix=m_norm_mix, norm_ffn=m_norm_ffn,
                     norm_final=m_norm_final, ffn_in=m_ffn_in, ffn_out=m_ffn_out, ab_in=m_ab_in, ab_out=m_ab_out,
                     mla_q_norm=m_mla_q_norm, mla_kv_norm=m_mla_kv_norm, mla_wq_b=m_mla_wq_b, mla_wkv_b=m_mla_wkv_b,
                     cmlp_v_norm=m_cmlp_v_norm, cmlp_ws=m_cmlp_ws, cmlp_bs=m_cmlp_bs, cd_in=m_cd_in, cd_out=m_cd_out,
                     ret_decay_fwd=m_ret_decay_fwd, ret_decay_bwd=m_ret_decay_bwd, ret_norm=m_ret_norm,
                     swa_sink=m_swa_sink)
    moments_v = dict(c_ctx=v_c_ctx, ada_w=v_ada_w, ada_b=v_ada_b, norm_mix=v_norm_mix, norm_ffn=v_norm_ffn,
                     norm_final=v_norm_final, ffn_in=v_ffn_in, ffn_out=v_ffn_out, ab_in=v_ab_in, ab_out=v_ab_out,
                     mla_q_norm=v_mla_q_norm, mla_kv_norm=v_mla_kv_norm, mla_wq_b=v_mla_wq_b, mla_wkv_b=v_mla_wkv_b,
                     cmlp_v_norm=v_cmlp_v_norm, cmlp_ws=v_cmlp_ws, cmlp_bs=v_cmlp_bs, cd_in=v_cd_in, cd_out=v_cd_out,
                     ret_decay_fwd=v_ret_decay_fwd, ret_decay_bwd=v_ret_decay_bwd, ret_norm=v_ret_norm,
                     swa_sink=v_swa_sink)
    order = list(weights)

    Ns = ada_w.shape[2]
    (c_g,) = _all_gather("gather_c", [c])
    R = N_DEV * B + 8
    c_all = jnp.concatenate([c_g.reshape(N_DEV * B, D), jnp.broadcast_to(c_ctx[None], (8, D))], axis=0)
    ada_b_mine = lax.dynamic_slice_in_dim(ada_b, me * Ns, Ns, axis=1)[:, None, :]
    mods_shard = _ada_fwd(c_all, ada_w, ada_b_mine)
    (mods_g,) = _all_gather("gather_mods", [mods_shard])
    mods_full = jnp.transpose(mods_g, (1, 2, 0, 3)).reshape(NL, R, 6, D)
    mx = lax.dynamic_slice_in_dim(mods_full, me * B, B, axis=1)
    mh = jnp.broadcast_to(mods_full[:, N_DEV * B][:, None], (NL, B, 6, D))
    mods = jnp.stack([mx, mh], axis=2)

    big = ["ffn_in", "ffn_out", "ab_in", "ab_out", "cd_in", "cd_out", "mla_wq_b", "mla_wkv_b"]
    col_sharded = {"ffn_in", "ab_in", "cd_in", "mla_wq_b", "mla_wkv_b"}
    shards = {k: _to_bf16("cast_" + k, weights[k]) for k in big}
    first = {k: 0 if k.startswith("cd_") else 1 for k in big}
    a_keys = [k for k in big if first[k] and not k.startswith("ffn_")]
    b_keys = ["ffn_in", "ffn_out"]
    early = _all_gather("gather_wA", [shards[k][:1] for k in a_keys])
    (rn_g,) = _all_gather("gather_ret_norm", [ret_norm])
    small_first = [mods, rn_g] + list(early)
    w_groups = [
        ("B", {k: (0, 1) for k in b_keys}),
        ("C1", {"ffn_in": (1, 2), "ffn_out": (1, 2), "cd_in": (0, 1), "cd_out": (0, 1)}),
        ("C2", {k: (2, NL) if k.startswith("ffn_") else (1, weights[k].shape[0]) for k in big}),
    ]
    w_handles, w_after = [], small_first
    for gname, members in w_groups:
        handle, token = _exchange_start(f"gather_w{gname}_start", [shards[k][lo:hi] for k, (lo, hi) in members.items()],
                                        False, w_after)
        w_handles.append(handle)
        w_after = w_after + [token]
        mods = mods + token[0, 0]

    def unshard(k, g):
        if k in col_sharded:
            f = jnp.transpose(g, (1, 2, 0, 3)).reshape(g.shape[1], g.shape[2], -1)
        else:
            f = jnp.transpose(g, (1, 0, 2, 3)).reshape(g.shape[1], -1, g.shape[3])
        return [f[i] for i in range(f.shape[0])]

    rn_full = jnp.transpose(rn_g, (1, 0, 2)).reshape(NO, -1)
    small_p = dict(weights, ret_norm=rn_full)
    full0 = {k: [None] * weights[k].shape[0] for k in big}
    for k, g in zip(a_keys, early):
        full0[k][:1] = unshard(k, g)

    def later_weights(stage, newest):
        gname, members = w_groups[stage]
        landed = _exchange_wait(f"gather_w{gname}_wait", w_handles[stage], newest)
        landed = _fill_own(f"own_w{gname}", landed, [shards[k][lo:hi] for k, (lo, hi) in members.items()], False)
        for (k, (lo, hi)), land in zip(members.items(), landed):
            full0[k][lo:hi] = unshard(k, land)
        return _layer_weights(full0, small_p)

    def to_slots(k, gl):
        g = jnp.stack(gl)
        if k == "ffn_in":
            half = N_DEV // 2
            g = g.reshape(g.shape[0], 2, g.shape[2], half, -1)
            return jnp.transpose(g, (1, 3, 0, 2, 4)).reshape(N_DEV, g.shape[0], g.shape[2], -1)
        if k in col_sharded:
            return jnp.transpose(g.reshape(g.shape[0], g.shape[1], N_DEV, -1), (2, 0, 1, 3))
        return jnp.transpose(g.reshape(g.shape[0], N_DEV, -1, g.shape[2]), (1, 0, 2, 3))

    def big_grads(G):
        return dict(ffn_in=[g if g is None else _ffn_deinterleave(g) for g in G["ffn_in"]], ffn_out=G["ffn_out"],
                    ab_in=[g if g is None else _ab_unperm(g) for g in G["ab_in"]],
                    ab_out=G["ab_out"], cd_in=G["cd_in"], cd_out=G["cd_out"],
                    mla_wq_b=[g if g is None else _wq_unperm(g) for g in G["wq"]], mla_wkv_b=G["wkv"])

    sent = {}

    def early_grads(stage, G):
        parts = big_grads(G)
        if stage == 0:
            srcs = [to_slots(k, parts[k][first[k]:]) for k in big]
            handle, g_token = _exchange_start("scatter_gC_start", srcs, slotted=True)
        elif stage == 1:
            srcs = [to_slots(k, parts[k][:1]) for k in b_keys]
            handle, g_token = _exchange_start("scatter_gB_start", srcs, slotted=True)
        else:
            srcs = [to_slots(k, parts[k][:1]) for k in a_keys]
            handle, g_token = _exchange_start("scatter_gA_start", srcs, slotted=True)
        sent[stage] = (handle, srcs)
        return g_token[0, 0]

    loss_part, grad_x, dmods, G = _local_step(x, ctx, loss_target, mods, _layer_weights(full0, small_p),
                                              later_weights, early_grads)

    dmx = dmods[:, :, 0].reshape(NL, B, 6 * D)
    dmh = jnp.sum(dmods[:, :, 1], axis=1).reshape(NL, 1, 6 * D)
    (dm_g,) = _all_gather("gather_dmods", [jnp.concatenate([dmx, dmh], axis=1)])
    dmx_all = jnp.transpose(dm_g[:, :, :B], (1, 0, 2, 3)).reshape(NL, N_DEV * B, 6 * D)
    dmh_all = jnp.sum(dm_g[:, :, B], axis=0)
    dm_rows = jnp.concatenate([dmx_all, dmh_all[:, None], jnp.zeros((NL, 7, 6 * D), F32)], axis=1)
    g_ada_b = jnp.sum(dm_rows, axis=1)
    dm_mine = lax.dynamic_slice_in_dim(dm_rows, me * Ns, Ns, axis=2)
    g_ada_w, dcond = _ada_bwd(c_all, ada_w, dm_mine)
    sg = jax.nn.sigmoid(c_ctx)
    d_c_ctx_part = jnp.sum(dcond[:, N_DEV * B], axis=0) * (sg * (1.0 + c_ctx * (1.0 - sg)))

    def cat(parts):
        return jnp.concatenate([p.reshape(-1) for p in parts])

    dlg = jnp.stack([jnp.sum(G["lg"][j], axis=0) for j in range(NO)])
    sig_f, sig_b = jax.nn.sigmoid(-ret_decay_fwd), jax.nn.sigmoid(-ret_decay_bwd)
    small = dict(
        loss=loss_part[0, 0:1],
        c_ctx=d_c_ctx_part,
        norm_mix=cat(G["norm_mix"]), norm_ffn=cat(G["norm_ffn"]), norm_final=G["norm_final"].reshape(-1),
        mla_q_norm=cat(G["q_norm"]), mla_kv_norm=cat(G["kv_norm"]),
        cmlp_v_norm=cat([cat(G["v_norm"][j]) for j in range(NE)]),
        cmlp_ws=cat([jnp.stack(G["ws"][j]) for j in range(NE)]),
        cmlp_bs=cat([jnp.stack([b_[:, 0] for b_ in G["bs"][j]]) for j in range(NE)]),
        ret_decay_fwd=(dlg[:, 0] * sig_f).reshape(-1), ret_decay_bwd=(dlg[:, 1] * sig_b).reshape(-1),
        ret_norm=cat([cat(G["ret_norm"][j]) for j in range(NO)]),
        swa_sink=cat([jnp.sum(G["sink"][j], axis=0) for j in range(NO)]),
    )
    small_keys = list(small)
    sizes = [small[k].shape[0] for k in small_keys]
    total = sum(sizes)
    padded = -(-total // 2048) * 2048
    packed = jnp.concatenate([small[k] for k in small_keys] + [jnp.zeros((padded - total,), F32)]).reshape(-1, 128)
    (small_g,) = _all_gather("gather_small", [packed])

    landed_a = _fill_own("own_gA", _exchange_wait("scatter_gA_wait", sent[2][0], small_g), sent[2][1], True)
    landed0 = dict(zip(a_keys, landed_a))
    landed_b = _fill_own("own_gB", _exchange_wait("scatter_gB_wait", sent[1][0], grad_x), sent[1][1], True)
    landed0.update(zip(b_keys, landed_b))
    landed_c = _fill_own("own_gC", _exchange_wait("scatter_gC_wait", sent[0][0], grad_x), sent[0][1], True)
    landed = [[landed0[k], rest] if first[k] else [rest] for k, rest in zip(big, landed_c)]

    grads, deltas, new_m, new_v = {}, {}, {}, {}
    for k, land in zip(big, landed):
        grads[k], deltas[k], new_m[k], new_v[k] = _adamw_from_slots("adamw_" + k, weights[k], moments_m[k], moments_v[k], land)
    deltas["ada_w"], new_m["ada_w"], new_v["ada_w"] = [
        o.reshape(ada_w.shape) for o in _adamw("adamw_ada_w", _flat(ada_w), _flat(g_ada_w), _flat(m_ada_w), _flat(v_ada_w))]
    grads["ada_w"] = g_ada_w

    sums_only = ("loss", "ret_norm")

    def packed_of(src, fill):
        vals = [jnp.full((sizes[i],), fill, F32) if k in sums_only else src[k].reshape(-1)
                for i, k in enumerate(small_keys)]
        return jnp.concatenate(vals + [jnp.full((padded - total,), fill, F32)]).reshape(-1, 128)

    w_p, m_p, v_p = packed_of(weights, 0.0), packed_of(moments_m, 0.0), packed_of(moments_v, 1.0)

    def f_small(w_, m_, v_, land_):
        g = _sum_slots(land_)
        return (g,) + _adamw_math(w_, g, m_, v_)

    g_p, d_p, nm_p, nv_p = _ew("adamw_small", f_small, [w_p, m_p, v_p, small_g], [F32] * 4)
    offs = np.cumsum([0] + sizes)
    for i, k in enumerate(small_keys):
        sl = slice(int(offs[i]), int(offs[i + 1]))
        if k == "loss":
            loss = g_p.reshape(-1)[int(offs[i])]
        elif k == "ret_norm":
            g_full = g_p.reshape(-1)[sl].reshape(NO, -1)
            g_mine = lax.dynamic_slice_in_dim(g_full, me * ret_norm.shape[1], ret_norm.shape[1], axis=1)
            d_, m_, v_ = _adamw("adamw_ret_norm", *[jnp.pad(a, ((0, 8 - NO), (0, 128 - a.shape[1])), constant_values=cv)
                                                     for a, cv in ((ret_norm, 0.0), (g_mine, 0.0), (m_ret_norm, 0.0), (v_ret_norm, 1.0))])
            grads[k] = g_mine
            deltas[k], new_m[k], new_v[k] = [a[:NO, :ret_norm.shape[1]] for a in (d_, m_, v_)]
        else:
            shp = weights[k].shape
            grads[k], deltas[k], new_m[k], new_v[k] = [a.reshape(-1)[sl].reshape(shp) for a in (g_p, d_p, nm_p, nv_p)]
    pad_b = lambda a, cv=0.0: jnp.pad(a, ((0, 8 - NL), (0, 0)), constant_values=cv)
    d_, m_, v_ = _adamw("adamw_ada_b", pad_b(ada_b), pad_b(g_ada_b), pad_b(m_ada_b), pad_b(v_ada_b, 1.0))
    grads["ada_b"] = g_ada_b
    deltas["ada_b"], new_m["ada_b"], new_v["ada_b"] = d_[:NL], m_[:NL], v_[:NL]

    return (loss, grad_x, *[grads[k] for k in order], *[deltas[k] for k in order],
            *[new_m[k] for k in order], *[new_v[k] for k in order])
```

```python
import functools

import numpy as np
import jax
import jax.numpy as jnp
from jax import lax
from jax.experimental import pallas as pl
from jax.experimental.pallas import tpu as pltpu

F32 = jnp.float32
BF16 = jnp.bfloat16
EPS = 1e-6
NEG_INF = -1e30
N_DEV = 8
GRID_W = 64
ROPE_THETA = 10000.0
ROPE_DIM = 64
SWA_WINDOW = 128
MLA_SCALE = (128 + 64) ** -0.5
SWA_SCALE = 64 ** -0.5
RET_K_SCALE = 64 ** -0.5
ADAM_LR, ADAM_B1, ADAM_B2, ADAM_EPS, ADAM_WD, ADAM_STEP = 0.001, 0.9, 0.999, 1e-08, 0.01, 10
V7X_VMEM_LIMIT = 56 * 1024 * 1024
ROW_BLOCK_BUDGET = 24 * 1024 * 1024
MESH = pl.DeviceIdType.MESH


def _pallas(body, **kw):
    return pl.pallas_call(body, **kw)


def _params(sem=None):
    return pltpu.CompilerParams(dimension_semantics=sem, vmem_limit_bytes=V7X_VMEM_LIMIT)


def _tile(n, cap, align):
    best = None
    for t in range(align, min(n, cap) + 1, align):
        if n % t == 0:
            best = t
    return n if best is None else best


def _sds(shape, dtype):
    return jax.ShapeDtypeStruct(tuple(shape), dtype)


def _ew(name, f, ins, out_dtypes, cap_elems=131072):
    R, C = ins[0].shape[-2:]
    tr = _tile(R, max(16, cap_elems // C), 16)
    n_in = len(ins)

    def spec(a):
        if a.ndim == 2:
            return pl.BlockSpec((tr, C), lambda i: (i, 0))
        return pl.BlockSpec((a.shape[0], tr, C), lambda i: (0, i, 0))

    def body(*refs):
        outs = f(*[r[...] for r in refs[:n_in]])
        for r, o in zip(refs[n_in:], outs):
            r[...] = o.astype(r.dtype)

    return _pallas(
        body, name=name, grid=(R // tr,), in_specs=[spec(a) for a in ins],
        out_specs=[pl.BlockSpec((tr, C), lambda i: (i, 0)) for _ in out_dtypes],
        out_shape=[_sds((R, C), d) for d in out_dtypes], compiler_params=_params(("parallel",)),
    )(*ins)


def _to_bf16(name, w):
    w2 = w.reshape(-1, w.shape[-1])
    return _ew(name, lambda v: (v,), [w2], [BF16])[0].reshape(w.shape)


def _adamw_math(w, g, m, v):
    m = ADAM_B1 * m + (1.0 - ADAM_B1) * g
    v = ADAM_B2 * v + (1.0 - ADAM_B2) * (g * g)
    m_hat = m / (1.0 - ADAM_B1 ** ADAM_STEP)
    v_hat = v / (1.0 - ADAM_B2 ** ADAM_STEP)
    delta = -ADAM_LR * (m_hat / (jnp.sqrt(v_hat) + ADAM_EPS) + ADAM_WD * w)
    return delta, m, v


def _sum_slots(land):
    g = land[0].astype(F32)
    for s in range(1, land.shape[0]):
        g = g + land[s].astype(F32)
    return g


def _adamw_from_slots(name, w, m, v, lands):
    nl, K, C = w.shape
    tr = _tile(K, max(16, 65536 // C), 16)
    per = K // tr
    starts = np.cumsum([0] + [ld.shape[1] * per for ld in lands])
    ng = len(lands)

    def land_spec(g):
        lo, hi = int(starts[g]), int(starts[g + 1])
        return pl.BlockSpec((N_DEV, tr, C), lambda t: (0, jnp.clip(t, lo, hi - 1) - lo, 0))

    def body(*refs):
        w_ref, m_ref, v_ref = refs[:3]
        land_refs, out_refs = refs[3:3 + ng], refs[3 + ng:]
        t = pl.program_id(0)
        for g in range(ng):
            @pl.when((t >= int(starts[g])) & (t < int(starts[g + 1])))
            def _(g=g):
                grad = _sum_slots(land_refs[g][...])
                for r, o in zip(out_refs, (grad,) + _adamw_math(w_ref[...], grad, m_ref[...], v_ref[...])):
                    r[...] = o

    row = pl.BlockSpec((tr, C), lambda t: (t, 0))
    outs = _pallas(
        body, name=name, grid=(nl * per,), in_specs=[row] * 3 + [land_spec(g) for g in range(ng)],
        out_specs=[row] * 4, out_shape=[_sds((nl * K, C), F32)] * 4, compiler_params=_params(("parallel",)),
    )(w.reshape(-1, C), m.reshape(-1, C), v.reshape(-1, C), *[ld.reshape(N_DEV, -1, C) for ld in lands])
    return [o.reshape(w.shape) for o in outs]


def _adamw(name, w, g, m, v):
    outs = _ew(name, lambda w_, g_, m_, v_: _adamw_math(w_, g_, m_, v_), [w, g, m, v], [F32] * 3)
    return outs


def _mm(name, a, b, ta=False, tb=False, out_dtype=F32, add=None):
    M, K = (a.shape[1], a.shape[0]) if ta else a.shape
    N = b.shape[0] if tb else b.shape[1]
    tm = _tile(M, 1408, 128)
    tn = _tile(N, 1024, 128)
    if tn < 256 and N <= 2432:
        tn = N
    tk = _tile(K, 3072, 128)
    nk = K // tk
    a_spec = pl.BlockSpec((tk, tm), lambda i, j, k: (k, i)) if ta else pl.BlockSpec((tm, tk), lambda i, j, k: (i, k))
    b_spec = pl.BlockSpec((tn, tk), lambda i, j, k: (j, k)) if tb else pl.BlockSpec((tk, tn), lambda i, j, k: (k, j))
    o_spec = pl.BlockSpec((tm, tn), lambda i, j, k: (i, j))
    dims = (((0 if ta else 1,), (1 if tb else 0,)), ((), ()))
    has_add = add is not None

    def product(a_ref, b_ref):
        return lax.dot_general(a_ref[...].astype(BF16), b_ref[...].astype(BF16), dims, preferred_element_type=F32)

    def body_single(*refs):
        acc = product(refs[0], refs[1])
        if has_add:
            acc = acc + refs[2][...]
        refs[-1][...] = acc.astype(refs[-1].dtype)

    def body(*refs):
        a_ref, b_ref = refs[0], refs[1]
        add_ref = refs[2] if has_add else None
        o_ref, acc = refs[-2], refs[-1]
        k = pl.program_id(2)

        @pl.when(k == 0)
        def _():
            acc[...] = add_ref[...] if has_add else jnp.zeros_like(acc)

        acc[...] += product(a_ref, b_ref)

        @pl.when(k == nk - 1)
        def _():
            o_ref[...] = acc[...].astype(o_ref.dtype)

    ins = [a, b] + ([add] if has_add else [])
    specs = [a_spec, b_spec] + ([o_spec] if has_add else [])
    return _pallas(
        body_single if nk == 1 else body, name=name, grid=(M // tm, N // tn, nk), in_specs=specs, out_specs=o_spec,
        out_shape=_sds((M, N), out_dtype), scratch_shapes=[] if nk == 1 else [pltpu.VMEM((tm, tn), F32)],
        compiler_params=_params(("parallel", "parallel", "arbitrary")),
    )(*ins)


def _mm_pair(name, a1, a2, b, out_dtype):
    M, Kh = a1.shape
    N = b.shape[1]
    tm = _tile(M, 1408, 128)

    def body(a1_ref, a2_ref, b1_ref, b2_ref, o_ref):
        o_ref[...] = (_dot(a1_ref[...].astype(BF16), b1_ref[...].astype(BF16))
                      + _dot(a2_ref[...].astype(BF16), b2_ref[...].astype(BF16))).astype(o_ref.dtype)

    a_spec = pl.BlockSpec((tm, Kh), lambda i: (i, 0))
    return _pallas(
        body, name=name, grid=(M // tm,),
        in_specs=[a_spec, a_spec, pl.BlockSpec((Kh, N), lambda i: (0, 0)), pl.BlockSpec((Kh, N), lambda i: (1, 0))],
        out_specs=pl.BlockSpec((tm, N), lambda i: (i, 0)), out_shape=_sds((M, N), out_dtype),
        compiler_params=_params(("parallel",)),
    )(a1, a2, b, b)


def _ffn_tile(F):
    return F // 2 if (F // 2) % 128 == 0 else F


def _ffn_interleave(w):
    D, F2 = w.shape
    T = _ffn_tile(F2 // 2)
    nj = F2 // (2 * T)

    def body(a_ref, b_ref, o_ref):
        o_ref[:, :T] = a_ref[...]
        o_ref[:, T:] = b_ref[...]

    return _pallas(
        body, name="ffn_interleave", grid=(nj,),
        in_specs=[pl.BlockSpec((D, T), lambda j: (0, j)), pl.BlockSpec((D, T), lambda j: (0, j + nj))],
        out_specs=pl.BlockSpec((D, 2 * T), lambda j: (0, j)), out_shape=_sds((D, F2), w.dtype),
        compiler_params=_params(("parallel",)),
    )(w, w)


def _ffn_deinterleave(w):
    D, F2 = w.shape
    T = _ffn_tile(F2 // 2)
    nj = F2 // (2 * T)

    def body(w_ref, o_ref):
        o_ref[0] = w_ref[:, :T]
        o_ref[1] = w_ref[:, T:]

    return _pallas(
        body, name="ffn_deinterleave", grid=(nj,), in_specs=[pl.BlockSpec((D, 2 * T), lambda j: (0, j))],
        out_specs=pl.BlockSpec((2, D, T), lambda j: (0, 0, j)), out_shape=_sds((2, D, F2 // 2), w.dtype),
        compiler_params=_params(("parallel",)),
    )(w)


def _ffn_specs(M, D, F):
    T = _ffn_tile(F)
    tm = _tile(M, 512, 128)
    x_spec = pl.BlockSpec((tm, D), lambda i, j: (i, 0))
    wi_spec = pl.BlockSpec((D, 2 * T), lambda i, j: (0, j))
    wo_spec = pl.BlockSpec((T, D), lambda i, j: (j, 0))
    return T, tm, F // T, x_spec, wi_spec, wo_spec


def _ffn_fwd(name, xn, w_in, w_out):
    M, D = xn.shape
    T, tm, nj, x_spec, wi_spec, wo_spec = _ffn_specs(M, D, w_out.shape[0])

    def body(x_ref, wi_ref, wo_ref, y_ref, acc):
        j = pl.program_id(1)
        z = _dot(x_ref[...], wi_ref[...])
        hid = (jax.nn.silu(z[:, :T]) * z[:, T:]).astype(BF16)
        part = _dot(hid, wo_ref[...])

        @pl.when(j == 0)
        def _():
            acc[...] = part

        @pl.when(j > 0)
        def _():
            acc[...] += part

        @pl.when(j == nj - 1)
        def _():
            y_ref[...] = acc[...].astype(y_ref.dtype)

    return _pallas(
        body, name=name, grid=(M // tm, nj), in_specs=[x_spec, wi_spec, wo_spec], out_specs=x_spec,
        out_shape=_sds((M, D), BF16), scratch_shapes=[pltpu.VMEM((tm, D), F32)],
        compiler_params=_params(("parallel", "arbitrary")),
    )(xn, w_in, w_out)


def _ffn_mid_bwd(name, xn, dy, w_in, w_out):
    M, D = xn.shape
    F = w_out.shape[0]
    T, tm, nj, x_spec, wi_spec, wo_spec = _ffn_specs(M, D, F)

    def body(x_ref, dy_ref, wi_ref, wo_ref, h_ref, dz_ref):
        z = _dot(x_ref[...], wi_ref[...])
        a, b = z[:, :T], z[:, T:]
        dh = _dot_nt(dy_ref[...], wo_ref[...])
        sig = jax.nn.sigmoid(a)
        act = a * sig
        h_ref[...] = (act * b).astype(BF16)
        dz_ref[:, :T] = (dh * b * (sig * (1.0 + a * (1.0 - sig)))).astype(BF16)
        dz_ref[:, T:] = (dh * act).astype(BF16)

    return _pallas(
        body, name=name, grid=(M // tm, nj), in_specs=[x_spec, x_spec, wi_spec, wo_spec],
        out_specs=[pl.BlockSpec((tm, T), lambda i, j: (i, j)),
                   pl.BlockSpec((tm, 2 * T), lambda i, j: (i, j))],
        out_shape=[_sds((M, F), BF16), _sds((M, 2 * F), BF16)],
        compiler_params=_params(("parallel", "parallel")),
    )(xn, dy, w_in, w_out)


class _Rows:
    def __init__(self, name, f, B, L, n, tm, arrays, pieces, samp, samp_pieces, glob, out_arrays, out_pieces):
        self.name, self.f, self.B, self.L, self.n, self.tm = name, f, B, L, n, tm
        self.arrays, self.pieces, self.samp, self.samp_pieces, self.glob = arrays, pieces, samp, samp_pieces, glob
        self.out_arrays, self.out_pieces = out_arrays, out_pieces
        self.nx = n // tm

    def _samples_per_step(self, row_bytes):
        for bb in (4, 2, 1):
            if self.B % bb == 0 and 2 * bb * self.tm * row_bytes <= ROW_BLOCK_BUDGET:
                return bb
        return 1

    def _row_spec(self, bb, C, batched=True):
        tm = self.tm
        if batched:
            return pl.BlockSpec((bb, tm, C), lambda g, i: (g, i, 0))
        return pl.BlockSpec((tm, C), lambda g, i: (i, 0))

    def _in_specs(self, bb, unread=()):
        nx, tm = self.nx, self.tm
        specs = [self._row_spec(bb, a.shape[-1], a.ndim == 3) for a in self.arrays]
        for ai in unread:
            specs[ai] = pl.BlockSpec((bb, tm, self.arrays[ai].shape[-1]), lambda g, i: (0, 0, 0))
        specs += [pl.BlockSpec((bb, None) + s.shape[2:], lambda g, i: (g, i // nx, 0, 0)) for s in self.samp]
        specs += [pl.BlockSpec(g.shape, lambda g_, i, nd=g.ndim: (0,) * nd) for g in self.glob]
        return specs

    def _load(self, b, a_refs, s_refs, g_refs):
        args = [(a_refs[ai][b, :, cs:cs + cw] if self.arrays[ai].ndim == 3 else a_refs[ai][:, cs:cs + cw]).astype(F32)
                for ai, cs, cw in self.pieces]
        args += [s_refs[si][b, r:r + 1, :].astype(F32) for si, r in self.samp_pieces]
        args += [g[...].astype(F32) for g in g_refs]
        return args

    @staticmethod
    def _row_bytes(arrays):
        return sum(a.shape[-1] * a.dtype.itemsize for a in arrays if a.ndim == 3)

    def fwd(self):
        na, ns, ng = len(self.arrays), len(self.samp), len(self.glob)
        bb = self._samples_per_step(self._row_bytes(self.arrays)
                                    + sum(C * jnp.dtype(d).itemsize for C, d in self.out_arrays))

        def body(*refs):
            a_refs, s_refs, g_refs = refs[:na], refs[na:na + ns], refs[na + ns:na + ns + ng]
            o_refs = refs[na + ns + ng:]
            for b in range(bb):
                outs = self.f(*self._load(b, a_refs, s_refs, g_refs))
                for (oi, cs, cw), o in zip(self.out_pieces, outs):
                    o_refs[oi][b, :, cs:cs + cw] = o.astype(o_refs[oi].dtype)

        return _pallas(
            body, name=self.name + "_fwd", grid=(self.B // bb, self.L // self.tm), in_specs=self._in_specs(bb),
            out_specs=[self._row_spec(bb, C) for C, _ in self.out_arrays],
            out_shape=[_sds((self.B, self.L, C), d) for C, d in self.out_arrays],
            compiler_params=_params(("parallel", "parallel")),
        )(*self.arrays, *self.samp, *self.glob)

    def bwd(self, cts, grad_arrays, grad_glob=(), add=None, samp=None, unread=()):
        samp = self.samp if samp is None else samp
        na, ns, ng, nc = len(self.arrays), len(self.samp), len(self.glob), len(cts)
        ct_off = [c[1] if isinstance(c, tuple) else 0 for c in cts]
        cts = [c[0] if isinstance(c, tuple) else c for c in cts]
        add = add or {}
        add_keys = list(add)
        g_idx = list(grad_arrays)
        nx, B = self.nx, self.B
        n_in = na + ns + ng + nc + len(add_keys)
        n_pieces, n_sp = len(self.pieces), len(self.samp_pieces)
        streamed = list(self.arrays) + cts + [add[k] for k in add_keys]
        bb = self._samples_per_step(self._row_bytes(streamed) + sum(
            self.arrays[ai].shape[-1] * jnp.dtype(grad_arrays[ai]).itemsize for ai in g_idx))

        def body(*refs):
            a_refs, s_refs, g_refs = refs[:na], refs[na:na + ns], refs[na + ns:na + ns + ng]
            c_refs = refs[na + ns + ng:na + ns + ng + nc]
            add_refs = refs[na + ns + ng + nc:n_in]
            d_refs = refs[n_in:n_in + len(g_idx)]
            ds_refs = refs[n_in + len(g_idx):n_in + len(g_idx) + n_sp]
            dg_refs = refs[n_in + len(g_idx) + n_sp:]
            g_step, i = pl.program_id(0), pl.program_id(1)

            @pl.when((i == 0) | (i == nx))
            def _():
                for r in ds_refs:
                    r[...] = jnp.zeros_like(r)

            @pl.when((g_step == 0) & (i == 0))
            def _():
                for r in dg_refs:
                    r[...] = jnp.zeros_like(r)

            for b in range(bb):
                args = self._load(b, a_refs, s_refs, g_refs)
                _, vjp = jax.vjp(lambda *xs: tuple(self.f(*xs)), *args)
                grads = vjp(tuple(c_refs[oi][b, :, ct_off[oi] + cs:ct_off[oi] + cs + cw].astype(F32)
                                  for oi, cs, cw in self.out_pieces))
                for k, ai in enumerate(g_idx):
                    covered = sum(cw for pa, _, cw in self.pieces if pa == ai)
                    if covered < self.arrays[ai].shape[-1]:
                        d_refs[k][b] = jnp.zeros(d_refs[k].shape[1:], d_refs[k].dtype)
                    for (pa, cs, cw), gr in zip(self.pieces, grads[:n_pieces]):
                        if pa == ai:
                            if ai in add:
                                gr = gr + add_refs[add_keys.index(ai)][b, :, cs:cs + cw]
                            d_refs[k][b, :, cs:cs + cw] = gr.astype(d_refs[k].dtype)
                for r, gr in zip(ds_refs, grads[n_pieces:n_pieces + n_sp]):
                    r[b] += gr
                for r, gi in zip(dg_refs, grad_glob):
                    r[...] += grads[n_pieces + n_sp + gi]

        in_specs = self._in_specs(bb, unread) + [self._row_spec(bb, c.shape[-1]) for c in cts]
        in_specs += [self._row_spec(bb, add[k].shape[-1]) for k in add_keys]
        out_specs = [self._row_spec(bb, self.arrays[ai].shape[-1]) for ai in g_idx]
        out_shape = [_sds(self.arrays[ai].shape, grad_arrays[ai]) for ai in g_idx]
        for si, _ in self.samp_pieces:
            C = self.samp[si].shape[-1]
            out_specs.append(pl.BlockSpec((bb, None, 1, C), lambda g, i: (g, i // nx, 0, 0)))
            out_shape.append(_sds((B, 2, 1, C), F32))
        for gi in grad_glob:
            g = self.glob[gi]
            out_specs.append(pl.BlockSpec(g.shape, lambda g_, i, nd=g.ndim: (0,) * nd))
            out_shape.append(_sds(g.shape, F32))
        outs = _pallas(
            body, name=self.name + "_bwd", grid=(B // bb, self.L // self.tm), in_specs=in_specs, out_specs=out_specs,
            out_shape=out_shape, compiler_params=_params(("arbitrary", "arbitrary")),
        )(*self.arrays, *samp, *self.glob, *cts, *[add[k] for k in add_keys])
        ng_ = len(g_idx)
        return outs[:ng_], outs[ng_:ng_ + n_sp], outs[ng_ + n_sp:]


@jax.custom_vjp
def _rms(x, g):
    return x * lax.rsqrt(jnp.mean(x * x, axis=-1, keepdims=True) + EPS) * g


def _rms_fwd(x, g):
    r = lax.rsqrt(jnp.mean(x * x, axis=-1, keepdims=True) + EPS)
    xh = x * r
    return xh * g, (xh, r, g)


def _rms_bwd(res, dy):
    xh, r, g = res
    dxh = dy * g
    dx = r * (dxh - xh * jnp.mean(dxh * xh, axis=-1, keepdims=True))
    return dx, jnp.sum(dy * xh, axis=0, keepdims=True)


_rms.defvjp(_rms_fwd, _rms_bwd)


@jax.custom_vjp
def _swap_halves(x):
    w = x.shape[-1]
    lane = lax.broadcasted_iota(jnp.int32, x.shape, x.ndim - 1)
    up = pltpu.roll(x, w - ROPE_DIM // 2, x.ndim - 1)
    down = pltpu.roll(x, ROPE_DIM // 2, x.ndim - 1)
    return jnp.where(lane % ROPE_DIM < ROPE_DIM // 2, up, down)


_swap_halves.defvjp(lambda x: (_swap_halves(x), None), lambda _, ct: (_swap_halves(ct),))


@jax.custom_vjp
def _bdot(a, b):
    return jnp.dot(a.astype(BF16), b.astype(BF16), preferred_element_type=F32)


_bdot.defvjp(lambda a, b: (_bdot(a, b), (a, b)),
             lambda res, ct: (_dot_nt(ct.astype(BF16), res[1].astype(BF16)), _dot_tn(res[0].astype(BF16), ct.astype(BF16))))


def _rope(x, cos2, sin2):
    return x * cos2 + _swap_halves(x) * sin2


def _f_pre(s, shift, scale, g):
    return (_rms(s, g) * (1.0 + scale) + shift,)


def _f_res(s, y, gate):
    return (s + gate * y,)


def _f_res_pre(s, y, gate, shift, scale, g):
    s1 = s + gate * y
    return s1, _rms(s1, g) * (1.0 + scale) + shift


def _f_ab_pre(*args):
    kv_lat, q_lat, kpe = args[0:3]
    us, vs = args[3:7], args[7:11]
    cos2, sin2 = args[11:13]
    kv_norm, q_norm = args[13:15]
    vns, wss, bss = args[15:19], args[19:23], args[23:27]
    outs = [_rms(kv_lat, kv_norm), _rms(q_lat, q_norm), _rope(kpe, cos2, sin2)]
    for u, v, vn, ws, bs in zip(us, vs, vns, wss, bss):
        vg = _rms(jax.nn.gelu(v), vn)
        mixed = _bdot(ws, vg) + bs
        outs.append(jax.nn.gelu(u) * mixed)
    return tuple(outs)


def _f_q_rope(qn, qr, cos2, sin2):
    return qn, _rope(qr, cos2, sin2)


def _f_cd_pre(rk, rv, sk, sv, rq, rg, sq, c256, s256, c128, s128, c512, s512):
    return (_rope(rq, c256, s256), _rope(rk * RET_K_SCALE, c256, s256), rv,
            _rope(sq, c512, s512), _rope(sk, c128, s128), sv, rg)


def _f_cd_merge(*args):
    ys, rgs, rns = args[0:4], args[4:8], args[8:12]
    return tuple(_rms(y, rn) * jax.nn.silu(rg) for y, rg, rn in zip(ys, rgs, rns))


def _dot_nt(a, b):
    return lax.dot_general(a, b, (((1,), (1,)), ((), ())), preferred_element_type=F32)


def _dot_tn(a, b):
    return lax.dot_general(a, b, (((0,), (0,)), ((), ())), preferred_element_type=F32)


def _dot(a, b):
    return jnp.dot(a, b, preferred_element_type=F32)


def _tile_spec(tq, C):
    return pl.BlockSpec((None, tq, C), lambda b, i: (b, i, 0))


def _full_spec(L, C):
    return pl.BlockSpec((None, L, C), lambda b, i: (b, 0, 0))


def _mla_half(h):
    lane = lax.broadcasted_iota(jnp.int32, (1, 128), 1)
    return (lane < 64) if h % 2 == 0 else (lane >= 64)


def _mla_query(q_ref, h):
    pair = q_ref[:, 512 + 128 * (h // 2):640 + 128 * (h // 2)]
    return jnp.concatenate([q_ref[:, 128 * h:128 * h + 128], jnp.where(_mla_half(h), pair, jnp.zeros_like(pair))], axis=1)


def _mla_fill_keys(kcat, kv_ref, kpe_ref):
    kp = kpe_ref[...]
    for h in range(4):
        kcat[h, :, 0:128] = kv_ref[:, 256 * h:256 * h + 128]
        kcat[h, :, 128:256] = jnp.where(_mla_half(h), kp, jnp.zeros_like(kp))


def _mla_fwd(q, kv, kpe, n, tq):
    B, L, _ = q.shape

    def body(q_ref, kv_ref, kpe_ref, o_ref, lse_ref, kcat):
        i = pl.program_id(1)

        @pl.when(i == 0)
        def _():
            _mla_fill_keys(kcat, kv_ref, kpe_ref)

        def tile(keys):
            for h in range(4):
                s = _dot_nt(_mla_query(q_ref, h), kcat[h, keys, :]) * MLA_SCALE
                m = jnp.max(s, axis=1, keepdims=True)
                e = jnp.exp(s - m)
                l = jnp.sum(e, axis=1, keepdims=True)
                p = (e * (1.0 / l)).astype(BF16)
                o_ref[:, 128 * h:128 * h + 128] = _dot(p, kv_ref[keys, 256 * h + 128:256 * h + 256]).astype(o_ref.dtype)
                lse_ref[:, h:h + 1] = m + jnp.log(l)

        pl.when(i < n // tq)(functools.partial(tile, slice(0, L)))
        pl.when(i >= n // tq)(functools.partial(tile, slice(n, L)))

    return _pallas(
        body, name="mla_fwd", grid=(B, L // tq),
        in_specs=[_tile_spec(tq, 768), _full_spec(L, 1024), _full_spec(L, 128)],
        out_specs=[_tile_spec(tq, 512), _tile_spec(tq, 4)],
        out_shape=[_sds((B, L, 512), BF16), _sds((B, L, 4), F32)],
        scratch_shapes=[pltpu.VMEM((4, L, 256), BF16)],
        compiler_params=_params(("parallel", "arbitrary")),
    )(q, kv, kpe)


def _mla_bwd(q, kv, kpe, lse, do, n, tq):
    B, L, _ = q.shape

    def body(q_ref, kv_ref, kpe_ref, lse_ref, do_ref, dq_ref, dkv_ref, dkpe_ref, kcat):
        i = pl.program_id(1)

        @pl.when(i == 0)
        def _():
            dkv_ref[...] = jnp.zeros_like(dkv_ref)
            dkpe_ref[...] = jnp.zeros_like(dkpe_ref)
            _mla_fill_keys(kcat, kv_ref, kpe_ref)

        def tile(keys):
            rope_pair = None
            for h in range(4):
                qc, kc = _mla_query(q_ref, h), kcat[h, keys, :]
                v = kv_ref[keys, 256 * h + 128:256 * h + 256]
                p = jnp.exp(_dot_nt(qc, kc) * MLA_SCALE - lse_ref[:, h:h + 1])
                doh = do_ref[:, 128 * h:128 * h + 128].astype(BF16)
                dp = _dot_nt(doh, v)
                delta = jnp.sum(p * dp, axis=1, keepdims=True)
                ds = (p * (dp - delta) * MLA_SCALE).astype(BF16)
                dqc = _dot(ds, kc)
                dq_ref[:, 128 * h:128 * h + 128] = dqc[:, 0:128]
                if h % 2 == 0:
                    rope_pair = dqc[:, 128:256]
                else:
                    dq_ref[:, 512 + 128 * (h // 2):640 + 128 * (h // 2)] = jnp.where(_mla_half(h), dqc[:, 128:256], rope_pair)
                dkc = _dot_tn(ds, qc)
                dkv_ref[keys, 256 * h:256 * h + 128] += dkc[:, 0:128]
                dkpe_ref[keys, :] += dkc[:, 128:256]
                dkv_ref[keys, 256 * h + 128:256 * h + 256] += _dot_tn(p.astype(BF16), doh)

        pl.when(i < n // tq)(functools.partial(tile, slice(0, L)))
        pl.when(i >= n // tq)(functools.partial(tile, slice(n, L)))

    return _pallas(
        body, name="mla_bwd", grid=(B, L // tq),
        in_specs=[_tile_spec(tq, 768), _full_spec(L, 1024), _full_spec(L, 128), _tile_spec(tq, 4),
                  pl.BlockSpec((None, tq, 512), lambda b, i: (b, i, 0))],
        out_specs=[_tile_spec(tq, 768), _full_spec(L, 1024), _full_spec(L, 128)],
        out_shape=[_sds((B, L, 768), F32), _sds((B, L, 1024), F32), _sds((B, L, 128), F32)],
        scratch_shapes=[pltpu.VMEM((4, L, 256), BF16)],
        compiler_params=_params(("parallel", "arbitrary")),
    )(q, kv, kpe, lse, do)


def _swa_window(i, tq, n):
    W = min(tq + 2 * SWA_WINDOW, n)
    lo = jnp.clip(i * tq - SWA_WINDOW, 0, n - W)
    return pl.multiple_of(lo, 128), W


def _swa_mask(i, lo, tq, W, n):
    qpos = i * tq + lax.broadcasted_iota(jnp.int32, (tq, 1), 0)
    kpos = lo + lax.broadcasted_iota(jnp.int32, (1, W), 1)
    return (jnp.abs(qpos - kpos) <= SWA_WINDOW) & (qpos < n)


def _swa_fill(rep, src_ref):
    for g in range(2):
        rep[:, 256 * g:256 * g + 256] = jnp.concatenate([src_ref[:, 64 * g:64 * g + 64]] * 4, axis=1)


def _swa_head_rows(a):
    lane = lax.broadcasted_iota(jnp.int32, (1, 256), 1)
    return jnp.concatenate([jnp.where(lane // 64 == h, a, jnp.zeros_like(a)) for h in range(4)], axis=0)


def _swa_fold_rows(a, tq):
    lane = lax.broadcasted_iota(jnp.int32, (1, 256), 1)
    out = jnp.where(lane // 64 == 0, a[0:tq], 0.0)
    for h in range(1, 4):
        out = jnp.where(lane // 64 == h, a[h * tq:(h + 1) * tq], out)
    return out


def _swa_fwd(q, k, v, sink, n, tq):
    B, L, _ = q.shape

    def body(q_ref, k_ref, v_ref, sink_ref, o_ref, lse_ref, krep, vrep):
        i = pl.program_id(1)

        @pl.when(i == 0)
        def _():
            _swa_fill(krep, k_ref)
            _swa_fill(vrep, v_ref)

        def tile(band):
            for g in range(2):
                gc = slice(256 * g, 256 * g + 256)
                qs = _swa_head_rows(q_ref[:, gc])
                sk = jnp.concatenate([jnp.broadcast_to(sink_ref[0:1, 4 * g + h:4 * g + h + 1], (tq, 1))
                                      for h in range(4)], axis=0)
                s2 = _dot_nt(qs, krep[n:L, gc]) * SWA_SCALE
                m = jnp.maximum(jnp.max(s2, axis=1, keepdims=True), sk)
                if band:
                    lo, W = _swa_window(i, tq, n)
                    mask = jnp.concatenate([_swa_mask(i, lo, tq, W, n)] * 4, axis=0)
                    s1 = jnp.where(mask, _dot_nt(qs, krep[pl.ds(lo, W), gc]) * SWA_SCALE, NEG_INF)
                    m = jnp.maximum(m, jnp.max(s1, axis=1, keepdims=True))
                e2 = jnp.exp(s2 - m)
                l = jnp.sum(e2, axis=1, keepdims=True) + jnp.exp(sk - m)
                if band:
                    e1 = jnp.exp(s1 - m)
                    l = l + jnp.sum(e1, axis=1, keepdims=True)
                r = 1.0 / l
                o = _dot((e2 * r).astype(BF16), vrep[n:L, gc])
                if band:
                    o = o + _dot((e1 * r).astype(BF16), vrep[pl.ds(lo, W), gc])
                o_ref[:, gc] = _swa_fold_rows(o, tq).astype(o_ref.dtype)
                lse = m + jnp.log(l)
                for h in range(4):
                    lse_ref[:, 4 * g + h:4 * g + h + 1] = lse[h * tq:(h + 1) * tq]

        pl.when(i < n // tq)(functools.partial(tile, True))
        pl.when(i >= n // tq)(functools.partial(tile, False))

    return _pallas(
        body, name="swa_fwd", grid=(B, L // tq),
        in_specs=[_tile_spec(tq, 512), _full_spec(L, 128), _full_spec(L, 128), pl.BlockSpec((1, 8), lambda b, i: (0, 0))],
        out_specs=[_tile_spec(tq, 512), _tile_spec(tq, 8)],
        out_shape=[_sds((B, L, 512), BF16), _sds((B, L, 8), F32)],
        scratch_shapes=[pltpu.VMEM((L, 512), BF16), pltpu.VMEM((L, 512), BF16)],
        compiler_params=_params(("parallel", "arbitrary")),
    )(q, k, v, sink)


def _swa_bwd(q, k, v, sink, lse, do, n, tq):
    B, L, _ = q.shape

    def body(q_ref, k_ref, v_ref, sink_ref, lse_ref, do_ref, dq_ref, dk_ref, dv_ref, dsink_ref):
        i = pl.program_id(1)

        @pl.when(i == 0)
        def _():
            dk_ref[...] = jnp.zeros_like(dk_ref)
            dv_ref[...] = jnp.zeros_like(dv_ref)
            dsink_ref[...] = jnp.zeros_like(dsink_ref)

        def tile(band):
            kc, vc = k_ref[n:L, :], v_ref[n:L, :]
            if band:
                lo, W = _swa_window(i, tq, n)
                mask = _swa_mask(i, lo, tq, W, n)
                kl, vl = k_ref[pl.ds(lo, W), :], v_ref[pl.ds(lo, W), :]
            for h in range(8):
                g = h // 4
                cols = slice(64 * g, 64 * g + 64)
                qh = q_ref[:, 64 * h:64 * h + 64]
                lse_h = lse_ref[:, h:h + 1]
                doh = do_ref[:, 64 * h:64 * h + 64].astype(BF16)
                p2 = jnp.exp(_dot_nt(qh, kc[:, cols]) * SWA_SCALE - lse_h)
                ps = jnp.exp(sink_ref[0:1, h:h + 1] - lse_h)
                dp2 = _dot_nt(doh, vc[:, cols])
                delta = jnp.sum(p2 * dp2, axis=1, keepdims=True)
                if band:
                    p1 = jnp.exp(jnp.where(mask, _dot_nt(qh, kl[:, cols]) * SWA_SCALE, NEG_INF) - lse_h)
                    dp1 = _dot_nt(doh, vl[:, cols])
                    delta = delta + jnp.sum(p1 * dp1, axis=1, keepdims=True)
                ds2 = (p2 * (dp2 - delta) * SWA_SCALE).astype(BF16)
                dq = _dot(ds2, kc[:, cols])
                dk_ref[n:L, cols] += _dot_tn(ds2, qh)
                dv_ref[n:L, cols] += _dot_tn(p2.astype(BF16), doh)
                if band:
                    ds1 = (p1 * (dp1 - delta) * SWA_SCALE).astype(BF16)
                    dq = dq + _dot(ds1, kl[:, cols])
                    dk_ref[pl.ds(lo, W), cols] += _dot_tn(ds1, qh)
                    dv_ref[pl.ds(lo, W), cols] += _dot_tn(p1.astype(BF16), doh)
                dq_ref[:, 64 * h:64 * h + 64] = dq
                dsink_ref[0:1, h:h + 1] += jnp.sum(-ps * delta, axis=0, keepdims=True)

        pl.when(i < n // tq)(functools.partial(tile, True))
        pl.when(i >= n // tq)(functools.partial(tile, False))

    return _pallas(
        body, name="swa_bwd", grid=(B, L // tq),
        in_specs=[_tile_spec(tq, 512), _full_spec(L, 128), _full_spec(L, 128), pl.BlockSpec((1, 8), lambda b, i: (0, 0)),
                  _tile_spec(tq, 8), pl.BlockSpec((None, tq, 512), lambda b, i: (b, i, 1))],
        out_specs=[_tile_spec(tq, 512), _full_spec(L, 128), _full_spec(L, 128),
                   pl.BlockSpec((None, 1, 8), lambda b, i: (b, 0, 0))],
        out_shape=[_sds((B, L, 512), F32), _sds((B, L, 128), F32), _sds((B, L, 128), F32), _sds((B, 1, 8), F32)],
        compiler_params=_params(("parallel", "arbitrary")),
    )(q, k, v, sink, lse, do)


def _ret_decay(i, tq, L, n, ctx_tile):
    qi = i * tq + lax.broadcasted_iota(jnp.int32, (tq, 1), 0)
    kc = lax.broadcasted_iota(jnp.int32, (1, L - n), 1)
    d_hb = (n + kc - qi).astype(F32)
    if ctx_tile:
        return None, ((qi - L) - (kc - (L - n))).astype(F32), d_hb
    d_x = (qi - lax.broadcasted_iota(jnp.int32, (1, n), 1)).astype(F32)
    return d_x, (qi - (kc - (L - n))).astype(F32), d_hb


def _ret_head(a, h):
    lane = lax.broadcasted_iota(jnp.int32, (1, a.shape[1]), 1)
    return jnp.where(lane // 64 == h, a, jnp.zeros_like(a))


def _ret_mask(dist, lg):
    return jnp.where(dist >= 0.0, jnp.exp(lg * jnp.maximum(dist, 0.0)), 0.0)


def _ret_weights(n, L, d_x, d_hf, d_hb, lg_f, lg_b, with_grad):
    m_f, m_b = _ret_mask(d_hf, lg_f), _ret_mask(d_hb, lg_b)
    ctx = (slice(n, L), m_f + m_b)
    if with_grad:
        ctx += (m_f * jnp.maximum(d_hf, 0.0), m_b * jnp.maximum(d_hb, 0.0))
    if d_x is None:
        return [ctx]
    e_x = jnp.exp(jnp.where(d_x >= 0.0, lg_f, -lg_b) * d_x)
    lat = (slice(0, n), jnp.where(d_x == 0.0, 2.0 * e_x, e_x))
    if with_grad:
        lat += (e_x * jnp.maximum(d_x, 0.0), e_x * jnp.maximum(-d_x, 0.0))
    return [lat, ctx]


def _ret_fwd(q, k, v, lg, n, tq):
    B, L, _ = q.shape

    def body(q_ref, k_ref, v_ref, lg_ref, y_ref):
        i = pl.program_id(1)

        def tile(ctx_tile):
            dist = _ret_decay(i, tq, L, n, ctx_tile)
            for h in range(4):
                qh = _ret_head(q_ref[...], h)
                y = None
                for rows, dec_r in _ret_weights(n, L, *dist, lg_ref[0:1, h:h + 1], lg_ref[1:2, h:h + 1], False):
                    a = _dot_nt(qh, k_ref[rows, :]) * dec_r
                    part = _dot(a.astype(BF16), v_ref[rows, 128 * h:128 * h + 128])
                    y = part if y is None else y + part
                y_ref[:, 128 * h:128 * h + 128] = y

        pl.when(i < n // tq)(functools.partial(tile, False))
        pl.when(i >= n // tq)(functools.partial(tile, True))

    return _pallas(
        body, name="ret_fwd", grid=(B, L // tq),
        in_specs=[_tile_spec(tq, 256), _full_spec(L, 256), _full_spec(L, 512), pl.BlockSpec((2, 4), lambda b, i: (0, 0))],
        out_specs=_tile_spec(tq, 512), out_shape=_sds((B, L, 512), F32),
        compiler_params=_params(("parallel", "arbitrary")),
    )(q, k, v, lg)


def _ret_bwd(q, k, v, lg, dy, n, tq):
    B, L, _ = q.shape

    def body(q_ref, k_ref, v_ref, lg_ref, dy_ref, dq_ref, dk_ref, dv_ref, dlg_ref):
        i = pl.program_id(1)

        @pl.when(i == 0)
        def _():
            dk_ref[...] = jnp.zeros_like(dk_ref)
            dv_ref[...] = jnp.zeros_like(dv_ref)
            dlg_ref[...] = jnp.zeros_like(dlg_ref)

        def total(a):
            return jnp.sum(jnp.sum(a, axis=1, keepdims=True), axis=0, keepdims=True)

        def tile(ctx_tile):
            dist = _ret_decay(i, tq, L, n, ctx_tile)
            dq = None
            for h in range(4):
                vc = slice(128 * h, 128 * h + 128)
                qh = _ret_head(q_ref[...], h)
                dyh = dy_ref[:, vc].astype(BF16)
                dqh = None
                for rows, dec_r, wf_r, wb_r in _ret_weights(n, L, *dist, lg_ref[0:1, h:h + 1], lg_ref[1:2, h:h + 1], True):
                    k_all, vh = k_ref[rows, :], v_ref[rows, vc]
                    s = _dot_nt(qh, k_all)
                    gr = _dot_nt(dyh, vh)
                    ds = (gr * dec_r).astype(BF16)
                    part = _dot(ds, k_all)
                    dqh = part if dqh is None else dqh + part
                    dk_ref[rows, :] += _dot_tn(ds, qh)
                    dv_ref[rows, vc] += _dot_tn((s * dec_r).astype(BF16), dyh)
                    gs = gr * s
                    dlg_ref[0:1, h:h + 1] += total(gs * wf_r)
                    dlg_ref[1:2, h:h + 1] += total(gs * wb_r)
                dqh = _ret_head(dqh, h)
                dq = dqh if dq is None else dq + dqh
            dq_ref[...] = dq

        pl.when(i < n // tq)(functools.partial(tile, False))
        pl.when(i >= n // tq)(functools.partial(tile, True))

    return _pallas(
        body, name="ret_bwd", grid=(B, L // tq),
        in_specs=[_tile_spec(tq, 256), _full_spec(L, 256), _full_spec(L, 512), pl.BlockSpec((2, 4), lambda b, i: (0, 0)),
                  _tile_spec(tq, 512)],
        out_specs=[_tile_spec(tq, 256), _full_spec(L, 256), _full_spec(L, 512),
                   pl.BlockSpec((None, 2, 4), lambda b, i: (b, 0, 0))],
        out_shape=[_sds((B, L, 256), F32), _sds((B, L, 256), F32), _sds((B, L, 512), F32), _sds((B, 2, 4), F32)],
        compiler_params=_params(("parallel", "arbitrary")),
    )(q, k, v, lg, dy)


def _loss_head(s, target, g, n, tm):
    B, L, D = s.shape
    nx = n // tm

    def body(s_ref, t_ref, g_ref, ds_ref, dg_ref, loss_ref):
        b, i = pl.program_id(0), pl.program_id(1)

        @pl.when((b == 0) & (i == 0))
        def _():
            dg_ref[...] = jnp.zeros_like(dg_ref)
            loss_ref[...] = jnp.zeros_like(loss_ref)

        @pl.when(i < nx)
        def _():
            y, vjp = jax.vjp(_rms, s_ref[...], g_ref[...])
            err = y - t_ref[...]
            d_s, d_g = vjp(err * (1.0 / D))
            ds_ref[...] = d_s
            dg_ref[...] += d_g
            part = jnp.sum(jnp.sum(err * err, axis=1, keepdims=True), axis=0, keepdims=True) * (0.5 / D)
            loss_ref[...] += jnp.broadcast_to(part, loss_ref.shape)

        @pl.when(i >= nx)
        def _():
            ds_ref[...] = jnp.zeros_like(ds_ref)

    return _pallas(
        body, name="loss_head", grid=(B, L // tm),
        in_specs=[pl.BlockSpec((None, tm, D), lambda b, i: (b, i, 0)),
                  pl.BlockSpec((None, tm, D), lambda b, i: (b, jnp.minimum(i, nx - 1), 0)),
                  pl.BlockSpec((1, D), lambda b, i: (0, 0))],
        out_specs=[pl.BlockSpec((None, tm, D), lambda b, i: (b, i, 0)), pl.BlockSpec((1, D), lambda b, i: (0, 0)),
                   pl.BlockSpec((1, 128), lambda b, i: (0, 0))],
        out_shape=[_sds((B, L, D), F32), _sds((1, D), F32), _sds((1, 128), F32)],
        compiler_params=_params(("arbitrary", "arbitrary")),
    )(s, target, g)


def _ada_fwd(c_all, ada_w, ada_b):
    NL, D, Ns = ada_w.shape
    R = c_all.shape[0]

    def body(c_ref, w_ref, b_ref, o_ref):
        cond = jax.nn.silu(c_ref[...]).astype(BF16)
        o_ref[...] = _dot(cond, w_ref[...].astype(BF16)) + b_ref[...]

    return _pallas(
        body, name="ada_fwd", grid=(NL,),
        in_specs=[pl.BlockSpec((R, D), lambda l: (0, 0)), pl.BlockSpec((None, D, Ns), lambda l: (l, 0, 0)),
                  pl.BlockSpec((None, 1, Ns), lambda l: (l, 0, 0))],
        out_specs=pl.BlockSpec((None, R, Ns), lambda l: (l, 0, 0)), out_shape=_sds((NL, R, Ns), F32),
        compiler_params=_params(("parallel",)),
    )(c_all, ada_w, ada_b)


def _ada_bwd(c_all, ada_w, dmods):
    NL, D, Ns = ada_w.shape
    R = c_all.shape[0]

    def body(c_ref, w_ref, dm_ref, dw_ref, dc_ref):
        cond = jax.nn.silu(c_ref[...]).astype(BF16)
        dm = dm_ref[...].astype(BF16)
        dw_ref[...] = _dot_tn(cond, dm)
        dc_ref[...] = _dot_nt(dm, w_ref[...].astype(BF16))

    return _pallas(
        body, name="ada_bwd", grid=(NL,),
        in_specs=[pl.BlockSpec((R, D), lambda l: (0, 0)), pl.BlockSpec((None, D, Ns), lambda l: (l, 0, 0)),
                  pl.BlockSpec((None, R, Ns), lambda l: (l, 0, 0))],
        out_specs=[pl.BlockSpec((None, D, Ns), lambda l: (l, 0, 0)), pl.BlockSpec((None, R, D), lambda l: (l, 0, 0))],
        out_shape=[_sds((NL, D, Ns), F32), _sds((NL, R, D), F32)],
        compiler_params=_params(("parallel",)),
    )(c_all, ada_w, dmods)


def _my_index():
    return 4 * lax.axis_index("x") + 2 * lax.axis_index("y") + lax.axis_index("c")


def _peer(k):
    x, y, c = lax.axis_index("x"), lax.axis_index("y"), lax.axis_index("c")
    kx, ky, kc = (k >> 2) & 1, (k >> 1) & 1, k & 1
    px, py, pc = (x + kx) % 2, (y + ky) % 2, (c + kc) % 2
    return (px, py, pc), 4 * px + 2 * py + pc


def _all_gather(name, shards):
    na = len(shards)
    hbm = pl.BlockSpec(memory_space=pl.ANY)

    def body(*refs):
        in_refs, out_refs = refs[:na], refs[na:2 * na]
        send_sems, recv_sems, local_sems = refs[2 * na:]
        me = _my_index()
        sib_id, sib = _peer(1)
        chips = [_peer(k) for k in (4, 2, 6)]
        sib_chips = [4 * px + 2 * py + (1 - pc) for (px, py, pc), _ in chips]

        def copy(a, k, slot, to, src=None):
            dst = out_refs[a].at[slot]
            return pltpu.make_async_remote_copy(
                src_ref=dst if src is None else src, dst_ref=dst, send_sem=send_sems.at[a, k],
                recv_sem=recv_sems.at[a, k], device_id=to, device_id_type=MESH)

        first, passed, mine = [], [], []
        for a in range(na):
            cp = pltpu.make_async_copy(in_refs[a], out_refs[a].at[me], local_sems.at[a])
            cp.start()
            mine.append(cp)
            first.append(copy(a, 0, me, sib_id, src=in_refs[a]))
            first += [copy(a, 1 + j, me, pid, src=in_refs[a]) for j, (pid, _) in enumerate(chips)]
        for cp in first:
            cp.start()
        for a in range(na):
            for j, (pid, pidx) in enumerate(chips):
                copy(a, 1 + j, pidx, pid).wait_recv()
                fwd = copy(a, 4 + j, pidx, sib_id)
                fwd.start()
                passed.append(fwd)
        for a in range(na):
            copy(a, 0, sib, sib_id).wait_recv()
            for j in range(3):
                copy(a, 4 + j, sib_chips[j], sib_id).wait_recv()
        for cp in first + passed:
            cp.wait_send()
        for cp in mine:
            cp.wait()

    return _pallas(
        body, name=name, in_specs=[hbm] * na, out_specs=[hbm] * na,
        out_shape=[_sds((N_DEV,) + s.shape, s.dtype) for s in shards],
        scratch_shapes=[pltpu.SemaphoreType.DMA((na, 7)), pltpu.SemaphoreType.DMA((na, 7)),
                        pltpu.SemaphoreType.DMA((na,))],
    )(*shards)


_HBM = pl.BlockSpec(memory_space=pltpu.HBM)
_SEM = pl.BlockSpec(memory_space=pltpu.SEMAPHORE)
_DATAFLOW = pltpu.SideEffectType.DATAFLOW_SIDE_EFFECTING


def _exchange_start(name, srcs, slotted, after=()):
    na = len(srcs)
    lands = [lax.empty((N_DEV,) + (s.shape[1:] if slotted else s.shape), s.dtype) for s in srcs]

    def body(*refs):
        src_refs, land_refs = refs[:na], refs[na:2 * na]
        outs = refs[2 * na + len(after):]
        send_sems, recv_sems, token = outs[:na], outs[na:2 * na], outs[4 * na]
        me = _my_index()
        for a in range(na):
            for k in range(1, N_DEV):
                pid, pidx = _peer(k)
                pltpu.make_async_remote_copy(
                    src_ref=src_refs[a].at[pidx] if slotted else src_refs[a], dst_ref=land_refs[a].at[me],
                    send_sem=send_sems[a], recv_sem=recv_sems[a], device_id=pid, device_id_type=MESH).start()
        token[...] = jnp.zeros_like(token)

    ops = [pltpu.with_memory_space_constraint(a, pltpu.HBM) for a in list(srcs) + lands]
    outs = _pallas(
        body, name=name,
        out_shape=[pltpu.SemaphoreType.DMA(())] * (2 * na) + [pltpu.HBM(a.shape, a.dtype) for a in ops]
        + [_sds((8, 128), F32)],
        in_specs=[_HBM] * (2 * na) + [pl.BlockSpec(memory_space=pl.ANY)] * len(after),
        out_specs=[_SEM] * (2 * na) + [_HBM] * (2 * na) + [pl.BlockSpec(memory_space=pltpu.VMEM)],
        input_output_aliases={a: 2 * na + a for a in range(2 * na)},
        compiler_params=pltpu.CompilerParams(has_side_effects=_DATAFLOW),
    )(*ops, *after)
    return (na, outs[:2 * na], outs[2 * na:4 * na]), outs[4 * na]


def _exchange_wait(name, handle, after):
    na, sems, thru = handle

    def body(*refs):
        land_refs = refs[na:2 * na]
        send_sems, recv_sems = refs[2 * na:3 * na], refs[3 * na:4 * na]
        me_id = (lax.axis_index("x"), lax.axis_index("y"), lax.axis_index("c"))
        for a in range(na):
            seven = land_refs[a].at[pl.ds(0, N_DEV - 1)]
            drain = pltpu.make_async_remote_copy(src_ref=seven, dst_ref=seven, send_sem=send_sems[a],
                                                 recv_sem=recv_sems[a], device_id=me_id, device_id_type=MESH)
            drain.wait_send()
            drain.wait_recv()

    outs = _pallas(
        body, name=name, out_shape=[pltpu.HBM(a.shape, a.dtype) for a in thru],
        in_specs=[_HBM] * (2 * na) + [_SEM] * (2 * na) + [pl.BlockSpec(memory_space=pl.ANY)],
        out_specs=[_HBM] * (2 * na), input_output_aliases={a: a for a in range(2 * na)},
        compiler_params=pltpu.CompilerParams(has_side_effects=_DATAFLOW),
    )(*thru, *sems, after)
    return outs[na:]


def _fill_own(name, landed, owns, slotted):
    me = _my_index()
    return [jnp.where(lax.broadcasted_iota(jnp.int32, land.shape, 0) == me, own, land)
            for land, own in zip(landed, owns)]


def _rope_tables(n, L, width):
    t = jnp.arange(n)
    row = (t // GRID_W).astype(F32)
    col = (t % GRID_W).astype(F32)
    n_freq = ROPE_DIM // 4
    freqs = ROPE_THETA ** (-jnp.arange(n_freq, dtype=F32) / n_freq)
    ang = jnp.concatenate([row[:, None] * freqs, col[:, None] * freqs], axis=-1)
    cos, sin = jnp.cos(ang), jnp.sin(ang)
    cos2 = jnp.concatenate([cos, cos], axis=-1)
    sin2 = jnp.concatenate([-sin, sin], axis=-1)
    cos2 = jnp.concatenate([cos2, jnp.ones((L - n, ROPE_DIM), F32)], axis=0)
    sin2 = jnp.concatenate([sin2, jnp.zeros((L - n, ROPE_DIM), F32)], axis=0)
    reps = width // ROPE_DIM
    return jnp.tile(cos2, (1, reps)), jnp.tile(sin2, (1, reps))


def _ab_perm(w):
    return jnp.concatenate([w[:, 0:256], w[:, 320:1600], w[:, 256:320], w[:, 256:320]], axis=1)


def _ab_unperm(g):
    rope_key = (g[:, 1536:1600].astype(F32) + g[:, 1600:1664].astype(F32)).astype(g.dtype)
    return jnp.concatenate([g[:, 0:256], rope_key, g[:, 256:1536]], axis=1)


def _wq_perm(w):
    return jnp.concatenate([w[:, 192 * h:192 * h + 128] for h in range(4)]
                           + [w[:, 192 * h + 128:192 * h + 192] for h in range(4)], axis=1)


def _wq_unperm(g):
    return jnp.concatenate([g[:, sl] for h in range(4)
                            for sl in (slice(128 * h, 128 * h + 128), slice(512 + 64 * h, 576 + 64 * h))], axis=1)


def _flat(a):
    return a.reshape(-1, a.shape[-1])


def _layer_weights(full, p):
    NL, NE, NO = len(full["ffn_in"]), len(full["ab_in"]), len(full["cd_in"])
    groups = range(4)

    def each(f, mats):
        return [None if w is None else f(w) for w in mats]

    return dict(
        norm_mix=[p["norm_mix"][l][None] for l in range(NL)], norm_ffn=[p["norm_ffn"][l][None] for l in range(NL)],
        norm_final=p["norm_final"][None],
        ffn_in=each(_ffn_interleave, full["ffn_in"]), ffn_out=list(full["ffn_out"]),
        ab_in=each(_ab_perm, full["ab_in"]), ab_out=list(full["ab_out"]),
        wq=each(_wq_perm, full["mla_wq_b"]), wkv=list(full["mla_wkv_b"]),
        kv_norm=[p["mla_kv_norm"][j][None] for j in range(NE)], q_norm=[p["mla_q_norm"][j][None] for j in range(NE)],
        v_norm=[[p["cmlp_v_norm"][j][None, 128 * g:128 * g + 128] for g in groups] for j in range(NE)],
        ws=[[p["cmlp_ws"][j, g] for g in groups] for j in range(NE)],
        bs=[[p["cmlp_bs"][j, g][:, None] for g in groups] for j in range(NE)],
        cd_in=list(full["cd_in"]), cd_out=list(full["cd_out"]),
        lg=[jnp.stack([jax.nn.log_sigmoid(p["ret_decay_fwd"][j]), jax.nn.log_sigmoid(p["ret_decay_bwd"][j])])
            for j in range(NO)],
        sink=[p["swa_sink"][j][None] for j in range(NO)],
        ret_norm=[[p["ret_norm"][j][None, 128 * g:128 * g + 128] for g in groups] for j in range(NO)],
    )


def _local_step(x, ctx, target, mods, W, later_weights=None, early_grads=None):
    B, n, D = x.shape
    m = ctx.shape[1]
    L = n + m
    NL = mods.shape[0]
    tm = min(256, m)
    tq = min(256, m)
    cos512, sin512 = _rope_tables(n, L, 512)
    s = jnp.concatenate([x, ctx], axis=1)
    saved = []

    def rows(name, f, arrays, pieces, samp, samp_pieces, glob, out_arrays, out_pieces, tile=tm):
        return _Rows(name, f, B, L, n, tile, arrays, pieces, samp, samp_pieces, glob, out_arrays, out_pieces)

    def full(width, start=0):
        return [(0, start, width)]

    for l in range(NL):
        j = l // 2
        even = l % 2 == 0
        md = mods[l]
        if l in (1, 2) and later_weights is not None:
            W = later_weights(l, s)
        r = {}
        if l == 0:
            r["pre1"] = rows("pre_mix0", _f_pre, [s], full(D), [md], [(0, 0), (0, 1)], [W["norm_mix"][0]], [(D, BF16)], full(D))
            (xn,) = r["pre1"].fwd()
        r["xn"] = xn
        if even:
            z = _mm(f"ab_in{l}", _flat(xn), W["ab_in"][j], out_dtype=BF16).reshape(B, L, 1664)
            pieces = [(0, 0, 256), (0, 256, 256), (0, 1536, 128)]
            pieces += [(0, 512 + 128 * g, 128) for g in range(4)] + [(0, 1024 + 128 * g, 128) for g in range(4)]
            pieces += [(1, 0, 128), (2, 0, 128)]
            glob = [W["kv_norm"][j], W["q_norm"][j]] + W["v_norm"][j] + W["ws"][j] + W["bs"][j]
            abp = rows(f"ab_pre{l}", _f_ab_pre, [z, cos512, sin512], pieces, [], [], glob,
                       [(256, BF16), (256, BF16), (128, BF16), (512, BF16)],
                       [(0, 0, 256), (1, 0, 256), (2, 0, 128)] + [(3, 128 * g, 128) for g in range(4)], tile=128)
            kvn, qn, kpe, cm = abp.fwd()
            kv = _mm(f"wkv{l}", _flat(kvn), W["wkv"][j], out_dtype=BF16).reshape(B, L, 1024)
            q0 = _mm(f"wq{l}", _flat(qn), W["wq"][j]).reshape(B, L, 768)
            qrp = rows(f"q_rope{l}", _f_q_rope, [q0, cos512, sin512], [(0, 0, 512), (0, 512, 256), (1, 0, 256), (2, 0, 256)],
                       [], [], [], [(768, BF16)], [(0, 0, 512), (0, 512, 256)])
            (q,) = qrp.fwd()
            o, lse = _mla_fwd(q, kv, kpe, n, tq)
            y = _mm_pair(f"ab_out{l}", _flat(o), _flat(cm), W["ab_out"][j], BF16).reshape(B, L, D)
            r.update(z=z, abp=abp, kvn=kvn, qn=qn, kpe=kpe, cm=cm, kv=kv, qrp=qrp, q=q, o=o, lse=lse)
        else:
            z = _mm(f"cd_in{l}", _flat(xn), W["cd_in"][j], out_dtype=BF16).reshape(B, L, 2304)
            pieces = [(0, 0, 256), (0, 256, 512), (0, 768, 128), (0, 896, 128), (0, 1024, 256), (0, 1280, 512), (0, 1792, 512)]
            pieces += [(1, 0, 256), (2, 0, 256), (1, 0, 128), (2, 0, 128), (1, 0, 512), (2, 0, 512)]
            cdp = rows(f"cd_pre{l}", _f_cd_pre, [z, cos512, sin512], pieces, [], [], [],
                       [(256, BF16), (256, BF16), (512, BF16), (512, BF16), (128, BF16), (128, BF16), (512, F32)],
                       [(k_, 0, w_) for k_, w_ in enumerate((256, 256, 512, 512, 128, 128, 512))])
            rq, rk, rv, sq, sk, sv, rg = cdp.fwd()
            yret = _ret_fwd(rq, rk, rv, W["lg"][j], n, tq)
            osw, lse = _swa_fwd(sq, sk, sv, W["sink"][j], n, tq)
            mrg = rows(f"cd_merge{l}", _f_cd_merge, [yret, rg],
                       [(0, 128 * g, 128) for g in range(4)] + [(1, 128 * g, 128) for g in range(4)], [], [],
                       W["ret_norm"][j], [(512, BF16)], [(0, 128 * g, 128) for g in range(4)])
            (yr,) = mrg.fwd()
            y = _mm_pair(f"cd_out{l}", _flat(yr), _flat(osw), W["cd_out"][j], BF16).reshape(B, L, D)
            r.update(z=z, cdp=cdp, rq=rq, rk=rk, rv=rv, sq=sq, sk=sk, sv=sv, rg=rg, yret=yret, osw=osw, lse=lse,
                     mrg=mrg, yr=yr)
        two, outs2 = [(0, 0, D), (1, 0, D)], [(D, F32), (D, BF16)]
        r["mix_out"] = rows(f"mix_res_pre{l}", _f_res_pre, [s, y], two, [md], [(0, 2), (0, 3), (0, 4)],
                            [W["norm_ffn"][l]], outs2, two)
        s1, xn2 = r["mix_out"].fwd()
        if l == 0 and later_weights is not None:
            W = later_weights(0, s1)
        y2 = _ffn_fwd(f"ffn{l}", _flat(xn2), W["ffn_in"][l], W["ffn_out"][l]).reshape(B, L, D)
        if l < NL - 1:
            r["ffn_out"] = rows(f"ffn_res_pre{l}", _f_res_pre, [s1, y2], two, [md, mods[l + 1]], [(0, 5), (1, 0), (1, 1)],
                                [W["norm_mix"][l + 1]], outs2, two)
            s, xn = r["ffn_out"].fwd()
        else:
            r["ffn_out"] = rows(f"res_ffn{l}", _f_res, [s1, y2], two, [md], [(0, 5)], [], [(D, F32)], full(D))
            (s,) = r["ffn_out"].fwd()
        r["xn2"] = xn2
        saved.append(r)

    ds, d_norm_final, loss = _loss_head(s, target, W["norm_final"], n, tm)

    G = {k: [None] * len(v) for k, v in W.items() if isinstance(v, list)}
    G["norm_final"] = d_norm_final
    dm = [[None] * 6 for _ in range(NL)]
    dxn_next = None
    for l in reversed(range(NL)):
        j = l // 2
        even = l % 2 == 0
        r = saved[l]
        zero = early_grads(0, G) if l == 0 and early_grads is not None else None
        if l == NL - 1:
            (ds1, dy2), (dm[l][5],), _ = r["ffn_out"].bwd([ds], {0: F32, 1: BF16}, unread=(0,))
        else:
            smp = None if zero is None else [mods[l] + zero, mods[l + 1] + zero]
            (ds1, dy2), (dm[l][5], dm[l + 1][0], dm[l + 1][1]), (G["norm_mix"][l + 1],) = r["ffn_out"].bwd(
                [ds, dxn_next], {0: F32, 1: BF16}, grad_glob=(0,), samp=smp)
        dy2f, xn2f = _flat(dy2), _flat(r["xn2"])
        hid, dz2 = _ffn_mid_bwd(f"ffn_mid{l}", xn2f, dy2f, W["ffn_in"][l], W["ffn_out"][l])
        G["ffn_out"][l] = _mm(f"g_ffn_out{l}", hid, dy2f, ta=True, out_dtype=BF16)
        G["ffn_in"][l] = _mm(f"g_ffn_in{l}", xn2f, dz2, ta=True, out_dtype=BF16)
        dxn2 = _mm(f"d_xn2{l}", dz2, W["ffn_in"][l], tb=True, out_dtype=BF16).reshape(B, L, D)
        zero = early_grads(1, G) if l == 0 and early_grads is not None else None
        smp = None if zero is None else [mods[l] + zero]
        (ds0, dy), (dm[l][2], dm[l][3], dm[l][4]), (G["norm_ffn"][l],) = r["mix_out"].bwd(
            [ds1, dxn2], {0: F32, 1: BF16}, grad_glob=(0,), samp=smp)
        dyf = _flat(dy)
        if even:
            w_out = W["ab_out"][j]
            dcat = _mm(f"d_cat{l}", dyf, w_out, tb=True, out_dtype=BF16).reshape(B, L, -1)
            G["ab_out"][l // 2] = jnp.concatenate(
                [_mm(f"g_ab_out_a{l}", _flat(r["o"]), dyf, ta=True, out_dtype=BF16),
                 _mm(f"g_ab_out_b{l}", _flat(r["cm"]), dyf, ta=True, out_dtype=BF16)], axis=0)
            dq, dkv, dkpe = _mla_bwd(r["q"], r["kv"], r["kpe"], r["lse"], dcat, n, tq)
            (dq0,), _, _ = r["qrp"].bwd([dq], {0: BF16}, unread=(0,))
            dq0f, dkvf = _flat(dq0), _flat(dkv)
            G["wq"][j] = _mm(f"g_wq{l}", _flat(r["qn"]), dq0f, ta=True, out_dtype=BF16)
            G["wkv"][j] = _mm(f"g_wkv{l}", _flat(r["kvn"]), dkvf, ta=True, out_dtype=BF16)
            dqn = _mm(f"d_qn{l}", dq0f, W["wq"][j], tb=True).reshape(B, L, 256)
            dkvn = _mm(f"d_kvn{l}", dkvf, W["wkv"][j], tb=True).reshape(B, L, 256)
            (dz,), _, gg = r["abp"].bwd([dkvn, dqn, dkpe, (dcat, 512)], {0: BF16}, grad_glob=tuple(range(14)))
            G["kv_norm"][j], G["q_norm"][j] = gg[0], gg[1]
            G["v_norm"][j], G["ws"][j], G["bs"][j] = list(gg[2:6]), list(gg[6:10]), list(gg[10:14])
            w_in, key = W["ab_in"][j], "ab_in"
        else:
            w_out = W["cd_out"][j]
            dcat = _mm(f"d_cat{l}", dyf, w_out, tb=True, out_dtype=BF16).reshape(B, L, -1)
            G["cd_out"][j] = jnp.concatenate(
                [_mm(f"g_cd_out_a{l}", _flat(r["yr"]), dyf, ta=True, out_dtype=BF16),
                 _mm(f"g_cd_out_b{l}", _flat(r["osw"]), dyf, ta=True, out_dtype=BF16)], axis=0)
            (dyret, drg), _, gg = r["mrg"].bwd([(dcat, 0)], {0: F32, 1: F32}, grad_glob=(0, 1, 2, 3))
            G["ret_norm"][j] = list(gg)
            drq, drk, drv, dlg = _ret_bwd(r["rq"], r["rk"], r["rv"], W["lg"][j], dyret, n, tq)
            dsq, dsk, dsv, dsink = _swa_bwd(r["sq"], r["sk"], r["sv"], W["sink"][j], r["lse"], dcat, n, tq)
            G["lg"][j], G["sink"][j] = dlg, dsink
            (dz,), _, _ = r["cdp"].bwd([drq, drk, drv, dsq, dsk, dsv, drg], {0: BF16}, unread=(0,))
            w_in, key = W["cd_in"][j], "cd_in"
        dzf = _flat(dz)
        G[key][j] = _mm(f"g_{key}{l}", _flat(r["xn"]), dzf, ta=True, out_dtype=BF16)
        dxn = _mm(f"d_xn{l}", dzf, w_in, tb=True, out_dtype=BF16).reshape(B, L, D)
        if l == 0 and early_grads is not None:
            smp = [mods[0] + early_grads(2, G)]
        if l == 0:
            (ds,), (dm[0][0], dm[0][1]), (G["norm_mix"][0],) = r["pre1"].bwd([dxn], {0: F32}, grad_glob=(0,),
                                                                            add={0: ds0}, samp=smp)
        else:
            ds, dxn_next = ds0, dxn
    dmods = jnp.stack([jnp.concatenate(d, axis=2) for d in dm])
    return loss, ds[:, :n], dmods, G


def kernel(x, c, ctx, c_ctx, ada_w, ada_b, norm_mix, norm_ffn, norm_final, ffn_in, ffn_out, ab_in, ab_out, mla_q_norm, mla_kv_norm, mla_wq_b, mla_wkv_b, cmlp_v_norm, cmlp_ws, cmlp_bs, cd_in, cd_out, ret_decay_fwd, ret_decay_bwd, ret_norm, swa_sink, loss_target, m_c_ctx, m_ada_w, m_ada_b, m_norm_mix, m_norm_ffn, m_norm_final, m_ffn_in, m_ffn_out, m_ab_in, m_ab_out, m_mla_q_norm, m_mla_kv_norm, m_mla_wq_b, m_mla_wkv_b, m_cmlp_v_norm, m_cmlp_ws, m_cmlp_bs, m_cd_in, m_cd_out, m_ret_decay_fwd, m_ret_decay_bwd, m_ret_norm, m_swa_sink, v_c_ctx, v_ada_w, v_ada_b, v_norm_mix, v_norm_ffn, v_norm_final, v_ffn_in, v_ffn_out, v_ab_in, v_ab_out, v_mla_q_norm, v_mla_kv_norm, v_mla_wq_b, v_mla_wkv_b, v_cmlp_v_norm, v_cmlp_ws, v_cmlp_bs, v_cd_in, v_cd_out, v_ret_decay_fwd, v_ret_decay_bwd, v_ret_norm, v_swa_sink):
    B, n, D = x.shape
    NL = ada_w.shape[0]
    NE, NO = ab_in.shape[0], cd_in.shape[0]
    me = _my_index()
    weights = dict(c_ctx=c_ctx, ada_w=ada_w, ada_b=ada_b, norm_mix=norm_mix, norm_ffn=norm_ffn, norm_final=norm_final,
                   ffn_in=ffn_in, ffn_out=ffn_out, ab_in=ab_in, ab_out=ab_out, mla_q_norm=mla_q_norm,
                   mla_kv_norm=mla_kv_norm, mla_wq_b=mla_wq_b, mla_wkv_b=mla_wkv_b, cmlp_v_norm=cmlp_v_norm,
                   cmlp_ws=cmlp_ws, cmlp_bs=cmlp_bs, cd_in=cd_in, cd_out=cd_out, ret_decay_fwd=ret_decay_fwd,
                   ret_decay_bwd=ret_decay_bwd, ret_norm=ret_norm, swa_sink=swa_sink)
    moments_m = dict(c_ctx=m_c_ctx, ada_w=m_ada_w, ada_b=m_ada_b, norm_mix=m_norm_mix, norm_ffn=m_norm_ffn,
                     norm_final=m_norm_final, ffn_in=m_ffn_in, ffn_out=m_ffn_out, ab_in=m_ab_in, ab_out=m_ab_out,
                     mla_q_norm=m_mla_q_norm, mla_kv_norm=m_mla_kv_norm, mla_wq_b=m_mla_wq_b, mla_wkv_b=m_mla_wkv_b,
                     cmlp_v_norm=m_cmlp_v_norm, cmlp_ws=m_cmlp_ws, cmlp_bs=m_cmlp_bs, cd_in=m_cd_in, cd_out=m_cd_out,
                     ret_decay_fwd=m_ret_decay_fwd, ret_decay_bwd=m_ret_decay_bwd, ret_norm=m_ret_norm,
                     swa_sink=m_swa_sink)
    moments_v = dict(c_ctx=v_c_ctx, ada_w=v_ada_w, ada_b=v_ada_b, norm_mix=v_norm_mix, norm_ffn=v_norm_ffn,
                     norm_final=v_norm_final, ffn_in=v_ffn_in, ffn_out=v_ffn_out, ab_in=v_ab_in, ab_out=v_ab_out,
                     mla_q_norm=v_mla_q_norm, mla_kv_norm=v_mla_kv_norm, mla_wq_b=v_mla_wq_b, mla_wkv_b=v_mla_wkv_b,
                     cmlp_v_norm=v_cmlp_v_norm, cmlp_ws=v_cmlp_ws, cmlp_bs=v_cmlp_bs, cd_in=v_cd_in, cd_out=v_cd_out,
                     ret_decay_fwd=v_ret_decay_fwd, ret_decay_bwd=v_ret_decay_bwd, ret_norm=v_ret_norm,
                     swa_sink=v_swa_sink)
    order = list(weights)

    Ns = ada_w.shape[2]
    (c_g,) = _all_gather("gather_c", [c])
    R = N_DEV * B + 8
    c_all = jnp.concatenate([c_g.reshape(N_DEV * B, D), jnp.broadcast_to(c_ctx[None], (8, D))], axis=0)
    ada_b_mine = lax.dynamic_slice_in_dim(ada_b, me * Ns, Ns, axis=1)[:, None, :]
    mods_shard = _ada_fwd(c_all, ada_w, ada_b_mine)
    (mods_g,) = _all_gather("gather_mods", [mods_shard])
    mods_full = jnp.transpose(mods_g, (1, 2, 0, 3)).reshape(NL, R, 6, D)
    mx = lax.dynamic_slice_in_dim(mods_full, me * B, B, axis=1)
    mh = jnp.broadcast_to(mods_full[:, N_DEV * B][:, None], (NL, B, 6, D))
    mods = jnp.stack([mx, mh], axis=2)

    big = ["ffn_in", "ffn_out", "ab_in", "ab_out", "cd_in", "cd_out", "mla_wq_b", "mla_wkv_b"]
    col_sharded = {"ffn_in", "ab_in", "cd_in", "mla_wq_b", "mla_wkv_b"}
    shards = {k: _to_bf16("cast_" + k, weights[k]) for k in big}
    first = {k: 0 if k.startswith("cd_") else 1 for k in big}
    a_keys = [k for k in big if first[k] and not k.startswith("ffn_")]
    b_keys = ["ffn_in", "ffn_out"]
    early = _all_gather("gather_wA", [shards[k][:1] for k in a_keys])
    (rn_g,) = _all_gather("gather_ret_norm", [ret_norm])
    small_first = [mods, rn_g] + list(early)
    w_groups = [
        ("B", {k: (0, 1) for k in b_keys}),
        ("C1", {"ffn_in": (1, 2), "ffn_out": (1, 2), "cd_in": (0, 1), "cd_out": (0, 1)}),
        ("C2", {k: (2, NL) if k.startswith("ffn_") else (1, weights[k].shape[0]) for k in big}),
    ]
    w_handles, w_after = [], small_first
    for gname, members in w_groups:
        handle, token = _exchange_start(f"gather_w{gname}_start", [shards[k][lo:hi] for k, (lo, hi) in members.items()],
                                        False, w_after)
        w_handles.append(handle)
        w_after = w_after + [token]
        mods = mods + token[0, 0]

    def unshard(k, g):
        if k in col_sharded:
            f = jnp.transpose(g, (1, 2, 0, 3)).reshape(g.shape[1], g.shape[2], -1)
        else:
            f = jnp.transpose(g, (1, 0, 2, 3)).reshape(g.shape[1], -1, g.shape[3])
        return [f[i] for i in range(f.shape[0])]

    rn_full = jnp.transpose(rn_g, (1, 0, 2)).reshape(NO, -1)
    small_p = dict(weights, ret_norm=rn_full)
    full0 = {k: [None] * weights[k].shape[0] for k in big}
    for k, g in zip(a_keys, early):
        full0[k][:1] = unshard(k, g)

    def later_weights(stage, newest):
        gname, members = w_groups[stage]
        landed = _exchange_wait(f"gather_w{gname}_wait", w_handles[stage], newest)
        landed = _fill_own(f"own_w{gname}", landed, [shards[k][lo:hi] for k, (lo, hi) in members.items()], False)
        for (k, (lo, hi)), land in zip(members.items(), landed):
            full0[k][lo:hi] = unshard(k, land)
        return _layer_weights(full0, small_p)

    def to_slots(k, gl):
        g = jnp.stack(gl)
        if k == "ffn_in":
            half = N_DEV // 2
            g = g.reshape(g.shape[0], 2, g.shape[2], half, -1)
            return jnp.transpose(g, (1, 3, 0, 2, 4)).reshape(N_DEV, g.shape[0], g.shape[2], -1)
        if k in col_sharded:
            return jnp.transpose(g.reshape(g.shape[0], g.shape[1], N_DEV, -1), (2, 0, 1, 3))
        return jnp.transpose(g.reshape(g.shape[0], N_DEV, -1, g.shape[2]), (1, 0, 2, 3))

    def big_grads(G):
        return dict(ffn_in=[g if g is None else _ffn_deinterleave(g) for g in G["ffn_in"]], ffn_out=G["ffn_out"],
                    ab_in=[g if g is None else _ab_unperm(g) for g in G["ab_in"]],
                    ab_out=G["ab_out"], cd_in=G["cd_in"], cd_out=G["cd_out"],
                    mla_wq_b=[g if g is None else _wq_unperm(g) for g in G["wq"]], mla_wkv_b=G["wkv"])

    sent = {}

    def early_grads(stage, G):
        parts = big_grads(G)
        if stage == 0:
            srcs = [to_slots(k, parts[k][first[k]:]) for k in big]
            handle, g_token = _exchange_start("scatter_gC_start", srcs, slotted=True)
        elif stage == 1:
            srcs = [to_slots(k, parts[k][:1]) for k in b_keys]
            handle, g_token = _exchange_start("scatter_gB_start", srcs, slotted=True)
        else:
            srcs = [to_slots(k, parts[k][:1]) for k in a_keys]
            handle, g_token = _exchange_start("scatter_gA_start", srcs, slotted=True)
        sent[stage] = (handle, srcs)
        return g_token[0, 0]

    loss_part, grad_x, dmods, G = _local_step(x, ctx, loss_target, mods, _layer_weights(full0, small_p),
                                              later_weights, early_grads)

    dmx = dmods[:, :, 0].reshape(NL, B, 6 * D)
    dmh = jnp.sum(dmods[:, :, 1], axis=1).reshape(NL, 1, 6 * D)
    (dm_g,) = _all_gather("gather_dmods", [jnp.concatenate([dmx, dmh], axis=1)])
    dmx_all = jnp.transpose(dm_g[:, :, :B], (1, 0, 2, 3)).reshape(NL, N_DEV * B, 6 * D)
    dmh_all = jnp.sum(dm_g[:, :, B], axis=0)
    dm_rows = jnp.concatenate([dmx_all, dmh_all[:, None], jnp.zeros((NL, 7, 6 * D), F32)], axis=1)
    g_ada_b = jnp.sum(dm_rows, axis=1)
    dm_mine = lax.dynamic_slice_in_dim(dm_rows, me * Ns, Ns, axis=2)
    g_ada_w, dcond = _ada_bwd(c_all, ada_w, dm_mine)
    sg = jax.nn.sigmoid(c_ctx)
    d_c_ctx_part = jnp.sum(dcond[:, N_DEV * B], axis=0) * (sg * (1.0 + c_ctx * (1.0 - sg)))

    def cat(parts):
        return jnp.concatenate([p.reshape(-1) for p in parts])

    dlg = jnp.stack([jnp.sum(G["lg"][j], axis=0) for j in range(NO)])
    sig_f, sig_b = jax.nn.sigmoid(-ret_decay_fwd), jax.nn.sigmoid(-ret_decay_bwd)
    small = dict(
        loss=loss_part[0, 0:1],
        c_ctx=d_c_ctx_part,
        norm_mix=cat(G["norm_mix"]), norm_ffn=cat(G["norm_ffn"]), norm_final=G["norm_final"].reshape(-1),
        mla_q_norm=cat(G["q_norm"]), mla_kv_norm=cat(G["kv_norm"]),
        cmlp_v_norm=cat([cat(G["v_norm"][j]) for j in range(NE)]),
        cmlp_ws=cat([jnp.stack(G["ws"][j]) for j in range(NE)]),
        cmlp_bs=cat([jnp.stack([b_[:, 0] for b_ in G["bs"][j]]) for j in range(NE)]),
        ret_decay_fwd=(dlg[:, 0] * sig_f).reshape(-1), ret_decay_bwd=(dlg[:, 1] * sig_b).reshape(-1),
        ret_norm=cat([cat(G["ret_norm"][j]) for j in range(NO)]),
        swa_sink=cat([jnp.sum(G["sink"][j], axis=0) for j in range(NO)]),
    )
    small_keys = list(small)
    sizes = [small[k].shape[0] for k in small_keys]
    total = sum(sizes)
    padded = -(-total // 2048) * 2048
    packed = jnp.concatenate([small[k] for k in small_keys] + [jnp.zeros((padded - total,), F32)]).reshape(-1, 128)
    (small_g,) = _all_gather("gather_small", [packed])

    landed_a = _fill_own("own_gA", _exchange_wait("scatter_gA_wait", sent[2][0], small_g), sent[2][1], True)
    landed0 = dict(zip(a_keys, landed_a))
    landed_b = _fill_own("own_gB", _exchange_wait("scatter_gB_wait", sent[1][0], grad_x), sent[1][1], True)
    landed0.update(zip(b_keys, landed_b))
    landed_c = _fill_own("own_gC", _exchange_wait("scatter_gC_wait", sent[0][0], grad_x), sent[0][1], True)
    landed = [[landed0[k], rest] if first[k] else [rest] for k, rest in zip(big, landed_c)]

    grads, deltas, new_m, new_v = {}, {}, {}, {}
    for k, land in zip(big, landed):
        grads[k], deltas[k], new_m[k], new_v[k] = _adamw_from_slots("adamw_" + k, weights[k], moments_m[k], moments_v[k], land)
    deltas["ada_w"], new_m["ada_w"], new_v["ada_w"] = [
        o.reshape(ada_w.shape) for o in _adamw("adamw_ada_w", _flat(ada_w), _flat(g_ada_w), _flat(m_ada_w), _flat(v_ada_w))]
    grads["ada_w"] = g_ada_w

    sums_only = ("loss", "ret_norm")

    def packed_of(src, fill):
        vals = [jnp.full((sizes[i],), fill, F32) if k in sums_only else src[k].reshape(-1)
                for i, k in enumerate(small_keys)]
        return jnp.concatenate(vals + [jnp.full((padded - total,), fill, F32)]).reshape(-1, 128)

    w_p, m_p, v_p = packed_of(weights, 0.0), packed_of(moments_m, 0.0), packed_of(moments_v, 1.0)

    def f_small(w_, m_, v_, land_):
        g = _sum_slots(land_)
        return (g,) + _adamw_math(w_, g, m_, v_)

    g_p, d_p, nm_p, nv_p = _ew("adamw_small", f_small, [w_p, m_p, v_p, small_g], [F32] * 4)
    offs = np.cumsum([0] + sizes)
    for i, k in enumerate(small_keys):
        sl = slice(int(offs[i]), int(offs[i + 1]))
        if k == "loss":
            loss = g_p.reshape(-1)[int(offs[i])]
        elif k == "ret_norm":
            g_full = g_p.reshape(-1)[sl].reshape(NO, -1)
            g_mine = lax.dynamic_slice_in_dim(g_full, me * ret_norm.shape[1], ret_norm.shape[1], axis=1)
            d_, m_, v_ = _adamw("adamw_ret_norm", *[jnp.pad(a, ((0, 8 - NO), (0, 128 - a.shape[1])), constant_values=cv)
                                                     for a, cv in ((ret_norm, 0.0), (g_mine, 0.0), (m_ret_norm, 0.0), (v_ret_norm, 1.0))])
            grads[k] = g_mine
            deltas[k], new_m[k], new_v[k] = [a[:NO, :ret_norm.shape[1]] for a in (d_, m_, v_)]
        else:
            shp = weights[k].shape
            grads[k], deltas[k], new_m[k], new_v[k] = [a.reshape(-1)[sl].reshape(shp) for a in (g_p, d_p, nm_p, nv_p)]
    pad_b = lambda a, cv=0.0: jnp.pad(a, ((0, 8 - NL), (0, 0)), constant_values=cv)
    d_, m_, v_ = _adamw("adamw_ada_b", pad_b(ada_b), pad_b(g_ada_b), pad_b(m_ada_b), pad_b(v_ada_b, 1.0))
    grads["ada_b"] = g_ada_b
    deltas["ada_b"], new_m["ada_b"], new_v["ada_b"] = d_[:NL], m_[:NL], v_[:NL]

    return (loss, grad_x, *[grads[k] for k in order], *[deltas[k] for k in order],
            *[new_m[k] for k in order], *[new_v[k] for k in order])
```

```python
import functools

import numpy as np
import jax
import jax.numpy as jnp
from jax import lax
from jax.experimental import pallas as pl
from jax.experimental.pallas import tpu as pltpu

F32 = jnp.float32
BF16 = jnp.bfloat16
EPS = 1e-6
NEG_INF = -1e30
N_DEV = 8
GRID_W = 64
ROPE_THETA = 10000.0
ROPE_DIM = 64
SWA_WINDOW = 128
MLA_SCALE = (128 + 64) ** -0.5
SWA_SCALE = 64 ** -0.5
RET_K_SCALE = 64 ** -0.5
ADAM_LR, ADAM_B1, ADAM_B2, ADAM_EPS, ADAM_WD, ADAM_STEP = 0.001, 0.9, 0.999, 1e-08, 0.01, 10
V7X_VMEM_LIMIT = 56 * 1024 * 1024
ROW_BLOCK_BUDGET = 34 * 1024 * 1024
MESH = pl.DeviceIdType.MESH


def _pallas(body, **kw):
    return pl.pallas_call(body, **kw)


def _params(sem=None):
    return pltpu.CompilerParams(dimension_semantics=sem, vmem_limit_bytes=V7X_VMEM_LIMIT)


def _tile(n, cap, align):
    best = None
    for t in range(align, min(n, cap) + 1, align):
        if n % t == 0:
            best = t
    return n if best is None else best


def _sds(shape, dtype):
    return jax.ShapeDtypeStruct(tuple(shape), dtype)


def _ew(name, f, ins, out_dtypes, cap_elems=131072):
    R, C = ins[0].shape[-2:]
    tr = _tile(R, max(16, cap_elems // C), 16)
    n_in = len(ins)

    def spec(a):
        if a.ndim == 2:
            return pl.BlockSpec((tr, C), lambda i: (i, 0))
        return pl.BlockSpec((a.shape[0], tr, C), lambda i: (0, i, 0))

    def body(*refs):
        outs = f(*[r[...] for r in refs[:n_in]])
        for r, o in zip(refs[n_in:], outs):
            r[...] = o.astype(r.dtype)

    return _pallas(
        body, name=name, grid=(R // tr,), in_specs=[spec(a) for a in ins],
        out_specs=[pl.BlockSpec((tr, C), lambda i: (i, 0)) for _ in out_dtypes],
        out_shape=[_sds((R, C), d) for d in out_dtypes], compiler_params=_params(("parallel",)),
    )(*ins)


def _to_bf16(name, w):
    w2 = w.reshape(-1, w.shape[-1])
    return _ew(name, lambda v: (v,), [w2], [BF16])[0].reshape(w.shape)


def _adamw_math(w, g, m, v):
    m = ADAM_B1 * m + (1.0 - ADAM_B1) * g
    v = ADAM_B2 * v + (1.0 - ADAM_B2) * (g * g)
    m_hat = m / (1.0 - ADAM_B1 ** ADAM_STEP)
    v_hat = v / (1.0 - ADAM_B2 ** ADAM_STEP)
    delta = -ADAM_LR * (m_hat / (jnp.sqrt(v_hat) + ADAM_EPS) + ADAM_WD * w)
    return delta, m, v


def _sum_slots(land):
    g = land[0].astype(F32)
    for s in range(1, land.shape[0]):
        g = g + land[s].astype(F32)
    return g


def _adamw_from_slots(name, w, m, v, lands):
    nl, K, C = w.shape
    tr = _tile(K, max(16, 65536 // C), 16)
    per = K // tr
    starts = np.cumsum([0] + [ld.shape[1] * per for ld in lands])
    ng = len(lands)

    def land_spec(g):
        lo, hi = int(starts[g]), int(starts[g + 1])
        return pl.BlockSpec((N_DEV, tr, C), lambda t: (0, jnp.clip(t, lo, hi - 1) - lo, 0))

    def body(*refs):
        w_ref, m_ref, v_ref = refs[:3]
        land_refs, out_refs = refs[3:3 + ng], refs[3 + ng:]
        t = pl.program_id(0)
        for g in range(ng):
            @pl.when((t >= int(starts[g])) & (t < int(starts[g + 1])))
            def _(g=g):
                grad = _sum_slots(land_refs[g][...])
                for r, o in zip(out_refs, (grad,) + _adamw_math(w_ref[...], grad, m_ref[...], v_ref[...])):
                    r[...] = o

    row = pl.BlockSpec((tr, C), lambda t: (t, 0))
    outs = _pallas(
        body, name=name, grid=(nl * per,), in_specs=[row] * 3 + [land_spec(g) for g in range(ng)],
        out_specs=[row] * 4, out_shape=[_sds((nl * K, C), F32)] * 4, compiler_params=_params(("parallel",)),
    )(w.reshape(-1, C), m.reshape(-1, C), v.reshape(-1, C), *[ld.reshape(N_DEV, -1, C) for ld in lands])
    return [o.reshape(w.shape) for o in outs]


def _adamw(name, w, g, m, v):
    outs = _ew(name, lambda w_, g_, m_, v_: _adamw_math(w_, g_, m_, v_), [w, g, m, v], [F32] * 3)
    return outs


def _mm(name, a, b, ta=False, tb=False, out_dtype=F32, add=None):
    M, K = (a.shape[1], a.shape[0]) if ta else a.shape
    N = b.shape[0] if tb else b.shape[1]
    tm = _tile(M, 1408, 128)
    tn = _tile(N, 1024, 128)
    if tn < 256 and N <= 2432:
        tn = N
    tk = _tile(K, 3072, 128)
    nk = K // tk
    a_spec = pl.BlockSpec((tk, tm), lambda i, j, k: (k, i)) if ta else pl.BlockSpec((tm, tk), lambda i, j, k: (i, k))
    b_spec = pl.BlockSpec((tn, tk), lambda i, j, k: (j, k)) if tb else pl.BlockSpec((tk, tn), lambda i, j, k: (k, j))
    o_spec = pl.BlockSpec((tm, tn), lambda i, j, k: (i, j))
    dims = (((0 if ta else 1,), (1 if tb else 0,)), ((), ()))
    has_add = add is not None

    def product(a_ref, b_ref):
        return lax.dot_general(a_ref[...].astype(BF16), b_ref[...].astype(BF16), dims, preferred_element_type=F32)

    def body_single(*refs):
        acc = product(refs[0], refs[1])
        if has_add:
            acc = acc + refs[2][...]
        refs[-1][...] = acc.astype(refs[-1].dtype)

    def body(*refs):
        a_ref, b_ref = refs[0], refs[1]
        add_ref = refs[2] if has_add else None
        o_ref, acc = refs[-2], refs[-1]
        k = pl.program_id(2)

        @pl.when(k == 0)
        def _():
            acc[...] = add_ref[...] if has_add else jnp.zeros_like(acc)

        acc[...] += product(a_ref, b_ref)

        @pl.when(k == nk - 1)
        def _():
            o_ref[...] = acc[...].astype(o_ref.dtype)

    ins = [a, b] + ([add] if has_add else [])
    specs = [a_spec, b_spec] + ([o_spec] if has_add else [])
    return _pallas(
        body_single if nk == 1 else body, name=name, grid=(M // tm, N // tn, nk), in_specs=specs, out_specs=o_spec,
        out_shape=_sds((M, N), out_dtype), scratch_shapes=[] if nk == 1 else [pltpu.VMEM((tm, tn), F32)],
        compiler_params=_params(("parallel", "parallel", "arbitrary")),
    )(*ins)


def _mm_pair(name, a1, a2, b, out_dtype):
    M, Kh = a1.shape
    N = b.shape[1]
    tm = _tile(M, 1408, 128)

    def body(a1_ref, a2_ref, b1_ref, b2_ref, o_ref):
        o_ref[...] = (_dot(a1_ref[...].astype(BF16), b1_ref[...].astype(BF16))
                      + _dot(a2_ref[...].astype(BF16), b2_ref[...].astype(BF16))).astype(o_ref.dtype)

    a_spec = pl.BlockSpec((tm, Kh), lambda i: (i, 0))
    return _pallas(
        body, name=name, grid=(M // tm,),
        in_specs=[a_spec, a_spec, pl.BlockSpec((Kh, N), lambda i: (0, 0)), pl.BlockSpec((Kh, N), lambda i: (1, 0))],
        out_specs=pl.BlockSpec((tm, N), lambda i: (i, 0)), out_shape=_sds((M, N), out_dtype),
        compiler_params=_params(("parallel",)),
    )(a1, a2, b, b)


def _ffn_tile(F):
    return F // 2 if (F // 2) % 128 == 0 else F


def _ffn_interleave(w):
    D, F2 = w.shape
    T = _ffn_tile(F2 // 2)
    nj = F2 // (2 * T)

    def body(a_ref, b_ref, o_ref):
        o_ref[:, :T] = a_ref[...]
        o_ref[:, T:] = b_ref[...]

    return _pallas(
        body, name="ffn_interleave", grid=(nj,),
        in_specs=[pl.BlockSpec((D, T), lambda j: (0, j)), pl.BlockSpec((D, T), lambda j: (0, j + nj))],
        out_specs=pl.BlockSpec((D, 2 * T), lambda j: (0, j)), out_shape=_sds((D, F2), w.dtype),
        compiler_params=_params(("parallel",)),
    )(w, w)


def _ffn_deinterleave(w):
    D, F2 = w.shape
    T = _ffn_tile(F2 // 2)
    nj = F2 // (2 * T)

    def body(w_ref, o_ref):
        o_ref[0] = w_ref[:, :T]
        o_ref[1] = w_ref[:, T:]

    return _pallas(
        body, name="ffn_deinterleave", grid=(nj,), in_specs=[pl.BlockSpec((D, 2 * T), lambda j: (0, j))],
        out_specs=pl.BlockSpec((2, D, T), lambda j: (0, 0, j)), out_shape=_sds((2, D, F2 // 2), w.dtype),
        compiler_params=_params(("parallel",)),
    )(w)


def _ffn_specs(M, D, F):
    T = _ffn_tile(F)
    tm = _tile(M, 512, 128)
    x_spec = pl.BlockSpec((tm, D), lambda i, j: (i, 0))
    wi_spec = pl.BlockSpec((D, 2 * T), lambda i, j: (0, j))
    wo_spec = pl.BlockSpec((T, D), lambda i, j: (j, 0))
    return T, tm, F // T, x_spec, wi_spec, wo_spec


def _ffn_fwd(name, xn, w_in, w_out):
    M, D = xn.shape
    T, tm, nj, x_spec, wi_spec, wo_spec = _ffn_specs(M, D, w_out.shape[0])

    def body(x_ref, wi_ref, wo_ref, y_ref, acc):
        j = pl.program_id(1)
        z = _dot(x_ref[...], wi_ref[...])
        hid = (jax.nn.silu(z[:, :T]) * z[:, T:]).astype(BF16)
        part = _dot(hid, wo_ref[...])

        @pl.when(j == 0)
        def _():
            acc[...] = part

        @pl.when(j > 0)
        def _():
            acc[...] += part

        @pl.when(j == nj - 1)
        def _():
            y_ref[...] = acc[...].astype(y_ref.dtype)

    return _pallas(
        body, name=name, grid=(M // tm, nj), in_specs=[x_spec, wi_spec, wo_spec], out_specs=x_spec,
        out_shape=_sds((M, D), BF16), scratch_shapes=[pltpu.VMEM((tm, D), F32)],
        compiler_params=_params(("parallel", "arbitrary")),
    )(xn, w_in, w_out)


def _ffn_mid_bwd(name, xn, dy, w_in, w_out):
    M, D = xn.shape
    F = w_out.shape[0]
    T, tm, nj, x_spec, wi_spec, wo_spec = _ffn_specs(M, D, F)

    def body(x_ref, dy_ref, wi_ref, wo_ref, h_ref, dz_ref):
        z = _dot(x_ref[...], wi_ref[...])
        a, b = z[:, :T], z[:, T:]
        dh = _dot_nt(dy_ref[...], wo_ref[...])
        sig = jax.nn.sigmoid(a)
        act = a * sig
        h_ref[...] = (act * b).astype(BF16)
        dz_ref[:, :T] = (dh * b * (sig * (1.0 + a * (1.0 - sig)))).astype(BF16)
        dz_ref[:, T:] = (dh * act).astype(BF16)

    return _pallas(
        body, name=name, grid=(M // tm, nj), in_specs=[x_spec, x_spec, wi_spec, wo_spec],
        out_specs=[pl.BlockSpec((tm, T), lambda i, j: (i, j)),
                   pl.BlockSpec((tm, 2 * T), lambda i, j: (i, j))],
        out_shape=[_sds((M, F), BF16), _sds((M, 2 * F), BF16)],
        compiler_params=_params(("parallel", "parallel")),
    )(xn, dy, w_in, w_out)


class _Rows:
    def __init__(self, name, f, B, L, n, tm, arrays, pieces, samp, samp_pieces, glob, out_arrays, out_pieces):
        self.name, self.f, self.B, self.L, self.n, self.tm = name, f, B, L, n, tm
        self.arrays, self.pieces, self.samp, self.samp_pieces, self.glob = arrays, pieces, samp, samp_pieces, glob
        self.out_arrays, self.out_pieces = out_arrays, out_pieces
        self.nx = n // tm

    def _samples_per_step(self, row_bytes):
        for bb in (4, 2, 1):
            if self.B % bb == 0 and 2 * bb * self.tm * row_bytes <= ROW_BLOCK_BUDGET:
                return bb
        return 1

    def _row_spec(self, bb, C, batched=True):
        tm = self.tm
        if batched:
            return pl.BlockSpec((bb, tm, C), lambda g, i: (g, i, 0))
        return pl.BlockSpec((tm, C), lambda g, i: (i, 0))

    def _in_specs(self, bb, unread=()):
        nx, tm = self.nx, self.tm
        specs = [self._row_spec(bb, a.shape[-1], a.ndim == 3) for a in self.arrays]
        for ai in unread:
            specs[ai] = pl.BlockSpec((bb, tm, self.arrays[ai].shape[-1]), lambda g, i: (0, 0, 0))
        specs += [pl.BlockSpec((bb, None) + s.shape[2:], lambda g, i: (g, i // nx, 0, 0)) for s in self.samp]
        specs += [pl.BlockSpec(g.shape, lambda g_, i, nd=g.ndim: (0,) * nd) for g in self.glob]
        return specs

    def _load(self, b, a_refs, s_refs, g_refs):
        args = [(a_refs[ai][b, :, cs:cs + cw] if self.arrays[ai].ndim == 3 else a_refs[ai][:, cs:cs + cw]).astype(F32)
                for ai, cs, cw in self.pieces]
        args += [s_refs[si][b, r:r + 1, :].astype(F32) for si, r in self.samp_pieces]
        args += [g[...].astype(F32) for g in g_refs]
        return args

    @staticmethod
    def _row_bytes(arrays):
        return sum(a.shape[-1] * a.dtype.itemsize for a in arrays if a.ndim == 3)

    def fwd(self):
        na, ns, ng = len(self.arrays), len(self.samp), len(self.glob)
        bb = self._samples_per_step(self._row_bytes(self.arrays)
                                    + sum(C * jnp.dtype(d).itemsize for C, d in self.out_arrays))

        def body(*refs):
            a_refs, s_refs, g_refs = refs[:na], refs[na:na + ns], refs[na + ns:na + ns + ng]
            o_refs = refs[na + ns + ng:]
            for b in range(bb):
                outs = self.f(*self._load(b, a_refs, s_refs, g_refs))
                for (oi, cs, cw), o in zip(self.out_pieces, outs):
                    o_refs[oi][b, :, cs:cs + cw] = o.astype(o_refs[oi].dtype)

        return _pallas(
            body, name=self.name + "_fwd", grid=(self.B // bb, self.L // self.tm), in_specs=self._in_specs(bb),
            out_specs=[self._row_spec(bb, C) for C, _ in self.out_arrays],
            out_shape=[_sds((self.B, self.L, C), d) for C, d in self.out_arrays],
            compiler_params=_params(("parallel", "parallel")),
        )(*self.arrays, *self.samp, *self.glob)

    def bwd(self, cts, grad_arrays, grad_glob=(), add=None, samp=None, unread=()):
        samp = self.samp if samp is None else samp
        na, ns, ng, nc = len(self.arrays), len(self.samp), len(self.glob), len(cts)
        ct_off = [c[1] if isinstance(c, tuple) else 0 for c in cts]
        cts = [c[0] if isinstance(c, tuple) else c for c in cts]
        add = add or {}
        add_keys = list(add)
        g_idx = list(grad_arrays)
        nx, B = self.nx, self.B
        n_in = na + ns + ng + nc + len(add_keys)
        n_pieces, n_sp = len(self.pieces), len(self.samp_pieces)
        streamed = list(self.arrays) + cts + [add[k] for k in add_keys]
        bb = self._samples_per_step(self._row_bytes(streamed) + sum(
            self.arrays[ai].shape[-1] * jnp.dtype(grad_arrays[ai]).itemsize for ai in g_idx))

        def body(*refs):
            a_refs, s_refs, g_refs = refs[:na], refs[na:na + ns], refs[na + ns:na + ns + ng]
            c_refs = refs[na + ns + ng:na + ns + ng + nc]
            add_refs = refs[na + ns + ng + nc:n_in]
            d_refs = refs[n_in:n_in + len(g_idx)]
            ds_refs = refs[n_in + len(g_idx):n_in + len(g_idx) + n_sp]
            dg_refs = refs[n_in + len(g_idx) + n_sp:]
            g_step, i = pl.program_id(0), pl.program_id(1)

            @pl.when((i == 0) | (i == nx))
            def _():
                for r in ds_refs:
                    r[...] = jnp.zeros_like(r)

            @pl.when((g_step == 0) & (i == 0))
            def _():
                for r in dg_refs:
                    r[...] = jnp.zeros_like(r)

            for b in range(bb):
                args = self._load(b, a_refs, s_refs, g_refs)
                _, vjp = jax.vjp(lambda *xs: tuple(self.f(*xs)), *args)
                grads = vjp(tuple(c_refs[oi][b, :, ct_off[oi] + cs:ct_off[oi] + cs + cw].astype(F32)
                                  for oi, cs, cw in self.out_pieces))
                for k, ai in enumerate(g_idx):
                    covered = sum(cw for pa, _, cw in self.pieces if pa == ai)
                    if covered < self.arrays[ai].shape[-1]:
                        d_refs[k][b] = jnp.zeros(d_refs[k].shape[1:], d_refs[k].dtype)
                    for (pa, cs, cw), gr in zip(self.pieces, grads[:n_pieces]):
                        if pa == ai:
                            if ai in add:
                                gr = gr + add_refs[add_keys.index(ai)][b, :, cs:cs + cw]
                            d_refs[k][b, :, cs:cs + cw] = gr.astype(d_refs[k].dtype)
                for r, gr in zip(ds_refs, grads[n_pieces:n_pieces + n_sp]):
                    r[b] += gr
                for r, gi in zip(dg_refs, grad_glob):
                    r[...] += grads[n_pieces + n_sp + gi]

        in_specs = self._in_specs(bb, unread) + [self._row_spec(bb, c.shape[-1]) for c in cts]
        in_specs += [self._row_spec(bb, add[k].shape[-1]) for k in add_keys]
        out_specs = [self._row_spec(bb, self.arrays[ai].shape[-1]) for ai in g_idx]
        out_shape = [_sds(self.arrays[ai].shape, grad_arrays[ai]) for ai in g_idx]
        for si, _ in self.samp_pieces:
            C = self.samp[si].shape[-1]
            out_specs.append(pl.BlockSpec((bb, None, 1, C), lambda g, i: (g, i // nx, 0, 0)))
            out_shape.append(_sds((B, 2, 1, C), F32))
        for gi in grad_glob:
            g = self.glob[gi]
            out_specs.append(pl.BlockSpec(g.shape, lambda g_, i, nd=g.ndim: (0,) * nd))
            out_shape.append(_sds(g.shape, F32))
        outs = _pallas(
            body, name=self.name + "_bwd", grid=(B // bb, self.L // self.tm), in_specs=in_specs, out_specs=out_specs,
            out_shape=out_shape, compiler_params=_params(("arbitrary", "arbitrary")),
        )(*self.arrays, *samp, *self.glob, *cts, *[add[k] for k in add_keys])
        ng_ = len(g_idx)
        return outs[:ng_], outs[ng_:ng_ + n_sp], outs[ng_ + n_sp:]


@jax.custom_vjp
def _rms(x, g):
    return x * lax.rsqrt(jnp.mean(x * x, axis=-1, keepdims=True) + EPS) * g


def _rms_fwd(x, g):
    r = lax.rsqrt(jnp.mean(x * x, axis=-1, keepdims=True) + EPS)
    xh = x * r
    return xh * g, (xh, r, g)


def _rms_bwd(res, dy):
    xh, r, g = res
    dxh = dy * g
    dx = r * (dxh - xh * jnp.mean(dxh * xh, axis=-1, keepdims=True))
    return dx, jnp.sum(dy * xh, axis=0, keepdims=True)


_rms.defvjp(_rms_fwd, _rms_bwd)


@jax.custom_vjp
def _swap_halves(x):
    w = x.shape[-1]
    lane = lax.broadcasted_iota(jnp.int32, x.shape, x.ndim - 1)
    up = pltpu.roll(x, w - ROPE_DIM // 2, x.ndim - 1)
    down = pltpu.roll(x, ROPE_DIM // 2, x.ndim - 1)
    return jnp.where(lane % ROPE_DIM < ROPE_DIM // 2, up, down)


_swap_halves.defvjp(lambda x: (_swap_halves(x), None), lambda _, ct: (_swap_halves(ct),))


@jax.custom_vjp
def _bdot(a, b):
    return jnp.dot(a.astype(BF16), b.astype(BF16), preferred_element_type=F32)


_bdot.defvjp(lambda a, b: (_bdot(a, b), (a, b)),
             lambda res, ct: (_dot_nt(ct.astype(BF16), res[1].astype(BF16)), _dot_tn(res[0].astype(BF16), ct.astype(BF16))))


def _rope(x, cos2, sin2):
    return x * cos2 + _swap_halves(x) * sin2


def _f_pre(s, shift, scale, g):
    return (_rms(s, g) * (1.0 + scale) + shift,)


def _f_res(s, y, gate):
    return (s + gate * y,)


def _f_res_pre(s, y, gate, shift, scale, g):
    s1 = s + gate * y
    return s1, _rms(s1, g) * (1.0 + scale) + shift


def _f_ab_pre(*args):
    kv_lat, q_lat, kpe = args[0:3]
    us, vs = args[3:7], args[7:11]
    cos2, sin2 = args[11:13]
    kv_norm, q_norm = args[13:15]
    vns, wss, bss = args[15:19], args[19:23], args[23:27]
    outs = [_rms(kv_lat, kv_norm), _rms(q_lat, q_norm), _rope(kpe, cos2, sin2)]
    for u, v, vn, ws, bs in zip(us, vs, vns, wss, bss):
        vg = _rms(jax.nn.gelu(v), vn)
        mixed = _bdot(ws, vg) + bs
        outs.append(jax.nn.gelu(u) * mixed)
    return tuple(outs)


def _f_q_rope(qn, qr, cos2, sin2):
    return qn, _rope(qr, cos2, sin2)


def _f_cd_pre(rk, rv, sk, sv, rq, rg, sq, c256, s256, c128, s128, c512, s512):
    return (_rope(rq, c256, s256), _rope(rk * RET_K_SCALE, c256, s256), rv,
            _rope(sq, c512, s512), _rope(sk, c128, s128), sv, rg)


def _f_cd_merge(*args):
    ys, rgs, rns = args[0:4], args[4:8], args[8:12]
    return tuple(_rms(y, rn) * jax.nn.silu(rg) for y, rg, rn in zip(ys, rgs, rns))


def _dot_nt(a, b):
    return lax.dot_general(a, b, (((1,), (1,)), ((), ())), preferred_element_type=F32)


def _dot_tn(a, b):
    return lax.dot_general(a, b, (((0,), (0,)), ((), ())), preferred_element_type=F32)


def _dot(a, b):
    return jnp.dot(a, b, preferred_element_type=F32)


def _tile_spec(tq, C):
    return pl.BlockSpec((None, tq, C), lambda b, i: (b, i, 0))


def _full_spec(L, C):
    return pl.BlockSpec((None, L, C), lambda b, i: (b, 0, 0))


def _mla_half(h):
    lane = lax.broadcasted_iota(jnp.int32, (1, 128), 1)
    return (lane < 64) if h % 2 == 0 else (lane >= 64)


def _mla_query(q_ref, h):
    pair = q_ref[:, 512 + 128 * (h // 2):640 + 128 * (h // 2)]
    return jnp.concatenate([q_ref[:, 128 * h:128 * h + 128], jnp.where(_mla_half(h), pair, jnp.zeros_like(pair))], axis=1)


def _mla_fill_keys(kcat, kv_ref, kpe_ref):
    kp = kpe_ref[...]
    for h in range(4):
        kcat[h, :, 0:128] = kv_ref[:, 256 * h:256 * h + 128]
        kcat[h, :, 128:256] = jnp.where(_mla_half(h), kp, jnp.zeros_like(kp))


def _mla_fwd(q, kv, kpe, n, tq):
    B, L, _ = q.shape

    def body(q_ref, kv_ref, kpe_ref, o_ref, lse_ref, kcat):
        i = pl.program_id(1)

        @pl.when(i == 0)
        def _():
            _mla_fill_keys(kcat, kv_ref, kpe_ref)

        def tile(keys):
            for h in range(4):
                s = _dot_nt(_mla_query(q_ref, h), kcat[h, keys, :]) * MLA_SCALE
                m = jnp.max(s, axis=1, keepdims=True)
                e = jnp.exp(s - m)
                l = jnp.sum(e, axis=1, keepdims=True)
                p = (e * (1.0 / l)).astype(BF16)
                o_ref[:, 128 * h:128 * h + 128] = _dot(p, kv_ref[keys, 256 * h + 128:256 * h + 256]).astype(o_ref.dtype)
                lse_ref[:, h:h + 1] = m + jnp.log(l)

        pl.when(i < n // tq)(functools.partial(tile, slice(0, L)))
        pl.when(i >= n // tq)(functools.partial(tile, slice(n, L)))

    return _pallas(
        body, name="mla_fwd", grid=(B, L // tq),
        in_specs=[_tile_spec(tq, 768), _full_spec(L, 1024), _full_spec(L, 128)],
        out_specs=[_tile_spec(tq, 512), _tile_spec(tq, 4)],
        out_shape=[_sds((B, L, 512), BF16), _sds((B, L, 4), F32)],
        scratch_shapes=[pltpu.VMEM((4, L, 256), BF16)],
        compiler_params=_params(("parallel", "arbitrary")),
    )(q, kv, kpe)


def _mla_bwd(q, kv, kpe, lse, do, n, tq):
    B, L, _ = q.shape

    def body(q_ref, kv_ref, kpe_ref, lse_ref, do_ref, dq_ref, dkv_ref, dkpe_ref, kcat):
        i = pl.program_id(1)

        @pl.when(i == 0)
        def _():
            dkv_ref[...] = jnp.zeros_like(dkv_ref)
            dkpe_ref[...] = jnp.zeros_like(dkpe_ref)
            _mla_fill_keys(kcat, kv_ref, kpe_ref)

        def tile(keys):
            rope_pair = None
            for h in range(4):
                qc, kc = _mla_query(q_ref, h), kcat[h, keys, :]
                v = kv_ref[keys, 256 * h + 128:256 * h + 256]
                p = jnp.exp(_dot_nt(qc, kc) * MLA_SCALE - lse_ref[:, h:h + 1])
                doh = do_ref[:, 128 * h:128 * h + 128].astype(BF16)
                dp = _dot_nt(doh, v)
                delta = jnp.sum(p * dp, axis=1, keepdims=True)
                ds = (p * (dp - delta) * MLA_SCALE).astype(BF16)
                dqc = _dot(ds, kc)
                dq_ref[:, 128 * h:128 * h + 128] = dqc[:, 0:128]
                if h % 2 == 0:
                    rope_pair = dqc[:, 128:256]
                else:
                    dq_ref[:, 512 + 128 * (h // 2):640 + 128 * (h // 2)] = jnp.where(_mla_half(h), dqc[:, 128:256], rope_pair)
                dkc = _dot_tn(ds, qc)
                dkv_ref[keys, 256 * h:256 * h + 128] += dkc[:, 0:128]
                dkpe_ref[keys, :] += dkc[:, 128:256]
                dkv_ref[keys, 256 * h + 128:256 * h + 256] += _dot_tn(p.astype(BF16), doh)

        pl.when(i < n // tq)(functools.partial(tile, slice(0, L)))
        pl.when(i >= n // tq)(functools.partial(tile, slice(n, L)))

    return _pallas(
        body, name="mla_bwd", grid=(B, L // tq),
        in_specs=[_tile_spec(tq, 768), _full_spec(L, 1024), _full_spec(L, 128), _tile_spec(tq, 4),
                  pl.BlockSpec((None, tq, 512), lambda b, i: (b, i, 0))],
        out_specs=[_tile_spec(tq, 768), _full_spec(L, 1024), _full_spec(L, 128)],
        out_shape=[_sds((B, L, 768), F32), _sds((B, L, 1024), F32), _sds((B, L, 128), F32)],
        scratch_shapes=[pltpu.VMEM((4, L, 256), BF16)],
        compiler_params=_params(("parallel", "arbitrary")),
    )(q, kv, kpe, lse, do)


def _swa_window(i, tq, n):
    W = min(tq + 2 * SWA_WINDOW, n)
    lo = jnp.clip(i * tq - SWA_WINDOW, 0, n - W)
    return pl.multiple_of(lo, 128), W


def _swa_mask(i, lo, tq, W, n):
    qpos = i * tq + lax.broadcasted_iota(jnp.int32, (tq, 1), 0)
    kpos = lo + lax.broadcasted_iota(jnp.int32, (1, W), 1)
    return (jnp.abs(qpos - kpos) <= SWA_WINDOW) & (qpos < n)


def _swa_fill(rep, src_ref):
    for g in range(2):
        rep[:, 256 * g:256 * g + 256] = jnp.concatenate([src_ref[:, 64 * g:64 * g + 64]] * 4, axis=1)


def _swa_head_rows(a):
    lane = lax.broadcasted_iota(jnp.int32, (1, 256), 1)
    return jnp.concatenate([jnp.where(lane // 64 == h, a, jnp.zeros_like(a)) for h in range(4)], axis=0)


def _swa_fold_rows(a, tq):
    lane = lax.broadcasted_iota(jnp.int32, (1, 256), 1)
    out = jnp.where(lane // 64 == 0, a[0:tq], 0.0)
    for h in range(1, 4):
        out = jnp.where(lane // 64 == h, a[h * tq:(h + 1) * tq], out)
    return out


def _swa_fwd(q, k, v, sink, n, tq):
    B, L, _ = q.shape

    def body(q_ref, k_ref, v_ref, sink_ref, o_ref, lse_ref, krep, vrep):
        i = pl.program_id(1)

        @pl.when(i == 0)
        def _():
            _swa_fill(krep, k_ref)
            _swa_fill(vrep, v_ref)

        def tile(band):
            for g in range(2):
                gc = slice(256 * g, 256 * g + 256)
                qs = _swa_head_rows(q_ref[:, gc])
                sk = jnp.concatenate([jnp.broadcast_to(sink_ref[0:1, 4 * g + h:4 * g + h + 1], (tq, 1))
                                      for h in range(4)], axis=0)
                s2 = _dot_nt(qs, krep[n:L, gc]) * SWA_SCALE
                m = jnp.maximum(jnp.max(s2, axis=1, keepdims=True), sk)
                if band:
                    lo, W = _swa_window(i, tq, n)
                    mask = jnp.concatenate([_swa_mask(i, lo, tq, W, n)] * 4, axis=0)
                    s1 = jnp.where(mask, _dot_nt(qs, krep[pl.ds(lo, W), gc]) * SWA_SCALE, NEG_INF)
                    m = jnp.maximum(m, jnp.max(s1, axis=1, keepdims=True))
                e2 = jnp.exp(s2 - m)
                l = jnp.sum(e2, axis=1, keepdims=True) + jnp.exp(sk - m)
                if band:
                    e1 = jnp.exp(s1 - m)
                    l = l + jnp.sum(e1, axis=1, keepdims=True)
                r = 1.0 / l
                o = _dot((e2 * r).astype(BF16), vrep[n:L, gc])
                if band:
                    o = o + _dot((e1 * r).astype(BF16), vrep[pl.ds(lo, W), gc])
                o_ref[:, gc] = _swa_fold_rows(o, tq).astype(o_ref.dtype)
                lse = m + jnp.log(l)
                for h in range(4):
                    lse_ref[:, 4 * g + h:4 * g + h + 1] = lse[h * tq:(h + 1) * tq]

        pl.when(i < n // tq)(functools.partial(tile, True))
        pl.when(i >= n // tq)(functools.partial(tile, False))

    return _pallas(
        body, name="swa_fwd", grid=(B, L // tq),
        in_specs=[_tile_spec(tq, 512), _full_spec(L, 128), _full_spec(L, 128), pl.BlockSpec((1, 8), lambda b, i: (0, 0))],
        out_specs=[_tile_spec(tq, 512), _tile_spec(tq, 8)],
        out_shape=[_sds((B, L, 512), BF16), _sds((B, L, 8), F32)],
        scratch_shapes=[pltpu.VMEM((L, 512), BF16), pltpu.VMEM((L, 512), BF16)],
        compiler_params=_params(("parallel", "arbitrary")),
    )(q, k, v, sink)


def _swa_bwd(q, k, v, sink, lse, do, n, tq):
    B, L, _ = q.shape

    def body(q_ref, k_ref, v_ref, sink_ref, lse_ref, do_ref, dq_ref, dk_ref, dv_ref, dsink_ref):
        i = pl.program_id(1)

        @pl.when(i == 0)
        def _():
            dk_ref[...] = jnp.zeros_like(dk_ref)
            dv_ref[...] = jnp.zeros_like(dv_ref)
            dsink_ref[...] = jnp.zeros_like(dsink_ref)

        def tile(band):
            kc, vc = k_ref[n:L, :], v_ref[n:L, :]
            if band:
                lo, W = _swa_window(i, tq, n)
                mask = _swa_mask(i, lo, tq, W, n)
                kl, vl = k_ref[pl.ds(lo, W), :], v_ref[pl.ds(lo, W), :]
            for h in range(8):
                g = h // 4
                cols = slice(64 * g, 64 * g + 64)
                qh = q_ref[:, 64 * h:64 * h + 64]
                lse_h = lse_ref[:, h:h + 1]
                doh = do_ref[:, 64 * h:64 * h + 64].astype(BF16)
                p2 = jnp.exp(_dot_nt(qh, kc[:, cols]) * SWA_SCALE - lse_h)
                ps = jnp.exp(sink_ref[0:1, h:h + 1] - lse_h)
                dp2 = _dot_nt(doh, vc[:, cols])
                delta = jnp.sum(p2 * dp2, axis=1, keepdims=True)
                if band:
                    p1 = jnp.exp(jnp.where(mask, _dot_nt(qh, kl[:, cols]) * SWA_SCALE, NEG_INF) - lse_h)
                    dp1 = _dot_nt(doh, vl[:, cols])
                    delta = delta + jnp.sum(p1 * dp1, axis=1, keepdims=True)
                ds2 = (p2 * (dp2 - delta) * SWA_SCALE).astype(BF16)
                dq = _dot(ds2, kc[:, cols])
                dk_ref[n:L, cols] += _dot_tn(ds2, qh)
                dv_ref[n:L, cols] += _dot_tn(p2.astype(BF16), doh)
                if band:
                    ds1 = (p1 * (dp1 - delta) * SWA_SCALE).astype(BF16)
                    dq = dq + _dot(ds1, kl[:, cols])
                    dk_ref[pl.ds(lo, W), cols] += _dot_tn(ds1, qh)
                    dv_ref[pl.ds(lo, W), cols] += _dot_tn(p1.astype(BF16), doh)
                dq_ref[:, 64 * h:64 * h + 64] = dq
                dsink_ref[0:1, h:h + 1] += jnp.sum(-ps * delta, axis=0, keepdims=True)

        pl.when(i < n // tq)(functools.partial(tile, True))
        pl.when(i >= n // tq)(functools.partial(tile, False))

    return _pallas(
        body, name="swa_bwd", grid=(B, L // tq),
        in_specs=[_tile_spec(tq, 512), _full_spec(L, 128), _full_spec(L, 128), pl.BlockSpec((1, 8), lambda b, i: (0, 0)),
                  _tile_spec(tq, 8), pl.BlockSpec((None, tq, 512), lambda b, i: (b, i, 1))],
        out_specs=[_tile_spec(tq, 512), _full_spec(L, 128), _full_spec(L, 128),
                   pl.BlockSpec((None, 1, 8), lambda b, i: (b, 0, 0))],
        out_shape=[_sds((B, L, 512), F32), _sds((B, L, 128), F32), _sds((B, L, 128), F32), _sds((B, 1, 8), F32)],
        compiler_params=_params(("parallel", "arbitrary")),
    )(q, k, v, sink, lse, do)


def _ret_decay(i, tq, L, n, ctx_tile):
    qi = i * tq + lax.broadcasted_iota(jnp.int32, (tq, 1), 0)
    kc = lax.broadcasted_iota(jnp.int32, (1, L - n), 1)
    d_hb = (n + kc - qi).astype(F32)
    if ctx_tile:
        return None, ((qi - L) - (kc - (L - n))).astype(F32), d_hb
    d_x = (qi - lax.broadcasted_iota(jnp.int32, (1, n), 1)).astype(F32)
    return d_x, (qi - (kc - (L - n))).astype(F32), d_hb


def _ret_head(a, h):
    lane = lax.broadcasted_iota(jnp.int32, (1, a.shape[1]), 1)
    return jnp.where(lane // 64 == h, a, jnp.zeros_like(a))


def _ret_mask(dist, lg):
    return jnp.where(dist >= 0.0, jnp.exp(lg * jnp.maximum(dist, 0.0)), 0.0)


def _ret_weights(n, L, d_x, d_hf, d_hb, lg_f, lg_b, with_grad):
    m_f, m_b = _ret_mask(d_hf, lg_f), _ret_mask(d_hb, lg_b)
    ctx = (slice(n, L), m_f + m_b)
    if with_grad:
        ctx += (m_f * jnp.maximum(d_hf, 0.0), m_b * jnp.maximum(d_hb, 0.0))
    if d_x is None:
        return [ctx]
    e_x = jnp.exp(jnp.where(d_x >= 0.0, lg_f, -lg_b) * d_x)
    lat = (slice(0, n), jnp.where(d_x == 0.0, 2.0 * e_x, e_x))
    if with_grad:
        lat += (e_x * jnp.maximum(d_x, 0.0), e_x * jnp.maximum(-d_x, 0.0))
    return [lat, ctx]


def _ret_fwd(q, k, v, lg, n, tq):
    B, L, _ = q.shape

    def body(q_ref, k_ref, v_ref, lg_ref, y_ref):
        i = pl.program_id(1)

        def tile(ctx_tile):
            dist = _ret_decay(i, tq, L, n, ctx_tile)
            for h in range(4):
                qh = _ret_head(q_ref[...], h)
                y = None
                for rows, dec_r in _ret_weights(n, L, *dist, lg_ref[0:1, h:h + 1], lg_ref[1:2, h:h + 1], False):
                    a = _dot_nt(qh, k_ref[rows, :]) * dec_r
                    part = _dot(a.astype(BF16), v_ref[rows, 128 * h:128 * h + 128])
                    y = part if y is None else y + part
                y_ref[:, 128 * h:128 * h + 128] = y

        pl.when(i < n // tq)(functools.partial(tile, False))
        pl.when(i >= n // tq)(functools.partial(tile, True))

    return _pallas(
        body, name="ret_fwd", grid=(B, L // tq),
        in_specs=[_tile_spec(tq, 256), _full_spec(L, 256), _full_spec(L, 512), pl.BlockSpec((2, 4), lambda b, i: (0, 0))],
        out_specs=_tile_spec(tq, 512), out_shape=_sds((B, L, 512), F32),
        compiler_params=_params(("parallel", "arbitrary")),
    )(q, k, v, lg)


def _ret_bwd(q, k, v, lg, dy, n, tq):
    B, L, _ = q.shape

    def body(q_ref, k_ref, v_ref, lg_ref, dy_ref, dq_ref, dk_ref, dv_ref, dlg_ref):
        i = pl.program_id(1)

        @pl.when(i == 0)
        def _():
            dk_ref[...] = jnp.zeros_like(dk_ref)
            dv_ref[...] = jnp.zeros_like(dv_ref)
            dlg_ref[...] = jnp.zeros_like(dlg_ref)

        def total(a):
            return jnp.sum(jnp.sum(a, axis=1, keepdims=True), axis=0, keepdims=True)

        def tile(ctx_tile):
            dist = _ret_decay(i, tq, L, n, ctx_tile)
            dq = None
            for h in range(4):
                vc = slice(128 * h, 128 * h + 128)
                qh = _ret_head(q_ref[...], h)
                dyh = dy_ref[:, vc].astype(BF16)
                dqh = None
                for rows, dec_r, wf_r, wb_r in _ret_weights(n, L, *dist, lg_ref[0:1, h:h + 1], lg_ref[1:2, h:h + 1], True):
                    k_all, vh = k_ref[rows, :], v_ref[rows, vc]
                    s = _dot_nt(qh, k_all)
                    gr = _dot_nt(dyh, vh)
                    ds = (gr * dec_r).astype(BF16)
                    part = _dot(ds, k_all)
                    dqh = part if dqh is None else dqh + part
                    dk_ref[rows, :] += _dot_tn(ds, qh)
                    dv_ref[rows, vc] += _dot_tn((s * dec_r).astype(BF16), dyh)
                    gs = gr * s
                    dlg_ref[0:1, h:h + 1] += total(gs * wf_r)
                    dlg_ref[1:2, h:h + 1] += total(gs * wb_r)
                dqh = _ret_head(dqh, h)
                dq = dqh if dq is None else dq + dqh
            dq_ref[...] = dq

        pl.when(i < n // tq)(functools.partial(tile, False))
        pl.when(i >= n // tq)(functools.partial(tile, True))

    return _pallas(
        body, name="ret_bwd", grid=(B, L // tq),
        in_specs=[_tile_spec(tq, 256), _full_spec(L, 256), _full_spec(L, 512), pl.BlockSpec((2, 4), lambda b, i: (0, 0)),
                  _tile_spec(tq, 512)],
        out_specs=[_tile_spec(tq, 256), _full_spec(L, 256), _full_spec(L, 512),
                   pl.BlockSpec((None, 2, 4), lambda b, i: (b, 0, 0))],
        out_shape=[_sds((B, L, 256), F32), _sds((B, L, 256), F32), _sds((B, L, 512), F32), _sds((B, 2, 4), F32)],
        compiler_params=_params(("parallel", "arbitrary")),
    )(q, k, v, lg, dy)


def _loss_head(s, target, g, n, tm):
    B, L, D = s.shape
    nx = n // tm

    def body(s_ref, t_ref, g_ref, ds_ref, dg_ref, loss_ref):
        b, i = pl.program_id(0), pl.program_id(1)

        @pl.when((b == 0) & (i == 0))
        def _():
            dg_ref[...] = jnp.zeros_like(dg_ref)
            loss_ref[...] = jnp.zeros_like(loss_ref)

        @pl.when(i < nx)
        def _():
            y, vjp = jax.vjp(_rms, s_ref[...], g_ref[...])
            err = y - t_ref[...]
            d_s, d_g = vjp(err * (1.0 / D))
            ds_ref[...] = d_s
            dg_ref[...] += d_g
            part = jnp.sum(jnp.sum(err * err, axis=1, keepdims=True), axis=0, keepdims=True) * (0.5 / D)
            loss_ref[...] += jnp.broadcast_to(part, loss_ref.shape)

        @pl.when(i >= nx)
        def _():
            ds_ref[...] = jnp.zeros_like(ds_ref)

    return _pallas(
        body, name="loss_head", grid=(B, L // tm),
        in_specs=[pl.BlockSpec((None, tm, D), lambda b, i: (b, i, 0)),
                  pl.BlockSpec((None, tm, D), lambda b, i: (b, jnp.minimum(i, nx - 1), 0)),
                  pl.BlockSpec((1, D), lambda b, i: (0, 0))],
        out_specs=[pl.BlockSpec((None, tm, D), lambda b, i: (b, i, 0)), pl.BlockSpec((1, D), lambda b, i: (0, 0)),
                   pl.BlockSpec((1, 128), lambda b, i: (0, 0))],
        out_shape=[_sds((B, L, D), F32), _sds((1, D), F32), _sds((1, 128), F32)],
        compiler_params=_params(("arbitrary", "arbitrary")),
    )(s, target, g)


def _ada_fwd(c_all, ada_w, ada_b):
    NL, D, Ns = ada_w.shape
    R = c_all.shape[0]

    def body(c_ref, w_ref, b_ref, o_ref):
        cond = jax.nn.silu(c_ref[...]).astype(BF16)
        o_ref[...] = _dot(cond, w_ref[...].astype(BF16)) + b_ref[...]

    return _pallas(
        body, name="ada_fwd", grid=(NL,),
        in_specs=[pl.BlockSpec((R, D), lambda l: (0, 0)), pl.BlockSpec((None, D, Ns), lambda l: (l, 0, 0)),
                  pl.BlockSpec((None, 1, Ns), lambda l: (l, 0, 0))],
        out_specs=pl.BlockSpec((None, R, Ns), lambda l: (l, 0, 0)), out_shape=_sds((NL, R, Ns), F32),
        compiler_params=_params(("parallel",)),
    )(c_all, ada_w, ada_b)


def _ada_bwd(c_all, ada_w, dmods):
    NL, D, Ns = ada_w.shape
    R = c_all.shape[0]

    def body(c_ref, w_ref, dm_ref, dw_ref, dc_ref):
        cond = jax.nn.silu(c_ref[...]).astype(BF16)
        dm = dm_ref[...].astype(BF16)
        dw_ref[...] = _dot_tn(cond, dm)
        dc_ref[...] = _dot_nt(dm, w_ref[...].astype(BF16))

    return _pallas(
        body, name="ada_bwd", grid=(NL,),
        in_specs=[pl.BlockSpec((R, D), lambda l: (0, 0)), pl.BlockSpec((None, D, Ns), lambda l: (l, 0, 0)),
                  pl.BlockSpec((None, R, Ns), lambda l: (l, 0, 0))],
        out_specs=[pl.BlockSpec((None, D, Ns), lambda l: (l, 0, 0)), pl.BlockSpec((None, R, D), lambda l: (l, 0, 0))],
        out_shape=[_sds((NL, D, Ns), F32), _sds((NL, R, D), F32)],
        compiler_params=_params(("parallel",)),
    )(c_all, ada_w, dmods)


def _my_index():
    return 4 * lax.axis_index("x") + 2 * lax.axis_index("y") + lax.axis_index("c")


def _peer(k):
    x, y, c = lax.axis_index("x"), lax.axis_index("y"), lax.axis_index("c")
    kx, ky, kc = (k >> 2) & 1, (k >> 1) & 1, k & 1
    px, py, pc = (x + kx) % 2, (y + ky) % 2, (c + kc) % 2
    return (px, py, pc), 4 * px + 2 * py + pc


def _all_gather(name, shards):
    na = len(shards)
    hbm = pl.BlockSpec(memory_space=pl.ANY)

    def body(*refs):
        in_refs, out_refs = refs[:na], refs[na:2 * na]
        send_sems, recv_sems, local_sems = refs[2 * na:]
        me = _my_index()
        sib_id, sib = _peer(1)
        chips = [_peer(k) for k in (4, 2, 6)]
        sib_chips = [4 * px + 2 * py + (1 - pc) for (px, py, pc), _ in chips]

        def copy(a, k, slot, to, src=None):
            dst = out_refs[a].at[slot]
            return pltpu.make_async_remote_copy(
                src_ref=dst if src is None else src, dst_ref=dst, send_sem=send_sems.at[a, k],
                recv_sem=recv_sems.at[a, k], device_id=to, device_id_type=MESH)

        first, passed, mine = [], [], []
        for a in range(na):
            cp = pltpu.make_async_copy(in_refs[a], out_refs[a].at[me], local_sems.at[a])
            cp.start()
            mine.append(cp)
            first.append(copy(a, 0, me, sib_id, src=in_refs[a]))
            first += [copy(a, 1 + j, me, pid, src=in_refs[a]) for j, (pid, _) in enumerate(chips)]
        for cp in first:
            cp.start()
        for a in range(na):
            for j, (pid, pidx) in enumerate(chips):
                copy(a, 1 + j, pidx, pid).wait_recv()
                fwd = copy(a, 4 + j, pidx, sib_id)
                fwd.start()
                passed.append(fwd)
        for a in range(na):
            copy(a, 0, sib, sib_id).wait_recv()
            for j in range(3):
                copy(a, 4 + j, sib_chips[j], sib_id).wait_recv()
        for cp in first + passed:
            cp.wait_send()
        for cp in mine:
            cp.wait()

    return _pallas(
        body, name=name, in_specs=[hbm] * na, out_specs=[hbm] * na,
        out_shape=[_sds((N_DEV,) + s.shape, s.dtype) for s in shards],
        scratch_shapes=[pltpu.SemaphoreType.DMA((na, 7)), pltpu.SemaphoreType.DMA((na, 7)),
                        pltpu.SemaphoreType.DMA((na,))],
    )(*shards)


_HBM = pl.BlockSpec(memory_space=pltpu.HBM)
_SEM = pl.BlockSpec(memory_space=pltpu.SEMAPHORE)
_DATAFLOW = pltpu.SideEffectType.DATAFLOW_SIDE_EFFECTING


def _exchange_start(name, srcs, slotted, after=()):
    na = len(srcs)
    lands = [lax.empty((N_DEV,) + (s.shape[1:] if slotted else s.shape), s.dtype) for s in srcs]

    def body(*refs):
        src_refs, land_refs = refs[:na], refs[na:2 * na]
        outs = refs[2 * na + len(after):]
        send_sems, recv_sems, token = outs[:na], outs[na:2 * na], outs[4 * na]
        me = _my_index()
        for a in range(na):
            for k in range(1, N_DEV):
                pid, pidx = _peer(k)
                pltpu.make_async_remote_copy(
                    src_ref=src_refs[a].at[pidx] if slotted else src_refs[a], dst_ref=land_refs[a].at[me],
                    send_sem=send_sems[a], recv_sem=recv_sems[a], device_id=pid, device_id_type=MESH).start()
        token[...] = jnp.zeros_like(token)

    ops = [pltpu.with_memory_space_constraint(a, pltpu.HBM) for a in list(srcs) + lands]
    outs = _pallas(
        body, name=name,
        out_shape=[pltpu.SemaphoreType.DMA(())] * (2 * na) + [pltpu.HBM(a.shape, a.dtype) for a in ops]
        + [_sds((8, 128), F32)],
        in_specs=[_HBM] * (2 * na) + [pl.BlockSpec(memory_space=pl.ANY)] * len(after),
        out_specs=[_SEM] * (2 * na) + [_HBM] * (2 * na) + [pl.BlockSpec(memory_space=pltpu.VMEM)],
        input_output_aliases={a: 2 * na + a for a in range(2 * na)},
        compiler_params=pltpu.CompilerParams(has_side_effects=_DATAFLOW),
    )(*ops, *after)
    return (na, outs[:2 * na], outs[2 * na:4 * na]), outs[4 * na]


def _exchange_wait(name, handle, after):
    na, sems, thru = handle

    def body(*refs):
        land_refs = refs[na:2 * na]
        send_sems, recv_sems = refs[2 * na:3 * na], refs[3 * na:4 * na]
        me_id = (lax.axis_index("x"), lax.axis_index("y"), lax.axis_index("c"))
        for a in range(na):
            seven = land_refs[a].at[pl.ds(0, N_DEV - 1)]
            drain = pltpu.make_async_remote_copy(src_ref=seven, dst_ref=seven, send_sem=send_sems[a],
                                                 recv_sem=recv_sems[a], device_id=me_id, device_id_type=MESH)
            drain.wait_send()
            drain.wait_recv()

    outs = _pallas(
        body, name=name, out_shape=[pltpu.HBM(a.shape, a.dtype) for a in thru],
        in_specs=[_HBM] * (2 * na) + [_SEM] * (2 * na) + [pl.BlockSpec(memory_space=pl.ANY)],
        out_specs=[_HBM] * (2 * na), input_output_aliases={a: a for a in range(2 * na)},
        compiler_params=pltpu.CompilerParams(has_side_effects=_DATAFLOW),
    )(*thru, *sems, after)
    return outs[na:]


def _fill_own(name, landed, owns, slotted):
    me = _my_index()
    return [jnp.where(lax.broadcasted_iota(jnp.int32, land.shape, 0) == me, own, land)
            for land, own in zip(landed, owns)]


def _rope_tables(n, L, width):
    t = jnp.arange(n)
    row = (t // GRID_W).astype(F32)
    col = (t % GRID_W).astype(F32)
    n_freq = ROPE_DIM // 4
    freqs = ROPE_THETA ** (-jnp.arange(n_freq, dtype=F32) / n_freq)
    ang = jnp.concatenate([row[:, None] * freqs, col[:, None] * freqs], axis=-1)
    cos, sin = jnp.cos(ang), jnp.sin(ang)
    cos2 = jnp.concatenate([cos, cos], axis=-1)
    sin2 = jnp.concatenate([-sin, sin], axis=-1)
    cos2 = jnp.concatenate([cos2, jnp.ones((L - n, ROPE_DIM), F32)], axis=0)
    sin2 = jnp.concatenate([sin2, jnp.zeros((L - n, ROPE_DIM), F32)], axis=0)
    reps = width // ROPE_DIM
    return jnp.tile(cos2, (1, reps)), jnp.tile(sin2, (1, reps))


def _ab_perm(w):
    return jnp.concatenate([w[:, 0:256], w[:, 320:1600], w[:, 256:320], w[:, 256:320]], axis=1)


def _ab_unperm(g):
    rope_key = (g[:, 1536:1600].astype(F32) + g[:, 1600:1664].astype(F32)).astype(g.dtype)
    return jnp.concatenate([g[:, 0:256], rope_key, g[:, 256:1536]], axis=1)


def _wq_perm(w):
    return jnp.concatenate([w[:, 192 * h:192 * h + 128] for h in range(4)]
                           + [w[:, 192 * h + 128:192 * h + 192] for h in range(4)], axis=1)


def _wq_unperm(g):
    return jnp.concatenate([g[:, sl] for h in range(4)
                            for sl in (slice(128 * h, 128 * h + 128), slice(512 + 64 * h, 576 + 64 * h))], axis=1)


def _flat(a):
    return a.reshape(-1, a.shape[-1])


def _layer_weights(full, p):
    NL, NE, NO = len(full["ffn_in"]), len(full["ab_in"]), len(full["cd_in"])
    groups = range(4)

    def each(f, mats):
        return [None if w is None else f(w) for w in mats]

    return dict(
        norm_mix=[p["norm_mix"][l][None] for l in range(NL)], norm_ffn=[p["norm_ffn"][l][None] for l in range(NL)],
        norm_final=p["norm_final"][None],
        ffn_in=each(_ffn_interleave, full["ffn_in"]), ffn_out=list(full["ffn_out"]),
        ab_in=each(_ab_perm, full["ab_in"]), ab_out=list(full["ab_out"]),
        wq=each(_wq_perm, full["mla_wq_b"]), wkv=list(full["mla_wkv_b"]),
        kv_norm=[p["mla_kv_norm"][j][None] for j in range(NE)], q_norm=[p["mla_q_norm"][j][None] for j in range(NE)],
        v_norm=[[p["cmlp_v_norm"][j][None, 128 * g:128 * g + 128] for g in groups] for j in range(NE)],
        ws=[[p["cmlp_ws"][j, g] for g in groups] for j in range(NE)],
        bs=[[p["cmlp_bs"][j, g][:, None] for g in groups] for j in range(NE)],
        cd_in=list(full["cd_in"]), cd_out=list(full["cd_out"]),
        lg=[jnp.stack([jax.nn.log_sigmoid(p["ret_decay_fwd"][j]), jax.nn.log_sigmoid(p["ret_decay_bwd"][j])])
            for j in range(NO)],
        sink=[p["swa_sink"][j][None] for j in range(NO)],
        ret_norm=[[p["ret_norm"][j][None, 128 * g:128 * g + 128] for g in groups] for j in range(NO)],
    )


def _local_step(x, ctx, target, mods, W, later_weights=None, early_grads=None):
    B, n, D = x.shape
    m = ctx.shape[1]
    L = n + m
    NL = mods.shape[0]
    tm = min(256, m)
    tq = min(256, m)
    cos512, sin512 = _rope_tables(n, L, 512)
    s = jnp.concatenate([x, ctx], axis=1)
    saved = []

    def rows(name, f, arrays, pieces, samp, samp_pieces, glob, out_arrays, out_pieces, tile=tm):
        return _Rows(name, f, B, L, n, tile, arrays, pieces, samp, samp_pieces, glob, out_arrays, out_pieces)

    def full(width, start=0):
        return [(0, start, width)]

    for l in range(NL):
        j = l // 2
        even = l % 2 == 0
        md = mods[l]
        if l in (1, 2) and later_weights is not None:
            W = later_weights(l, s)
        r = {}
        if l == 0:
            r["pre1"] = rows("pre_mix0", _f_pre, [s], full(D), [md], [(0, 0), (0, 1)], [W["norm_mix"][0]], [(D, BF16)], full(D))
            (xn,) = r["pre1"].fwd()
        r["xn"] = xn
        if even:
            z = _mm(f"ab_in{l}", _flat(xn), W["ab_in"][j]).reshape(B, L, 1664)
            pieces = [(0, 0, 256), (0, 256, 256), (0, 1536, 128)]
            pieces += [(0, 512 + 128 * g, 128) for g in range(4)] + [(0, 1024 + 128 * g, 128) for g in range(4)]
            pieces += [(1, 0, 128), (2, 0, 128)]
            glob = [W["kv_norm"][j], W["q_norm"][j]] + W["v_norm"][j] + W["ws"][j] + W["bs"][j]
            abp = rows(f"ab_pre{l}", _f_ab_pre, [z, cos512, sin512], pieces, [], [], glob,
                       [(256, BF16), (256, BF16), (128, BF16), (512, BF16)],
                       [(0, 0, 256), (1, 0, 256), (2, 0, 128)] + [(3, 128 * g, 128) for g in range(4)], tile=128)
            kvn, qn, kpe, cm = abp.fwd()
            kv = _mm(f"wkv{l}", _flat(kvn), W["wkv"][j], out_dtype=BF16).reshape(B, L, 1024)
            q0 = _mm(f"wq{l}", _flat(qn), W["wq"][j]).reshape(B, L, 768)
            qrp = rows(f"q_rope{l}", _f_q_rope, [q0, cos512, sin512], [(0, 0, 512), (0, 512, 256), (1, 0, 256), (2, 0, 256)],
                       [], [], [], [(768, BF16)], [(0, 0, 512), (0, 512, 256)])
            (q,) = qrp.fwd()
            o, lse = _mla_fwd(q, kv, kpe, n, tq)
            y = _mm_pair(f"ab_out{l}", _flat(o), _flat(cm), W["ab_out"][j], BF16).reshape(B, L, D)
            r.update(z=z, abp=abp, kvn=kvn, qn=qn, kpe=kpe, cm=cm, kv=kv, qrp=qrp, q=q, o=o, lse=lse)
        else:
            z = _mm(f"cd_in{l}", _flat(xn), W["cd_in"][j]).reshape(B, L, 2304)
            pieces = [(0, 0, 256), (0, 256, 512), (0, 768, 128), (0, 896, 128), (0, 1024, 256), (0, 1280, 512), (0, 1792, 512)]
            pieces += [(1, 0, 256), (2, 0, 256), (1, 0, 128), (2, 0, 128), (1, 0, 512), (2, 0, 512)]
            cdp = rows(f"cd_pre{l}", _f_cd_pre, [z, cos512, sin512], pieces, [], [], [],
                       [(256, BF16), (256, BF16), (512, BF16), (512, BF16), (128, BF16), (128, BF16), (512, F32)],
                       [(k_, 0, w_) for k_, w_ in enumerate((256, 256, 512, 512, 128, 128, 512))])
            rq, rk, rv, sq, sk, sv, rg = cdp.fwd()
            yret = _ret_fwd(rq, rk, rv, W["lg"][j], n, tq)
            osw, lse = _swa_fwd(sq, sk, sv, W["sink"][j], n, tq)
            mrg = rows(f"cd_merge{l}", _f_cd_merge, [yret, rg],
                       [(0, 128 * g, 128) for g in range(4)] + [(1, 128 * g, 128) for g in range(4)], [], [],
                       W["ret_norm"][j], [(512, BF16)], [(0, 128 * g, 128) for g in range(4)])
            (yr,) = mrg.fwd()
            y = _mm_pair(f"cd_out{l}", _flat(yr), _flat(osw), W["cd_out"][j], BF16).reshape(B, L, D)
            r.update(z=z, cdp=cdp, rq=rq, rk=rk, rv=rv, sq=sq, sk=sk, sv=sv, rg=rg, yret=yret, osw=osw, lse=lse,
                     mrg=mrg, yr=yr)
        two, outs2 = [(0, 0, D), (1, 0, D)], [(D, F32), (D, BF16)]
        r["mix_out"] = rows(f"mix_res_pre{l}", _f_res_pre, [s, y], two, [md], [(0, 2), (0, 3), (0, 4)],
                            [W["norm_ffn"][l]], outs2, two)
        s1, xn2 = r["mix_out"].fwd()
        if l == 0 and later_weights is not None:
            W = later_weights(0, s1)
        y2 = _ffn_fwd(f"ffn{l}", _flat(xn2), W["ffn_in"][l], W["ffn_out"][l]).reshape(B, L, D)
        if l < NL - 1:
            r["ffn_out"] = rows(f"ffn_res_pre{l}", _f_res_pre, [s1, y2], two, [md, mods[l + 1]], [(0, 5), (1, 0), (1, 1)],
                                [W["norm_mix"][l + 1]], outs2, two)
            s, xn = r["ffn_out"].fwd()
        else:
            r["ffn_out"] = rows(f"res_ffn{l}", _f_res, [s1, y2], two, [md], [(0, 5)], [], [(D, F32)], full(D))
            (s,) = r["ffn_out"].fwd()
        r["xn2"] = xn2
        saved.append(r)

    ds, d_norm_final, loss = _loss_head(s, target, W["norm_final"], n, tm)

    G = {k: [None] * len(v) for k, v in W.items() if isinstance(v, list)}
    G["norm_final"] = d_norm_final
    dm = [[None] * 6 for _ in range(NL)]
    dxn_next = None
    for l in reversed(range(NL)):
        j = l // 2
        even = l % 2 == 0
        r = saved[l]
        zero = early_grads(0, G) if l == 0 and early_grads is not None else None
        if l == NL - 1:
            (ds1, dy2), (dm[l][5],), _ = r["ffn_out"].bwd([ds], {0: F32, 1: BF16}, unread=(0,))
        else:
            smp = None if zero is None else [mods[l] + zero, mods[l + 1] + zero]
            (ds1, dy2), (dm[l][5], dm[l + 1][0], dm[l + 1][1]), (G["norm_mix"][l + 1],) = r["ffn_out"].bwd(
                [ds, dxn_next], {0: F32, 1: BF16}, grad_glob=(0,), samp=smp)
        dy2f, xn2f = _flat(dy2), _flat(r["xn2"])
        hid, dz2 = _ffn_mid_bwd(f"ffn_mid{l}", xn2f, dy2f, W["ffn_in"][l], W["ffn_out"][l])
        G["ffn_out"][l] = _mm(f"g_ffn_out{l}", hid, dy2f, ta=True, out_dtype=BF16)
        G["ffn_in"][l] = _mm(f"g_ffn_in{l}", xn2f, dz2, ta=True, out_dtype=BF16)
        dxn2 = _mm(f"d_xn2{l}", dz2, W["ffn_in"][l], tb=True, out_dtype=BF16).reshape(B, L, D)
        zero = early_grads(1, G) if l == 0 and early_grads is not None else None
        smp = None if zero is None else [mods[l] + zero]
        (ds0, dy), (dm[l][2], dm[l][3], dm[l][4]), (G["norm_ffn"][l],) = r["mix_out"].bwd(
            [ds1, dxn2], {0: F32, 1: BF16}, grad_glob=(0,), samp=smp)
        dyf = _flat(dy)
        if even:
            w_out = W["ab_out"][j]
            dcat = _mm(f"d_cat{l}", dyf, w_out, tb=True, out_dtype=BF16).reshape(B, L, -1)
            G["ab_out"][l // 2] = jnp.concatenate(
                [_mm(f"g_ab_out_a{l}", _flat(r["o"]), dyf, ta=True, out_dtype=BF16),
                 _mm(f"g_ab_out_b{l}", _flat(r["cm"]), dyf, ta=True, out_dtype=BF16)], axis=0)
            dq, dkv, dkpe = _mla_bwd(r["q"], r["kv"], r["kpe"], r["lse"], dcat, n, tq)
            (dq0,), _, _ = r["qrp"].bwd([dq], {0: BF16}, unread=(0,))
            dq0f, dkvf = _flat(dq0), _flat(dkv)
            G["wq"][j] = _mm(f"g_wq{l}", _flat(r["qn"]), dq0f, ta=True, out_dtype=BF16)
            G["wkv"][j] = _mm(f"g_wkv{l}", _flat(r["kvn"]), dkvf, ta=True, out_dtype=BF16)
            dqn = _mm(f"d_qn{l}", dq0f, W["wq"][j], tb=True).reshape(B, L, 256)
            dkvn = _mm(f"d_kvn{l}", dkvf, W["wkv"][j], tb=True).reshape(B, L, 256)
            (dz,), _, gg = r["abp"].bwd([dkvn, dqn, dkpe, (dcat, 512)], {0: BF16}, grad_glob=tuple(range(14)))
            G["kv_norm"][j], G["q_norm"][j] = gg[0], gg[1]
            G["v_norm"][j], G["ws"][j], G["bs"][j] = list(gg[2:6]), list(gg[6:10]), list(gg[10:14])
            w_in, key = W["ab_in"][j], "ab_in"
        else:
            w_out = W["cd_out"][j]
            dcat = _mm(f"d_cat{l}", dyf, w_out, tb=True, out_dtype=BF16).reshape(B, L, -1)
            G["cd_out"][j] = jnp.concatenate(
                [_mm(f"g_cd_out_a{l}", _flat(r["yr"]), dyf, ta=True, out_dtype=BF16),
                 _mm(f"g_cd_out_b{l}", _flat(r["osw"]), dyf, ta=True, out_dtype=BF16)], axis=0)
            (dyret, drg), _, gg = r["mrg"].bwd([(dcat, 0)], {0: F32, 1: F32}, grad_glob=(0, 1, 2, 3))
            G["ret_norm"][j] = list(gg)
            drq, drk, drv, dlg = _ret_bwd(r["rq"], r["rk"], r["rv"], W["lg"][j], dyret, n, tq)
            dsq, dsk, dsv, dsink = _swa_bwd(r["sq"], r["sk"], r["sv"], W["sink"][j], r["lse"], dcat, n, tq)
            G["lg"][j], G["sink"][j] = dlg, dsink
            (dz,), _, _ = r["cdp"].bwd([drq, drk, drv, dsq, dsk, dsv, drg], {0: BF16}, unread=(0,))
            w_in, key = W["cd_in"][j], "cd_in"
        dzf = _flat(dz)
        G[key][j] = _mm(f"g_{key}{l}", _flat(r["xn"]), dzf, ta=True, out_dtype=BF16)
        dxn = _mm(f"d_xn{l}", dzf, w_in, tb=True, out_dtype=BF16).reshape(B, L, D)
        if l == 0 and early_grads is not None:
            smp = [mods[0] + early_grads(2, G)]
        if l == 0:
            (ds,), (dm[0][0], dm[0][1]), (G["norm_mix"][0],) = r["pre1"].bwd([dxn], {0: F32}, grad_glob=(0,),
                                                                            add={0: ds0}, samp=smp)
        else:
            ds, dxn_next = ds0, dxn
    dmods = jnp.stack([jnp.concatenate(d, axis=2) for d in dm])
    return loss, ds[:, :n], dmods, G


def kernel(x, c, ctx, c_ctx, ada_w, ada_b, norm_mix, norm_ffn, norm_final, ffn_in, ffn_out, ab_in, ab_out, mla_q_norm, mla_kv_norm, mla_wq_b, mla_wkv_b, cmlp_v_norm, cmlp_ws, cmlp_bs, cd_in, cd_out, ret_decay_fwd, ret_decay_bwd, ret_norm, swa_sink, loss_target, m_c_ctx, m_ada_w, m_ada_b, m_norm_mix, m_norm_ffn, m_norm_final, m_ffn_in, m_ffn_out, m_ab_in, m_ab_out, m_mla_q_norm, m_mla_kv_norm, m_mla_wq_b, m_mla_wkv_b, m_cmlp_v_norm, m_cmlp_ws, m_cmlp_bs, m_cd_in, m_cd_out, m_ret_decay_fwd, m_ret_decay_bwd, m_ret_norm, m_swa_sink, v_c_ctx, v_ada_w, v_ada_b, v_norm_mix, v_norm_ffn, v_norm_final, v_ffn_in, v_ffn_out, v_ab_in, v_ab_out, v_mla_q_norm, v_mla_kv_norm, v_mla_wq_b, v_mla_wkv_b, v_cmlp_v_norm, v_cmlp_ws, v_cmlp_bs, v_cd_in, v_cd_out, v_ret_decay_fwd, v_ret_decay_bwd, v_ret_norm, v_swa_sink):
    B, n, D = x.shape
    NL = ada_w.shape[0]
    NE, NO = ab_in.shape[0], cd_in.shape[0]
    me = _my_index()
    weights = dict(c_ctx=c_ctx, ada_w=ada_w, ada_b=ada_b, norm_mix=norm_mix, norm_ffn=norm_ffn, norm_final=norm_final,
                   ffn_in=ffn_in, ffn_out=ffn_out, ab_in=ab_in, ab_out=ab_out, mla_q_norm=mla_q_norm,
                   mla_kv_norm=mla_kv_norm, mla_wq_b=mla_wq_b, mla_wkv_b=mla_wkv_b, cmlp_v_norm=cmlp_v_norm,
                   cmlp_ws=cmlp_ws, cmlp_bs=cmlp_bs, cd_in=cd_in, cd_out=cd_out, ret_decay_fwd=ret_decay_fwd,
                   ret_decay_bwd=ret_decay_bwd, ret_norm=ret_norm, swa_sink=swa_sink)
    moments_m = dict(c_ctx=m_c_ctx, ada_w=m_ada_w, ada_b=m_ada_b, norm_mix=m_norm_mix, norm_ffn=m_norm_ffn,
                     norm_final=m_norm_final, ffn_in=m_ffn_in, ffn_out=m_ffn_out, ab_in=m_ab_in, ab_out=m_ab_out,
                     mla_q_norm=m_mla_q_norm, mla_kv_norm=m_mla_kv_norm, mla_wq_b=m_mla_wq_b, mla_wkv_b=m_mla_wkv_b,
                     cmlp_v_norm=m_cmlp_v_norm, cmlp_ws=m_cmlp_ws, cmlp_bs=m_cmlp_bs, cd_in=m_cd_in, cd_out=m_cd_out,
                     ret_decay_fwd=m_ret_decay_fwd, ret_decay_bwd=m_ret_decay_bwd, ret_norm=m_ret_norm,
                     swa_sink=m_swa_sink)
    moments_v = dict(c_ctx=v_c_ctx, ada_w=v_ada_w, ada_b=v_ada_b, norm_mix=v_norm_mix, norm_ffn=v_norm_ffn,
                     norm_final=v_norm_final, ffn_in=v_ffn_in, ffn_out=v_ffn_out, ab_in=v_ab_in, ab_out=v_ab_out,
                     mla_q_norm=v_mla_q_norm, mla_kv_norm=v_mla_kv_norm, mla_wq_b=v_mla_wq_b, mla_wkv_b=v_mla_wkv_b,
                     cmlp_v_norm=v_cmlp_v_norm, cmlp_ws=v_cmlp_ws, cmlp_bs=v_cmlp_bs, cd_in=v_cd_in, cd_out=v_cd_out,
                     ret_decay_fwd=v_ret_decay_fwd, ret_decay_bwd=v_ret_decay_bwd, ret_norm=v_ret_norm,
                     swa_sink=v_swa_sink)
    order = list(weights)

    Ns = ada_w.shape[2]
    (c_g,) = _all_gather("gather_c", [c])
    R = N_DEV * B + 8
    c_all = jnp.concatenate([c_g.reshape(N_DEV * B, D), jnp.broadcast_to(c_ctx[None], (8, D))], axis=0)
    ada_b_mine = lax.dynamic_slice_in_dim(ada_b, me * Ns, Ns, axis=1)[:, None, :]
    mods_shard = _ada_fwd(c_all, ada_w, ada_b_mine)
    (mods_g,) = _all_gather("gather_mods", [mods_shard])
    mods_full = jnp.transpose(mods_g, (1, 2, 0, 3)).reshape(NL, R, 6, D)
    mx = lax.dynamic_slice_in_dim(mods_full, me * B, B, axis=1)
    mh = jnp.broadcast_to(mods_full[:, N_DEV * B][:, None], (NL, B, 6, D))
    mods = jnp.stack([mx, mh], axis=2)

    big = ["ffn_in", "ffn_out", "ab_in", "ab_out", "cd_in", "cd_out", "mla_wq_b", "mla_wkv_b"]
    col_sharded = {"ffn_in", "ab_in", "cd_in", "mla_wq_b", "mla_wkv_b"}
    shards = {k: _to_bf16("cast_" + k, weights[k]) for k in big}
    first = {k: 0 if k.startswith("cd_") else 1 for k in big}
    a_keys = [k for k in big if first[k] and not k.startswith("ffn_")]
    b_keys = ["ffn_in", "ffn_out"]
    early = _all_gather("gather_wA", [shards[k][:1] for k in a_keys])
    (rn_g,) = _all_gather("gather_ret_norm", [ret_norm])
    small_first = [mods, rn_g] + list(early)
    w_groups = [
        ("B", {k: (0, 1) for k in b_keys}),
        ("C1", {"ffn_in": (1, 2), "ffn_out": (1, 2), "cd_in": (0, 1), "cd_out": (0, 1)}),
        ("C2", {k: (2, NL) if k.startswith("ffn_") else (1, weights[k].shape[0]) for k in big}),
    ]
    w_handles, w_after = [], small_first
    for gname, members in w_groups:
        handle, token = _exchange_start(f"gather_w{gname}_start", [shards[k][lo:hi] for k, (lo, hi) in members.items()],
                                        False, w_after)
        w_handles.append(handle)
        w_after = w_after + [token]
        mods = mods + token[0, 0]

    def unshard(k, g):
        if k in col_sharded:
            f = jnp.transpose(g, (1, 2, 0, 3)).reshape(g.shape[1], g.shape[2], -1)
        else:
            f = jnp.transpose(g, (1, 0, 2, 3)).reshape(g.shape[1], -1, g.shape[3])
        return [f[i] for i in range(f.shape[0])]

    rn_full = jnp.transpose(rn_g, (1, 0, 2)).reshape(NO, -1)
    small_p = dict(weights, ret_norm=rn_full)
    full0 = {k: [None] * weights[k].shape[0] for k in big}
    for k, g in zip(a_keys, early):
        full0[k][:1] = unshard(k, g)

    def later_weights(stage, newest):
        gname, members = w_groups[stage]
        landed = _exchange_wait(f"gather_w{gname}_wait", w_handles[stage], newest)
        landed = _fill_own(f"own_w{gname}", landed, [shards[k][lo:hi] for k, (lo, hi) in members.items()], False)
        for (k, (lo, hi)), land in zip(members.items(), landed):
            full0[k][lo:hi] = unshard(k, land)
        return _layer_weights(full0, small_p)

    def to_slots(k, gl):
        g = jnp.stack(gl)
        if k == "ffn_in":
            half = N_DEV // 2
            g = g.reshape(g.shape[0], 2, g.shape[2], half, -1)
            return jnp.transpose(g, (1, 3, 0, 2, 4)).reshape(N_DEV, g.shape[0], g.shape[2], -1)
        if k in col_sharded:
            return jnp.transpose(g.reshape(g.shape[0], g.shape[1], N_DEV, -1), (2, 0, 1, 3))
        return jnp.transpose(g.reshape(g.shape[0], N_DEV, -1, g.shape[2]), (1, 0, 2, 3))

    def big_grads(G):
        return dict(ffn_in=[g if g is None else _ffn_deinterleave(g) for g in G["ffn_in"]], ffn_out=G["ffn_out"],
                    ab_in=[g if g is None else _ab_unperm(g) for g in G["ab_in"]],
                    ab_out=G["ab_out"], cd_in=G["cd_in"], cd_out=G["cd_out"],
                    mla_wq_b=[g if g is None else _wq_unperm(g) for g in G["wq"]], mla_wkv_b=G["wkv"])

    sent = {}

    def early_grads(stage, G):
        parts = big_grads(G)
        if stage == 0:
            srcs = [to_slots(k, parts[k][first[k]:]) for k in big]
            handle, g_token = _exchange_start("scatter_gC_start", srcs, slotted=True)
        elif stage == 1:
            srcs = [to_slots(k, parts[k][:1]) for k in b_keys]
            handle, g_token = _exchange_start("scatter_gB_start", srcs, slotted=True)
        else:
            srcs = [to_slots(k, parts[k][:1]) for k in a_keys]
            handle, g_token = _exchange_start("scatter_gA_start", srcs, slotted=True)
        sent[stage] = (handle, srcs)
        return g_token[0, 0]

    loss_part, grad_x, dmods, G = _local_step(x, ctx, loss_target, mods, _layer_weights(full0, small_p),
                                              later_weights, early_grads)

    dmx = dmods[:, :, 0].reshape(NL, B, 6 * D)
    dmh = jnp.sum(dmods[:, :, 1], axis=1).reshape(NL, 1, 6 * D)
    (dm_g,) = _all_gather("gather_dmods", [jnp.concatenate([dmx, dmh], axis=1)])
    dmx_all = jnp.transpose(dm_g[:, :, :B], (1, 0, 2, 3)).reshape(NL, N_DEV * B, 6 * D)
    dmh_all = jnp.sum(dm_g[:, :, B], axis=0)
    dm_rows = jnp.concatenate([dmx_all, dmh_all[:, None], jnp.zeros((NL, 7, 6 * D), F32)], axis=1)
    g_ada_b = jnp.sum(dm_rows, axis=1)
    dm_mine = lax.dynamic_slice_in_dim(dm_rows, me * Ns, Ns, axis=2)
    g_ada_w, dcond = _ada_bwd(c_all, ada_w, dm_mine)
    sg = jax.nn.sigmoid(c_ctx)
    d_c_ctx_part = jnp.sum(dcond[:, N_DEV * B], axis=0) * (sg * (1.0 + c_ctx * (1.0 - sg)))

    def cat(parts):
        return jnp.concatenate([p.reshape(-1) for p in parts])

    dlg = jnp.stack([jnp.sum(G["lg"][j], axis=0) for j in range(NO)])
    sig_f, sig_b = jax.nn.sigmoid(-ret_decay_fwd), jax.nn.sigmoid(-ret_decay_bwd)
    small = dict(
        loss=loss_part[0, 0:1],
        c_ctx=d_c_ctx_part,
        norm_mix=cat(G["norm_mix"]), norm_ffn=cat(G["norm_ffn"]), norm_final=G["norm_final"].reshape(-1),
        mla_q_norm=cat(G["q_norm"]), mla_kv_norm=cat(G["kv_norm"]),
        cmlp_v_norm=cat([cat(G["v_norm"][j]) for j in range(NE)]),
        cmlp_ws=cat([jnp.stack(G["ws"][j]) for j in range(NE)]),
        cmlp_bs=cat([jnp.stack([b_[:, 0] for b_ in G["bs"][j]]) for j in range(NE)]),
        ret_decay_fwd=(dlg[:, 0] * sig_f).reshape(-1), ret_decay_bwd=(dlg[:, 1] * sig_b).reshape(-1),
        ret_norm=cat([cat(G["ret_norm"][j]) for j in range(NO)]),
        swa_sink=cat([jnp.sum(G["sink"][j], axis=0) for j in range(NO)]),
    )
    small_keys = list(small)
    sizes = [small[k].shape[0] for k in small_keys]
    total = sum(sizes)
    padded = -(-total // 2048) * 2048
    packed = jnp.concatenate([small[k] for k in small_keys] + [jnp.zeros((padded - total,), F32)]).reshape(-1, 128)
    (small_g,) = _all_gather("gather_small", [packed])

    landed_a = _fill_own("own_gA", _exchange_wait("scatter_gA_wait", sent[2][0], small_g), sent[2][1], True)
    landed0 = dict(zip(a_keys, landed_a))
    landed_b = _fill_own("own_gB", _exchange_wait("scatter_gB_wait", sent[1][0], grad_x), sent[1][1], True)
    landed0.update(zip(b_keys, landed_b))
    landed_c = _fill_own("own_gC", _exchange_wait("scatter_gC_wait", sent[0][0], grad_x), sent[0][1], True)
    landed = [[landed0[k], rest] if first[k] else [rest] for k, rest in zip(big, landed_c)]

    grads, deltas, new_m, new_v = {}, {}, {}, {}
    for k, land in zip(big, landed):
        grads[k], deltas[k], new_m[k], new_v[k] = _adamw_from_slots("adamw_" + k, weights[k], moments_m[k], moments_v[k], land)
    deltas["ada_w"], new_m["ada_w"], new_v["ada_w"] = [
        o.reshape(ada_w.shape) for o in _adamw("adamw_ada_w", _flat(ada_w), _flat(g_ada_w), _flat(m_ada_w), _flat(v_ada_w))]
    grads["ada_w"] = g_ada_w

    sums_only = ("loss", "ret_norm")

    def packed_of(src, fill):
        vals = [jnp.full((sizes[i],), fill, F32) if k in sums_only else src[k].reshape(-1)
                for i, k in enumerate(small_keys)]
        return jnp.concatenate(vals + [jnp.full((padded - total,), fill, F32)]).reshape(-1, 128)

    w_p, m_p, v_p = packed_of(weights, 0.0), packed_of(moments_m, 0.0), packed_of(moments_v, 1.0)

    def f_small(w_, m_, v_, land_):
        g = _sum_slots(land_)
        return (g,) + _adamw_math(w_, g, m_, v_)

    g_p, d_p, nm_p, nv_p = _ew("adamw_small", f_small, [w_p, m_p, v_p, small_g], [F32] * 4)
    offs = np.cumsum([0] + sizes)
    for i, k in enumerate(small_keys):
        sl = slice(int(offs[i]), int(offs[i + 1]))
        if k == "loss":
            loss = g_p.reshape(-1)[int(offs[i])]
        elif k == "ret_norm":
            g_full = g_p.reshape(-1)[sl].reshape(NO, -1)
            g_mine = lax.dynamic_slice_in_dim(g_full, me * ret_norm.shape[1], ret_norm.shape[1], axis=1)
            d_, m_, v_ = _adamw("adamw_ret_norm", *[jnp.pad(a, ((0, 8 - NO), (0, 128 - a.shape[1])), constant_values=cv)
                                                     for a, cv in ((ret_norm, 0.0), (g_mine, 0.0), (m_ret_norm, 0.0), (v_ret_norm, 1.0))])
            grads[k] = g_mine
            deltas[k], new_m[k], new_v[k] = [a[:NO, :ret_norm.shape[1]] for a in (d_, m_, v_)]
        else:
            shp = weights[k].shape
            grads[k], deltas[k], new_m[k], new_v[k] = [a.reshape(-1)[sl].reshape(shp) for a in (g_p, d_p, nm_p, nv_p)]
    pad_b = lambda a, cv=0.0: jnp.pad(a, ((0, 8 - NL), (0, 0)), constant_values=cv)
    d_, m_, v_ = _adamw("adamw_ada_b", pad_b(ada_b), pad_b(g_ada_b), pad_b(m_ada_b), pad_b(v_ada_b, 1.0))
    grads["ada_b"] = g_ada_b
    deltas["ada_b"], new_m["ada_b"], new_v["ada_b"] = d_[:NL], m_[:NL], v_[:NL]

    return (loss, grad_x, *[grads[k] for k in order], *[deltas[k] for k in order],
            *[new_m[k] for k in order], *[new_v[k] for k in order])
```
